```python
import math
import jax
import jax.numpy as jnp
from jax import lax
import numpy as np

D_MODEL = 1024
BATCH = 2
SEQ = 8192
DEPTH = 2

GROUP_W = D_MODEL // 4
SSM_HEADS = 4
SSM_HEAD_DIM = GROUP_W // SSM_HEADS
SSM_GROUPS = 2
SSM_STATE = 128
SSM_CONV = 4
SSM_CHUNK = 128
SSM_XBC = GROUP_W + 2 * SSM_GROUPS * SSM_STATE
MLA_HEADS = 4
MLA_NOPE = 64
MLA_ROPE = 32
MLA_V = GROUP_W // MLA_HEADS
MLA_Q_RANK = 256
MLA_KV_RANK = 128
RET_HEADS = 4
RET_DK = 64
RET_DV = GROUP_W // RET_HEADS
RET_CHUNK = 128
NSA_HEADS = 4
NSA_DK = 64
NSA_DV = GROUP_W // NSA_HEADS
NSA_CMP_LEN = 32
NSA_CMP_STRIDE = 16
NSA_CMP_HID = 256
NSA_SLC_LEN = 64
NSA_TOPN = 16
NSA_WIN = 512
D_FF = 4 * D_MODEL
Q_BLOCK = 128
ROPE_THETA = 10000.0
EPS = 1e-5
NEG_INF = -1e30
FORCED_SCORE = 1e9
DEEPNORM_ALPHA = (2.0 * DEPTH) ** 0.25
DEEPNORM_BETA = (8.0 * DEPTH) ** -0.25

IN_SPLITS = (
    GROUP_W, SSM_XBC, SSM_HEADS,
    MLA_Q_RANK, MLA_KV_RANK, MLA_ROPE,
    RET_HEADS * RET_DK, RET_HEADS * RET_DK, RET_HEADS * RET_DV, GROUP_W,
    NSA_HEADS * NSA_DK, NSA_DK, NSA_DV, NSA_DK, NSA_DV, NSA_DK, NSA_DV, 3 * NSA_HEADS,
)
D_IN = sum(IN_SPLITS)

kernel_name = 'hybrid_ssd_mla_retention_nsa'


def split_columns(proj):
    offsets = [int(o) for o in np.cumsum(IN_SPLITS)[:-1]]
    return jnp.split(proj, offsets, axis=-1)


def layer_norm(x, g, b):
    xf = x.astype(jnp.float32)
    mu = jnp.mean(xf, axis=-1, keepdims=True)
    var = jnp.mean(jnp.square(xf - mu), axis=-1, keepdims=True)
    return ((xf - mu) * lax.rsqrt(var + EPS)).astype(x.dtype) * g + b


def rms_norm(x, g=None):
    xf = x.astype(jnp.float32)
    y = (xf * lax.rsqrt(jnp.mean(xf * xf, axis=-1, keepdims=True) + EPS)).astype(x.dtype)
    return y if g is None else y * g


def rope_tables(seq, dim):
    inv = ROPE_THETA ** (-jnp.arange(0, dim, 2, dtype=jnp.float32) / dim)
    ang = jnp.arange(seq, dtype=jnp.float32)[:, None] * inv[None, :]
    return jnp.cos(ang), jnp.sin(ang)


def apply_rope(x, cos, sin):
    x1, x2 = jnp.split(x, 2, axis=-1)
    c = cos[:, None, :].astype(x.dtype)
    s = sin[:, None, :].astype(x.dtype)
    return jnp.concatenate([x1 * c - x2 * s, x1 * s + x2 * c], axis=-1)


def causal_block_attention(q, k, v):
    bsz, S, H, D = q.shape
    nb = S // Q_BLOCK
    q_b = jnp.moveaxis(q.reshape(bsz, nb, Q_BLOCK, H, D), 1, 0)
    kpos = jnp.arange(S)

    def one_block(args):
        i, qi = args
        qpos = i * Q_BLOCK + jnp.arange(Q_BLOCK)
        s = jnp.einsum('bqhd,bkhd->bhqk', qi, k).astype(jnp.float32)
        s = jnp.where(kpos[None, :] <= qpos[:, None], s, NEG_INF)
        p = jax.nn.softmax(s, axis=-1).astype(v.dtype)
        return jnp.einsum('bhqk,bkhd->bqhd', p, v)

    out = lax.map(one_block, (jnp.arange(nb), q_b))
    return jnp.moveaxis(out, 0, 1).reshape(bsz, S, H, v.shape[-1])


def causal_depthwise_conv(u, w, b):
    y = lax.conv_general_dilated(u, w[:, None, :], window_strides=(1,), padding=[(SSM_CONV - 1, 0)],
                                 dimension_numbers=('NWC', 'WIO', 'NWC'), feature_group_count=u.shape[-1])
    return y + b


def segsum_exp(a):
    T = a.shape[-1]
    cs = jnp.cumsum(a, axis=-1)
    diff = cs[..., :, None] - cs[..., None, :]
    mask = jnp.tril(jnp.ones((T, T), dtype=bool))
    return jnp.where(mask, jnp.exp(jnp.where(mask, diff, 0.0)), 0.0)


def mamba2_ssd(z, xbc, dt_raw, conv_w, conv_b, dt_bias, a_log, d_skip, norm_g):
    bsz, S, _ = z.shape
    H, P, N, L = SSM_HEADS, SSM_HEAD_DIM, SSM_STATE, SSM_CHUNK
    nc = S // L
    xbc = jax.nn.silu(causal_depthwise_conv(xbc, conv_w, conv_b))
    xs, b_in, c_in = jnp.split(xbc, [GROUP_W, GROUP_W + SSM_GROUPS * N], axis=-1)
    xs = xs.reshape(bsz, S, H, P)
    rep = H // SSM_GROUPS
    b_h = jnp.repeat(b_in.reshape(bsz, S, SSM_GROUPS, N), rep, axis=2)
    c_h = jnp.repeat(c_in.reshape(bsz, S, SSM_GROUPS, N), rep, axis=2)
    dt = jax.nn.softplus(dt_raw + dt_bias)
    a = jnp.moveaxis((dt * -jnp.exp(a_log)).reshape(bsz, nc, L, H), 3, 1)
    xdt = (xs * dt[..., None]).reshape(bsz, nc, L, H, P)
    bc = b_h.reshape(bsz, nc, L, H, N)
    cc = c_h.reshape(bsz, nc, L, H, N)
    a_cs = jnp.cumsum(a, axis=-1)
    scores = jnp.einsum('bclhn,bcshn->bhcls', cc, bc) * segsum_exp(a)
    y_diag = jnp.einsum('bhcls,bcshp->bclhp', scores, xdt)
    decay_to_end = jnp.exp(a_cs[..., -1:] - a_cs)
    states = jnp.einsum('bclhn,bhcl,bclhp->bchpn', bc, decay_to_end, xdt)
    chunk_decay = jnp.exp(a_cs[..., -1])

    def step(h_state, inp):
        st, dec = inp
        return h_state * dec[..., None, None] + st, h_state

    _, prev = lax.scan(step, jnp.zeros_like(states[:, 0]),
                       (jnp.moveaxis(states, 1, 0), jnp.moveaxis(chunk_decay, 2, 0)))
    prev = jnp.moveaxis(prev, 0, 1)
    y_off = jnp.einsum('bclhn,bchpn,bhcl->bclhp', cc, prev, jnp.exp(a_cs))
    y = (y_diag + y_off).reshape(bsz, S, H, P) + xs * d_skip[:, None]
    y = y.reshape(bsz, S, GROUP_W) * jax.nn.silu(z)
    return rms_norm(y, norm_g)


def mla(c_q, c_kv, k_rope, q_norm_g, w_uq, kv_norm_g, w_ukv, cos, sin):
    bsz, S, _ = c_q.shape
    H = MLA_HEADS
    q = (rms_norm(c_q, q_norm_g) @ w_uq).reshape(bsz, S, H, MLA_NOPE + MLA_ROPE)
    q_nope, q_pe = jnp.split(q, [MLA_NOPE], axis=-1)
    q = jnp.concatenate([q_nope, apply_rope(q_pe, cos, sin)], axis=-1) * ((MLA_NOPE + MLA_ROPE) ** -0.5)
    kv = (rms_norm(c_kv, kv_norm_g) @ w_ukv).reshape(bsz, S, H, MLA_NOPE + MLA_V)
    k_nope, v = jnp.split(kv, [MLA_NOPE], axis=-1)
    k_pe = jnp.broadcast_to(apply_rope(k_rope[:, :, None, :], cos, sin), (bsz, S, H, MLA_ROPE))
    k = jnp.concatenate([k_nope, k_pe], axis=-1)
    return causal_block_attention(q, k, v).reshape(bsz, S, GROUP_W)


def retention(q, k, v, gate, cos, sin):
    bsz, S, _ = q.shape
    H, dk, dv, L = RET_HEADS, RET_DK, RET_DV, RET_CHUNK
    nc = S // L
    dt = q.dtype
    q = apply_rope(q.reshape(bsz, S, H, dk), cos, sin)
    k = apply_rope(k.reshape(bsz, S, H, dk), cos, sin) * (dk ** -0.5)
    qc = q.reshape(bsz, nc, L, H, dk)
    kc = k.reshape(bsz, nc, L, H, dk)
    vc = v.reshape(bsz, nc, L, H, dv)
    log_gamma = jnp.log1p(-jnp.exp2(-5.0 - jnp.arange(H, dtype=jnp.float32)))
    pos = jnp.arange(L, dtype=jnp.float32)
    diff = pos[:, None] - pos[None, :]
    decay_in = jnp.where(diff >= 0, jnp.exp(jnp.maximum(diff, 0.0)[None] * log_gamma[:, None, None]), 0.0).astype(dt)
    zeta = jnp.exp((L - 1 - pos)[None] * log_gamma[:, None]).astype(dt)
    xi = jnp.exp((pos + 1.0)[None] * log_gamma[:, None]).astype(dt)
    chunk_decay = jnp.exp(L * log_gamma).astype(dt)
    scores = jnp.einsum('bclhd,bcshd->bhcls', qc, kc) * decay_in[None, :, None]
    y_in = jnp.einsum('bhcls,bcshv->bclhv', scores, vc)
    kv = jnp.einsum('bclhd,hl,bclhv->bchdv', kc, zeta, vc)

    def step(state, kv_c):
        return state * chunk_decay[None, :, None, None] + kv_c, state

    _, prev = lax.scan(step, jnp.zeros_like(kv[:, 0]), jnp.moveaxis(kv, 1, 0))
    prev = jnp.moveaxis(prev, 0, 1)
    y_cross = jnp.einsum('bclhd,hl,bchdv->bclhv', qc, xi, prev)
    y = rms_norm((y_in + y_cross).reshape(bsz, S, H, dv))
    return y.reshape(bsz, S, GROUP_W) * jax.nn.silu(gate)


def nsa(q, k_cmp_raw, v_cmp_raw, k_slc, v_slc, k_win, v_win, gate_logits,
        pe_k, w1_k, w2_k, pe_v, w1_v, w2_v):
    bsz, S, _ = q.shape
    H, dk, dv = NSA_HEADS, NSA_DK, NSA_DV
    f32 = jnp.float32
    q = q.reshape(bsz, S, H, dk) * (dk ** -0.5)
    gates = jax.nn.sigmoid(gate_logits.reshape(bsz, S, H, 3))
    slopes = jnp.exp2(-8.0 * jnp.arange(1, H + 1, dtype=f32) / H)
    n_cmp = (S - NSA_CMP_LEN) // NSA_CMP_STRIDE + 1
    cmp_start = jnp.arange(n_cmp) * NSA_CMP_STRIDE
    cmp_tok = cmp_start[:, None] + jnp.arange(NSA_CMP_LEN)[None, :]
    cmp_end = cmp_start + NSA_CMP_LEN - 1
    cmp_center = cmp_start.astype(f32) + 0.5 * (NSA_CMP_LEN - 1)

    def compress(u, pe, w1, w2):
        blocks = (u[:, cmp_tok] + pe).reshape(bsz, n_cmp, NSA_CMP_LEN * u.shape[-1])
        return jax.nn.silu(blocks @ w1) @ w2

    k_cmp = compress(k_cmp_raw, pe_k, w1_k, w2_k)
    v_cmp = compress(v_cmp_raw, pe_v, w1_v, w2_v)
    n_slc = S // NSA_SLC_LEN
    top_n = min(NSA_TOPN, n_slc)
    slc_start = jnp.arange(n_slc) * NSA_SLC_LEN
    overlap = ((cmp_start[:, None] < slc_start[None, :] + NSA_SLC_LEN)
               & (cmp_end[:, None] >= slc_start[None, :])).astype(f32)
    k_blocks = k_slc.reshape(bsz, n_slc, NSA_SLC_LEN, dk)
    v_blocks = v_slc.reshape(bsz, n_slc, NSA_SLC_LEN, dv)
    bidx = jnp.arange(bsz)[:, None, None]
    k_win_pad = jnp.pad(k_win, ((0, 0), (NSA_WIN, 0), (0, 0)))
    v_win_pad = jnp.pad(v_win, ((0, 0), (NSA_WIN, 0), (0, 0)))
    nb = S // Q_BLOCK
    q_b = jnp.moveaxis(q.reshape(bsz, nb, Q_BLOCK, H, dk), 1, 0)
    g_b = jnp.moveaxis(gates.reshape(bsz, nb, Q_BLOCK, H, 3), 1, 0)

    def one_block(args):
        i, qi, gi = args
        qpos = i * Q_BLOCK + jnp.arange(Q_BLOCK)
        qpos_f = qpos.astype(f32)
        valid_c = cmp_end[None, :] <= qpos[:, None]
        s_c = (jnp.einsum('bqhd,bnd->bhqn', qi, k_cmp).astype(f32)
               - slopes[:, None, None] * (qpos_f[:, None] - cmp_center[None, :]))
        s_c = jnp.where(valid_c, s_c, NEG_INF)
        p_c = jax.nn.softmax(s_c, axis=-1) * jnp.any(valid_c, axis=-1)[:, None].astype(f32)
        o_c = jnp.einsum('bhqn,bnd->bqhd', p_c.astype(v_cmp.dtype), v_cmp)
        imp = jnp.einsum('bhqn,nj->bqj', p_c, overlap)
        blk = jnp.arange(n_slc)
        q_blk = qpos // NSA_SLC_LEN
        causal_blk = slc_start[None, :] <= qpos[:, None]
        forced = (blk[None, :] == 0) | (blk[None, :] == q_blk[:, None]) | (blk[None, :] == q_blk[:, None] - 1)
        imp = jnp.where(forced & causal_blk, FORCED_SCORE, imp)
        imp = jnp.where(causal_blk, imp, -1.0)
        _, sel = lax.top_k(imp, top_n)
        k_sel = k_blocks[bidx, sel].reshape(bsz, Q_BLOCK, top_n * NSA_SLC_LEN, dk)
        v_sel = v_blocks[bidx, sel].reshape(bsz, Q_BLOCK, top_n * NSA_SLC_LEN, dv)
        pos_sel = (sel[..., None] * NSA_SLC_LEN + jnp.arange(NSA_SLC_LEN)).reshape(bsz, Q_BLOCK, -1)
        dist_s = (qpos[None, :, None] - pos_sel).astype(f32)
        s_s = (jnp.einsum('bqhd,bqkd->bhqk', qi, k_sel).astype(f32)
               - slopes[None, :, None, None] * dist_s[:, None])
        s_s = jnp.where((dist_s >= 0)[:, None], s_s, NEG_INF)
        p_s = jax.nn.softmax(s_s, axis=-1).astype(v_sel.dtype)
        o_s = jnp.einsum('bhqk,bqkd->bqhd', p_s, v_sel)
        k_w = lax.dynamic_slice_in_dim(k_win_pad, i * Q_BLOCK, Q_BLOCK + NSA_WIN, axis=1)
        v_w = lax.dynamic_slice_in_dim(v_win_pad, i * Q_BLOCK, Q_BLOCK + NSA_WIN, axis=1)
        kpos = i * Q_BLOCK - NSA_WIN + jnp.arange(Q_BLOCK + NSA_WIN)
        dist_w = qpos[:, None] - kpos[None, :]
        valid_w = (dist_w >= 0) & (dist_w < NSA_WIN) & (kpos[None, :] >= 0)
        s_w = (jnp.einsum('bqhd,bkd->bhqk', qi, k_w).astype(f32)
               - slopes[:, None, None] * dist_w.astype(f32))
        s_w = jnp.where(valid_w, s_w, NEG_INF)
        p_w = jax.nn.softmax(s_w, axis=-1).astype(v_w.dtype)
        o_w = jnp.einsum('bhqk,bkd->bqhd', p_w, v_w)
        return gi[..., 0:1] * o_c + gi[..., 1:2] * o_s + gi[..., 2:3] * o_w

    out = lax.map(one_block, (jnp.arange(nb), q_b, g_b))
    return jnp.moveaxis(out, 0, 1).reshape(bsz, S, GROUP_W)


def setup_inputs(seed: int = 0) -> dict:
    key = jax.random.key(seed)
    ks = jax.random.split(key, 32)
    f32 = jnp.float32
    L = DEPTH

    def nrm(k, shape, fan_in, scale=1.0):
        return jax.random.normal(k, shape, f32) * (scale * fan_in ** -0.5)

    def gain(k, shape):
        return 1.0 + 0.02 * jax.random.normal(k, shape, f32)

    def small(k, shape):
        return 0.02 * jax.random.normal(k, shape, f32)

    x = jax.random.normal(ks[0], (BATCH, SEQ, D_MODEL), f32)
    dt0 = jnp.exp(jax.random.uniform(ks[6], (L, SSM_HEADS), f32, math.log(1e-3), math.log(1e-1)))
    return {
        'x': x,
        'ln_emb_g': gain(ks[1], (D_MODEL,)),
        'ln_emb_b': small(ks[2], (D_MODEL,)),
        'w_in': nrm(ks[3], (L, D_MODEL, D_IN), D_MODEL),
        'conv_w': nrm(ks[4], (L, SSM_CONV, SSM_XBC), SSM_CONV),
        'conv_b': small(ks[5], (L, SSM_XBC)),
        'dt_bias': dt0 + jnp.log(-jnp.expm1(-dt0)),
        'a_log': jnp.log(jax.random.uniform(ks[7], (L, SSM_HEADS), f32, 1.0, 16.0)),
        'd_skip': gain(ks[8], (L, SSM_HEADS)),
        'ssm_norm_g': gain(ks[9], (L, GROUP_W)),
        'q_norm_g': gain(ks[10], (L, MLA_Q_RANK)),
        'w_uq': nrm(ks[11], (L, MLA_Q_RANK, MLA_HEADS * (MLA_NOPE + MLA_ROPE)), MLA_Q_RANK),
        'kv_norm_g': gain(ks[12], (L, MLA_KV_RANK)),
        'w_ukv': nrm(ks[13], (L, MLA_KV_RANK, MLA_HEADS * (MLA_NOPE + MLA_V)), MLA_KV_RANK),
        'cmp_pe_k': small(ks[14], (L, NSA_CMP_LEN, NSA_DK)),
        'cmp_w1_k': nrm(ks[15], (L, NSA_CMP_LEN * NSA_DK, NSA_CMP_HID), NSA_CMP_LEN * NSA_DK),
        'cmp_w2_k': nrm(ks[16], (L, NSA_CMP_HID, NSA_DK), NSA_CMP_HID),
        'cmp_pe_v': small(ks[17], (L, NSA_CMP_LEN, NSA_DV)),
        'cmp_w1_v': nrm(ks[18], (L, NSA_CMP_LEN * NSA_DV, NSA_CMP_HID), NSA_CMP_LEN * NSA_DV),
        'cmp_w2_v': nrm(ks[19], (L, NSA_CMP_HID, NSA_DV), NSA_CMP_HID),
        'w_out': nrm(ks[20], (L, D_MODEL, D_MODEL), D_MODEL, DEEPNORM_BETA),
        'ln1_g': gain(ks[21], (L, D_MODEL)),
        'ln1_b': small(ks[22], (L, D_MODEL)),
        'w_mlp1': nrm(ks[23], (L, D_MODEL, D_FF), D_MODEL),
        'w_mlp2': nrm(ks[24], (L, D_FF, D_MODEL), D_FF, DEEPNORM_BETA),
        'ln2_g': gain(ks[25], (L, D_MODEL)),
        'ln2_b': small(ks[26], (L, D_MODEL)),
    }


def reference(x, ln_emb_g, ln_emb_b, w_in, conv_w, conv_b, dt_bias, a_log, d_skip, ssm_norm_g,
              q_norm_g, w_uq, kv_norm_g, w_ukv, cmp_pe_k, cmp_w1_k, cmp_w2_k, cmp_pe_v, cmp_w1_v,
              cmp_w2_v, w_out, ln1_g, ln1_b, w_mlp1, w_mlp2, ln2_g, ln2_b):
    S = x.shape[1]
    cos_m, sin_m = rope_tables(S, MLA_ROPE)
    cos_r, sin_r = rope_tables(S, RET_DK)
    h = layer_norm(x, ln_emb_g, ln_emb_b)
    for l in range(DEPTH):
        (ssm_z, ssm_xbc, ssm_dt, mla_cq, mla_ckv, mla_kr, ret_q, ret_k, ret_v, ret_g,
         nsa_q, nsa_kc, nsa_vc, nsa_ks, nsa_vs, nsa_kw, nsa_vw, nsa_gate) = split_columns(h @ w_in[l])
        y_a = mamba2_ssd(ssm_z, ssm_xbc, ssm_dt, conv_w[l], conv_b[l], dt_bias[l], a_log[l], d_skip[l], ssm_norm_g[l])
        y_b = mla(mla_cq, mla_ckv, mla_kr, q_norm_g[l], w_uq[l], kv_norm_g[l], w_ukv[l], cos_m, sin_m)
        y_c = retention(ret_q, ret_k, ret_v, ret_g, cos_r, sin_r)
        y_d = nsa(nsa_q, nsa_kc, nsa_vc, nsa_ks, nsa_vs, nsa_kw, nsa_vw, nsa_gate,
                  cmp_pe_k[l], cmp_w1_k[l], cmp_w2_k[l], cmp_pe_v[l], cmp_w1_v[l], cmp_w2_v[l])
        mix = jnp.concatenate([y_a, y_b, y_c, y_d], axis=-1) @ w_out[l]
        h = layer_norm(DEEPNORM_ALPHA * h + mix, ln1_g[l], ln1_b[l])
        ff = jnp.square(jax.nn.relu(h @ w_mlp1[l])) @ w_mlp2[l]
        h = layer_norm(DEEPNORM_ALPHA * h + ff, ln2_g[l], ln2_b[l])
    return h
```

```python
import functools
import math

import jax
import jax.numpy as jnp
import numpy as np
from jax import lax
from jax.experimental import pallas as pl
from jax.experimental.pallas import tpu as pltpu

F32 = jnp.float32
BF16 = jnp.bfloat16
HIGHEST = lax.Precision.HIGHEST

D_MODEL = 1024
DEPTH = 2
GROUP_W = D_MODEL // 4
SSM_HEADS = 4
SSM_HEAD_DIM = GROUP_W // SSM_HEADS
SSM_GROUPS = 2
SSM_STATE = 128
SSM_CONV = 4
SSM_CHUNK = 128
SSM_XBC = GROUP_W + 2 * SSM_GROUPS * SSM_STATE
MLA_HEADS = 4
MLA_NOPE = 64
MLA_ROPE = 32
MLA_V = GROUP_W // MLA_HEADS
MLA_Q_RANK = 256
MLA_KV_RANK = 128
RET_HEADS = 4
RET_DK = 64
RET_DV = GROUP_W // RET_HEADS
RET_CHUNK = 128
NSA_HEADS = 4
NSA_DK = 64
NSA_DV = GROUP_W // NSA_HEADS
NSA_CMP_LEN = 32
NSA_CMP_STRIDE = 16
NSA_CMP_HID = 256
NSA_SLC_LEN = 64
NSA_TOPN = 16
NSA_WIN = 512
D_FF = 4 * D_MODEL
Q_BLOCK = 128
ROPE_THETA = 10000.0
EPS = 1e-5
NEG_INF = -1e30
FORCED_SCORE = 1e9
DEEPNORM_ALPHA = (2.0 * DEPTH) ** 0.25

IN_SPLITS = (
    GROUP_W, SSM_XBC, SSM_HEADS,
    MLA_Q_RANK, MLA_KV_RANK, MLA_ROPE,
    RET_HEADS * RET_DK, RET_HEADS * RET_DK, RET_HEADS * RET_DV, GROUP_W,
    NSA_HEADS * NSA_DK, NSA_DK, NSA_DV, NSA_DK, NSA_DV, NSA_DK, NSA_DV, 3 * NSA_HEADS,
)

LANE = 128
W_SSM = GROUP_W + SSM_XBC + LANE
W_MLA = MLA_Q_RANK + MLA_KV_RANK + LANE
W_RET = 4 * GROUP_W
W_NSA = NSA_HEADS * LANE + 3 * LANE + LANE
W_PROJ = W_SSM + W_MLA + W_RET + W_NSA

NSA_TILE = 512
MLA_TILE = 512
VMEM_LIMIT = 48 * 1024 * 1024


def _cparams(*sem):
    return pltpu.CompilerParams(dimension_semantics=sem, vmem_limit_bytes=VMEM_LIMIT)


def _mm(a, b):
    return jnp.dot(a.astype(BF16), b.astype(BF16), preferred_element_type=F32)


def _mm_nt(a, b):
    return lax.dot_general(a.astype(BF16), b.astype(BF16), (((1,), (1,)), ((), ())),
                           preferred_element_type=F32)


def _mm_f32(a, b):
    return jnp.dot(a, b, precision=HIGHEST, preferred_element_type=F32)


def _silu(x):
    return x * jax.nn.sigmoid(x)


def _softplus(x):
    return jnp.maximum(x, 0.0) + jnp.log1p(jnp.exp(-jnp.abs(x)))


def _layer_norm(x, g, b):
    mu = jnp.mean(x, axis=-1, keepdims=True)
    xc = x - mu
    var = jnp.mean(xc * xc, axis=-1, keepdims=True)
    return xc * lax.rsqrt(var + EPS) * g + b


def _iota(shape, dim):
    return lax.broadcasted_iota(jnp.int32, shape, dim)


def _ln_kernel(x_ref, g_ref, b_ref, o_ref):
    o_ref[...] = _layer_norm(x_ref[...], g_ref[...], b_ref[...])


def _entry_ln(x2, g, b, tm=512):
    T, D = x2.shape
    return pl.pallas_call(
        _ln_kernel, name="entry_ln",
        grid=(T // tm,),
        in_specs=[pl.BlockSpec((tm, D), lambda i: (i, 0)),
                  pl.BlockSpec((1, D), lambda i: (0, 0)),
                  pl.BlockSpec((1, D), lambda i: (0, 0))],
        out_specs=pl.BlockSpec((tm, D), lambda i: (i, 0)),
        out_shape=jax.ShapeDtypeStruct((T, D), F32),
        compiler_params=_cparams("parallel"),
    )(x2, g.reshape(1, D), b.reshape(1, D))


def _pad_cols(w, width):
    return jnp.pad(w, ((0, 0), (0, width - w.shape[1])))


def _layout_w_in(w):
    offs = np.concatenate([[0], np.cumsum(IN_SPLITS)])
    p = [w[:, int(offs[i]):int(offs[i + 1])] for i in range(len(IN_SPLITS))]
    (ssm_z, ssm_xbc, ssm_dt, mla_cq, mla_ckv, mla_kr, ret_q, ret_k, ret_v, ret_g,
     nsa_q, nsa_kc, nsa_vc, nsa_ks, nsa_vs, nsa_kw, nsa_vw, nsa_gate) = p
    nsa_q_heads = [_pad_cols(nsa_q[:, h * NSA_DK:(h + 1) * NSA_DK], LANE) for h in range(NSA_HEADS)]
    cols = [ssm_z, ssm_xbc, _pad_cols(ssm_dt, LANE),
            mla_cq, mla_ckv, _pad_cols(mla_kr, LANE),
            ret_q, ret_k, ret_v, ret_g,
            *nsa_q_heads, nsa_kc, nsa_vc, nsa_ks, nsa_vs, nsa_kw, nsa_vw, _pad_cols(nsa_gate, LANE)]
    out = jnp.concatenate(cols, axis=1)
    assert out.shape[1] == W_PROJ
    return out.astype(BF16)


def _in_proj_kernel(h_ref, w_ref, ssm_ref, mla_ref, ret_ref, nsa_ref):
    hb = h_ref[...].astype(BF16)
    off = 0
    for ref, width in ((ssm_ref, W_SSM), (mla_ref, W_MLA), (ret_ref, W_RET), (nsa_ref, W_NSA)):
        ref[...] = jnp.dot(hb, w_ref[:, off:off + width], preferred_element_type=F32)
        off += width


def _in_proj(h2, w_p, tm=256):
    T, D = h2.shape
    widths = (W_SSM, W_MLA, W_RET, W_NSA)
    return pl.pallas_call(
        _in_proj_kernel, name="in_proj",
        grid=(T // tm,),
        in_specs=[pl.BlockSpec((tm, D), lambda i: (i, 0)),
                  pl.BlockSpec((D, W_PROJ), lambda i: (0, 0))],
        out_specs=[pl.BlockSpec((tm, w), lambda i: (i, 0)) for w in widths],
        out_shape=[jax.ShapeDtypeStruct((T, w), F32) for w in widths],
        compiler_params=_cparams("parallel"),
    )(h2, w_p)


def _ssm_kernel(p_ref, cw_ref, cb_ref, dtb_ref, alog_ref, dskip_ref, ng_ref, o_ref, state_ref, ext_ref):
    L, H, P, N = SSM_CHUNK, SSM_HEADS, SSM_HEAD_DIM, SSM_STATE
    c = pl.program_id(1)

    @pl.when(c == 0)
    def _():
        state_ref[...] = jnp.zeros_like(state_ref)
        ext_ref[0:8, :] = jnp.zeros((8, SSM_XBC), F32)

    z = p_ref[:, 0:GROUP_W]
    ext_ref[8:8 + L, :] = p_ref[:, GROUP_W:GROUP_W + SSM_XBC]
    conv = cb_ref[...]
    for j in range(SSM_CONV):
        conv = conv + ext_ref[pl.ds(8 - (SSM_CONV - 1) + j, L), :] * cw_ref[j:j + 1, :]
    ext_ref[0:8, :] = ext_ref[L:L + 8, :]
    xbc = _silu(conv)
    xs = xbc[:, 0:GROUP_W]
    b_in = xbc[:, GROUP_W:GROUP_W + SSM_GROUPS * N]
    c_in = xbc[:, GROUP_W + SSM_GROUPS * N:]

    dt = _softplus(p_ref[:, GROUP_W + SSM_XBC:] + dtb_ref[...])
    a = dt * (-jnp.exp(alog_ref[...]))
    row = _iota((L, L), 0)
    col = _iota((L, L), 1)
    tril = col <= row
    cs = _mm_f32(jnp.where(tril, 1.0, 0.0), a)
    cs_t = cs.T
    ecs = jnp.exp(cs)
    dte = jnp.exp(cs[L - 1:L, :] - cs)
    expand = jnp.where(_iota((LANE, H * P), 0) == _iota((LANE, H * P), 1) // P, 1.0, 0.0)
    dt_x = _mm_f32(dt, expand)
    ecs_x = _mm_f32(ecs, expand)
    dte_x = _mm_f32(dte, expand)

    xdt = xs * dt_x
    wx = xdt * dte_x
    head_of_lane = _iota((L, H * P), 1) // P
    y = xs * dskip_ref[...]
    y_off = []
    rep = H // SSM_GROUPS
    for g in range(SSM_GROUPS):
        cg = c_in[:, g * N:(g + 1) * N]
        bg = b_in[:, g * N:(g + 1) * N]
        cb = _mm_nt(cg, bg)
        for h in range(g * rep, (g + 1) * rep):
            diff = cs[:, h:h + 1] - cs_t[h:h + 1, :]
            seg = jnp.where(tril, jnp.exp(jnp.where(tril, diff, 0.0)), 0.0)
            yh = _mm(cb * seg, xdt)
            y = y + jnp.where(head_of_lane == h, yh, 0.0)
        lanes = slice(g * rep * P, (g + 1) * rep * P)
        st_prev = state_ref[:, lanes]
        y_off.append(_mm(cg, st_prev))
        state_ref[:, lanes] = st_prev * ecs_x[L - 1:L, lanes] + _mm(bg.T, wx[:, lanes])
    y = y + jnp.concatenate(y_off, axis=1) * ecs_x
    y = y * _silu(z)
    ms = jnp.mean(y * y, axis=-1, keepdims=True)
    o_ref[...] = y * lax.rsqrt(ms + EPS) * ng_ref[...]


def _ssm(p_ssm, conv_w, conv_b, dt_bias, a_log, d_skip, norm_g):
    B, S, _ = p_ssm.shape
    L = SSM_CHUNK
    pad_h = lambda v: jnp.pad(v, (0, LANE - SSM_HEADS)).reshape(1, LANE)
    const = lambda shape: pl.BlockSpec(shape, lambda b, c: (0,) * len(shape))
    return pl.pallas_call(
        _ssm_kernel, name="ssm",
        grid=(B, S // L),
        in_specs=[pl.BlockSpec((None, L, W_SSM), lambda b, c: (b, c, 0)),
                  const((SSM_CONV, SSM_XBC)), const((1, SSM_XBC)), const((1, LANE)), const((1, LANE)),
                  const((1, GROUP_W)), const((1, GROUP_W))],
        out_specs=pl.BlockSpec((None, L, GROUP_W), lambda b, c: (b, c, 0)),
        out_shape=jax.ShapeDtypeStruct((B, S, GROUP_W), F32),
        scratch_shapes=[pltpu.VMEM((SSM_STATE, GROUP_W), F32),
                        pltpu.VMEM((L + 8, SSM_XBC), F32)],
        compiler_params=_cparams("parallel", "arbitrary"),
    )(p_ssm, conv_w, conv_b.reshape(1, -1), pad_h(dt_bias), pad_h(a_log),
      jnp.repeat(d_skip, SSM_HEAD_DIM).reshape(1, GROUP_W), norm_g.reshape(1, GROUP_W))


def _ret_kernel(p_ref, cos_ref, sin_ref, dec_ref, zeta_ref, xi_ref, cd_ref, o_ref, state_ref):
    L, H, DK, DV = RET_CHUNK, RET_HEADS, RET_DK, RET_DV
    W = H * DK
    c = pl.program_id(1)

    @pl.when(c == 0)
    def _():
        state_ref[...] = jnp.zeros_like(state_ref)

    q = p_ref[:, 0:W]
    k = p_ref[:, W:2 * W]
    v = p_ref[:, 2 * W:3 * W]
    gate = p_ref[:, 3 * W:4 * W]
    lane = _iota((L, W), 1)
    first_half = (lane % DK) < (DK // 2)
    head_of_lane = lane // DK

    def rope(x):
        partner = jnp.where(first_half, pltpu.roll(x, W - DK // 2, 1), pltpu.roll(x, DK // 2, 1))
        return x * cos_ref[...] + partner * sin_ref[...]

    qr = rope(q)
    kr = rope(k) * (DK ** -0.5)
    y = jnp.zeros((L, H * DV), F32)
    for h in range(H):
        qh = jnp.where(head_of_lane == h, qr, 0.0)
        sc = _mm_nt(qh, kr) * dec_ref[h]
        y = y + jnp.where(head_of_lane == h, _mm(sc, v), 0.0)
    st = state_ref[...]
    y = y + _mm(qr * xi_ref[...], st)
    same_head = (_iota((W, H * DV), 0) // DK) == (_iota((W, H * DV), 1) // DV)
    kv = _mm((kr * zeta_ref[...]).T, v)
    state_ref[...] = st * cd_ref[...] + jnp.where(same_head, kv, 0.0)
    ms = _mm_f32(y * y, jnp.where(same_head, 1.0 / DV, 0.0))
    o_ref[...] = y * lax.rsqrt(ms + EPS) * _silu(gate)


def _ret_tables(S):
    H, DK, L = RET_HEADS, RET_DK, RET_CHUNK
    inv = ROPE_THETA ** (-jnp.arange(0, DK, 2, dtype=F32) / DK)
    ang = jnp.arange(S, dtype=F32)[:, None] * inv[None, :]
    cos, sin = jnp.cos(ang), jnp.sin(ang)
    cos_t = jnp.tile(jnp.concatenate([cos, cos], axis=1), (1, H))
    sin_t = jnp.tile(jnp.concatenate([-sin, sin], axis=1), (1, H))
    log_gamma = jnp.log1p(-jnp.exp2(-5.0 - jnp.arange(H, dtype=F32)))
    pos = jnp.arange(L, dtype=F32)
    diff = pos[:, None] - pos[None, :]
    decay_in = jnp.where(diff >= 0, jnp.exp(jnp.maximum(diff, 0.0)[None] * log_gamma[:, None, None]), 0.0)
    zeta = jnp.exp((L - 1 - pos)[None] * log_gamma[:, None])
    xi = jnp.exp((pos + 1.0)[None] * log_gamma[:, None])
    chunk_decay = jnp.exp(L * log_gamma)
    zeta_x = jnp.repeat(zeta.T, DK, axis=1)
    xi_x = jnp.repeat(xi.T, DK, axis=1)
    cd_x = jnp.repeat(chunk_decay, RET_DV).reshape(1, H * RET_DV)
    return cos_t, sin_t, decay_in, zeta_x, xi_x, cd_x


def _ret(p_ret, tables):
    B, S, _ = p_ret.shape
    L, H = RET_CHUNK, RET_HEADS
    W = H * RET_DK
    cos_t, sin_t, decay_in, zeta_x, xi_x, cd_x = tables
    const = lambda shape: pl.BlockSpec(shape, lambda b, c: (0,) * len(shape))
    return pl.pallas_call(
        _ret_kernel, name="retention",
        grid=(B, S // L),
        in_specs=[pl.BlockSpec((None, L, W_RET), lambda b, c: (b, c, 0)),
                  pl.BlockSpec((L, W), lambda b, c: (c, 0)),
                  pl.BlockSpec((L, W), lambda b, c: (c, 0)),
                  const((H, L, L)), const((L, W)), const((L, W)), const((1, H * RET_DV))],
        out_specs=pl.BlockSpec((None, L, GROUP_W), lambda b, c: (b, c, 0)),
        out_shape=jax.ShapeDtypeStruct((B, S, GROUP_W), F32),
        scratch_shapes=[pltpu.VMEM((W, H * RET_DV), F32)],
        compiler_params=_cparams("parallel", "arbitrary"),
    )(p_ret, cos_t, sin_t, decay_in, zeta_x, xi_x, cd_x)


def _mla_prep_kernel(p_ref, qg_ref, wq_ref, kvg_ref, wk_ref, wv_ref, cos_ref, sin_ref,
                     q_ref, k_ref, v_ref):
    tm = p_ref.shape[0]
    cq = p_ref[:, 0:MLA_Q_RANK]
    ckv = p_ref[:, MLA_Q_RANK:MLA_Q_RANK + MLA_KV_RANK]
    kr = p_ref[:, MLA_Q_RANK + MLA_KV_RANK:]

    def rms(x, g):
        return x * lax.rsqrt(jnp.mean(x * x, axis=-1, keepdims=True) + EPS) * g

    q = _mm(rms(cq, qg_ref[...]), wq_ref[...])
    kvl = rms(ckv, kvg_ref[...])
    kn = _mm(kvl, wk_ref[...])
    vv = _mm(kvl, wv_ref[...])
    kr_sh = pltpu.roll(kr, MLA_NOPE, 1)
    lane = _iota((tm, LANE), 1)
    half = MLA_ROPE // 2
    low = (lane >= MLA_NOPE) & (lane < MLA_NOPE + half)
    cos = cos_ref[...]
    sin = sin_ref[...]

    def rope(x):
        partner = jnp.where(low, pltpu.roll(x, LANE - half, 1), pltpu.roll(x, half, 1))
        return x * cos + partner * sin

    scale = (MLA_NOPE + MLA_ROPE) ** -0.5
    for h in range(MLA_HEADS):
        sl = slice(h * LANE, (h + 1) * LANE)
        q_ref[h] = (rope(q[:, sl]) * scale).astype(BF16)
        k_ref[h] = rope(kn[:, sl] + kr_sh).astype(BF16)
        v_ref[h] = vv[:, sl].astype(BF16)


def _mla_tables(S):
    half = MLA_ROPE // 2
    inv = ROPE_THETA ** (-jnp.arange(0, MLA_ROPE, 2, dtype=F32) / MLA_ROPE)
    ang = jnp.arange(S, dtype=F32)[:, None] * inv[None, :]
    cos, sin = jnp.cos(ang), jnp.sin(ang)
    ones = jnp.ones((S, MLA_NOPE), F32)
    tail = LANE - MLA_NOPE - MLA_ROPE
    cos_t = jnp.concatenate([ones, cos, cos, jnp.ones((S, tail), F32)], axis=1)
    sin_t = jnp.concatenate([jnp.zeros((S, MLA_NOPE), F32), -sin, sin, jnp.zeros((S, tail), F32)], axis=1)
    del half
    return cos_t, sin_t


def _layout_mla_weights(w_uq, w_ukv):
    H = MLA_HEADS
    dq = MLA_NOPE + MLA_ROPE
    wq = jnp.concatenate([_pad_cols(w_uq[:, h * dq:(h + 1) * dq], LANE) for h in range(H)], axis=1)
    dkv = MLA_NOPE + MLA_V
    wk, wv = [], []
    for h in range(H):
        blk = w_ukv[:, h * dkv:(h + 1) * dkv]
        wk.append(_pad_cols(blk[:, :MLA_NOPE], LANE))
        v = blk[:, MLA_NOPE:]
        zero = jnp.zeros_like(v)
        wv.append(jnp.concatenate([v, zero] if h % 2 == 0 else [zero, v], axis=1))
    return wq.astype(BF16), jnp.concatenate(wk, axis=1).astype(BF16), jnp.concatenate(wv, axis=1).astype(BF16)


def _mla_prep(p_mla, q_norm_g, kv_norm_g, wq, wk, wv, tables, tm=512):
    B, S, _ = p_mla.shape
    H = MLA_HEADS
    cos_t, sin_t = tables
    const = lambda shape: pl.BlockSpec(shape, lambda b, i: (0,) * len(shape))
    qkv_spec = pl.BlockSpec((None, H, tm, LANE), lambda b, i: (b, 0, i, 0))
    qkv_shape = jax.ShapeDtypeStruct((B, H, S, LANE), BF16)
    return pl.pallas_call(
        _mla_prep_kernel, name="mla_prep",
        grid=(B, S // tm),
        in_specs=[pl.BlockSpec((None, tm, W_MLA), lambda b, i: (b, i, 0)),
                  const((1, MLA_Q_RANK)), const((MLA_Q_RANK, H * LANE)),
                  const((1, MLA_KV_RANK)), const((MLA_KV_RANK, H * LANE)), const((MLA_KV_RANK, H * LANE)),
                  pl.BlockSpec((tm, LANE), lambda b, i: (i, 0)),
                  pl.BlockSpec((tm, LANE), lambda b, i: (i, 0))],
        out_specs=[qkv_spec, qkv_spec, qkv_spec],
        out_shape=[qkv_shape, qkv_shape, qkv_shape],
        compiler_params=_cparams("parallel", "parallel"),
    )(p_mla, q_norm_g.reshape(1, -1), wq, kv_norm_g.reshape(1, -1), wk, wv, cos_t, sin_t)


def _mla_attn_kernel(q_ref, k_ref, v_ref, o_ref, m_ref, l_ref, acc_ref):
    H = MLA_HEADS
    tq, tk = q_ref.shape[1], k_ref.shape[1]
    i = pl.program_id(1)
    j = pl.program_id(2)

    @pl.when(j == 0)
    def _():
        m_ref[...] = jnp.full(m_ref.shape, NEG_INF, F32)
        l_ref[...] = jnp.zeros_like(l_ref)
        acc_ref[...] = jnp.zeros_like(acc_ref)

    @pl.when(j <= i)
    def _():
        causal = (j * tk + _iota((tq, tk), 1)) <= (i * tq + _iota((tq, tk), 0))
        for h in range(H):
            s = lax.dot_general(q_ref[h], k_ref[h], (((1,), (1,)), ((), ())), preferred_element_type=F32)
            s = jnp.where(causal, s, NEG_INF)
            m_prev = m_ref[h]
            m_new = jnp.maximum(m_prev, jnp.max(s, axis=-1, keepdims=True))
            alpha = jnp.exp(m_prev - m_new)
            p = jnp.exp(s - m_new)
            l_ref[h] = alpha * l_ref[h] + jnp.sum(p, axis=-1, keepdims=True)
            acc_ref[h] = alpha * acc_ref[h] + jnp.dot(p.astype(BF16), v_ref[h], preferred_element_type=F32)
            m_ref[h] = m_new

    @pl.when(j == i)
    def _():
        for pair in range(H // 2):
            h0, h1 = 2 * pair, 2 * pair + 1
            o_ref[:, pair * LANE:(pair + 1) * LANE] = acc_ref[h0] / l_ref[h0] + acc_ref[h1] / l_ref[h1]


def _mla_attn(q, k, v):
    B, H, S, _ = q.shape
    t = min(MLA_TILE, S)
    nq = S // t
    return pl.pallas_call(
        _mla_attn_kernel, name="mla_attn",
        grid=(B, nq, nq),
        in_specs=[pl.BlockSpec((None, H, t, LANE), lambda b, i, j: (b, 0, i, 0)),
                  pl.BlockSpec((None, H, t, LANE), lambda b, i, j: (b, 0, jnp.minimum(i, j), 0)),
                  pl.BlockSpec((None, H, t, LANE), lambda b, i, j: (b, 0, jnp.minimum(i, j), 0))],
        out_specs=pl.BlockSpec((None, t, GROUP_W), lambda b, i, j: (b, i, 0)),
        out_shape=jax.ShapeDtypeStruct((B, S, GROUP_W), F32),
        scratch_shapes=[pltpu.VMEM((H, t, 1), F32), pltpu.VMEM((H, t, 1), F32), pltpu.VMEM((H, t, LANE), F32)],
        compiler_params=_cparams("parallel", "parallel", "arbitrary"),
    )(q, k, v)


def _nsa_cmp_kernel(uk_ref, uv_ref, pek_ref, pev_ref, w1k_ref, w1v_ref, w2k_ref, w2v_ref, o_ref, sh_ref):
    nb = uk_ref.shape[0]
    half = uk_ref.shape[1]

    def hidden(u_ref, pe_ref, w1_ref):
        u = u_ref[...]
        first = _mm(u + pe_ref[0:1, :], w1_ref[0:half, :])
        second = _mm(u + pe_ref[1:2, :], w1_ref[half:2 * half, :])
        sh_ref[0:nb, :] = second
        sh_ref[nb:nb + 8, :] = jnp.zeros((8, NSA_CMP_HID), F32)
        return first + sh_ref[pl.ds(1, nb), :]

    hk = _silu(hidden(uk_ref, pek_ref, w1k_ref))
    hv = _silu(hidden(uv_ref, pev_ref, w1v_ref))
    o_ref[...] = (_mm(hk, w2k_ref[...]) + _mm(hv, w2v_ref[...])).astype(o_ref.dtype)


def _nsa_compress(uk, uv, pe_k, w1_k, w2_k, pe_v, w1_v, w2_v):
    B, nb, half = uk.shape
    hid = NSA_CMP_HID
    const = lambda shape: pl.BlockSpec(shape, lambda b: (0,) * len(shape))
    w2k = _pad_cols(w2_k, LANE).astype(BF16)
    w2v = jnp.pad(w2_v, ((0, 0), (NSA_DK, 0))).astype(BF16)
    return pl.pallas_call(
        _nsa_cmp_kernel, name="nsa_compress",
        grid=(B,),
        in_specs=[pl.BlockSpec((None, nb, half), lambda b: (b, 0, 0)),
                  pl.BlockSpec((None, nb, half), lambda b: (b, 0, 0)),
                  const((2, half)), const((2, half)),
                  const((2 * half, hid)), const((2 * half, hid)),
                  const((hid, LANE)), const((hid, LANE))],
        out_specs=pl.BlockSpec((None, nb, LANE), lambda b: (b, 0, 0)),
        out_shape=jax.ShapeDtypeStruct((B, nb, LANE), BF16),
        scratch_shapes=[pltpu.VMEM((nb + 8, hid), F32)],
        compiler_params=_cparams("parallel"),
    )(uk, uv, pe_k.reshape(2, half), pe_v.reshape(2, half), w1_k.astype(BF16), w1_v.astype(BF16), w2k, w2v)


def _stack_heads(q_ref):
    parts = [q_ref[:, h * LANE:(h + 1) * LANE] for h in range(NSA_HEADS)]
    return jnp.concatenate(parts, axis=0) * (NSA_DK ** -0.5)


def _unstack_heads(o):
    Q = o.shape[0] // NSA_HEADS
    lane = _iota((Q, LANE), 1)
    out = []
    for pair in range(NSA_HEADS // 2):
        even = o[(2 * pair) * Q:(2 * pair + 1) * Q]
        odd = o[(2 * pair + 1) * Q:(2 * pair + 2) * Q]
        out.append(jnp.where(lane < NSA_DV, pltpu.roll(even, NSA_DV, 1), odd))
    return jnp.concatenate(out, axis=1)


def _slope_of_row(head):
    slopes = [2.0 ** (-8.0 * (h + 1) / NSA_HEADS) for h in range(NSA_HEADS)]
    out = jnp.full(head.shape, slopes[-1], F32)
    for h in range(NSA_HEADS - 2, -1, -1):
        out = jnp.where(head == h, slopes[h], out)
    return out


def _nsa_sel_kernel(q_ref, kvc_ref, ov_ref, oc_ref, sel_ref, any_ref, *, n_slc, top_n):
    Q, H = Q_BLOCK, NSA_HEADS
    qb = pl.program_id(1)
    nc = kvc_ref.shape[0]
    qs = _stack_heads(q_ref)
    kvc = kvc_ref[...]
    s = _mm_nt(qs, kvc)
    row = _iota((H * Q, nc), 0)
    n = _iota((H * Q, nc), 1)
    qpos = qb * Q + row % Q
    slope = _slope_of_row(row // Q)
    center = (n * NSA_CMP_STRIDE).astype(F32) + 0.5 * (NSA_CMP_LEN - 1)
    s = s - slope * (qpos.astype(F32) - center)
    valid = (n * NSA_CMP_STRIDE + (NSA_CMP_LEN - 1)) <= qpos
    s = jnp.where(valid, s, NEG_INF)
    e = jnp.exp(s - jnp.max(s, axis=-1, keepdims=True))
    p = e / jnp.sum(e, axis=-1, keepdims=True)
    p = p * jnp.where(qpos >= NSA_CMP_LEN - 1, 1.0, 0.0)
    oc_ref[...] = _unstack_heads(_mm(p, kvc))

    p_sum = p[0:Q]
    for h in range(1, H):
        p_sum = p_sum + p[h * Q:(h + 1) * Q]
    imp = _mm_f32(p_sum, ov_ref[...])
    blk = _iota((Q, LANE), 1)
    q_blk = (qb * Q + _iota((Q, LANE), 0)) // NSA_SLC_LEN
    causal = blk <= q_blk
    forced = (blk == 0) | (blk == q_blk) | (blk == q_blk - 1)
    imp = jnp.where(causal, jnp.where(forced, FORCED_SCORE, imp), -1.0)
    imp = jnp.where(blk < n_slc, imp, -2.0)
    blk_f = blk.astype(F32)
    sel = jnp.zeros((Q, LANE), F32)
    for _ in range(top_n):
        m = jnp.max(imp, axis=-1, keepdims=True)
        first = jnp.min(jnp.where(imp == m, blk_f, float(LANE)), axis=-1, keepdims=True)
        hit = blk_f == first
        sel = jnp.where(hit, 1.0, sel)
        imp = jnp.where(hit, -3.0, imp)
    sel = jnp.where(causal, sel, 0.0)
    sel_ref[...] = sel.astype(BF16)
    any_ref[...] = jnp.max(sel, axis=0, keepdims=True)


def _nsa_select(p_nsa, kvc, overlap):
    B, S, _ = p_nsa.shape
    Q = Q_BLOCK
    nqb = S // Q
    nc = kvc.shape[1]
    n_slc = S // NSA_SLC_LEN
    kern = functools.partial(_nsa_sel_kernel, n_slc=n_slc, top_n=min(NSA_TOPN, n_slc))
    return pl.pallas_call(
        kern, name="nsa_select",
        grid=(B, nqb),
        in_specs=[pl.BlockSpec((None, Q, NSA_HEADS * LANE), lambda b, i: (b, i, 0)),
                  pl.BlockSpec((None, nc, LANE), lambda b, i: (b, 0, 0)),
                  pl.BlockSpec((nc, LANE), lambda b, i: (0, 0))],
        out_specs=[pl.BlockSpec((None, Q, GROUP_W), lambda b, i: (b, i, 0)),
                   pl.BlockSpec((None, Q, LANE), lambda b, i: (b, i, 0)),
                   pl.BlockSpec((None, None, 1, LANE), lambda b, i: (b, i, 0, 0))],
        out_shape=[jax.ShapeDtypeStruct((B, S, GROUP_W), F32),
                   jax.ShapeDtypeStruct((B, S, LANE), BF16),
                   jax.ShapeDtypeStruct((B, nqb, 1, LANE), F32)],
        compiler_params=_cparams("parallel", "parallel"),
    )(p_nsa, kvc, overlap)


def _nsa_attn_kernel(flags_ref, q_ref, gate_ref, oc_ref, sel_ref, kvs_ref, kvw_ref, o_ref,
                     m_ref, l_ref, acc_ref, *, nt):
    Q, H, TK = Q_BLOCK, NSA_HEADS, NSA_TILE
    b = pl.program_id(0)
    qb = pl.program_id(1)
    nqb = pl.num_programs(1)
    qs = _stack_heads(q_ref).astype(BF16)
    slopes = [2.0 ** (-8.0 * (h + 1) / H) for h in range(H)]

    m_ref[...] = jnp.full(m_ref.shape, NEG_INF, F32)
    l_ref[...] = jnp.zeros_like(l_ref)
    acc_ref[...] = jnp.zeros_like(acc_ref)
    selm = sel_ref[...]
    blk_minus_local = _iota((LANE, TK), 0) - _iota((LANE, TK), 1) // NSA_SLC_LEN
    qi = _iota((Q, TK), 0)
    ki = _iota((Q, TK), 1)
    blocks_per_tile = TK // NSA_SLC_LEN

    def tile(t, carry):
        @pl.when(flags_ref[(b * nqb + qb) * nt + t] > 0)
        def _():
            expand = jnp.where(blk_minus_local == t * blocks_per_tile, 1.0, 0.0).astype(BF16)
            picked = jnp.dot(selm, expand, preferred_element_type=F32)
            dist = qb * Q + qi - t * TK - ki
            keep = picked * jnp.where(dist >= 0, 1.0, 0.0) > 0.5
            dist_f = dist.astype(F32)
            kv = kvs_ref[pl.ds(pl.multiple_of(t * TK, TK), TK), :]
            s_all = lax.dot_general(qs, kv, (((1,), (1,)), ((), ())), preferred_element_type=F32)
            for h in range(H):
                rows = slice(h * Q, (h + 1) * Q)
                s = jnp.where(keep, s_all[rows] - slopes[h] * dist_f, NEG_INF)
                m_prev = m_ref[rows]
                m_new = jnp.maximum(m_prev, jnp.max(s, axis=-1, keepdims=True))
                alpha = jnp.exp(m_prev - m_new)
                p = jnp.exp(s - m_new)
                l_ref[rows] = alpha * l_ref[rows] + jnp.sum(p, axis=-1, keepdims=True)
                acc_ref[rows] = alpha * acc_ref[rows] + jnp.dot(p.astype(BF16), kv, preferred_element_type=F32)
                m_ref[rows] = m_new
        return carry

    lax.fori_loop(0, (qb * Q) // TK + 1, tile, 0)
    o_s = _unstack_heads(acc_ref[...] / l_ref[...])

    W = Q + NSA_WIN
    kvw = kvw_ref[pl.ds(pl.multiple_of(qb * Q, Q), W), :]
    sw_all = lax.dot_general(qs, kvw, (((1,), (1,)), ((), ())), preferred_element_type=F32)
    dist_w = _iota((Q, W), 0) - _iota((Q, W), 1) + NSA_WIN
    kpos = qb * Q - NSA_WIN + _iota((Q, W), 1)
    ok = jnp.where(dist_w >= 0, 1.0, 0.0) * jnp.where(dist_w < NSA_WIN, 1.0, 0.0) * jnp.where(kpos >= 0, 1.0, 0.0)
    ok = ok > 0.5
    dist_wf = dist_w.astype(F32)
    p_w = []
    for h in range(H):
        s = jnp.where(ok, sw_all[h * Q:(h + 1) * Q] - slopes[h] * dist_wf, NEG_INF)
        e = jnp.exp(s - jnp.max(s, axis=-1, keepdims=True))
        p_w.append(e / jnp.sum(e, axis=-1, keepdims=True))
    o_w = _unstack_heads(_mm(jnp.concatenate(p_w, axis=0), kvw))

    gates = jax.nn.sigmoid(gate_ref[...])
    lane_head = _iota((LANE, GROUP_W), 1) // NSA_DV
    src = _iota((LANE, GROUP_W), 0)
    out = None
    for j, branch in enumerate((oc_ref[...], o_s, o_w)):
        g = _mm_f32(gates, jnp.where(src == lane_head * 3 + j, 1.0, 0.0))
        out = g * branch if out is None else out + g * branch
    o_ref[...] = out


def _nsa_attend(p_nsa, o_c, selm, flags, kvs, kvw_pad):
    B, S, _ = p_nsa.shape
    Q = Q_BLOCK
    nqb = S // Q
    nt = S // NSA_TILE
    gate_blk = (W_NSA - LANE) // LANE
    kern = functools.partial(_nsa_attn_kernel, nt=nt)
    grid_spec = pltpu.PrefetchScalarGridSpec(
        num_scalar_prefetch=1,
        grid=(B, nqb),
        in_specs=[pl.BlockSpec((None, Q, NSA_HEADS * LANE), lambda b, i, f: (b, i, 0)),
                  pl.BlockSpec((None, Q, LANE), lambda b, i, f: (b, i, gate_blk)),
                  pl.BlockSpec((None, Q, GROUP_W), lambda b, i, f: (b, i, 0)),
                  pl.BlockSpec((None, Q, LANE), lambda b, i, f: (b, i, 0)),
                  pl.BlockSpec((None, S, LANE), lambda b, i, f: (b, 0, 0)),
                  pl.BlockSpec((None, S + NSA_WIN, LANE), lambda b, i, f: (b, 0, 0))],
        out_specs=pl.BlockSpec((None, Q, GROUP_W), lambda b, i, f: (b, i, 0)),
        scratch_shapes=[pltpu.VMEM((NSA_HEADS * Q, 1), F32), pltpu.VMEM((NSA_HEADS * Q, 1), F32),
                        pltpu.VMEM((NSA_HEADS * Q, LANE), F32)],
    )
    return pl.pallas_call(
        kern, name="nsa_attend",
        grid_spec=grid_spec,
        out_shape=jax.ShapeDtypeStruct((B, S, GROUP_W), F32),
        compiler_params=_cparams("parallel", "parallel"),
    )(flags, p_nsa, p_nsa, o_c, selm, kvs, kvw_pad)


def _nsa_overlap(S):
    nc = S // NSA_CMP_STRIDE
    n = np.arange(nc)[:, None]
    j = np.arange(LANE)[None, :]
    start = n * NSA_CMP_STRIDE
    ov = (start < (j + 1) * NSA_SLC_LEN) & (start + NSA_CMP_LEN - 1 >= j * NSA_SLC_LEN)
    ov &= (n < (S - NSA_CMP_LEN) // NSA_CMP_STRIDE + 1) & (j < S // NSA_SLC_LEN)
    return jnp.asarray(ov.astype(np.float32))


def _nsa(p_nsa, pe_k, w1_k, w2_k, pe_v, w1_v, w2_v, overlap):
    B, S, _ = p_nsa.shape
    q_w = NSA_HEADS * LANE
    half = NSA_CMP_STRIDE * NSA_DK
    kvc_raw = p_nsa[:, :, q_w:q_w + LANE]
    uk = kvc_raw[:, :, :NSA_DK].reshape(B, S // NSA_CMP_STRIDE, half)
    uv = kvc_raw[:, :, NSA_DK:].reshape(B, S // NSA_CMP_STRIDE, half)
    kvc = _nsa_compress(uk, uv, pe_k, w1_k, w2_k, pe_v, w1_v, w2_v)
    o_c, selm, blk_any = _nsa_select(p_nsa, kvc, overlap)
    per_tile = NSA_TILE // NSA_SLC_LEN
    nt = S // NSA_TILE
    flags = blk_any[:, :, 0, :nt * per_tile].reshape(B, S // Q_BLOCK, nt, per_tile).max(axis=-1)
    flags = (flags > 0).astype(jnp.int32).reshape(-1)
    kvs = p_nsa[:, :, q_w + LANE:q_w + 2 * LANE].astype(BF16)
    kvw_pad = jnp.pad(p_nsa[:, :, q_w + 2 * LANE:q_w + 3 * LANE].astype(BF16), ((0, 0), (NSA_WIN, 0), (0, 0)))
    return _nsa_attend(p_nsa, o_c, selm, flags, kvs, kvw_pad)


def _out_proj_kernel(h_ref, ya_ref, yb_ref, yc_ref, yd_ref, w_ref, g_ref, b_ref, o_ref):
    mix = None
    for idx, y_ref in enumerate((ya_ref, yb_ref, yc_ref, yd_ref)):
        part = _mm(y_ref[...], w_ref[idx * GROUP_W:(idx + 1) * GROUP_W, :])
        mix = part if mix is None else mix + part
    o_ref[...] = _layer_norm(DEEPNORM_ALPHA * h_ref[...] + mix, g_ref[...], b_ref[...])


def _out_proj(h2, ys, w_out, g, b, tm=512):
    T, D = h2.shape
    row = lambda w: pl.BlockSpec((tm, w), lambda i: (i, 0))
    const = lambda shape: pl.BlockSpec(shape, lambda i: (0,) * len(shape))
    return pl.pallas_call(
        _out_proj_kernel, name="out_proj_ln",
        grid=(T // tm,),
        in_specs=[row(D), row(GROUP_W), row(GROUP_W), row(GROUP_W), row(GROUP_W),
                  const((D, D)), const((1, D)), const((1, D))],
        out_specs=row(D),
        out_shape=jax.ShapeDtypeStruct((T, D), F32),
        compiler_params=_cparams("parallel"),
    )(h2, *ys, w_out.astype(BF16), g.reshape(1, D), b.reshape(1, D))


def _mlp_kernel(h_ref, w1_ref, w2_ref, g_ref, b_ref, o_ref, acc_ref):
    f = pl.program_id(1)

    @pl.when(f == 0)
    def _():
        acc_ref[...] = jnp.zeros_like(acc_ref)

    a = jnp.maximum(_mm(h_ref[...], w1_ref[...]), 0.0)
    acc_ref[...] += _mm(a * a, w2_ref[...])

    @pl.when(f == pl.num_programs(1) - 1)
    def _():
        o_ref[...] = _layer_norm(DEEPNORM_ALPHA * h_ref[...] + acc_ref[...], g_ref[...], b_ref[...])


def _mlp(h2, w1, w2, g, b, tm=512, tf=1024):
    T, D = h2.shape
    F = w1.shape[1]
    return pl.pallas_call(
        _mlp_kernel, name="mlp_ln",
        grid=(T // tm, F // tf),
        in_specs=[pl.BlockSpec((tm, D), lambda i, f: (i, 0)),
                  pl.BlockSpec((D, tf), lambda i, f: (0, f)),
                  pl.BlockSpec((tf, D), lambda i, f: (f, 0)),
                  pl.BlockSpec((1, D), lambda i, f: (0, 0)),
                  pl.BlockSpec((1, D), lambda i, f: (0, 0))],
        out_specs=pl.BlockSpec((tm, D), lambda i, f: (i, 0)),
        out_shape=jax.ShapeDtypeStruct((T, D), F32),
        scratch_shapes=[pltpu.VMEM((tm, D), F32)],
        compiler_params=_cparams("parallel", "arbitrary"),
    )(h2, w1.astype(BF16), w2.astype(BF16), g.reshape(1, D), b.reshape(1, D))


def kernel(x, ln_emb_g, ln_emb_b, w_in, conv_w, conv_b, dt_bias, a_log, d_skip, ssm_norm_g, q_norm_g, w_uq, kv_norm_g, w_ukv, cmp_pe_k, cmp_w1_k, cmp_w2_k, cmp_pe_v, cmp_w1_v, cmp_w2_v, w_out, ln1_g, ln1_b, w_mlp1, w_mlp2, ln2_g, ln2_b):
    B, S, D = x.shape
    assert D == D_MODEL and S % NSA_TILE == 0 and S // NSA_SLC_LEN <= LANE
    T = B * S
    ret_tables = _ret_tables(S)
    mla_tables = _mla_tables(S)
    overlap = _nsa_overlap(S)
    h = _entry_ln(x.reshape(T, D), ln_emb_g, ln_emb_b)
    for l in range(w_in.shape[0]):
        p_ssm, p_mla, p_ret, p_nsa = _in_proj(h, _layout_w_in(w_in[l]))
        y_a = _ssm(p_ssm.reshape(B, S, W_SSM), conv_w[l], conv_b[l], dt_bias[l], a_log[l], d_skip[l], ssm_norm_g[l])
        wq, wk, wv = _layout_mla_weights(w_uq[l], w_ukv[l])
        q, k, v = _mla_prep(p_mla.reshape(B, S, W_MLA), q_norm_g[l], kv_norm_g[l], wq, wk, wv, mla_tables)
        y_b = _mla_attn(q, k, v)
        y_c = _ret(p_ret.reshape(B, S, W_RET), ret_tables)
        y_d = _nsa(p_nsa.reshape(B, S, W_NSA), cmp_pe_k[l], cmp_w1_k[l], cmp_w2_k[l],
                   cmp_pe_v[l], cmp_w1_v[l], cmp_w2_v[l], overlap)
        ys = [y.reshape(T, GROUP_W) for y in (y_a, y_b, y_c, y_d)]
        h = _out_proj(h, ys, w_out[l], ln1_g[l], ln1_b[l])
        h = _mlp(h, w_mlp1[l], w_mlp2[l], ln2_g[l], ln2_b[l])
    return h.reshape(B, S, D)
```

```python
import functools
import math

import jax
import jax.numpy as jnp
import numpy as np
from jax import lax
from jax.experimental import pallas as pl
from jax.experimental.pallas import tpu as pltpu

F32 = jnp.float32
BF16 = jnp.bfloat16
HIGHEST = lax.Precision.HIGHEST

D_MODEL = 1024
DEPTH = 2
GROUP_W = D_MODEL // 4
SSM_HEADS = 4
SSM_HEAD_DIM = GROUP_W // SSM_HEADS
SSM_GROUPS = 2
SSM_STATE = 128
SSM_CONV = 4
SSM_CHUNK = 128
SSM_XBC = GROUP_W + 2 * SSM_GROUPS * SSM_STATE
MLA_HEADS = 4
MLA_NOPE = 64
MLA_ROPE = 32
MLA_V = GROUP_W // MLA_HEADS
MLA_Q_RANK = 256
MLA_KV_RANK = 128
RET_HEADS = 4
RET_DK = 64
RET_DV = GROUP_W // RET_HEADS
RET_CHUNK = 128
NSA_HEADS = 4
NSA_DK = 64
NSA_DV = GROUP_W // NSA_HEADS
NSA_CMP_LEN = 32
NSA_CMP_STRIDE = 16
NSA_CMP_HID = 256
NSA_SLC_LEN = 64
NSA_TOPN = 16
NSA_WIN = 512
D_FF = 4 * D_MODEL
Q_BLOCK = 128
ROPE_THETA = 10000.0
EPS = 1e-5
NEG_INF = -1e30
LOG2_E = math.log2(math.e)
FORCED_SCORE = 1e9
DEEPNORM_ALPHA = (2.0 * DEPTH) ** 0.25

IN_SPLITS = (
    GROUP_W, SSM_XBC, SSM_HEADS,
    MLA_Q_RANK, MLA_KV_RANK, MLA_ROPE,
    RET_HEADS * RET_DK, RET_HEADS * RET_DK, RET_HEADS * RET_DV, GROUP_W,
    NSA_HEADS * NSA_DK, NSA_DK, NSA_DV, NSA_DK, NSA_DV, NSA_DK, NSA_DV, 3 * NSA_HEADS,
)

LANE = 128
W_SSM = GROUP_W + SSM_XBC + LANE
W_MLA = MLA_Q_RANK + MLA_KV_RANK + LANE
W_RET = 4 * GROUP_W
W_NSA = NSA_HEADS * LANE + 3 * LANE + LANE
W_PROJ = W_SSM + W_MLA + W_RET + W_NSA

NSA_TILE = 512
MLA_TILE = 512
VMEM_LIMIT = 48 * 1024 * 1024


def _cparams(*sem):
    return pltpu.CompilerParams(dimension_semantics=sem, vmem_limit_bytes=VMEM_LIMIT)


def _mm(a, b):
    return jnp.dot(a.astype(BF16), b.astype(BF16), preferred_element_type=F32)


def _mm_nt(a, b):
    return lax.dot_general(a.astype(BF16), b.astype(BF16), (((1,), (1,)), ((), ())),
                           preferred_element_type=F32)


def _mm_f32(a, b):
    return jnp.dot(a, b, precision=HIGHEST, preferred_element_type=F32)


def _silu(x):
    return x * jax.nn.sigmoid(x)


def _softplus(x):
    return jnp.maximum(x, 0.0) + jnp.log1p(jnp.exp(-jnp.abs(x)))


def _layer_norm(x, g, b):
    mu = jnp.mean(x, axis=-1, keepdims=True)
    xc = x - mu
    var = jnp.mean(xc * xc, axis=-1, keepdims=True)
    return xc * lax.rsqrt(var + EPS) * g + b


def _iota(shape, dim):
    return lax.broadcasted_iota(jnp.int32, shape, dim)


def _ln_kernel(x_ref, g_ref, b_ref, o_ref):
    o_ref[...] = _layer_norm(x_ref[...], g_ref[...], b_ref[...])


def _entry_ln(x2, g, b, tm=512):
    T, D = x2.shape
    return pl.pallas_call(
        _ln_kernel, name="entry_ln",
        grid=(T // tm,),
        in_specs=[pl.BlockSpec((tm, D), lambda i: (i, 0)),
                  pl.BlockSpec((1, D), lambda i: (0, 0)),
                  pl.BlockSpec((1, D), lambda i: (0, 0))],
        out_specs=pl.BlockSpec((tm, D), lambda i: (i, 0)),
        out_shape=jax.ShapeDtypeStruct((T, D), F32),
        compiler_params=_cparams("parallel"),
    )(x2, g.reshape(1, D), b.reshape(1, D))


def _pad_cols(w, width):
    return jnp.pad(w, ((0, 0), (0, width - w.shape[1])))


def _layout_w_in(w):
    offs = np.concatenate([[0], np.cumsum(IN_SPLITS)])
    p = [w[:, int(offs[i]):int(offs[i + 1])] for i in range(len(IN_SPLITS))]
    (ssm_z, ssm_xbc, ssm_dt, mla_cq, mla_ckv, mla_kr, ret_q, ret_k, ret_v, ret_g,
     nsa_q, nsa_kc, nsa_vc, nsa_ks, nsa_vs, nsa_kw, nsa_vw, nsa_gate) = p
    nsa_q_heads = [_pad_cols(nsa_q[:, h * NSA_DK:(h + 1) * NSA_DK], LANE) for h in range(NSA_HEADS)]
    cols = [ssm_z, ssm_xbc, _pad_cols(ssm_dt, LANE),
            mla_cq, mla_ckv, _pad_cols(mla_kr, LANE),
            ret_q, ret_k, ret_v, ret_g,
            *nsa_q_heads, nsa_kc, nsa_vc, nsa_ks, nsa_vs, nsa_kw, nsa_vw, _pad_cols(nsa_gate, LANE)]
    out = jnp.concatenate(cols, axis=1)
    assert out.shape[1] == W_PROJ
    return out.astype(BF16)


def _in_proj_kernel(h_ref, w_ref, ssm_ref, mla_ref, ret_ref, nsa_ref):
    hb = h_ref[...].astype(BF16)
    off = 0
    for ref, width in ((ssm_ref, W_SSM), (mla_ref, W_MLA), (ret_ref, W_RET), (nsa_ref, W_NSA)):
        ref[...] = jnp.dot(hb, w_ref[:, off:off + width], preferred_element_type=F32)
        off += width


def _in_proj(h2, w_p, tm=256):
    T, D = h2.shape
    widths = (W_SSM, W_MLA, W_RET, W_NSA)
    return pl.pallas_call(
        _in_proj_kernel, name="in_proj",
        grid=(T // tm,),
        in_specs=[pl.BlockSpec((tm, D), lambda i: (i, 0)),
                  pl.BlockSpec((D, W_PROJ), lambda i: (0, 0))],
        out_specs=[pl.BlockSpec((tm, w), lambda i: (i, 0)) for w in widths],
        out_shape=[jax.ShapeDtypeStruct((T, w), F32) for w in widths],
        compiler_params=_cparams("parallel"),
    )(h2, w_p)


def _ssm_kernel(p_ref, cw_ref, cb_ref, dtb_ref, alog_ref, dskip_ref, ng_ref, o_ref, state_ref, ext_ref):
    L, H, P, N = SSM_CHUNK, SSM_HEADS, SSM_HEAD_DIM, SSM_STATE
    c = pl.program_id(1)

    @pl.when(c == 0)
    def _():
        state_ref[...] = jnp.zeros_like(state_ref)
        ext_ref[0:8, :] = jnp.zeros((8, SSM_XBC), F32)

    z = p_ref[:, 0:GROUP_W]
    ext_ref[8:8 + L, :] = p_ref[:, GROUP_W:GROUP_W + SSM_XBC]
    conv = cb_ref[...]
    for j in range(SSM_CONV):
        conv = conv + ext_ref[pl.ds(8 - (SSM_CONV - 1) + j, L), :] * cw_ref[j:j + 1, :]
    ext_ref[0:8, :] = ext_ref[L:L + 8, :]
    xbc = _silu(conv)
    xs = xbc[:, 0:GROUP_W]
    b_in = xbc[:, GROUP_W:GROUP_W + SSM_GROUPS * N]
    c_in = xbc[:, GROUP_W + SSM_GROUPS * N:]

    dt = _softplus(p_ref[:, GROUP_W + SSM_XBC:] + dtb_ref[...])
    a = dt * (-jnp.exp(alog_ref[...]))
    row = _iota((L, L), 0)
    col = _iota((L, L), 1)
    tril = col <= row
    cs = _mm_f32(jnp.where(tril, 1.0, 0.0), a)
    cs_t = cs.T
    ecs = jnp.exp(cs)
    dte = jnp.exp(cs[L - 1:L, :] - cs)
    expand = jnp.where(_iota((LANE, H * P), 0) == _iota((LANE, H * P), 1) // P, 1.0, 0.0)
    dt_x = _mm_f32(dt, expand)
    ecs_x = _mm_f32(ecs, expand)
    dte_x = _mm_f32(dte, expand)

    xdt = xs * dt_x
    wx = xdt * dte_x
    head_of_lane = _iota((L, H * P), 1) // P
    y = xs * dskip_ref[...]
    y_off = []
    rep = H // SSM_GROUPS
    for g in range(SSM_GROUPS):
        cg = c_in[:, g * N:(g + 1) * N]
        bg = b_in[:, g * N:(g + 1) * N]
        cb = _mm_nt(cg, bg)
        for h in range(g * rep, (g + 1) * rep):
            diff = cs[:, h:h + 1] - cs_t[h:h + 1, :]
            seg = jnp.where(tril, jnp.exp(jnp.where(tril, diff, 0.0)), 0.0)
            yh = _mm(cb * seg, xdt)
            y = y + jnp.where(head_of_lane == h, yh, 0.0)
        lanes = slice(g * rep * P, (g + 1) * rep * P)
        st_prev = state_ref[:, lanes]
        y_off.append(_mm(cg, st_prev))
        state_ref[:, lanes] = st_prev * ecs_x[L - 1:L, lanes] + _mm(bg.T, wx[:, lanes])
    y = y + jnp.concatenate(y_off, axis=1) * ecs_x
    y = y * _silu(z)
    ms = jnp.mean(y * y, axis=-1, keepdims=True)
    o_ref[...] = y * lax.rsqrt(ms + EPS) * ng_ref[...]


def _ssm(p_ssm, conv_w, conv_b, dt_bias, a_log, d_skip, norm_g):
    B, S, _ = p_ssm.shape
    L = SSM_CHUNK
    pad_h = lambda v: jnp.pad(v, (0, LANE - SSM_HEADS)).reshape(1, LANE)
    const = lambda shape: pl.BlockSpec(shape, lambda b, c: (0,) * len(shape))
    return pl.pallas_call(
        _ssm_kernel, name="ssm",
        grid=(B, S // L),
        in_specs=[pl.BlockSpec((None, L, W_SSM), lambda b, c: (b, c, 0)),
                  const((SSM_CONV, SSM_XBC)), const((1, SSM_XBC)), const((1, LANE)), const((1, LANE)),
                  const((1, GROUP_W)), const((1, GROUP_W))],
        out_specs=pl.BlockSpec((None, L, GROUP_W), lambda b, c: (b, c, 0)),
        out_shape=jax.ShapeDtypeStruct((B, S, GROUP_W), F32),
        scratch_shapes=[pltpu.VMEM((SSM_STATE, GROUP_W), F32),
                        pltpu.VMEM((L + 8, SSM_XBC), F32)],
        compiler_params=_cparams("parallel", "arbitrary"),
    )(p_ssm, conv_w, conv_b.reshape(1, -1), pad_h(dt_bias), pad_h(a_log),
      jnp.repeat(d_skip, SSM_HEAD_DIM).reshape(1, GROUP_W), norm_g.reshape(1, GROUP_W))


def _ret_kernel(p_ref, cos_ref, sin_ref, dec_ref, zeta_ref, xi_ref, cd_ref, o_ref, state_ref):
    L, H, DK, DV = RET_CHUNK, RET_HEADS, RET_DK, RET_DV
    W = H * DK
    c = pl.program_id(1)

    @pl.when(c == 0)
    def _():
        state_ref[...] = jnp.zeros_like(state_ref)

    q = p_ref[:, 0:W]
    k = p_ref[:, W:2 * W]
    v = p_ref[:, 2 * W:3 * W]
    gate = p_ref[:, 3 * W:4 * W]
    lane = _iota((L, W), 1)
    first_half = (lane % DK) < (DK // 2)
    head_of_lane = lane // DK

    def rope(x):
        partner = jnp.where(first_half, pltpu.roll(x, W - DK // 2, 1), pltpu.roll(x, DK // 2, 1))
        return x * cos_ref[...] + partner * sin_ref[...]

    qr = rope(q)
    kr = rope(k) * (DK ** -0.5)
    y = jnp.zeros((L, H * DV), F32)
    for h in range(H):
        qh = jnp.where(head_of_lane == h, qr, 0.0)
        sc = _mm_nt(qh, kr) * dec_ref[h]
        y = y + jnp.where(head_of_lane == h, _mm(sc, v), 0.0)
    st = state_ref[...]
    y = y + _mm(qr * xi_ref[...], st)
    same_head = (_iota((W, H * DV), 0) // DK) == (_iota((W, H * DV), 1) // DV)
    kv = _mm((kr * zeta_ref[...]).T, v)
    state_ref[...] = st * cd_ref[...] + jnp.where(same_head, kv, 0.0)
    ms = _mm_f32(y * y, jnp.where(same_head, 1.0 / DV, 0.0))
    o_ref[...] = y * lax.rsqrt(ms + EPS) * _silu(gate)


def _ret_tables(S):
    H, DK, L = RET_HEADS, RET_DK, RET_CHUNK
    inv = ROPE_THETA ** (-jnp.arange(0, DK, 2, dtype=F32) / DK)
    ang = jnp.arange(S, dtype=F32)[:, None] * inv[None, :]
    cos, sin = jnp.cos(ang), jnp.sin(ang)
    cos_t = jnp.tile(jnp.concatenate([cos, cos], axis=1), (1, H))
    sin_t = jnp.tile(jnp.concatenate([-sin, sin], axis=1), (1, H))
    log_gamma = jnp.log1p(-jnp.exp2(-5.0 - jnp.arange(H, dtype=F32)))
    pos = jnp.arange(L, dtype=F32)
    diff = pos[:, None] - pos[None, :]
    decay_in = jnp.where(diff >= 0, jnp.exp(jnp.maximum(diff, 0.0)[None] * log_gamma[:, None, None]), 0.0)
    zeta = jnp.exp((L - 1 - pos)[None] * log_gamma[:, None])
    xi = jnp.exp((pos + 1.0)[None] * log_gamma[:, None])
    chunk_decay = jnp.exp(L * log_gamma)
    zeta_x = jnp.repeat(zeta.T, DK, axis=1)
    xi_x = jnp.repeat(xi.T, DK, axis=1)
    cd_x = jnp.repeat(chunk_decay, RET_DV).reshape(1, H * RET_DV)
    return cos_t, sin_t, decay_in, zeta_x, xi_x, cd_x


def _ret(p_ret, tables):
    B, S, _ = p_ret.shape
    L, H = RET_CHUNK, RET_HEADS
    W = H * RET_DK
    cos_t, sin_t, decay_in, zeta_x, xi_x, cd_x = tables
    const = lambda shape: pl.BlockSpec(shape, lambda b, c: (0,) * len(shape))
    return pl.pallas_call(
        _ret_kernel, name="retention",
        grid=(B, S // L),
        in_specs=[pl.BlockSpec((None, L, W_RET), lambda b, c: (b, c, 0)),
                  pl.BlockSpec((L, W), lambda b, c: (c, 0)),
                  pl.BlockSpec((L, W), lambda b, c: (c, 0)),
                  const((H, L, L)), const((L, W)), const((L, W)), const((1, H * RET_DV))],
        out_specs=pl.BlockSpec((None, L, GROUP_W), lambda b, c: (b, c, 0)),
        out_shape=jax.ShapeDtypeStruct((B, S, GROUP_W), F32),
        scratch_shapes=[pltpu.VMEM((W, H * RET_DV), F32)],
        compiler_params=_cparams("parallel", "arbitrary"),
    )(p_ret, cos_t, sin_t, decay_in, zeta_x, xi_x, cd_x)


def _mla_prep_kernel(p_ref, qg_ref, wq_ref, kvg_ref, wk_ref, wv_ref, cos_ref, sin_ref,
                     q_ref, k_ref, v_ref):
    tm = p_ref.shape[0]
    cq = p_ref[:, 0:MLA_Q_RANK]
    ckv = p_ref[:, MLA_Q_RANK:MLA_Q_RANK + MLA_KV_RANK]
    kr = p_ref[:, MLA_Q_RANK + MLA_KV_RANK:]

    def rms(x, g):
        return x * lax.rsqrt(jnp.mean(x * x, axis=-1, keepdims=True) + EPS) * g

    q = _mm(rms(cq, qg_ref[...]), wq_ref[...])
    kvl = rms(ckv, kvg_ref[...])
    kn = _mm(kvl, wk_ref[...])
    vv = _mm(kvl, wv_ref[...])
    kr_sh = pltpu.roll(kr, MLA_NOPE, 1)
    lane = _iota((tm, LANE), 1)
    half = MLA_ROPE // 2
    low = (lane >= MLA_NOPE) & (lane < MLA_NOPE + half)
    cos = cos_ref[...]
    sin = sin_ref[...]

    def rope(x):
        partner = jnp.where(low, pltpu.roll(x, LANE - half, 1), pltpu.roll(x, half, 1))
        return x * cos + partner * sin

    scale = (MLA_NOPE + MLA_ROPE) ** -0.5 * LOG2_E
    for h in range(MLA_HEADS):
        sl = slice(h * LANE, (h + 1) * LANE)
        q_ref[h] = (rope(q[:, sl]) * scale).astype(BF16)
        k_ref[h] = rope(kn[:, sl] + kr_sh).astype(BF16)
        v_ref[h] = jnp.where(lane == _mla_ones_lane(h), 1.0, vv[:, sl]).astype(BF16)


def _mla_tables(S):
    half = MLA_ROPE // 2
    inv = ROPE_THETA ** (-jnp.arange(0, MLA_ROPE, 2, dtype=F32) / MLA_ROPE)
    ang = jnp.arange(S, dtype=F32)[:, None] * inv[None, :]
    cos, sin = jnp.cos(ang), jnp.sin(ang)
    ones = jnp.ones((S, MLA_NOPE), F32)
    tail = LANE - MLA_NOPE - MLA_ROPE
    cos_t = jnp.concatenate([ones, cos, cos, jnp.ones((S, tail), F32)], axis=1)
    sin_t = jnp.concatenate([jnp.zeros((S, MLA_NOPE), F32), -sin, sin, jnp.zeros((S, tail), F32)], axis=1)
    del half
    return cos_t, sin_t


def _layout_mla_weights(w_uq, w_ukv):
    H = MLA_HEADS
    dq = MLA_NOPE + MLA_ROPE
    wq = jnp.concatenate([_pad_cols(w_uq[:, h * dq:(h + 1) * dq], LANE) for h in range(H)], axis=1)
    dkv = MLA_NOPE + MLA_V
    wk, wv = [], []
    for h in range(H):
        blk = w_ukv[:, h * dkv:(h + 1) * dkv]
        wk.append(_pad_cols(blk[:, :MLA_NOPE], LANE))
        v = blk[:, MLA_NOPE:]
        zero = jnp.zeros_like(v)
        wv.append(jnp.concatenate([v, zero] if h % 2 == 0 else [zero, v], axis=1))
    return wq.astype(BF16), jnp.concatenate(wk, axis=1).astype(BF16), jnp.concatenate(wv, axis=1).astype(BF16)


def _mla_prep(p_mla, q_norm_g, kv_norm_g, wq, wk, wv, tables, tm=512):
    B, S, _ = p_mla.shape
    H = MLA_HEADS
    cos_t, sin_t = tables
    const = lambda shape: pl.BlockSpec(shape, lambda b, i: (0,) * len(shape))
    qkv_spec = pl.BlockSpec((None, H, tm, LANE), lambda b, i: (b, 0, i, 0))
    qkv_shape = jax.ShapeDtypeStruct((B, H, S, LANE), BF16)
    return pl.pallas_call(
        _mla_prep_kernel, name="mla_prep",
        grid=(B, S // tm),
        in_specs=[pl.BlockSpec((None, tm, W_MLA), lambda b, i: (b, i, 0)),
                  const((1, MLA_Q_RANK)), const((MLA_Q_RANK, H * LANE)),
                  const((1, MLA_KV_RANK)), const((MLA_KV_RANK, H * LANE)), const((MLA_KV_RANK, H * LANE)),
                  pl.BlockSpec((tm, LANE), lambda b, i: (i, 0)),
                  pl.BlockSpec((tm, LANE), lambda b, i: (i, 0))],
        out_specs=[qkv_spec, qkv_spec, qkv_spec],
        out_shape=[qkv_shape, qkv_shape, qkv_shape],
        compiler_params=_cparams("parallel", "parallel"),
    )(p_mla, q_norm_g.reshape(1, -1), wq, kv_norm_g.reshape(1, -1), wk, wv, cos_t, sin_t)


def _mla_ones_lane(h):
    return MLA_V if h % 2 == 0 else 0


def _mla_attn_kernel(q_ref, k_ref, v_ref, o_ref, m_ref, acc_ref):
    H = MLA_HEADS
    tq, tk = q_ref.shape[1], k_ref.shape[1]
    i = pl.program_id(1)
    j = pl.program_id(2)

    @pl.when(j == 0)
    def _():
        m_ref[...] = jnp.full(m_ref.shape, NEG_INF, F32)
        acc_ref[...] = jnp.zeros_like(acc_ref)

    def sweep(masked):
        if masked:
            causal = _iota((tq, tk), 1) <= _iota((tq, tk), 0)
        scores = [lax.dot_general(q_ref[h], k_ref[h], (((1,), (1,)), ((), ())), preferred_element_type=F32)
                  for h in range(H)]
        for h in range(H):
            s = scores[h]
            if masked:
                s = jnp.where(causal, s, NEG_INF)
            m_prev = m_ref[h]
            m_new = jnp.maximum(m_prev, jnp.max(s, axis=-1, keepdims=True))
            p = jnp.exp2(s - jnp.tile(m_new, (1, tk // LANE)))
            acc_ref[h] = (jnp.exp2(m_prev - m_new) * acc_ref[h]
                          + jnp.dot(p.astype(BF16), v_ref[h], preferred_element_type=F32))
            m_ref[h] = m_new

    @pl.when(j < i)
    def _():
        sweep(False)

    @pl.when(j == i)
    def _():
        sweep(True)
        lane = _iota((tq, LANE), 1)
        for pair in range(H // 2):
            he, ho = 2 * pair, 2 * pair + 1
            acc_e, acc_o = acc_ref[he], acc_ref[ho]
            le = acc_e[:, _mla_ones_lane(he):_mla_ones_lane(he) + 1]
            lo = acc_o[:, _mla_ones_lane(ho):_mla_ones_lane(ho) + 1]
            o_ref[:, pair * LANE:(pair + 1) * LANE] = jnp.where(lane < MLA_V, acc_e / le, acc_o / lo)


def _mla_attn(q, k, v):
    B, H, S, _ = q.shape
    t = min(MLA_TILE, S)
    nq = S // t
    return pl.pallas_call(
        _mla_attn_kernel, name="mla_attn",
        grid=(B, nq, nq),
        in_specs=[pl.BlockSpec((None, H, t, LANE), lambda b, i, j: (b, 0, i, 0)),
                  pl.BlockSpec((None, H, t, LANE), lambda b, i, j: (b, 0, jnp.minimum(i, j), 0)),
                  pl.BlockSpec((None, H, t, LANE), lambda b, i, j: (b, 0, jnp.minimum(i, j), 0))],
        out_specs=pl.BlockSpec((None, t, GROUP_W), lambda b, i, j: (b, i, 0)),
        out_shape=jax.ShapeDtypeStruct((B, S, GROUP_W), F32),
        scratch_shapes=[pltpu.VMEM((H, t, LANE), F32), pltpu.VMEM((H, t, LANE), F32)],
        compiler_params=_cparams("parallel", "parallel", "arbitrary"),
    )(q, k, v)


POS_HI = NSA_DK
POS_LO = NSA_DK + 3
POS_ONE = NSA_DK + 6
POS_PAD = NSA_DK + 7
ONES_LANE = NSA_DV


def _split_bf16(x, parts=3):
    out, rem = [], np.float64(x)
    for _ in range(parts):
        piece = np.float64(np.float32(rem).astype(jnp.bfloat16).astype(np.float32))
        out.append(float(piece))
        rem = rem - piece
    return out


def _nsa_query_table():
    H = NSA_HEADS
    tab = np.zeros((2 * H, LANE), np.float32)
    for h in range(H):
        c = 2.0 ** (-8.0 * (h + 1) / H) * LOG2_E
        pieces = _split_bf16(c)
        tab[h, POS_HI:POS_HI + 3] = pieces
        tab[h, POS_LO:POS_LO + 3] = pieces
        tab[h, POS_PAD] = NEG_INF
        tab[H + h, POS_ONE] = -sum(pieces)
    return jnp.asarray(tab)


def _nsa_pos_lanes(pos, lo_offset=0.0):
    t = np.zeros((len(pos), LANE - NSA_DK), np.float32)
    t[:, POS_HI - NSA_DK:POS_HI - NSA_DK + 3] = (NSA_SLC_LEN * (pos // NSA_SLC_LEN))[:, None]
    t[:, POS_LO - NSA_DK:POS_LO - NSA_DK + 3] = (pos % NSA_SLC_LEN + lo_offset)[:, None]
    t[:, POS_ONE - NSA_DK] = 1.0
    return t


def _nsa_queries(q_ref, qtab_ref, qb):
    Q, H = q_ref.shape[0], NSA_HEADS
    qpos = (qb * Q + _iota((Q, 1), 0)).astype(F32)
    out = []
    for h in range(H):
        q = q_ref[:, h * LANE:(h + 1) * LANE] * (NSA_DK ** -0.5 * LOG2_E)
        out.append((q + qtab_ref[h:h + 1, :] + qtab_ref[H + h:H + h + 1, :] * qpos).astype(BF16))
    return out


def _normalise(o):
    return o / o[:, ONES_LANE:ONES_LANE + 1]


def _unstack_heads(o):
    Q = o.shape[0] // NSA_HEADS
    lane = _iota((Q, LANE), 1)
    out = []
    for pair in range(NSA_HEADS // 2):
        even = o[(2 * pair) * Q:(2 * pair + 1) * Q]
        odd = o[(2 * pair + 1) * Q:(2 * pair + 2) * Q]
        out.append(jnp.where(lane < NSA_DV, even, pltpu.roll(odd, NSA_DV, 1)))
    return jnp.concatenate(out, axis=1)


def _nsa_cmp_kernel(uk_ref, uv_ref, pek_ref, pev_ref, w1k_ref, w1v_ref, w2k_ref, w2v_ref, cpos_ref,
                    kc_ref, vc_ref, sh_ref):
    nb = uk_ref.shape[0]
    half = uk_ref.shape[1]

    def hidden(u_ref, pe_ref, w1_ref):
        u = u_ref[...]
        first = _mm(u + pe_ref[0:1, :], w1_ref[0:half, :])
        second = _mm(u + pe_ref[1:2, :], w1_ref[half:2 * half, :])
        sh_ref[0:nb, :] = second
        sh_ref[nb:nb + 8, :] = jnp.zeros((8, NSA_CMP_HID), F32)
        return first + sh_ref[pl.ds(1, nb), :]

    hk = _silu(hidden(uk_ref, pek_ref, w1k_ref))
    hv = _silu(hidden(uv_ref, pev_ref, w1v_ref))
    kc_ref[...] = (_mm(hk, w2k_ref[...]) + cpos_ref[...]).astype(BF16)
    ones_lane = jnp.where(_iota((1, LANE), 1) == ONES_LANE, 1.0, 0.0)
    vc_ref[...] = (_mm(hv, w2v_ref[...]) + ones_lane).astype(BF16)


def _nsa_compress(uk, uv, pe_k, w1_k, w2_k, pe_v, w1_v, w2_v):
    B, nb, half = uk.shape
    hid = NSA_CMP_HID
    const = lambda shape: pl.BlockSpec(shape, lambda b: (0,) * len(shape))
    w2k = _pad_cols(w2_k, LANE).astype(BF16)
    w2v = _pad_cols(w2_v, LANE).astype(BF16)
    centre = _nsa_pos_lanes(np.arange(nb) * NSA_CMP_STRIDE, 0.5 * (NSA_CMP_LEN - 1))
    cpos = jnp.asarray(np.concatenate([np.zeros((nb, NSA_DK), np.float32), centre], axis=1))
    out_spec = pl.BlockSpec((None, nb, LANE), lambda b: (b, 0, 0))
    out_shape = jax.ShapeDtypeStruct((B, nb, LANE), BF16)
    return pl.pallas_call(
        _nsa_cmp_kernel, name="nsa_compress",
        grid=(B,),
        in_specs=[pl.BlockSpec((None, nb, half), lambda b: (b, 0, 0)),
                  pl.BlockSpec((None, nb, half), lambda b: (b, 0, 0)),
                  const((2, half)), const((2, half)),
                  const((2 * half, hid)), const((2 * half, hid)),
                  const((hid, LANE)), const((hid, LANE)), const((nb, LANE))],
        out_specs=[out_spec, out_spec],
        out_shape=[out_shape, out_shape],
        scratch_shapes=[pltpu.VMEM((nb + 8, hid), F32)],
        compiler_params=_cparams("parallel"),
    )(uk, uv, pe_k.reshape(2, half), pe_v.reshape(2, half), w1_k.astype(BF16), w1_v.astype(BF16), w2k, w2v, cpos)


def _nsa_sel_kernel(q_ref, qtab_ref, kc_ref, vc_ref, ovt_ref, oc_ref, selb_ref, any_ref, *, n_slc, top_n):
    Q, H = Q_BLOCK, NSA_HEADS
    qb = pl.program_id(1)
    nc = kc_ref.shape[0]
    qs = jnp.concatenate(_nsa_queries(q_ref, qtab_ref, qb), axis=0)
    s = lax.dot_general(qs, kc_ref[...], (((1,), (1,)), ((), ())), preferred_element_type=F32)
    qpos = qb * Q + (_iota((H * Q, nc), 0) & (Q - 1))
    block_end = _iota((H * Q, nc), 1) * NSA_CMP_STRIDE + (NSA_CMP_LEN - 1)
    s = jnp.where(block_end <= qpos, s, NEG_INF)
    e = jnp.exp2(s - jnp.max(s, axis=-1, keepdims=True))
    qpos_col = qb * Q + (_iota((H * Q, 1), 0) & (Q - 1))
    has_block = jnp.where(qpos_col >= NSA_CMP_LEN - 1, 1.0, 0.0)
    p = e * (has_block / jnp.sum(e, axis=-1, keepdims=True))
    oc_ref[...] = _unstack_heads(jnp.dot(p.astype(BF16), vc_ref[...], preferred_element_type=F32))

    p_sum = p[0:Q]
    for h in range(1, H):
        p_sum = p_sum + p[h * Q:(h + 1) * Q]
    imp = lax.dot_general(ovt_ref[...], p_sum, (((1,), (1,)), ((), ())), precision=HIGHEST,
                          preferred_element_type=F32)
    blk = _iota((LANE, Q), 0)
    q_blk = (qb * Q + _iota((LANE, Q), 1)) >> int(math.log2(NSA_SLC_LEN))
    causal = blk <= q_blk
    for forced_blk in (0, q_blk, q_blk - 1):
        imp = jnp.where(blk == forced_blk, FORCED_SCORE, imp)
    imp = jnp.where(causal, imp, -1.0)
    imp = jnp.where(blk < n_slc, imp, -2.0)
    blk_f = blk.astype(F32)
    sel = jnp.zeros((LANE, Q), F32)
    for _ in range(top_n):
        m = jnp.max(imp, axis=0, keepdims=True)
        first = jnp.min(jnp.where(imp == m, blk_f, float(LANE)), axis=0, keepdims=True)
        hit = blk_f == first
        sel = jnp.where(hit, 1.0, sel)
        imp = jnp.where(hit, -3.0, imp)
    sel = jnp.where(causal, sel, 0.0).T
    selb_ref[...] = jnp.where(sel > 0.5, 0.0, NEG_INF).astype(BF16)
    any_ref[...] = jnp.max(sel, axis=0, keepdims=True)


def _nsa_select(p_nsa, qtab, kc, vc, overlap_t):
    B, S, _ = p_nsa.shape
    Q = Q_BLOCK
    nqb = S // Q
    nc = kc.shape[1]
    n_slc = S // NSA_SLC_LEN
    kern = functools.partial(_nsa_sel_kernel, n_slc=n_slc, top_n=min(NSA_TOPN, n_slc))
    return pl.pallas_call(
        kern, name="nsa_select",
        grid=(B, nqb),
        in_specs=[pl.BlockSpec((None, Q, NSA_HEADS * LANE), lambda b, i: (b, i, 0)),
                  pl.BlockSpec((2 * NSA_HEADS, LANE), lambda b, i: (0, 0)),
                  pl.BlockSpec((None, nc, LANE), lambda b, i: (b, 0, 0)),
                  pl.BlockSpec((None, nc, LANE), lambda b, i: (b, 0, 0)),
                  pl.BlockSpec((LANE, nc), lambda b, i: (0, 0))],
        out_specs=[pl.BlockSpec((None, Q, GROUP_W), lambda b, i: (b, i, 0)),
                   pl.BlockSpec((None, Q, LANE), lambda b, i: (b, i, 0)),
                   pl.BlockSpec((None, None, 1, LANE), lambda b, i: (b, i, 0, 0))],
        out_shape=[jax.ShapeDtypeStruct((B, S, GROUP_W), F32),
                   jax.ShapeDtypeStruct((B, S, LANE), BF16),
                   jax.ShapeDtypeStruct((B, nqb, 1, LANE), F32)],
        compiler_params=_cparams("parallel", "parallel"),
    )(p_nsa, qtab, kc, vc, overlap_t)


def _nsa_attn_kernel(flags_ref, q_ref, qtab_ref, gate_ref, oc_ref, selb_ref, ks_ref, vs_ref, kw_ref, vw_ref,
                     o_ref, m_ref, acc_ref, *, nt):
    Q, H, TK = Q_BLOCK, NSA_HEADS, NSA_TILE
    b = pl.program_id(0)
    qb = pl.program_id(1)
    nqb = pl.num_programs(1)
    qh = _nsa_queries(q_ref, qtab_ref, qb)
    nt_dims = (((1,), (1,)), ((), ()))

    selb = selb_ref[...]
    qs_sel = jnp.concatenate([jnp.concatenate([q, selb], axis=1) for q in qh], axis=0)
    m_ref[...] = jnp.full(m_ref.shape, NEG_INF, F32)
    acc_ref[...] = jnp.zeros_like(acc_ref)

    def update(t, diagonal):
        rows = pl.ds(pl.multiple_of(t * TK, TK), TK)
        s = lax.dot_general(qs_sel, ks_ref[rows, :], nt_dims, preferred_element_type=F32)
        if diagonal:
            ahead = _iota((H * Q, TK), 1) - (_iota((H * Q, TK), 0) & (Q - 1))
            s = jnp.where(ahead <= qb * Q - t * TK, s, NEG_INF)
        m_prev = m_ref[...]
        m_new = jnp.maximum(m_prev, jnp.max(s, axis=-1, keepdims=True))
        p = jnp.exp2(s - jnp.tile(m_new, (1, TK // LANE)))
        acc_ref[...] = (jnp.exp2(m_prev - m_new) * acc_ref[...]
                        + jnp.dot(p.astype(BF16), vs_ref[rows, :], preferred_element_type=F32))
        m_ref[...] = m_new

    def tile(t, carry):
        @pl.when(flags_ref[(b * nqb + qb) * nt + t] > 0)
        def _():
            update(t, False)
        return carry

    t_diag = (qb * Q) // TK
    lax.fori_loop(0, t_diag, tile, 0)
    update(t_diag, True)
    o_s = _unstack_heads(_normalise(acc_ref[...]))

    W = Q + NSA_WIN
    rows_w = pl.ds(pl.multiple_of(qb * Q, Q), W)
    qs_win = jnp.concatenate(qh, axis=0)
    s = lax.dot_general(qs_win, kw_ref[rows_w, :], nt_dims, preferred_element_type=F32)
    dist_w = (_iota((H * Q, W), 0) & (Q - 1)) - _iota((H * Q, W), 1) + NSA_WIN
    s = jnp.where(dist_w >= 0, jnp.where(dist_w < NSA_WIN, s, NEG_INF), NEG_INF)
    p = jnp.exp2(s - jnp.max(s, axis=-1, keepdims=True))
    o_w = _unstack_heads(_normalise(jnp.dot(p.astype(BF16), vw_ref[rows_w, :], preferred_element_type=F32)))

    gates = jax.nn.sigmoid(gate_ref[...])
    lane_head = _iota((LANE, GROUP_W), 1) // NSA_DV
    src = _iota((LANE, GROUP_W), 0)
    out = None
    for j, branch in enumerate((oc_ref[...], o_s, o_w)):
        g = _mm_f32(gates, jnp.where(src == lane_head * 3 + j, 1.0, 0.0))
        out = g * branch if out is None else out + g * branch
    o_ref[...] = out


def _nsa_attend(p_nsa, qtab, o_c, selb, flags, ks, vs, kw, vw):
    B, S, _ = p_nsa.shape
    Q = Q_BLOCK
    nqb = S // Q
    nt = S // NSA_TILE
    gate_blk = (W_NSA - LANE) // LANE
    kern = functools.partial(_nsa_attn_kernel, nt=nt)
    whole = lambda rows, width: pl.BlockSpec((None, rows, width), lambda b, i, f: (b, 0, 0))
    grid_spec = pltpu.PrefetchScalarGridSpec(
        num_scalar_prefetch=1,
        grid=(B, nqb),
        in_specs=[pl.BlockSpec((None, Q, NSA_HEADS * LANE), lambda b, i, f: (b, i, 0)),
                  pl.BlockSpec((2 * NSA_HEADS, LANE), lambda b, i, f: (0, 0)),
                  pl.BlockSpec((None, Q, LANE), lambda b, i, f: (b, i, gate_blk)),
                  pl.BlockSpec((None, Q, GROUP_W), lambda b, i, f: (b, i, 0)),
                  pl.BlockSpec((None, Q, LANE), lambda b, i, f: (b, i, 0)),
                  whole(S, 2 * LANE), whole(S, LANE), whole(S + NSA_WIN, LANE), whole(S + NSA_WIN, LANE)],
        out_specs=pl.BlockSpec((None, Q, GROUP_W), lambda b, i, f: (b, i, 0)),
        scratch_shapes=[pltpu.VMEM((NSA_HEADS * Q, LANE), F32), pltpu.VMEM((NSA_HEADS * Q, LANE), F32)],
    )
    return pl.pallas_call(
        kern, name="nsa_attend",
        grid_spec=grid_spec,
        out_shape=jax.ShapeDtypeStruct((B, S, GROUP_W), F32),
        compiler_params=_cparams("parallel", "parallel"),
    )(flags, p_nsa, qtab, p_nsa, o_c, selb, ks, vs, kw, vw)


def _nsa_tables(S):
    nc = S // NSA_CMP_STRIDE
    n = np.arange(nc)[None, :]
    j = np.arange(LANE)[:, None]
    start = n * NSA_CMP_STRIDE
    ov = (start < (j + 1) * NSA_SLC_LEN) & (start + NSA_CMP_LEN - 1 >= j * NSA_SLC_LEN)
    ov &= (n < (S - NSA_CMP_LEN) // NSA_CMP_STRIDE + 1) & (j < S // NSA_SLC_LEN)
    pos = np.arange(S)
    sel_lanes = np.concatenate([_nsa_pos_lanes(pos),
                                (pos[:, None] // NSA_SLC_LEN == np.arange(LANE)[None, :]).astype(np.float32)], axis=1)
    win_lanes = np.concatenate([np.zeros((NSA_WIN, LANE - NSA_DK), np.float32), _nsa_pos_lanes(pos)], axis=0)
    win_lanes[:NSA_WIN, POS_ONE - NSA_DK] = 1.0
    win_lanes[:NSA_WIN, POS_PAD - NSA_DK] = 1.0
    ones = np.zeros((S, LANE - NSA_DV), np.float32)
    ones[:, ONES_LANE - NSA_DV] = 1.0
    as_bf16 = lambda a: jnp.asarray(a, dtype=BF16)
    return (_nsa_query_table(), jnp.asarray(ov.astype(np.float32)), as_bf16(sel_lanes), as_bf16(win_lanes),
            as_bf16(ones))


def _nsa(p_nsa, pe_k, w1_k, w2_k, pe_v, w1_v, w2_v, tables):
    B, S, _ = p_nsa.shape
    qtab, overlap_t, sel_lanes, win_lanes, ones = tables
    q_w = NSA_HEADS * LANE
    half = NSA_CMP_STRIDE * NSA_DK
    piece = lambda idx: p_nsa[:, :, q_w + idx * NSA_DK:q_w + (idx + 1) * NSA_DK]
    uk = piece(0).reshape(B, S // NSA_CMP_STRIDE, half)
    uv = piece(1).reshape(B, S // NSA_CMP_STRIDE, half)
    kc, vc = _nsa_compress(uk, uv, pe_k, w1_k, w2_k, pe_v, w1_v, w2_v)
    o_c, selb, blk_any = _nsa_select(p_nsa, qtab, kc, vc, overlap_t)
    per_tile = NSA_TILE // NSA_SLC_LEN
    nt = S // NSA_TILE
    flags = blk_any[:, :, 0, :nt * per_tile].reshape(B, S // Q_BLOCK, nt, per_tile).max(axis=-1)
    flags = (flags > 0).astype(jnp.int32).reshape(-1)
    bcast = lambda t: jnp.broadcast_to(t, (B,) + t.shape)
    ks = jnp.concatenate([piece(2).astype(BF16), bcast(sel_lanes)], axis=-1)
    vs = jnp.concatenate([piece(3).astype(BF16), bcast(ones)], axis=-1)
    pad = ((0, 0), (NSA_WIN, 0), (0, 0))
    kw = jnp.concatenate([jnp.pad(piece(4).astype(BF16), pad), bcast(win_lanes)], axis=-1)
    vw = jnp.pad(jnp.concatenate([piece(5).astype(BF16), bcast(ones)], axis=-1), pad)
    return _nsa_attend(p_nsa, qtab, o_c, selb, flags, ks, vs, kw, vw)


def _out_proj_kernel(h_ref, ya_ref, yb_ref, yc_ref, yd_ref, w_ref, g_ref, b_ref, o_ref):
    mix = None
    for idx, y_ref in enumerate((ya_ref, yb_ref, yc_ref, yd_ref)):
        part = _mm(y_ref[...], w_ref[idx * GROUP_W:(idx + 1) * GROUP_W, :])
        mix = part if mix is None else mix + part
    o_ref[...] = _layer_norm(DEEPNORM_ALPHA * h_ref[...] + mix, g_ref[...], b_ref[...])


def _out_proj(h2, ys, w_out, g, b, tm=512):
    T, D = h2.shape
    row = lambda w: pl.BlockSpec((tm, w), lambda i: (i, 0))
    const = lambda shape: pl.BlockSpec(shape, lambda i: (0,) * len(shape))
    return pl.pallas_call(
        _out_proj_kernel, name="out_proj_ln",
        grid=(T // tm,),
        in_specs=[row(D), row(GROUP_W), row(GROUP_W), row(GROUP_W), row(GROUP_W),
                  const((D, D)), const((1, D)), const((1, D))],
        out_specs=row(D),
        out_shape=jax.ShapeDtypeStruct((T, D), F32),
        compiler_params=_cparams("parallel"),
    )(h2, *ys, w_out.astype(BF16), g.reshape(1, D), b.reshape(1, D))


def _mlp_kernel(h_ref, w1_ref, w2_ref, g_ref, b_ref, o_ref, acc_ref):
    f = pl.program_id(1)

    @pl.when(f == 0)
    def _():
        acc_ref[...] = jnp.zeros_like(acc_ref)

    a = jnp.maximum(_mm(h_ref[...], w1_ref[...]), 0.0)
    acc_ref[...] += _mm(a * a, w2_ref[...])

    @pl.when(f == pl.num_programs(1) - 1)
    def _():
        o_ref[...] = _layer_norm(DEEPNORM_ALPHA * h_ref[...] + acc_ref[...], g_ref[...], b_ref[...])


def _mlp(h2, w1, w2, g, b, tm=512, tf=1024):
    T, D = h2.shape
    F = w1.shape[1]
    return pl.pallas_call(
        _mlp_kernel, name="mlp_ln",
        grid=(T // tm, F // tf),
        in_specs=[pl.BlockSpec((tm, D), lambda i, f: (i, 0)),
                  pl.BlockSpec((D, tf), lambda i, f: (0, f)),
                  pl.BlockSpec((tf, D), lambda i, f: (f, 0)),
                  pl.BlockSpec((1, D), lambda i, f: (0, 0)),
                  pl.BlockSpec((1, D), lambda i, f: (0, 0))],
        out_specs=pl.BlockSpec((tm, D), lambda i, f: (i, 0)),
        out_shape=jax.ShapeDtypeStruct((T, D), F32),
        scratch_shapes=[pltpu.VMEM((tm, D), F32)],
        compiler_params=_cparams("parallel", "arbitrary"),
    )(h2, w1.astype(BF16), w2.astype(BF16), g.reshape(1, D), b.reshape(1, D))


def kernel(x, ln_emb_g, ln_emb_b, w_in, conv_w, conv_b, dt_bias, a_log, d_skip, ssm_norm_g, q_norm_g, w_uq, kv_norm_g, w_ukv, cmp_pe_k, cmp_w1_k, cmp_w2_k, cmp_pe_v, cmp_w1_v, cmp_w2_v, w_out, ln1_g, ln1_b, w_mlp1, w_mlp2, ln2_g, ln2_b):
    B, S, D = x.shape
    assert D == D_MODEL and S % NSA_TILE == 0 and S // NSA_SLC_LEN <= LANE
    T = B * S
    ret_tables = _ret_tables(S)
    mla_tables = _mla_tables(S)
    nsa_tables = _nsa_tables(S)
    h = _entry_ln(x.reshape(T, D), ln_emb_g, ln_emb_b)
    for l in range(w_in.shape[0]):
        p_ssm, p_mla, p_ret, p_nsa = _in_proj(h, _layout_w_in(w_in[l]))
        y_a = _ssm(p_ssm.reshape(B, S, W_SSM), conv_w[l], conv_b[l], dt_bias[l], a_log[l], d_skip[l], ssm_norm_g[l])
        wq, wk, wv = _layout_mla_weights(w_uq[l], w_ukv[l])
        q, k, v = _mla_prep(p_mla.reshape(B, S, W_MLA), q_norm_g[l], kv_norm_g[l], wq, wk, wv, mla_tables)
        y_b = _mla_attn(q, k, v)
        y_c = _ret(p_ret.reshape(B, S, W_RET), ret_tables)
        y_d = _nsa(p_nsa.reshape(B, S, W_NSA), cmp_pe_k[l], cmp_w1_k[l], cmp_w2_k[l],
                   cmp_pe_v[l], cmp_w1_v[l], cmp_w2_v[l], nsa_tables)
        ys = [y.reshape(T, GROUP_W) for y in (y_a, y_b, y_c, y_d)]
        h = _out_proj(h, ys, w_out[l], ln1_g[l], ln1_b[l])
        h = _mlp(h, w_mlp1[l], w_mlp2[l], ln2_g[l], ln2_b[l])
    return h.reshape(B, S, D)
```

```python
import functools
import math

import jax
import jax.numpy as jnp
import numpy as np
from jax import lax
from jax.experimental import pallas as pl
from jax.experimental.pallas import tpu as pltpu

F32 = jnp.float32
BF16 = jnp.bfloat16
HIGHEST = lax.Precision.HIGHEST

D_MODEL = 1024
DEPTH = 2
GROUP_W = D_MODEL // 4
SSM_HEADS = 4
SSM_HEAD_DIM = GROUP_W // SSM_HEADS
SSM_GROUPS = 2
SSM_STATE = 128
SSM_CONV = 4
SSM_CHUNK = 128
SSM_XBC = GROUP_W + 2 * SSM_GROUPS * SSM_STATE
MLA_HEADS = 4
MLA_NOPE = 64
MLA_ROPE = 32
MLA_V = GROUP_W // MLA_HEADS
MLA_Q_RANK = 256
MLA_KV_RANK = 128
RET_HEADS = 4
RET_DK = 64
RET_DV = GROUP_W // RET_HEADS
RET_CHUNK = 128
NSA_HEADS = 4
NSA_DK = 64
NSA_DV = GROUP_W // NSA_HEADS
NSA_CMP_LEN = 32
NSA_CMP_STRIDE = 16
NSA_CMP_HID = 256
NSA_SLC_LEN = 64
NSA_TOPN = 16
NSA_WIN = 512
D_FF = 4 * D_MODEL
Q_BLOCK = 128
ROPE_THETA = 10000.0
EPS = 1e-5
NEG_INF = -1e30
LOG2_E = math.log2(math.e)
FORCED_SCORE = 1e9
DEEPNORM_ALPHA = (2.0 * DEPTH) ** 0.25

IN_SPLITS = (
    GROUP_W, SSM_XBC, SSM_HEADS,
    MLA_Q_RANK, MLA_KV_RANK, MLA_ROPE,
    RET_HEADS * RET_DK, RET_HEADS * RET_DK, RET_HEADS * RET_DV, GROUP_W,
    NSA_HEADS * NSA_DK, NSA_DK, NSA_DV, NSA_DK, NSA_DV, NSA_DK, NSA_DV, 3 * NSA_HEADS,
)

LANE = 128
W_SSM = GROUP_W + SSM_XBC + LANE
W_MLA = MLA_Q_RANK + MLA_KV_RANK + LANE
W_RET = 4 * GROUP_W
W_NSA = NSA_HEADS * LANE + 3 * LANE + LANE
W_PROJ = W_SSM + W_MLA + W_RET + W_NSA

NSA_TILE = 512
MLA_TILE = 512
VMEM_LIMIT = 48 * 1024 * 1024


def _cparams(*sem):
    return pltpu.CompilerParams(dimension_semantics=sem, vmem_limit_bytes=VMEM_LIMIT)


def _mm(a, b):
    return jnp.dot(a.astype(BF16), b.astype(BF16), preferred_element_type=F32)


def _mm_nt(a, b):
    return lax.dot_general(a.astype(BF16), b.astype(BF16), (((1,), (1,)), ((), ())),
                           preferred_element_type=F32)


def _mm_f32(a, b):
    return jnp.dot(a, b, precision=HIGHEST, preferred_element_type=F32)


def _silu(x):
    return x * jax.nn.sigmoid(x)


def _softplus(x):
    return jnp.maximum(x, 0.0) + jnp.log1p(jnp.exp(-jnp.abs(x)))


def _layer_norm(x, g, b):
    mu = jnp.mean(x, axis=-1, keepdims=True)
    xc = x - mu
    var = jnp.mean(xc * xc, axis=-1, keepdims=True)
    return xc * lax.rsqrt(var + EPS) * g + b


def _iota(shape, dim):
    return lax.broadcasted_iota(jnp.int32, shape, dim)


def _ln_kernel(x_ref, g_ref, b_ref, o_ref):
    o_ref[...] = _layer_norm(x_ref[...], g_ref[...], b_ref[...])


def _entry_ln(x2, g, b, tm=512):
    T, D = x2.shape
    return pl.pallas_call(
        _ln_kernel, name="entry_ln",
        grid=(T // tm,),
        in_specs=[pl.BlockSpec((tm, D), lambda i: (i, 0)),
                  pl.BlockSpec((1, D), lambda i: (0, 0)),
                  pl.BlockSpec((1, D), lambda i: (0, 0))],
        out_specs=pl.BlockSpec((tm, D), lambda i: (i, 0)),
        out_shape=jax.ShapeDtypeStruct((T, D), F32),
        compiler_params=_cparams("parallel"),
    )(x2, g.reshape(1, D), b.reshape(1, D))


def _pad_cols(w, width):
    return jnp.pad(w, ((0, 0), (0, width - w.shape[1])))


def _layout_w_in(w):
    offs = np.concatenate([[0], np.cumsum(IN_SPLITS)])
    p = [w[:, int(offs[i]):int(offs[i + 1])] for i in range(len(IN_SPLITS))]
    (ssm_z, ssm_xbc, ssm_dt, mla_cq, mla_ckv, mla_kr, ret_q, ret_k, ret_v, ret_g,
     nsa_q, nsa_kc, nsa_vc, nsa_ks, nsa_vs, nsa_kw, nsa_vw, nsa_gate) = p
    nsa_q_heads = [_pad_cols(nsa_q[:, h * NSA_DK:(h + 1) * NSA_DK], LANE) for h in range(NSA_HEADS)]
    cols = [ssm_z, ssm_xbc, _pad_cols(ssm_dt, LANE),
            mla_cq, mla_ckv, _pad_cols(mla_kr, LANE),
            ret_q, ret_k, ret_v, ret_g,
            *nsa_q_heads, nsa_kc, nsa_vc, nsa_ks, nsa_vs, nsa_kw, nsa_vw, _pad_cols(nsa_gate, LANE)]
    out = jnp.concatenate(cols, axis=1)
    assert out.shape[1] == W_PROJ
    return out.astype(BF16)


def _in_proj_kernel(h_ref, w_ref, ssm_ref, mla_ref, ret_ref, nsa_ref):
    hb = h_ref[...].astype(BF16)
    off = 0
    for ref, width in ((ssm_ref, W_SSM), (mla_ref, W_MLA), (ret_ref, W_RET), (nsa_ref, W_NSA)):
        ref[...] = jnp.dot(hb, w_ref[:, off:off + width], preferred_element_type=F32)
        off += width


def _in_proj(h2, w_p, tm=256):
    T, D = h2.shape
    widths = (W_SSM, W_MLA, W_RET, W_NSA)
    return pl.pallas_call(
        _in_proj_kernel, name="in_proj",
        grid=(T // tm,),
        in_specs=[pl.BlockSpec((tm, D), lambda i: (i, 0)),
                  pl.BlockSpec((D, W_PROJ), lambda i: (0, 0))],
        out_specs=[pl.BlockSpec((tm, w), lambda i: (i, 0)) for w in widths],
        out_shape=[jax.ShapeDtypeStruct((T, w), F32) for w in widths],
        compiler_params=_cparams("parallel"),
    )(h2, w_p)


def _ssm_kernel(p_ref, cw_ref, cb_ref, dtb_ref, alog_ref, dskip_ref, ng_ref, o_ref, state_ref, ext_ref):
    @pl.when(pl.program_id(0) == 0)
    def _():
        state_ref[...] = jnp.zeros_like(state_ref)
        ext_ref[:, 0:8, :] = jnp.zeros((ext_ref.shape[0], 8, SSM_XBC), F32)

    for b in range(p_ref.shape[0]):
        _ssm_chunk(p_ref.at[b], cw_ref, cb_ref, dtb_ref, alog_ref, dskip_ref, ng_ref,
                   o_ref.at[b], state_ref.at[b], ext_ref.at[b])


def _ssm_chunk(p_ref, cw_ref, cb_ref, dtb_ref, alog_ref, dskip_ref, ng_ref, o_ref, state_ref, ext_ref):
    L, H, P, N = SSM_CHUNK, SSM_HEADS, SSM_HEAD_DIM, SSM_STATE
    z = p_ref[:, 0:GROUP_W]
    ext_ref[8:8 + L, :] = p_ref[:, GROUP_W:GROUP_W + SSM_XBC]
    conv = cb_ref[...]
    for j in range(SSM_CONV):
        conv = conv + ext_ref[pl.ds(8 - (SSM_CONV - 1) + j, L), :] * cw_ref[j:j + 1, :]
    ext_ref[0:8, :] = ext_ref[L:L + 8, :]
    xbc = _silu(conv)
    xs = xbc[:, 0:GROUP_W]
    b_in = xbc[:, GROUP_W:GROUP_W + SSM_GROUPS * N]
    c_in = xbc[:, GROUP_W + SSM_GROUPS * N:]

    dt = _softplus(p_ref[:, GROUP_W + SSM_XBC:] + dtb_ref[...])
    a = dt * (-jnp.exp(alog_ref[...]))
    row = _iota((L, L), 0)
    col = _iota((L, L), 1)
    tril = col <= row
    cs = _mm_f32(jnp.where(tril, 1.0, 0.0), a)
    cs_t = cs.T
    ecs = jnp.exp(cs)
    dte = jnp.exp(cs[L - 1:L, :] - cs)
    expand = jnp.where(_iota((LANE, H * P), 0) == _iota((LANE, H * P), 1) // P, 1.0, 0.0)
    dt_x = _mm_f32(dt, expand)
    ecs_x = _mm_f32(ecs, expand)
    dte_x = _mm_f32(dte, expand)

    xdt = xs * dt_x
    wx = xdt * dte_x
    head_of_lane = _iota((L, H * P), 1) // P
    y = xs * dskip_ref[...]
    y_off = []
    rep = H // SSM_GROUPS
    for g in range(SSM_GROUPS):
        cg = c_in[:, g * N:(g + 1) * N]
        bg = b_in[:, g * N:(g + 1) * N]
        cb = _mm_nt(cg, bg)
        for h in range(g * rep, (g + 1) * rep):
            diff = cs[:, h:h + 1] - cs_t[h:h + 1, :]
            seg = jnp.where(tril, jnp.exp(jnp.where(tril, diff, 0.0)), 0.0)
            yh = _mm(cb * seg, xdt)
            y = y + jnp.where(head_of_lane == h, yh, 0.0)
        lanes = slice(g * rep * P, (g + 1) * rep * P)
        st_prev = state_ref[:, lanes]
        y_off.append(_mm(cg, st_prev))
        state_ref[:, lanes] = st_prev * ecs_x[L - 1:L, lanes] + _mm(bg.T, wx[:, lanes])
    y = y + jnp.concatenate(y_off, axis=1) * ecs_x
    y = y * _silu(z)
    ms = jnp.mean(y * y, axis=-1, keepdims=True)
    o_ref[...] = y * lax.rsqrt(ms + EPS) * ng_ref[...]


def _ssm(p_ssm, conv_w, conv_b, dt_bias, a_log, d_skip, norm_g):
    B, S, _ = p_ssm.shape
    L = SSM_CHUNK
    pad_h = lambda v: jnp.pad(v, (0, LANE - SSM_HEADS)).reshape(1, LANE)
    const = lambda shape: pl.BlockSpec(shape, lambda c: (0,) * len(shape))
    return pl.pallas_call(
        _ssm_kernel, name="ssm",
        grid=(S // L,),
        in_specs=[pl.BlockSpec((B, L, W_SSM), lambda c: (0, c, 0)),
                  const((SSM_CONV, SSM_XBC)), const((1, SSM_XBC)), const((1, LANE)), const((1, LANE)),
                  const((1, GROUP_W)), const((1, GROUP_W))],
        out_specs=pl.BlockSpec((B, L, GROUP_W), lambda c: (0, c, 0)),
        out_shape=jax.ShapeDtypeStruct((B, S, GROUP_W), F32),
        scratch_shapes=[pltpu.VMEM((B, SSM_STATE, GROUP_W), F32),
                        pltpu.VMEM((B, L + 8, SSM_XBC), F32)],
        compiler_params=_cparams("arbitrary"),
    )(p_ssm, conv_w, conv_b.reshape(1, -1), pad_h(dt_bias), pad_h(a_log),
      jnp.repeat(d_skip, SSM_HEAD_DIM).reshape(1, GROUP_W), norm_g.reshape(1, GROUP_W))


def _ret_kernel(p_ref, cos_ref, sin_ref, dec_ref, zeta_ref, xi_ref, cd_ref, o_ref, state_ref):
    @pl.when(pl.program_id(0) == 0)
    def _():
        state_ref[...] = jnp.zeros_like(state_ref)

    for b in range(p_ref.shape[0]):
        _ret_chunk(p_ref.at[b], cos_ref, sin_ref, dec_ref, zeta_ref, xi_ref, cd_ref, o_ref.at[b], state_ref.at[b])


def _ret_chunk(p_ref, cos_ref, sin_ref, dec_ref, zeta_ref, xi_ref, cd_ref, o_ref, state_ref):
    L, H, DK, DV = RET_CHUNK, RET_HEADS, RET_DK, RET_DV
    W = H * DK
    q = p_ref[:, 0:W]
    k = p_ref[:, W:2 * W]
    v = p_ref[:, 2 * W:3 * W]
    gate = p_ref[:, 3 * W:4 * W]
    lane = _iota((L, W), 1)
    first_half = (lane % DK) < (DK // 2)
    head_of_lane = lane // DK

    def rope(x):
        partner = jnp.where(first_half, pltpu.roll(x, W - DK // 2, 1), pltpu.roll(x, DK // 2, 1))
        return x * cos_ref[...] + partner * sin_ref[...]

    qr = rope(q)
    kr = rope(k) * (DK ** -0.5)
    y = jnp.zeros((L, H * DV), F32)
    for h in range(H):
        qh = jnp.where(head_of_lane == h, qr, 0.0)
        sc = _mm_nt(qh, kr) * dec_ref[h]
        y = y + jnp.where(head_of_lane == h, _mm(sc, v), 0.0)
    st = state_ref[...]
    y = y + _mm(qr * xi_ref[...], st)
    same_head = (_iota((W, H * DV), 0) // DK) == (_iota((W, H * DV), 1) // DV)
    kv = _mm((kr * zeta_ref[...]).T, v)
    state_ref[...] = st * cd_ref[...] + jnp.where(same_head, kv, 0.0)
    ms = _mm_f32(y * y, jnp.where(same_head, 1.0 / DV, 0.0))
    o_ref[...] = y * lax.rsqrt(ms + EPS) * _silu(gate)


def _ret_tables(S):
    H, DK, L = RET_HEADS, RET_DK, RET_CHUNK
    inv = ROPE_THETA ** (-jnp.arange(0, DK, 2, dtype=F32) / DK)
    ang = jnp.arange(S, dtype=F32)[:, None] * inv[None, :]
    cos, sin = jnp.cos(ang), jnp.sin(ang)
    cos_t = jnp.tile(jnp.concatenate([cos, cos], axis=1), (1, H))
    sin_t = jnp.tile(jnp.concatenate([-sin, sin], axis=1), (1, H))
    log_gamma = jnp.log1p(-jnp.exp2(-5.0 - jnp.arange(H, dtype=F32)))
    pos = jnp.arange(L, dtype=F32)
    diff = pos[:, None] - pos[None, :]
    decay_in = jnp.where(diff >= 0, jnp.exp(jnp.maximum(diff, 0.0)[None] * log_gamma[:, None, None]), 0.0)
    zeta = jnp.exp((L - 1 - pos)[None] * log_gamma[:, None])
    xi = jnp.exp((pos + 1.0)[None] * log_gamma[:, None])
    chunk_decay = jnp.exp(L * log_gamma)
    zeta_x = jnp.repeat(zeta.T, DK, axis=1)
    xi_x = jnp.repeat(xi.T, DK, axis=1)
    cd_x = jnp.repeat(chunk_decay, RET_DV).reshape(1, H * RET_DV)
    return cos_t, sin_t, decay_in, zeta_x, xi_x, cd_x


def _ret(p_ret, tables):
    B, S, _ = p_ret.shape
    L, H = RET_CHUNK, RET_HEADS
    W = H * RET_DK
    cos_t, sin_t, decay_in, zeta_x, xi_x, cd_x = tables
    const = lambda shape: pl.BlockSpec(shape, lambda c: (0,) * len(shape))
    return pl.pallas_call(
        _ret_kernel, name="retention",
        grid=(S // L,),
        in_specs=[pl.BlockSpec((B, L, W_RET), lambda c: (0, c, 0)),
                  pl.BlockSpec((L, W), lambda c: (c, 0)),
                  pl.BlockSpec((L, W), lambda c: (c, 0)),
                  const((H, L, L)), const((L, W)), const((L, W)), const((1, H * RET_DV))],
        out_specs=pl.BlockSpec((B, L, GROUP_W), lambda c: (0, c, 0)),
        out_shape=jax.ShapeDtypeStruct((B, S, GROUP_W), F32),
        scratch_shapes=[pltpu.VMEM((B, W, H * RET_DV), F32)],
        compiler_params=_cparams("arbitrary"),
    )(p_ret, cos_t, sin_t, decay_in, zeta_x, xi_x, cd_x)


def _mla_prep_kernel(p_ref, qg_ref, wq_ref, kvg_ref, wk_ref, wv_ref, cos_ref, sin_ref,
                     q_ref, k_ref, v_ref):
    tm = p_ref.shape[0]
    cq = p_ref[:, 0:MLA_Q_RANK]
    ckv = p_ref[:, MLA_Q_RANK:MLA_Q_RANK + MLA_KV_RANK]
    kr = p_ref[:, MLA_Q_RANK + MLA_KV_RANK:]

    def rms(x, g):
        return x * lax.rsqrt(jnp.mean(x * x, axis=-1, keepdims=True) + EPS) * g

    q = _mm(rms(cq, qg_ref[...]), wq_ref[...])
    kvl = rms(ckv, kvg_ref[...])
    kn = _mm(kvl, wk_ref[...])
    vv = _mm(kvl, wv_ref[...])
    kr_sh = pltpu.roll(kr, MLA_NOPE, 1)
    lane = _iota((tm, LANE), 1)
    half = MLA_ROPE // 2
    low = (lane >= MLA_NOPE) & (lane < MLA_NOPE + half)
    cos = cos_ref[...]
    sin = sin_ref[...]

    def rope(x):
        partner = jnp.where(low, pltpu.roll(x, LANE - half, 1), pltpu.roll(x, half, 1))
        return x * cos + partner * sin

    scale = (MLA_NOPE + MLA_ROPE) ** -0.5 * LOG2_E
    for h in range(MLA_HEADS):
        sl = slice(h * LANE, (h + 1) * LANE)
        q_ref[h] = (rope(q[:, sl]) * scale).astype(BF16)
        k_ref[h] = rope(kn[:, sl] + kr_sh).astype(BF16)
        v_ref[h] = jnp.where(lane == _mla_ones_lane(h), 1.0, vv[:, sl]).astype(BF16)


def _mla_tables(S):
    half = MLA_ROPE // 2
    inv = ROPE_THETA ** (-jnp.arange(0, MLA_ROPE, 2, dtype=F32) / MLA_ROPE)
    ang = jnp.arange(S, dtype=F32)[:, None] * inv[None, :]
    cos, sin = jnp.cos(ang), jnp.sin(ang)
    ones = jnp.ones((S, MLA_NOPE), F32)
    tail = LANE - MLA_NOPE - MLA_ROPE
    cos_t = jnp.concatenate([ones, cos, cos, jnp.ones((S, tail), F32)], axis=1)
    sin_t = jnp.concatenate([jnp.zeros((S, MLA_NOPE), F32), -sin, sin, jnp.zeros((S, tail), F32)], axis=1)
    del half
    return cos_t, sin_t


def _layout_mla_weights(w_uq, w_ukv):
    H = MLA_HEADS
    dq = MLA_NOPE + MLA_ROPE
    wq = jnp.concatenate([_pad_cols(w_uq[:, h * dq:(h + 1) * dq], LANE) for h in range(H)], axis=1)
    dkv = MLA_NOPE + MLA_V
    wk, wv = [], []
    for h in range(H):
        blk = w_ukv[:, h * dkv:(h + 1) * dkv]
        wk.append(_pad_cols(blk[:, :MLA_NOPE], LANE))
        v = blk[:, MLA_NOPE:]
        zero = jnp.zeros_like(v)
        wv.append(jnp.concatenate([v, zero] if h % 2 == 0 else [zero, v], axis=1))
    return wq.astype(BF16), jnp.concatenate(wk, axis=1).astype(BF16), jnp.concatenate(wv, axis=1).astype(BF16)


def _mla_prep(p_mla, q_norm_g, kv_norm_g, wq, wk, wv, tables, tm=512):
    B, S, _ = p_mla.shape
    H = MLA_HEADS
    cos_t, sin_t = tables
    const = lambda shape: pl.BlockSpec(shape, lambda b, i: (0,) * len(shape))
    qkv_spec = pl.BlockSpec((None, H, tm, LANE), lambda b, i: (b, 0, i, 0))
    qkv_shape = jax.ShapeDtypeStruct((B, H, S, LANE), BF16)
    return pl.pallas_call(
        _mla_prep_kernel, name="mla_prep",
        grid=(B, S // tm),
        in_specs=[pl.BlockSpec((None, tm, W_MLA), lambda b, i: (b, i, 0)),
                  const((1, MLA_Q_RANK)), const((MLA_Q_RANK, H * LANE)),
                  const((1, MLA_KV_RANK)), const((MLA_KV_RANK, H * LANE)), const((MLA_KV_RANK, H * LANE)),
                  pl.BlockSpec((tm, LANE), lambda b, i: (i, 0)),
                  pl.BlockSpec((tm, LANE), lambda b, i: (i, 0))],
        out_specs=[qkv_spec, qkv_spec, qkv_spec],
        out_shape=[qkv_shape, qkv_shape, qkv_shape],
        compiler_params=_cparams("parallel", "parallel"),
    )(p_mla, q_norm_g.reshape(1, -1), wq, kv_norm_g.reshape(1, -1), wk, wv, cos_t, sin_t)


def _mla_ones_lane(h):
    return MLA_V if h % 2 == 0 else 0


def _mla_attn_kernel(qi_ref, kj_ref, q_ref, k_ref, v_ref, o_ref, m_ref, acc_ref):
    H = MLA_HEADS
    tq, tk = q_ref.shape[1], k_ref.shape[1]
    i = qi_ref[pl.program_id(1)]
    j = kj_ref[pl.program_id(1)]

    @pl.when(j == 0)
    def _():
        m_ref[...] = jnp.full(m_ref.shape, NEG_INF, F32)
        acc_ref[...] = jnp.zeros_like(acc_ref)

    def sweep(masked):
        if masked:
            causal = _iota((tq, tk), 1) <= _iota((tq, tk), 0)
        scores = [lax.dot_general(q_ref[h], k_ref[h], (((1,), (1,)), ((), ())), preferred_element_type=F32)
                  for h in range(H)]
        for h in range(H):
            s = scores[h]
            if masked:
                s = jnp.where(causal, s, NEG_INF)
            m_prev = m_ref[h]
            m_new = jnp.maximum(m_prev, jnp.max(s, axis=-1, keepdims=True))
            p = jnp.exp2(s - jnp.tile(m_new, (1, tk // LANE)))
            acc_ref[h] = (jnp.exp2(m_prev - m_new) * acc_ref[h]
                          + jnp.dot(p.astype(BF16), v_ref[h], preferred_element_type=F32))
            m_ref[h] = m_new

    @pl.when(j < i)
    def _():
        sweep(False)

    @pl.when(j == i)
    def _():
        sweep(True)
        lane = _iota((tq, LANE), 1)
        for pair in range(H // 2):
            he, ho = 2 * pair, 2 * pair + 1
            acc_e, acc_o = acc_ref[he], acc_ref[ho]
            le = acc_e[:, _mla_ones_lane(he):_mla_ones_lane(he) + 1]
            lo = acc_o[:, _mla_ones_lane(ho):_mla_ones_lane(ho) + 1]
            o_ref[:, pair * LANE:(pair + 1) * LANE] = jnp.where(lane < MLA_V, acc_e / le, acc_o / lo)


def _mla_attn(q, k, v):
    B, H, S, _ = q.shape
    t = min(MLA_TILE, S)
    nq = S // t
    pairs = [(i, j) for i in range(nq) for j in range(i + 1)]
    qi = jnp.asarray([p[0] for p in pairs], jnp.int32)
    kj = jnp.asarray([p[1] for p in pairs], jnp.int32)
    grid_spec = pltpu.PrefetchScalarGridSpec(
        num_scalar_prefetch=2,
        grid=(B, len(pairs)),
        in_specs=[pl.BlockSpec((None, H, t, LANE), lambda b, p, qi, kj: (b, 0, qi[p], 0)),
                  pl.BlockSpec((None, H, t, LANE), lambda b, p, qi, kj: (b, 0, kj[p], 0)),
                  pl.BlockSpec((None, H, t, LANE), lambda b, p, qi, kj: (b, 0, kj[p], 0))],
        out_specs=pl.BlockSpec((None, t, GROUP_W), lambda b, p, qi, kj: (b, qi[p], 0)),
        scratch_shapes=[pltpu.VMEM((H, t, LANE), F32), pltpu.VMEM((H, t, LANE), F32)],
    )
    return pl.pallas_call(
        _mla_attn_kernel, name="mla_attn",
        grid_spec=grid_spec,
        out_shape=jax.ShapeDtypeStruct((B, S, GROUP_W), F32),
        compiler_params=_cparams("parallel", "arbitrary"),
    )(qi, kj, q, k, v)


POS_HI = NSA_DK
POS_LO = NSA_DK + 3
POS_ONE = NSA_DK + 6
POS_PAD = NSA_DK + 7
ONES_LANE = NSA_DV


def _split_bf16(x, parts=3):
    out, rem = [], np.float64(x)
    for _ in range(parts):
        piece = np.float64(np.float32(rem).astype(jnp.bfloat16).astype(np.float32))
        out.append(float(piece))
        rem = rem - piece
    return out


def _nsa_query_table():
    H = NSA_HEADS
    tab = np.zeros((2 * H, LANE), np.float32)
    for h in range(H):
        c = 2.0 ** (-8.0 * (h + 1) / H) * LOG2_E
        pieces = _split_bf16(c)
        tab[h, POS_HI:POS_HI + 3] = pieces
        tab[h, POS_LO:POS_LO + 3] = pieces
        tab[h, POS_PAD] = NEG_INF
        tab[H + h, POS_ONE] = -sum(pieces)
    return jnp.asarray(tab)


def _nsa_pos_lanes(pos, lo_offset=0.0):
    t = np.zeros((len(pos), LANE - NSA_DK), np.float32)
    t[:, POS_HI - NSA_DK:POS_HI - NSA_DK + 3] = (NSA_SLC_LEN * (pos // NSA_SLC_LEN))[:, None]
    t[:, POS_LO - NSA_DK:POS_LO - NSA_DK + 3] = (pos % NSA_SLC_LEN + lo_offset)[:, None]
    t[:, POS_ONE - NSA_DK] = 1.0
    return t


def _nsa_queries(q_ref, qtab_ref, qb):
    Q, H = q_ref.shape[0], NSA_HEADS
    qpos = (qb * Q + _iota((Q, 1), 0)).astype(F32)
    out = []
    for h in range(H):
        q = q_ref[:, h * LANE:(h + 1) * LANE] * (NSA_DK ** -0.5 * LOG2_E)
        out.append((q + qtab_ref[h:h + 1, :] + qtab_ref[H + h:H + h + 1, :] * qpos).astype(BF16))
    return out


def _normalise(o):
    return o / o[:, ONES_LANE:ONES_LANE + 1]


def _unstack_heads(o):
    Q = o.shape[0] // NSA_HEADS
    lane = _iota((Q, LANE), 1)
    out = []
    for pair in range(NSA_HEADS // 2):
        even = o[(2 * pair) * Q:(2 * pair + 1) * Q]
        odd = o[(2 * pair + 1) * Q:(2 * pair + 2) * Q]
        out.append(jnp.where(lane < NSA_DV, even, pltpu.roll(odd, NSA_DV, 1)))
    return jnp.concatenate(out, axis=1)


def _nsa_cmp_kernel(uk_ref, uv_ref, pek_ref, pev_ref, w1k_ref, w1v_ref, w2k_ref, w2v_ref, cpos_ref,
                    kc_ref, vc_ref, sh_ref):
    nb = uk_ref.shape[0]
    half = uk_ref.shape[1]

    def hidden(u_ref, pe_ref, w1_ref):
        u = u_ref[...]
        first = _mm(u + pe_ref[0:1, :], w1_ref[0:half, :])
        second = _mm(u + pe_ref[1:2, :], w1_ref[half:2 * half, :])
        sh_ref[0:nb, :] = second
        sh_ref[nb:nb + 8, :] = jnp.zeros((8, NSA_CMP_HID), F32)
        return first + sh_ref[pl.ds(1, nb), :]

    hk = _silu(hidden(uk_ref, pek_ref, w1k_ref))
    hv = _silu(hidden(uv_ref, pev_ref, w1v_ref))
    kc_ref[...] = (_mm(hk, w2k_ref[...]) + cpos_ref[...]).astype(BF16)
    ones_lane = jnp.where(_iota((1, LANE), 1) == ONES_LANE, 1.0, 0.0)
    vc_ref[...] = (_mm(hv, w2v_ref[...]) + ones_lane).astype(BF16)


def _nsa_compress(uk, uv, pe_k, w1_k, w2_k, pe_v, w1_v, w2_v):
    B, nb, half = uk.shape
    hid = NSA_CMP_HID
    const = lambda shape: pl.BlockSpec(shape, lambda b: (0,) * len(shape))
    w2k = _pad_cols(w2_k, LANE).astype(BF16)
    w2v = _pad_cols(w2_v, LANE).astype(BF16)
    centre = _nsa_pos_lanes(np.arange(nb) * NSA_CMP_STRIDE, 0.5 * (NSA_CMP_LEN - 1))
    cpos = jnp.asarray(np.concatenate([np.zeros((nb, NSA_DK), np.float32), centre], axis=1))
    out_spec = pl.BlockSpec((None, nb, LANE), lambda b: (b, 0, 0))
    out_shape = jax.ShapeDtypeStruct((B, nb, LANE), BF16)
    return pl.pallas_call(
        _nsa_cmp_kernel, name="nsa_compress",
        grid=(B,),
        in_specs=[pl.BlockSpec((None, nb, half), lambda b: (b, 0, 0)),
                  pl.BlockSpec((None, nb, half), lambda b: (b, 0, 0)),
                  const((2, half)), const((2, half)),
                  const((2 * half, hid)), const((2 * half, hid)),
                  const((hid, LANE)), const((hid, LANE)), const((nb, LANE))],
        out_specs=[out_spec, out_spec],
        out_shape=[out_shape, out_shape],
        scratch_shapes=[pltpu.VMEM((nb + 8, hid), F32)],
        compiler_params=_cparams("parallel"),
    )(uk, uv, pe_k.reshape(2, half), pe_v.reshape(2, half), w1_k.astype(BF16), w1_v.astype(BF16), w2k, w2v, cpos)


def _nsa_sel_kernel(q_ref, qtab_ref, kc_ref, vc_ref, ovt_ref, oc_ref, selb_ref, any_ref, *, n_slc, top_n):
    Q, H = Q_BLOCK, NSA_HEADS
    qb = pl.program_id(1)
    nc = kc_ref.shape[0]
    qs = jnp.concatenate(_nsa_queries(q_ref, qtab_ref, qb), axis=0)
    s = lax.dot_general(qs, kc_ref[...], (((1,), (1,)), ((), ())), preferred_element_type=F32)
    qpos = qb * Q + (_iota((H * Q, nc), 0) & (Q - 1))
    block_end = _iota((H * Q, nc), 1) * NSA_CMP_STRIDE + (NSA_CMP_LEN - 1)
    s = jnp.where(block_end <= qpos, s, NEG_INF)
    e = jnp.exp2(s - jnp.max(s, axis=-1, keepdims=True))
    qpos_col = qb * Q + (_iota((H * Q, 1), 0) & (Q - 1))
    has_block = jnp.where(qpos_col >= NSA_CMP_LEN - 1, 1.0, 0.0)
    p = e * (has_block / jnp.sum(e, axis=-1, keepdims=True))
    oc_ref[...] = _unstack_heads(jnp.dot(p.astype(BF16), vc_ref[...], preferred_element_type=F32))

    p_sum = p[0:Q]
    for h in range(1, H):
        p_sum = p_sum + p[h * Q:(h + 1) * Q]
    imp = lax.dot_general(ovt_ref[...], p_sum, (((1,), (1,)), ((), ())), precision=HIGHEST,
                          preferred_element_type=F32)
    blk = _iota((LANE, Q), 0)
    q_blk = (qb * Q + _iota((LANE, Q), 1)) >> int(math.log2(NSA_SLC_LEN))
    causal = blk <= q_blk
    for forced_blk in (0, q_blk, q_blk - 1):
        imp = jnp.where(blk == forced_blk, FORCED_SCORE, imp)
    imp = jnp.where(causal, imp, -1.0)
    imp = jnp.where(blk < n_slc, imp, -2.0)
    blk_f = blk.astype(F32)
    sel = jnp.zeros((LANE, Q), F32)
    for _ in range(top_n):
        m = jnp.max(imp, axis=0, keepdims=True)
        first = jnp.min(jnp.where(imp == m, blk_f, float(LANE)), axis=0, keepdims=True)
        hit = blk_f == first
        sel = jnp.where(hit, 1.0, sel)
        imp = jnp.where(hit, -3.0, imp)
    sel = jnp.where(causal, sel, 0.0).T
    selb_ref[...] = jnp.where(sel > 0.5, 0.0, NEG_INF).astype(BF16)
    any_ref[...] = jnp.max(sel, axis=0, keepdims=True)


def _nsa_select(p_nsa, qtab, kc, vc, overlap_t):
    B, S, _ = p_nsa.shape
    Q = Q_BLOCK
    nqb = S // Q
    nc = kc.shape[1]
    n_slc = S // NSA_SLC_LEN
    kern = functools.partial(_nsa_sel_kernel, n_slc=n_slc, top_n=min(NSA_TOPN, n_slc))
    return pl.pallas_call(
        kern, name="nsa_select",
        grid=(B, nqb),
        in_specs=[pl.BlockSpec((None, Q, NSA_HEADS * LANE), lambda b, i: (b, i, 0)),
                  pl.BlockSpec((2 * NSA_HEADS, LANE), lambda b, i: (0, 0)),
                  pl.BlockSpec((None, nc, LANE), lambda b, i: (b, 0, 0)),
                  pl.BlockSpec((None, nc, LANE), lambda b, i: (b, 0, 0)),
                  pl.BlockSpec((LANE, nc), lambda b, i: (0, 0))],
        out_specs=[pl.BlockSpec((None, Q, GROUP_W), lambda b, i: (b, i, 0)),
                   pl.BlockSpec((None, Q, LANE), lambda b, i: (b, i, 0)),
                   pl.BlockSpec((None, None, 1, LANE), lambda b, i: (b, i, 0, 0))],
        out_shape=[jax.ShapeDtypeStruct((B, S, GROUP_W), F32),
                   jax.ShapeDtypeStruct((B, S, LANE), BF16),
                   jax.ShapeDtypeStruct((B, nqb, 1, LANE), F32)],
        compiler_params=_cparams("parallel", "parallel"),
    )(p_nsa, qtab, kc, vc, overlap_t)


def _nsa_attn_kernel(flags_ref, q_ref, qtab_ref, gate_ref, oc_ref, selb_ref, ks_ref, vs_ref, kw_ref, vw_ref,
                     o_ref, m_ref, acc_ref, *, nt):
    Q, H, TK = Q_BLOCK, NSA_HEADS, NSA_TILE
    PART = 2 * Q
    b = pl.program_id(0)
    qb = pl.program_id(1)
    nqb = pl.num_programs(1)
    qh = _nsa_queries(q_ref, qtab_ref, qb)
    nt_dims = (((1,), (1,)), ((), ()))

    selb = selb_ref[...]
    qs_sel = jnp.concatenate([jnp.concatenate([q, selb], axis=1) for q in qh], axis=0)
    m_ref[...] = jnp.full(m_ref.shape, NEG_INF, F32)
    acc_ref[...] = jnp.zeros_like(acc_ref)

    def update(t, diagonal):
        rows = pl.ds(pl.multiple_of(t * TK, TK), TK)
        k_tile, v_tile = ks_ref[rows, :], vs_ref[rows, :]
        parts = [slice(i * PART, (i + 1) * PART) for i in range(H * Q // PART)]
        scores = [lax.dot_general(qs_sel[r], k_tile, nt_dims, preferred_element_type=F32) for r in parts]
        for r, s in zip(parts, scores):
            if diagonal:
                ahead = _iota((PART, TK), 1) - (_iota((PART, TK), 0) & (Q - 1))
                s = jnp.where(ahead <= qb * Q - t * TK, s, NEG_INF)
            m_prev = m_ref[r]
            m_new = jnp.maximum(m_prev, jnp.max(s, axis=-1, keepdims=True))
            p = jnp.exp2(s - jnp.tile(m_new, (1, TK // LANE)))
            acc_ref[r] = (jnp.exp2(m_prev - m_new) * acc_ref[r]
                          + jnp.dot(p.astype(BF16), v_tile, preferred_element_type=F32))
            m_ref[r] = m_new

    def tile(t, carry):
        @pl.when(flags_ref[(b * nqb + qb) * nt + t] > 0)
        def _():
            update(t, False)
        return carry

    t_diag = (qb * Q) // TK
    lax.fori_loop(0, t_diag, tile, 0)
    update(t_diag, True)
    o_s = _unstack_heads(_normalise(acc_ref[...]))

    W = Q + NSA_WIN
    rows_w = pl.ds(pl.multiple_of(qb * Q, Q), W)
    qs_win = jnp.concatenate(qh, axis=0)
    s = lax.dot_general(qs_win, kw_ref[rows_w, :], nt_dims, preferred_element_type=F32)
    dist_w = (_iota((H * Q, W), 0) & (Q - 1)) - _iota((H * Q, W), 1) + NSA_WIN
    s = jnp.where(dist_w >= 0, jnp.where(dist_w < NSA_WIN, s, NEG_INF), NEG_INF)
    p = jnp.exp2(s - jnp.max(s, axis=-1, keepdims=True))
    o_w = _unstack_heads(_normalise(jnp.dot(p.astype(BF16), vw_ref[rows_w, :], preferred_element_type=F32)))

    gates = jax.nn.sigmoid(gate_ref[...])
    lane_head = _iota((LANE, GROUP_W), 1) // NSA_DV
    src = _iota((LANE, GROUP_W), 0)
    out = None
    for j, branch in enumerate((oc_ref[...], o_s, o_w)):
        g = _mm_f32(gates, jnp.where(src == lane_head * 3 + j, 1.0, 0.0))
        out = g * branch if out is None else out + g * branch
    o_ref[...] = out


def _nsa_attend(p_nsa, qtab, o_c, selb, flags, ks, vs, kw, vw):
    B, S, _ = p_nsa.shape
    Q = Q_BLOCK
    nqb = S // Q
    nt = S // NSA_TILE
    gate_blk = (W_NSA - LANE) // LANE
    kern = functools.partial(_nsa_attn_kernel, nt=nt)
    whole = lambda rows, width: pl.BlockSpec((None, rows, width), lambda b, i, f: (b, 0, 0))
    grid_spec = pltpu.PrefetchScalarGridSpec(
        num_scalar_prefetch=1,
        grid=(B, nqb),
        in_specs=[pl.BlockSpec((None, Q, NSA_HEADS * LANE), lambda b, i, f: (b, i, 0)),
                  pl.BlockSpec((2 * NSA_HEADS, LANE), lambda b, i, f: (0, 0)),
                  pl.BlockSpec((None, Q, LANE), lambda b, i, f: (b, i, gate_blk)),
                  pl.BlockSpec((None, Q, GROUP_W), lambda b, i, f: (b, i, 0)),
                  pl.BlockSpec((None, Q, LANE), lambda b, i, f: (b, i, 0)),
                  whole(S, 2 * LANE), whole(S, LANE), whole(S + NSA_WIN, LANE), whole(S + NSA_WIN, LANE)],
        out_specs=pl.BlockSpec((None, Q, GROUP_W), lambda b, i, f: (b, i, 0)),
        scratch_shapes=[pltpu.VMEM((NSA_HEADS * Q, LANE), F32), pltpu.VMEM((NSA_HEADS * Q, LANE), F32)],
    )
    return pl.pallas_call(
        kern, name="nsa_attend",
        grid_spec=grid_spec,
        out_shape=jax.ShapeDtypeStruct((B, S, GROUP_W), F32),
        compiler_params=_cparams("parallel", "parallel"),
    )(flags, p_nsa, qtab, p_nsa, o_c, selb, ks, vs, kw, vw)


def _nsa_tables(S):
    nc = S // NSA_CMP_STRIDE
    n = np.arange(nc)[None, :]
    j = np.arange(LANE)[:, None]
    start = n * NSA_CMP_STRIDE
    ov = (start < (j + 1) * NSA_SLC_LEN) & (start + NSA_CMP_LEN - 1 >= j * NSA_SLC_LEN)
    ov &= (n < (S - NSA_CMP_LEN) // NSA_CMP_STRIDE + 1) & (j < S // NSA_SLC_LEN)
    pos = np.arange(S)
    sel_lanes = np.concatenate([_nsa_pos_lanes(pos),
                                (pos[:, None] // NSA_SLC_LEN == np.arange(LANE)[None, :]).astype(np.float32)], axis=1)
    win_lanes = np.concatenate([np.zeros((NSA_WIN, LANE - NSA_DK), np.float32), _nsa_pos_lanes(pos)], axis=0)
    win_lanes[:NSA_WIN, POS_ONE - NSA_DK] = 1.0
    win_lanes[:NSA_WIN, POS_PAD - NSA_DK] = 1.0
    ones = np.zeros((S, LANE - NSA_DV), np.float32)
    ones[:, ONES_LANE - NSA_DV] = 1.0
    as_bf16 = lambda a: jnp.asarray(a, dtype=BF16)
    return (_nsa_query_table(), jnp.asarray(ov.astype(np.float32)), as_bf16(sel_lanes), as_bf16(win_lanes),
            as_bf16(ones))


def _nsa(p_nsa, pe_k, w1_k, w2_k, pe_v, w1_v, w2_v, tables):
    B, S, _ = p_nsa.shape
    qtab, overlap_t, sel_lanes, win_lanes, ones = tables
    q_w = NSA_HEADS * LANE
    half = NSA_CMP_STRIDE * NSA_DK
    piece = lambda idx: p_nsa[:, :, q_w + idx * NSA_DK:q_w + (idx + 1) * NSA_DK]
    uk = piece(0).reshape(B, S // NSA_CMP_STRIDE, half)
    uv = piece(1).reshape(B, S // NSA_CMP_STRIDE, half)
    kc, vc = _nsa_compress(uk, uv, pe_k, w1_k, w2_k, pe_v, w1_v, w2_v)
    o_c, selb, blk_any = _nsa_select(p_nsa, qtab, kc, vc, overlap_t)
    per_tile = NSA_TILE // NSA_SLC_LEN
    nt = S // NSA_TILE
    flags = blk_any[:, :, 0, :nt * per_tile].reshape(B, S // Q_BLOCK, nt, per_tile).max(axis=-1)
    flags = (flags > 0).astype(jnp.int32).reshape(-1)
    bcast = lambda t: jnp.broadcast_to(t, (B,) + t.shape)
    ks = jnp.concatenate([piece(2).astype(BF16), bcast(sel_lanes)], axis=-1)
    vs = jnp.concatenate([piece(3).astype(BF16), bcast(ones)], axis=-1)
    pad = ((0, 0), (NSA_WIN, 0), (0, 0))
    kw = jnp.concatenate([jnp.pad(piece(4).astype(BF16), pad), bcast(win_lanes)], axis=-1)
    vw = jnp.pad(jnp.concatenate([piece(5).astype(BF16), bcast(ones)], axis=-1), pad)
    return _nsa_attend(p_nsa, qtab, o_c, selb, flags, ks, vs, kw, vw)


def _out_proj_kernel(h_ref, ya_ref, yb_ref, yc_ref, yd_ref, w_ref, g_ref, b_ref, o_ref):
    mix = None
    for idx, y_ref in enumerate((ya_ref, yb_ref, yc_ref, yd_ref)):
        part = _mm(y_ref[...], w_ref[idx * GROUP_W:(idx + 1) * GROUP_W, :])
        mix = part if mix is None else mix + part
    o_ref[...] = _layer_norm(DEEPNORM_ALPHA * h_ref[...] + mix, g_ref[...], b_ref[...])


def _out_proj(h2, ys, w_out, g, b, tm=512):
    T, D = h2.shape
    row = lambda w: pl.BlockSpec((tm, w), lambda i: (i, 0))
    const = lambda shape: pl.BlockSpec(shape, lambda i: (0,) * len(shape))
    return pl.pallas_call(
        _out_proj_kernel, name="out_proj_ln",
        grid=(T // tm,),
        in_specs=[row(D), row(GROUP_W), row(GROUP_W), row(GROUP_W), row(GROUP_W),
                  const((D, D)), const((1, D)), const((1, D))],
        out_specs=row(D),
        out_shape=jax.ShapeDtypeStruct((T, D), F32),
        compiler_params=_cparams("parallel"),
    )(h2, *ys, w_out.astype(BF16), g.reshape(1, D), b.reshape(1, D))


def _mlp_kernel(h_ref, w1_ref, w2_ref, g_ref, b_ref, o_ref, acc_ref):
    f = pl.program_id(1)

    @pl.when(f == 0)
    def _():
        acc_ref[...] = jnp.zeros_like(acc_ref)

    a = jnp.maximum(_mm(h_ref[...], w1_ref[...]), 0.0)
    acc_ref[...] += _mm(a * a, w2_ref[...])

    @pl.when(f == pl.num_programs(1) - 1)
    def _():
        o_ref[...] = _layer_norm(DEEPNORM_ALPHA * h_ref[...] + acc_ref[...], g_ref[...], b_ref[...])


def _mlp(h2, w1, w2, g, b, tm=1024, tf=1024):
    T, D = h2.shape
    F = w1.shape[1]
    return pl.pallas_call(
        _mlp_kernel, name="mlp_ln",
        grid=(T // tm, F // tf),
        in_specs=[pl.BlockSpec((tm, D), lambda i, f: (i, 0)),
                  pl.BlockSpec((D, tf), lambda i, f: (0, f)),
                  pl.BlockSpec((tf, D), lambda i, f: (f, 0)),
                  pl.BlockSpec((1, D), lambda i, f: (0, 0)),
                  pl.BlockSpec((1, D), lambda i, f: (0, 0))],
        out_specs=pl.BlockSpec((tm, D), lambda i, f: (i, 0)),
        out_shape=jax.ShapeDtypeStruct((T, D), F32),
        scratch_shapes=[pltpu.VMEM((tm, D), F32)],
        compiler_params=_cparams("parallel", "arbitrary"),
    )(h2, w1.astype(BF16), w2.astype(BF16), g.reshape(1, D), b.reshape(1, D))


def kernel(x, ln_emb_g, ln_emb_b, w_in, conv_w, conv_b, dt_bias, a_log, d_skip, ssm_norm_g, q_norm_g, w_uq, kv_norm_g, w_ukv, cmp_pe_k, cmp_w1_k, cmp_w2_k, cmp_pe_v, cmp_w1_v, cmp_w2_v, w_out, ln1_g, ln1_b, w_mlp1, w_mlp2, ln2_g, ln2_b):
    B, S, D = x.shape
    assert D == D_MODEL and S % NSA_TILE == 0 and S // NSA_SLC_LEN <= LANE
    T = B * S
    ret_tables = _ret_tables(S)
    mla_tables = _mla_tables(S)
    nsa_tables = _nsa_tables(S)
    h = _entry_ln(x.reshape(T, D), ln_emb_g, ln_emb_b)
    for l in range(w_in.shape[0]):
        p_ssm, p_mla, p_ret, p_nsa = _in_proj(h, _layout_w_in(w_in[l]))
        y_a = _ssm(p_ssm.reshape(B, S, W_SSM), conv_w[l], conv_b[l], dt_bias[l], a_log[l], d_skip[l], ssm_norm_g[l])
        wq, wk, wv = _layout_mla_weights(w_uq[l], w_ukv[l])
        q, k, v = _mla_prep(p_mla.reshape(B, S, W_MLA), q_norm_g[l], kv_norm_g[l], wq, wk, wv, mla_tables)
        y_b = _mla_attn(q, k, v)
        y_c = _ret(p_ret.reshape(B, S, W_RET), ret_tables)
        y_d = _nsa(p_nsa.reshape(B, S, W_NSA), cmp_pe_k[l], cmp_w1_k[l], cmp_w2_k[l],
                   cmp_pe_v[l], cmp_w1_v[l], cmp_w2_v[l], nsa_tables)
        ys = [y.reshape(T, GROUP_W) for y in (y_a, y_b, y_c, y_d)]
        h = _out_proj(h, ys, w_out[l], ln1_g[l], ln1_b[l])
        h = _mlp(h, w_mlp1[l], w_mlp2[l], ln2_g[l], ln2_b[l])
    return h.reshape(B, S, D)
```

```python
import functools
import math

import jax
import jax.numpy as jnp
import numpy as np
from jax import lax
from jax.experimental import pallas as pl
from jax.experimental.pallas import tpu as pltpu

F32 = jnp.float32
BF16 = jnp.bfloat16
HIGHEST = lax.Precision.HIGHEST

D_MODEL = 1024
DEPTH = 2
GROUP_W = D_MODEL // 4
SSM_HEADS = 4
SSM_HEAD_DIM = GROUP_W // SSM_HEADS
SSM_GROUPS = 2
SSM_STATE = 128
SSM_CONV = 4
SSM_CHUNK = 128
SSM_XBC = GROUP_W + 2 * SSM_GROUPS * SSM_STATE
MLA_HEADS = 4
MLA_NOPE = 64
MLA_ROPE = 32
MLA_V = GROUP_W // MLA_HEADS
MLA_Q_RANK = 256
MLA_KV_RANK = 128
RET_HEADS = 4
RET_DK = 64
RET_DV = GROUP_W // RET_HEADS
RET_CHUNK = 128
NSA_HEADS = 4
NSA_DK = 64
NSA_DV = GROUP_W // NSA_HEADS
NSA_CMP_LEN = 32
NSA_CMP_STRIDE = 16
NSA_CMP_HID = 256
NSA_SLC_LEN = 64
NSA_TOPN = 16
NSA_WIN = 512
D_FF = 4 * D_MODEL
NSA_Q = 256
ROPE_THETA = 10000.0
EPS = 1e-5
NEG_INF = -1e30
LOG2_E = math.log2(math.e)
FORCED_SCORE = 1e9
DEEPNORM_ALPHA = (2.0 * DEPTH) ** 0.25

IN_SPLITS = (
    GROUP_W, SSM_XBC, SSM_HEADS,
    MLA_Q_RANK, MLA_KV_RANK, MLA_ROPE,
    RET_HEADS * RET_DK, RET_HEADS * RET_DK, RET_HEADS * RET_DV, GROUP_W,
    NSA_HEADS * NSA_DK, NSA_DK, NSA_DV, NSA_DK, NSA_DV, NSA_DK, NSA_DV, 3 * NSA_HEADS,
)

LANE = 128
W_SSM = GROUP_W + SSM_XBC + LANE
W_MLA = MLA_Q_RANK + MLA_KV_RANK + LANE
W_RET = 4 * GROUP_W
W_NSA = NSA_HEADS * LANE + 3 * LANE + LANE
W_PROJ = W_SSM + W_MLA + W_RET + W_NSA

NSA_TILE = 512
MLA_TILE = 512
MLA_Q_TILE = 1024
VMEM_LIMIT = 48 * 1024 * 1024


def _cparams(*sem):
    return pltpu.CompilerParams(dimension_semantics=sem, vmem_limit_bytes=VMEM_LIMIT)


def _mm(a, b):
    return jnp.dot(a.astype(BF16), b.astype(BF16), preferred_element_type=F32)


def _mm_nt(a, b):
    return lax.dot_general(a.astype(BF16), b.astype(BF16), (((1,), (1,)), ((), ())),
                           preferred_element_type=F32)


def _mm_f32(a, b):
    return jnp.dot(a, b, precision=HIGHEST, preferred_element_type=F32)


def _silu(x):
    return x * jax.nn.sigmoid(x)


def _softplus(x):
    return jnp.maximum(x, 0.0) + jnp.log1p(jnp.exp(-jnp.abs(x)))


def _layer_norm(x, g, b):
    mu = jnp.mean(x, axis=-1, keepdims=True)
    xc = x - mu
    var = jnp.mean(xc * xc, axis=-1, keepdims=True)
    return xc * lax.rsqrt(var + EPS) * g + b


def _iota(shape, dim):
    return lax.broadcasted_iota(jnp.int32, shape, dim)


def _ln_kernel(x_ref, g_ref, b_ref, o_ref):
    o_ref[...] = _layer_norm(x_ref[...], g_ref[...], b_ref[...])


def _entry_ln(x2, g, b, tm=512):
    T, D = x2.shape
    return pl.pallas_call(
        _ln_kernel, name="entry_ln",
        grid=(T // tm,),
        in_specs=[pl.BlockSpec((tm, D), lambda i: (i, 0)),
                  pl.BlockSpec((1, D), lambda i: (0, 0)),
                  pl.BlockSpec((1, D), lambda i: (0, 0))],
        out_specs=pl.BlockSpec((tm, D), lambda i: (i, 0)),
        out_shape=jax.ShapeDtypeStruct((T, D), F32),
        compiler_params=_cparams("parallel"),
    )(x2, g.reshape(1, D), b.reshape(1, D))


def _pad_cols(w, width):
    return jnp.pad(w, ((0, 0), (0, width - w.shape[1])))


def _layout_w_in(w):
    offs = np.concatenate([[0], np.cumsum(IN_SPLITS)])
    p = [w[:, int(offs[i]):int(offs[i + 1])] for i in range(len(IN_SPLITS))]
    (ssm_z, ssm_xbc, ssm_dt, mla_cq, mla_ckv, mla_kr, ret_q, ret_k, ret_v, ret_g,
     nsa_q, nsa_kc, nsa_vc, nsa_ks, nsa_vs, nsa_kw, nsa_vw, nsa_gate) = p
    nsa_q_heads = [_pad_cols(nsa_q[:, h * NSA_DK:(h + 1) * NSA_DK], LANE) for h in range(NSA_HEADS)]
    cols = [ssm_z, ssm_xbc, _pad_cols(ssm_dt, LANE),
            mla_cq, mla_ckv, _pad_cols(mla_kr, LANE),
            ret_q, ret_k, ret_v, ret_g,
            *nsa_q_heads, nsa_kc, nsa_vc, nsa_ks, nsa_vs, nsa_kw, nsa_vw, _pad_cols(nsa_gate, LANE)]
    out = jnp.concatenate(cols, axis=1)
    assert out.shape[1] == W_PROJ
    return out.astype(BF16)


def _in_proj_kernel(h_ref, w_ref, ssm_ref, mla_ref, ret_ref, nsa_ref):
    hb = h_ref[...].astype(BF16)
    off = 0
    for ref, width in ((ssm_ref, W_SSM), (mla_ref, W_MLA), (ret_ref, W_RET), (nsa_ref, W_NSA)):
        ref[...] = jnp.dot(hb, w_ref[:, off:off + width], preferred_element_type=F32)
        off += width


def _in_proj(h2, w_p, tm=256):
    T, D = h2.shape
    widths = (W_SSM, W_MLA, W_RET, W_NSA)
    return pl.pallas_call(
        _in_proj_kernel, name="in_proj",
        grid=(T // tm,),
        in_specs=[pl.BlockSpec((tm, D), lambda i: (i, 0)),
                  pl.BlockSpec((D, W_PROJ), lambda i: (0, 0))],
        out_specs=[pl.BlockSpec((tm, w), lambda i: (i, 0)) for w in widths],
        out_shape=[jax.ShapeDtypeStruct((T, w), F32) for w in widths],
        compiler_params=_cparams("parallel"),
    )(h2, w_p)


def _ssm_kernel(p_ref, cw_ref, cb_ref, dtb_ref, alog_ref, dskip_ref, ng_ref, o_ref, state_ref, ext_ref):
    @pl.when(pl.program_id(0) == 0)
    def _():
        state_ref[...] = jnp.zeros_like(state_ref)
        ext_ref[:, 0:8, :] = jnp.zeros((ext_ref.shape[0], 8, SSM_XBC), F32)

    for b in range(p_ref.shape[0]):
        _ssm_chunk(p_ref.at[b], cw_ref, cb_ref, dtb_ref, alog_ref, dskip_ref, ng_ref,
                   o_ref.at[b], state_ref.at[b], ext_ref.at[b])


def _ssm_chunk(p_ref, cw_ref, cb_ref, dtb_ref, alog_ref, dskip_ref, ng_ref, o_ref, state_ref, ext_ref):
    L, H, P, N = SSM_CHUNK, SSM_HEADS, SSM_HEAD_DIM, SSM_STATE
    z = p_ref[:, 0:GROUP_W]
    ext_ref[8:8 + L, :] = p_ref[:, GROUP_W:GROUP_W + SSM_XBC]
    conv = cb_ref[...]
    for j in range(SSM_CONV):
        conv = conv + ext_ref[pl.ds(8 - (SSM_CONV - 1) + j, L), :] * cw_ref[j:j + 1, :]
    ext_ref[0:8, :] = ext_ref[L:L + 8, :]
    xbc = _silu(conv)
    xs = xbc[:, 0:GROUP_W]
    b_in = xbc[:, GROUP_W:GROUP_W + SSM_GROUPS * N]
    c_in = xbc[:, GROUP_W + SSM_GROUPS * N:]

    dt = _softplus(p_ref[:, GROUP_W + SSM_XBC:] + dtb_ref[...])
    a = dt * (-jnp.exp(alog_ref[...]))
    row = _iota((L, L), 0)
    col = _iota((L, L), 1)
    tril = col <= row
    cs = _mm_f32(jnp.where(tril, 1.0, 0.0), a)
    cs_t = cs.T
    ecs = jnp.exp(cs)
    dte = jnp.exp(cs[L - 1:L, :] - cs)
    expand = jnp.where(_iota((LANE, H * P), 0) == _iota((LANE, H * P), 1) // P, 1.0, 0.0)
    dt_x = _mm_f32(dt, expand)
    ecs_x = _mm_f32(ecs, expand)
    dte_x = _mm_f32(dte, expand)

    xdt = xs * dt_x
    wx = xdt * dte_x
    head_of_lane = _iota((L, H * P), 1) // P
    y = xs * dskip_ref[...]
    y_off = []
    rep = H // SSM_GROUPS
    for g in range(SSM_GROUPS):
        cg = c_in[:, g * N:(g + 1) * N]
        bg = b_in[:, g * N:(g + 1) * N]
        cb = _mm_nt(cg, bg)
        for h in range(g * rep, (g + 1) * rep):
            diff = cs[:, h:h + 1] - cs_t[h:h + 1, :]
            seg = jnp.where(tril, jnp.exp(jnp.where(tril, diff, 0.0)), 0.0)
            yh = _mm(cb * seg, xdt)
            y = y + jnp.where(head_of_lane == h, yh, 0.0)
        lanes = slice(g * rep * P, (g + 1) * rep * P)
        st_prev = state_ref[:, lanes]
        y_off.append(_mm(cg, st_prev))
        state_ref[:, lanes] = st_prev * ecs_x[L - 1:L, lanes] + _mm(bg.T, wx[:, lanes])
    y = y + jnp.concatenate(y_off, axis=1) * ecs_x
    y = y * _silu(z)
    ms = jnp.mean(y * y, axis=-1, keepdims=True)
    o_ref[...] = y * lax.rsqrt(ms + EPS) * ng_ref[...]


def _ssm(p_ssm, conv_w, conv_b, dt_bias, a_log, d_skip, norm_g):
    B, S, _ = p_ssm.shape
    L = SSM_CHUNK
    pad_h = lambda v: jnp.pad(v, (0, LANE - SSM_HEADS)).reshape(1, LANE)
    const = lambda shape: pl.BlockSpec(shape, lambda c: (0,) * len(shape))
    return pl.pallas_call(
        _ssm_kernel, name="ssm",
        grid=(S // L,),
        in_specs=[pl.BlockSpec((B, L, W_SSM), lambda c: (0, c, 0)),
                  const((SSM_CONV, SSM_XBC)), const((1, SSM_XBC)), const((1, LANE)), const((1, LANE)),
                  const((1, GROUP_W)), const((1, GROUP_W))],
        out_specs=pl.BlockSpec((B, L, GROUP_W), lambda c: (0, c, 0)),
        out_shape=jax.ShapeDtypeStruct((B, S, GROUP_W), F32),
        scratch_shapes=[pltpu.VMEM((B, SSM_STATE, GROUP_W), F32),
                        pltpu.VMEM((B, L + 8, SSM_XBC), F32)],
        compiler_params=_cparams("arbitrary"),
    )(p_ssm, conv_w, conv_b.reshape(1, -1), pad_h(dt_bias), pad_h(a_log),
      jnp.repeat(d_skip, SSM_HEAD_DIM).reshape(1, GROUP_W), norm_g.reshape(1, GROUP_W))


def _ret_kernel(p_ref, cos_ref, sin_ref, dec_ref, zeta_ref, xi_ref, cd_ref, o_ref, state_ref):
    @pl.when(pl.program_id(0) == 0)
    def _():
        state_ref[...] = jnp.zeros_like(state_ref)

    for b in range(p_ref.shape[0]):
        _ret_chunk(p_ref.at[b], cos_ref, sin_ref, dec_ref, zeta_ref, xi_ref, cd_ref, o_ref.at[b], state_ref.at[b])


def _ret_chunk(p_ref, cos_ref, sin_ref, dec_ref, zeta_ref, xi_ref, cd_ref, o_ref, state_ref):
    L, H, DK, DV = RET_CHUNK, RET_HEADS, RET_DK, RET_DV
    W = H * DK
    q = p_ref[:, 0:W]
    k = p_ref[:, W:2 * W]
    v = p_ref[:, 2 * W:3 * W]
    gate = p_ref[:, 3 * W:4 * W]
    lane = _iota((L, W), 1)
    first_half = (lane % DK) < (DK // 2)
    head_of_lane = lane // DK

    def rope(x):
        partner = jnp.where(first_half, pltpu.roll(x, W - DK // 2, 1), pltpu.roll(x, DK // 2, 1))
        return x * cos_ref[...] + partner * sin_ref[...]

    qr = rope(q)
    kr = rope(k) * (DK ** -0.5)
    y = jnp.zeros((L, H * DV), F32)
    for h in range(H):
        qh = jnp.where(head_of_lane == h, qr, 0.0)
        sc = _mm_nt(qh, kr) * dec_ref[h]
        y = y + jnp.where(head_of_lane == h, _mm(sc, v), 0.0)
    st = state_ref[...]
    y = y + _mm(qr * xi_ref[...], st)
    same_head = (_iota((W, H * DV), 0) // DK) == (_iota((W, H * DV), 1) // DV)
    kv = _mm((kr * zeta_ref[...]).T, v)
    state_ref[...] = st * cd_ref[...] + jnp.where(same_head, kv, 0.0)
    ms = _mm_f32(y * y, jnp.where(same_head, 1.0 / DV, 0.0))
    o_ref[...] = y * lax.rsqrt(ms + EPS) * _silu(gate)


def _ret_tables(S):
    H, DK, L = RET_HEADS, RET_DK, RET_CHUNK
    inv = ROPE_THETA ** (-jnp.arange(0, DK, 2, dtype=F32) / DK)
    ang = jnp.arange(S, dtype=F32)[:, None] * inv[None, :]
    cos, sin = jnp.cos(ang), jnp.sin(ang)
    cos_t = jnp.tile(jnp.concatenate([cos, cos], axis=1), (1, H))
    sin_t = jnp.tile(jnp.concatenate([-sin, sin], axis=1), (1, H))
    log_gamma = jnp.log1p(-jnp.exp2(-5.0 - jnp.arange(H, dtype=F32)))
    pos = jnp.arange(L, dtype=F32)
    diff = pos[:, None] - pos[None, :]
    decay_in = jnp.where(diff >= 0, jnp.exp(jnp.maximum(diff, 0.0)[None] * log_gamma[:, None, None]), 0.0)
    zeta = jnp.exp((L - 1 - pos)[None] * log_gamma[:, None])
    xi = jnp.exp((pos + 1.0)[None] * log_gamma[:, None])
    chunk_decay = jnp.exp(L * log_gamma)
    zeta_x = jnp.repeat(zeta.T, DK, axis=1)
    xi_x = jnp.repeat(xi.T, DK, axis=1)
    cd_x = jnp.repeat(chunk_decay, RET_DV).reshape(1, H * RET_DV)
    return cos_t, sin_t, decay_in, zeta_x, xi_x, cd_x


def _ret(p_ret, tables):
    B, S, _ = p_ret.shape
    L, H = RET_CHUNK, RET_HEADS
    W = H * RET_DK
    cos_t, sin_t, decay_in, zeta_x, xi_x, cd_x = tables
    const = lambda shape: pl.BlockSpec(shape, lambda c: (0,) * len(shape))
    return pl.pallas_call(
        _ret_kernel, name="retention",
        grid=(S // L,),
        in_specs=[pl.BlockSpec((B, L, W_RET), lambda c: (0, c, 0)),
                  pl.BlockSpec((L, W), lambda c: (c, 0)),
                  pl.BlockSpec((L, W), lambda c: (c, 0)),
                  const((H, L, L)), const((L, W)), const((L, W)), const((1, H * RET_DV))],
        out_specs=pl.BlockSpec((B, L, GROUP_W), lambda c: (0, c, 0)),
        out_shape=jax.ShapeDtypeStruct((B, S, GROUP_W), F32),
        scratch_shapes=[pltpu.VMEM((B, W, H * RET_DV), F32)],
        compiler_params=_cparams("arbitrary"),
    )(p_ret, cos_t, sin_t, decay_in, zeta_x, xi_x, cd_x)


def _mla_prep_kernel(p_ref, qg_ref, wq_ref, kvg_ref, wk_ref, wv_ref, cos_ref, sin_ref,
                     q_ref, k_ref, v_ref):
    tm = p_ref.shape[0]
    cq = p_ref[:, 0:MLA_Q_RANK]
    ckv = p_ref[:, MLA_Q_RANK:MLA_Q_RANK + MLA_KV_RANK]
    kr = p_ref[:, MLA_Q_RANK + MLA_KV_RANK:]

    def rms(x, g):
        return x * lax.rsqrt(jnp.mean(x * x, axis=-1, keepdims=True) + EPS) * g

    q = _mm(rms(cq, qg_ref[...]), wq_ref[...])
    kvl = rms(ckv, kvg_ref[...])
    kn = _mm(kvl, wk_ref[...])
    vv = _mm(kvl, wv_ref[...])
    kr_sh = pltpu.roll(kr, MLA_NOPE, 1)
    lane = _iota((tm, LANE), 1)
    half = MLA_ROPE // 2
    low = (lane >= MLA_NOPE) & (lane < MLA_NOPE + half)
    cos = cos_ref[...]
    sin = sin_ref[...]

    def rope(x):
        partner = jnp.where(low, pltpu.roll(x, LANE - half, 1), pltpu.roll(x, half, 1))
        return x * cos + partner * sin

    scale = (MLA_NOPE + MLA_ROPE) ** -0.5 * LOG2_E
    for h in range(MLA_HEADS):
        sl = slice(h * LANE, (h + 1) * LANE)
        q_ref[h] = (rope(q[:, sl]) * scale).astype(BF16)
        k_ref[h] = rope(kn[:, sl] + kr_sh).astype(BF16)
        v_ref[h] = jnp.where(lane == _mla_ones_lane(h), 1.0, vv[:, sl]).astype(BF16)


def _mla_tables(S):
    half = MLA_ROPE // 2
    inv = ROPE_THETA ** (-jnp.arange(0, MLA_ROPE, 2, dtype=F32) / MLA_ROPE)
    ang = jnp.arange(S, dtype=F32)[:, None] * inv[None, :]
    cos, sin = jnp.cos(ang), jnp.sin(ang)
    ones = jnp.ones((S, MLA_NOPE), F32)
    tail = LANE - MLA_NOPE - MLA_ROPE
    cos_t = jnp.concatenate([ones, cos, cos, jnp.ones((S, tail), F32)], axis=1)
    sin_t = jnp.concatenate([jnp.zeros((S, MLA_NOPE), F32), -sin, sin, jnp.zeros((S, tail), F32)], axis=1)
    del half
    return cos_t, sin_t


def _layout_mla_weights(w_uq, w_ukv):
    H = MLA_HEADS
    dq = MLA_NOPE + MLA_ROPE
    wq = jnp.concatenate([_pad_cols(w_uq[:, h * dq:(h + 1) * dq], LANE) for h in range(H)], axis=1)
    dkv = MLA_NOPE + MLA_V
    wk, wv = [], []
    for h in range(H):
        blk = w_ukv[:, h * dkv:(h + 1) * dkv]
        wk.append(_pad_cols(blk[:, :MLA_NOPE], LANE))
        v = blk[:, MLA_NOPE:]
        zero = jnp.zeros_like(v)
        wv.append(jnp.concatenate([v, zero] if h % 2 == 0 else [zero, v], axis=1))
    return wq.astype(BF16), jnp.concatenate(wk, axis=1).astype(BF16), jnp.concatenate(wv, axis=1).astype(BF16)


def _mla_prep(p_mla, q_norm_g, kv_norm_g, wq, wk, wv, tables, tm=512):
    B, S, _ = p_mla.shape
    H = MLA_HEADS
    cos_t, sin_t = tables
    const = lambda shape: pl.BlockSpec(shape, lambda b, i: (0,) * len(shape))
    qkv_spec = pl.BlockSpec((None, H, tm, LANE), lambda b, i: (b, 0, i, 0))
    qkv_shape = jax.ShapeDtypeStruct((B, H, S, LANE), BF16)
    return pl.pallas_call(
        _mla_prep_kernel, name="mla_prep",
        grid=(B, S // tm),
        in_specs=[pl.BlockSpec((None, tm, W_MLA), lambda b, i: (b, i, 0)),
                  const((1, MLA_Q_RANK)), const((MLA_Q_RANK, H * LANE)),
                  const((1, MLA_KV_RANK)), const((MLA_KV_RANK, H * LANE)), const((MLA_KV_RANK, H * LANE)),
                  pl.BlockSpec((tm, LANE), lambda b, i: (i, 0)),
                  pl.BlockSpec((tm, LANE), lambda b, i: (i, 0))],
        out_specs=[qkv_spec, qkv_spec, qkv_spec],
        out_shape=[qkv_shape, qkv_shape, qkv_shape],
        compiler_params=_cparams("parallel", "parallel"),
    )(p_mla, q_norm_g.reshape(1, -1), wq, kv_norm_g.reshape(1, -1), wk, wv, cos_t, sin_t)


def _mla_ones_lane(h):
    return MLA_V if h % 2 == 0 else 0


def _mla_attn_kernel(qi_ref, kj_ref, q_ref, k_ref, v_ref, o_ref, m_ref, acc_ref):
    H = MLA_HEADS
    tq, tk = q_ref.shape[1], k_ref.shape[1]
    i = qi_ref[pl.program_id(1)]
    j = kj_ref[pl.program_id(1)]

    @pl.when(j == 0)
    def _():
        m_ref[...] = jnp.full(m_ref.shape, NEG_INF, F32)
        acc_ref[...] = jnp.zeros_like(acc_ref)

    def sweep(masked):
        if masked:
            causal = (_iota((tq, tk), 1) - _iota((tq, tk), 0)) <= i * tq - j * tk
        scores = [lax.dot_general(q_ref[h], k_ref[h], (((1,), (1,)), ((), ())), preferred_element_type=F32)
                  for h in range(H)]
        for h in range(H):
            s = scores[h]
            if masked:
                s = jnp.where(causal, s, NEG_INF)
            m_prev = m_ref[h]
            m_new = jnp.maximum(m_prev, jnp.max(s, axis=-1, keepdims=True))
            p = jnp.exp2(s - jnp.tile(m_new, (1, tk // LANE)))
            acc_ref[h] = (jnp.exp2(m_prev - m_new) * acc_ref[h]
                          + jnp.dot(p.astype(BF16), v_ref[h], preferred_element_type=F32))
            m_ref[h] = m_new

    ratio = tq // tk

    @pl.when(j < ratio * i)
    def _():
        sweep(False)

    @pl.when(j >= ratio * i)
    def _():
        sweep(True)

    @pl.when(j == ratio * i + ratio - 1)
    def _():
        lane = _iota((tq, LANE), 1)
        for pair in range(H // 2):
            he, ho = 2 * pair, 2 * pair + 1
            acc_e, acc_o = acc_ref[he], acc_ref[ho]
            le = acc_e[:, _mla_ones_lane(he):_mla_ones_lane(he) + 1]
            lo = acc_o[:, _mla_ones_lane(ho):_mla_ones_lane(ho) + 1]
            o_ref[:, pair * LANE:(pair + 1) * LANE] = jnp.where(lane < MLA_V, acc_e / le, acc_o / lo)


def _mla_attn(q, k, v):
    B, H, S, _ = q.shape
    t = min(MLA_TILE, S)
    tq = min(MLA_Q_TILE, S)
    pairs = [(i, j) for i in range(S // tq) for j in range((i + 1) * tq // t)]
    qi = jnp.asarray([p[0] for p in pairs], jnp.int32)
    kj = jnp.asarray([p[1] for p in pairs], jnp.int32)
    grid_spec = pltpu.PrefetchScalarGridSpec(
        num_scalar_prefetch=2,
        grid=(B, len(pairs)),
        in_specs=[pl.BlockSpec((None, H, tq, LANE), lambda b, p, qi, kj: (b, 0, qi[p], 0)),
                  pl.BlockSpec((None, H, t, LANE), lambda b, p, qi, kj: (b, 0, kj[p], 0)),
                  pl.BlockSpec((None, H, t, LANE), lambda b, p, qi, kj: (b, 0, kj[p], 0))],
        out_specs=pl.BlockSpec((None, tq, GROUP_W), lambda b, p, qi, kj: (b, qi[p], 0)),
        scratch_shapes=[pltpu.VMEM((H, tq, LANE), F32), pltpu.VMEM((H, tq, LANE), F32)],
    )
    return pl.pallas_call(
        _mla_attn_kernel, name="mla_attn",
        grid_spec=grid_spec,
        out_shape=jax.ShapeDtypeStruct((B, S, GROUP_W), F32),
        compiler_params=_cparams("parallel", "arbitrary"),
    )(qi, kj, q, k, v)


POS_HI = NSA_DK
POS_LO = NSA_DK + 3
POS_ONE = NSA_DK + 6
POS_PAD = NSA_DK + 7
ONES_LANE = NSA_DV


def _split_bf16(x, parts=3):
    out, rem = [], np.float64(x)
    for _ in range(parts):
        piece = np.float64(np.float32(rem).astype(jnp.bfloat16).astype(np.float32))
        out.append(float(piece))
        rem = rem - piece
    return out


def _nsa_query_table():
    H = NSA_HEADS
    tab = np.zeros((2 * H, LANE), np.float32)
    for h in range(H):
        c = 2.0 ** (-8.0 * (h + 1) / H) * LOG2_E
        pieces = _split_bf16(c)
        tab[h, POS_HI:POS_HI + 3] = pieces
        tab[h, POS_LO:POS_LO + 3] = pieces
        tab[h, POS_PAD] = NEG_INF
        tab[H + h, POS_ONE] = -sum(pieces)
    return jnp.asarray(tab)


def _nsa_pos_lanes(pos, lo_offset=0.0):
    t = np.zeros((len(pos), LANE - NSA_DK), np.float32)
    t[:, POS_HI - NSA_DK:POS_HI - NSA_DK + 3] = (NSA_SLC_LEN * (pos // NSA_SLC_LEN))[:, None]
    t[:, POS_LO - NSA_DK:POS_LO - NSA_DK + 3] = (pos % NSA_SLC_LEN + lo_offset)[:, None]
    t[:, POS_ONE - NSA_DK] = 1.0
    return t


def _nsa_queries(q_ref, qtab_ref, qb):
    Q, H = q_ref.shape[0], NSA_HEADS
    qpos = (qb * Q + _iota((Q, 1), 0)).astype(F32)
    out = []
    for h in range(H):
        q = q_ref[:, h * LANE:(h + 1) * LANE] * (NSA_DK ** -0.5 * LOG2_E)
        out.append((q + qtab_ref[h:h + 1, :] + qtab_ref[H + h:H + h + 1, :] * qpos).astype(BF16))
    return out


def _normalise(o):
    return o / o[:, ONES_LANE:ONES_LANE + 1]


def _unstack_heads(o):
    Q = o.shape[0] // NSA_HEADS
    lane = _iota((Q, LANE), 1)
    out = []
    for pair in range(NSA_HEADS // 2):
        even = o[(2 * pair) * Q:(2 * pair + 1) * Q]
        odd = o[(2 * pair + 1) * Q:(2 * pair + 2) * Q]
        out.append(jnp.where(lane < NSA_DV, even, pltpu.roll(odd, NSA_DV, 1)))
    return jnp.concatenate(out, axis=1)


def _nsa_cmp_kernel(uk_ref, uv_ref, pek_ref, pev_ref, w1k_ref, w1v_ref, w2k_ref, w2v_ref, cpos_ref,
                    kc_ref, vc_ref, sh_ref):
    nb = uk_ref.shape[0]
    half = uk_ref.shape[1]

    def hidden(u_ref, pe_ref, w1_ref):
        u = u_ref[...]
        first = _mm(u + pe_ref[0:1, :], w1_ref[0:half, :])
        second = _mm(u + pe_ref[1:2, :], w1_ref[half:2 * half, :])
        sh_ref[0:nb, :] = second
        sh_ref[nb:nb + 8, :] = jnp.zeros((8, NSA_CMP_HID), F32)
        return first + sh_ref[pl.ds(1, nb), :]

    hk = _silu(hidden(uk_ref, pek_ref, w1k_ref))
    hv = _silu(hidden(uv_ref, pev_ref, w1v_ref))
    kc_ref[...] = (_mm(hk, w2k_ref[...]) + cpos_ref[...]).astype(BF16)
    ones_lane = jnp.where(_iota((1, LANE), 1) == ONES_LANE, 1.0, 0.0)
    vc_ref[...] = (_mm(hv, w2v_ref[...]) + ones_lane).astype(BF16)


def _nsa_compress(uk, uv, pe_k, w1_k, w2_k, pe_v, w1_v, w2_v):
    B, nb, half = uk.shape
    hid = NSA_CMP_HID
    const = lambda shape: pl.BlockSpec(shape, lambda b: (0,) * len(shape))
    w2k = _pad_cols(w2_k, LANE).astype(BF16)
    w2v = _pad_cols(w2_v, LANE).astype(BF16)
    centre = _nsa_pos_lanes(np.arange(nb) * NSA_CMP_STRIDE, 0.5 * (NSA_CMP_LEN - 1))
    cpos = jnp.asarray(np.concatenate([np.zeros((nb, NSA_DK), np.float32), centre], axis=1))
    out_spec = pl.BlockSpec((None, nb, LANE), lambda b: (b, 0, 0))
    out_shape = jax.ShapeDtypeStruct((B, nb, LANE), BF16)
    return pl.pallas_call(
        _nsa_cmp_kernel, name="nsa_compress",
        grid=(B,),
        in_specs=[pl.BlockSpec((None, nb, half), lambda b: (b, 0, 0)),
                  pl.BlockSpec((None, nb, half), lambda b: (b, 0, 0)),
                  const((2, half)), const((2, half)),
                  const((2 * half, hid)), const((2 * half, hid)),
                  const((hid, LANE)), const((hid, LANE)), const((nb, LANE))],
        out_specs=[out_spec, out_spec],
        out_shape=[out_shape, out_shape],
        scratch_shapes=[pltpu.VMEM((nb + 8, hid), F32)],
        compiler_params=_cparams("parallel"),
    )(uk, uv, pe_k.reshape(2, half), pe_v.reshape(2, half), w1_k.astype(BF16), w1_v.astype(BF16), w2k, w2v, cpos)


def _nsa_sel_kernel(q_ref, qtab_ref, kc_ref, vc_ref, ovt_ref, oc_ref, selb_ref, any_ref, *, n_slc, top_n):
    Q, H = q_ref.shape[0], NSA_HEADS
    qb = pl.program_id(1)
    nc = kc_ref.shape[0]
    qs = jnp.concatenate(_nsa_queries(q_ref, qtab_ref, qb), axis=0)
    s = lax.dot_general(qs, kc_ref[...], (((1,), (1,)), ((), ())), preferred_element_type=F32)
    qpos = qb * Q + (_iota((H * Q, nc), 0) & (Q - 1))
    block_end = _iota((H * Q, nc), 1) * NSA_CMP_STRIDE + (NSA_CMP_LEN - 1)
    s = jnp.where(block_end <= qpos, s, NEG_INF)
    e = jnp.exp2(s - jnp.max(s, axis=-1, keepdims=True))
    qpos_col = qb * Q + (_iota((H * Q, 1), 0) & (Q - 1))
    has_block = jnp.where(qpos_col >= NSA_CMP_LEN - 1, 1.0, 0.0)
    p = e * (has_block / jnp.sum(e, axis=-1, keepdims=True))
    oc_ref[...] = _unstack_heads(jnp.dot(p.astype(BF16), vc_ref[...], preferred_element_type=F32))

    p_sum = p[0:Q]
    for h in range(1, H):
        p_sum = p_sum + p[h * Q:(h + 1) * Q]
    imp = lax.dot_general(ovt_ref[...], p_sum, (((1,), (1,)), ((), ())), precision=HIGHEST,
                          preferred_element_type=F32)
    blk = _iota((LANE, Q), 0)
    q_blk = (qb * Q + _iota((LANE, Q), 1)) >> int(math.log2(NSA_SLC_LEN))
    causal = blk <= q_blk
    for forced_blk in (0, q_blk, q_blk - 1):
        imp = jnp.where(blk == forced_blk, FORCED_SCORE, imp)
    imp = jnp.where(causal, imp, -1.0)
    imp = jnp.where(blk < n_slc, imp, -2.0)
    blk_f = blk.astype(F32)
    sel = jnp.zeros((LANE, Q), F32)
    for _ in range(top_n):
        m = jnp.max(imp, axis=0, keepdims=True)
        first = jnp.min(jnp.where(imp == m, blk_f, float(LANE)), axis=0, keepdims=True)
        hit = blk_f == first
        sel = jnp.where(hit, 1.0, sel)
        imp = jnp.where(hit, -3.0, imp)
    sel = jnp.where(causal, sel, 0.0).T
    selb_ref[...] = jnp.where(sel > 0.5, 0.0, NEG_INF).astype(BF16)
    any_ref[...] = jnp.max(sel, axis=0, keepdims=True)


def _nsa_select(p_nsa, qtab, kc, vc, overlap_t):
    B, S, _ = p_nsa.shape
    Q = NSA_Q
    nqb = S // Q
    nc = kc.shape[1]
    n_slc = S // NSA_SLC_LEN
    kern = functools.partial(_nsa_sel_kernel, n_slc=n_slc, top_n=min(NSA_TOPN, n_slc))
    return pl.pallas_call(
        kern, name="nsa_select",
        grid=(B, nqb),
        in_specs=[pl.BlockSpec((None, Q, NSA_HEADS * LANE), lambda b, i: (b, i, 0)),
                  pl.BlockSpec((2 * NSA_HEADS, LANE), lambda b, i: (0, 0)),
                  pl.BlockSpec((None, nc, LANE), lambda b, i: (b, 0, 0)),
                  pl.BlockSpec((None, nc, LANE), lambda b, i: (b, 0, 0)),
                  pl.BlockSpec((LANE, nc), lambda b, i: (0, 0))],
        out_specs=[pl.BlockSpec((None, Q, GROUP_W), lambda b, i: (b, i, 0)),
                   pl.BlockSpec((None, Q, LANE), lambda b, i: (b, i, 0)),
                   pl.BlockSpec((None, None, 1, LANE), lambda b, i: (b, i, 0, 0))],
        out_shape=[jax.ShapeDtypeStruct((B, S, GROUP_W), F32),
                   jax.ShapeDtypeStruct((B, S, LANE), BF16),
                   jax.ShapeDtypeStruct((B, nqb, 1, LANE), F32)],
        compiler_params=_cparams("parallel", "parallel"),
    )(p_nsa, qtab, kc, vc, overlap_t)


def _nsa_attn_kernel(flags_ref, q_ref, qtab_ref, gate_ref, oc_ref, selb_ref, ks_ref, vs_ref, kw_ref, vw_ref,
                     o_ref, m_ref, acc_ref, *, nt):
    Q, H, TK = q_ref.shape[0], NSA_HEADS, NSA_TILE
    PART = 2 * Q
    b = pl.program_id(0)
    qb = pl.program_id(1)
    nqb = pl.num_programs(1)
    qh = _nsa_queries(q_ref, qtab_ref, qb)
    nt_dims = (((1,), (1,)), ((), ()))

    selb = selb_ref[...]
    qs_sel = jnp.concatenate([jnp.concatenate([q, selb], axis=1) for q in qh], axis=0)
    m_ref[...] = jnp.full(m_ref.shape, NEG_INF, F32)
    acc_ref[...] = jnp.zeros_like(acc_ref)

    def update(t, diagonal):
        rows = pl.ds(pl.multiple_of(t * TK, TK), TK)
        k_tile, v_tile = ks_ref[rows, :], vs_ref[rows, :]
        parts = [slice(i * PART, (i + 1) * PART) for i in range(H * Q // PART)]
        scores = [lax.dot_general(qs_sel[r], k_tile, nt_dims, preferred_element_type=F32) for r in parts]
        for r, s in zip(parts, scores):
            if diagonal:
                ahead = _iota((PART, TK), 1) - (_iota((PART, TK), 0) & (Q - 1))
                s = jnp.where(ahead <= qb * Q - t * TK, s, NEG_INF)
            m_prev = m_ref[r]
            m_new = jnp.maximum(m_prev, jnp.max(s, axis=-1, keepdims=True))
            p = jnp.exp2(s - jnp.tile(m_new, (1, TK // LANE)))
            acc_ref[r] = (jnp.exp2(m_prev - m_new) * acc_ref[r]
                          + jnp.dot(p.astype(BF16), v_tile, preferred_element_type=F32))
            m_ref[r] = m_new

    def tile(t, carry):
        @pl.when(flags_ref[(b * nqb + qb) * nt + t] > 0)
        def _():
            update(t, False)
        return carry

    t_diag = (qb * Q) // TK
    lax.fori_loop(0, t_diag, tile, 0)
    update(t_diag, True)
    o_s = _unstack_heads(_normalise(acc_ref[...]))

    W = Q + NSA_WIN
    rows_w = pl.ds(pl.multiple_of(qb * Q, Q), W)
    qs_win = jnp.concatenate(qh, axis=0)
    s = lax.dot_general(qs_win, kw_ref[rows_w, :], nt_dims, preferred_element_type=F32)
    dist_w = (_iota((H * Q, W), 0) & (Q - 1)) - _iota((H * Q, W), 1) + NSA_WIN
    s = jnp.where(dist_w >= 0, jnp.where(dist_w < NSA_WIN, s, NEG_INF), NEG_INF)
    p = jnp.exp2(s - jnp.max(s, axis=-1, keepdims=True))
    o_w = _unstack_heads(_normalise(jnp.dot(p.astype(BF16), vw_ref[rows_w, :], preferred_element_type=F32)))

    gates = jax.nn.sigmoid(gate_ref[...])
    lane_head = _iota((LANE, GROUP_W), 1) // NSA_DV
    src = _iota((LANE, GROUP_W), 0)
    out = None
    for j, branch in enumerate((oc_ref[...], o_s, o_w)):
        g = _mm_f32(gates, jnp.where(src == lane_head * 3 + j, 1.0, 0.0))
        out = g * branch if out is None else out + g * branch
    o_ref[...] = out


def _nsa_attend(p_nsa, qtab, o_c, selb, flags, ks, vs, kw, vw):
    B, S, _ = p_nsa.shape
    Q = NSA_Q
    nqb = S // Q
    nt = S // NSA_TILE
    gate_blk = (W_NSA - LANE) // LANE
    kern = functools.partial(_nsa_attn_kernel, nt=nt)
    whole = lambda rows, width: pl.BlockSpec((None, rows, width), lambda b, i, f: (b, 0, 0))
    grid_spec = pltpu.PrefetchScalarGridSpec(
        num_scalar_prefetch=1,
        grid=(B, nqb),
        in_specs=[pl.BlockSpec((None, Q, NSA_HEADS * LANE), lambda b, i, f: (b, i, 0)),
                  pl.BlockSpec((2 * NSA_HEADS, LANE), lambda b, i, f: (0, 0)),
                  pl.BlockSpec((None, Q, LANE), lambda b, i, f: (b, i, gate_blk)),
                  pl.BlockSpec((None, Q, GROUP_W), lambda b, i, f: (b, i, 0)),
                  pl.BlockSpec((None, Q, LANE), lambda b, i, f: (b, i, 0)),
                  whole(S, 2 * LANE), whole(S, LANE), whole(S + NSA_WIN, LANE), whole(S + NSA_WIN, LANE)],
        out_specs=pl.BlockSpec((None, Q, GROUP_W), lambda b, i, f: (b, i, 0)),
        scratch_shapes=[pltpu.VMEM((NSA_HEADS * Q, LANE), F32), pltpu.VMEM((NSA_HEADS * Q, LANE), F32)],
    )
    return pl.pallas_call(
        kern, name="nsa_attend",
        grid_spec=grid_spec,
        out_shape=jax.ShapeDtypeStruct((B, S, GROUP_W), F32),
        compiler_params=_cparams("parallel", "parallel"),
    )(flags, p_nsa, qtab, p_nsa, o_c, selb, ks, vs, kw, vw)


def _nsa_tables(S):
    nc = S // NSA_CMP_STRIDE
    n = np.arange(nc)[None, :]
    j = np.arange(LANE)[:, None]
    start = n * NSA_CMP_STRIDE
    ov = (start < (j + 1) * NSA_SLC_LEN) & (start + NSA_CMP_LEN - 1 >= j * NSA_SLC_LEN)
    ov &= (n < (S - NSA_CMP_LEN) // NSA_CMP_STRIDE + 1) & (j < S // NSA_SLC_LEN)
    pos = np.arange(S)
    sel_lanes = np.concatenate([_nsa_pos_lanes(pos),
                                (pos[:, None] // NSA_SLC_LEN == np.arange(LANE)[None, :]).astype(np.float32)], axis=1)
    win_lanes = np.concatenate([np.zeros((NSA_WIN, LANE - NSA_DK), np.float32), _nsa_pos_lanes(pos)], axis=0)
    win_lanes[:NSA_WIN, POS_ONE - NSA_DK] = 1.0
    win_lanes[:NSA_WIN, POS_PAD - NSA_DK] = 1.0
    ones = np.zeros((S, LANE - NSA_DV), np.float32)
    ones[:, ONES_LANE - NSA_DV] = 1.0
    as_bf16 = lambda a: jnp.asarray(a, dtype=BF16)
    return (_nsa_query_table(), jnp.asarray(ov.astype(np.float32)), as_bf16(sel_lanes), as_bf16(win_lanes),
            as_bf16(ones))


def _nsa(p_nsa, pe_k, w1_k, w2_k, pe_v, w1_v, w2_v, tables):
    B, S, _ = p_nsa.shape
    qtab, overlap_t, sel_lanes, win_lanes, ones = tables
    q_w = NSA_HEADS * LANE
    half = NSA_CMP_STRIDE * NSA_DK
    piece = lambda idx: p_nsa[:, :, q_w + idx * NSA_DK:q_w + (idx + 1) * NSA_DK]
    uk = piece(0).reshape(B, S // NSA_CMP_STRIDE, half)
    uv = piece(1).reshape(B, S // NSA_CMP_STRIDE, half)
    kc, vc = _nsa_compress(uk, uv, pe_k, w1_k, w2_k, pe_v, w1_v, w2_v)
    o_c, selb, blk_any = _nsa_select(p_nsa, qtab, kc, vc, overlap_t)
    per_tile = NSA_TILE // NSA_SLC_LEN
    nt = S // NSA_TILE
    flags = blk_any[:, :, 0, :nt * per_tile].reshape(B, S // NSA_Q, nt, per_tile).max(axis=-1)
    flags = (flags > 0).astype(jnp.int32).reshape(-1)
    bcast = lambda t: jnp.broadcast_to(t, (B,) + t.shape)
    ks = jnp.concatenate([piece(2).astype(BF16), bcast(sel_lanes)], axis=-1)
    vs = jnp.concatenate([piece(3).astype(BF16), bcast(ones)], axis=-1)
    pad = ((0, 0), (NSA_WIN, 0), (0, 0))
    kw = jnp.concatenate([jnp.pad(piece(4).astype(BF16), pad), bcast(win_lanes)], axis=-1)
    vw = jnp.pad(jnp.concatenate([piece(5).astype(BF16), bcast(ones)], axis=-1), pad)
    return _nsa_attend(p_nsa, qtab, o_c, selb, flags, ks, vs, kw, vw)


def _out_proj_kernel(h_ref, ya_ref, yb_ref, yc_ref, yd_ref, w_ref, g_ref, b_ref, o_ref):
    mix = None
    for idx, y_ref in enumerate((ya_ref, yb_ref, yc_ref, yd_ref)):
        part = _mm(y_ref[...], w_ref[idx * GROUP_W:(idx + 1) * GROUP_W, :])
        mix = part if mix is None else mix + part
    o_ref[...] = _layer_norm(DEEPNORM_ALPHA * h_ref[...] + mix, g_ref[...], b_ref[...])


def _out_proj(h2, ys, w_out, g, b, tm=512):
    T, D = h2.shape
    row = lambda w: pl.BlockSpec((tm, w), lambda i: (i, 0))
    const = lambda shape: pl.BlockSpec(shape, lambda i: (0,) * len(shape))
    return pl.pallas_call(
        _out_proj_kernel, name="out_proj_ln",
        grid=(T // tm,),
        in_specs=[row(D), row(GROUP_W), row(GROUP_W), row(GROUP_W), row(GROUP_W),
                  const((D, D)), const((1, D)), const((1, D))],
        out_specs=row(D),
        out_shape=jax.ShapeDtypeStruct((T, D), F32),
        compiler_params=_cparams("parallel"),
    )(h2, *ys, w_out, g.reshape(1, D), b.reshape(1, D))


def _mlp_kernel(h_ref, w1_ref, w2_ref, g_ref, b_ref, o_ref, acc_ref):
    f = pl.program_id(1)

    @pl.when(f == 0)
    def _():
        acc_ref[...] = jnp.zeros_like(acc_ref)

    a = jnp.maximum(_mm(h_ref[...], w1_ref[...]), 0.0)
    acc_ref[...] += _mm(a * a, w2_ref[...])

    @pl.when(f == pl.num_programs(1) - 1)
    def _():
        o_ref[...] = _layer_norm(DEEPNORM_ALPHA * h_ref[...] + acc_ref[...], g_ref[...], b_ref[...])


def _mlp(h2, w1, w2, g, b, tm=1024, tf=1024):
    T, D = h2.shape
    F = w1.shape[1]
    return pl.pallas_call(
        _mlp_kernel, name="mlp_ln",
        grid=(T // tm, F // tf),
        in_specs=[pl.BlockSpec((tm, D), lambda i, f: (i, 0)),
                  pl.BlockSpec((D, tf), lambda i, f: (0, f)),
                  pl.BlockSpec((tf, D), lambda i, f: (f, 0)),
                  pl.BlockSpec((1, D), lambda i, f: (0, 0)),
                  pl.BlockSpec((1, D), lambda i, f: (0, 0))],
        out_specs=pl.BlockSpec((tm, D), lambda i, f: (i, 0)),
        out_shape=jax.ShapeDtypeStruct((T, D), F32),
        scratch_shapes=[pltpu.VMEM((tm, D), F32)],
        compiler_params=_cparams("parallel", "arbitrary"),
    )(h2, w1, w2, g.reshape(1, D), b.reshape(1, D))


def kernel(x, ln_emb_g, ln_emb_b, w_in, conv_w, conv_b, dt_bias, a_log, d_skip, ssm_norm_g, q_norm_g, w_uq, kv_norm_g, w_ukv, cmp_pe_k, cmp_w1_k, cmp_w2_k, cmp_pe_v, cmp_w1_v, cmp_w2_v, w_out, ln1_g, ln1_b, w_mlp1, w_mlp2, ln2_g, ln2_b):
    B, S, D = x.shape
    assert D == D_MODEL and S % NSA_TILE == 0 and S // NSA_SLC_LEN <= LANE
    T = B * S
    ret_tables = _ret_tables(S)
    mla_tables = _mla_tables(S)
    nsa_tables = _nsa_tables(S)
    h = _entry_ln(x.reshape(T, D), ln_emb_g, ln_emb_b)
    for l in range(w_in.shape[0]):
        p_ssm, p_mla, p_ret, p_nsa = _in_proj(h, _layout_w_in(w_in[l]))
        y_a = _ssm(p_ssm.reshape(B, S, W_SSM), conv_w[l], conv_b[l], dt_bias[l], a_log[l], d_skip[l], ssm_norm_g[l])
        wq, wk, wv = _layout_mla_weights(w_uq[l], w_ukv[l])
        q, k, v = _mla_prep(p_mla.reshape(B, S, W_MLA), q_norm_g[l], kv_norm_g[l], wq, wk, wv, mla_tables)
        y_b = _mla_attn(q, k, v)
        y_c = _ret(p_ret.reshape(B, S, W_RET), ret_tables)
        y_d = _nsa(p_nsa.reshape(B, S, W_NSA), cmp_pe_k[l], cmp_w1_k[l], cmp_w2_k[l],
                   cmp_pe_v[l], cmp_w1_v[l], cmp_w2_v[l], nsa_tables)
        ys = [y.reshape(T, GROUP_W) for y in (y_a, y_b, y_c, y_d)]
        h = _out_proj(h, ys, w_out[l], ln1_g[l], ln1_b[l])
        h = _mlp(h, w_mlp1[l], w_mlp2[l], ln2_g[l], ln2_b[l])
    return h.reshape(B, S, D)
```

```python
import functools
import math

import jax
import jax.numpy as jnp
import numpy as np
from jax import lax
from jax.experimental import pallas as pl
from jax.experimental.pallas import tpu as pltpu

F32 = jnp.float32
BF16 = jnp.bfloat16
HIGHEST = lax.Precision.HIGHEST

D_MODEL = 1024
DEPTH = 2
GROUP_W = D_MODEL // 4
SSM_HEADS = 4
SSM_HEAD_DIM = GROUP_W // SSM_HEADS
SSM_GROUPS = 2
SSM_STATE = 128
SSM_CONV = 4
SSM_CHUNK = 128
SSM_XBC = GROUP_W + 2 * SSM_GROUPS * SSM_STATE
MLA_HEADS = 4
MLA_NOPE = 64
MLA_ROPE = 32
MLA_V = GROUP_W // MLA_HEADS
MLA_Q_RANK = 256
MLA_KV_RANK = 128
RET_HEADS = 4
RET_DK = 64
RET_DV = GROUP_W // RET_HEADS
RET_CHUNK = 128
NSA_HEADS = 4
NSA_DK = 64
NSA_DV = GROUP_W // NSA_HEADS
NSA_CMP_LEN = 32
NSA_CMP_STRIDE = 16
NSA_CMP_HID = 256
NSA_SLC_LEN = 64
NSA_TOPN = 16
NSA_WIN = 512
D_FF = 4 * D_MODEL
NSA_Q = 256
ROPE_THETA = 10000.0
EPS = 1e-5
NEG_INF = -1e30
LOG2_E = math.log2(math.e)
FORCED_SCORE = 1e9
DEEPNORM_ALPHA = (2.0 * DEPTH) ** 0.25

IN_SPLITS = (
    GROUP_W, SSM_XBC, SSM_HEADS,
    MLA_Q_RANK, MLA_KV_RANK, MLA_ROPE,
    RET_HEADS * RET_DK, RET_HEADS * RET_DK, RET_HEADS * RET_DV, GROUP_W,
    NSA_HEADS * NSA_DK, NSA_DK, NSA_DV, NSA_DK, NSA_DV, NSA_DK, NSA_DV, 3 * NSA_HEADS,
)

LANE = 128
W_SSM = GROUP_W + SSM_XBC + LANE
W_MLA = MLA_Q_RANK + MLA_KV_RANK + LANE
W_RET = 4 * GROUP_W
W_NSA = NSA_HEADS * LANE + LANE + LANE
W_KV = 2 * LANE + 3 * LANE
W_PROJ = W_SSM + W_MLA + W_RET + W_NSA + W_KV

NSA_TILE = 512
MLA_TILE = 1024
MLA_Q_TILE = 1024
VMEM_LIMIT = 48 * 1024 * 1024


def _cparams(*sem):
    return pltpu.CompilerParams(dimension_semantics=sem, vmem_limit_bytes=VMEM_LIMIT)


def _mm(a, b):
    return jnp.dot(a.astype(BF16), b.astype(BF16), preferred_element_type=F32)


def _mm_nt(a, b):
    return lax.dot_general(a.astype(BF16), b.astype(BF16), (((1,), (1,)), ((), ())),
                           preferred_element_type=F32)


def _mm_f32(a, b):
    return jnp.dot(a, b, precision=HIGHEST, preferred_element_type=F32)


def _silu(x):
    return x * jax.nn.sigmoid(x)


def _softplus(x):
    return jnp.maximum(x, 0.0) + jnp.log1p(jnp.exp(-jnp.abs(x)))


def _layer_norm(x, g, b):
    mu = jnp.mean(x, axis=-1, keepdims=True)
    xc = x - mu
    var = jnp.mean(xc * xc, axis=-1, keepdims=True)
    return xc * lax.rsqrt(var + EPS) * g + b


def _iota(shape, dim):
    return lax.broadcasted_iota(jnp.int32, shape, dim)


def _ln_kernel(x_ref, g_ref, b_ref, o_ref):
    o_ref[...] = _layer_norm(x_ref[...], g_ref[...], b_ref[...])


def _entry_ln(x2, g, b, tm=512):
    T, D = x2.shape
    return pl.pallas_call(
        _ln_kernel, name="entry_ln",
        grid=(T // tm,),
        in_specs=[pl.BlockSpec((tm, D), lambda i: (i, 0)),
                  pl.BlockSpec((1, D), lambda i: (0, 0)),
                  pl.BlockSpec((1, D), lambda i: (0, 0))],
        out_specs=pl.BlockSpec((tm, D), lambda i: (i, 0)),
        out_shape=jax.ShapeDtypeStruct((T, D), F32),
        compiler_params=_cparams("parallel"),
    )(x2, g.reshape(1, D), b.reshape(1, D))


def _pad_cols(w, width):
    return jnp.pad(w, ((0, 0), (0, width - w.shape[1])))


def _layout_w_in(w):
    offs = np.concatenate([[0], np.cumsum(IN_SPLITS)])
    p = [w[:, int(offs[i]):int(offs[i + 1])] for i in range(len(IN_SPLITS))]
    (ssm_z, ssm_xbc, ssm_dt, mla_cq, mla_ckv, mla_kr, ret_q, ret_k, ret_v, ret_g,
     nsa_q, nsa_kc, nsa_vc, nsa_ks, nsa_vs, nsa_kw, nsa_vw, nsa_gate) = p
    nsa_q_heads = [_pad_cols(nsa_q[:, h * NSA_DK:(h + 1) * NSA_DK], LANE) for h in range(NSA_HEADS)]
    cols = [ssm_z, ssm_xbc, _pad_cols(ssm_dt, LANE),
            mla_cq, mla_ckv, _pad_cols(mla_kr, LANE),
            ret_q, ret_k, ret_v, ret_g,
            *nsa_q_heads, nsa_kc, nsa_vc, _pad_cols(nsa_gate, LANE),
            _pad_cols(nsa_ks, 2 * LANE), _pad_cols(nsa_vs, LANE), _pad_cols(nsa_kw, LANE), _pad_cols(nsa_vw, LANE)]
    out = jnp.concatenate(cols, axis=1)
    assert out.shape[1] == W_PROJ
    return out.astype(BF16)


def _in_proj_kernel(h_ref, w_ref, kvtab_ref, ssm_ref, mla_ref, ret_ref, nsa_ref, kv_ref):
    hb = h_ref[...].astype(BF16)
    off = 0
    for ref, width in ((ssm_ref, W_SSM), (mla_ref, W_MLA), (ret_ref, W_RET), (nsa_ref, W_NSA)):
        ref[...] = jnp.dot(hb, w_ref[:, off:off + width], preferred_element_type=F32)
        off += width
    kv = jnp.dot(hb, w_ref[:, off:off + W_KV], preferred_element_type=F32)
    kv_ref[...] = (kv + kvtab_ref[...].astype(F32)).astype(BF16)


def _in_proj(h2, w_p, kv_table, tm=256):
    T, D = h2.shape
    S = kv_table.shape[0]
    widths = (W_SSM, W_MLA, W_RET, W_NSA)
    return pl.pallas_call(
        _in_proj_kernel, name="in_proj",
        grid=(T // tm,),
        in_specs=[pl.BlockSpec((tm, D), lambda i: (i, 0)),
                  pl.BlockSpec((D, W_PROJ), lambda i: (0, 0)),
                  pl.BlockSpec((tm, W_KV), lambda i: (i % (S // tm), 0))],
        out_specs=[pl.BlockSpec((tm, w), lambda i: (i, 0)) for w in widths + (W_KV,)],
        out_shape=[jax.ShapeDtypeStruct((T, w), F32) for w in widths] + [jax.ShapeDtypeStruct((T, W_KV), BF16)],
        compiler_params=_cparams("parallel"),
    )(h2, w_p, kv_table)


def _ssm_kernel(p_ref, cw_ref, cb_ref, dtb_ref, alog_ref, dskip_ref, ng_ref, o_ref, state_ref, ext_ref):
    @pl.when(pl.program_id(0) == 0)
    def _():
        state_ref[...] = jnp.zeros_like(state_ref)
        ext_ref[:, 0:8, :] = jnp.zeros((ext_ref.shape[0], 8, SSM_XBC), F32)

    for b in range(p_ref.shape[0]):
        _ssm_chunk(p_ref.at[b], cw_ref, cb_ref, dtb_ref, alog_ref, dskip_ref, ng_ref,
                   o_ref.at[b], state_ref.at[b], ext_ref.at[b])


def _ssm_chunk(p_ref, cw_ref, cb_ref, dtb_ref, alog_ref, dskip_ref, ng_ref, o_ref, state_ref, ext_ref):
    L, H, P, N = SSM_CHUNK, SSM_HEADS, SSM_HEAD_DIM, SSM_STATE
    z = p_ref[:, 0:GROUP_W]
    ext_ref[8:8 + L, :] = p_ref[:, GROUP_W:GROUP_W + SSM_XBC]
    conv = cb_ref[...]
    for j in range(SSM_CONV):
        conv = conv + ext_ref[pl.ds(8 - (SSM_CONV - 1) + j, L), :] * cw_ref[j:j + 1, :]
    ext_ref[0:8, :] = ext_ref[L:L + 8, :]
    xbc = _silu(conv)
    xs = xbc[:, 0:GROUP_W]
    b_in = xbc[:, GROUP_W:GROUP_W + SSM_GROUPS * N]
    c_in = xbc[:, GROUP_W + SSM_GROUPS * N:]

    dt = _softplus(p_ref[:, GROUP_W + SSM_XBC:] + dtb_ref[...])
    a = dt * (-jnp.exp(alog_ref[...]))
    row = _iota((L, L), 0)
    col = _iota((L, L), 1)
    tril = col <= row
    cs = _mm_f32(jnp.where(tril, 1.0, 0.0), a)
    cs_t = cs.T
    ecs = jnp.exp(cs)
    dte = jnp.exp(cs[L - 1:L, :] - cs)
    expand = jnp.where(_iota((LANE, H * P), 0) == _iota((LANE, H * P), 1) // P, 1.0, 0.0)
    dt_x = _mm_f32(dt, expand)
    ecs_x = _mm_f32(ecs, expand)
    dte_x = _mm_f32(dte, expand)

    xdt = xs * dt_x
    wx = xdt * dte_x
    head_of_lane = _iota((L, H * P), 1) // P
    y = xs * dskip_ref[...]
    y_off = []
    rep = H // SSM_GROUPS
    for g in range(SSM_GROUPS):
        cg = c_in[:, g * N:(g + 1) * N]
        bg = b_in[:, g * N:(g + 1) * N]
        cb = _mm_nt(cg, bg)
        for h in range(g * rep, (g + 1) * rep):
            diff = cs[:, h:h + 1] - cs_t[h:h + 1, :]
            seg = jnp.where(tril, jnp.exp(jnp.where(tril, diff, 0.0)), 0.0)
            yh = _mm(cb * seg, xdt)
            y = y + jnp.where(head_of_lane == h, yh, 0.0)
        lanes = slice(g * rep * P, (g + 1) * rep * P)
        st_prev = state_ref[:, lanes]
        y_off.append(_mm(cg, st_prev))
        state_ref[:, lanes] = st_prev * ecs_x[L - 1:L, lanes] + _mm(bg.T, wx[:, lanes])
    y = y + jnp.concatenate(y_off, axis=1) * ecs_x
    y = y * _silu(z)
    ms = jnp.mean(y * y, axis=-1, keepdims=True)
    o_ref[...] = y * lax.rsqrt(ms + EPS) * ng_ref[...]


def _ssm(p_ssm, conv_w, conv_b, dt_bias, a_log, d_skip, norm_g):
    B, S, _ = p_ssm.shape
    L = SSM_CHUNK
    pad_h = lambda v: jnp.pad(v, (0, LANE - SSM_HEADS)).reshape(1, LANE)
    const = lambda shape: pl.BlockSpec(shape, lambda c: (0,) * len(shape))
    return pl.pallas_call(
        _ssm_kernel, name="ssm",
        grid=(S // L,),
        in_specs=[pl.BlockSpec((B, L, W_SSM), lambda c: (0, c, 0)),
                  const((SSM_CONV, SSM_XBC)), const((1, SSM_XBC)), const((1, LANE)), const((1, LANE)),
                  const((1, GROUP_W)), const((1, GROUP_W))],
        out_specs=pl.BlockSpec((B, L, GROUP_W), lambda c: (0, c, 0)),
        out_shape=jax.ShapeDtypeStruct((B, S, GROUP_W), F32),
        scratch_shapes=[pltpu.VMEM((B, SSM_STATE, GROUP_W), F32),
                        pltpu.VMEM((B, L + 8, SSM_XBC), F32)],
        compiler_params=_cparams("arbitrary"),
    )(p_ssm, conv_w, conv_b.reshape(1, -1), pad_h(dt_bias), pad_h(a_log),
      jnp.repeat(d_skip, SSM_HEAD_DIM).reshape(1, GROUP_W), norm_g.reshape(1, GROUP_W))


def _ret_kernel(p_ref, cos_ref, sin_ref, dec_ref, zeta_ref, xi_ref, cd_ref, o_ref, state_ref):
    @pl.when(pl.program_id(0) == 0)
    def _():
        state_ref[...] = jnp.zeros_like(state_ref)

    for b in range(p_ref.shape[0]):
        _ret_chunk(p_ref.at[b], cos_ref, sin_ref, dec_ref, zeta_ref, xi_ref, cd_ref, o_ref.at[b], state_ref.at[b])


def _ret_chunk(p_ref, cos_ref, sin_ref, dec_ref, zeta_ref, xi_ref, cd_ref, o_ref, state_ref):
    L, H, DK, DV = RET_CHUNK, RET_HEADS, RET_DK, RET_DV
    W = H * DK
    q = p_ref[:, 0:W]
    k = p_ref[:, W:2 * W]
    v = p_ref[:, 2 * W:3 * W]
    gate = p_ref[:, 3 * W:4 * W]
    lane = _iota((L, W), 1)
    first_half = (lane % DK) < (DK // 2)
    head_of_lane = lane // DK

    def rope(x):
        partner = jnp.where(first_half, pltpu.roll(x, W - DK // 2, 1), pltpu.roll(x, DK // 2, 1))
        return x * cos_ref[...] + partner * sin_ref[...]

    qr = rope(q)
    kr = rope(k) * (DK ** -0.5)
    y = jnp.zeros((L, H * DV), F32)
    for h in range(H):
        qh = jnp.where(head_of_lane == h, qr, 0.0)
        sc = _mm_nt(qh, kr) * dec_ref[h]
        y = y + jnp.where(head_of_lane == h, _mm(sc, v), 0.0)
    st = state_ref[...]
    y = y + _mm(qr * xi_ref[...], st)
    same_head = (_iota((W, H * DV), 0) // DK) == (_iota((W, H * DV), 1) // DV)
    kv = _mm((kr * zeta_ref[...]).T, v)
    state_ref[...] = st * cd_ref[...] + jnp.where(same_head, kv, 0.0)
    ms = _mm_f32(y * y, jnp.where(same_head, 1.0 / DV, 0.0))
    o_ref[...] = y * lax.rsqrt(ms + EPS) * _silu(gate)


def _ret_tables(S):
    H, DK, L = RET_HEADS, RET_DK, RET_CHUNK
    inv = ROPE_THETA ** (-jnp.arange(0, DK, 2, dtype=F32) / DK)
    ang = jnp.arange(S, dtype=F32)[:, None] * inv[None, :]
    cos, sin = jnp.cos(ang), jnp.sin(ang)
    cos_t = jnp.tile(jnp.concatenate([cos, cos], axis=1), (1, H))
    sin_t = jnp.tile(jnp.concatenate([-sin, sin], axis=1), (1, H))
    log_gamma = jnp.log1p(-jnp.exp2(-5.0 - jnp.arange(H, dtype=F32)))
    pos = jnp.arange(L, dtype=F32)
    diff = pos[:, None] - pos[None, :]
    decay_in = jnp.where(diff >= 0, jnp.exp(jnp.maximum(diff, 0.0)[None] * log_gamma[:, None, None]), 0.0)
    zeta = jnp.exp((L - 1 - pos)[None] * log_gamma[:, None])
    xi = jnp.exp((pos + 1.0)[None] * log_gamma[:, None])
    chunk_decay = jnp.exp(L * log_gamma)
    zeta_x = jnp.repeat(zeta.T, DK, axis=1)
    xi_x = jnp.repeat(xi.T, DK, axis=1)
    cd_x = jnp.repeat(chunk_decay, RET_DV).reshape(1, H * RET_DV)
    return cos_t, sin_t, decay_in, zeta_x, xi_x, cd_x


def _ret(p_ret, tables):
    B, S, _ = p_ret.shape
    L, H = RET_CHUNK, RET_HEADS
    W = H * RET_DK
    cos_t, sin_t, decay_in, zeta_x, xi_x, cd_x = tables
    const = lambda shape: pl.BlockSpec(shape, lambda c: (0,) * len(shape))
    return pl.pallas_call(
        _ret_kernel, name="retention",
        grid=(S // L,),
        in_specs=[pl.BlockSpec((B, L, W_RET), lambda c: (0, c, 0)),
                  pl.BlockSpec((L, W), lambda c: (c, 0)),
                  pl.BlockSpec((L, W), lambda c: (c, 0)),
                  const((H, L, L)), const((L, W)), const((L, W)), const((1, H * RET_DV))],
        out_specs=pl.BlockSpec((B, L, GROUP_W), lambda c: (0, c, 0)),
        out_shape=jax.ShapeDtypeStruct((B, S, GROUP_W), F32),
        scratch_shapes=[pltpu.VMEM((B, W, H * RET_DV), F32)],
        compiler_params=_cparams("arbitrary"),
    )(p_ret, cos_t, sin_t, decay_in, zeta_x, xi_x, cd_x)


def _mla_prep_kernel(p_ref, qg_ref, wq_ref, kvg_ref, wk_ref, wv_ref, cos_ref, sin_ref,
                     q_ref, k_ref, v_ref):
    tm = p_ref.shape[0]
    cq = p_ref[:, 0:MLA_Q_RANK]
    ckv = p_ref[:, MLA_Q_RANK:MLA_Q_RANK + MLA_KV_RANK]
    kr = p_ref[:, MLA_Q_RANK + MLA_KV_RANK:]

    def rms(x, g):
        return x * lax.rsqrt(jnp.mean(x * x, axis=-1, keepdims=True) + EPS) * g

    q = _mm(rms(cq, qg_ref[...]), wq_ref[...])
    kvl = rms(ckv, kvg_ref[...])
    kn = _mm(kvl, wk_ref[...])
    vv = _mm(kvl, wv_ref[...])
    kr_sh = pltpu.roll(kr, MLA_NOPE, 1)
    lane = _iota((tm, LANE), 1)
    half = MLA_ROPE // 2
    low = (lane >= MLA_NOPE) & (lane < MLA_NOPE + half)
    cos = cos_ref[...]
    sin = sin_ref[...]

    def rope(x):
        partner = jnp.where(low, pltpu.roll(x, LANE - half, 1), pltpu.roll(x, half, 1))
        return x * cos + partner * sin

    scale = (MLA_NOPE + MLA_ROPE) ** -0.5 * LOG2_E
    for h in range(MLA_HEADS):
        sl = slice(h * LANE, (h + 1) * LANE)
        q_ref[h] = (rope(q[:, sl]) * scale).astype(BF16)
        k_ref[h] = rope(kn[:, sl] + kr_sh).astype(BF16)
        v_ref[h] = jnp.where(lane == _mla_ones_lane(h), 1.0, vv[:, sl]).astype(BF16)


def _mla_tables(S):
    half = MLA_ROPE // 2
    inv = ROPE_THETA ** (-jnp.arange(0, MLA_ROPE, 2, dtype=F32) / MLA_ROPE)
    ang = jnp.arange(S, dtype=F32)[:, None] * inv[None, :]
    cos, sin = jnp.cos(ang), jnp.sin(ang)
    ones = jnp.ones((S, MLA_NOPE), F32)
    tail = LANE - MLA_NOPE - MLA_ROPE
    cos_t = jnp.concatenate([ones, cos, cos, jnp.ones((S, tail), F32)], axis=1)
    sin_t = jnp.concatenate([jnp.zeros((S, MLA_NOPE), F32), -sin, sin, jnp.zeros((S, tail), F32)], axis=1)
    del half
    return cos_t, sin_t


def _layout_mla_weights(w_uq, w_ukv):
    H = MLA_HEADS
    dq = MLA_NOPE + MLA_ROPE
    wq = jnp.concatenate([_pad_cols(w_uq[:, h * dq:(h + 1) * dq], LANE) for h in range(H)], axis=1)
    dkv = MLA_NOPE + MLA_V
    wk, wv = [], []
    for h in range(H):
        blk = w_ukv[:, h * dkv:(h + 1) * dkv]
        wk.append(_pad_cols(blk[:, :MLA_NOPE], LANE))
        v = blk[:, MLA_NOPE:]
        zero = jnp.zeros_like(v)
        wv.append(jnp.concatenate([v, zero] if h % 2 == 0 else [zero, v], axis=1))
    return wq.astype(BF16), jnp.concatenate(wk, axis=1).astype(BF16), jnp.concatenate(wv, axis=1).astype(BF16)


def _mla_prep(p_mla, q_norm_g, kv_norm_g, wq, wk, wv, tables, tm=512):
    B, S, _ = p_mla.shape
    H = MLA_HEADS
    cos_t, sin_t = tables
    const = lambda shape: pl.BlockSpec(shape, lambda b, i: (0,) * len(shape))
    qkv_spec = pl.BlockSpec((None, H, tm, LANE), lambda b, i: (b, 0, i, 0))
    qkv_shape = jax.ShapeDtypeStruct((B, H, S, LANE), BF16)
    return pl.pallas_call(
        _mla_prep_kernel, name="mla_prep",
        grid=(B, S // tm),
        in_specs=[pl.BlockSpec((None, tm, W_MLA), lambda b, i: (b, i, 0)),
                  const((1, MLA_Q_RANK)), const((MLA_Q_RANK, H * LANE)),
                  const((1, MLA_KV_RANK)), const((MLA_KV_RANK, H * LANE)), const((MLA_KV_RANK, H * LANE)),
                  pl.BlockSpec((tm, LANE), lambda b, i: (i, 0)),
                  pl.BlockSpec((tm, LANE), lambda b, i: (i, 0))],
        out_specs=[qkv_spec, qkv_spec, qkv_spec],
        out_shape=[qkv_shape, qkv_shape, qkv_shape],
        compiler_params=_cparams("parallel", "parallel"),
    )(p_mla, q_norm_g.reshape(1, -1), wq, kv_norm_g.reshape(1, -1), wk, wv, cos_t, sin_t)


def _mla_ones_lane(h):
    return MLA_V if h % 2 == 0 else 0


def _mla_attn_kernel(qi_ref, kj_ref, q_ref, k_ref, v_ref, o_ref, m_ref, acc_ref):
    H = MLA_HEADS
    tq, tk = q_ref.shape[1], k_ref.shape[1]
    i = qi_ref[pl.program_id(1)]
    j = kj_ref[pl.program_id(1)]

    @pl.when(j == 0)
    def _():
        m_ref[...] = jnp.full(m_ref.shape, NEG_INF, F32)
        acc_ref[...] = jnp.zeros_like(acc_ref)

    def sweep(masked):
        if masked:
            causal = (_iota((tq, tk), 1) - _iota((tq, tk), 0)) <= i * tq - j * tk
        scores = [lax.dot_general(q_ref[h], k_ref[h], (((1,), (1,)), ((), ())), preferred_element_type=F32)
                  for h in range(H)]
        for h in range(H):
            s = scores[h]
            if masked:
                s = jnp.where(causal, s, NEG_INF)
            m_prev = m_ref[h]
            m_new = jnp.maximum(m_prev, jnp.max(s, axis=-1, keepdims=True))
            p = jnp.exp2(s - jnp.tile(m_new, (1, tk // LANE)))
            acc_ref[h] = (jnp.exp2(m_prev - m_new) * acc_ref[h]
                          + jnp.dot(p.astype(BF16), v_ref[h], preferred_element_type=F32))
            m_ref[h] = m_new

    ratio = tq // tk

    @pl.when(j < ratio * i)
    def _():
        sweep(False)

    @pl.when(j >= ratio * i)
    def _():
        sweep(True)

    @pl.when(j == ratio * i + ratio - 1)
    def _():
        lane = _iota((tq, LANE), 1)
        for pair in range(H // 2):
            he, ho = 2 * pair, 2 * pair + 1
            acc_e, acc_o = acc_ref[he], acc_ref[ho]
            le = acc_e[:, _mla_ones_lane(he):_mla_ones_lane(he) + 1]
            lo = acc_o[:, _mla_ones_lane(ho):_mla_ones_lane(ho) + 1]
            o_ref[:, pair * LANE:(pair + 1) * LANE] = jnp.where(lane < MLA_V, acc_e / le, acc_o / lo)


def _mla_attn(q, k, v):
    B, H, S, _ = q.shape
    t = min(MLA_TILE, S)
    tq = min(MLA_Q_TILE, S)
    pairs = [(i, j) for i in range(S // tq) for j in range((i + 1) * tq // t)]
    qi = jnp.asarray([p[0] for p in pairs], jnp.int32)
    kj = jnp.asarray([p[1] for p in pairs], jnp.int32)
    grid_spec = pltpu.PrefetchScalarGridSpec(
        num_scalar_prefetch=2,
        grid=(B, len(pairs)),
        in_specs=[pl.BlockSpec((None, H, tq, LANE), lambda b, p, qi, kj: (b, 0, qi[p], 0)),
                  pl.BlockSpec((None, H, t, LANE), lambda b, p, qi, kj: (b, 0, kj[p], 0)),
                  pl.BlockSpec((None, H, t, LANE), lambda b, p, qi, kj: (b, 0, kj[p], 0))],
        out_specs=pl.BlockSpec((None, tq, GROUP_W), lambda b, p, qi, kj: (b, qi[p], 0)),
        scratch_shapes=[pltpu.VMEM((H, tq, LANE), F32), pltpu.VMEM((H, tq, LANE), F32)],
    )
    return pl.pallas_call(
        _mla_attn_kernel, name="mla_attn",
        grid_spec=grid_spec,
        out_shape=jax.ShapeDtypeStruct((B, S, GROUP_W), F32),
        compiler_params=_cparams("parallel", "arbitrary"),
    )(qi, kj, q, k, v)


POS_HI = NSA_DK
POS_LO = NSA_DK + 3
POS_ONE = NSA_DK + 6
ONES_LANE = NSA_DV


def _split_bf16(x, parts=3):
    out, rem = [], np.float64(x)
    for _ in range(parts):
        piece = np.float64(np.float32(rem).astype(jnp.bfloat16).astype(np.float32))
        out.append(float(piece))
        rem = rem - piece
    return out


def _nsa_query_table():
    H = NSA_HEADS
    tab = np.zeros((2 * H, LANE), np.float32)
    for h in range(H):
        c = 2.0 ** (-8.0 * (h + 1) / H) * LOG2_E
        pieces = _split_bf16(c)
        tab[h, POS_HI:POS_HI + 3] = pieces
        tab[h, POS_LO:POS_LO + 3] = pieces
        tab[H + h, POS_ONE] = -sum(pieces)
    return jnp.asarray(tab)


def _nsa_pos_lanes(pos, lo_offset=0.0):
    t = np.zeros((len(pos), LANE - NSA_DK), np.float32)
    t[:, POS_HI - NSA_DK:POS_HI - NSA_DK + 3] = (NSA_SLC_LEN * (pos // NSA_SLC_LEN))[:, None]
    t[:, POS_LO - NSA_DK:POS_LO - NSA_DK + 3] = (pos % NSA_SLC_LEN + lo_offset)[:, None]
    t[:, POS_ONE - NSA_DK] = 1.0
    return t


def _nsa_queries(q_ref, qtab_ref, qb):
    Q, H = q_ref.shape[0], NSA_HEADS
    qpos = (qb * Q + _iota((Q, 1), 0)).astype(F32)
    out = []
    for h in range(H):
        q = q_ref[:, h * LANE:(h + 1) * LANE] * (NSA_DK ** -0.5 * LOG2_E)
        out.append((q + qtab_ref[h:h + 1, :] + qtab_ref[H + h:H + h + 1, :] * qpos).astype(BF16))
    return out


def _normalise(o):
    return o / o[:, ONES_LANE:ONES_LANE + 1]


def _unstack_heads(o):
    Q = o.shape[0] // NSA_HEADS
    lane = _iota((Q, LANE), 1)
    out = []
    for pair in range(NSA_HEADS // 2):
        even = o[(2 * pair) * Q:(2 * pair + 1) * Q]
        odd = o[(2 * pair + 1) * Q:(2 * pair + 2) * Q]
        out.append(jnp.where(lane < NSA_DV, even, pltpu.roll(odd, NSA_DV, 1)))
    return jnp.concatenate(out, axis=1)


def _nsa_cmp_kernel(uk_ref, uv_ref, pek_ref, pev_ref, w1k_ref, w1v_ref, w2k_ref, w2v_ref, cpos_ref,
                    kc_ref, vc_ref, sh_ref):
    nb = uk_ref.shape[0]
    half = uk_ref.shape[1]

    def hidden(u_ref, pe_ref, w1_ref):
        u = u_ref[...]
        first = _mm(u + pe_ref[0:1, :], w1_ref[0:half, :])
        second = _mm(u + pe_ref[1:2, :], w1_ref[half:2 * half, :])
        sh_ref[0:nb, :] = second
        sh_ref[nb:nb + 8, :] = jnp.zeros((8, NSA_CMP_HID), F32)
        return first + sh_ref[pl.ds(1, nb), :]

    hk = _silu(hidden(uk_ref, pek_ref, w1k_ref))
    hv = _silu(hidden(uv_ref, pev_ref, w1v_ref))
    kc_ref[...] = (_mm(hk, w2k_ref[...]) + cpos_ref[...]).astype(BF16)
    ones_lane = jnp.where(_iota((1, LANE), 1) == ONES_LANE, 1.0, 0.0)
    vc_ref[...] = (_mm(hv, w2v_ref[...]) + ones_lane).astype(BF16)


def _nsa_compress(uk, uv, pe_k, w1_k, w2_k, pe_v, w1_v, w2_v):
    B, nb, half = uk.shape
    hid = NSA_CMP_HID
    const = lambda shape: pl.BlockSpec(shape, lambda b: (0,) * len(shape))
    w2k = _pad_cols(w2_k, LANE).astype(BF16)
    w2v = _pad_cols(w2_v, LANE).astype(BF16)
    centre = _nsa_pos_lanes(np.arange(nb) * NSA_CMP_STRIDE, 0.5 * (NSA_CMP_LEN - 1))
    cpos = jnp.asarray(np.concatenate([np.zeros((nb, NSA_DK), np.float32), centre], axis=1))
    out_spec = pl.BlockSpec((None, nb, LANE), lambda b: (b, 0, 0))
    out_shape = jax.ShapeDtypeStruct((B, nb, LANE), BF16)
    return pl.pallas_call(
        _nsa_cmp_kernel, name="nsa_compress",
        grid=(B,),
        in_specs=[pl.BlockSpec((None, nb, half), lambda b: (b, 0, 0)),
                  pl.BlockSpec((None, nb, half), lambda b: (b, 0, 0)),
                  const((2, half)), const((2, half)),
                  const((2 * half, hid)), const((2 * half, hid)),
                  const((hid, LANE)), const((hid, LANE)), const((nb, LANE))],
        out_specs=[out_spec, out_spec],
        out_shape=[out_shape, out_shape],
        scratch_shapes=[pltpu.VMEM((nb + 8, hid), F32)],
        compiler_params=_cparams("parallel"),
    )(uk, uv, pe_k.reshape(2, half), pe_v.reshape(2, half), w1_k.astype(BF16), w1_v.astype(BF16), w2k, w2v, cpos)


def _nsa_sel_kernel(q_ref, qtab_ref, kc_ref, vc_ref, ovt_ref, oc_ref, selb_ref, any_ref, *, n_slc, top_n):
    Q, H = q_ref.shape[0], NSA_HEADS
    qb = pl.program_id(1)
    nc = kc_ref.shape[0]
    qs = jnp.concatenate(_nsa_queries(q_ref, qtab_ref, qb), axis=0)
    s = lax.dot_general(qs, kc_ref[...], (((1,), (1,)), ((), ())), preferred_element_type=F32)
    qpos = qb * Q + (_iota((H * Q, nc), 0) & (Q - 1))
    block_end = _iota((H * Q, nc), 1) * NSA_CMP_STRIDE + (NSA_CMP_LEN - 1)
    s = jnp.where(block_end <= qpos, s, NEG_INF)
    e = jnp.exp2(s - jnp.max(s, axis=-1, keepdims=True))
    qpos_col = qb * Q + (_iota((H * Q, 1), 0) & (Q - 1))
    has_block = jnp.where(qpos_col >= NSA_CMP_LEN - 1, 1.0, 0.0)
    p = e * (has_block / jnp.sum(e, axis=-1, keepdims=True))
    oc_ref[...] = _unstack_heads(jnp.dot(p.astype(BF16), vc_ref[...], preferred_element_type=F32))

    p_sum = p[0:Q]
    for h in range(1, H):
        p_sum = p_sum + p[h * Q:(h + 1) * Q]
    imp = lax.dot_general(ovt_ref[...], p_sum, (((1,), (1,)), ((), ())), precision=HIGHEST,
                          preferred_element_type=F32)
    blk = _iota((LANE, Q), 0)
    q_blk = (qb * Q + _iota((LANE, Q), 1)) >> int(math.log2(NSA_SLC_LEN))
    causal = blk <= q_blk
    for forced_blk in (0, q_blk, q_blk - 1):
        imp = jnp.where(blk == forced_blk, FORCED_SCORE, imp)
    imp = jnp.where(causal, imp, -1.0)
    imp = jnp.where(blk < n_slc, imp, -2.0)
    blk_f = blk.astype(F32)
    sel = jnp.zeros((LANE, Q), F32)
    for _ in range(top_n):
        m = jnp.max(imp, axis=0, keepdims=True)
        first = jnp.min(jnp.where(imp == m, blk_f, float(LANE)), axis=0, keepdims=True)
        hit = blk_f == first
        sel = jnp.where(hit, 1.0, sel)
        imp = jnp.where(hit, -3.0, imp)
    sel = jnp.where(causal, sel, 0.0).T
    selb_ref[...] = jnp.where(sel > 0.5, 0.0, NEG_INF).astype(BF16)
    any_ref[...] = jnp.max(sel, axis=0, keepdims=True)


def _nsa_select(p_nsa, qtab, kc, vc, overlap_t):
    B, S, _ = p_nsa.shape
    Q = NSA_Q
    nqb = S // Q
    nc = kc.shape[1]
    n_slc = S // NSA_SLC_LEN
    kern = functools.partial(_nsa_sel_kernel, n_slc=n_slc, top_n=min(NSA_TOPN, n_slc))
    return pl.pallas_call(
        kern, name="nsa_select",
        grid=(B, nqb),
        in_specs=[pl.BlockSpec((None, Q, NSA_HEADS * LANE), lambda b, i: (b, i, 0)),
                  pl.BlockSpec((2 * NSA_HEADS, LANE), lambda b, i: (0, 0)),
                  pl.BlockSpec((None, nc, LANE), lambda b, i: (b, 0, 0)),
                  pl.BlockSpec((None, nc, LANE), lambda b, i: (b, 0, 0)),
                  pl.BlockSpec((LANE, nc), lambda b, i: (0, 0))],
        out_specs=[pl.BlockSpec((None, Q, GROUP_W), lambda b, i: (b, i, 0)),
                   pl.BlockSpec((None, Q, LANE), lambda b, i: (b, i, 0)),
                   pl.BlockSpec((None, None, 1, LANE), lambda b, i: (b, i, 0, 0))],
        out_shape=[jax.ShapeDtypeStruct((B, S, GROUP_W), F32),
                   jax.ShapeDtypeStruct((B, S, LANE), BF16),
                   jax.ShapeDtypeStruct((B, nqb, 1, LANE), F32)],
        compiler_params=_cparams("parallel", "parallel"),
    )(p_nsa, qtab, kc, vc, overlap_t)


def _nsa_attn_kernel(flags_ref, q_ref, qtab_ref, gate_ref, oc_ref, selb_ref, ks_ref, vs_ref, kw_ref, vw_ref,
                     o_ref, m_ref, acc_ref, *, nt):
    Q, H, TK = q_ref.shape[0], NSA_HEADS, NSA_TILE
    PART = 2 * Q
    b = pl.program_id(0)
    qb = pl.program_id(1)
    nqb = pl.num_programs(1)
    qh = _nsa_queries(q_ref, qtab_ref, qb)
    nt_dims = (((1,), (1,)), ((), ()))

    selb = selb_ref[...]
    qs_sel = jnp.concatenate([jnp.concatenate([q, selb], axis=1) for q in qh], axis=0)
    m_ref[...] = jnp.full(m_ref.shape, NEG_INF, F32)
    acc_ref[...] = jnp.zeros_like(acc_ref)

    def update(t, diagonal):
        rows = pl.ds(pl.multiple_of(t * TK, TK), TK)
        k_tile, v_tile = ks_ref[rows, :], vs_ref[rows, :]
        parts = [slice(i * PART, (i + 1) * PART) for i in range(H * Q // PART)]
        scores = [lax.dot_general(qs_sel[r], k_tile, nt_dims, preferred_element_type=F32) for r in parts]
        for r, s in zip(parts, scores):
            if diagonal:
                ahead = _iota((PART, TK), 1) - (_iota((PART, TK), 0) & (Q - 1))
                s = jnp.where(ahead <= qb * Q - t * TK, s, NEG_INF)
            m_prev = m_ref[r]
            m_new = jnp.maximum(m_prev, jnp.max(s, axis=-1, keepdims=True))
            p = jnp.exp2(s - jnp.tile(m_new, (1, TK // LANE)))
            acc_ref[r] = (jnp.exp2(m_prev - m_new) * acc_ref[r]
                          + jnp.dot(p.astype(BF16), v_tile, preferred_element_type=F32))
            m_ref[r] = m_new

    def tile(t, carry):
        @pl.when(flags_ref[(b * nqb + qb) * nt + t] > 0)
        def _():
            update(t, False)
        return carry

    t_diag = (qb * Q) // TK
    lax.fori_loop(0, t_diag, tile, 0)
    update(t_diag, True)
    o_s = _unstack_heads(_normalise(acc_ref[...]))

    W = Q + NSA_WIN
    start = jnp.maximum(qb * Q - NSA_WIN, 0)
    rows_w = pl.ds(pl.multiple_of(start, Q), W)
    qs_win = jnp.concatenate(qh, axis=0)
    s = lax.dot_general(qs_win, kw_ref[rows_w, :], nt_dims, preferred_element_type=F32)
    lead = qb * Q - start
    q_minus_k = (_iota((H * Q, W), 0) & (Q - 1)) - _iota((H * Q, W), 1)
    s = jnp.where(q_minus_k >= -lead, jnp.where(q_minus_k < NSA_WIN - lead, s, NEG_INF), NEG_INF)
    p = jnp.exp2(s - jnp.max(s, axis=-1, keepdims=True))
    o_w = _unstack_heads(_normalise(jnp.dot(p.astype(BF16), vw_ref[rows_w, :], preferred_element_type=F32)))

    gates = jax.nn.sigmoid(gate_ref[...])
    lane_head = _iota((LANE, GROUP_W), 1) // NSA_DV
    src = _iota((LANE, GROUP_W), 0)
    out = None
    for j, branch in enumerate((oc_ref[...], o_s, o_w)):
        g = _mm_f32(gates, jnp.where(src == lane_head * 3 + j, 1.0, 0.0))
        out = g * branch if out is None else out + g * branch
    o_ref[...] = out


def _nsa_attend(p_nsa, qtab, o_c, selb, flags, kv):
    B, S, _ = p_nsa.shape
    Q = NSA_Q
    nqb = S // Q
    nt = S // NSA_TILE
    gate_blk = (W_NSA - LANE) // LANE
    kern = functools.partial(_nsa_attn_kernel, nt=nt)
    slab = lambda width, col: pl.BlockSpec((None, S, width), lambda b, i, f: (b, 0, col))
    grid_spec = pltpu.PrefetchScalarGridSpec(
        num_scalar_prefetch=1,
        grid=(B, nqb),
        in_specs=[pl.BlockSpec((None, Q, NSA_HEADS * LANE), lambda b, i, f: (b, i, 0)),
                  pl.BlockSpec((2 * NSA_HEADS, LANE), lambda b, i, f: (0, 0)),
                  pl.BlockSpec((None, Q, LANE), lambda b, i, f: (b, i, gate_blk)),
                  pl.BlockSpec((None, Q, GROUP_W), lambda b, i, f: (b, i, 0)),
                  pl.BlockSpec((None, Q, LANE), lambda b, i, f: (b, i, 0)),
                  slab(2 * LANE, 0), slab(LANE, 2), slab(LANE, 3), slab(LANE, 4)],
        out_specs=pl.BlockSpec((None, Q, GROUP_W), lambda b, i, f: (b, i, 0)),
        scratch_shapes=[pltpu.VMEM((NSA_HEADS * Q, LANE), F32), pltpu.VMEM((NSA_HEADS * Q, LANE), F32)],
    )
    return pl.pallas_call(
        kern, name="nsa_attend",
        grid_spec=grid_spec,
        out_shape=jax.ShapeDtypeStruct((B, S, GROUP_W), F32),
        compiler_params=_cparams("parallel", "parallel"),
    )(flags, p_nsa, qtab, p_nsa, o_c, selb, kv, kv, kv, kv)


def _nsa_tables(S):
    nc = S // NSA_CMP_STRIDE
    n = np.arange(nc)[None, :]
    j = np.arange(LANE)[:, None]
    start = n * NSA_CMP_STRIDE
    ov = (start < (j + 1) * NSA_SLC_LEN) & (start + NSA_CMP_LEN - 1 >= j * NSA_SLC_LEN)
    ov &= (n < (S - NSA_CMP_LEN) // NSA_CMP_STRIDE + 1) & (j < S // NSA_SLC_LEN)
    pos = np.arange(S)
    k_zero = np.zeros((S, NSA_DK), np.float32)
    block_onehot = (pos[:, None] // NSA_SLC_LEN == np.arange(LANE)[None, :]).astype(np.float32)
    v_lanes = np.zeros((S, LANE), np.float32)
    v_lanes[:, ONES_LANE] = 1.0
    kv_table = np.concatenate([k_zero, _nsa_pos_lanes(pos), block_onehot, v_lanes,
                               k_zero, _nsa_pos_lanes(pos), v_lanes], axis=1)
    assert kv_table.shape[1] == W_KV
    return _nsa_query_table(), jnp.asarray(ov.astype(np.float32)), jnp.asarray(kv_table, dtype=BF16)


def _nsa(p_nsa, kv, pe_k, w1_k, w2_k, pe_v, w1_v, w2_v, tables):
    B, S, _ = p_nsa.shape
    qtab, overlap_t, _ = tables
    q_w = NSA_HEADS * LANE
    half = NSA_CMP_STRIDE * NSA_DK
    piece = lambda idx: p_nsa[:, :, q_w + idx * NSA_DK:q_w + (idx + 1) * NSA_DK]
    uk = piece(0).reshape(B, S // NSA_CMP_STRIDE, half)
    uv = piece(1).reshape(B, S // NSA_CMP_STRIDE, half)
    kc, vc = _nsa_compress(uk, uv, pe_k, w1_k, w2_k, pe_v, w1_v, w2_v)
    o_c, selb, blk_any = _nsa_select(p_nsa, qtab, kc, vc, overlap_t)
    per_tile = NSA_TILE // NSA_SLC_LEN
    nt = S // NSA_TILE
    flags = blk_any[:, :, 0, :nt * per_tile].reshape(B, S // NSA_Q, nt, per_tile).max(axis=-1)
    flags = (flags > 0).astype(jnp.int32).reshape(-1)
    return _nsa_attend(p_nsa, qtab, o_c, selb, flags, kv)


def _out_proj_kernel(h_ref, ya_ref, yb_ref, yc_ref, yd_ref, w_ref, g_ref, b_ref, o_ref):
    mix = None
    for idx, y_ref in enumerate((ya_ref, yb_ref, yc_ref, yd_ref)):
        part = _mm(y_ref[...], w_ref[idx * GROUP_W:(idx + 1) * GROUP_W, :])
        mix = part if mix is None else mix + part
    o_ref[...] = _layer_norm(DEEPNORM_ALPHA * h_ref[...] + mix, g_ref[...], b_ref[...])


def _out_proj(h2, ys, w_out, layer, g, b, tm=512):
    T, D = h2.shape
    row = lambda w: pl.BlockSpec((tm, w), lambda i: (i, 0))
    const = lambda shape: pl.BlockSpec(shape, lambda i: (0,) * len(shape))
    return pl.pallas_call(
        _out_proj_kernel, name="out_proj_ln",
        grid=(T // tm,),
        in_specs=[row(D), row(GROUP_W), row(GROUP_W), row(GROUP_W), row(GROUP_W),
                  pl.BlockSpec((None, D, D), lambda i: (layer, 0, 0)), const((1, D)), const((1, D))],
        out_specs=row(D),
        out_shape=jax.ShapeDtypeStruct((T, D), F32),
        compiler_params=_cparams("parallel"),
    )(h2, *ys, w_out, g.reshape(1, D), b.reshape(1, D))


def _mlp_kernel(h_ref, w1_ref, w2_ref, g_ref, b_ref, o_ref, acc_ref):
    f = pl.program_id(1)

    @pl.when(f == 0)
    def _():
        acc_ref[...] = jnp.zeros_like(acc_ref)

    a = jnp.maximum(_mm(h_ref[...], w1_ref[...]), 0.0)
    acc_ref[...] += _mm(a * a, w2_ref[...])

    @pl.when(f == pl.num_programs(1) - 1)
    def _():
        o_ref[...] = _layer_norm(DEEPNORM_ALPHA * h_ref[...] + acc_ref[...], g_ref[...], b_ref[...])


def _mlp(h2, w1, w2, layer, g, b, tm=1024, tf=1024):
    T, D = h2.shape
    F = w1.shape[2]
    return pl.pallas_call(
        _mlp_kernel, name="mlp_ln",
        grid=(T // tm, F // tf),
        in_specs=[pl.BlockSpec((tm, D), lambda i, f: (i, 0)),
                  pl.BlockSpec((None, D, tf), lambda i, f: (layer, 0, f)),
                  pl.BlockSpec((None, tf, D), lambda i, f: (layer, f, 0)),
                  pl.BlockSpec((1, D), lambda i, f: (0, 0)),
                  pl.BlockSpec((1, D), lambda i, f: (0, 0))],
        out_specs=pl.BlockSpec((tm, D), lambda i, f: (i, 0)),
        out_shape=jax.ShapeDtypeStruct((T, D), F32),
        scratch_shapes=[pltpu.VMEM((tm, D), F32)],
        compiler_params=_cparams("parallel", "arbitrary"),
    )(h2, w1, w2, g.reshape(1, D), b.reshape(1, D))


def kernel(x, ln_emb_g, ln_emb_b, w_in, conv_w, conv_b, dt_bias, a_log, d_skip, ssm_norm_g, q_norm_g, w_uq, kv_norm_g, w_ukv, cmp_pe_k, cmp_w1_k, cmp_w2_k, cmp_pe_v, cmp_w1_v, cmp_w2_v, w_out, ln1_g, ln1_b, w_mlp1, w_mlp2, ln2_g, ln2_b):
    B, S, D = x.shape
    assert D == D_MODEL and S % NSA_TILE == 0 and S // NSA_SLC_LEN <= LANE
    T = B * S
    ret_tables = _ret_tables(S)
    mla_tables = _mla_tables(S)
    nsa_tables = _nsa_tables(S)
    h = _entry_ln(x.reshape(T, D), ln_emb_g, ln_emb_b)
    for l in range(w_in.shape[0]):
        p_ssm, p_mla, p_ret, p_nsa, nsa_kv = _in_proj(h, _layout_w_in(w_in[l]), nsa_tables[2])
        y_a = _ssm(p_ssm.reshape(B, S, W_SSM), conv_w[l], conv_b[l], dt_bias[l], a_log[l], d_skip[l], ssm_norm_g[l])
        wq, wk, wv = _layout_mla_weights(w_uq[l], w_ukv[l])
        q, k, v = _mla_prep(p_mla.reshape(B, S, W_MLA), q_norm_g[l], kv_norm_g[l], wq, wk, wv, mla_tables)
        y_b = _mla_attn(q, k, v)
        y_c = _ret(p_ret.reshape(B, S, W_RET), ret_tables)
        y_d = _nsa(p_nsa.reshape(B, S, W_NSA), nsa_kv.reshape(B, S, W_KV), cmp_pe_k[l], cmp_w1_k[l], cmp_w2_k[l],
                   cmp_pe_v[l], cmp_w1_v[l], cmp_w2_v[l], nsa_tables)
        ys = [y.reshape(T, GROUP_W) for y in (y_a, y_b, y_c, y_d)]
        h = _out_proj(h, ys, w_out, l, ln1_g[l], ln1_b[l])
        h = _mlp(h, w_mlp1, w_mlp2, l, ln2_g[l], ln2_b[l])
    return h.reshape(B, S, D)
```

```python
import functools
import math

import jax
import jax.numpy as jnp
import numpy as np
from jax import lax
from jax.experimental import pallas as pl
from jax.experimental.pallas import tpu as pltpu

F32 = jnp.float32
BF16 = jnp.bfloat16
HIGHEST = lax.Precision.HIGHEST

D_MODEL = 1024
DEPTH = 2
GROUP_W = D_MODEL // 4
SSM_HEADS = 4
SSM_HEAD_DIM = GROUP_W // SSM_HEADS
SSM_GROUPS = 2
SSM_STATE = 128
SSM_CONV = 4
SSM_CHUNK = 128
SSM_XBC = GROUP_W + 2 * SSM_GROUPS * SSM_STATE
MLA_HEADS = 4
MLA_NOPE = 64
MLA_ROPE = 32
MLA_V = GROUP_W // MLA_HEADS
MLA_Q_RANK = 256
MLA_KV_RANK = 128
RET_HEADS = 4
RET_DK = 64
RET_DV = GROUP_W // RET_HEADS
RET_CHUNK = 128
NSA_HEADS = 4
NSA_DK = 64
NSA_DV = GROUP_W // NSA_HEADS
NSA_CMP_LEN = 32
NSA_CMP_STRIDE = 16
NSA_CMP_HID = 256
NSA_SLC_LEN = 64
NSA_TOPN = 16
NSA_WIN = 512
D_FF = 4 * D_MODEL
NSA_Q = 256
ROPE_THETA = 10000.0
EPS = 1e-5
NEG_INF = -1e30
LOG2_E = math.log2(math.e)
FORCED_SCORE = 1e9
DEEPNORM_ALPHA = (2.0 * DEPTH) ** 0.25

IN_SPLITS = (
    GROUP_W, SSM_XBC, SSM_HEADS,
    MLA_Q_RANK, MLA_KV_RANK, MLA_ROPE,
    RET_HEADS * RET_DK, RET_HEADS * RET_DK, RET_HEADS * RET_DV, GROUP_W,
    NSA_HEADS * NSA_DK, NSA_DK, NSA_DV, NSA_DK, NSA_DV, NSA_DK, NSA_DV, 3 * NSA_HEADS,
)

LANE = 128
W_SSM = GROUP_W + SSM_XBC + LANE
W_MLA = MLA_Q_RANK + MLA_KV_RANK + LANE
W_RET = 4 * GROUP_W
W_NSA = NSA_HEADS * LANE + LANE + LANE
W_KV = 2 * LANE + 3 * LANE
W_PROJ = W_SSM + W_MLA + W_RET + W_NSA + 4 * NSA_DK

NSA_TILE = 512
MLA_TILE = 1024
MLA_Q_TILE = 1024
VMEM_LIMIT = 48 * 1024 * 1024


def _cparams(*sem):
    return pltpu.CompilerParams(dimension_semantics=sem, vmem_limit_bytes=VMEM_LIMIT)


def _mm(a, b):
    return jnp.dot(a.astype(BF16), b.astype(BF16), preferred_element_type=F32)


def _mm_nt(a, b):
    return lax.dot_general(a.astype(BF16), b.astype(BF16), (((1,), (1,)), ((), ())),
                           preferred_element_type=F32)


def _mm_f32(a, b):
    return jnp.dot(a, b, precision=HIGHEST, preferred_element_type=F32)


def _silu(x):
    return x * jax.nn.sigmoid(x)


def _softplus(x):
    return jnp.maximum(x, 0.0) + jnp.log1p(jnp.exp(-jnp.abs(x)))


def _layer_norm(x, g, b):
    mu = jnp.mean(x, axis=-1, keepdims=True)
    xc = x - mu
    var = jnp.mean(xc * xc, axis=-1, keepdims=True)
    return xc * lax.rsqrt(var + EPS) * g + b


def _iota(shape, dim):
    return lax.broadcasted_iota(jnp.int32, shape, dim)


def _ln_kernel(x_ref, g_ref, b_ref, o_ref):
    o_ref[...] = _layer_norm(x_ref[...], g_ref[...], b_ref[...])


def _entry_ln(x2, g, b, tm=512):
    T, D = x2.shape
    return pl.pallas_call(
        _ln_kernel, name="entry_ln",
        grid=(T // tm,),
        in_specs=[pl.BlockSpec((tm, D), lambda i: (i, 0)),
                  pl.BlockSpec((1, D), lambda i: (0, 0)),
                  pl.BlockSpec((1, D), lambda i: (0, 0))],
        out_specs=pl.BlockSpec((tm, D), lambda i: (i, 0)),
        out_shape=jax.ShapeDtypeStruct((T, D), F32),
        compiler_params=_cparams("parallel"),
    )(x2, g.reshape(1, D), b.reshape(1, D))


def _pad_cols(w, width):
    return jnp.pad(w, ((0, 0), (0, width - w.shape[1])))


def _layout_w_in(w):
    offs = np.concatenate([[0], np.cumsum(IN_SPLITS)])
    p = [w[:, int(offs[i]):int(offs[i + 1])] for i in range(len(IN_SPLITS))]
    (ssm_z, ssm_xbc, ssm_dt, mla_cq, mla_ckv, mla_kr, ret_q, ret_k, ret_v, ret_g,
     nsa_q, nsa_kc, nsa_vc, nsa_ks, nsa_vs, nsa_kw, nsa_vw, nsa_gate) = p
    nsa_q_heads = [_pad_cols(nsa_q[:, h * NSA_DK:(h + 1) * NSA_DK], LANE) for h in range(NSA_HEADS)]
    cols = [ssm_z, ssm_xbc, _pad_cols(ssm_dt, LANE),
            mla_cq, mla_ckv, _pad_cols(mla_kr, LANE),
            ret_q, ret_k, ret_v, ret_g,
            *nsa_q_heads, nsa_kc, nsa_vc, _pad_cols(nsa_gate, LANE),
            nsa_ks, nsa_vs, nsa_kw, nsa_vw]
    out = jnp.concatenate(cols, axis=1)
    assert out.shape[1] == W_PROJ
    return out.astype(BF16)


def _in_proj_kernel(h_ref, w_ref, kvtab_ref, ssm_ref, mla_ref, ret_ref, nsa_ref, kv_ref):
    hb = h_ref[...].astype(BF16)
    off = 0
    for ref, width in ((ssm_ref, W_SSM), (mla_ref, W_MLA), (ret_ref, W_RET), (nsa_ref, W_NSA)):
        ref[...] = jnp.dot(hb, w_ref[:, off:off + width], preferred_element_type=F32)
        off += width
    kv = jnp.dot(hb, w_ref[:, off:off + 4 * NSA_DK], preferred_element_type=F32)
    low = _iota((kv.shape[0], LANE), 1) < NSA_DK
    sel_kv, win_kv = kv[:, :LANE], kv[:, LANE:]
    pieces = {0: sel_kv, 2: pltpu.roll(sel_kv, NSA_DK, 1), 3: win_kv, 4: pltpu.roll(win_kv, NSA_DK, 1)}
    for slab in range(W_KV // LANE):
        lanes = slice(slab * LANE, (slab + 1) * LANE)
        tab = kvtab_ref[:, lanes]
        if slab in pieces:
            kv_ref[:, lanes] = (jnp.where(low, pieces[slab], 0.0) + tab.astype(F32)).astype(BF16)
        else:
            kv_ref[:, lanes] = tab


def _in_proj(h2, w_p, kv_table, tm=256):
    T, D = h2.shape
    S = kv_table.shape[0]
    widths = (W_SSM, W_MLA, W_RET, W_NSA)
    return pl.pallas_call(
        _in_proj_kernel, name="in_proj",
        grid=(T // tm,),
        in_specs=[pl.BlockSpec((tm, D), lambda i: (i, 0)),
                  pl.BlockSpec((D, W_PROJ), lambda i: (0, 0)),
                  pl.BlockSpec((tm, W_KV), lambda i: (i % (S // tm), 0))],
        out_specs=[pl.BlockSpec((tm, w), lambda i: (i, 0)) for w in widths + (W_KV,)],
        out_shape=[jax.ShapeDtypeStruct((T, w), F32) for w in widths] + [jax.ShapeDtypeStruct((T, W_KV), BF16)],
        compiler_params=_cparams("parallel"),
    )(h2, w_p, kv_table)


def _ssm_kernel(p_ref, cw_ref, cb_ref, dtb_ref, alog_ref, dskip_ref, ng_ref, o_ref, state_ref, ext_ref):
    @pl.when(pl.program_id(0) == 0)
    def _():
        state_ref[...] = jnp.zeros_like(state_ref)
        ext_ref[:, 0:8, :] = jnp.zeros((ext_ref.shape[0], 8, SSM_XBC), F32)

    for b in range(p_ref.shape[0]):
        _ssm_chunk(p_ref.at[b], cw_ref, cb_ref, dtb_ref, alog_ref, dskip_ref, ng_ref,
                   o_ref.at[b], state_ref.at[b], ext_ref.at[b])


def _ssm_chunk(p_ref, cw_ref, cb_ref, dtb_ref, alog_ref, dskip_ref, ng_ref, o_ref, state_ref, ext_ref):
    L, H, P, N = SSM_CHUNK, SSM_HEADS, SSM_HEAD_DIM, SSM_STATE
    z = p_ref[:, 0:GROUP_W]
    ext_ref[8:8 + L, :] = p_ref[:, GROUP_W:GROUP_W + SSM_XBC]
    conv = cb_ref[...]
    for j in range(SSM_CONV):
        conv = conv + ext_ref[pl.ds(8 - (SSM_CONV - 1) + j, L), :] * cw_ref[j:j + 1, :]
    ext_ref[0:8, :] = ext_ref[L:L + 8, :]
    xbc = _silu(conv)
    xs = xbc[:, 0:GROUP_W]
    b_in = xbc[:, GROUP_W:GROUP_W + SSM_GROUPS * N]
    c_in = xbc[:, GROUP_W + SSM_GROUPS * N:]

    dt = _softplus(p_ref[:, GROUP_W + SSM_XBC:] + dtb_ref[...])
    a = dt * (-jnp.exp(alog_ref[...]))
    row = _iota((L, L), 0)
    col = _iota((L, L), 1)
    tril = col <= row
    cs = _mm_f32(jnp.where(tril, 1.0, 0.0), a)
    cs_t = cs.T
    ecs = jnp.exp(cs)
    dte = jnp.exp(cs[L - 1:L, :] - cs)
    expand = jnp.where(_iota((LANE, H * P), 0) == _iota((LANE, H * P), 1) // P, 1.0, 0.0)
    dt_x = _mm_f32(dt, expand)
    ecs_x = _mm_f32(ecs, expand)
    dte_x = _mm_f32(dte, expand)

    xdt = xs * dt_x
    wx = xdt * dte_x
    head_of_lane = _iota((L, H * P), 1) // P
    y = xs * dskip_ref[...]
    y_off = []
    rep = H // SSM_GROUPS
    for g in range(SSM_GROUPS):
        cg = c_in[:, g * N:(g + 1) * N]
        bg = b_in[:, g * N:(g + 1) * N]
        cb = _mm_nt(cg, bg)
        for h in range(g * rep, (g + 1) * rep):
            diff = cs[:, h:h + 1] - cs_t[h:h + 1, :]
            seg = jnp.where(tril, jnp.exp(jnp.where(tril, diff, 0.0)), 0.0)
            yh = _mm(cb * seg, xdt)
            y = y + jnp.where(head_of_lane == h, yh, 0.0)
        lanes = slice(g * rep * P, (g + 1) * rep * P)
        st_prev = state_ref[:, lanes]
        y_off.append(_mm(cg, st_prev))
        state_ref[:, lanes] = st_prev * ecs_x[L - 1:L, lanes] + _mm(bg.T, wx[:, lanes])
    y = y + jnp.concatenate(y_off, axis=1) * ecs_x
    y = y * _silu(z)
    ms = jnp.mean(y * y, axis=-1, keepdims=True)
    o_ref[...] = (y * lax.rsqrt(ms + EPS) * ng_ref[...]).astype(o_ref.dtype)


def _ssm(p_ssm, conv_w, conv_b, dt_bias, a_log, d_skip, norm_g):
    B, S, _ = p_ssm.shape
    L = SSM_CHUNK
    pad_h = lambda v: jnp.pad(v, (0, LANE - SSM_HEADS)).reshape(1, LANE)
    const = lambda shape: pl.BlockSpec(shape, lambda c: (0,) * len(shape))
    return pl.pallas_call(
        _ssm_kernel, name="ssm",
        grid=(S // L,),
        in_specs=[pl.BlockSpec((B, L, W_SSM), lambda c: (0, c, 0)),
                  const((SSM_CONV, SSM_XBC)), const((1, SSM_XBC)), const((1, LANE)), const((1, LANE)),
                  const((1, GROUP_W)), const((1, GROUP_W))],
        out_specs=pl.BlockSpec((B, L, GROUP_W), lambda c: (0, c, 0)),
        out_shape=jax.ShapeDtypeStruct((B, S, GROUP_W), BF16),
        scratch_shapes=[pltpu.VMEM((B, SSM_STATE, GROUP_W), F32),
                        pltpu.VMEM((B, L + 8, SSM_XBC), F32)],
        compiler_params=_cparams("arbitrary"),
    )(p_ssm, conv_w, conv_b.reshape(1, -1), pad_h(dt_bias), pad_h(a_log),
      jnp.repeat(d_skip, SSM_HEAD_DIM).reshape(1, GROUP_W), norm_g.reshape(1, GROUP_W))


def _ret_kernel(p_ref, cos_ref, sin_ref, dec_ref, zeta_ref, xi_ref, cd_ref, o_ref, state_ref):
    @pl.when(pl.program_id(0) == 0)
    def _():
        state_ref[...] = jnp.zeros_like(state_ref)

    for b in range(p_ref.shape[0]):
        _ret_chunk(p_ref.at[b], cos_ref, sin_ref, dec_ref, zeta_ref, xi_ref, cd_ref, o_ref.at[b], state_ref.at[b])


def _ret_chunk(p_ref, cos_ref, sin_ref, dec_ref, zeta_ref, xi_ref, cd_ref, o_ref, state_ref):
    L, H, DK, DV = RET_CHUNK, RET_HEADS, RET_DK, RET_DV
    W = H * DK
    q = p_ref[:, 0:W]
    k = p_ref[:, W:2 * W]
    v = p_ref[:, 2 * W:3 * W]
    gate = p_ref[:, 3 * W:4 * W]
    lane = _iota((L, W), 1)
    first_half = (lane % DK) < (DK // 2)
    head_of_lane = lane // DK

    def rope(x):
        partner = jnp.where(first_half, pltpu.roll(x, W - DK // 2, 1), pltpu.roll(x, DK // 2, 1))
        return x * cos_ref[...] + partner * sin_ref[...]

    qr = rope(q)
    kr = rope(k) * (DK ** -0.5)
    y = jnp.zeros((L, H * DV), F32)
    for h in range(H):
        qh = jnp.where(head_of_lane == h, qr, 0.0)
        sc = _mm_nt(qh, kr) * dec_ref[h]
        y = y + jnp.where(head_of_lane == h, _mm(sc, v), 0.0)
    st = state_ref[...]
    y = y + _mm(qr * xi_ref[...], st)
    same_head = (_iota((W, H * DV), 0) // DK) == (_iota((W, H * DV), 1) // DV)
    kv = _mm((kr * zeta_ref[...]).T, v)
    state_ref[...] = st * cd_ref[...] + jnp.where(same_head, kv, 0.0)
    ms = _mm_f32(y * y, jnp.where(same_head, 1.0 / DV, 0.0))
    o_ref[...] = (y * lax.rsqrt(ms + EPS) * _silu(gate)).astype(o_ref.dtype)


def _ret_tables(S):
    H, DK, L = RET_HEADS, RET_DK, RET_CHUNK
    inv = ROPE_THETA ** (-np.arange(0, DK, 2, dtype=np.float64) / DK)
    ang = np.arange(S, dtype=np.float64)[:, None] * inv[None, :]
    cos, sin = np.cos(ang), np.sin(ang)
    cos_t = np.tile(np.concatenate([cos, cos], axis=1), (1, H))
    sin_t = np.tile(np.concatenate([-sin, sin], axis=1), (1, H))
    log_gamma = np.log1p(-np.exp2(-5.0 - np.arange(H, dtype=np.float64)))
    pos = np.arange(L, dtype=np.float64)
    diff = pos[:, None] - pos[None, :]
    decay_in = np.where(diff >= 0, np.exp(np.maximum(diff, 0.0)[None] * log_gamma[:, None, None]), 0.0)
    zeta = np.exp((L - 1 - pos)[None] * log_gamma[:, None])
    xi = np.exp((pos + 1.0)[None] * log_gamma[:, None])
    chunk_decay = np.exp(L * log_gamma)
    zeta_x = np.repeat(zeta.T, DK, axis=1)
    xi_x = np.repeat(xi.T, DK, axis=1)
    cd_x = np.repeat(chunk_decay, RET_DV).reshape(1, H * RET_DV)
    return tuple(jnp.asarray(t, dtype=F32) for t in (cos_t, sin_t, decay_in, zeta_x, xi_x, cd_x))


def _ret(p_ret, tables):
    B, S, _ = p_ret.shape
    L, H = RET_CHUNK, RET_HEADS
    W = H * RET_DK
    cos_t, sin_t, decay_in, zeta_x, xi_x, cd_x = tables
    const = lambda shape: pl.BlockSpec(shape, lambda c: (0,) * len(shape))
    return pl.pallas_call(
        _ret_kernel, name="retention",
        grid=(S // L,),
        in_specs=[pl.BlockSpec((B, L, W_RET), lambda c: (0, c, 0)),
                  pl.BlockSpec((L, W), lambda c: (c, 0)),
                  pl.BlockSpec((L, W), lambda c: (c, 0)),
                  const((H, L, L)), const((L, W)), const((L, W)), const((1, H * RET_DV))],
        out_specs=pl.BlockSpec((B, L, GROUP_W), lambda c: (0, c, 0)),
        out_shape=jax.ShapeDtypeStruct((B, S, GROUP_W), BF16),
        scratch_shapes=[pltpu.VMEM((B, W, H * RET_DV), F32)],
        compiler_params=_cparams("arbitrary"),
    )(p_ret, cos_t, sin_t, decay_in, zeta_x, xi_x, cd_x)


def _mla_prep_kernel(p_ref, qg_ref, wq_ref, kvg_ref, wk_ref, wv_ref, cos_ref, sin_ref,
                     q_ref, k_ref, v_ref):
    tm = p_ref.shape[0]
    cq = p_ref[:, 0:MLA_Q_RANK]
    ckv = p_ref[:, MLA_Q_RANK:MLA_Q_RANK + MLA_KV_RANK]
    kr = p_ref[:, MLA_Q_RANK + MLA_KV_RANK:]

    def rms(x, g):
        return x * lax.rsqrt(jnp.mean(x * x, axis=-1, keepdims=True) + EPS) * g

    q = _mm(rms(cq, qg_ref[...]), wq_ref[...])
    kvl = rms(ckv, kvg_ref[...])
    kn = _mm(kvl, wk_ref[...])
    vv = _mm(kvl, wv_ref[...])
    kr_sh = pltpu.roll(kr, MLA_NOPE, 1)
    lane = _iota((tm, LANE), 1)
    half = MLA_ROPE // 2
    low = (lane >= MLA_NOPE) & (lane < MLA_NOPE + half)
    cos = cos_ref[...]
    sin = sin_ref[...]

    def rope(x):
        partner = jnp.where(low, pltpu.roll(x, LANE - half, 1), pltpu.roll(x, half, 1))
        return x * cos + partner * sin

    scale = (MLA_NOPE + MLA_ROPE) ** -0.5 * LOG2_E
    for h in range(MLA_HEADS):
        sl = slice(h * LANE, (h + 1) * LANE)
        q_ref[h] = (rope(q[:, sl]) * scale).astype(BF16)
        k_ref[h] = rope(kn[:, sl] + kr_sh).astype(BF16)
        v_ref[h] = jnp.where(lane == _mla_ones_lane(h), 1.0, vv[:, sl]).astype(BF16)


def _mla_tables(S):
    inv = ROPE_THETA ** (-np.arange(0, MLA_ROPE, 2, dtype=np.float64) / MLA_ROPE)
    ang = np.arange(S, dtype=np.float64)[:, None] * inv[None, :]
    cos, sin = np.cos(ang), np.sin(ang)
    tail = LANE - MLA_NOPE - MLA_ROPE
    cos_t = np.concatenate([np.ones((S, MLA_NOPE)), cos, cos, np.ones((S, tail))], axis=1)
    sin_t = np.concatenate([np.zeros((S, MLA_NOPE)), -sin, sin, np.zeros((S, tail))], axis=1)
    return jnp.asarray(cos_t, dtype=F32), jnp.asarray(sin_t, dtype=F32)


def _layout_mla_weights(w_uq, w_ukv):
    H = MLA_HEADS
    dq = MLA_NOPE + MLA_ROPE
    wq = jnp.concatenate([_pad_cols(w_uq[:, h * dq:(h + 1) * dq], LANE) for h in range(H)], axis=1)
    dkv = MLA_NOPE + MLA_V
    wk, wv = [], []
    for h in range(H):
        blk = w_ukv[:, h * dkv:(h + 1) * dkv]
        wk.append(_pad_cols(blk[:, :MLA_NOPE], LANE))
        v = blk[:, MLA_NOPE:]
        zero = jnp.zeros_like(v)
        wv.append(jnp.concatenate([v, zero] if h % 2 == 0 else [zero, v], axis=1))
    return wq.astype(BF16), jnp.concatenate(wk, axis=1).astype(BF16), jnp.concatenate(wv, axis=1).astype(BF16)


def _mla_prep(p_mla, q_norm_g, kv_norm_g, wq, wk, wv, tables, tm=512):
    B, S, _ = p_mla.shape
    H = MLA_HEADS
    cos_t, sin_t = tables
    const = lambda shape: pl.BlockSpec(shape, lambda b, i: (0,) * len(shape))
    qkv_spec = pl.BlockSpec((None, H, tm, LANE), lambda b, i: (b, 0, i, 0))
    qkv_shape = jax.ShapeDtypeStruct((B, H, S, LANE), BF16)
    return pl.pallas_call(
        _mla_prep_kernel, name="mla_prep",
        grid=(B, S // tm),
        in_specs=[pl.BlockSpec((None, tm, W_MLA), lambda b, i: (b, i, 0)),
                  const((1, MLA_Q_RANK)), const((MLA_Q_RANK, H * LANE)),
                  const((1, MLA_KV_RANK)), const((MLA_KV_RANK, H * LANE)), const((MLA_KV_RANK, H * LANE)),
                  pl.BlockSpec((tm, LANE), lambda b, i: (i, 0)),
                  pl.BlockSpec((tm, LANE), lambda b, i: (i, 0))],
        out_specs=[qkv_spec, qkv_spec, qkv_spec],
        out_shape=[qkv_shape, qkv_shape, qkv_shape],
        compiler_params=_cparams("parallel", "parallel"),
    )(p_mla, q_norm_g.reshape(1, -1), wq, kv_norm_g.reshape(1, -1), wk, wv, cos_t, sin_t)


def _mla_ones_lane(h):
    return MLA_V if h % 2 == 0 else 0


def _mla_attn_kernel(qi_ref, kj_ref, q_ref, k_ref, v_ref, o_ref, m_ref, acc_ref):
    H = MLA_HEADS
    tq, tk = q_ref.shape[1], k_ref.shape[1]
    i = qi_ref[pl.program_id(1)]
    j = kj_ref[pl.program_id(1)]

    @pl.when(j == 0)
    def _():
        m_ref[...] = jnp.full(m_ref.shape, NEG_INF, F32)
        acc_ref[...] = jnp.zeros_like(acc_ref)

    def sweep(masked):
        if masked:
            causal = (_iota((tq, tk), 1) - _iota((tq, tk), 0)) <= i * tq - j * tk
        scores = [lax.dot_general(q_ref[h], k_ref[h], (((1,), (1,)), ((), ())), preferred_element_type=F32)
                  for h in range(H)]
        for h in range(H):
            s = scores[h]
            if masked:
                s = jnp.where(causal, s, NEG_INF)
            m_prev = m_ref[h]
            m_new = jnp.maximum(m_prev, jnp.max(s, axis=-1, keepdims=True))
            p = jnp.exp2(s - jnp.tile(m_new, (1, tk // LANE)))
            acc_ref[h] = (jnp.exp2(m_prev - m_new) * acc_ref[h]
                          + jnp.dot(p.astype(BF16), v_ref[h], preferred_element_type=F32))
            m_ref[h] = m_new

    ratio = tq // tk

    @pl.when(j < ratio * i)
    def _():
        sweep(False)

    @pl.when(j >= ratio * i)
    def _():
        sweep(True)

    @pl.when(j == ratio * i + ratio - 1)
    def _():
        lane = _iota((tq, LANE), 1)
        for pair in range(H // 2):
            he, ho = 2 * pair, 2 * pair + 1
            acc_e, acc_o = acc_ref[he], acc_ref[ho]
            le = acc_e[:, _mla_ones_lane(he):_mla_ones_lane(he) + 1]
            lo = acc_o[:, _mla_ones_lane(ho):_mla_ones_lane(ho) + 1]
            o_ref[:, pair * LANE:(pair + 1) * LANE] = jnp.where(lane < MLA_V, acc_e / le, acc_o / lo).astype(o_ref.dtype)


def _mla_attn(q, k, v):
    B, H, S, _ = q.shape
    t = min(MLA_TILE, S)
    tq = min(MLA_Q_TILE, S)
    pairs = [(i, j) for i in range(S // tq) for j in range((i + 1) * tq // t)]
    qi = jnp.asarray([p[0] for p in pairs], jnp.int32)
    kj = jnp.asarray([p[1] for p in pairs], jnp.int32)
    grid_spec = pltpu.PrefetchScalarGridSpec(
        num_scalar_prefetch=2,
        grid=(B, len(pairs)),
        in_specs=[pl.BlockSpec((None, H, tq, LANE), lambda b, p, qi, kj: (b, 0, qi[p], 0)),
                  pl.BlockSpec((None, H, t, LANE), lambda b, p, qi, kj: (b, 0, kj[p], 0)),
                  pl.BlockSpec((None, H, t, LANE), lambda b, p, qi, kj: (b, 0, kj[p], 0))],
        out_specs=pl.BlockSpec((None, tq, GROUP_W), lambda b, p, qi, kj: (b, qi[p], 0)),
        scratch_shapes=[pltpu.VMEM((H, tq, LANE), F32), pltpu.VMEM((H, tq, LANE), F32)],
    )
    return pl.pallas_call(
        _mla_attn_kernel, name="mla_attn",
        grid_spec=grid_spec,
        out_shape=jax.ShapeDtypeStruct((B, S, GROUP_W), BF16),
        compiler_params=_cparams("parallel", "arbitrary"),
    )(qi, kj, q, k, v)


POS_HI = NSA_DK
POS_LO = NSA_DK + 3
POS_ONE = NSA_DK + 6
ONES_LANE = NSA_DV


def _split_bf16(x, parts=3):
    out, rem = [], np.float64(x)
    for _ in range(parts):
        piece = np.float64(np.float32(rem).astype(jnp.bfloat16).astype(np.float32))
        out.append(float(piece))
        rem = rem - piece
    return out


def _nsa_query_table():
    H = NSA_HEADS
    tab = np.zeros((2 * H, LANE), np.float32)
    for h in range(H):
        c = 2.0 ** (-8.0 * (h + 1) / H) * LOG2_E
        pieces = _split_bf16(c)
        tab[h, POS_HI:POS_HI + 3] = pieces
        tab[h, POS_LO:POS_LO + 3] = pieces
        tab[H + h, POS_ONE] = -sum(pieces)
    return jnp.asarray(tab)


def _nsa_pos_lanes(pos, lo_offset=0.0):
    t = np.zeros((len(pos), LANE - NSA_DK), np.float32)
    t[:, POS_HI - NSA_DK:POS_HI - NSA_DK + 3] = (NSA_SLC_LEN * (pos // NSA_SLC_LEN))[:, None]
    t[:, POS_LO - NSA_DK:POS_LO - NSA_DK + 3] = (pos % NSA_SLC_LEN + lo_offset)[:, None]
    t[:, POS_ONE - NSA_DK] = 1.0
    return t


def _nsa_queries(q_ref, qtab_ref, qb):
    Q, H = q_ref.shape[0], NSA_HEADS
    qpos = (qb * Q + _iota((Q, 1), 0)).astype(F32)
    out = []
    for h in range(H):
        q = q_ref[:, h * LANE:(h + 1) * LANE] * (NSA_DK ** -0.5 * LOG2_E)
        out.append((q + qtab_ref[h:h + 1, :] + qtab_ref[H + h:H + h + 1, :] * qpos).astype(BF16))
    return out


def _normalise(o):
    return o / o[:, ONES_LANE:ONES_LANE + 1]


def _unstack_heads(o):
    Q = o.shape[0] // NSA_HEADS
    lane = _iota((Q, LANE), 1)
    out = []
    for pair in range(NSA_HEADS // 2):
        even = o[(2 * pair) * Q:(2 * pair + 1) * Q]
        odd = o[(2 * pair + 1) * Q:(2 * pair + 2) * Q]
        out.append(jnp.where(lane < NSA_DV, even, pltpu.roll(odd, NSA_DV, 1)))
    return jnp.concatenate(out, axis=1)


def _nsa_cmp_kernel(uk_ref, uv_ref, pek_ref, pev_ref, w1k_ref, w1v_ref, w2k_ref, w2v_ref, cpos_ref,
                    kc_ref, vc_ref, sh_ref):
    nb = uk_ref.shape[0]
    half = uk_ref.shape[1]

    def hidden(u_ref, pe_ref, w1_ref):
        u = u_ref[...]
        first = _mm(u + pe_ref[0:1, :], w1_ref[0:half, :])
        second = _mm(u + pe_ref[1:2, :], w1_ref[half:2 * half, :])
        sh_ref[0:nb, :] = second
        sh_ref[nb:nb + 8, :] = jnp.zeros((8, NSA_CMP_HID), F32)
        return first + sh_ref[pl.ds(1, nb), :]

    hk = _silu(hidden(uk_ref, pek_ref, w1k_ref))
    hv = _silu(hidden(uv_ref, pev_ref, w1v_ref))
    kc_ref[...] = (_mm(hk, w2k_ref[...]) + cpos_ref[...]).astype(BF16)
    ones_lane = jnp.where(_iota((1, LANE), 1) == ONES_LANE, 1.0, 0.0)
    vc_ref[...] = (_mm(hv, w2v_ref[...]) + ones_lane).astype(BF16)


def _nsa_compress(uk, uv, pe_k, w1_k, w2_k, pe_v, w1_v, w2_v):
    B, nb, half = uk.shape
    hid = NSA_CMP_HID
    const = lambda shape: pl.BlockSpec(shape, lambda b: (0,) * len(shape))
    w2k = _pad_cols(w2_k, LANE).astype(BF16)
    w2v = _pad_cols(w2_v, LANE).astype(BF16)
    centre = _nsa_pos_lanes(np.arange(nb) * NSA_CMP_STRIDE, 0.5 * (NSA_CMP_LEN - 1))
    cpos = jnp.asarray(np.concatenate([np.zeros((nb, NSA_DK), np.float32), centre], axis=1))
    out_spec = pl.BlockSpec((None, nb, LANE), lambda b: (b, 0, 0))
    out_shape = jax.ShapeDtypeStruct((B, nb, LANE), BF16)
    return pl.pallas_call(
        _nsa_cmp_kernel, name="nsa_compress",
        grid=(B,),
        in_specs=[pl.BlockSpec((None, nb, half), lambda b: (b, 0, 0)),
                  pl.BlockSpec((None, nb, half), lambda b: (b, 0, 0)),
                  const((2, half)), const((2, half)),
                  const((2 * half, hid)), const((2 * half, hid)),
                  const((hid, LANE)), const((hid, LANE)), const((nb, LANE))],
        out_specs=[out_spec, out_spec],
        out_shape=[out_shape, out_shape],
        scratch_shapes=[pltpu.VMEM((nb + 8, hid), F32)],
        compiler_params=_cparams("parallel"),
    )(uk, uv, pe_k.reshape(2, half), pe_v.reshape(2, half), w1_k.astype(BF16), w1_v.astype(BF16), w2k, w2v, cpos)


def _nsa_sel_kernel(q_ref, qtab_ref, kc_ref, vc_ref, ovt_ref, oc_ref, selb_ref, any_ref, imp_ref, *, n_slc, top_n):
    Q, H = q_ref.shape[0], NSA_HEADS
    qb = pl.program_id(1)
    nc = kc_ref.shape[0]
    qs = jnp.concatenate(_nsa_queries(q_ref, qtab_ref, qb), axis=0)
    nt_dims = (((1,), (1,)), ((), ()))

    def attend(ncols):
        s = lax.dot_general(qs, kc_ref[0:ncols, :], nt_dims, preferred_element_type=F32)
        qpos = qb * Q + (_iota((H * Q, ncols), 0) & (Q - 1))
        block_end = _iota((H * Q, ncols), 1) * NSA_CMP_STRIDE + (NSA_CMP_LEN - 1)
        s = jnp.where(block_end <= qpos, s, NEG_INF)
        e = jnp.exp2(s - jnp.max(s, axis=-1, keepdims=True))
        qpos_col = qb * Q + (_iota((H * Q, 1), 0) & (Q - 1))
        has_block = jnp.where(qpos_col >= NSA_CMP_LEN - 1, 1.0, 0.0)
        p = e * (has_block / jnp.sum(e, axis=-1, keepdims=True))
        oc_ref[...] = _unstack_heads(jnp.dot(p.astype(BF16), vc_ref[0:ncols, :], preferred_element_type=F32))
        p_sum = p[0:Q]
        for h in range(1, H):
            p_sum = p_sum + p[h * Q:(h + 1) * Q]
        imp_ref[...] = lax.dot_general(ovt_ref[:, 0:ncols], p_sum, nt_dims, precision=HIGHEST,
                                       preferred_element_type=F32)

    tiles_needed = ((qb + 1) * Q // NSA_CMP_STRIDE + LANE - 1) // LANE
    for tiles in range(1, nc // LANE + 1):
        pl.when(tiles_needed == tiles)(functools.partial(attend, tiles * LANE))

    imp = imp_ref[...]
    blk = _iota((LANE, Q), 0)
    q_blk = (qb * Q + _iota((LANE, Q), 1)) >> int(math.log2(NSA_SLC_LEN))
    causal = blk <= q_blk
    for forced_blk in (0, q_blk, q_blk - 1):
        imp = jnp.where(blk == forced_blk, FORCED_SCORE, imp)
    imp = jnp.where(causal, imp, -1.0)
    imp = jnp.where(blk < n_slc, imp, -2.0)
    blk_f = blk.astype(F32)
    sel = jnp.zeros((LANE, Q), F32)
    for _ in range(top_n):
        m = jnp.max(imp, axis=0, keepdims=True)
        first = jnp.min(jnp.where(imp == m, blk_f, float(LANE)), axis=0, keepdims=True)
        hit = blk_f == first
        sel = jnp.where(hit, 1.0, sel)
        imp = jnp.where(hit, -3.0, imp)
    sel = jnp.where(causal, sel, 0.0).T
    selb_ref[...] = jnp.where(sel > 0.5, 0.0, NEG_INF).astype(BF16)
    any_ref[...] = jnp.max(sel, axis=0, keepdims=True)


def _nsa_select(p_nsa, qtab, kc, vc, overlap_t):
    B, S, _ = p_nsa.shape
    Q = NSA_Q
    nqb = S // Q
    nc = kc.shape[1]
    n_slc = S // NSA_SLC_LEN
    kern = functools.partial(_nsa_sel_kernel, n_slc=n_slc, top_n=min(NSA_TOPN, n_slc))
    return pl.pallas_call(
        kern, name="nsa_select",
        grid=(B, nqb),
        in_specs=[pl.BlockSpec((None, Q, NSA_HEADS * LANE), lambda b, i: (b, i, 0)),
                  pl.BlockSpec((2 * NSA_HEADS, LANE), lambda b, i: (0, 0)),
                  pl.BlockSpec((None, nc, LANE), lambda b, i: (b, 0, 0)),
                  pl.BlockSpec((None, nc, LANE), lambda b, i: (b, 0, 0)),
                  pl.BlockSpec((LANE, nc), lambda b, i: (0, 0))],
        out_specs=[pl.BlockSpec((None, Q, GROUP_W), lambda b, i: (b, i, 0)),
                   pl.BlockSpec((None, Q, LANE), lambda b, i: (b, i, 0)),
                   pl.BlockSpec((None, None, 1, LANE), lambda b, i: (b, i, 0, 0))],
        out_shape=[jax.ShapeDtypeStruct((B, S, GROUP_W), F32),
                   jax.ShapeDtypeStruct((B, S, LANE), BF16),
                   jax.ShapeDtypeStruct((B, nqb, 1, LANE), F32)],
        scratch_shapes=[pltpu.VMEM((LANE, Q), F32)],
        compiler_params=_cparams("parallel", "parallel"),
    )(p_nsa, qtab, kc, vc, overlap_t)


def _nsa_attn_kernel(flags_ref, q_ref, qtab_ref, gate_ref, oc_ref, selb_ref, ks_ref, vs_ref, kw_ref, vw_ref,
                     o_ref, m_ref, acc_ref, *, nt):
    Q, H, TK = q_ref.shape[0], NSA_HEADS, NSA_TILE
    PART = 2 * Q
    b = pl.program_id(0)
    qb = pl.program_id(1)
    nqb = pl.num_programs(1)
    qh = _nsa_queries(q_ref, qtab_ref, qb)
    nt_dims = (((1,), (1,)), ((), ()))

    selb = selb_ref[...]
    qs_sel = jnp.concatenate([jnp.concatenate([q, selb], axis=1) for q in qh], axis=0)
    m_ref[...] = jnp.full(m_ref.shape, NEG_INF, F32)
    acc_ref[...] = jnp.zeros_like(acc_ref)

    def update(t, diagonal):
        rows = pl.ds(pl.multiple_of(t * TK, TK), TK)
        k_tile, v_tile = ks_ref[rows, :], vs_ref[rows, :]
        parts = [slice(i * PART, (i + 1) * PART) for i in range(H * Q // PART)]
        scores = [lax.dot_general(qs_sel[r], k_tile, nt_dims, preferred_element_type=F32) for r in parts]
        for r, s in zip(parts, scores):
            if diagonal:
                ahead = _iota((PART, TK), 1) - (_iota((PART, TK), 0) & (Q - 1))
                s = jnp.where(ahead <= qb * Q - t * TK, s, NEG_INF)
            m_prev = m_ref[r]
            m_new = jnp.maximum(m_prev, jnp.max(s, axis=-1, keepdims=True))
            p = jnp.exp2(s - jnp.tile(m_new, (1, TK // LANE)))
            acc_ref[r] = (jnp.exp2(m_prev - m_new) * acc_ref[r]
                          + jnp.dot(p.astype(BF16), v_tile, preferred_element_type=F32))
            m_ref[r] = m_new

    def tile(t, carry):
        @pl.when(flags_ref[(b * nqb + qb) * nt + t] > 0)
        def _():
            update(t, False)
        return carry

    t_diag = (qb * Q) // TK
    lax.fori_loop(0, t_diag, tile, 0)
    update(t_diag, True)
    o_s = _unstack_heads(_normalise(acc_ref[...]))

    W = Q + NSA_WIN
    start = jnp.maximum(qb * Q - NSA_WIN, 0)
    rows_w = pl.ds(pl.multiple_of(start, Q), W)
    qs_win = jnp.concatenate(qh, axis=0)
    s = lax.dot_general(qs_win, kw_ref[rows_w, :], nt_dims, preferred_element_type=F32)
    lead = qb * Q - start
    q_minus_k = (_iota((H * Q, W), 0) & (Q - 1)) - _iota((H * Q, W), 1)
    s = jnp.where(q_minus_k >= -lead, jnp.where(q_minus_k < NSA_WIN - lead, s, NEG_INF), NEG_INF)
    p = jnp.exp2(s - jnp.max(s, axis=-1, keepdims=True))
    o_w = _unstack_heads(_normalise(jnp.dot(p.astype(BF16), vw_ref[rows_w, :], preferred_element_type=F32)))

    gates = jax.nn.sigmoid(gate_ref[...])
    lane_head = _iota((LANE, GROUP_W), 1) // NSA_DV
    src = _iota((LANE, GROUP_W), 0)
    out = None
    for j, branch in enumerate((oc_ref[...], o_s, o_w)):
        g = _mm_f32(gates, jnp.where(src == lane_head * 3 + j, 1.0, 0.0))
        out = g * branch if out is None else out + g * branch
    o_ref[...] = out.astype(o_ref.dtype)


def _nsa_attend(p_nsa, qtab, o_c, selb, flags, kv):
    B, S, _ = p_nsa.shape
    Q = NSA_Q
    nqb = S // Q
    nt = S // NSA_TILE
    gate_blk = (W_NSA - LANE) // LANE
    kern = functools.partial(_nsa_attn_kernel, nt=nt)
    slab = lambda width, col: pl.BlockSpec((None, S, width), lambda b, i, f: (b, 0, col))
    grid_spec = pltpu.PrefetchScalarGridSpec(
        num_scalar_prefetch=1,
        grid=(B, nqb),
        in_specs=[pl.BlockSpec((None, Q, NSA_HEADS * LANE), lambda b, i, f: (b, i, 0)),
                  pl.BlockSpec((2 * NSA_HEADS, LANE), lambda b, i, f: (0, 0)),
                  pl.BlockSpec((None, Q, LANE), lambda b, i, f: (b, i, gate_blk)),
                  pl.BlockSpec((None, Q, GROUP_W), lambda b, i, f: (b, i, 0)),
                  pl.BlockSpec((None, Q, LANE), lambda b, i, f: (b, i, 0)),
                  slab(2 * LANE, 0), slab(LANE, 2), slab(LANE, 3), slab(LANE, 4)],
        out_specs=pl.BlockSpec((None, Q, GROUP_W), lambda b, i, f: (b, i, 0)),
        scratch_shapes=[pltpu.VMEM((NSA_HEADS * Q, LANE), F32), pltpu.VMEM((NSA_HEADS * Q, LANE), F32)],
    )
    return pl.pallas_call(
        kern, name="nsa_attend",
        grid_spec=grid_spec,
        out_shape=jax.ShapeDtypeStruct((B, S, GROUP_W), BF16),
        compiler_params=_cparams("parallel", "parallel"),
    )(flags, p_nsa, qtab, p_nsa, o_c, selb, kv, kv, kv, kv)


def _nsa_tables(S):
    nc = S // NSA_CMP_STRIDE
    n = np.arange(nc)[None, :]
    j = np.arange(LANE)[:, None]
    start = n * NSA_CMP_STRIDE
    ov = (start < (j + 1) * NSA_SLC_LEN) & (start + NSA_CMP_LEN - 1 >= j * NSA_SLC_LEN)
    ov &= (n < (S - NSA_CMP_LEN) // NSA_CMP_STRIDE + 1) & (j < S // NSA_SLC_LEN)
    pos = np.arange(S)
    k_zero = np.zeros((S, NSA_DK), np.float32)
    block_onehot = (pos[:, None] // NSA_SLC_LEN == np.arange(LANE)[None, :]).astype(np.float32)
    v_lanes = np.zeros((S, LANE), np.float32)
    v_lanes[:, ONES_LANE] = 1.0
    kv_table = np.concatenate([k_zero, _nsa_pos_lanes(pos), block_onehot, v_lanes,
                               k_zero, _nsa_pos_lanes(pos), v_lanes], axis=1)
    assert kv_table.shape[1] == W_KV
    return _nsa_query_table(), jnp.asarray(ov.astype(np.float32)), jnp.asarray(kv_table, dtype=BF16)


def _nsa(p_nsa, kv, pe_k, w1_k, w2_k, pe_v, w1_v, w2_v, tables):
    B, S, _ = p_nsa.shape
    qtab, overlap_t, _ = tables
    q_w = NSA_HEADS * LANE
    half = NSA_CMP_STRIDE * NSA_DK
    piece = lambda idx: p_nsa[:, :, q_w + idx * NSA_DK:q_w + (idx + 1) * NSA_DK]
    uk = piece(0).reshape(B, S // NSA_CMP_STRIDE, half)
    uv = piece(1).reshape(B, S // NSA_CMP_STRIDE, half)
    kc, vc = _nsa_compress(uk, uv, pe_k, w1_k, w2_k, pe_v, w1_v, w2_v)
    o_c, selb, blk_any = _nsa_select(p_nsa, qtab, kc, vc, overlap_t)
    per_tile = NSA_TILE // NSA_SLC_LEN
    nt = S // NSA_TILE
    flags = blk_any[:, :, 0, :nt * per_tile].reshape(B, S // NSA_Q, nt, per_tile).max(axis=-1)
    flags = (flags > 0).astype(jnp.int32).reshape(-1)
    return _nsa_attend(p_nsa, qtab, o_c, selb, flags, kv)


def _out_proj_kernel(h_ref, ya_ref, yb_ref, yc_ref, yd_ref, w_ref, g_ref, b_ref, o_ref):
    mix = None
    for idx, y_ref in enumerate((ya_ref, yb_ref, yc_ref, yd_ref)):
        part = _mm(y_ref[...], w_ref[idx * GROUP_W:(idx + 1) * GROUP_W, :])
        mix = part if mix is None else mix + part
    o_ref[...] = _layer_norm(DEEPNORM_ALPHA * h_ref[...] + mix, g_ref[...], b_ref[...])


def _out_proj(h2, ys, w_out, layer, g, b, tm=512):
    T, D = h2.shape
    row = lambda w: pl.BlockSpec((tm, w), lambda i: (i, 0))
    const = lambda shape: pl.BlockSpec(shape, lambda i: (0,) * len(shape))
    return pl.pallas_call(
        _out_proj_kernel, name="out_proj_ln",
        grid=(T // tm,),
        in_specs=[row(D), row(GROUP_W), row(GROUP_W), row(GROUP_W), row(GROUP_W),
                  pl.BlockSpec((None, D, D), lambda i: (layer, 0, 0)), const((1, D)), const((1, D))],
        out_specs=row(D),
        out_shape=jax.ShapeDtypeStruct((T, D), F32),
        compiler_params=_cparams("parallel"),
    )(h2, *ys, w_out, g.reshape(1, D), b.reshape(1, D))


def _mlp_kernel(h_ref, w1_ref, w2_ref, g_ref, b_ref, o_ref, acc_ref):
    f = pl.program_id(1)

    @pl.when(f == 0)
    def _():
        acc_ref[...] = jnp.zeros_like(acc_ref)

    a = jnp.maximum(_mm(h_ref[...], w1_ref[...]), 0.0)
    acc_ref[...] += _mm(a * a, w2_ref[...])

    @pl.when(f == pl.num_programs(1) - 1)
    def _():
        o_ref[...] = _layer_norm(DEEPNORM_ALPHA * h_ref[...] + acc_ref[...], g_ref[...], b_ref[...])


def _mlp(h2, w1, w2, layer, g, b, tm=1024, tf=1024):
    T, D = h2.shape
    F = w1.shape[2]
    return pl.pallas_call(
        _mlp_kernel, name="mlp_ln",
        grid=(T // tm, F // tf),
        in_specs=[pl.BlockSpec((tm, D), lambda i, f: (i, 0)),
                  pl.BlockSpec((None, D, tf), lambda i, f: (layer, 0, f)),
                  pl.BlockSpec((None, tf, D), lambda i, f: (layer, f, 0)),
                  pl.BlockSpec((1, D), lambda i, f: (0, 0)),
                  pl.BlockSpec((1, D), lambda i, f: (0, 0))],
        out_specs=pl.BlockSpec((tm, D), lambda i, f: (i, 0)),
        out_shape=jax.ShapeDtypeStruct((T, D), F32),
        scratch_shapes=[pltpu.VMEM((tm, D), F32)],
        compiler_params=_cparams("parallel", "arbitrary"),
    )(h2, w1, w2, g.reshape(1, D), b.reshape(1, D))


def kernel(x, ln_emb_g, ln_emb_b, w_in, conv_w, conv_b, dt_bias, a_log, d_skip, ssm_norm_g, q_norm_g, w_uq, kv_norm_g, w_ukv, cmp_pe_k, cmp_w1_k, cmp_w2_k, cmp_pe_v, cmp_w1_v, cmp_w2_v, w_out, ln1_g, ln1_b, w_mlp1, w_mlp2, ln2_g, ln2_b):
    B, S, D = x.shape
    assert D == D_MODEL and S % NSA_TILE == 0 and S // NSA_SLC_LEN <= LANE
    T = B * S
    ret_tables = _ret_tables(S)
    mla_tables = _mla_tables(S)
    nsa_tables = _nsa_tables(S)
    h = _entry_ln(x.reshape(T, D), ln_emb_g, ln_emb_b)
    for l in range(w_in.shape[0]):
        p_ssm, p_mla, p_ret, p_nsa, nsa_kv = _in_proj(h, _layout_w_in(w_in[l]), nsa_tables[2])
        y_a = _ssm(p_ssm.reshape(B, S, W_SSM), conv_w[l], conv_b[l], dt_bias[l], a_log[l], d_skip[l], ssm_norm_g[l])
        wq, wk, wv = _layout_mla_weights(w_uq[l], w_ukv[l])
        q, k, v = _mla_prep(p_mla.reshape(B, S, W_MLA), q_norm_g[l], kv_norm_g[l], wq, wk, wv, mla_tables)
        y_b = _mla_attn(q, k, v)
        y_c = _ret(p_ret.reshape(B, S, W_RET), ret_tables)
        y_d = _nsa(p_nsa.reshape(B, S, W_NSA), nsa_kv.reshape(B, S, W_KV), cmp_pe_k[l], cmp_w1_k[l], cmp_w2_k[l],
                   cmp_pe_v[l], cmp_w1_v[l], cmp_w2_v[l], nsa_tables)
        ys = [y.reshape(T, GROUP_W) for y in (y_a, y_b, y_c, y_d)]
        h = _out_proj(h, ys, w_out, l, ln1_g[l], ln1_b[l])
        h = _mlp(h, w_mlp1, w_mlp2, l, ln2_g[l], ln2_b[l])
    return h.reshape(B, S, D)
```

```python
import functools
import math

import jax
import jax.numpy as jnp
import numpy as np
from jax import lax
from jax.experimental import pallas as pl
from jax.experimental.pallas import tpu as pltpu

F32 = jnp.float32
BF16 = jnp.bfloat16
HIGHEST = lax.Precision.HIGHEST

D_MODEL = 1024
DEPTH = 2
GROUP_W = D_MODEL // 4
SSM_HEADS = 4
SSM_HEAD_DIM = GROUP_W // SSM_HEADS
SSM_GROUPS = 2
SSM_STATE = 128
SSM_CONV = 4
SSM_CHUNK = 128
SSM_XBC = GROUP_W + 2 * SSM_GROUPS * SSM_STATE
MLA_HEADS = 4
MLA_NOPE = 64
MLA_ROPE = 32
MLA_V = GROUP_W // MLA_HEADS
MLA_Q_RANK = 256
MLA_KV_RANK = 128
RET_HEADS = 4
RET_DK = 64
RET_DV = GROUP_W // RET_HEADS
RET_CHUNK = 128
NSA_HEADS = 4
NSA_DK = 64
NSA_DV = GROUP_W // NSA_HEADS
NSA_CMP_LEN = 32
NSA_CMP_STRIDE = 16
NSA_CMP_HID = 256
NSA_SLC_LEN = 64
NSA_TOPN = 16
NSA_WIN = 512
D_FF = 4 * D_MODEL
NSA_Q = 512
ROPE_THETA = 10000.0
EPS = 1e-5
NEG_INF = -1e30
LOG2_E = math.log2(math.e)
FORCED_SCORE = 1e9
DEEPNORM_ALPHA = (2.0 * DEPTH) ** 0.25

IN_SPLITS = (
    GROUP_W, SSM_XBC, SSM_HEADS,
    MLA_Q_RANK, MLA_KV_RANK, MLA_ROPE,
    RET_HEADS * RET_DK, RET_HEADS * RET_DK, RET_HEADS * RET_DV, GROUP_W,
    NSA_HEADS * NSA_DK, NSA_DK, NSA_DV, NSA_DK, NSA_DV, NSA_DK, NSA_DV, 3 * NSA_HEADS,
)

LANE = 128
W_SSM = GROUP_W + SSM_XBC + LANE
W_MLA = MLA_Q_RANK + MLA_KV_RANK + LANE
W_RET = 4 * GROUP_W
W_NSA = NSA_HEADS * LANE + LANE + LANE
W_KV = 2 * LANE + 3 * LANE
W_PROJ = W_SSM + W_MLA + W_RET + W_NSA + 4 * NSA_DK

NSA_TILE = 512
NSA_SINK = 128
MLA_TILE = 1024
MLA_Q_TILE = 1024
VMEM_LIMIT = 48 * 1024 * 1024


def _cparams(*sem):
    return pltpu.CompilerParams(dimension_semantics=sem, vmem_limit_bytes=VMEM_LIMIT)


def _mm(a, b):
    return jnp.dot(a.astype(BF16), b.astype(BF16), preferred_element_type=F32)


def _mm_nt(a, b):
    return lax.dot_general(a.astype(BF16), b.astype(BF16), (((1,), (1,)), ((), ())),
                           preferred_element_type=F32)


def _mm_f32(a, b):
    return jnp.dot(a, b, precision=HIGHEST, preferred_element_type=F32)


def _silu(x):
    return x * jax.nn.sigmoid(x)


def _softplus(x):
    return jnp.maximum(x, 0.0) + jnp.log1p(jnp.exp(-jnp.abs(x)))


def _layer_norm(x, g, b):
    mu = jnp.mean(x, axis=-1, keepdims=True)
    xc = x - mu
    var = jnp.mean(xc * xc, axis=-1, keepdims=True)
    return xc * lax.rsqrt(var + EPS) * g + b


def _iota(shape, dim):
    return lax.broadcasted_iota(jnp.int32, shape, dim)


def _ln_kernel(x_ref, g_ref, b_ref, o_ref):
    o_ref[...] = _layer_norm(x_ref[...], g_ref[...], b_ref[...])


def _entry_ln(x2, g, b, tm=512):
    T, D = x2.shape
    return pl.pallas_call(
        _ln_kernel, name="entry_ln",
        grid=(T // tm,),
        in_specs=[pl.BlockSpec((tm, D), lambda i: (i, 0)),
                  pl.BlockSpec((1, D), lambda i: (0, 0)),
                  pl.BlockSpec((1, D), lambda i: (0, 0))],
        out_specs=pl.BlockSpec((tm, D), lambda i: (i, 0)),
        out_shape=jax.ShapeDtypeStruct((T, D), F32),
        compiler_params=_cparams("parallel"),
    )(x2, g.reshape(1, D), b.reshape(1, D))


def _pad_cols(w, width):
    return jnp.pad(w, ((0, 0), (0, width - w.shape[1])))


def _layout_w_in(w):
    offs = np.concatenate([[0], np.cumsum(IN_SPLITS)])
    p = [w[:, int(offs[i]):int(offs[i + 1])] for i in range(len(IN_SPLITS))]
    (ssm_z, ssm_xbc, ssm_dt, mla_cq, mla_ckv, mla_kr, ret_q, ret_k, ret_v, ret_g,
     nsa_q, nsa_kc, nsa_vc, nsa_ks, nsa_vs, nsa_kw, nsa_vw, nsa_gate) = p
    nsa_q_heads = [_pad_cols(nsa_q[:, h * NSA_DK:(h + 1) * NSA_DK], LANE) for h in range(NSA_HEADS)]
    cols = [ssm_z, ssm_xbc, _pad_cols(ssm_dt, LANE),
            mla_cq, mla_ckv, _pad_cols(mla_kr, LANE),
            ret_q, ret_k, ret_v, ret_g,
            *nsa_q_heads, nsa_kc, nsa_vc, _pad_cols(nsa_gate, LANE),
            nsa_ks, nsa_vs, nsa_kw, nsa_vw]
    out = jnp.concatenate(cols, axis=1)
    assert out.shape[1] == W_PROJ
    return out.astype(BF16)


def _in_proj_kernel(h_ref, w_ref, kvtab_ref, ssm_ref, mla_ref, ret_ref, nsa_ref, kv_ref):
    hb = h_ref[...].astype(BF16)
    off = 0
    for ref, width in ((ssm_ref, W_SSM), (mla_ref, W_MLA), (ret_ref, W_RET), (nsa_ref, W_NSA)):
        ref[...] = jnp.dot(hb, w_ref[:, off:off + width], preferred_element_type=F32)
        off += width
    kv = jnp.dot(hb, w_ref[:, off:off + 4 * NSA_DK], preferred_element_type=F32)
    low = _iota((kv.shape[0], LANE), 1) < NSA_DK
    sel_kv, win_kv = kv[:, :LANE], kv[:, LANE:]
    pieces = {0: sel_kv, 2: pltpu.roll(sel_kv, NSA_DK, 1), 3: win_kv, 4: pltpu.roll(win_kv, NSA_DK, 1)}
    for slab in range(W_KV // LANE):
        lanes = slice(slab * LANE, (slab + 1) * LANE)
        tab = kvtab_ref[:, lanes]
        if slab in pieces:
            kv_ref[:, lanes] = (jnp.where(low, pieces[slab], 0.0) + tab.astype(F32)).astype(BF16)
        else:
            kv_ref[:, lanes] = tab


def _in_proj(h2, w_p, kv_table, tm=256):
    T, D = h2.shape
    S = kv_table.shape[0]
    widths = (W_SSM, W_MLA, W_RET, W_NSA)
    return pl.pallas_call(
        _in_proj_kernel, name="in_proj",
        grid=(T // tm,),
        in_specs=[pl.BlockSpec((tm, D), lambda i: (i, 0)),
                  pl.BlockSpec((D, W_PROJ), lambda i: (0, 0)),
                  pl.BlockSpec((tm, W_KV), lambda i: (i % (S // tm), 0))],
        out_specs=[pl.BlockSpec((tm, w), lambda i: (i, 0)) for w in widths + (W_KV,)],
        out_shape=[jax.ShapeDtypeStruct((T, w), F32) for w in widths] + [jax.ShapeDtypeStruct((T, W_KV), BF16)],
        compiler_params=_cparams("parallel"),
    )(h2, w_p, kv_table)


def _ssm_kernel(p_ref, cw_ref, cb_ref, dtb_ref, alog_ref, dskip_ref, ng_ref, o_ref, state_ref, ext_ref):
    @pl.when(pl.program_id(0) == 0)
    def _():
        state_ref[...] = jnp.zeros_like(state_ref)
        ext_ref[:, 0:8, :] = jnp.zeros((ext_ref.shape[0], 8, SSM_XBC), F32)

    for b in range(p_ref.shape[0]):
        _ssm_chunk(p_ref.at[b], cw_ref, cb_ref, dtb_ref, alog_ref, dskip_ref, ng_ref,
                   o_ref.at[b], state_ref.at[b], ext_ref.at[b])


def _ssm_chunk(p_ref, cw_ref, cb_ref, dtb_ref, alog_ref, dskip_ref, ng_ref, o_ref, state_ref, ext_ref):
    L, H, P, N = SSM_CHUNK, SSM_HEADS, SSM_HEAD_DIM, SSM_STATE
    z = p_ref[:, 0:GROUP_W]
    ext_ref[8:8 + L, :] = p_ref[:, GROUP_W:GROUP_W + SSM_XBC]
    conv = cb_ref[...]
    for j in range(SSM_CONV):
        conv = conv + ext_ref[pl.ds(8 - (SSM_CONV - 1) + j, L), :] * cw_ref[j:j + 1, :]
    ext_ref[0:8, :] = ext_ref[L:L + 8, :]
    xbc = _silu(conv)
    xs = xbc[:, 0:GROUP_W]
    b_in = xbc[:, GROUP_W:GROUP_W + SSM_GROUPS * N]
    c_in = xbc[:, GROUP_W + SSM_GROUPS * N:]

    dt = _softplus(p_ref[:, GROUP_W + SSM_XBC:] + dtb_ref[...])
    a = dt * (-jnp.exp(alog_ref[...]))
    row = _iota((L, L), 0)
    col = _iota((L, L), 1)
    tril = col <= row
    cs = _mm_f32(jnp.where(tril, 1.0, 0.0), a)
    cs_t = cs.T
    ecs = jnp.exp(cs)
    dte = jnp.exp(cs[L - 1:L, :] - cs)
    expand = jnp.where(_iota((LANE, H * P), 0) == _iota((LANE, H * P), 1) // P, 1.0, 0.0)
    dt_x = _mm_f32(dt, expand)
    ecs_x = _mm_f32(ecs, expand)
    dte_x = _mm_f32(dte, expand)

    xdt = xs * dt_x
    wx = xdt * dte_x
    head_of_lane = _iota((L, H * P), 1) // P
    y = xs * dskip_ref[...]
    y_off = []
    rep = H // SSM_GROUPS
    for g in range(SSM_GROUPS):
        cg = c_in[:, g * N:(g + 1) * N]
        bg = b_in[:, g * N:(g + 1) * N]
        cb = _mm_nt(cg, bg)
        for h in range(g * rep, (g + 1) * rep):
            diff = cs[:, h:h + 1] - cs_t[h:h + 1, :]
            seg = jnp.where(tril, jnp.exp(jnp.where(tril, diff, 0.0)), 0.0)
            yh = _mm(cb * seg, xdt)
            y = y + jnp.where(head_of_lane == h, yh, 0.0)
        lanes = slice(g * rep * P, (g + 1) * rep * P)
        st_prev = state_ref[:, lanes]
        y_off.append(_mm(cg, st_prev))
        state_ref[:, lanes] = st_prev * ecs_x[L - 1:L, lanes] + _mm(bg.T, wx[:, lanes])
    y = y + jnp.concatenate(y_off, axis=1) * ecs_x
    y = y * _silu(z)
    ms = jnp.mean(y * y, axis=-1, keepdims=True)
    o_ref[...] = (y * lax.rsqrt(ms + EPS) * ng_ref[...]).astype(o_ref.dtype)


def _ssm(p_ssm, conv_w, conv_b, dt_bias, a_log, d_skip, norm_g):
    B, S, _ = p_ssm.shape
    L = SSM_CHUNK
    pad_h = lambda v: jnp.pad(v, (0, LANE - SSM_HEADS)).reshape(1, LANE)
    const = lambda shape: pl.BlockSpec(shape, lambda c: (0,) * len(shape))
    return pl.pallas_call(
        _ssm_kernel, name="ssm",
        grid=(S // L,),
        in_specs=[pl.BlockSpec((B, L, W_SSM), lambda c: (0, c, 0)),
                  const((SSM_CONV, SSM_XBC)), const((1, SSM_XBC)), const((1, LANE)), const((1, LANE)),
                  const((1, GROUP_W)), const((1, GROUP_W))],
        out_specs=pl.BlockSpec((B, L, GROUP_W), lambda c: (0, c, 0)),
        out_shape=jax.ShapeDtypeStruct((B, S, GROUP_W), BF16),
        scratch_shapes=[pltpu.VMEM((B, SSM_STATE, GROUP_W), F32),
                        pltpu.VMEM((B, L + 8, SSM_XBC), F32)],
        compiler_params=_cparams("arbitrary"),
    )(p_ssm, conv_w, conv_b.reshape(1, -1), pad_h(dt_bias), pad_h(a_log),
      jnp.repeat(d_skip, SSM_HEAD_DIM).reshape(1, GROUP_W), norm_g.reshape(1, GROUP_W))


def _ret_kernel(p_ref, cos_ref, sin_ref, dec_ref, zeta_ref, xi_ref, cd_ref, o_ref, state_ref):
    @pl.when(pl.program_id(0) == 0)
    def _():
        state_ref[...] = jnp.zeros_like(state_ref)

    for b in range(p_ref.shape[0]):
        _ret_chunk(p_ref.at[b], cos_ref, sin_ref, dec_ref, zeta_ref, xi_ref, cd_ref, o_ref.at[b], state_ref.at[b])


def _ret_chunk(p_ref, cos_ref, sin_ref, dec_ref, zeta_ref, xi_ref, cd_ref, o_ref, state_ref):
    L, H, DK, DV = RET_CHUNK, RET_HEADS, RET_DK, RET_DV
    W = H * DK
    q = p_ref[:, 0:W]
    k = p_ref[:, W:2 * W]
    v = p_ref[:, 2 * W:3 * W]
    gate = p_ref[:, 3 * W:4 * W]
    lane = _iota((L, W), 1)
    first_half = (lane % DK) < (DK // 2)
    head_of_lane = lane // DK

    def rope(x):
        partner = jnp.where(first_half, pltpu.roll(x, W - DK // 2, 1), pltpu.roll(x, DK // 2, 1))
        return x * cos_ref[...] + partner * sin_ref[...]

    qr = rope(q)
    kr = rope(k) * (DK ** -0.5)
    y = jnp.zeros((L, H * DV), F32)
    for h in range(H):
        qh = jnp.where(head_of_lane == h, qr, 0.0)
        sc = _mm_nt(qh, kr) * dec_ref[h]
        y = y + jnp.where(head_of_lane == h, _mm(sc, v), 0.0)
    st = state_ref[...]
    y = y + _mm(qr * xi_ref[...], st)
    same_head = (_iota((W, H * DV), 0) // DK) == (_iota((W, H * DV), 1) // DV)
    kv = _mm((kr * zeta_ref[...]).T, v)
    state_ref[...] = st * cd_ref[...] + jnp.where(same_head, kv, 0.0)
    ms = _mm_f32(y * y, jnp.where(same_head, 1.0 / DV, 0.0))
    o_ref[...] = (y * lax.rsqrt(ms + EPS) * _silu(gate)).astype(o_ref.dtype)


def _ret_tables(S):
    H, DK, L = RET_HEADS, RET_DK, RET_CHUNK
    inv = ROPE_THETA ** (-np.arange(0, DK, 2, dtype=np.float64) / DK)
    ang = np.arange(S, dtype=np.float64)[:, None] * inv[None, :]
    cos, sin = np.cos(ang), np.sin(ang)
    cos_t = np.tile(np.concatenate([cos, cos], axis=1), (1, H))
    sin_t = np.tile(np.concatenate([-sin, sin], axis=1), (1, H))
    log_gamma = np.log1p(-np.exp2(-5.0 - np.arange(H, dtype=np.float64)))
    pos = np.arange(L, dtype=np.float64)
    diff = pos[:, None] - pos[None, :]
    decay_in = np.where(diff >= 0, np.exp(np.maximum(diff, 0.0)[None] * log_gamma[:, None, None]), 0.0)
    zeta = np.exp((L - 1 - pos)[None] * log_gamma[:, None])
    xi = np.exp((pos + 1.0)[None] * log_gamma[:, None])
    chunk_decay = np.exp(L * log_gamma)
    zeta_x = np.repeat(zeta.T, DK, axis=1)
    xi_x = np.repeat(xi.T, DK, axis=1)
    cd_x = np.repeat(chunk_decay, RET_DV).reshape(1, H * RET_DV)
    return tuple(jnp.asarray(t, dtype=F32) for t in (cos_t, sin_t, decay_in, zeta_x, xi_x, cd_x))


def _ret(p_ret, tables):
    B, S, _ = p_ret.shape
    L, H = RET_CHUNK, RET_HEADS
    W = H * RET_DK
    cos_t, sin_t, decay_in, zeta_x, xi_x, cd_x = tables
    const = lambda shape: pl.BlockSpec(shape, lambda c: (0,) * len(shape))
    return pl.pallas_call(
        _ret_kernel, name="retention",
        grid=(S // L,),
        in_specs=[pl.BlockSpec((B, L, W_RET), lambda c: (0, c, 0)),
                  pl.BlockSpec((L, W), lambda c: (c, 0)),
                  pl.BlockSpec((L, W), lambda c: (c, 0)),
                  const((H, L, L)), const((L, W)), const((L, W)), const((1, H * RET_DV))],
        out_specs=pl.BlockSpec((B, L, GROUP_W), lambda c: (0, c, 0)),
        out_shape=jax.ShapeDtypeStruct((B, S, GROUP_W), BF16),
        scratch_shapes=[pltpu.VMEM((B, W, H * RET_DV), F32)],
        compiler_params=_cparams("arbitrary"),
    )(p_ret, cos_t, sin_t, decay_in, zeta_x, xi_x, cd_x)


def _mla_prep_kernel(p_ref, qg_ref, wq_ref, kvg_ref, wk_ref, wv_ref, cos_ref, sin_ref,
                     q_ref, k_ref, v_ref):
    tm = p_ref.shape[0]
    cq = p_ref[:, 0:MLA_Q_RANK]
    ckv = p_ref[:, MLA_Q_RANK:MLA_Q_RANK + MLA_KV_RANK]
    kr = p_ref[:, MLA_Q_RANK + MLA_KV_RANK:]

    def rms(x, g):
        return x * lax.rsqrt(jnp.mean(x * x, axis=-1, keepdims=True) + EPS) * g

    q = _mm(rms(cq, qg_ref[...]), wq_ref[...])
    kvl = rms(ckv, kvg_ref[...])
    kn = _mm(kvl, wk_ref[...])
    vv = _mm(kvl, wv_ref[...])
    kr_sh = pltpu.roll(kr, MLA_NOPE, 1)
    lane = _iota((tm, LANE), 1)
    half = MLA_ROPE // 2
    low = (lane >= MLA_NOPE) & (lane < MLA_NOPE + half)
    cos = cos_ref[...]
    sin = sin_ref[...]

    def rope(x):
        partner = jnp.where(low, pltpu.roll(x, LANE - half, 1), pltpu.roll(x, half, 1))
        return x * cos + partner * sin

    scale = (MLA_NOPE + MLA_ROPE) ** -0.5 * LOG2_E
    for h in range(MLA_HEADS):
        sl = slice(h * LANE, (h + 1) * LANE)
        q_ref[h] = (rope(q[:, sl]) * scale).astype(BF16)
        k_ref[h] = rope(kn[:, sl] + kr_sh).astype(BF16)
        v_ref[h] = jnp.where(lane == _mla_ones_lane(h), 1.0, vv[:, sl]).astype(BF16)


def _mla_tables(S):
    inv = ROPE_THETA ** (-np.arange(0, MLA_ROPE, 2, dtype=np.float64) / MLA_ROPE)
    ang = np.arange(S, dtype=np.float64)[:, None] * inv[None, :]
    cos, sin = np.cos(ang), np.sin(ang)
    tail = LANE - MLA_NOPE - MLA_ROPE
    cos_t = np.concatenate([np.ones((S, MLA_NOPE)), cos, cos, np.ones((S, tail))], axis=1)
    sin_t = np.concatenate([np.zeros((S, MLA_NOPE)), -sin, sin, np.zeros((S, tail))], axis=1)
    return jnp.asarray(cos_t, dtype=F32), jnp.asarray(sin_t, dtype=F32)


def _layout_mla_weights(w_uq, w_ukv):
    H = MLA_HEADS
    dq = MLA_NOPE + MLA_ROPE
    wq = jnp.concatenate([_pad_cols(w_uq[:, h * dq:(h + 1) * dq], LANE) for h in range(H)], axis=1)
    dkv = MLA_NOPE + MLA_V
    wk, wv = [], []
    for h in range(H):
        blk = w_ukv[:, h * dkv:(h + 1) * dkv]
        wk.append(_pad_cols(blk[:, :MLA_NOPE], LANE))
        v = blk[:, MLA_NOPE:]
        zero = jnp.zeros_like(v)
        wv.append(jnp.concatenate([v, zero] if h % 2 == 0 else [zero, v], axis=1))
    return wq.astype(BF16), jnp.concatenate(wk, axis=1).astype(BF16), jnp.concatenate(wv, axis=1).astype(BF16)


def _mla_prep(p_mla, q_norm_g, kv_norm_g, wq, wk, wv, tables, tm=512):
    B, S, _ = p_mla.shape
    H = MLA_HEADS
    cos_t, sin_t = tables
    const = lambda shape: pl.BlockSpec(shape, lambda b, i: (0,) * len(shape))
    qkv_spec = pl.BlockSpec((None, H, tm, LANE), lambda b, i: (b, 0, i, 0))
    qkv_shape = jax.ShapeDtypeStruct((B, H, S, LANE), BF16)
    return pl.pallas_call(
        _mla_prep_kernel, name="mla_prep",
        grid=(B, S // tm),
        in_specs=[pl.BlockSpec((None, tm, W_MLA), lambda b, i: (b, i, 0)),
                  const((1, MLA_Q_RANK)), const((MLA_Q_RANK, H * LANE)),
                  const((1, MLA_KV_RANK)), const((MLA_KV_RANK, H * LANE)), const((MLA_KV_RANK, H * LANE)),
                  pl.BlockSpec((tm, LANE), lambda b, i: (i, 0)),
                  pl.BlockSpec((tm, LANE), lambda b, i: (i, 0))],
        out_specs=[qkv_spec, qkv_spec, qkv_spec],
        out_shape=[qkv_shape, qkv_shape, qkv_shape],
        compiler_params=_cparams("parallel", "parallel"),
    )(p_mla, q_norm_g.reshape(1, -1), wq, kv_norm_g.reshape(1, -1), wk, wv, cos_t, sin_t)


def _mla_ones_lane(h):
    return MLA_V if h % 2 == 0 else 0


def _mla_attn_kernel(qi_ref, kj_ref, q_ref, k_ref, v_ref, o_ref, m_ref, acc_ref):
    H = MLA_HEADS
    tq, tk = q_ref.shape[1], k_ref.shape[1]
    i = qi_ref[pl.program_id(1)]
    j = kj_ref[pl.program_id(1)]

    @pl.when(j == 0)
    def _():
        m_ref[...] = jnp.full(m_ref.shape, NEG_INF, F32)
        acc_ref[...] = jnp.zeros_like(acc_ref)

    def sweep(masked):
        if masked:
            causal = (_iota((tq, tk), 1) - _iota((tq, tk), 0)) <= i * tq - j * tk
        scores = [lax.dot_general(q_ref[h], k_ref[h], (((1,), (1,)), ((), ())), preferred_element_type=F32)
                  for h in range(H)]
        for h in range(H):
            s = scores[h]
            if masked:
                s = jnp.where(causal, s, NEG_INF)
            m_prev = m_ref[h]
            m_new = jnp.maximum(m_prev, jnp.max(s, axis=-1, keepdims=True))
            p = jnp.exp2(s - jnp.tile(m_new, (1, tk // LANE)))
            acc_ref[h] = (jnp.exp2(m_prev - m_new) * acc_ref[h]
                          + jnp.dot(p.astype(BF16), v_ref[h], preferred_element_type=F32))
            m_ref[h] = m_new

    ratio = tq // tk

    @pl.when(j < ratio * i)
    def _():
        sweep(False)

    @pl.when(j >= ratio * i)
    def _():
        sweep(True)

    @pl.when(j == ratio * i + ratio - 1)
    def _():
        lane = _iota((tq, LANE), 1)
        for pair in range(H // 2):
            he, ho = 2 * pair, 2 * pair + 1
            acc_e, acc_o = acc_ref[he], acc_ref[ho]
            le = acc_e[:, _mla_ones_lane(he):_mla_ones_lane(he) + 1]
            lo = acc_o[:, _mla_ones_lane(ho):_mla_ones_lane(ho) + 1]
            o_ref[:, pair * LANE:(pair + 1) * LANE] = jnp.where(lane < MLA_V, acc_e / le, acc_o / lo).astype(o_ref.dtype)


def _mla_attn(q, k, v):
    B, H, S, _ = q.shape
    t = min(MLA_TILE, S)
    tq = min(MLA_Q_TILE, S)
    pairs = [(i, j) for i in range(S // tq) for j in range((i + 1) * tq // t)]
    qi = jnp.asarray([p[0] for p in pairs], jnp.int32)
    kj = jnp.asarray([p[1] for p in pairs], jnp.int32)
    grid_spec = pltpu.PrefetchScalarGridSpec(
        num_scalar_prefetch=2,
        grid=(B, len(pairs)),
        in_specs=[pl.BlockSpec((None, H, tq, LANE), lambda b, p, qi, kj: (b, 0, qi[p], 0)),
                  pl.BlockSpec((None, H, t, LANE), lambda b, p, qi, kj: (b, 0, kj[p], 0)),
                  pl.BlockSpec((None, H, t, LANE), lambda b, p, qi, kj: (b, 0, kj[p], 0))],
        out_specs=pl.BlockSpec((None, tq, GROUP_W), lambda b, p, qi, kj: (b, qi[p], 0)),
        scratch_shapes=[pltpu.VMEM((H, tq, LANE), F32), pltpu.VMEM((H, tq, LANE), F32)],
    )
    return pl.pallas_call(
        _mla_attn_kernel, name="mla_attn",
        grid_spec=grid_spec,
        out_shape=jax.ShapeDtypeStruct((B, S, GROUP_W), BF16),
        compiler_params=_cparams("parallel", "arbitrary"),
    )(qi, kj, q, k, v)


POS_HI = NSA_DK
POS_LO = NSA_DK + 3
POS_ONE = NSA_DK + 6
ONES_LANE = NSA_DV


def _split_bf16(x, parts=3):
    out, rem = [], np.float64(x)
    for _ in range(parts):
        piece = np.float64(np.float32(rem).astype(jnp.bfloat16).astype(np.float32))
        out.append(float(piece))
        rem = rem - piece
    return out


def _nsa_query_table():
    H = NSA_HEADS
    tab = np.zeros((2 * H, LANE), np.float32)
    for h in range(H):
        c = 2.0 ** (-8.0 * (h + 1) / H) * LOG2_E
        pieces = _split_bf16(c)
        tab[h, POS_HI:POS_HI + 3] = pieces
        tab[h, POS_LO:POS_LO + 3] = pieces
        tab[H + h, POS_ONE] = -sum(pieces)
    return jnp.asarray(tab)


def _nsa_pos_lanes(pos, lo_offset=0.0):
    t = np.zeros((len(pos), LANE - NSA_DK), np.float32)
    t[:, POS_HI - NSA_DK:POS_HI - NSA_DK + 3] = (NSA_SLC_LEN * (pos // NSA_SLC_LEN))[:, None]
    t[:, POS_LO - NSA_DK:POS_LO - NSA_DK + 3] = (pos % NSA_SLC_LEN + lo_offset)[:, None]
    t[:, POS_ONE - NSA_DK] = 1.0
    return t


def _nsa_queries(q_ref, qtab_ref, qb):
    Q, H = q_ref.shape[0], NSA_HEADS
    qpos = (qb * Q + _iota((Q, 1), 0)).astype(F32)
    out = []
    for h in range(H):
        q = q_ref[:, h * LANE:(h + 1) * LANE] * (NSA_DK ** -0.5 * LOG2_E)
        out.append((q + qtab_ref[h:h + 1, :] + qtab_ref[H + h:H + h + 1, :] * qpos).astype(BF16))
    return out


def _normalise(o):
    return o / o[:, ONES_LANE:ONES_LANE + 1]


def _unstack_heads(o):
    Q = o.shape[0] // NSA_HEADS
    lane = _iota((Q, LANE), 1)
    out = []
    for pair in range(NSA_HEADS // 2):
        even = o[(2 * pair) * Q:(2 * pair + 1) * Q]
        odd = o[(2 * pair + 1) * Q:(2 * pair + 2) * Q]
        out.append(jnp.where(lane < NSA_DV, even, pltpu.roll(odd, NSA_DV, 1)))
    return jnp.concatenate(out, axis=1)


def _nsa_cmp_kernel(uk_ref, uv_ref, pek_ref, pev_ref, w1k_ref, w1v_ref, w2k_ref, w2v_ref, cpos_ref,
                    kc_ref, vc_ref, sh_ref):
    nb = uk_ref.shape[0]
    half = uk_ref.shape[1]

    def hidden(u_ref, pe_ref, w1_ref):
        u = u_ref[...]
        first = _mm(u + pe_ref[0:1, :], w1_ref[0:half, :])
        second = _mm(u + pe_ref[1:2, :], w1_ref[half:2 * half, :])
        sh_ref[0:nb, :] = second
        sh_ref[nb:nb + 8, :] = jnp.zeros((8, NSA_CMP_HID), F32)
        return first + sh_ref[pl.ds(1, nb), :]

    hk = _silu(hidden(uk_ref, pek_ref, w1k_ref))
    hv = _silu(hidden(uv_ref, pev_ref, w1v_ref))
    kc_ref[...] = (_mm(hk, w2k_ref[...]) + cpos_ref[...]).astype(BF16)
    ones_lane = jnp.where(_iota((1, LANE), 1) == ONES_LANE, 1.0, 0.0)
    vc_ref[...] = (_mm(hv, w2v_ref[...]) + ones_lane).astype(BF16)


def _nsa_compress(uk, uv, pe_k, w1_k, w2_k, pe_v, w1_v, w2_v):
    B, nb, half = uk.shape
    hid = NSA_CMP_HID
    const = lambda shape: pl.BlockSpec(shape, lambda b: (0,) * len(shape))
    w2k = _pad_cols(w2_k, LANE).astype(BF16)
    w2v = _pad_cols(w2_v, LANE).astype(BF16)
    centre = _nsa_pos_lanes(np.arange(nb) * NSA_CMP_STRIDE, 0.5 * (NSA_CMP_LEN - 1))
    cpos = jnp.asarray(np.concatenate([np.zeros((nb, NSA_DK), np.float32), centre], axis=1))
    out_spec = pl.BlockSpec((None, nb, LANE), lambda b: (b, 0, 0))
    out_shape = jax.ShapeDtypeStruct((B, nb, LANE), BF16)
    return pl.pallas_call(
        _nsa_cmp_kernel, name="nsa_compress",
        grid=(B,),
        in_specs=[pl.BlockSpec((None, nb, half), lambda b: (b, 0, 0)),
                  pl.BlockSpec((None, nb, half), lambda b: (b, 0, 0)),
                  const((2, half)), const((2, half)),
                  const((2 * half, hid)), const((2 * half, hid)),
                  const((hid, LANE)), const((hid, LANE)), const((nb, LANE))],
        out_specs=[out_spec, out_spec],
        out_shape=[out_shape, out_shape],
        scratch_shapes=[pltpu.VMEM((nb + 8, hid), F32)],
        compiler_params=_cparams("parallel"),
    )(uk, uv, pe_k.reshape(2, half), pe_v.reshape(2, half), w1_k.astype(BF16), w1_v.astype(BF16), w2k, w2v, cpos)


def _nsa_sel_kernel(q_ref, qtab_ref, kc_ref, vc_ref, ovt_ref, oc_ref, selb_ref, any_ref, imp_ref, *, n_slc, top_n):
    Q, H = q_ref.shape[0], NSA_HEADS
    qb = pl.program_id(1)
    nc = kc_ref.shape[0]
    qs = jnp.concatenate(_nsa_queries(q_ref, qtab_ref, qb), axis=0)
    nt_dims = (((1,), (1,)), ((), ()))

    def attend(ncols):
        s = lax.dot_general(qs, kc_ref[0:ncols, :], nt_dims, preferred_element_type=F32)
        qpos = qb * Q + (_iota((H * Q, ncols), 0) & (Q - 1))
        block_end = _iota((H * Q, ncols), 1) * NSA_CMP_STRIDE + (NSA_CMP_LEN - 1)
        s = jnp.where(block_end <= qpos, s, NEG_INF)
        e = jnp.exp2(s - jnp.max(s, axis=-1, keepdims=True))
        qpos_col = qb * Q + (_iota((H * Q, 1), 0) & (Q - 1))
        has_block = jnp.where(qpos_col >= NSA_CMP_LEN - 1, 1.0, 0.0)
        p = e * (has_block / jnp.sum(e, axis=-1, keepdims=True))
        oc_ref[...] = _unstack_heads(jnp.dot(p.astype(BF16), vc_ref[0:ncols, :], preferred_element_type=F32))
        p_sum = p[0:Q]
        for h in range(1, H):
            p_sum = p_sum + p[h * Q:(h + 1) * Q]
        imp_ref[...] = lax.dot_general(ovt_ref[:, 0:ncols], p_sum, nt_dims, precision=HIGHEST,
                                       preferred_element_type=F32)

    tiles_needed = ((qb + 1) * Q // NSA_CMP_STRIDE + LANE - 1) // LANE
    for tiles in range(1, nc // LANE + 1):
        pl.when(tiles_needed == tiles)(functools.partial(attend, tiles * LANE))

    imp = imp_ref[...]
    blk = _iota((LANE, Q), 0)
    q_blk = (qb * Q + _iota((LANE, Q), 1)) >> int(math.log2(NSA_SLC_LEN))
    causal = blk <= q_blk
    for forced_blk in (0, q_blk, q_blk - 1):
        imp = jnp.where(blk == forced_blk, FORCED_SCORE, imp)
    imp = jnp.where(causal, imp, -1.0)
    imp = jnp.where(blk < n_slc, imp, -2.0)
    blk_f = blk.astype(F32)
    sel = jnp.zeros((LANE, Q), F32)
    for _ in range(top_n):
        m = jnp.max(imp, axis=0, keepdims=True)
        first = jnp.min(jnp.where(imp == m, blk_f, float(LANE)), axis=0, keepdims=True)
        hit = blk_f == first
        sel = jnp.where(hit, 1.0, sel)
        imp = jnp.where(hit, -3.0, imp)
    sel = jnp.where(causal, sel, 0.0).T
    selb_ref[...] = jnp.where(sel > 0.5, 0.0, NEG_INF).astype(BF16)
    any_ref[...] = jnp.max(sel, axis=0, keepdims=True)


def _nsa_select(p_nsa, qtab, kc, vc, overlap_t):
    B, S, _ = p_nsa.shape
    Q = NSA_Q
    nqb = S // Q
    nc = kc.shape[1]
    n_slc = S // NSA_SLC_LEN
    kern = functools.partial(_nsa_sel_kernel, n_slc=n_slc, top_n=min(NSA_TOPN, n_slc))
    return pl.pallas_call(
        kern, name="nsa_select",
        grid=(B, nqb),
        in_specs=[pl.BlockSpec((None, Q, NSA_HEADS * LANE), lambda b, i: (b, i, 0)),
                  pl.BlockSpec((2 * NSA_HEADS, LANE), lambda b, i: (0, 0)),
                  pl.BlockSpec((None, nc, LANE), lambda b, i: (b, 0, 0)),
                  pl.BlockSpec((None, nc, LANE), lambda b, i: (b, 0, 0)),
                  pl.BlockSpec((LANE, nc), lambda b, i: (0, 0))],
        out_specs=[pl.BlockSpec((None, Q, GROUP_W), lambda b, i: (b, i, 0)),
                   pl.BlockSpec((None, Q, LANE), lambda b, i: (b, i, 0)),
                   pl.BlockSpec((None, None, 1, LANE), lambda b, i: (b, i, 0, 0))],
        out_shape=[jax.ShapeDtypeStruct((B, S, GROUP_W), F32),
                   jax.ShapeDtypeStruct((B, S, LANE), BF16),
                   jax.ShapeDtypeStruct((B, nqb, 1, LANE), F32)],
        scratch_shapes=[pltpu.VMEM((LANE, Q), F32)],
        compiler_params=_cparams("parallel", "parallel"),
    )(p_nsa, qtab, kc, vc, overlap_t)


def _nsa_attn_kernel(flags_ref, q_ref, qtab_ref, gate_ref, oc_ref, selb_ref, ks_ref, vs_ref, kw_ref, vw_ref,
                     o_ref, m_ref, acc_ref, *, nt):
    Q, H, TK = q_ref.shape[0], NSA_HEADS, NSA_TILE
    PART = 2 * Q
    b = pl.program_id(0)
    qb = pl.program_id(1)
    nqb = pl.num_programs(1)
    qh = _nsa_queries(q_ref, qtab_ref, qb)
    nt_dims = (((1,), (1,)), ((), ()))

    selb = selb_ref[...]
    qs_sel = jnp.concatenate([jnp.concatenate([q, selb], axis=1) for q in qh], axis=0)
    sink_off = jnp.where(_iota(selb.shape, 1) < NSA_SINK // NSA_SLC_LEN, NEG_INF, selb.astype(F32)).astype(BF16)
    qs_loop = jnp.concatenate([jnp.concatenate([q, sink_off], axis=1) for q in qh], axis=0)
    m_ref[...] = jnp.full(m_ref.shape, NEG_INF, F32)
    acc_ref[...] = jnp.zeros_like(acc_ref)

    def update(t, diagonal):
        rows = pl.ds(pl.multiple_of(t * TK, TK), TK)
        k_tile, v_tile = ks_ref[rows, :], vs_ref[rows, :]
        parts = [slice(i * PART, (i + 1) * PART) for i in range(H * Q // PART)]
        qs_t = qs_sel if diagonal else qs_loop
        scores = [lax.dot_general(qs_t[r], k_tile, nt_dims, preferred_element_type=F32) for r in parts]
        if diagonal:
            k_sink, v_sink = ks_ref[0:NSA_SINK, :], vs_ref[0:NSA_SINK, :]
            sink_scores = [lax.dot_general(qs_sel[r], k_sink, nt_dims, preferred_element_type=F32) for r in parts]
            sink_bias = jnp.where(t > 0, 0.0, NEG_INF)
        for idx, (r, s) in enumerate(zip(parts, scores)):
            m_prev = m_ref[r]
            if diagonal:
                ahead = _iota((PART, TK), 1) - (_iota((PART, TK), 0) & (Q - 1))
                s = jnp.where(ahead <= qb * Q - t * TK, s, NEG_INF)
                s_sink = sink_scores[idx] + sink_bias
                m_prev = jnp.maximum(m_prev, jnp.max(s_sink, axis=-1, keepdims=True))
            m_new = jnp.maximum(m_prev, jnp.max(s, axis=-1, keepdims=True))
            p = jnp.exp2(s - jnp.tile(m_new, (1, TK // LANE)))
            update_acc = jnp.dot(p.astype(BF16), v_tile, preferred_element_type=F32)
            if diagonal:
                p_sink = jnp.exp2(s_sink - jnp.tile(m_new, (1, NSA_SINK // LANE)))
                update_acc = update_acc + jnp.dot(p_sink.astype(BF16), v_sink, preferred_element_type=F32)
            acc_ref[r] = jnp.exp2(m_ref[r] - m_new) * acc_ref[r] + update_acc
            m_ref[r] = m_new

    def tile(t, carry):
        @pl.when(flags_ref[(b * nqb + qb) * nt + t] > 0)
        def _():
            update(t, False)
        return carry

    t_diag = (qb * Q) // TK
    lax.fori_loop(0, t_diag, tile, 0)
    update(t_diag, True)
    o_s = _unstack_heads(_normalise(acc_ref[...]))

    W = Q + NSA_WIN
    start = jnp.maximum(qb * Q - NSA_WIN, 0)
    rows_w = pl.ds(pl.multiple_of(start, Q), W)
    qs_win = jnp.concatenate(qh, axis=0)
    s = lax.dot_general(qs_win, kw_ref[rows_w, :], nt_dims, preferred_element_type=F32)
    lead = qb * Q - start
    q_minus_k = (_iota((H * Q, W), 0) & (Q - 1)) - _iota((H * Q, W), 1)
    s = jnp.where(q_minus_k >= -lead, jnp.where(q_minus_k < NSA_WIN - lead, s, NEG_INF), NEG_INF)
    p = jnp.exp2(s - jnp.max(s, axis=-1, keepdims=True))
    o_w = _unstack_heads(_normalise(jnp.dot(p.astype(BF16), vw_ref[rows_w, :], preferred_element_type=F32)))

    gates = jax.nn.sigmoid(gate_ref[...])
    lane_head = _iota((LANE, GROUP_W), 1) // NSA_DV
    src = _iota((LANE, GROUP_W), 0)
    out = None
    for j, branch in enumerate((oc_ref[...], o_s, o_w)):
        g = _mm_f32(gates, jnp.where(src == lane_head * 3 + j, 1.0, 0.0))
        out = g * branch if out is None else out + g * branch
    o_ref[...] = out.astype(o_ref.dtype)


def _nsa_attend(p_nsa, qtab, o_c, selb, flags, kv):
    B, S, _ = p_nsa.shape
    Q = NSA_Q
    nqb = S // Q
    nt = S // NSA_TILE
    gate_blk = (W_NSA - LANE) // LANE
    kern = functools.partial(_nsa_attn_kernel, nt=nt)
    slab = lambda width, col: pl.BlockSpec((None, S, width), lambda b, i, f: (b, 0, col))
    grid_spec = pltpu.PrefetchScalarGridSpec(
        num_scalar_prefetch=1,
        grid=(B, nqb),
        in_specs=[pl.BlockSpec((None, Q, NSA_HEADS * LANE), lambda b, i, f: (b, i, 0)),
                  pl.BlockSpec((2 * NSA_HEADS, LANE), lambda b, i, f: (0, 0)),
                  pl.BlockSpec((None, Q, LANE), lambda b, i, f: (b, i, gate_blk)),
                  pl.BlockSpec((None, Q, GROUP_W), lambda b, i, f: (b, i, 0)),
                  pl.BlockSpec((None, Q, LANE), lambda b, i, f: (b, i, 0)),
                  slab(2 * LANE, 0), slab(LANE, 2), slab(LANE, 3), slab(LANE, 4)],
        out_specs=pl.BlockSpec((None, Q, GROUP_W), lambda b, i, f: (b, i, 0)),
        scratch_shapes=[pltpu.VMEM((NSA_HEADS * Q, LANE), F32), pltpu.VMEM((NSA_HEADS * Q, LANE), F32)],
    )
    return pl.pallas_call(
        kern, name="nsa_attend",
        grid_spec=grid_spec,
        out_shape=jax.ShapeDtypeStruct((B, S, GROUP_W), BF16),
        compiler_params=_cparams("parallel", "parallel"),
    )(flags, p_nsa, qtab, p_nsa, o_c, selb, kv, kv, kv, kv)


def _nsa_tables(S):
    nc = S // NSA_CMP_STRIDE
    n = np.arange(nc)[None, :]
    j = np.arange(LANE)[:, None]
    start = n * NSA_CMP_STRIDE
    ov = (start < (j + 1) * NSA_SLC_LEN) & (start + NSA_CMP_LEN - 1 >= j * NSA_SLC_LEN)
    ov &= (n < (S - NSA_CMP_LEN) // NSA_CMP_STRIDE + 1) & (j < S // NSA_SLC_LEN)
    pos = np.arange(S)
    k_zero = np.zeros((S, NSA_DK), np.float32)
    block_onehot = (pos[:, None] // NSA_SLC_LEN == np.arange(LANE)[None, :]).astype(np.float32)
    v_lanes = np.zeros((S, LANE), np.float32)
    v_lanes[:, ONES_LANE] = 1.0
    kv_table = np.concatenate([k_zero, _nsa_pos_lanes(pos), block_onehot, v_lanes,
                               k_zero, _nsa_pos_lanes(pos), v_lanes], axis=1)
    assert kv_table.shape[1] == W_KV
    return _nsa_query_table(), jnp.asarray(ov.astype(np.float32)), jnp.asarray(kv_table, dtype=BF16)


def _nsa(p_nsa, kv, pe_k, w1_k, w2_k, pe_v, w1_v, w2_v, tables):
    B, S, _ = p_nsa.shape
    qtab, overlap_t, _ = tables
    q_w = NSA_HEADS * LANE
    half = NSA_CMP_STRIDE * NSA_DK
    piece = lambda idx: p_nsa[:, :, q_w + idx * NSA_DK:q_w + (idx + 1) * NSA_DK]
    uk = piece(0).reshape(B, S // NSA_CMP_STRIDE, half)
    uv = piece(1).reshape(B, S // NSA_CMP_STRIDE, half)
    kc, vc = _nsa_compress(uk, uv, pe_k, w1_k, w2_k, pe_v, w1_v, w2_v)
    o_c, selb, blk_any = _nsa_select(p_nsa, qtab, kc, vc, overlap_t)
    per_tile = NSA_TILE // NSA_SLC_LEN
    nt = S // NSA_TILE
    blk_any = blk_any[:, :, 0, :nt * per_tile].at[:, :, :NSA_SINK // NSA_SLC_LEN].set(0.0)
    flags = blk_any.reshape(B, S // NSA_Q, nt, per_tile).max(axis=-1)
    flags = (flags > 0).astype(jnp.int32).reshape(-1)
    return _nsa_attend(p_nsa, qtab, o_c, selb, flags, kv)


def _out_proj_kernel(h_ref, ya_ref, yb_ref, yc_ref, yd_ref, w_ref, g_ref, b_ref, o_ref):
    mix = None
    for idx, y_ref in enumerate((ya_ref, yb_ref, yc_ref, yd_ref)):
        part = _mm(y_ref[...], w_ref[idx * GROUP_W:(idx + 1) * GROUP_W, :])
        mix = part if mix is None else mix + part
    o_ref[...] = _layer_norm(DEEPNORM_ALPHA * h_ref[...] + mix, g_ref[...], b_ref[...])


def _out_proj(h2, ys, w_out, layer, g, b, tm=512):
    T, D = h2.shape
    row = lambda w: pl.BlockSpec((tm, w), lambda i: (i, 0))
    const = lambda shape: pl.BlockSpec(shape, lambda i: (0,) * len(shape))
    return pl.pallas_call(
        _out_proj_kernel, name="out_proj_ln",
        grid=(T // tm,),
        in_specs=[row(D), row(GROUP_W), row(GROUP_W), row(GROUP_W), row(GROUP_W),
                  pl.BlockSpec((None, D, D), lambda i: (layer, 0, 0)), const((1, D)), const((1, D))],
        out_specs=row(D),
        out_shape=jax.ShapeDtypeStruct((T, D), F32),
        compiler_params=_cparams("parallel"),
    )(h2, *ys, w_out, g.reshape(1, D), b.reshape(1, D))


def _mlp_kernel(h_ref, w1_ref, w2_ref, g_ref, b_ref, o_ref, acc_ref):
    f = pl.program_id(1)

    @pl.when(f == 0)
    def _():
        acc_ref[...] = jnp.zeros_like(acc_ref)

    a = jnp.maximum(_mm(h_ref[...], w1_ref[...]), 0.0)
    acc_ref[...] += _mm(a * a, w2_ref[...])

    @pl.when(f == pl.num_programs(1) - 1)
    def _():
        o_ref[...] = _layer_norm(DEEPNORM_ALPHA * h_ref[...] + acc_ref[...], g_ref[...], b_ref[...])


def _mlp(h2, w1, w2, layer, g, b, tm=1024, tf=1024):
    T, D = h2.shape
    F = w1.shape[2]
    return pl.pallas_call(
        _mlp_kernel, name="mlp_ln",
        grid=(T // tm, F // tf),
        in_specs=[pl.BlockSpec((tm, D), lambda i, f: (i, 0)),
                  pl.BlockSpec((None, D, tf), lambda i, f: (layer, 0, f)),
                  pl.BlockSpec((None, tf, D), lambda i, f: (layer, f, 0)),
                  pl.BlockSpec((1, D), lambda i, f: (0, 0)),
                  pl.BlockSpec((1, D), lambda i, f: (0, 0))],
        out_specs=pl.BlockSpec((tm, D), lambda i, f: (i, 0)),
        out_shape=jax.ShapeDtypeStruct((T, D), F32),
        scratch_shapes=[pltpu.VMEM((tm, D), F32)],
        compiler_params=_cparams("parallel", "arbitrary"),
    )(h2, w1, w2, g.reshape(1, D), b.reshape(1, D))


def kernel(x, ln_emb_g, ln_emb_b, w_in, conv_w, conv_b, dt_bias, a_log, d_skip, ssm_norm_g, q_norm_g, w_uq, kv_norm_g, w_ukv, cmp_pe_k, cmp_w1_k, cmp_w2_k, cmp_pe_v, cmp_w1_v, cmp_w2_v, w_out, ln1_g, ln1_b, w_mlp1, w_mlp2, ln2_g, ln2_b):
    B, S, D = x.shape
    assert D == D_MODEL and S % NSA_TILE == 0 and S // NSA_SLC_LEN <= LANE
    T = B * S
    ret_tables = _ret_tables(S)
    mla_tables = _mla_tables(S)
    nsa_tables = _nsa_tables(S)
    h = _entry_ln(x.reshape(T, D), ln_emb_g, ln_emb_b)
    for l in range(w_in.shape[0]):
        p_ssm, p_mla, p_ret, p_nsa, nsa_kv = _in_proj(h, _layout_w_in(w_in[l]), nsa_tables[2])
        y_a = _ssm(p_ssm.reshape(B, S, W_SSM), conv_w[l], conv_b[l], dt_bias[l], a_log[l], d_skip[l], ssm_norm_g[l])
        wq, wk, wv = _layout_mla_weights(w_uq[l], w_ukv[l])
        q, k, v = _mla_prep(p_mla.reshape(B, S, W_MLA), q_norm_g[l], kv_norm_g[l], wq, wk, wv, mla_tables)
        y_b = _mla_attn(q, k, v)
        y_c = _ret(p_ret.reshape(B, S, W_RET), ret_tables)
        y_d = _nsa(p_nsa.reshape(B, S, W_NSA), nsa_kv.reshape(B, S, W_KV), cmp_pe_k[l], cmp_w1_k[l], cmp_w2_k[l],
                   cmp_pe_v[l], cmp_w1_v[l], cmp_w2_v[l], nsa_tables)
        ys = [y.reshape(T, GROUP_W) for y in (y_a, y_b, y_c, y_d)]
        h = _out_proj(h, ys, w_out, l, ln1_g[l], ln1_b[l])
        h = _mlp(h, w_mlp1, w_mlp2, l, ln2_g[l], ln2_b[l])
    return h.reshape(B, S, D)
```

```python
import functools
import math

import jax
import jax.numpy as jnp
import numpy as np
from jax import lax
from jax.experimental import pallas as pl
from jax.experimental.pallas import tpu as pltpu

F32 = jnp.float32
BF16 = jnp.bfloat16
HIGHEST = lax.Precision.HIGHEST

D_MODEL = 1024
DEPTH = 2
GROUP_W = D_MODEL // 4
SSM_HEADS = 4
SSM_HEAD_DIM = GROUP_W // SSM_HEADS
SSM_GROUPS = 2
SSM_STATE = 128
SSM_CONV = 4
SSM_CHUNK = 128
SSM_XBC = GROUP_W + 2 * SSM_GROUPS * SSM_STATE
MLA_HEADS = 4
MLA_NOPE = 64
MLA_ROPE = 32
MLA_V = GROUP_W // MLA_HEADS
MLA_Q_RANK = 256
MLA_KV_RANK = 128
RET_HEADS = 4
RET_DK = 64
RET_DV = GROUP_W // RET_HEADS
RET_CHUNK = 128
NSA_HEADS = 4
NSA_DK = 64
NSA_DV = GROUP_W // NSA_HEADS
NSA_CMP_LEN = 32
NSA_CMP_STRIDE = 16
NSA_CMP_HID = 256
NSA_SLC_LEN = 64
NSA_TOPN = 16
NSA_WIN = 512
D_FF = 4 * D_MODEL
NSA_Q = 512
ROPE_THETA = 10000.0
EPS = 1e-5
NEG_INF = -1e30
LOG2_E = math.log2(math.e)
FORCED_SCORE = 1e9
DEEPNORM_ALPHA = (2.0 * DEPTH) ** 0.25

IN_SPLITS = (
    GROUP_W, SSM_XBC, SSM_HEADS,
    MLA_Q_RANK, MLA_KV_RANK, MLA_ROPE,
    RET_HEADS * RET_DK, RET_HEADS * RET_DK, RET_HEADS * RET_DV, GROUP_W,
    NSA_HEADS * NSA_DK, NSA_DK, NSA_DV, NSA_DK, NSA_DV, NSA_DK, NSA_DV, 3 * NSA_HEADS,
)

LANE = 128
W_SSM = GROUP_W + SSM_XBC + LANE
W_MLA = MLA_Q_RANK + MLA_KV_RANK + LANE
W_RET = 4 * GROUP_W
W_NSA = NSA_HEADS * LANE + LANE + LANE
W_KV = 2 * LANE + 3 * LANE
W_PROJ = W_SSM + W_MLA + W_RET + W_NSA + 4 * NSA_DK

NSA_TILE = 512
NSA_SINK = 128
MLA_TILE = 1024
MLA_Q_TILE = 1024
VMEM_LIMIT = 48 * 1024 * 1024


def _cparams(*sem):
    return pltpu.CompilerParams(dimension_semantics=sem, vmem_limit_bytes=VMEM_LIMIT)


def _mm(a, b):
    return jnp.dot(a.astype(BF16), b.astype(BF16), preferred_element_type=F32)


def _mm_nt(a, b):
    return lax.dot_general(a.astype(BF16), b.astype(BF16), (((1,), (1,)), ((), ())),
                           preferred_element_type=F32)


def _mm_f32(a, b):
    return jnp.dot(a, b, precision=HIGHEST, preferred_element_type=F32)


def _silu(x):
    return x * jax.nn.sigmoid(x)


def _softplus(x):
    return jnp.maximum(x, 0.0) + jnp.log1p(jnp.exp(-jnp.abs(x)))


def _layer_norm(x, g, b):
    mu = jnp.mean(x, axis=-1, keepdims=True)
    xc = x - mu
    var = jnp.mean(xc * xc, axis=-1, keepdims=True)
    return xc * lax.rsqrt(var + EPS) * g + b


def _iota(shape, dim):
    return lax.broadcasted_iota(jnp.int32, shape, dim)


def _ln_kernel(x_ref, g_ref, b_ref, o_ref):
    o_ref[...] = _layer_norm(x_ref[...], g_ref[...], b_ref[...])


def _entry_ln(x2, g, b, tm=512):
    T, D = x2.shape
    return pl.pallas_call(
        _ln_kernel, name="entry_ln",
        grid=(T // tm,),
        in_specs=[pl.BlockSpec((tm, D), lambda i: (i, 0)),
                  pl.BlockSpec((1, D), lambda i: (0, 0)),
                  pl.BlockSpec((1, D), lambda i: (0, 0))],
        out_specs=pl.BlockSpec((tm, D), lambda i: (i, 0)),
        out_shape=jax.ShapeDtypeStruct((T, D), F32),
        compiler_params=_cparams("parallel"),
    )(x2, g.reshape(1, D), b.reshape(1, D))


def _pad_cols(w, width):
    return jnp.pad(w, ((0, 0), (0, width - w.shape[1])))


def _layout_w_in(w):
    offs = np.concatenate([[0], np.cumsum(IN_SPLITS)])
    p = [w[:, int(offs[i]):int(offs[i + 1])] for i in range(len(IN_SPLITS))]
    (ssm_z, ssm_xbc, ssm_dt, mla_cq, mla_ckv, mla_kr, ret_q, ret_k, ret_v, ret_g,
     nsa_q, nsa_kc, nsa_vc, nsa_ks, nsa_vs, nsa_kw, nsa_vw, nsa_gate) = p
    nsa_q_heads = [_pad_cols(nsa_q[:, h * NSA_DK:(h + 1) * NSA_DK], LANE) for h in range(NSA_HEADS)]
    cols = [ssm_z, ssm_xbc, _pad_cols(ssm_dt, LANE),
            mla_cq, mla_ckv, _pad_cols(mla_kr, LANE),
            ret_q, ret_k, ret_v, ret_g,
            *nsa_q_heads, nsa_kc, nsa_vc, _pad_cols(nsa_gate, LANE),
            nsa_ks, nsa_vs, nsa_kw, nsa_vw]
    out = jnp.concatenate(cols, axis=1)
    assert out.shape[1] == W_PROJ
    return out.astype(BF16)


def _in_proj_kernel(h_ref, w_ref, kvtab_ref, ssm_ref, mla_ref, ret_ref, nsa_ref, kv_ref):
    hb = h_ref[...].astype(BF16)
    off = 0
    for ref, width in ((ssm_ref, W_SSM), (mla_ref, W_MLA), (ret_ref, W_RET), (nsa_ref, W_NSA)):
        ref[...] = jnp.dot(hb, w_ref[:, off:off + width], preferred_element_type=F32)
        off += width
    kv = jnp.dot(hb, w_ref[:, off:off + 4 * NSA_DK], preferred_element_type=F32)
    low = _iota((kv.shape[0], LANE), 1) < NSA_DK
    sel_kv, win_kv = kv[:, :LANE], kv[:, LANE:]
    pieces = {0: sel_kv, 2: pltpu.roll(sel_kv, NSA_DK, 1), 3: win_kv, 4: pltpu.roll(win_kv, NSA_DK, 1)}
    for slab in range(W_KV // LANE):
        lanes = slice(slab * LANE, (slab + 1) * LANE)
        tab = kvtab_ref[:, lanes]
        if slab in pieces:
            kv_ref[:, lanes] = (jnp.where(low, pieces[slab], 0.0) + tab.astype(F32)).astype(BF16)
        else:
            kv_ref[:, lanes] = tab


def _in_proj(h2, w_p, kv_table, tm=256):
    T, D = h2.shape
    S = kv_table.shape[0]
    widths = (W_SSM, W_MLA, W_RET, W_NSA)
    return pl.pallas_call(
        _in_proj_kernel, name="in_proj",
        grid=(T // tm,),
        in_specs=[pl.BlockSpec((tm, D), lambda i: (i, 0)),
                  pl.BlockSpec((D, W_PROJ), lambda i: (0, 0)),
                  pl.BlockSpec((tm, W_KV), lambda i: (i % (S // tm), 0))],
        out_specs=[pl.BlockSpec((tm, w), lambda i: (i, 0)) for w in widths + (W_KV,)],
        out_shape=[jax.ShapeDtypeStruct((T, w), F32) for w in widths] + [jax.ShapeDtypeStruct((T, W_KV), BF16)],
        compiler_params=_cparams("parallel"),
    )(h2, w_p, kv_table)


def _ssm_chunk(p_ref, cw_ref, cb_ref, dtb_ref, alog_ref, dskip_ref, ng_ref, o_ref, state_ref, ext_ref):
    L, H, P, N = SSM_CHUNK, SSM_HEADS, SSM_HEAD_DIM, SSM_STATE
    z = p_ref[:, 0:GROUP_W]
    ext_ref[8:8 + L, :] = p_ref[:, GROUP_W:GROUP_W + SSM_XBC]
    conv = cb_ref[...]
    for j in range(SSM_CONV):
        conv = conv + ext_ref[pl.ds(8 - (SSM_CONV - 1) + j, L), :] * cw_ref[j:j + 1, :]
    ext_ref[0:8, :] = ext_ref[L:L + 8, :]
    xbc = _silu(conv)
    xs = xbc[:, 0:GROUP_W]
    b_in = xbc[:, GROUP_W:GROUP_W + SSM_GROUPS * N]
    c_in = xbc[:, GROUP_W + SSM_GROUPS * N:]

    dt = _softplus(p_ref[:, GROUP_W + SSM_XBC:] + dtb_ref[...])
    a = dt * (-jnp.exp(alog_ref[...]))
    row = _iota((L, L), 0)
    col = _iota((L, L), 1)
    tril = col <= row
    cs = _mm_f32(jnp.where(tril, 1.0, 0.0), a)
    cs_t = cs.T
    ecs = jnp.exp(cs)
    dte = jnp.exp(cs[L - 1:L, :] - cs)
    expand = jnp.where(_iota((LANE, H * P), 0) == _iota((LANE, H * P), 1) // P, 1.0, 0.0)
    dt_x = _mm_f32(dt, expand)
    ecs_x = _mm_f32(ecs, expand)
    dte_x = _mm_f32(dte, expand)

    xdt = xs * dt_x
    wx = xdt * dte_x
    head_of_lane = _iota((L, H * P), 1) // P
    y = xs * dskip_ref[...]
    y_off = []
    rep = H // SSM_GROUPS
    for g in range(SSM_GROUPS):
        cg = c_in[:, g * N:(g + 1) * N]
        bg = b_in[:, g * N:(g + 1) * N]
        cb = _mm_nt(cg, bg)
        for h in range(g * rep, (g + 1) * rep):
            diff = cs[:, h:h + 1] - cs_t[h:h + 1, :]
            seg = jnp.where(tril, jnp.exp(jnp.where(tril, diff, 0.0)), 0.0)
            yh = _mm(cb * seg, xdt)
            y = y + jnp.where(head_of_lane == h, yh, 0.0)
        lanes = slice(g * rep * P, (g + 1) * rep * P)
        st_prev = state_ref[:, lanes]
        y_off.append(_mm(cg, st_prev))
        state_ref[:, lanes] = st_prev * ecs_x[L - 1:L, lanes] + _mm(bg.T, wx[:, lanes])
    y = y + jnp.concatenate(y_off, axis=1) * ecs_x
    y = y * _silu(z)
    ms = jnp.mean(y * y, axis=-1, keepdims=True)
    o_ref[...] = (y * lax.rsqrt(ms + EPS) * ng_ref[...]).astype(o_ref.dtype)


def _recurrent_kernel(ps_ref, cw_ref, cb_ref, dtb_ref, alog_ref, dskip_ref, ng_ref,
                      pr_ref, cos_ref, sin_ref, dec_ref, zeta_ref, xi_ref, cd_ref,
                      oa_ref, oc_ref, sstate_ref, ext_ref, rstate_ref):
    @pl.when(pl.program_id(0) == 0)
    def _():
        sstate_ref[...] = jnp.zeros_like(sstate_ref)
        rstate_ref[...] = jnp.zeros_like(rstate_ref)
        ext_ref[:, 0:8, :] = jnp.zeros((ext_ref.shape[0], 8, SSM_XBC), F32)

    for b in range(ps_ref.shape[0]):
        _ssm_chunk(ps_ref.at[b], cw_ref, cb_ref, dtb_ref, alog_ref, dskip_ref, ng_ref,
                   oa_ref.at[b], sstate_ref.at[b], ext_ref.at[b])
        _ret_chunk(pr_ref.at[b], cos_ref, sin_ref, dec_ref, zeta_ref, xi_ref, cd_ref, oc_ref.at[b], rstate_ref.at[b])


def _ret_chunk(p_ref, cos_ref, sin_ref, dec_ref, zeta_ref, xi_ref, cd_ref, o_ref, state_ref):
    L, H, DK, DV = RET_CHUNK, RET_HEADS, RET_DK, RET_DV
    W = H * DK
    q = p_ref[:, 0:W]
    k = p_ref[:, W:2 * W]
    v = p_ref[:, 2 * W:3 * W]
    gate = p_ref[:, 3 * W:4 * W]
    lane = _iota((L, W), 1)
    first_half = (lane % DK) < (DK // 2)
    head_of_lane = lane // DK

    def rope(x):
        partner = jnp.where(first_half, pltpu.roll(x, W - DK // 2, 1), pltpu.roll(x, DK // 2, 1))
        return x * cos_ref[...] + partner * sin_ref[...]

    qr = rope(q)
    kr = rope(k) * (DK ** -0.5)
    y = jnp.zeros((L, H * DV), F32)
    for h in range(H):
        qh = jnp.where(head_of_lane == h, qr, 0.0)
        sc = _mm_nt(qh, kr) * dec_ref[h]
        y = y + jnp.where(head_of_lane == h, _mm(sc, v), 0.0)
    st = state_ref[...]
    y = y + _mm(qr * xi_ref[...], st)
    same_head = (_iota((W, H * DV), 0) // DK) == (_iota((W, H * DV), 1) // DV)
    kv = _mm((kr * zeta_ref[...]).T, v)
    state_ref[...] = st * cd_ref[...] + jnp.where(same_head, kv, 0.0)
    ms = _mm_f32(y * y, jnp.where(same_head, 1.0 / DV, 0.0))
    o_ref[...] = (y * lax.rsqrt(ms + EPS) * _silu(gate)).astype(o_ref.dtype)


def _ret_tables(S):
    H, DK, L = RET_HEADS, RET_DK, RET_CHUNK
    inv = ROPE_THETA ** (-np.arange(0, DK, 2, dtype=np.float64) / DK)
    ang = np.arange(S, dtype=np.float64)[:, None] * inv[None, :]
    cos, sin = np.cos(ang), np.sin(ang)
    cos_t = np.tile(np.concatenate([cos, cos], axis=1), (1, H))
    sin_t = np.tile(np.concatenate([-sin, sin], axis=1), (1, H))
    log_gamma = np.log1p(-np.exp2(-5.0 - np.arange(H, dtype=np.float64)))
    pos = np.arange(L, dtype=np.float64)
    diff = pos[:, None] - pos[None, :]
    decay_in = np.where(diff >= 0, np.exp(np.maximum(diff, 0.0)[None] * log_gamma[:, None, None]), 0.0)
    zeta = np.exp((L - 1 - pos)[None] * log_gamma[:, None])
    xi = np.exp((pos + 1.0)[None] * log_gamma[:, None])
    chunk_decay = np.exp(L * log_gamma)
    zeta_x = np.repeat(zeta.T, DK, axis=1)
    xi_x = np.repeat(xi.T, DK, axis=1)
    cd_x = np.repeat(chunk_decay, RET_DV).reshape(1, H * RET_DV)
    return tuple(jnp.asarray(t, dtype=F32) for t in (cos_t, sin_t, decay_in, zeta_x, xi_x, cd_x))


def _ssm_and_retention(p_ssm, conv_w, conv_b, dt_bias, a_log, d_skip, norm_g, p_ret, tables):
    B, S, _ = p_ssm.shape
    L, H = SSM_CHUNK, RET_HEADS
    assert RET_CHUNK == L
    W = H * RET_DK
    cos_t, sin_t, decay_in, zeta_x, xi_x, cd_x = tables
    pad_h = lambda v: jnp.pad(v, (0, LANE - SSM_HEADS)).reshape(1, LANE)
    const = lambda shape: pl.BlockSpec(shape, lambda c: (0,) * len(shape))
    chunk = lambda width: pl.BlockSpec((B, L, width), lambda c: (0, c, 0))
    out_shape = jax.ShapeDtypeStruct((B, S, GROUP_W), BF16)
    return pl.pallas_call(
        _recurrent_kernel, name="ssm_retention",
        grid=(S // L,),
        in_specs=[chunk(W_SSM),
                  const((SSM_CONV, SSM_XBC)), const((1, SSM_XBC)), const((1, LANE)), const((1, LANE)),
                  const((1, GROUP_W)), const((1, GROUP_W)),
                  chunk(W_RET),
                  pl.BlockSpec((L, W), lambda c: (c, 0)), pl.BlockSpec((L, W), lambda c: (c, 0)),
                  const((H, L, L)), const((L, W)), const((L, W)), const((1, H * RET_DV))],
        out_specs=[chunk(GROUP_W), chunk(GROUP_W)],
        out_shape=[out_shape, out_shape],
        scratch_shapes=[pltpu.VMEM((B, SSM_STATE, GROUP_W), F32),
                        pltpu.VMEM((B, L + 8, SSM_XBC), F32),
                        pltpu.VMEM((B, W, H * RET_DV), F32)],
        compiler_params=_cparams("arbitrary"),
    )(p_ssm, conv_w, conv_b.reshape(1, -1), pad_h(dt_bias), pad_h(a_log),
      jnp.repeat(d_skip, SSM_HEAD_DIM).reshape(1, GROUP_W), norm_g.reshape(1, GROUP_W),
      p_ret, cos_t, sin_t, decay_in, zeta_x, xi_x, cd_x)


def _mla_prep_kernel(p_ref, qg_ref, wq_ref, kvg_ref, wk_ref, wv_ref, cos_ref, sin_ref,
                     q_ref, k_ref, v_ref):
    tm = p_ref.shape[0]
    cq = p_ref[:, 0:MLA_Q_RANK]
    ckv = p_ref[:, MLA_Q_RANK:MLA_Q_RANK + MLA_KV_RANK]
    kr = p_ref[:, MLA_Q_RANK + MLA_KV_RANK:]

    def rms(x, g):
        return x * lax.rsqrt(jnp.mean(x * x, axis=-1, keepdims=True) + EPS) * g

    q = _mm(rms(cq, qg_ref[...]), wq_ref[...])
    kvl = rms(ckv, kvg_ref[...])
    kn = _mm(kvl, wk_ref[...])
    vv = _mm(kvl, wv_ref[...])
    kr_sh = pltpu.roll(kr, MLA_NOPE, 1)
    lane = _iota((tm, LANE), 1)
    half = MLA_ROPE // 2
    low = (lane >= MLA_NOPE) & (lane < MLA_NOPE + half)
    cos = cos_ref[...]
    sin = sin_ref[...]

    def rope(x):
        partner = jnp.where(low, pltpu.roll(x, LANE - half, 1), pltpu.roll(x, half, 1))
        return x * cos + partner * sin

    scale = (MLA_NOPE + MLA_ROPE) ** -0.5 * LOG2_E
    for h in range(MLA_HEADS):
        sl = slice(h * LANE, (h + 1) * LANE)
        q_ref[h] = (rope(q[:, sl]) * scale).astype(BF16)
        k_ref[h] = rope(kn[:, sl] + kr_sh).astype(BF16)
        v_ref[h] = jnp.where(lane == _mla_ones_lane(h), 1.0, vv[:, sl]).astype(BF16)


def _mla_tables(S):
    inv = ROPE_THETA ** (-np.arange(0, MLA_ROPE, 2, dtype=np.float64) / MLA_ROPE)
    ang = np.arange(S, dtype=np.float64)[:, None] * inv[None, :]
    cos, sin = np.cos(ang), np.sin(ang)
    tail = LANE - MLA_NOPE - MLA_ROPE
    cos_t = np.concatenate([np.ones((S, MLA_NOPE)), cos, cos, np.ones((S, tail))], axis=1)
    sin_t = np.concatenate([np.zeros((S, MLA_NOPE)), -sin, sin, np.zeros((S, tail))], axis=1)
    return jnp.asarray(cos_t, dtype=F32), jnp.asarray(sin_t, dtype=F32)


def _layout_mla_weights(w_uq, w_ukv):
    H = MLA_HEADS
    dq = MLA_NOPE + MLA_ROPE
    wq = jnp.concatenate([_pad_cols(w_uq[:, h * dq:(h + 1) * dq], LANE) for h in range(H)], axis=1)
    dkv = MLA_NOPE + MLA_V
    wk, wv = [], []
    for h in range(H):
        blk = w_ukv[:, h * dkv:(h + 1) * dkv]
        wk.append(_pad_cols(blk[:, :MLA_NOPE], LANE))
        v = blk[:, MLA_NOPE:]
        zero = jnp.zeros_like(v)
        wv.append(jnp.concatenate([v, zero] if h % 2 == 0 else [zero, v], axis=1))
    return wq.astype(BF16), jnp.concatenate(wk, axis=1).astype(BF16), jnp.concatenate(wv, axis=1).astype(BF16)


def _mla_prep(p_mla, q_norm_g, kv_norm_g, wq, wk, wv, tables, tm=512):
    B, S, _ = p_mla.shape
    H = MLA_HEADS
    cos_t, sin_t = tables
    const = lambda shape: pl.BlockSpec(shape, lambda b, i: (0,) * len(shape))
    qkv_spec = pl.BlockSpec((None, H, tm, LANE), lambda b, i: (b, 0, i, 0))
    qkv_shape = jax.ShapeDtypeStruct((B, H, S, LANE), BF16)
    return pl.pallas_call(
        _mla_prep_kernel, name="mla_prep",
        grid=(B, S // tm),
        in_specs=[pl.BlockSpec((None, tm, W_MLA), lambda b, i: (b, i, 0)),
                  const((1, MLA_Q_RANK)), const((MLA_Q_RANK, H * LANE)),
                  const((1, MLA_KV_RANK)), const((MLA_KV_RANK, H * LANE)), const((MLA_KV_RANK, H * LANE)),
                  pl.BlockSpec((tm, LANE), lambda b, i: (i, 0)),
                  pl.BlockSpec((tm, LANE), lambda b, i: (i, 0))],
        out_specs=[qkv_spec, qkv_spec, qkv_spec],
        out_shape=[qkv_shape, qkv_shape, qkv_shape],
        compiler_params=_cparams("parallel", "parallel"),
    )(p_mla, q_norm_g.reshape(1, -1), wq, kv_norm_g.reshape(1, -1), wk, wv, cos_t, sin_t)


def _mla_ones_lane(h):
    return MLA_V if h % 2 == 0 else 0


def _mla_attn_kernel(qi_ref, kj_ref, q_ref, k_ref, v_ref, o_ref, m_ref, acc_ref):
    H = MLA_HEADS
    tq, tk = q_ref.shape[1], k_ref.shape[1]
    i = qi_ref[pl.program_id(1)]
    j = kj_ref[pl.program_id(1)]

    @pl.when(j == 0)
    def _():
        m_ref[...] = jnp.full(m_ref.shape, NEG_INF, F32)
        acc_ref[...] = jnp.zeros_like(acc_ref)

    def sweep(masked):
        if masked:
            causal = (_iota((tq, tk), 1) - _iota((tq, tk), 0)) <= i * tq - j * tk
        scores = [lax.dot_general(q_ref[h], k_ref[h], (((1,), (1,)), ((), ())), preferred_element_type=F32)
                  for h in range(H)]
        for h in range(H):
            s = scores[h]
            if masked:
                s = jnp.where(causal, s, NEG_INF)
            m_prev = m_ref[h]
            m_new = jnp.maximum(m_prev, jnp.max(s, axis=-1, keepdims=True))
            p = jnp.exp2(s - jnp.tile(m_new, (1, tk // LANE)))
            acc_ref[h] = (jnp.exp2(m_prev - m_new) * acc_ref[h]
                          + jnp.dot(p.astype(BF16), v_ref[h], preferred_element_type=F32))
            m_ref[h] = m_new

    ratio = tq // tk

    @pl.when(j < ratio * i)
    def _():
        sweep(False)

    @pl.when(j >= ratio * i)
    def _():
        sweep(True)

    @pl.when(j == ratio * i + ratio - 1)
    def _():
        lane = _iota((tq, LANE), 1)
        for pair in range(H // 2):
            he, ho = 2 * pair, 2 * pair + 1
            acc_e, acc_o = acc_ref[he], acc_ref[ho]
            le = acc_e[:, _mla_ones_lane(he):_mla_ones_lane(he) + 1]
            lo = acc_o[:, _mla_ones_lane(ho):_mla_ones_lane(ho) + 1]
            o_ref[:, pair * LANE:(pair + 1) * LANE] = jnp.where(lane < MLA_V, acc_e / le, acc_o / lo).astype(o_ref.dtype)


def _mla_attn(q, k, v):
    B, H, S, _ = q.shape
    t = min(MLA_TILE, S)
    tq = min(MLA_Q_TILE, S)
    pairs = [(i, j) for i in range(S // tq) for j in range((i + 1) * tq // t)]
    qi = jnp.asarray([p[0] for p in pairs], jnp.int32)
    kj = jnp.asarray([p[1] for p in pairs], jnp.int32)
    grid_spec = pltpu.PrefetchScalarGridSpec(
        num_scalar_prefetch=2,
        grid=(B, len(pairs)),
        in_specs=[pl.BlockSpec((None, H, tq, LANE), lambda b, p, qi, kj: (b, 0, qi[p], 0)),
                  pl.BlockSpec((None, H, t, LANE), lambda b, p, qi, kj: (b, 0, kj[p], 0)),
                  pl.BlockSpec((None, H, t, LANE), lambda b, p, qi, kj: (b, 0, kj[p], 0))],
        out_specs=pl.BlockSpec((None, tq, GROUP_W), lambda b, p, qi, kj: (b, qi[p], 0)),
        scratch_shapes=[pltpu.VMEM((H, tq, LANE), F32), pltpu.VMEM((H, tq, LANE), F32)],
    )
    return pl.pallas_call(
        _mla_attn_kernel, name="mla_attn",
        grid_spec=grid_spec,
        out_shape=jax.ShapeDtypeStruct((B, S, GROUP_W), BF16),
        compiler_params=_cparams("parallel", "arbitrary"),
    )(qi, kj, q, k, v)


POS_HI = NSA_DK
POS_LO = NSA_DK + 3
POS_ONE = NSA_DK + 6
ONES_LANE = NSA_DV


def _split_bf16(x, parts=3):
    out, rem = [], np.float64(x)
    for _ in range(parts):
        piece = np.float64(np.float32(rem).astype(jnp.bfloat16).astype(np.float32))
        out.append(float(piece))
        rem = rem - piece
    return out


def _nsa_query_table():
    H = NSA_HEADS
    tab = np.zeros((2 * H, LANE), np.float32)
    for h in range(H):
        c = 2.0 ** (-8.0 * (h + 1) / H) * LOG2_E
        pieces = _split_bf16(c)
        tab[h, POS_HI:POS_HI + 3] = pieces
        tab[h, POS_LO:POS_LO + 3] = pieces
        tab[H + h, POS_ONE] = -sum(pieces)
    return jnp.asarray(tab)


def _nsa_pos_lanes(pos, lo_offset=0.0):
    t = np.zeros((len(pos), LANE - NSA_DK), np.float32)
    t[:, POS_HI - NSA_DK:POS_HI - NSA_DK + 3] = (NSA_SLC_LEN * (pos // NSA_SLC_LEN))[:, None]
    t[:, POS_LO - NSA_DK:POS_LO - NSA_DK + 3] = (pos % NSA_SLC_LEN + lo_offset)[:, None]
    t[:, POS_ONE - NSA_DK] = 1.0
    return t


def _nsa_queries(q_ref, qtab_ref, qb):
    Q, H = q_ref.shape[0], NSA_HEADS
    qpos = (qb * Q + _iota((Q, 1), 0)).astype(F32)
    out = []
    for h in range(H):
        q = q_ref[:, h * LANE:(h + 1) * LANE] * (NSA_DK ** -0.5 * LOG2_E)
        out.append((q + qtab_ref[h:h + 1, :] + qtab_ref[H + h:H + h + 1, :] * qpos).astype(BF16))
    return out


def _normalise(o):
    return o / o[:, ONES_LANE:ONES_LANE + 1]


def _unstack_heads(o):
    Q = o.shape[0] // NSA_HEADS
    lane = _iota((Q, LANE), 1)
    out = []
    for pair in range(NSA_HEADS // 2):
        even = o[(2 * pair) * Q:(2 * pair + 1) * Q]
        odd = o[(2 * pair + 1) * Q:(2 * pair + 2) * Q]
        out.append(jnp.where(lane < NSA_DV, even, pltpu.roll(odd, NSA_DV, 1)))
    return jnp.concatenate(out, axis=1)


def _nsa_cmp_kernel(uk_ref, uv_ref, pek_ref, pev_ref, w1k_ref, w1v_ref, w2k_ref, w2v_ref, cpos_ref,
                    kc_ref, vc_ref, sh_ref):
    nb = uk_ref.shape[0]
    half = uk_ref.shape[1]

    def hidden(u_ref, pe_ref, w1_ref):
        u = u_ref[...]
        first = _mm(u + pe_ref[0:1, :], w1_ref[0:half, :])
        second = _mm(u + pe_ref[1:2, :], w1_ref[half:2 * half, :])
        sh_ref[0:nb, :] = second
        sh_ref[nb:nb + 8, :] = jnp.zeros((8, NSA_CMP_HID), F32)
        return first + sh_ref[pl.ds(1, nb), :]

    hk = _silu(hidden(uk_ref, pek_ref, w1k_ref))
    hv = _silu(hidden(uv_ref, pev_ref, w1v_ref))
    kc_ref[...] = (_mm(hk, w2k_ref[...]) + cpos_ref[...]).astype(BF16)
    ones_lane = jnp.where(_iota((1, LANE), 1) == ONES_LANE, 1.0, 0.0)
    vc_ref[...] = (_mm(hv, w2v_ref[...]) + ones_lane).astype(BF16)


def _nsa_compress(uk, uv, pe_k, w1_k, w2_k, pe_v, w1_v, w2_v):
    B, nb, half = uk.shape
    hid = NSA_CMP_HID
    const = lambda shape: pl.BlockSpec(shape, lambda b: (0,) * len(shape))
    w2k = _pad_cols(w2_k, LANE).astype(BF16)
    w2v = _pad_cols(w2_v, LANE).astype(BF16)
    centre = _nsa_pos_lanes(np.arange(nb) * NSA_CMP_STRIDE, 0.5 * (NSA_CMP_LEN - 1))
    cpos = jnp.asarray(np.concatenate([np.zeros((nb, NSA_DK), np.float32), centre], axis=1))
    out_spec = pl.BlockSpec((None, nb, LANE), lambda b: (b, 0, 0))
    out_shape = jax.ShapeDtypeStruct((B, nb, LANE), BF16)
    return pl.pallas_call(
        _nsa_cmp_kernel, name="nsa_compress",
        grid=(B,),
        in_specs=[pl.BlockSpec((None, nb, half), lambda b: (b, 0, 0)),
                  pl.BlockSpec((None, nb, half), lambda b: (b, 0, 0)),
                  const((2, half)), const((2, half)),
                  const((2 * half, hid)), const((2 * half, hid)),
                  const((hid, LANE)), const((hid, LANE)), const((nb, LANE))],
        out_specs=[out_spec, out_spec],
        out_shape=[out_shape, out_shape],
        scratch_shapes=[pltpu.VMEM((nb + 8, hid), F32)],
        compiler_params=_cparams("parallel"),
    )(uk, uv, pe_k.reshape(2, half), pe_v.reshape(2, half), w1_k.astype(BF16), w1_v.astype(BF16), w2k, w2v, cpos)


def _nsa_sel_kernel(q_ref, qtab_ref, kc_ref, vc_ref, ovt_ref, oc_ref, selb_ref, any_ref, imp_ref, *, n_slc, top_n):
    Q, H = q_ref.shape[0], NSA_HEADS
    qb = pl.program_id(1)
    nc = kc_ref.shape[0]
    qs = jnp.concatenate(_nsa_queries(q_ref, qtab_ref, qb), axis=0)
    nt_dims = (((1,), (1,)), ((), ()))

    def attend(ncols):
        s = lax.dot_general(qs, kc_ref[0:ncols, :], nt_dims, preferred_element_type=F32)
        qpos = qb * Q + (_iota((H * Q, ncols), 0) & (Q - 1))
        block_end = _iota((H * Q, ncols), 1) * NSA_CMP_STRIDE + (NSA_CMP_LEN - 1)
        s = jnp.where(block_end <= qpos, s, NEG_INF)
        e = jnp.exp2(s - jnp.max(s, axis=-1, keepdims=True))
        qpos_col = qb * Q + (_iota((H * Q, 1), 0) & (Q - 1))
        has_block = jnp.where(qpos_col >= NSA_CMP_LEN - 1, 1.0, 0.0)
        p = e * (has_block / jnp.sum(e, axis=-1, keepdims=True))
        oc_ref[...] = _unstack_heads(jnp.dot(p.astype(BF16), vc_ref[0:ncols, :], preferred_element_type=F32))
        p_sum = p[0:Q]
        for h in range(1, H):
            p_sum = p_sum + p[h * Q:(h + 1) * Q]
        imp_ref[...] = lax.dot_general(ovt_ref[:, 0:ncols], p_sum, nt_dims, precision=HIGHEST,
                                       preferred_element_type=F32)

    tiles_needed = ((qb + 1) * Q // NSA_CMP_STRIDE + LANE - 1) // LANE
    for tiles in range(1, nc // LANE + 1):
        pl.when(tiles_needed == tiles)(functools.partial(attend, tiles * LANE))

    imp = imp_ref[...]
    blk = _iota((LANE, Q), 0)
    q_blk = (qb * Q + _iota((LANE, Q), 1)) >> int(math.log2(NSA_SLC_LEN))
    causal = blk <= q_blk
    for forced_blk in (0, q_blk, q_blk - 1):
        imp = jnp.where(blk == forced_blk, FORCED_SCORE, imp)
    imp = jnp.where(causal, imp, -1.0)
    imp = jnp.where(blk < n_slc, imp, -2.0)
    blk_f = blk.astype(F32)
    sel = jnp.zeros((LANE, Q), F32)
    for _ in range(top_n):
        m = jnp.max(imp, axis=0, keepdims=True)
        first = jnp.min(jnp.where(imp == m, blk_f, float(LANE)), axis=0, keepdims=True)
        hit = blk_f == first
        sel = jnp.where(hit, 1.0, sel)
        imp = jnp.where(hit, -3.0, imp)
    sel = jnp.where(causal, sel, 0.0).T
    selb_ref[...] = jnp.where(sel > 0.5, 0.0, NEG_INF).astype(BF16)
    any_ref[...] = jnp.max(sel, axis=0, keepdims=True)


def _nsa_select(p_nsa, qtab, kc, vc, overlap_t):
    B, S, _ = p_nsa.shape
    Q = NSA_Q
    nqb = S // Q
    nc = kc.shape[1]
    n_slc = S // NSA_SLC_LEN
    kern = functools.partial(_nsa_sel_kernel, n_slc=n_slc, top_n=min(NSA_TOPN, n_slc))
    return pl.pallas_call(
        kern, name="nsa_select",
        grid=(B, nqb),
        in_specs=[pl.BlockSpec((None, Q, NSA_HEADS * LANE), lambda b, i: (b, i, 0)),
                  pl.BlockSpec((2 * NSA_HEADS, LANE), lambda b, i: (0, 0)),
                  pl.BlockSpec((None, nc, LANE), lambda b, i: (b, 0, 0)),
                  pl.BlockSpec((None, nc, LANE), lambda b, i: (b, 0, 0)),
                  pl.BlockSpec((LANE, nc), lambda b, i: (0, 0))],
        out_specs=[pl.BlockSpec((None, Q, GROUP_W), lambda b, i: (b, i, 0)),
                   pl.BlockSpec((None, Q, LANE), lambda b, i: (b, i, 0)),
                   pl.BlockSpec((None, None, 1, LANE), lambda b, i: (b, i, 0, 0))],
        out_shape=[jax.ShapeDtypeStruct((B, S, GROUP_W), F32),
                   jax.ShapeDtypeStruct((B, S, LANE), BF16),
                   jax.ShapeDtypeStruct((B, nqb, 1, LANE), F32)],
        scratch_shapes=[pltpu.VMEM((LANE, Q), F32)],
        compiler_params=_cparams("parallel", "parallel"),
    )(p_nsa, qtab, kc, vc, overlap_t)


def _nsa_attn_kernel(flags_ref, q_ref, qtab_ref, gate_ref, oc_ref, selb_ref, ks_ref, vs_ref, kw_ref, vw_ref,
                     o_ref, m_ref, acc_ref, *, nt):
    Q, H, TK = q_ref.shape[0], NSA_HEADS, NSA_TILE
    PART = 2 * Q
    b = pl.program_id(0)
    qb = pl.program_id(1)
    nqb = pl.num_programs(1)
    qh = _nsa_queries(q_ref, qtab_ref, qb)
    nt_dims = (((1,), (1,)), ((), ()))

    selb = selb_ref[...]
    qs_sel = jnp.concatenate([jnp.concatenate([q, selb], axis=1) for q in qh], axis=0)
    sink_off = jnp.where(_iota(selb.shape, 1) < NSA_SINK // NSA_SLC_LEN, NEG_INF, selb.astype(F32)).astype(BF16)
    qs_loop = jnp.concatenate([jnp.concatenate([q, sink_off], axis=1) for q in qh], axis=0)
    m_ref[...] = jnp.full(m_ref.shape, NEG_INF, F32)
    acc_ref[...] = jnp.zeros_like(acc_ref)

    def update(t, diagonal):
        rows = pl.ds(pl.multiple_of(t * TK, TK), TK)
        k_tile, v_tile = ks_ref[rows, :], vs_ref[rows, :]
        parts = [slice(i * PART, (i + 1) * PART) for i in range(H * Q // PART)]
        qs_t = qs_sel if diagonal else qs_loop
        scores = [lax.dot_general(qs_t[r], k_tile, nt_dims, preferred_element_type=F32) for r in parts]
        if diagonal:
            k_sink, v_sink = ks_ref[0:NSA_SINK, :], vs_ref[0:NSA_SINK, :]
            sink_scores = [lax.dot_general(qs_sel[r], k_sink, nt_dims, preferred_element_type=F32) for r in parts]
            sink_bias = jnp.where(t > 0, 0.0, NEG_INF)
        for idx, (r, s) in enumerate(zip(parts, scores)):
            m_prev = m_ref[r]
            if diagonal:
                ahead = _iota((PART, TK), 1) - (_iota((PART, TK), 0) & (Q - 1))
                s = jnp.where(ahead <= qb * Q - t * TK, s, NEG_INF)
                s_sink = sink_scores[idx] + sink_bias
                m_prev = jnp.maximum(m_prev, jnp.max(s_sink, axis=-1, keepdims=True))
            m_new = jnp.maximum(m_prev, jnp.max(s, axis=-1, keepdims=True))
            p = jnp.exp2(s - jnp.tile(m_new, (1, TK // LANE)))
            update_acc = jnp.dot(p.astype(BF16), v_tile, preferred_element_type=F32)
            if diagonal:
                p_sink = jnp.exp2(s_sink - jnp.tile(m_new, (1, NSA_SINK // LANE)))
                update_acc = update_acc + jnp.dot(p_sink.astype(BF16), v_sink, preferred_element_type=F32)
            acc_ref[r] = jnp.exp2(m_ref[r] - m_new) * acc_ref[r] + update_acc
            m_ref[r] = m_new

    def tile(t, carry):
        @pl.when(flags_ref[(b * nqb + qb) * nt + t] > 0)
        def _():
            update(t, False)
        return carry

    t_diag = (qb * Q) // TK
    lax.fori_loop(0, t_diag, tile, 0)
    update(t_diag, True)
    o_s = _unstack_heads(_normalise(acc_ref[...]))

    W = Q + NSA_WIN
    start = jnp.maximum(qb * Q - NSA_WIN, 0)
    rows_w = pl.ds(pl.multiple_of(start, Q), W)
    qs_win = jnp.concatenate(qh, axis=0)
    s = lax.dot_general(qs_win, kw_ref[rows_w, :], nt_dims, preferred_element_type=F32)
    lead = qb * Q - start
    q_minus_k = (_iota((H * Q, W), 0) & (Q - 1)) - _iota((H * Q, W), 1)
    s = jnp.where(q_minus_k >= -lead, jnp.where(q_minus_k < NSA_WIN - lead, s, NEG_INF), NEG_INF)
    p = jnp.exp2(s - jnp.max(s, axis=-1, keepdims=True))
    o_w = _unstack_heads(_normalise(jnp.dot(p.astype(BF16), vw_ref[rows_w, :], preferred_element_type=F32)))

    gates = jax.nn.sigmoid(gate_ref[...])
    lane_head = _iota((LANE, GROUP_W), 1) // NSA_DV
    src = _iota((LANE, GROUP_W), 0)
    out = None
    for j, branch in enumerate((oc_ref[...], o_s, o_w)):
        g = _mm_f32(gates, jnp.where(src == lane_head * 3 + j, 1.0, 0.0))
        out = g * branch if out is None else out + g * branch
    o_ref[...] = out.astype(o_ref.dtype)


def _nsa_attend(p_nsa, qtab, o_c, selb, flags, kv):
    B, S, _ = p_nsa.shape
    Q = NSA_Q
    nqb = S // Q
    nt = S // NSA_TILE
    gate_blk = (W_NSA - LANE) // LANE
    kern = functools.partial(_nsa_attn_kernel, nt=nt)
    slab = lambda width, col: pl.BlockSpec((None, S, width), lambda b, i, f: (b, 0, col))
    grid_spec = pltpu.PrefetchScalarGridSpec(
        num_scalar_prefetch=1,
        grid=(B, nqb),
        in_specs=[pl.BlockSpec((None, Q, NSA_HEADS * LANE), lambda b, i, f: (b, i, 0)),
                  pl.BlockSpec((2 * NSA_HEADS, LANE), lambda b, i, f: (0, 0)),
                  pl.BlockSpec((None, Q, LANE), lambda b, i, f: (b, i, gate_blk)),
                  pl.BlockSpec((None, Q, GROUP_W), lambda b, i, f: (b, i, 0)),
                  pl.BlockSpec((None, Q, LANE), lambda b, i, f: (b, i, 0)),
                  slab(2 * LANE, 0), slab(LANE, 2), slab(LANE, 3), slab(LANE, 4)],
        out_specs=pl.BlockSpec((None, Q, GROUP_W), lambda b, i, f: (b, i, 0)),
        scratch_shapes=[pltpu.VMEM((NSA_HEADS * Q, LANE), F32), pltpu.VMEM((NSA_HEADS * Q, LANE), F32)],
    )
    return pl.pallas_call(
        kern, name="nsa_attend",
        grid_spec=grid_spec,
        out_shape=jax.ShapeDtypeStruct((B, S, GROUP_W), BF16),
        compiler_params=_cparams("parallel", "parallel"),
    )(flags, p_nsa, qtab, p_nsa, o_c, selb, kv, kv, kv, kv)


def _nsa_tables(S):
    nc = S // NSA_CMP_STRIDE
    n = np.arange(nc)[None, :]
    j = np.arange(LANE)[:, None]
    start = n * NSA_CMP_STRIDE
    ov = (start < (j + 1) * NSA_SLC_LEN) & (start + NSA_CMP_LEN - 1 >= j * NSA_SLC_LEN)
    ov &= (n < (S - NSA_CMP_LEN) // NSA_CMP_STRIDE + 1) & (j < S // NSA_SLC_LEN)
    pos = np.arange(S)
    k_zero = np.zeros((S, NSA_DK), np.float32)
    block_onehot = (pos[:, None] // NSA_SLC_LEN == np.arange(LANE)[None, :]).astype(np.float32)
    v_lanes = np.zeros((S, LANE), np.float32)
    v_lanes[:, ONES_LANE] = 1.0
    kv_table = np.concatenate([k_zero, _nsa_pos_lanes(pos), block_onehot, v_lanes,
                               k_zero, _nsa_pos_lanes(pos), v_lanes], axis=1)
    assert kv_table.shape[1] == W_KV
    return _nsa_query_table(), jnp.asarray(ov.astype(np.float32)), jnp.asarray(kv_table, dtype=BF16)


def _nsa(p_nsa, kv, pe_k, w1_k, w2_k, pe_v, w1_v, w2_v, tables):
    B, S, _ = p_nsa.shape
    qtab, overlap_t, _ = tables
    q_w = NSA_HEADS * LANE
    half = NSA_CMP_STRIDE * NSA_DK
    piece = lambda idx: p_nsa[:, :, q_w + idx * NSA_DK:q_w + (idx + 1) * NSA_DK]
    uk = piece(0).reshape(B, S // NSA_CMP_STRIDE, half)
    uv = piece(1).reshape(B, S // NSA_CMP_STRIDE, half)
    kc, vc = _nsa_compress(uk, uv, pe_k, w1_k, w2_k, pe_v, w1_v, w2_v)
    o_c, selb, blk_any = _nsa_select(p_nsa, qtab, kc, vc, overlap_t)
    per_tile = NSA_TILE // NSA_SLC_LEN
    nt = S // NSA_TILE
    blk_any = blk_any[:, :, 0, :nt * per_tile].at[:, :, :NSA_SINK // NSA_SLC_LEN].set(0.0)
    flags = blk_any.reshape(B, S // NSA_Q, nt, per_tile).max(axis=-1)
    flags = (flags > 0).astype(jnp.int32).reshape(-1)
    return _nsa_attend(p_nsa, qtab, o_c, selb, flags, kv)


def _out_proj_kernel(h_ref, ya_ref, yb_ref, yc_ref, yd_ref, w_ref, g_ref, b_ref, o_ref, wb_ref):
    @pl.when(pl.program_id(0) == 0)
    def _():
        wb_ref[...] = w_ref[...].astype(BF16)

    mix = None
    for idx, y_ref in enumerate((ya_ref, yb_ref, yc_ref, yd_ref)):
        part = _mm(y_ref[...], wb_ref[idx * GROUP_W:(idx + 1) * GROUP_W, :])
        mix = part if mix is None else mix + part
    o_ref[...] = _layer_norm(DEEPNORM_ALPHA * h_ref[...] + mix, g_ref[...], b_ref[...])


def _out_proj(h2, ys, w_out, layer, g, b, tm=512):
    T, D = h2.shape
    row = lambda w: pl.BlockSpec((tm, w), lambda i: (i, 0))
    const = lambda shape: pl.BlockSpec(shape, lambda i: (0,) * len(shape))
    return pl.pallas_call(
        _out_proj_kernel, name="out_proj_ln",
        grid=(T // tm,),
        in_specs=[row(D), row(GROUP_W), row(GROUP_W), row(GROUP_W), row(GROUP_W),
                  pl.BlockSpec((None, D, D), lambda i: (layer, 0, 0)), const((1, D)), const((1, D))],
        out_specs=row(D),
        out_shape=jax.ShapeDtypeStruct((T, D), F32),
        scratch_shapes=[pltpu.VMEM((D, D), BF16)],
        compiler_params=_cparams("arbitrary"),
    )(h2, *ys, w_out, g.reshape(1, D), b.reshape(1, D))


def _mlp_kernel(h_ref, w1_ref, w2_ref, g_ref, b_ref, o_ref, acc_ref):
    f = pl.program_id(1)

    @pl.when(f == 0)
    def _():
        acc_ref[...] = jnp.zeros_like(acc_ref)

    a = jnp.maximum(_mm(h_ref[...], w1_ref[...]), 0.0)
    acc_ref[...] += _mm(a * a, w2_ref[...])

    @pl.when(f == pl.num_programs(1) - 1)
    def _():
        o_ref[...] = _layer_norm(DEEPNORM_ALPHA * h_ref[...] + acc_ref[...], g_ref[...], b_ref[...])


def _mlp(h2, w1, w2, layer, g, b, tm=1024, tf=1024):
    T, D = h2.shape
    F = w1.shape[2]
    return pl.pallas_call(
        _mlp_kernel, name="mlp_ln",
        grid=(T // tm, F // tf),
        in_specs=[pl.BlockSpec((tm, D), lambda i, f: (i, 0)),
                  pl.BlockSpec((None, D, tf), lambda i, f: (layer, 0, f)),
                  pl.BlockSpec((None, tf, D), lambda i, f: (layer, f, 0)),
                  pl.BlockSpec((1, D), lambda i, f: (0, 0)),
                  pl.BlockSpec((1, D), lambda i, f: (0, 0))],
        out_specs=pl.BlockSpec((tm, D), lambda i, f: (i, 0)),
        out_shape=jax.ShapeDtypeStruct((T, D), F32),
        scratch_shapes=[pltpu.VMEM((tm, D), F32)],
        compiler_params=_cparams("parallel", "arbitrary"),
    )(h2, w1, w2, g.reshape(1, D), b.reshape(1, D))


def kernel(x, ln_emb_g, ln_emb_b, w_in, conv_w, conv_b, dt_bias, a_log, d_skip, ssm_norm_g, q_norm_g, w_uq, kv_norm_g, w_ukv, cmp_pe_k, cmp_w1_k, cmp_w2_k, cmp_pe_v, cmp_w1_v, cmp_w2_v, w_out, ln1_g, ln1_b, w_mlp1, w_mlp2, ln2_g, ln2_b):
    B, S, D = x.shape
    assert D == D_MODEL and S % NSA_TILE == 0 and S // NSA_SLC_LEN <= LANE
    T = B * S
    ret_tables = _ret_tables(S)
    mla_tables = _mla_tables(S)
    nsa_tables = _nsa_tables(S)
    h = _entry_ln(x.reshape(T, D), ln_emb_g, ln_emb_b)
    for l in range(w_in.shape[0]):
        p_ssm, p_mla, p_ret, p_nsa, nsa_kv = _in_proj(h, _layout_w_in(w_in[l]), nsa_tables[2])
        y_a, y_c = _ssm_and_retention(p_ssm.reshape(B, S, W_SSM), conv_w[l], conv_b[l], dt_bias[l], a_log[l],
                                      d_skip[l], ssm_norm_g[l], p_ret.reshape(B, S, W_RET), ret_tables)
        wq, wk, wv = _layout_mla_weights(w_uq[l], w_ukv[l])
        q, k, v = _mla_prep(p_mla.reshape(B, S, W_MLA), q_norm_g[l], kv_norm_g[l], wq, wk, wv, mla_tables)
        y_b = _mla_attn(q, k, v)
        y_d = _nsa(p_nsa.reshape(B, S, W_NSA), nsa_kv.reshape(B, S, W_KV), cmp_pe_k[l], cmp_w1_k[l], cmp_w2_k[l],
                   cmp_pe_v[l], cmp_w1_v[l], cmp_w2_v[l], nsa_tables)
        ys = [y.reshape(T, GROUP_W) for y in (y_a, y_b, y_c, y_d)]
        h = _out_proj(h, ys, w_out, l, ln1_g[l], ln1_b[l])
        h = _mlp(h, w_mlp1, w_mlp2, l, ln2_g[l], ln2_b[l])
    return h.reshape(B, S, D)
```

```python
import functools
import math

import jax
import jax.numpy as jnp
import numpy as np
from jax import lax
from jax.experimental import pallas as pl
from jax.experimental.pallas import tpu as pltpu

F32 = jnp.float32
BF16 = jnp.bfloat16
HIGHEST = lax.Precision.HIGHEST

D_MODEL = 1024
DEPTH = 2
GROUP_W = D_MODEL // 4
SSM_HEADS = 4
SSM_HEAD_DIM = GROUP_W // SSM_HEADS
SSM_GROUPS = 2
SSM_STATE = 128
SSM_CONV = 4
SSM_CHUNK = 128
SSM_XBC = GROUP_W + 2 * SSM_GROUPS * SSM_STATE
MLA_HEADS = 4
MLA_NOPE = 64
MLA_ROPE = 32
MLA_V = GROUP_W // MLA_HEADS
MLA_Q_RANK = 256
MLA_KV_RANK = 128
RET_HEADS = 4
RET_DK = 64
RET_DV = GROUP_W // RET_HEADS
RET_CHUNK = 128
NSA_HEADS = 4
NSA_DK = 64
NSA_DV = GROUP_W // NSA_HEADS
NSA_CMP_LEN = 32
NSA_CMP_STRIDE = 16
NSA_CMP_HID = 256
NSA_SLC_LEN = 64
NSA_TOPN = 16
NSA_WIN = 512
D_FF = 4 * D_MODEL
NSA_Q = 512
ROPE_THETA = 10000.0
EPS = 1e-5
NEG_INF = -1e30
LOG2_E = math.log2(math.e)
FORCED_SCORE = 1e9
DEEPNORM_ALPHA = (2.0 * DEPTH) ** 0.25

IN_SPLITS = (
    GROUP_W, SSM_XBC, SSM_HEADS,
    MLA_Q_RANK, MLA_KV_RANK, MLA_ROPE,
    RET_HEADS * RET_DK, RET_HEADS * RET_DK, RET_HEADS * RET_DV, GROUP_W,
    NSA_HEADS * NSA_DK, NSA_DK, NSA_DV, NSA_DK, NSA_DV, NSA_DK, NSA_DV, 3 * NSA_HEADS,
)

LANE = 128
W_SSM = GROUP_W + SSM_XBC + LANE
W_MLA = MLA_Q_RANK + MLA_KV_RANK + LANE
W_RET = 4 * GROUP_W
W_NSA = NSA_HEADS * LANE + LANE + LANE
W_KV = 2 * LANE + 3 * LANE
W_PROJ = W_SSM + W_MLA + W_RET + W_NSA + 4 * NSA_DK

NSA_TILE = 512
NSA_SINK = 128
MLA_TILE = 1024
VMEM_LIMIT = 48 * 1024 * 1024


def _cparams(*sem):
    return pltpu.CompilerParams(dimension_semantics=sem, vmem_limit_bytes=VMEM_LIMIT)


def _mm(a, b):
    return jnp.dot(a.astype(BF16), b.astype(BF16), preferred_element_type=F32)


def _mm_nt(a, b):
    return lax.dot_general(a.astype(BF16), b.astype(BF16), (((1,), (1,)), ((), ())),
                           preferred_element_type=F32)


def _mm_f32(a, b):
    return jnp.dot(a, b, precision=HIGHEST, preferred_element_type=F32)


def _silu(x):
    return x * jax.nn.sigmoid(x)


def _softplus(x):
    return jnp.maximum(x, 0.0) + jnp.log1p(jnp.exp(-jnp.abs(x)))


def _layer_norm(x, g, b):
    mu = jnp.mean(x, axis=-1, keepdims=True)
    xc = x - mu
    var = jnp.mean(xc * xc, axis=-1, keepdims=True)
    return xc * lax.rsqrt(var + EPS) * g + b


def _iota(shape, dim):
    return lax.broadcasted_iota(jnp.int32, shape, dim)


def _pad_cols(w, width):
    return jnp.pad(w, ((0, 0), (0, width - w.shape[1])))


def _layout_w_in(w):
    offs = np.concatenate([[0], np.cumsum(IN_SPLITS)])
    p = [w[:, int(offs[i]):int(offs[i + 1])] for i in range(len(IN_SPLITS))]
    (ssm_z, ssm_xbc, ssm_dt, mla_cq, mla_ckv, mla_kr, ret_q, ret_k, ret_v, ret_g,
     nsa_q, nsa_kc, nsa_vc, nsa_ks, nsa_vs, nsa_kw, nsa_vw, nsa_gate) = p
    nsa_q_heads = [_pad_cols(nsa_q[:, h * NSA_DK:(h + 1) * NSA_DK], LANE) for h in range(NSA_HEADS)]
    cols = [ssm_z, ssm_xbc, _pad_cols(ssm_dt, LANE),
            mla_cq, mla_ckv, _pad_cols(mla_kr, LANE),
            ret_q, ret_k, ret_v, ret_g,
            *nsa_q_heads, nsa_kc, nsa_vc, _pad_cols(nsa_gate, LANE),
            nsa_ks, nsa_vs, nsa_kw, nsa_vw]
    out = jnp.concatenate(cols, axis=1)
    assert out.shape[1] == W_PROJ
    return out.astype(BF16)


def _in_proj_kernel(*refs, entry_ln):
    if entry_ln:
        h_ref, g_ref, b_ref, w_ref, kvtab_ref, hn_ref, *outs = refs
        hn = _layer_norm(h_ref[...], g_ref[...], b_ref[...])
        hn_ref[...] = hn
    else:
        h_ref, w_ref, kvtab_ref, *outs = refs
        hn = h_ref[...]
    ssm_ref, mla_ref, ret_ref, nsa_ref, kv_ref, uk_ref, uv_ref, kcv_ref = outs
    hb = hn.astype(BF16)
    off = 0
    for ref, width in ((ssm_ref, W_SSM), (mla_ref, W_MLA), (ret_ref, W_RET), (nsa_ref, W_NSA)):
        ref[...] = jnp.dot(hb, w_ref[:, off:off + width], preferred_element_type=F32)
        off += width
    groups = h_ref.shape[0] // NSA_CMP_STRIDE
    kc_lane = NSA_HEADS * LANE
    kcv_ref[...] = nsa_ref[:, kc_lane:kc_lane + LANE]
    for t in range(NSA_CMP_STRIDE):
        piece = kcv_ref[pl.ds(t, groups, stride=NSA_CMP_STRIDE), :]
        uk_ref[:, t * NSA_DK:(t + 1) * NSA_DK] = piece[:, :NSA_DK]
        uv_ref[:, t * NSA_DV:(t + 1) * NSA_DV] = piece[:, NSA_DK:]
    kv = jnp.dot(hb, w_ref[:, off:off + 4 * NSA_DK], preferred_element_type=F32)
    low = _iota((kv.shape[0], LANE), 1) < NSA_DK
    sel_kv, win_kv = kv[:, :LANE], kv[:, LANE:]
    pieces = {0: sel_kv, 2: pltpu.roll(sel_kv, NSA_DK, 1), 3: win_kv, 4: pltpu.roll(win_kv, NSA_DK, 1)}
    for slab in range(W_KV // LANE):
        lanes = slice(slab * LANE, (slab + 1) * LANE)
        tab = kvtab_ref[:, lanes]
        if slab in pieces:
            kv_ref[:, lanes] = (jnp.where(low, pieces[slab], 0.0) + tab.astype(F32)).astype(BF16)
        else:
            kv_ref[:, lanes] = tab


def _in_proj(h2, w_p, kv_table, entry_ln=None, tm=256):
    T, D = h2.shape
    S = kv_table.shape[0]
    widths = (W_SSM, W_MLA, W_RET, W_NSA)
    half = NSA_CMP_STRIDE * NSA_DK
    row = lambda w: pl.BlockSpec((tm, w), lambda i: (i, 0))
    const = lambda shape: pl.BlockSpec(shape, lambda i: (0,) * len(shape))
    in_specs = [const((D, W_PROJ)), pl.BlockSpec((tm, W_KV), lambda i: (i % (S // tm), 0))]
    out_specs = [row(w) for w in widths + (W_KV,)] + [pl.BlockSpec((tm // NSA_CMP_STRIDE, half), lambda i: (i, 0))] * 2
    out_shape = ([jax.ShapeDtypeStruct((T, w), F32) for w in widths] + [jax.ShapeDtypeStruct((T, W_KV), BF16)]
                 + [jax.ShapeDtypeStruct((T // NSA_CMP_STRIDE, half), F32)] * 2)
    operands = (w_p, kv_table)
    if entry_ln is not None:
        in_specs = [const((1, D)), const((1, D))] + in_specs
        out_specs = [row(D)] + out_specs
        out_shape = [jax.ShapeDtypeStruct((T, D), F32)] + out_shape
        operands = tuple(v.reshape(1, D) for v in entry_ln) + operands
    return pl.pallas_call(
        functools.partial(_in_proj_kernel, entry_ln=entry_ln is not None), name="in_proj",
        grid=(T // tm,),
        in_specs=[row(D)] + in_specs,
        out_specs=out_specs,
        out_shape=out_shape,
        scratch_shapes=[pltpu.VMEM((tm, LANE), F32)],
        compiler_params=_cparams("parallel"),
    )(h2, *operands)


def _ssm_chunk(p_ref, cw_ref, cb_ref, dtb_ref, alog_ref, dskip_ref, ng_ref, o_ref, state_ref, ext_ref):
    L, H, P, N = SSM_CHUNK, SSM_HEADS, SSM_HEAD_DIM, SSM_STATE
    z = p_ref[:, 0:GROUP_W]
    ext_ref[8:8 + L, :] = p_ref[:, GROUP_W:GROUP_W + SSM_XBC]
    conv = cb_ref[...]
    for j in range(SSM_CONV):
        conv = conv + ext_ref[pl.ds(8 - (SSM_CONV - 1) + j, L), :] * cw_ref[j:j + 1, :]
    ext_ref[0:8, :] = ext_ref[L:L + 8, :]
    xbc = _silu(conv)
    xs = xbc[:, 0:GROUP_W]
    b_in = xbc[:, GROUP_W:GROUP_W + SSM_GROUPS * N]
    c_in = xbc[:, GROUP_W + SSM_GROUPS * N:]

    dt = _softplus(p_ref[:, GROUP_W + SSM_XBC:] + dtb_ref[...])
    a = dt * (-jnp.exp(alog_ref[...]))
    row = _iota((L, L), 0)
    col = _iota((L, L), 1)
    tril = col <= row
    cs = _mm_f32(jnp.where(tril, 1.0, 0.0), a)
    cs_t = cs.T
    ecs = jnp.exp(cs)
    dte = jnp.exp(cs[L - 1:L, :] - cs)
    expand = jnp.where(_iota((LANE, H * P), 0) == _iota((LANE, H * P), 1) // P, 1.0, 0.0)
    dt_x = _mm_f32(dt, expand)
    ecs_x = _mm_f32(ecs, expand)
    dte_x = _mm_f32(dte, expand)

    xdt = xs * dt_x
    wx = xdt * dte_x
    head_of_lane = _iota((L, H * P), 1) // P
    y = xs * dskip_ref[...]
    y_off = []
    rep = H // SSM_GROUPS
    for g in range(SSM_GROUPS):
        cg = c_in[:, g * N:(g + 1) * N]
        bg = b_in[:, g * N:(g + 1) * N]
        cb = _mm_nt(cg, bg)
        for h in range(g * rep, (g + 1) * rep):
            diff = cs[:, h:h + 1] - cs_t[h:h + 1, :]
            seg = jnp.where(tril, jnp.exp(jnp.where(tril, diff, 0.0)), 0.0)
            yh = _mm(cb * seg, xdt)
            y = y + jnp.where(head_of_lane == h, yh, 0.0)
        lanes = slice(g * rep * P, (g + 1) * rep * P)
        st_prev = state_ref[:, lanes]
        y_off.append(_mm(cg, st_prev))
        state_ref[:, lanes] = st_prev * ecs_x[L - 1:L, lanes] + _mm(bg.T, wx[:, lanes])
    y = y + jnp.concatenate(y_off, axis=1) * ecs_x
    y = y * _silu(z)
    ms = jnp.mean(y * y, axis=-1, keepdims=True)
    o_ref[...] = (y * lax.rsqrt(ms + EPS) * ng_ref[...]).astype(o_ref.dtype)


def _recurrent_kernel(ps_ref, cw_ref, cb_ref, dtb_ref, alog_ref, dskip_ref, ng_ref,
                      pr_ref, cos_ref, sin_ref, dec_ref, zeta_ref, xi_ref, cd_ref,
                      oa_ref, oc_ref, sstate_ref, ext_ref, rstate_ref):
    @pl.when(pl.program_id(0) == 0)
    def _():
        sstate_ref[...] = jnp.zeros_like(sstate_ref)
        rstate_ref[...] = jnp.zeros_like(rstate_ref)
        ext_ref[:, 0:8, :] = jnp.zeros((ext_ref.shape[0], 8, SSM_XBC), F32)

    for b in range(ps_ref.shape[0]):
        _ssm_chunk(ps_ref.at[b], cw_ref, cb_ref, dtb_ref, alog_ref, dskip_ref, ng_ref,
                   oa_ref.at[b], sstate_ref.at[b], ext_ref.at[b])
        _ret_chunk(pr_ref.at[b], cos_ref, sin_ref, dec_ref, zeta_ref, xi_ref, cd_ref, oc_ref.at[b], rstate_ref.at[b])


def _ret_chunk(p_ref, cos_ref, sin_ref, dec_ref, zeta_ref, xi_ref, cd_ref, o_ref, state_ref):
    L, H, DK, DV = RET_CHUNK, RET_HEADS, RET_DK, RET_DV
    W = H * DK
    q = p_ref[:, 0:W]
    k = p_ref[:, W:2 * W]
    v = p_ref[:, 2 * W:3 * W]
    gate = p_ref[:, 3 * W:4 * W]
    lane = _iota((L, W), 1)
    first_half = (lane % DK) < (DK // 2)
    head_of_lane = lane // DK

    def rope(x):
        partner = jnp.where(first_half, pltpu.roll(x, W - DK // 2, 1), pltpu.roll(x, DK // 2, 1))
        return x * cos_ref[...] + partner * sin_ref[...]

    qr = rope(q)
    kr = rope(k) * (DK ** -0.5)
    y = jnp.zeros((L, H * DV), F32)
    for h in range(H):
        qh = jnp.where(head_of_lane == h, qr, 0.0)
        sc = _mm_nt(qh, kr) * dec_ref[h]
        y = y + jnp.where(head_of_lane == h, _mm(sc, v), 0.0)
    st = state_ref[...]
    y = y + _mm(qr * xi_ref[...], st)
    same_head = (_iota((W, H * DV), 0) // DK) == (_iota((W, H * DV), 1) // DV)
    kv = _mm((kr * zeta_ref[...]).T, v)
    state_ref[...] = st * cd_ref[...] + jnp.where(same_head, kv, 0.0)
    ms = _mm_f32(y * y, jnp.where(same_head, 1.0 / DV, 0.0))
    o_ref[...] = (y * lax.rsqrt(ms + EPS) * _silu(gate)).astype(o_ref.dtype)


def _ret_tables(S):
    H, DK, L = RET_HEADS, RET_DK, RET_CHUNK
    inv = ROPE_THETA ** (-np.arange(0, DK, 2, dtype=np.float64) / DK)
    ang = np.arange(S, dtype=np.float64)[:, None] * inv[None, :]
    cos, sin = np.cos(ang), np.sin(ang)
    cos_t = np.tile(np.concatenate([cos, cos], axis=1), (1, H))
    sin_t = np.tile(np.concatenate([-sin, sin], axis=1), (1, H))
    log_gamma = np.log1p(-np.exp2(-5.0 - np.arange(H, dtype=np.float64)))
    pos = np.arange(L, dtype=np.float64)
    diff = pos[:, None] - pos[None, :]
    decay_in = np.where(diff >= 0, np.exp(np.maximum(diff, 0.0)[None] * log_gamma[:, None, None]), 0.0)
    zeta = np.exp((L - 1 - pos)[None] * log_gamma[:, None])
    xi = np.exp((pos + 1.0)[None] * log_gamma[:, None])
    chunk_decay = np.exp(L * log_gamma)
    zeta_x = np.repeat(zeta.T, DK, axis=1)
    xi_x = np.repeat(xi.T, DK, axis=1)
    cd_x = np.repeat(chunk_decay, RET_DV).reshape(1, H * RET_DV)
    return tuple(jnp.asarray(t, dtype=F32) for t in (cos_t, sin_t, decay_in, zeta_x, xi_x, cd_x))


def _ssm_and_retention(p_ssm, conv_w, conv_b, dt_bias, a_log, d_skip, norm_g, p_ret, tables):
    B, S, _ = p_ssm.shape
    L, H = SSM_CHUNK, RET_HEADS
    assert RET_CHUNK == L
    W = H * RET_DK
    cos_t, sin_t, decay_in, zeta_x, xi_x, cd_x = tables
    pad_h = lambda v: jnp.pad(v, (0, LANE - SSM_HEADS)).reshape(1, LANE)
    const = lambda shape: pl.BlockSpec(shape, lambda c: (0,) * len(shape))
    chunk = lambda width: pl.BlockSpec((B, L, width), lambda c: (0, c, 0))
    out_shape = jax.ShapeDtypeStruct((B, S, GROUP_W), BF16)
    return pl.pallas_call(
        _recurrent_kernel, name="ssm_retention",
        grid=(S // L,),
        in_specs=[chunk(W_SSM),
                  const((SSM_CONV, SSM_XBC)), const((1, SSM_XBC)), const((1, LANE)), const((1, LANE)),
                  const((1, GROUP_W)), const((1, GROUP_W)),
                  chunk(W_RET),
                  pl.BlockSpec((L, W), lambda c: (c, 0)), pl.BlockSpec((L, W), lambda c: (c, 0)),
                  const((H, L, L)), const((L, W)), const((L, W)), const((1, H * RET_DV))],
        out_specs=[chunk(GROUP_W), chunk(GROUP_W)],
        out_shape=[out_shape, out_shape],
        scratch_shapes=[pltpu.VMEM((B, SSM_STATE, GROUP_W), F32),
                        pltpu.VMEM((B, L + 8, SSM_XBC), F32),
                        pltpu.VMEM((B, W, H * RET_DV), F32)],
        compiler_params=_cparams("arbitrary"),
    )(p_ssm, conv_w, conv_b.reshape(1, -1), pad_h(dt_bias), pad_h(a_log),
      jnp.repeat(d_skip, SSM_HEAD_DIM).reshape(1, GROUP_W), norm_g.reshape(1, GROUP_W),
      p_ret, cos_t, sin_t, decay_in, zeta_x, xi_x, cd_x)


def _mla_prep_kernel(p_ref, qg_ref, wq_ref, kvg_ref, wk_ref, wv_ref, cos_ref, sin_ref,
                     q_ref, k_ref, v_ref):
    tm = p_ref.shape[0]
    cq = p_ref[:, 0:MLA_Q_RANK]
    ckv = p_ref[:, MLA_Q_RANK:MLA_Q_RANK + MLA_KV_RANK]
    kr = p_ref[:, MLA_Q_RANK + MLA_KV_RANK:]

    def rms(x, g):
        return x * lax.rsqrt(jnp.mean(x * x, axis=-1, keepdims=True) + EPS) * g

    q = _mm(rms(cq, qg_ref[...]), wq_ref[...])
    kvl = rms(ckv, kvg_ref[...])
    kn = _mm(kvl, wk_ref[...])
    vv = _mm(kvl, wv_ref[...])
    kr_sh = pltpu.roll(kr, MLA_NOPE, 1)
    lane = _iota((tm, LANE), 1)
    half = MLA_ROPE // 2
    low = (lane >= MLA_NOPE) & (lane < MLA_NOPE + half)
    cos = cos_ref[...]
    sin = sin_ref[...]

    def rope(x):
        partner = jnp.where(low, pltpu.roll(x, LANE - half, 1), pltpu.roll(x, half, 1))
        return x * cos + partner * sin

    scale = (MLA_NOPE + MLA_ROPE) ** -0.5 * LOG2_E
    for h in range(MLA_HEADS):
        sl = slice(h * LANE, (h + 1) * LANE)
        q_ref[h] = (rope(q[:, sl]) * scale).astype(BF16)
        k_ref[h] = rope(kn[:, sl] + kr_sh).astype(BF16)
        v_ref[h] = jnp.where(lane == _mla_ones_lane(h), 1.0, vv[:, sl]).astype(BF16)


def _mla_tables(S):
    inv = ROPE_THETA ** (-np.arange(0, MLA_ROPE, 2, dtype=np.float64) / MLA_ROPE)
    ang = np.arange(S, dtype=np.float64)[:, None] * inv[None, :]
    cos, sin = np.cos(ang), np.sin(ang)
    tail = LANE - MLA_NOPE - MLA_ROPE
    cos_t = np.concatenate([np.ones((S, MLA_NOPE)), cos, cos, np.ones((S, tail))], axis=1)
    sin_t = np.concatenate([np.zeros((S, MLA_NOPE)), -sin, sin, np.zeros((S, tail))], axis=1)
    return jnp.asarray(cos_t, dtype=F32), jnp.asarray(sin_t, dtype=F32)


def _layout_mla_weights(w_uq, w_ukv):
    H = MLA_HEADS
    dq = MLA_NOPE + MLA_ROPE
    wq = jnp.concatenate([_pad_cols(w_uq[:, h * dq:(h + 1) * dq], LANE) for h in range(H)], axis=1)
    dkv = MLA_NOPE + MLA_V
    wk, wv = [], []
    for h in range(H):
        blk = w_ukv[:, h * dkv:(h + 1) * dkv]
        wk.append(_pad_cols(blk[:, :MLA_NOPE], LANE))
        v = blk[:, MLA_NOPE:]
        zero = jnp.zeros_like(v)
        wv.append(jnp.concatenate([v, zero] if h % 2 == 0 else [zero, v], axis=1))
    return wq.astype(BF16), jnp.concatenate(wk, axis=1).astype(BF16), jnp.concatenate(wv, axis=1).astype(BF16)


def _mla_prep(p_mla, q_norm_g, kv_norm_g, wq, wk, wv, tables, tm=512):
    B, S, _ = p_mla.shape
    H = MLA_HEADS
    cos_t, sin_t = tables
    const = lambda shape: pl.BlockSpec(shape, lambda b, i: (0,) * len(shape))
    qkv_spec = pl.BlockSpec((None, H, tm, LANE), lambda b, i: (b, 0, i, 0))
    qkv_shape = jax.ShapeDtypeStruct((B, H, S, LANE), BF16)
    return pl.pallas_call(
        _mla_prep_kernel, name="mla_prep",
        grid=(B, S // tm),
        in_specs=[pl.BlockSpec((None, tm, W_MLA), lambda b, i: (b, i, 0)),
                  const((1, MLA_Q_RANK)), const((MLA_Q_RANK, H * LANE)),
                  const((1, MLA_KV_RANK)), const((MLA_KV_RANK, H * LANE)), const((MLA_KV_RANK, H * LANE)),
                  pl.BlockSpec((tm, LANE), lambda b, i: (i, 0)),
                  pl.BlockSpec((tm, LANE), lambda b, i: (i, 0))],
        out_specs=[qkv_spec, qkv_spec, qkv_spec],
        out_shape=[qkv_shape, qkv_shape, qkv_shape],
        compiler_params=_cparams("parallel", "parallel"),
    )(p_mla, q_norm_g.reshape(1, -1), wq, kv_norm_g.reshape(1, -1), wk, wv, cos_t, sin_t)


def _mla_ones_lane(h):
    return MLA_V if h % 2 == 0 else 0


def _mla_attn_kernel(qi_ref, kj_ref, q_ref, k_ref, v_ref, o_ref, m_ref, acc_ref):
    H = MLA_HEADS
    tq, tk = q_ref.shape[1], k_ref.shape[1]
    i = qi_ref[pl.program_id(1)]
    j = kj_ref[pl.program_id(1)]

    @pl.when(j == 0)
    def _():
        m_ref[...] = jnp.full(m_ref.shape, NEG_INF, F32)
        acc_ref[...] = jnp.zeros_like(acc_ref)

    def sweep(blocks):
        nt_dims = (((1,), (1,)), ((), ()))
        scores = [[lax.dot_general(q_ref[h, r0:r0 + nr, :], k_ref[h, 0:nk, :], nt_dims, preferred_element_type=F32)
                   for (r0, nr, nk, _) in blocks] for h in range(H)]
        for h in range(H):
            for (r0, nr, nk, offset), s in zip(blocks, scores[h]):
                if offset is not None:
                    s = jnp.where(_iota((nr, nk), 1) - _iota((nr, nk), 0) <= offset, s, NEG_INF)
                rows = slice(r0, r0 + nr)
                m_prev = m_ref[h, rows]
                m_new = jnp.maximum(m_prev, jnp.max(s, axis=-1, keepdims=True))
                p = jnp.exp2(s - jnp.tile(m_new, (1, nk // LANE)))
                acc_ref[h, rows] = (jnp.exp2(m_prev - m_new) * acc_ref[h, rows]
                                    + jnp.dot(p.astype(BF16), v_ref[h, 0:nk, :], preferred_element_type=F32))
                m_ref[h, rows] = m_new

    assert tq == tk
    half = tq // 2

    @pl.when(j < i)
    def _():
        sweep([(0, tq, tk, None)])

    @pl.when(j == i)
    def _():
        sweep([(0, half, half, 0), (half, half, tk, half)])
        lane = _iota((tq, LANE), 1)
        for pair in range(H // 2):
            he, ho = 2 * pair, 2 * pair + 1
            acc_e, acc_o = acc_ref[he], acc_ref[ho]
            le = acc_e[:, _mla_ones_lane(he):_mla_ones_lane(he) + 1]
            lo = acc_o[:, _mla_ones_lane(ho):_mla_ones_lane(ho) + 1]
            o_ref[:, pair * LANE:(pair + 1) * LANE] = jnp.where(lane < MLA_V, acc_e / le, acc_o / lo).astype(o_ref.dtype)


def _mla_attn(q, k, v):
    B, H, S, _ = q.shape
    t = min(MLA_TILE, S)
    tq = t
    pairs = [(i, j) for i in range(S // tq) for j in range((i + 1) * tq // t)]
    qi = jnp.asarray([p[0] for p in pairs], jnp.int32)
    kj = jnp.asarray([p[1] for p in pairs], jnp.int32)
    grid_spec = pltpu.PrefetchScalarGridSpec(
        num_scalar_prefetch=2,
        grid=(B, len(pairs)),
        in_specs=[pl.BlockSpec((None, H, tq, LANE), lambda b, p, qi, kj: (b, 0, qi[p], 0)),
                  pl.BlockSpec((None, H, t, LANE), lambda b, p, qi, kj: (b, 0, kj[p], 0)),
                  pl.BlockSpec((None, H, t, LANE), lambda b, p, qi, kj: (b, 0, kj[p], 0))],
        out_specs=pl.BlockSpec((None, tq, GROUP_W), lambda b, p, qi, kj: (b, qi[p], 0)),
        scratch_shapes=[pltpu.VMEM((H, tq, LANE), F32), pltpu.VMEM((H, tq, LANE), F32)],
    )
    return pl.pallas_call(
        _mla_attn_kernel, name="mla_attn",
        grid_spec=grid_spec,
        out_shape=jax.ShapeDtypeStruct((B, S, GROUP_W), BF16),
        compiler_params=_cparams("parallel", "arbitrary"),
    )(qi, kj, q, k, v)


POS_HI = NSA_DK
POS_LO = NSA_DK + 3
POS_ONE = NSA_DK + 6
ONES_LANE = NSA_DV


def _split_bf16(x, parts=3):
    out, rem = [], np.float64(x)
    for _ in range(parts):
        piece = np.float64(np.float32(rem).astype(jnp.bfloat16).astype(np.float32))
        out.append(float(piece))
        rem = rem - piece
    return out


def _nsa_query_table():
    H = NSA_HEADS
    tab = np.zeros((2 * H, LANE), np.float32)
    for h in range(H):
        c = 2.0 ** (-8.0 * (h + 1) / H) * LOG2_E
        pieces = _split_bf16(c)
        tab[h, POS_HI:POS_HI + 3] = pieces
        tab[h, POS_LO:POS_LO + 3] = pieces
        tab[H + h, POS_ONE] = -sum(pieces)
    return jnp.asarray(tab)


def _nsa_pos_lanes(pos, lo_offset=0.0):
    t = np.zeros((len(pos), LANE - NSA_DK), np.float32)
    t[:, POS_HI - NSA_DK:POS_HI - NSA_DK + 3] = (NSA_SLC_LEN * (pos // NSA_SLC_LEN))[:, None]
    t[:, POS_LO - NSA_DK:POS_LO - NSA_DK + 3] = (pos % NSA_SLC_LEN + lo_offset)[:, None]
    t[:, POS_ONE - NSA_DK] = 1.0
    return t


def _nsa_queries(q_ref, qtab_ref, qb):
    Q, H = q_ref.shape[0], NSA_HEADS
    qpos = (qb * Q + _iota((Q, 1), 0)).astype(F32)
    out = []
    for h in range(H):
        q = q_ref[:, h * LANE:(h + 1) * LANE] * (NSA_DK ** -0.5 * LOG2_E)
        out.append((q + qtab_ref[h:h + 1, :] + qtab_ref[H + h:H + h + 1, :] * qpos).astype(BF16))
    return out


def _normalise(o):
    return o / o[:, ONES_LANE:ONES_LANE + 1]


def _unstack_heads(o):
    Q = o.shape[0] // NSA_HEADS
    lane = _iota((Q, LANE), 1)
    out = []
    for pair in range(NSA_HEADS // 2):
        even = o[(2 * pair) * Q:(2 * pair + 1) * Q]
        odd = o[(2 * pair + 1) * Q:(2 * pair + 2) * Q]
        out.append(jnp.where(lane < NSA_DV, even, pltpu.roll(odd, NSA_DV, 1)))
    return jnp.concatenate(out, axis=1)


def _nsa_cmp_kernel(uk_ref, uv_ref, pek_ref, pev_ref, w1k_ref, w1v_ref, w2k_ref, w2v_ref, cpos_ref,
                    kc_ref, vc_ref, sh_ref):
    nb = uk_ref.shape[0]
    half = uk_ref.shape[1]

    def hidden(u_ref, pe_ref, w1_ref):
        u = u_ref[...]
        first = _mm(u + pe_ref[0:1, :], w1_ref[0:half, :])
        second = _mm(u + pe_ref[1:2, :], w1_ref[half:2 * half, :])
        sh_ref[0:nb, :] = second
        sh_ref[nb:nb + 8, :] = jnp.zeros((8, NSA_CMP_HID), F32)
        return first + sh_ref[pl.ds(1, nb), :]

    hk = _silu(hidden(uk_ref, pek_ref, w1k_ref))
    hv = _silu(hidden(uv_ref, pev_ref, w1v_ref))
    kc_ref[...] = (_mm(hk, w2k_ref[...]) + cpos_ref[...]).astype(BF16)
    ones_lane = jnp.where(_iota((1, LANE), 1) == ONES_LANE, 1.0, 0.0)
    vc_ref[...] = (_mm(hv, w2v_ref[...]) + ones_lane).astype(BF16)


def _nsa_compress(uk, uv, pe_k, w1_k, w2_k, pe_v, w1_v, w2_v):
    B, nb, half = uk.shape
    hid = NSA_CMP_HID
    const = lambda shape: pl.BlockSpec(shape, lambda b: (0,) * len(shape))
    w2k = _pad_cols(w2_k, LANE).astype(BF16)
    w2v = _pad_cols(w2_v, LANE).astype(BF16)
    centre = _nsa_pos_lanes(np.arange(nb) * NSA_CMP_STRIDE, 0.5 * (NSA_CMP_LEN - 1))
    cpos = jnp.asarray(np.concatenate([np.zeros((nb, NSA_DK), np.float32), centre], axis=1))
    out_spec = pl.BlockSpec((None, nb, LANE), lambda b: (b, 0, 0))
    out_shape = jax.ShapeDtypeStruct((B, nb, LANE), BF16)
    return pl.pallas_call(
        _nsa_cmp_kernel, name="nsa_compress",
        grid=(B,),
        in_specs=[pl.BlockSpec((None, nb, half), lambda b: (b, 0, 0)),
                  pl.BlockSpec((None, nb, half), lambda b: (b, 0, 0)),
                  const((2, half)), const((2, half)),
                  const((2 * half, hid)), const((2 * half, hid)),
                  const((hid, LANE)), const((hid, LANE)), const((nb, LANE))],
        out_specs=[out_spec, out_spec],
        out_shape=[out_shape, out_shape],
        scratch_shapes=[pltpu.VMEM((nb + 8, hid), F32)],
        compiler_params=_cparams("parallel"),
    )(uk, uv, pe_k.reshape(2, half), pe_v.reshape(2, half), w1_k.astype(BF16), w1_v.astype(BF16), w2k, w2v, cpos)


def _nsa_sel_kernel(q_ref, qtab_ref, kc_ref, vc_ref, ovt_ref, oc_ref, selb_ref, any_ref, imp_ref, *, n_slc, top_n):
    Q, H = q_ref.shape[0], NSA_HEADS
    qb = pl.program_id(1)
    nc = kc_ref.shape[0]
    qs = jnp.concatenate(_nsa_queries(q_ref, qtab_ref, qb), axis=0)
    nt_dims = (((1,), (1,)), ((), ()))

    def attend(ncols):
        s = lax.dot_general(qs, kc_ref[0:ncols, :], nt_dims, preferred_element_type=F32)
        qpos = qb * Q + (_iota((H * Q, ncols), 0) & (Q - 1))
        block_end = _iota((H * Q, ncols), 1) * NSA_CMP_STRIDE + (NSA_CMP_LEN - 1)
        s = jnp.where(block_end <= qpos, s, NEG_INF)
        e = jnp.exp2(s - jnp.max(s, axis=-1, keepdims=True))
        qpos_col = qb * Q + (_iota((H * Q, 1), 0) & (Q - 1))
        has_block = jnp.where(qpos_col >= NSA_CMP_LEN - 1, 1.0, 0.0)
        p = e * (has_block / jnp.sum(e, axis=-1, keepdims=True))
        oc_ref[...] = _unstack_heads(jnp.dot(p.astype(BF16), vc_ref[0:ncols, :], preferred_element_type=F32))
        p_sum = p[0:Q]
        for h in range(1, H):
            p_sum = p_sum + p[h * Q:(h + 1) * Q]
        imp_ref[...] = lax.dot_general(ovt_ref[:, 0:ncols], p_sum, nt_dims, precision=HIGHEST,
                                       preferred_element_type=F32)

    tiles_needed = ((qb + 1) * Q // NSA_CMP_STRIDE + LANE - 1) // LANE
    for tiles in range(1, nc // LANE + 1):
        pl.when(tiles_needed == tiles)(functools.partial(attend, tiles * LANE))

    imp = imp_ref[...]
    blk = _iota((LANE, Q), 0)
    q_blk = (qb * Q + _iota((LANE, Q), 1)) >> int(math.log2(NSA_SLC_LEN))
    causal = blk <= q_blk
    for forced_blk in (0, q_blk, q_blk - 1):
        imp = jnp.where(blk == forced_blk, FORCED_SCORE, imp)
    imp = jnp.where(causal, imp, -1.0)
    imp = jnp.where(blk < n_slc, imp, -2.0)
    blk_f = blk.astype(F32)
    sel = jnp.zeros((LANE, Q), F32)
    for _ in range(top_n):
        m = jnp.max(imp, axis=0, keepdims=True)
        first = jnp.min(jnp.where(imp == m, blk_f, float(LANE)), axis=0, keepdims=True)
        hit = blk_f == first
        sel = jnp.where(hit, 1.0, sel)
        imp = jnp.where(hit, -3.0, imp)
    sel = jnp.where(causal, sel, 0.0).T
    selb_ref[...] = jnp.where(sel > 0.5, 0.0, NEG_INF).astype(BF16)
    any_ref[...] = jnp.max(sel, axis=0, keepdims=True)


def _nsa_select(p_nsa, qtab, kc, vc, overlap_t):
    B, S, _ = p_nsa.shape
    Q = NSA_Q
    nqb = S // Q
    nc = kc.shape[1]
    n_slc = S // NSA_SLC_LEN
    kern = functools.partial(_nsa_sel_kernel, n_slc=n_slc, top_n=min(NSA_TOPN, n_slc))
    return pl.pallas_call(
        kern, name="nsa_select",
        grid=(B, nqb),
        in_specs=[pl.BlockSpec((None, Q, NSA_HEADS * LANE), lambda b, i: (b, i, 0)),
                  pl.BlockSpec((2 * NSA_HEADS, LANE), lambda b, i: (0, 0)),
                  pl.BlockSpec((None, nc, LANE), lambda b, i: (b, 0, 0)),
                  pl.BlockSpec((None, nc, LANE), lambda b, i: (b, 0, 0)),
                  pl.BlockSpec((LANE, nc), lambda b, i: (0, 0))],
        out_specs=[pl.BlockSpec((None, Q, GROUP_W), lambda b, i: (b, i, 0)),
                   pl.BlockSpec((None, Q, LANE), lambda b, i: (b, i, 0)),
                   pl.BlockSpec((None, None, 1, LANE), lambda b, i: (b, i, 0, 0))],
        out_shape=[jax.ShapeDtypeStruct((B, S, GROUP_W), F32),
                   jax.ShapeDtypeStruct((B, S, LANE), BF16),
                   jax.ShapeDtypeStruct((B, nqb, 1, LANE), F32)],
        scratch_shapes=[pltpu.VMEM((LANE, Q), F32)],
        compiler_params=_cparams("parallel", "parallel"),
    )(p_nsa, qtab, kc, vc, overlap_t)


def _nsa_attn_kernel(flags_ref, q_ref, qtab_ref, gate_ref, oc_ref, selb_ref, ks_ref, vs_ref, kw_ref, vw_ref,
                     o_ref, m_ref, acc_ref, *, nt):
    Q, H, TK = q_ref.shape[0], NSA_HEADS, NSA_TILE
    PART = 2 * Q
    b = pl.program_id(0)
    qb = pl.program_id(1)
    nqb = pl.num_programs(1)
    qh = _nsa_queries(q_ref, qtab_ref, qb)
    nt_dims = (((1,), (1,)), ((), ()))

    selb = selb_ref[...]
    qs_sel = jnp.concatenate([jnp.concatenate([q, selb], axis=1) for q in qh], axis=0)
    sink_off = jnp.where(_iota(selb.shape, 1) < NSA_SINK // NSA_SLC_LEN, NEG_INF, selb.astype(F32)).astype(BF16)
    qs_loop = jnp.concatenate([jnp.concatenate([q, sink_off], axis=1) for q in qh], axis=0)
    m_ref[...] = jnp.full(m_ref.shape, NEG_INF, F32)
    acc_ref[...] = jnp.zeros_like(acc_ref)

    def update(t, diagonal):
        rows = pl.ds(pl.multiple_of(t * TK, TK), TK)
        k_tile, v_tile = ks_ref[rows, :], vs_ref[rows, :]
        parts = [slice(i * PART, (i + 1) * PART) for i in range(H * Q // PART)]
        qs_t = qs_sel if diagonal else qs_loop
        scores = [lax.dot_general(qs_t[r], k_tile, nt_dims, preferred_element_type=F32) for r in parts]
        if diagonal:
            k_sink, v_sink = ks_ref[0:NSA_SINK, :], vs_ref[0:NSA_SINK, :]
            sink_scores = [lax.dot_general(qs_sel[r], k_sink, nt_dims, preferred_element_type=F32) for r in parts]
            sink_bias = jnp.where(t > 0, 0.0, NEG_INF)
        for idx, (r, s) in enumerate(zip(parts, scores)):
            m_prev = m_ref[r]
            if diagonal:
                ahead = _iota((PART, TK), 1) - (_iota((PART, TK), 0) & (Q - 1))
                s = jnp.where(ahead <= qb * Q - t * TK, s, NEG_INF)
                s_sink = sink_scores[idx] + sink_bias
                m_prev = jnp.maximum(m_prev, jnp.max(s_sink, axis=-1, keepdims=True))
            m_new = jnp.maximum(m_prev, jnp.max(s, axis=-1, keepdims=True))
            p = jnp.exp2(s - jnp.tile(m_new, (1, TK // LANE)))
            update_acc = jnp.dot(p.astype(BF16), v_tile, preferred_element_type=F32)
            if diagonal:
                p_sink = jnp.exp2(s_sink - jnp.tile(m_new, (1, NSA_SINK // LANE)))
                update_acc = update_acc + jnp.dot(p_sink.astype(BF16), v_sink, preferred_element_type=F32)
            acc_ref[r] = jnp.exp2(m_ref[r] - m_new) * acc_ref[r] + update_acc
            m_ref[r] = m_new

    def tile(t, carry):
        @pl.when(flags_ref[(b * nqb + qb) * nt + t] > 0)
        def _():
            update(t, False)
        return carry

    t_diag = (qb * Q) // TK
    lax.fori_loop(0, t_diag, tile, 0)
    update(t_diag, True)
    o_s = _unstack_heads(_normalise(acc_ref[...]))

    W = Q + NSA_WIN
    start = jnp.maximum(qb * Q - NSA_WIN, 0)
    rows_w = pl.ds(pl.multiple_of(start, Q), W)
    qs_win = jnp.concatenate(qh, axis=0)
    s = lax.dot_general(qs_win, kw_ref[rows_w, :], nt_dims, preferred_element_type=F32)
    lead = qb * Q - start
    q_minus_k = (_iota((H * Q, W), 0) & (Q - 1)) - _iota((H * Q, W), 1)
    s = jnp.where(q_minus_k >= -lead, jnp.where(q_minus_k < NSA_WIN - lead, s, NEG_INF), NEG_INF)
    p = jnp.exp2(s - jnp.max(s, axis=-1, keepdims=True))
    o_w = _unstack_heads(_normalise(jnp.dot(p.astype(BF16), vw_ref[rows_w, :], preferred_element_type=F32)))

    gates = jax.nn.sigmoid(gate_ref[...])
    lane_head = _iota((LANE, GROUP_W), 1) // NSA_DV
    src = _iota((LANE, GROUP_W), 0)
    out = None
    for j, branch in enumerate((oc_ref[...], o_s, o_w)):
        g = _mm_f32(gates, jnp.where(src == lane_head * 3 + j, 1.0, 0.0))
        out = g * branch if out is None else out + g * branch
    o_ref[...] = out.astype(o_ref.dtype)


def _nsa_attend(p_nsa, qtab, o_c, selb, flags, kv):
    B, S, _ = p_nsa.shape
    Q = NSA_Q
    nqb = S // Q
    nt = S // NSA_TILE
    gate_blk = (W_NSA - LANE) // LANE
    kern = functools.partial(_nsa_attn_kernel, nt=nt)
    slab = lambda width, col: pl.BlockSpec((None, S, width), lambda b, i, f: (b, 0, col))
    grid_spec = pltpu.PrefetchScalarGridSpec(
        num_scalar_prefetch=1,
        grid=(B, nqb),
        in_specs=[pl.BlockSpec((None, Q, NSA_HEADS * LANE), lambda b, i, f: (b, i, 0)),
                  pl.BlockSpec((2 * NSA_HEADS, LANE), lambda b, i, f: (0, 0)),
                  pl.BlockSpec((None, Q, LANE), lambda b, i, f: (b, i, gate_blk)),
                  pl.BlockSpec((None, Q, GROUP_W), lambda b, i, f: (b, i, 0)),
                  pl.BlockSpec((None, Q, LANE), lambda b, i, f: (b, i, 0)),
                  slab(2 * LANE, 0), slab(LANE, 2), slab(LANE, 3), slab(LANE, 4)],
        out_specs=pl.BlockSpec((None, Q, GROUP_W), lambda b, i, f: (b, i, 0)),
        scratch_shapes=[pltpu.VMEM((NSA_HEADS * Q, LANE), F32), pltpu.VMEM((NSA_HEADS * Q, LANE), F32)],
    )
    return pl.pallas_call(
        kern, name="nsa_attend",
        grid_spec=grid_spec,
        out_shape=jax.ShapeDtypeStruct((B, S, GROUP_W), BF16),
        compiler_params=_cparams("parallel", "parallel"),
    )(flags, p_nsa, qtab, p_nsa, o_c, selb, kv, kv, kv, kv)


def _nsa_tables(S):
    nc = S // NSA_CMP_STRIDE
    n = np.arange(nc)[None, :]
    j = np.arange(LANE)[:, None]
    start = n * NSA_CMP_STRIDE
    ov = (start < (j + 1) * NSA_SLC_LEN) & (start + NSA_CMP_LEN - 1 >= j * NSA_SLC_LEN)
    ov &= (n < (S - NSA_CMP_LEN) // NSA_CMP_STRIDE + 1) & (j < S // NSA_SLC_LEN)
    pos = np.arange(S)
    k_zero = np.zeros((S, NSA_DK), np.float32)
    block_onehot = (pos[:, None] // NSA_SLC_LEN == np.arange(LANE)[None, :]).astype(np.float32)
    v_lanes = np.zeros((S, LANE), np.float32)
    v_lanes[:, ONES_LANE] = 1.0
    kv_table = np.concatenate([k_zero, _nsa_pos_lanes(pos), block_onehot, v_lanes,
                               k_zero, _nsa_pos_lanes(pos), v_lanes], axis=1)
    assert kv_table.shape[1] == W_KV
    return _nsa_query_table(), jnp.asarray(ov.astype(np.float32)), jnp.asarray(kv_table, dtype=BF16)


def _nsa(p_nsa, kv, uk, uv, pe_k, w1_k, w2_k, pe_v, w1_v, w2_v, tables):
    B, S, _ = p_nsa.shape
    qtab, overlap_t, _ = tables
    kc, vc = _nsa_compress(uk, uv, pe_k, w1_k, w2_k, pe_v, w1_v, w2_v)
    o_c, selb, blk_any = _nsa_select(p_nsa, qtab, kc, vc, overlap_t)
    per_tile = NSA_TILE // NSA_SLC_LEN
    nt = S // NSA_TILE
    not_sink = (np.arange(nt * per_tile) >= NSA_SINK // NSA_SLC_LEN).astype(np.float32)
    blk_any = blk_any[:, :, 0, :nt * per_tile] * not_sink
    flags = blk_any.reshape(B, S // NSA_Q, nt, per_tile).max(axis=-1)
    flags = (flags > 0).astype(jnp.int32).reshape(-1)
    return _nsa_attend(p_nsa, qtab, o_c, selb, flags, kv)


def _out_proj_kernel(h_ref, ya_ref, yb_ref, yc_ref, yd_ref, w_ref, g_ref, b_ref, o_ref, wb_ref):
    @pl.when(pl.program_id(0) == 0)
    def _():
        wb_ref[...] = w_ref[...].astype(BF16)

    mix = None
    for idx, y_ref in enumerate((ya_ref, yb_ref, yc_ref, yd_ref)):
        part = _mm(y_ref[...], wb_ref[idx * GROUP_W:(idx + 1) * GROUP_W, :])
        mix = part if mix is None else mix + part
    o_ref[...] = _layer_norm(DEEPNORM_ALPHA * h_ref[...] + mix, g_ref[...], b_ref[...])


def _out_proj(h2, ys, w_out, layer, g, b, tm=512):
    T, D = h2.shape
    row = lambda w: pl.BlockSpec((tm, w), lambda i: (i, 0))
    const = lambda shape: pl.BlockSpec(shape, lambda i: (0,) * len(shape))
    return pl.pallas_call(
        _out_proj_kernel, name="out_proj_ln",
        grid=(T // tm,),
        in_specs=[row(D), row(GROUP_W), row(GROUP_W), row(GROUP_W), row(GROUP_W),
                  pl.BlockSpec((None, D, D), lambda i: (layer, 0, 0)), const((1, D)), const((1, D))],
        out_specs=row(D),
        out_shape=jax.ShapeDtypeStruct((T, D), F32),
        scratch_shapes=[pltpu.VMEM((D, D), BF16)],
        compiler_params=_cparams("arbitrary"),
    )(h2, *ys, w_out, g.reshape(1, D), b.reshape(1, D))


def _mlp_kernel(h_ref, w1_ref, w2_ref, g_ref, b_ref, o_ref, acc_ref):
    f = pl.program_id(1)

    @pl.when(f == 0)
    def _():
        acc_ref[...] = jnp.zeros_like(acc_ref)

    a = jnp.maximum(_mm(h_ref[...], w1_ref[...]), 0.0)
    acc_ref[...] += _mm(a * a, w2_ref[...])

    @pl.when(f == pl.num_programs(1) - 1)
    def _():
        o_ref[...] = _layer_norm(DEEPNORM_ALPHA * h_ref[...] + acc_ref[...], g_ref[...], b_ref[...])


def _mlp(h2, w1, w2, layer, g, b, tm=1024, tf=1024):
    T, D = h2.shape
    F = w1.shape[2]
    return pl.pallas_call(
        _mlp_kernel, name="mlp_ln",
        grid=(T // tm, F // tf),
        in_specs=[pl.BlockSpec((tm, D), lambda i, f: (i, 0)),
                  pl.BlockSpec((None, D, tf), lambda i, f: (layer, 0, f)),
                  pl.BlockSpec((None, tf, D), lambda i, f: (layer, f, 0)),
                  pl.BlockSpec((1, D), lambda i, f: (0, 0)),
                  pl.BlockSpec((1, D), lambda i, f: (0, 0))],
        out_specs=pl.BlockSpec((tm, D), lambda i, f: (i, 0)),
        out_shape=jax.ShapeDtypeStruct((T, D), F32),
        scratch_shapes=[pltpu.VMEM((tm, D), F32)],
        compiler_params=_cparams("parallel", "arbitrary"),
    )(h2, w1, w2, g.reshape(1, D), b.reshape(1, D))


def kernel(x, ln_emb_g, ln_emb_b, w_in, conv_w, conv_b, dt_bias, a_log, d_skip, ssm_norm_g, q_norm_g, w_uq, kv_norm_g, w_ukv, cmp_pe_k, cmp_w1_k, cmp_w2_k, cmp_pe_v, cmp_w1_v, cmp_w2_v, w_out, ln1_g, ln1_b, w_mlp1, w_mlp2, ln2_g, ln2_b):
    B, S, D = x.shape
    assert D == D_MODEL and S % NSA_TILE == 0 and S // NSA_SLC_LEN <= LANE
    T = B * S
    ret_tables = _ret_tables(S)
    mla_tables = _mla_tables(S)
    nsa_tables = _nsa_tables(S)
    h = x.reshape(T, D)
    for l in range(w_in.shape[0]):
        if l == 0:
            h, *proj = _in_proj(h, _layout_w_in(w_in[l]), nsa_tables[2], entry_ln=(ln_emb_g, ln_emb_b))
        else:
            proj = _in_proj(h, _layout_w_in(w_in[l]), nsa_tables[2])
        p_ssm, p_mla, p_ret, p_nsa, nsa_kv, uk, uv = proj
        cmp_rows = (B, S // NSA_CMP_STRIDE, uk.shape[-1])
        y_a, y_c = _ssm_and_retention(p_ssm.reshape(B, S, W_SSM), conv_w[l], conv_b[l], dt_bias[l], a_log[l],
                                      d_skip[l], ssm_norm_g[l], p_ret.reshape(B, S, W_RET), ret_tables)
        wq, wk, wv = _layout_mla_weights(w_uq[l], w_ukv[l])
        q, k, v = _mla_prep(p_mla.reshape(B, S, W_MLA), q_norm_g[l], kv_norm_g[l], wq, wk, wv, mla_tables)
        y_b = _mla_attn(q, k, v)
        y_d = _nsa(p_nsa.reshape(B, S, W_NSA), nsa_kv.reshape(B, S, W_KV), uk.reshape(cmp_rows), uv.reshape(cmp_rows),
                   cmp_pe_k[l], cmp_w1_k[l], cmp_w2_k[l], cmp_pe_v[l], cmp_w1_v[l], cmp_w2_v[l], nsa_tables)
        ys = [y.reshape(T, GROUP_W) for y in (y_a, y_b, y_c, y_d)]
        h = _out_proj(h, ys, w_out, l, ln1_g[l], ln1_b[l])
        h = _mlp(h, w_mlp1, w_mlp2, l, ln2_g[l], ln2_b[l])
    return h.reshape(B, S, D)
```

```python
import functools
import math

import jax
import jax.numpy as jnp
import numpy as np
from jax import lax
from jax.experimental import pallas as pl
from jax.experimental.pallas import tpu as pltpu

F32 = jnp.float32
BF16 = jnp.bfloat16
HIGHEST = lax.Precision.HIGHEST

D_MODEL = 1024
DEPTH = 2
GROUP_W = D_MODEL // 4
SSM_HEADS = 4
SSM_HEAD_DIM = GROUP_W // SSM_HEADS
SSM_GROUPS = 2
SSM_STATE = 128
SSM_CONV = 4
SSM_CHUNK = 128
SSM_XBC = GROUP_W + 2 * SSM_GROUPS * SSM_STATE
MLA_HEADS = 4
MLA_NOPE = 64
MLA_ROPE = 32
MLA_V = GROUP_W // MLA_HEADS
MLA_Q_RANK = 256
MLA_KV_RANK = 128
RET_HEADS = 4
RET_DK = 64
RET_DV = GROUP_W // RET_HEADS
RET_CHUNK = 128
NSA_HEADS = 4
NSA_DK = 64
NSA_DV = GROUP_W // NSA_HEADS
NSA_CMP_LEN = 32
NSA_CMP_STRIDE = 16
NSA_CMP_HID = 256
NSA_SLC_LEN = 64
NSA_TOPN = 16
NSA_WIN = 512
D_FF = 4 * D_MODEL
NSA_Q = 512
ROPE_THETA = 10000.0
EPS = 1e-5
NEG_INF = -1e30
LOG2_E = math.log2(math.e)
FORCED_SCORE = 1e9
DEEPNORM_ALPHA = (2.0 * DEPTH) ** 0.25

IN_SPLITS = (
    GROUP_W, SSM_XBC, SSM_HEADS,
    MLA_Q_RANK, MLA_KV_RANK, MLA_ROPE,
    RET_HEADS * RET_DK, RET_HEADS * RET_DK, RET_HEADS * RET_DV, GROUP_W,
    NSA_HEADS * NSA_DK, NSA_DK, NSA_DV, NSA_DK, NSA_DV, NSA_DK, NSA_DV, 3 * NSA_HEADS,
)

LANE = 128
W_SSM = GROUP_W + SSM_XBC + LANE
W_MLA = MLA_Q_RANK + MLA_KV_RANK + LANE
W_RET = 4 * GROUP_W
W_NSA = NSA_HEADS * LANE + LANE + LANE
W_KV = 2 * LANE + 3 * LANE
W_PROJ = W_SSM + W_MLA + W_RET + W_NSA + 4 * NSA_DK

NSA_TILE = 512
NSA_SINK = 128
MLA_TILE = 1024
VMEM_LIMIT = 48 * 1024 * 1024


def _cparams(*sem):
    return pltpu.CompilerParams(dimension_semantics=sem, vmem_limit_bytes=VMEM_LIMIT)


def _mm(a, b):
    return jnp.dot(a.astype(BF16), b.astype(BF16), preferred_element_type=F32)


def _mm_nt(a, b):
    return lax.dot_general(a.astype(BF16), b.astype(BF16), (((1,), (1,)), ((), ())),
                           preferred_element_type=F32)


def _mm_f32(a, b):
    return jnp.dot(a, b, precision=HIGHEST, preferred_element_type=F32)


def _silu(x):
    return x * jax.nn.sigmoid(x)


def _softplus(x):
    return jnp.maximum(x, 0.0) + jnp.log1p(jnp.exp(-jnp.abs(x)))


def _layer_norm(x, g, b):
    mu = jnp.mean(x, axis=-1, keepdims=True)
    xc = x - mu
    var = jnp.mean(xc * xc, axis=-1, keepdims=True)
    return xc * lax.rsqrt(var + EPS) * g + b


def _iota(shape, dim):
    return lax.broadcasted_iota(jnp.int32, shape, dim)


def _pad_cols(w, width):
    return jnp.pad(w, ((0, 0), (0, width - w.shape[1])))


def _layout_w_in(w):
    offs = np.concatenate([[0], np.cumsum(IN_SPLITS)])
    p = [w[:, int(offs[i]):int(offs[i + 1])] for i in range(len(IN_SPLITS))]
    (ssm_z, ssm_xbc, ssm_dt, mla_cq, mla_ckv, mla_kr, ret_q, ret_k, ret_v, ret_g,
     nsa_q, nsa_kc, nsa_vc, nsa_ks, nsa_vs, nsa_kw, nsa_vw, nsa_gate) = p
    nsa_q_heads = [_pad_cols(nsa_q[:, h * NSA_DK:(h + 1) * NSA_DK], LANE) for h in range(NSA_HEADS)]
    cols = [ssm_z, ssm_xbc, _pad_cols(ssm_dt, LANE),
            mla_cq, mla_ckv, _pad_cols(mla_kr, LANE),
            ret_q, ret_k, ret_v, ret_g,
            *nsa_q_heads, nsa_kc, nsa_vc, _pad_cols(nsa_gate, LANE),
            nsa_ks, nsa_vs, nsa_kw, nsa_vw]
    out = jnp.concatenate(cols, axis=1)
    assert out.shape[1] == W_PROJ
    return out.astype(BF16)


def _in_proj_kernel(*refs, entry_ln):
    if entry_ln:
        h_ref, g_ref, b_ref, w_ref, kvtab_ref, hn_ref, *outs = refs
        hn = _layer_norm(h_ref[...], g_ref[...], b_ref[...])
        hn_ref[...] = hn
    else:
        h_ref, w_ref, kvtab_ref, *outs = refs
        hn = h_ref[...]
    ssm_ref, mla_ref, ret_ref, nsa_ref, kv_ref, uk_ref, uv_ref, kcv_ref = outs
    hb = hn.astype(BF16)
    off = 0
    for ref, width in ((ssm_ref, W_SSM), (mla_ref, W_MLA), (ret_ref, W_RET), (nsa_ref, W_NSA)):
        ref[...] = jnp.dot(hb, w_ref[:, off:off + width], preferred_element_type=F32)
        off += width
    groups = h_ref.shape[0] // NSA_CMP_STRIDE
    kc_lane = NSA_HEADS * LANE
    kcv_ref[...] = nsa_ref[:, kc_lane:kc_lane + LANE]
    for t in range(NSA_CMP_STRIDE):
        piece = kcv_ref[pl.ds(t, groups, stride=NSA_CMP_STRIDE), :]
        uk_ref[:, t * NSA_DK:(t + 1) * NSA_DK] = piece[:, :NSA_DK]
        uv_ref[:, t * NSA_DV:(t + 1) * NSA_DV] = piece[:, NSA_DK:]
    kv = jnp.dot(hb, w_ref[:, off:off + 4 * NSA_DK], preferred_element_type=F32)
    low = _iota((kv.shape[0], LANE), 1) < NSA_DK
    sel_kv, win_kv = kv[:, :LANE], kv[:, LANE:]
    pieces = {0: sel_kv, 2: pltpu.roll(sel_kv, NSA_DK, 1), 3: win_kv, 4: pltpu.roll(win_kv, NSA_DK, 1)}
    for slab in range(W_KV // LANE):
        lanes = slice(slab * LANE, (slab + 1) * LANE)
        tab = kvtab_ref[:, lanes]
        if slab in pieces:
            kv_ref[:, lanes] = (jnp.where(low, pieces[slab], 0.0) + tab.astype(F32)).astype(BF16)
        else:
            kv_ref[:, lanes] = tab


def _in_proj(h2, w_p, kv_table, entry_ln=None, tm=256):
    T, D = h2.shape
    S = kv_table.shape[0]
    widths = (W_SSM, W_MLA, W_RET, W_NSA)
    half = NSA_CMP_STRIDE * NSA_DK
    row = lambda w: pl.BlockSpec((tm, w), lambda i: (i, 0))
    const = lambda shape: pl.BlockSpec(shape, lambda i: (0,) * len(shape))
    in_specs = [const((D, W_PROJ)), pl.BlockSpec((tm, W_KV), lambda i: (i % (S // tm), 0))]
    out_specs = [row(w) for w in widths + (W_KV,)] + [pl.BlockSpec((tm // NSA_CMP_STRIDE, half), lambda i: (i, 0))] * 2
    out_shape = ([jax.ShapeDtypeStruct((T, w), F32) for w in widths] + [jax.ShapeDtypeStruct((T, W_KV), BF16)]
                 + [jax.ShapeDtypeStruct((T // NSA_CMP_STRIDE, half), F32)] * 2)
    operands = (w_p, kv_table)
    if entry_ln is not None:
        in_specs = [const((1, D)), const((1, D))] + in_specs
        out_specs = [row(D)] + out_specs
        out_shape = [jax.ShapeDtypeStruct((T, D), F32)] + out_shape
        operands = tuple(v.reshape(1, D) for v in entry_ln) + operands
    return pl.pallas_call(
        functools.partial(_in_proj_kernel, entry_ln=entry_ln is not None), name="in_proj",
        grid=(T // tm,),
        in_specs=[row(D)] + in_specs,
        out_specs=out_specs,
        out_shape=out_shape,
        scratch_shapes=[pltpu.VMEM((tm, LANE), F32)],
        compiler_params=_cparams("parallel"),
    )(h2, *operands)


def _ssm_chunk(p_ref, cw_ref, cb_ref, dtb_ref, alog_ref, dskip_ref, ng_ref, o_ref, state_ref, ext_ref):
    L, H, P, N = SSM_CHUNK, SSM_HEADS, SSM_HEAD_DIM, SSM_STATE
    z = p_ref[:, 0:GROUP_W]
    ext_ref[8:8 + L, :] = p_ref[:, GROUP_W:GROUP_W + SSM_XBC]
    conv = cb_ref[...]
    for j in range(SSM_CONV):
        conv = conv + ext_ref[pl.ds(8 - (SSM_CONV - 1) + j, L), :] * cw_ref[j:j + 1, :]
    ext_ref[0:8, :] = ext_ref[L:L + 8, :]
    xbc = _silu(conv)
    xs = xbc[:, 0:GROUP_W]
    b_in = xbc[:, GROUP_W:GROUP_W + SSM_GROUPS * N]
    c_in = xbc[:, GROUP_W + SSM_GROUPS * N:]

    dt = _softplus(p_ref[:, GROUP_W + SSM_XBC:] + dtb_ref[...])
    a = dt * (-jnp.exp(alog_ref[...]))
    row = _iota((L, L), 0)
    col = _iota((L, L), 1)
    tril = col <= row
    cs = _mm_f32(jnp.where(tril, 1.0, 0.0), a)
    cs_t = cs.T
    ecs = jnp.exp(cs)
    dte = jnp.exp(cs[L - 1:L, :] - cs)
    expand = jnp.where(_iota((LANE, H * P), 0) == _iota((LANE, H * P), 1) // P, 1.0, 0.0)
    dt_x = _mm_f32(dt, expand)
    ecs_x = _mm_f32(ecs, expand)
    dte_x = _mm_f32(dte, expand)

    xdt = xs * dt_x
    wx = xdt * dte_x
    head_of_lane = _iota((L, H * P), 1) // P
    y = xs * dskip_ref[...]
    y_off = []
    rep = H // SSM_GROUPS
    for g in range(SSM_GROUPS):
        cg = c_in[:, g * N:(g + 1) * N]
        bg = b_in[:, g * N:(g + 1) * N]
        cb = _mm_nt(cg, bg)
        for h in range(g * rep, (g + 1) * rep):
            diff = cs[:, h:h + 1] - cs_t[h:h + 1, :]
            seg = jnp.where(tril, jnp.exp(jnp.where(tril, diff, 0.0)), 0.0)
            yh = _mm(cb * seg, xdt)
            y = y + jnp.where(head_of_lane == h, yh, 0.0)
        lanes = slice(g * rep * P, (g + 1) * rep * P)
        st_prev = state_ref[:, lanes]
        y_off.append(_mm(cg, st_prev))
        state_ref[:, lanes] = st_prev * ecs_x[L - 1:L, lanes] + _mm(bg.T, wx[:, lanes])
    y = y + jnp.concatenate(y_off, axis=1) * ecs_x
    y = y * _silu(z)
    ms = jnp.mean(y * y, axis=-1, keepdims=True)
    o_ref[...] = (y * lax.rsqrt(ms + EPS) * ng_ref[...]).astype(o_ref.dtype)


def _recurrent_kernel(ps_ref, cw_ref, cb_ref, dtb_ref, alog_ref, dskip_ref, ng_ref,
                      pr_ref, cos_ref, sin_ref, dec_ref, zeta_ref, xi_ref, cd_ref,
                      oa_ref, oc_ref, sstate_ref, ext_ref, rstate_ref):
    @pl.when(pl.program_id(0) == 0)
    def _():
        sstate_ref[...] = jnp.zeros_like(sstate_ref)
        rstate_ref[...] = jnp.zeros_like(rstate_ref)
        ext_ref[:, 0:8, :] = jnp.zeros((ext_ref.shape[0], 8, SSM_XBC), F32)

    for b in range(ps_ref.shape[0]):
        _ssm_chunk(ps_ref.at[b], cw_ref, cb_ref, dtb_ref, alog_ref, dskip_ref, ng_ref,
                   oa_ref.at[b], sstate_ref.at[b], ext_ref.at[b])
        _ret_chunk(pr_ref.at[b], cos_ref, sin_ref, dec_ref, zeta_ref, xi_ref, cd_ref, oc_ref.at[b], rstate_ref.at[b])


def _ret_chunk(p_ref, cos_ref, sin_ref, dec_ref, zeta_ref, xi_ref, cd_ref, o_ref, state_ref):
    L, H, DK, DV = RET_CHUNK, RET_HEADS, RET_DK, RET_DV
    W = H * DK
    q = p_ref[:, 0:W]
    k = p_ref[:, W:2 * W]
    v = p_ref[:, 2 * W:3 * W]
    gate = p_ref[:, 3 * W:4 * W]
    lane = _iota((L, W), 1)
    first_half = (lane % DK) < (DK // 2)
    head_of_lane = lane // DK

    def rope(x):
        partner = jnp.where(first_half, pltpu.roll(x, W - DK // 2, 1), pltpu.roll(x, DK // 2, 1))
        return x * cos_ref[...] + partner * sin_ref[...]

    qr = rope(q)
    kr = rope(k) * (DK ** -0.5)
    y = jnp.zeros((L, H * DV), F32)
    for h in range(H):
        qh = jnp.where(head_of_lane == h, qr, 0.0)
        sc = _mm_nt(qh, kr) * dec_ref[h]
        y = y + jnp.where(head_of_lane == h, _mm(sc, v), 0.0)
    st = state_ref[...]
    y = y + _mm(qr * xi_ref[...], st)
    same_head = (_iota((W, H * DV), 0) // DK) == (_iota((W, H * DV), 1) // DV)
    kv = _mm((kr * zeta_ref[...]).T, v)
    state_ref[...] = st * cd_ref[...] + jnp.where(same_head, kv, 0.0)
    ms = _mm_f32(y * y, jnp.where(same_head, 1.0 / DV, 0.0))
    o_ref[...] = (y * lax.rsqrt(ms + EPS) * _silu(gate)).astype(o_ref.dtype)


def _ret_tables(S):
    H, DK, L = RET_HEADS, RET_DK, RET_CHUNK
    inv = ROPE_THETA ** (-np.arange(0, DK, 2, dtype=np.float64) / DK)
    ang = np.arange(S, dtype=np.float64)[:, None] * inv[None, :]
    cos, sin = np.cos(ang), np.sin(ang)
    cos_t = np.tile(np.concatenate([cos, cos], axis=1), (1, H))
    sin_t = np.tile(np.concatenate([-sin, sin], axis=1), (1, H))
    log_gamma = np.log1p(-np.exp2(-5.0 - np.arange(H, dtype=np.float64)))
    pos = np.arange(L, dtype=np.float64)
    diff = pos[:, None] - pos[None, :]
    decay_in = np.where(diff >= 0, np.exp(np.maximum(diff, 0.0)[None] * log_gamma[:, None, None]), 0.0)
    zeta = np.exp((L - 1 - pos)[None] * log_gamma[:, None])
    xi = np.exp((pos + 1.0)[None] * log_gamma[:, None])
    chunk_decay = np.exp(L * log_gamma)
    zeta_x = np.repeat(zeta.T, DK, axis=1)
    xi_x = np.repeat(xi.T, DK, axis=1)
    cd_x = np.repeat(chunk_decay, RET_DV).reshape(1, H * RET_DV)
    return tuple(jnp.asarray(t, dtype=F32) for t in (cos_t, sin_t, decay_in, zeta_x, xi_x, cd_x))


def _ssm_and_retention(p_ssm, conv_w, conv_b, dt_bias, a_log, d_skip, norm_g, p_ret, tables):
    B, S, _ = p_ssm.shape
    L, H = SSM_CHUNK, RET_HEADS
    assert RET_CHUNK == L
    W = H * RET_DK
    cos_t, sin_t, decay_in, zeta_x, xi_x, cd_x = tables
    pad_h = lambda v: jnp.pad(v, (0, LANE - SSM_HEADS)).reshape(1, LANE)
    const = lambda shape: pl.BlockSpec(shape, lambda c: (0,) * len(shape))
    chunk = lambda width: pl.BlockSpec((B, L, width), lambda c: (0, c, 0))
    out_shape = jax.ShapeDtypeStruct((B, S, GROUP_W), BF16)
    return pl.pallas_call(
        _recurrent_kernel, name="ssm_retention",
        grid=(S // L,),
        in_specs=[chunk(W_SSM),
                  const((SSM_CONV, SSM_XBC)), const((1, SSM_XBC)), const((1, LANE)), const((1, LANE)),
                  const((1, GROUP_W)), const((1, GROUP_W)),
                  chunk(W_RET),
                  pl.BlockSpec((L, W), lambda c: (c, 0)), pl.BlockSpec((L, W), lambda c: (c, 0)),
                  const((H, L, L)), const((L, W)), const((L, W)), const((1, H * RET_DV))],
        out_specs=[chunk(GROUP_W), chunk(GROUP_W)],
        out_shape=[out_shape, out_shape],
        scratch_shapes=[pltpu.VMEM((B, SSM_STATE, GROUP_W), F32),
                        pltpu.VMEM((B, L + 8, SSM_XBC), F32),
                        pltpu.VMEM((B, W, H * RET_DV), F32)],
        compiler_params=_cparams("arbitrary"),
    )(p_ssm, conv_w, conv_b.reshape(1, -1), pad_h(dt_bias), pad_h(a_log),
      jnp.repeat(d_skip, SSM_HEAD_DIM).reshape(1, GROUP_W), norm_g.reshape(1, GROUP_W),
      p_ret, cos_t, sin_t, decay_in, zeta_x, xi_x, cd_x)


def _mla_prep_kernel(p_ref, qg_ref, wq_ref, kvg_ref, wk_ref, wv_ref, cos_ref, sin_ref,
                     q_ref, k_ref, v_ref):
    tm = p_ref.shape[0]
    cq = p_ref[:, 0:MLA_Q_RANK]
    ckv = p_ref[:, MLA_Q_RANK:MLA_Q_RANK + MLA_KV_RANK]
    kr = p_ref[:, MLA_Q_RANK + MLA_KV_RANK:]

    def rms(x, g):
        return x * lax.rsqrt(jnp.mean(x * x, axis=-1, keepdims=True) + EPS) * g

    q = _mm(rms(cq, qg_ref[...]), wq_ref[...])
    kvl = rms(ckv, kvg_ref[...])
    kn = _mm(kvl, wk_ref[...])
    vv = _mm(kvl, wv_ref[...])
    kr_sh = pltpu.roll(kr, MLA_NOPE, 1)
    lane = _iota((tm, LANE), 1)
    half = MLA_ROPE // 2
    low = (lane >= MLA_NOPE) & (lane < MLA_NOPE + half)
    cos = cos_ref[...]
    sin = sin_ref[...]

    def rope(x):
        partner = jnp.where(low, pltpu.roll(x, LANE - half, 1), pltpu.roll(x, half, 1))
        return x * cos + partner * sin

    scale = (MLA_NOPE + MLA_ROPE) ** -0.5 * LOG2_E
    for h in range(MLA_HEADS):
        sl = slice(h * LANE, (h + 1) * LANE)
        q_ref[h] = (rope(q[:, sl]) * scale).astype(BF16)
        k_ref[h] = rope(kn[:, sl] + kr_sh).astype(BF16)
        v_ref[h] = jnp.where(lane == _mla_ones_lane(h), 1.0, vv[:, sl]).astype(BF16)


def _mla_tables(S):
    inv = ROPE_THETA ** (-np.arange(0, MLA_ROPE, 2, dtype=np.float64) / MLA_ROPE)
    ang = np.arange(S, dtype=np.float64)[:, None] * inv[None, :]
    cos, sin = np.cos(ang), np.sin(ang)
    tail = LANE - MLA_NOPE - MLA_ROPE
    cos_t = np.concatenate([np.ones((S, MLA_NOPE)), cos, cos, np.ones((S, tail))], axis=1)
    sin_t = np.concatenate([np.zeros((S, MLA_NOPE)), -sin, sin, np.zeros((S, tail))], axis=1)
    return jnp.asarray(cos_t, dtype=F32), jnp.asarray(sin_t, dtype=F32)


def _layout_mla_weights(w_uq, w_ukv):
    H = MLA_HEADS
    dq = MLA_NOPE + MLA_ROPE
    wq = jnp.concatenate([_pad_cols(w_uq[:, h * dq:(h + 1) * dq], LANE) for h in range(H)], axis=1)
    dkv = MLA_NOPE + MLA_V
    wk, wv = [], []
    for h in range(H):
        blk = w_ukv[:, h * dkv:(h + 1) * dkv]
        wk.append(_pad_cols(blk[:, :MLA_NOPE], LANE))
        v = blk[:, MLA_NOPE:]
        zero = jnp.zeros_like(v)
        wv.append(jnp.concatenate([v, zero] if h % 2 == 0 else [zero, v], axis=1))
    return wq.astype(BF16), jnp.concatenate(wk, axis=1).astype(BF16), jnp.concatenate(wv, axis=1).astype(BF16)


def _mla_prep(p_mla, q_norm_g, kv_norm_g, wq, wk, wv, tables, tm=512):
    B, S, _ = p_mla.shape
    H = MLA_HEADS
    cos_t, sin_t = tables
    const = lambda shape: pl.BlockSpec(shape, lambda b, i: (0,) * len(shape))
    qkv_spec = pl.BlockSpec((None, H, tm, LANE), lambda b, i: (b, 0, i, 0))
    qkv_shape = jax.ShapeDtypeStruct((B, H, S, LANE), BF16)
    return pl.pallas_call(
        _mla_prep_kernel, name="mla_prep",
        grid=(B, S // tm),
        in_specs=[pl.BlockSpec((None, tm, W_MLA), lambda b, i: (b, i, 0)),
                  const((1, MLA_Q_RANK)), const((MLA_Q_RANK, H * LANE)),
                  const((1, MLA_KV_RANK)), const((MLA_KV_RANK, H * LANE)), const((MLA_KV_RANK, H * LANE)),
                  pl.BlockSpec((tm, LANE), lambda b, i: (i, 0)),
                  pl.BlockSpec((tm, LANE), lambda b, i: (i, 0))],
        out_specs=[qkv_spec, qkv_spec, qkv_spec],
        out_shape=[qkv_shape, qkv_shape, qkv_shape],
        compiler_params=_cparams("parallel", "parallel"),
    )(p_mla, q_norm_g.reshape(1, -1), wq, kv_norm_g.reshape(1, -1), wk, wv, cos_t, sin_t)


def _mla_ones_lane(h):
    return MLA_V if h % 2 == 0 else 0


def _mla_attn_kernel(qi_ref, kj_ref, q_ref, k_ref, v_ref, o_ref, m_ref, acc_ref):
    H = MLA_HEADS
    tq, tk = q_ref.shape[1], k_ref.shape[1]
    i = qi_ref[pl.program_id(1)]
    j = kj_ref[pl.program_id(1)]

    @pl.when(j == 0)
    def _():
        m_ref[...] = jnp.full(m_ref.shape, NEG_INF, F32)
        acc_ref[...] = jnp.zeros_like(acc_ref)

    def sweep(blocks):
        nt_dims = (((1,), (1,)), ((), ()))
        scores = [[lax.dot_general(q_ref[h, r0:r0 + nr, :], k_ref[h, 0:nk, :], nt_dims, preferred_element_type=F32)
                   for (r0, nr, nk, _) in blocks] for h in range(H)]
        for h in range(H):
            for (r0, nr, nk, offset), s in zip(blocks, scores[h]):
                if offset is not None:
                    s = jnp.where(_iota((nr, nk), 1) - _iota((nr, nk), 0) <= offset, s, NEG_INF)
                rows = slice(r0, r0 + nr)
                m_prev = m_ref[h, rows]
                m_new = jnp.maximum(m_prev, jnp.max(s, axis=-1, keepdims=True))
                p = jnp.exp2(s - jnp.tile(m_new, (1, nk // LANE)))
                acc_ref[h, rows] = (jnp.exp2(m_prev - m_new) * acc_ref[h, rows]
                                    + jnp.dot(p.astype(BF16), v_ref[h, 0:nk, :], preferred_element_type=F32))
                m_ref[h, rows] = m_new

    assert tq == tk
    half = tq // 2

    @pl.when(j < i)
    def _():
        sweep([(0, tq, tk, None)])

    @pl.when(j == i)
    def _():
        sweep([(0, half, half, 0), (half, half, tk, half)])
        lane = _iota((tq, LANE), 1)
        for pair in range(H // 2):
            he, ho = 2 * pair, 2 * pair + 1
            acc_e, acc_o = acc_ref[he], acc_ref[ho]
            le = acc_e[:, _mla_ones_lane(he):_mla_ones_lane(he) + 1]
            lo = acc_o[:, _mla_ones_lane(ho):_mla_ones_lane(ho) + 1]
            o_ref[:, pair * LANE:(pair + 1) * LANE] = jnp.where(lane < MLA_V, acc_e / le, acc_o / lo).astype(o_ref.dtype)


def _mla_attn(q, k, v):
    B, H, S, _ = q.shape
    t = min(MLA_TILE, S)
    tq = t
    pairs = [(i, j) for i in range(S // tq) for j in range((i + 1) * tq // t)]
    qi = jnp.asarray([p[0] for p in pairs], jnp.int32)
    kj = jnp.asarray([p[1] for p in pairs], jnp.int32)
    grid_spec = pltpu.PrefetchScalarGridSpec(
        num_scalar_prefetch=2,
        grid=(B, len(pairs)),
        in_specs=[pl.BlockSpec((None, H, tq, LANE), lambda b, p, qi, kj: (b, 0, qi[p], 0)),
                  pl.BlockSpec((None, H, t, LANE), lambda b, p, qi, kj: (b, 0, kj[p], 0)),
                  pl.BlockSpec((None, H, t, LANE), lambda b, p, qi, kj: (b, 0, kj[p], 0))],
        out_specs=pl.BlockSpec((None, tq, GROUP_W), lambda b, p, qi, kj: (b, qi[p], 0)),
        scratch_shapes=[pltpu.VMEM((H, tq, LANE), F32), pltpu.VMEM((H, tq, LANE), F32)],
    )
    return pl.pallas_call(
        _mla_attn_kernel, name="mla_attn",
        grid_spec=grid_spec,
        out_shape=jax.ShapeDtypeStruct((B, S, GROUP_W), BF16),
        compiler_params=_cparams("parallel", "arbitrary"),
    )(qi, kj, q, k, v)


POS_HI = NSA_DK
POS_LO = NSA_DK + 3
POS_ONE = NSA_DK + 6
ONES_LANE = NSA_DV


def _split_bf16(x, parts=3):
    out, rem = [], np.float64(x)
    for _ in range(parts):
        piece = np.float64(np.float32(rem).astype(jnp.bfloat16).astype(np.float32))
        out.append(float(piece))
        rem = rem - piece
    return out


def _nsa_query_table():
    H = NSA_HEADS
    tab = np.zeros((2 * H, LANE), np.float32)
    for h in range(H):
        c = 2.0 ** (-8.0 * (h + 1) / H) * LOG2_E
        pieces = _split_bf16(c)
        tab[h, POS_HI:POS_HI + 3] = pieces
        tab[h, POS_LO:POS_LO + 3] = pieces
        tab[H + h, POS_ONE] = -sum(pieces)
    return jnp.asarray(tab)


def _nsa_pos_lanes(pos, lo_offset=0.0):
    t = np.zeros((len(pos), LANE - NSA_DK), np.float32)
    t[:, POS_HI - NSA_DK:POS_HI - NSA_DK + 3] = (NSA_SLC_LEN * (pos // NSA_SLC_LEN))[:, None]
    t[:, POS_LO - NSA_DK:POS_LO - NSA_DK + 3] = (pos % NSA_SLC_LEN + lo_offset)[:, None]
    t[:, POS_ONE - NSA_DK] = 1.0
    return t


def _nsa_queries(q_ref, qtab_ref, qb):
    Q, H = q_ref.shape[0], NSA_HEADS
    qpos = (qb * Q + _iota((Q, 1), 0)).astype(F32)
    out = []
    for h in range(H):
        q = q_ref[:, h * LANE:(h + 1) * LANE] * (NSA_DK ** -0.5 * LOG2_E)
        out.append((q + qtab_ref[h:h + 1, :] + qtab_ref[H + h:H + h + 1, :] * qpos).astype(BF16))
    return out


def _normalise(o):
    return o / o[:, ONES_LANE:ONES_LANE + 1]


def _stacked_gate(gates, branch):
    lanes = [3 * h + branch for h in range(NSA_HEADS)]
    return jnp.concatenate([gates[:, c:c + 1] for c in lanes], axis=0)


def _unstack_heads(o):
    Q = o.shape[0] // NSA_HEADS
    lane = _iota((Q, LANE), 1)
    out = []
    for pair in range(NSA_HEADS // 2):
        even = o[(2 * pair) * Q:(2 * pair + 1) * Q]
        odd = o[(2 * pair + 1) * Q:(2 * pair + 2) * Q]
        out.append(jnp.where(lane < NSA_DV, even, pltpu.roll(odd, NSA_DV, 1)))
    return jnp.concatenate(out, axis=1)


def _nsa_cmp_kernel(uk_ref, uv_ref, pek_ref, pev_ref, w1k_ref, w1v_ref, w2k_ref, w2v_ref, cpos_ref,
                    kc_ref, vc_ref, sh_ref):
    nb = uk_ref.shape[0]
    half = uk_ref.shape[1]

    def hidden(u_ref, pe_ref, w1_ref):
        u = u_ref[...]
        first = _mm(u + pe_ref[0:1, :], w1_ref[0:half, :])
        second = _mm(u + pe_ref[1:2, :], w1_ref[half:2 * half, :])
        sh_ref[0:nb, :] = second
        sh_ref[nb:nb + 8, :] = jnp.zeros((8, NSA_CMP_HID), F32)
        return first + sh_ref[pl.ds(1, nb), :]

    hk = _silu(hidden(uk_ref, pek_ref, w1k_ref))
    hv = _silu(hidden(uv_ref, pev_ref, w1v_ref))
    kc_ref[...] = (_mm(hk, w2k_ref[...]) + cpos_ref[...]).astype(BF16)
    ones_lane = jnp.where(_iota((1, LANE), 1) == ONES_LANE, 1.0, 0.0)
    vc_ref[...] = (_mm(hv, w2v_ref[...]) + ones_lane).astype(BF16)


def _nsa_compress(uk, uv, pe_k, w1_k, w2_k, pe_v, w1_v, w2_v):
    B, nb, half = uk.shape
    hid = NSA_CMP_HID
    const = lambda shape: pl.BlockSpec(shape, lambda b: (0,) * len(shape))
    w2k = _pad_cols(w2_k, LANE).astype(BF16)
    w2v = _pad_cols(w2_v, LANE).astype(BF16)
    centre = _nsa_pos_lanes(np.arange(nb) * NSA_CMP_STRIDE, 0.5 * (NSA_CMP_LEN - 1))
    cpos = jnp.asarray(np.concatenate([np.zeros((nb, NSA_DK), np.float32), centre], axis=1))
    out_spec = pl.BlockSpec((None, nb, LANE), lambda b: (b, 0, 0))
    out_shape = jax.ShapeDtypeStruct((B, nb, LANE), BF16)
    return pl.pallas_call(
        _nsa_cmp_kernel, name="nsa_compress",
        grid=(B,),
        in_specs=[pl.BlockSpec((None, nb, half), lambda b: (b, 0, 0)),
                  pl.BlockSpec((None, nb, half), lambda b: (b, 0, 0)),
                  const((2, half)), const((2, half)),
                  const((2 * half, hid)), const((2 * half, hid)),
                  const((hid, LANE)), const((hid, LANE)), const((nb, LANE))],
        out_specs=[out_spec, out_spec],
        out_shape=[out_shape, out_shape],
        scratch_shapes=[pltpu.VMEM((nb + 8, hid), F32)],
        compiler_params=_cparams("parallel"),
    )(uk, uv, pe_k.reshape(2, half), pe_v.reshape(2, half), w1_k.astype(BF16), w1_v.astype(BF16), w2k, w2v, cpos)


def _nsa_sel_kernel(q_ref, qtab_ref, gate_ref, kc_ref, vc_ref, ovt_ref, oc_ref, selb_ref, any_ref, imp_ref, *, n_slc, top_n):
    Q, H = q_ref.shape[0], NSA_HEADS
    qb = pl.program_id(1)
    nc = kc_ref.shape[0]
    qs = jnp.concatenate(_nsa_queries(q_ref, qtab_ref, qb), axis=0)
    nt_dims = (((1,), (1,)), ((), ()))

    def attend(ncols):
        s = lax.dot_general(qs, kc_ref[0:ncols, :], nt_dims, preferred_element_type=F32)
        qpos = qb * Q + (_iota((H * Q, ncols), 0) & (Q - 1))
        block_end = _iota((H * Q, ncols), 1) * NSA_CMP_STRIDE + (NSA_CMP_LEN - 1)
        s = jnp.where(block_end <= qpos, s, NEG_INF)
        e = jnp.exp2(s - jnp.max(s, axis=-1, keepdims=True))
        qpos_col = qb * Q + (_iota((H * Q, 1), 0) & (Q - 1))
        has_block = jnp.where(qpos_col >= NSA_CMP_LEN - 1, 1.0, 0.0)
        p = e * (has_block / jnp.sum(e, axis=-1, keepdims=True))
        o_c = jnp.dot(p.astype(BF16), vc_ref[0:ncols, :], preferred_element_type=F32)
        oc_ref[...] = _unstack_heads(_stacked_gate(jax.nn.sigmoid(gate_ref[...]), 0) * o_c)
        p_sum = p[0:Q]
        for h in range(1, H):
            p_sum = p_sum + p[h * Q:(h + 1) * Q]
        imp_ref[...] = lax.dot_general(ovt_ref[:, 0:ncols], p_sum, nt_dims, precision=HIGHEST,
                                       preferred_element_type=F32)

    tiles_needed = ((qb + 1) * Q // NSA_CMP_STRIDE + LANE - 1) // LANE
    for tiles in range(1, nc // LANE + 1):
        pl.when(tiles_needed == tiles)(functools.partial(attend, tiles * LANE))

    imp = imp_ref[...]
    blk = _iota((LANE, Q), 0)
    q_blk = (qb * Q + _iota((LANE, Q), 1)) >> int(math.log2(NSA_SLC_LEN))
    causal = blk <= q_blk
    for forced_blk in (0, q_blk, q_blk - 1):
        imp = jnp.where(blk == forced_blk, FORCED_SCORE, imp)
    imp = jnp.where(causal, imp, -1.0)
    imp = jnp.where(blk < n_slc, imp, -2.0)
    blk_f = blk.astype(F32)
    sel = jnp.zeros((LANE, Q), F32)
    for _ in range(top_n):
        m = jnp.max(imp, axis=0, keepdims=True)
        first = jnp.min(jnp.where(imp == m, blk_f, float(LANE)), axis=0, keepdims=True)
        hit = blk_f == first
        sel = jnp.where(hit, 1.0, sel)
        imp = jnp.where(hit, -3.0, imp)
    sel = jnp.where(causal, sel, 0.0).T
    selb_ref[...] = jnp.where(sel > 0.5, 0.0, NEG_INF).astype(BF16)
    any_ref[...] = jnp.max(sel, axis=0, keepdims=True)


def _nsa_select(p_nsa, qtab, kc, vc, overlap_t):
    B, S, _ = p_nsa.shape
    Q = NSA_Q
    nqb = S // Q
    nc = kc.shape[1]
    n_slc = S // NSA_SLC_LEN
    kern = functools.partial(_nsa_sel_kernel, n_slc=n_slc, top_n=min(NSA_TOPN, n_slc))
    return pl.pallas_call(
        kern, name="nsa_select",
        grid=(B, nqb),
        in_specs=[pl.BlockSpec((None, Q, NSA_HEADS * LANE), lambda b, i: (b, i, 0)),
                  pl.BlockSpec((2 * NSA_HEADS, LANE), lambda b, i: (0, 0)),
                  pl.BlockSpec((None, Q, LANE), lambda b, i: (b, i, (W_NSA - LANE) // LANE)),
                  pl.BlockSpec((None, nc, LANE), lambda b, i: (b, 0, 0)),
                  pl.BlockSpec((None, nc, LANE), lambda b, i: (b, 0, 0)),
                  pl.BlockSpec((LANE, nc), lambda b, i: (0, 0))],
        out_specs=[pl.BlockSpec((None, Q, GROUP_W), lambda b, i: (b, i, 0)),
                   pl.BlockSpec((None, Q, LANE), lambda b, i: (b, i, 0)),
                   pl.BlockSpec((None, None, 1, LANE), lambda b, i: (b, i, 0, 0))],
        out_shape=[jax.ShapeDtypeStruct((B, S, GROUP_W), F32),
                   jax.ShapeDtypeStruct((B, S, LANE), BF16),
                   jax.ShapeDtypeStruct((B, nqb, 1, LANE), F32)],
        scratch_shapes=[pltpu.VMEM((LANE, Q), F32)],
        compiler_params=_cparams("parallel", "parallel"),
    )(p_nsa, qtab, p_nsa, kc, vc, overlap_t)


def _nsa_attn_kernel(flags_ref, q_ref, qtab_ref, gate_ref, oc_ref, selb_ref, ks_ref, vs_ref, kw_ref, vw_ref,
                     o_ref, m_ref, acc_ref, ow_ref, *, nt):
    Q, H, TK = q_ref.shape[0], NSA_HEADS, NSA_TILE
    PART = 2 * Q
    b = pl.program_id(0)
    qb = pl.program_id(1)
    nqb = pl.num_programs(1)
    qh = _nsa_queries(q_ref, qtab_ref, qb)
    nt_dims = (((1,), (1,)), ((), ()))

    selb = selb_ref[...]
    qs_sel = jnp.concatenate([jnp.concatenate([q, selb], axis=1) for q in qh], axis=0)
    sink_off = jnp.where(_iota(selb.shape, 1) < NSA_SINK // NSA_SLC_LEN, NEG_INF, selb.astype(F32)).astype(BF16)
    qs_loop = jnp.concatenate([jnp.concatenate([q, sink_off], axis=1) for q in qh], axis=0)
    m_ref[...] = jnp.full(m_ref.shape, NEG_INF, F32)
    acc_ref[...] = jnp.zeros_like(acc_ref)

    def update(t, diagonal):
        rows = pl.ds(pl.multiple_of(t * TK, TK), TK)
        k_tile, v_tile = ks_ref[rows, :], vs_ref[rows, :]
        parts = [slice(i * PART, (i + 1) * PART) for i in range(H * Q // PART)]
        qs_t = qs_sel if diagonal else qs_loop
        scores = [lax.dot_general(qs_t[r], k_tile, nt_dims, preferred_element_type=F32) for r in parts]
        if diagonal:
            k_sink, v_sink = ks_ref[0:NSA_SINK, :], vs_ref[0:NSA_SINK, :]
            sink_scores = [lax.dot_general(qs_sel[r], k_sink, nt_dims, preferred_element_type=F32) for r in parts]
            sink_bias = jnp.where(t > 0, 0.0, NEG_INF)
        for idx, (r, s) in enumerate(zip(parts, scores)):
            m_prev = m_ref[r]
            if diagonal:
                ahead = _iota((PART, TK), 1) - (_iota((PART, TK), 0) & (Q - 1))
                s = jnp.where(ahead <= qb * Q - t * TK, s, NEG_INF)
                s_sink = sink_scores[idx] + sink_bias
                m_prev = jnp.maximum(m_prev, jnp.max(s_sink, axis=-1, keepdims=True))
            m_new = jnp.maximum(m_prev, jnp.max(s, axis=-1, keepdims=True))
            p = jnp.exp2(s - jnp.tile(m_new, (1, TK // LANE)))
            update_acc = jnp.dot(p.astype(BF16), v_tile, preferred_element_type=F32)
            if diagonal:
                p_sink = jnp.exp2(s_sink - jnp.tile(m_new, (1, NSA_SINK // LANE)))
                update_acc = update_acc + jnp.dot(p_sink.astype(BF16), v_sink, preferred_element_type=F32)
            acc_ref[r] = jnp.exp2(m_ref[r] - m_new) * acc_ref[r] + update_acc
            m_ref[r] = m_new

    def tile(t, carry):
        @pl.when(flags_ref[(b * nqb + qb) * nt + t] > 0)
        def _():
            update(t, False)
        return carry

    t_diag = (qb * Q) // TK
    lax.fori_loop(0, t_diag, tile, 0)
    update(t_diag, True)

    assert Q == NSA_WIN
    half = Q // 2

    def band(q_rows, first_key, n_keys, masks):
        keys = pl.ds(pl.multiple_of(first_key, half), n_keys)
        s = lax.dot_general(q_rows, kw_ref[keys, :], nt_dims, preferred_element_type=F32)
        groups = [s[:, g * half:(g + 1) * half] for g in range(n_keys // half)]
        s = jnp.concatenate([g if m is None else jnp.where(m, g, NEG_INF) for g, m in zip(groups, masks)], axis=1)
        p = jnp.exp2(s - jnp.max(s, axis=-1, keepdims=True))
        return _normalise(jnp.dot(p.astype(BF16), vw_ref[keys, :], preferred_element_type=F32))

    @pl.when(qb == 0)
    def _():
        qs_win = jnp.concatenate(qh, axis=0)
        row = _iota((H * Q, half), 0) & (Q - 1)
        col = _iota((H * Q, half), 1)
        ow_ref[...] = band(qs_win, 0, Q, [col <= row, col + half <= row])

    @pl.when(qb > 0)
    def _():
        row = _iota((H * half, half), 0) & (half - 1)
        col = _iota((H * half, half), 1)
        masks = [col > row, None, col <= row]
        for part in range(2):
            q_rows = jnp.concatenate([q[part * half:(part + 1) * half] for q in qh], axis=0)
            o_part = band(q_rows, (qb - 1) * Q + part * half, NSA_WIN + half, masks)
            for h in range(H):
                ow_ref[h * Q + part * half:h * Q + (part + 1) * half, :] = o_part[h * half:(h + 1) * half]

    gates = jax.nn.sigmoid(gate_ref[...])
    mixed = _stacked_gate(gates, 1) * _normalise(acc_ref[...]) + _stacked_gate(gates, 2) * ow_ref[...]
    o_ref[...] = (oc_ref[...] + _unstack_heads(mixed)).astype(o_ref.dtype)


def _nsa_attend(p_nsa, qtab, o_c, selb, flags, kv):
    B, S, _ = p_nsa.shape
    Q = NSA_Q
    nqb = S // Q
    nt = S // NSA_TILE
    gate_blk = (W_NSA - LANE) // LANE
    kern = functools.partial(_nsa_attn_kernel, nt=nt)
    slab = lambda width, col: pl.BlockSpec((None, S, width), lambda b, i, f: (b, 0, col))
    grid_spec = pltpu.PrefetchScalarGridSpec(
        num_scalar_prefetch=1,
        grid=(B, nqb),
        in_specs=[pl.BlockSpec((None, Q, NSA_HEADS * LANE), lambda b, i, f: (b, i, 0)),
                  pl.BlockSpec((2 * NSA_HEADS, LANE), lambda b, i, f: (0, 0)),
                  pl.BlockSpec((None, Q, LANE), lambda b, i, f: (b, i, gate_blk)),
                  pl.BlockSpec((None, Q, GROUP_W), lambda b, i, f: (b, i, 0)),
                  pl.BlockSpec((None, Q, LANE), lambda b, i, f: (b, i, 0)),
                  slab(2 * LANE, 0), slab(LANE, 2), slab(LANE, 3), slab(LANE, 4)],
        out_specs=pl.BlockSpec((None, Q, GROUP_W), lambda b, i, f: (b, i, 0)),
        scratch_shapes=[pltpu.VMEM((NSA_HEADS * Q, LANE), F32)] * 3,
    )
    return pl.pallas_call(
        kern, name="nsa_attend",
        grid_spec=grid_spec,
        out_shape=jax.ShapeDtypeStruct((B, S, GROUP_W), BF16),
        compiler_params=_cparams("parallel", "parallel"),
    )(flags, p_nsa, qtab, p_nsa, o_c, selb, kv, kv, kv, kv)


def _nsa_tables(S):
    nc = S // NSA_CMP_STRIDE
    n = np.arange(nc)[None, :]
    j = np.arange(LANE)[:, None]
    start = n * NSA_CMP_STRIDE
    ov = (start < (j + 1) * NSA_SLC_LEN) & (start + NSA_CMP_LEN - 1 >= j * NSA_SLC_LEN)
    ov &= (n < (S - NSA_CMP_LEN) // NSA_CMP_STRIDE + 1) & (j < S // NSA_SLC_LEN)
    pos = np.arange(S)
    k_zero = np.zeros((S, NSA_DK), np.float32)
    block_onehot = (pos[:, None] // NSA_SLC_LEN == np.arange(LANE)[None, :]).astype(np.float32)
    v_lanes = np.zeros((S, LANE), np.float32)
    v_lanes[:, ONES_LANE] = 1.0
    kv_table = np.concatenate([k_zero, _nsa_pos_lanes(pos), block_onehot, v_lanes,
                               k_zero, _nsa_pos_lanes(pos), v_lanes], axis=1)
    assert kv_table.shape[1] == W_KV
    return _nsa_query_table(), jnp.asarray(ov.astype(np.float32)), jnp.asarray(kv_table, dtype=BF16)


def _nsa(p_nsa, kv, uk, uv, pe_k, w1_k, w2_k, pe_v, w1_v, w2_v, tables):
    B, S, _ = p_nsa.shape
    qtab, overlap_t, _ = tables
    kc, vc = _nsa_compress(uk, uv, pe_k, w1_k, w2_k, pe_v, w1_v, w2_v)
    o_c, selb, blk_any = _nsa_select(p_nsa, qtab, kc, vc, overlap_t)
    per_tile = NSA_TILE // NSA_SLC_LEN
    nt = S // NSA_TILE
    not_sink = (np.arange(nt * per_tile) >= NSA_SINK // NSA_SLC_LEN).astype(np.float32)
    blk_any = blk_any[:, :, 0, :nt * per_tile] * not_sink
    flags = blk_any.reshape(B, S // NSA_Q, nt, per_tile).max(axis=-1)
    flags = (flags > 0).astype(jnp.int32).reshape(-1)
    return _nsa_attend(p_nsa, qtab, o_c, selb, flags, kv)


def _out_proj_kernel(h_ref, ya_ref, yb_ref, yc_ref, yd_ref, w_ref, g_ref, b_ref, o_ref, wb_ref):
    @pl.when(pl.program_id(0) == 0)
    def _():
        wb_ref[...] = w_ref[...].astype(BF16)

    mix = None
    for idx, y_ref in enumerate((ya_ref, yb_ref, yc_ref, yd_ref)):
        part = _mm(y_ref[...], wb_ref[idx * GROUP_W:(idx + 1) * GROUP_W, :])
        mix = part if mix is None else mix + part
    o_ref[...] = _layer_norm(DEEPNORM_ALPHA * h_ref[...] + mix, g_ref[...], b_ref[...])


def _out_proj(h2, ys, w_out, layer, g, b, tm=1024):
    T, D = h2.shape
    row = lambda w: pl.BlockSpec((tm, w), lambda i: (i, 0))
    const = lambda shape: pl.BlockSpec(shape, lambda i: (0,) * len(shape))
    return pl.pallas_call(
        _out_proj_kernel, name="out_proj_ln",
        grid=(T // tm,),
        in_specs=[row(D), row(GROUP_W), row(GROUP_W), row(GROUP_W), row(GROUP_W),
                  pl.BlockSpec((None, D, D), lambda i: (layer, 0, 0)), const((1, D)), const((1, D))],
        out_specs=row(D),
        out_shape=jax.ShapeDtypeStruct((T, D), F32),
        scratch_shapes=[pltpu.VMEM((D, D), BF16)],
        compiler_params=_cparams("arbitrary"),
    )(h2, *ys, w_out, g.reshape(1, D), b.reshape(1, D))


def _mlp_kernel(h_ref, w1_ref, w2_ref, g_ref, b_ref, o_ref, acc_ref):
    f = pl.program_id(1)

    @pl.when(f == 0)
    def _():
        acc_ref[...] = jnp.zeros_like(acc_ref)

    a = jnp.maximum(_mm(h_ref[...], w1_ref[...]), 0.0)
    acc_ref[...] += _mm(a * a, w2_ref[...])

    @pl.when(f == pl.num_programs(1) - 1)
    def _():
        o_ref[...] = _layer_norm(DEEPNORM_ALPHA * h_ref[...] + acc_ref[...], g_ref[...], b_ref[...])


def _mlp(h2, w1, w2, layer, g, b, tm=1024, tf=1024):
    T, D = h2.shape
    F = w1.shape[2]
    return pl.pallas_call(
        _mlp_kernel, name="mlp_ln",
        grid=(T // tm, F // tf),
        in_specs=[pl.BlockSpec((tm, D), lambda i, f: (i, 0)),
                  pl.BlockSpec((None, D, tf), lambda i, f: (layer, 0, f)),
                  pl.BlockSpec((None, tf, D), lambda i, f: (layer, f, 0)),
                  pl.BlockSpec((1, D), lambda i, f: (0, 0)),
                  pl.BlockSpec((1, D), lambda i, f: (0, 0))],
        out_specs=pl.BlockSpec((tm, D), lambda i, f: (i, 0)),
        out_shape=jax.ShapeDtypeStruct((T, D), F32),
        scratch_shapes=[pltpu.VMEM((tm, D), F32)],
        compiler_params=_cparams("parallel", "arbitrary"),
    )(h2, w1, w2, g.reshape(1, D), b.reshape(1, D))


def kernel(x, ln_emb_g, ln_emb_b, w_in, conv_w, conv_b, dt_bias, a_log, d_skip, ssm_norm_g, q_norm_g, w_uq, kv_norm_g, w_ukv, cmp_pe_k, cmp_w1_k, cmp_w2_k, cmp_pe_v, cmp_w1_v, cmp_w2_v, w_out, ln1_g, ln1_b, w_mlp1, w_mlp2, ln2_g, ln2_b):
    B, S, D = x.shape
    assert D == D_MODEL and S % NSA_TILE == 0 and S // NSA_SLC_LEN <= LANE
    T = B * S
    ret_tables = _ret_tables(S)
    mla_tables = _mla_tables(S)
    nsa_tables = _nsa_tables(S)
    h = x.reshape(T, D)
    for l in range(w_in.shape[0]):
        if l == 0:
            h, *proj = _in_proj(h, _layout_w_in(w_in[l]), nsa_tables[2], entry_ln=(ln_emb_g, ln_emb_b))
        else:
            proj = _in_proj(h, _layout_w_in(w_in[l]), nsa_tables[2])
        p_ssm, p_mla, p_ret, p_nsa, nsa_kv, uk, uv = proj
        cmp_rows = (B, S // NSA_CMP_STRIDE, uk.shape[-1])
        y_a, y_c = _ssm_and_retention(p_ssm.reshape(B, S, W_SSM), conv_w[l], conv_b[l], dt_bias[l], a_log[l],
                                      d_skip[l], ssm_norm_g[l], p_ret.reshape(B, S, W_RET), ret_tables)
        wq, wk, wv = _layout_mla_weights(w_uq[l], w_ukv[l])
        q, k, v = _mla_prep(p_mla.reshape(B, S, W_MLA), q_norm_g[l], kv_norm_g[l], wq, wk, wv, mla_tables)
        y_b = _mla_attn(q, k, v)
        y_d = _nsa(p_nsa.reshape(B, S, W_NSA), nsa_kv.reshape(B, S, W_KV), uk.reshape(cmp_rows), uv.reshape(cmp_rows),
                   cmp_pe_k[l], cmp_w1_k[l], cmp_w2_k[l], cmp_pe_v[l], cmp_w1_v[l], cmp_w2_v[l], nsa_tables)
        ys = [y.reshape(T, GROUP_W) for y in (y_a, y_b, y_c, y_d)]
        h = _out_proj(h, ys, w_out, l, ln1_g[l], ln1_b[l])
        h = _mlp(h, w_mlp1, w_mlp2, l, ln2_g[l], ln2_b[l])
    return h.reshape(B, S, D)
```

```python
import functools
import math

import jax
import jax.numpy as jnp
import numpy as np
from jax import lax
from jax.experimental import pallas as pl
from jax.experimental.pallas import tpu as pltpu

F32 = jnp.float32
BF16 = jnp.bfloat16

D_MODEL = 1024
DEPTH = 2
GROUP_W = D_MODEL // 4
SSM_HEADS = 4
SSM_HEAD_DIM = GROUP_W // SSM_HEADS
SSM_GROUPS = 2
SSM_STATE = 128
SSM_CONV = 4
SSM_CHUNK = 128
SSM_XBC = GROUP_W + 2 * SSM_GROUPS * SSM_STATE
MLA_HEADS = 4
MLA_NOPE = 64
MLA_ROPE = 32
MLA_V = GROUP_W // MLA_HEADS
MLA_Q_RANK = 256
MLA_KV_RANK = 128
RET_HEADS = 4
RET_DK = 64
RET_DV = GROUP_W // RET_HEADS
RET_CHUNK = 128
NSA_HEADS = 4
NSA_DK = 64
NSA_DV = GROUP_W // NSA_HEADS
NSA_CMP_LEN = 32
NSA_CMP_STRIDE = 16
NSA_CMP_HID = 256
NSA_SLC_LEN = 64
NSA_TOPN = 16
NSA_WIN = 512
D_FF = 4 * D_MODEL
NSA_Q = 512
ROPE_THETA = 10000.0
EPS = 1e-5
NEG_INF = -1e30
LOG2_E = math.log2(math.e)
FORCED_SCORE = 1e9
DEEPNORM_ALPHA = (2.0 * DEPTH) ** 0.25

IN_SPLITS = (
    GROUP_W, SSM_XBC, SSM_HEADS,
    MLA_Q_RANK, MLA_KV_RANK, MLA_ROPE,
    RET_HEADS * RET_DK, RET_HEADS * RET_DK, RET_HEADS * RET_DV, GROUP_W,
    NSA_HEADS * NSA_DK, NSA_DK, NSA_DV, NSA_DK, NSA_DV, NSA_DK, NSA_DV, 3 * NSA_HEADS,
)

LANE = 128
W_SSM = GROUP_W + SSM_XBC + LANE
W_MLA = MLA_Q_RANK + MLA_KV_RANK + LANE
W_RET = 4 * GROUP_W
W_NSA = NSA_HEADS * LANE + LANE + LANE
W_KV = 2 * LANE + 3 * LANE
W_PROJ = W_SSM + W_MLA + W_RET + W_NSA + 4 * NSA_DK

NSA_TILE = 512
NSA_SINK = 128
MLA_TILE = 1024
VMEM_LIMIT = 48 * 1024 * 1024


def _cparams(*sem):
    return pltpu.CompilerParams(dimension_semantics=sem, vmem_limit_bytes=VMEM_LIMIT)


def _mm(a, b):
    return jnp.dot(a.astype(BF16), b.astype(BF16), preferred_element_type=F32)


def _mm_nt(a, b):
    return lax.dot_general(a.astype(BF16), b.astype(BF16), (((1,), (1,)), ((), ())),
                           preferred_element_type=F32)


def _split_f32(x):
    hi = x.astype(BF16)
    rest = x - hi.astype(F32)
    mid = rest.astype(BF16)
    lo = (rest - mid.astype(F32)).astype(BF16)
    return hi, mid, lo


def _mm_f32(a, b, exact, dims=(((1,), (0,)), ((), ()))):
    fixed, pieces = (a.astype(BF16), _split_f32(b)) if exact == "a" else (b.astype(BF16), _split_f32(a))
    out = None
    for piece in pieces:
        lhs, rhs = (fixed, piece) if exact == "a" else (piece, fixed)
        part = lax.dot_general(lhs, rhs, dims, preferred_element_type=F32)
        out = part if out is None else out + part
    return out


def _silu(x):
    return x * jax.nn.sigmoid(x)


def _softplus(x):
    return jnp.maximum(x, 0.0) + jnp.log1p(jnp.exp(-jnp.abs(x)))


def _layer_norm(x, g, b):
    mu = jnp.mean(x, axis=-1, keepdims=True)
    xc = x - mu
    var = jnp.mean(xc * xc, axis=-1, keepdims=True)
    return xc * lax.rsqrt(var + EPS) * g + b


def _iota(shape, dim):
    return lax.broadcasted_iota(jnp.int32, shape, dim)


def _pad_cols(w, width):
    return jnp.pad(w, ((0, 0), (0, width - w.shape[1])))


def _layout_w_in(w):
    offs = np.concatenate([[0], np.cumsum(IN_SPLITS)])
    p = [w[:, int(offs[i]):int(offs[i + 1])] for i in range(len(IN_SPLITS))]
    (ssm_z, ssm_xbc, ssm_dt, mla_cq, mla_ckv, mla_kr, ret_q, ret_k, ret_v, ret_g,
     nsa_q, nsa_kc, nsa_vc, nsa_ks, nsa_vs, nsa_kw, nsa_vw, nsa_gate) = p
    nsa_q_heads = [_pad_cols(nsa_q[:, h * NSA_DK:(h + 1) * NSA_DK], LANE) for h in range(NSA_HEADS)]
    cols = [ssm_z, ssm_xbc, _pad_cols(ssm_dt, LANE),
            mla_cq, mla_ckv, _pad_cols(mla_kr, LANE),
            ret_q, ret_k, ret_v, ret_g,
            *nsa_q_heads, nsa_kc, nsa_vc, _pad_cols(nsa_gate, LANE),
            nsa_ks, nsa_vs, nsa_kw, nsa_vw]
    out = jnp.concatenate(cols, axis=1)
    assert out.shape[1] == W_PROJ
    return out.astype(BF16)


def _in_proj_kernel(*refs, entry_ln):
    if entry_ln:
        h_ref, g_ref, b_ref, w_ref, kvtab_ref, hn_ref, *outs = refs
        hn = _layer_norm(h_ref[...], g_ref[...], b_ref[...])
        hn_ref[...] = hn
    else:
        h_ref, w_ref, kvtab_ref, *outs = refs
        hn = h_ref[...]
    ssm_ref, mla_ref, ret_ref, nsa_ref, kv_ref, uk_ref, uv_ref, kcv_ref = outs
    hb = hn.astype(BF16)
    off = 0
    for ref, width in ((ssm_ref, W_SSM), (mla_ref, W_MLA), (ret_ref, W_RET), (nsa_ref, W_NSA)):
        ref[...] = jnp.dot(hb, w_ref[:, off:off + width], preferred_element_type=F32)
        off += width
    groups = h_ref.shape[0] // NSA_CMP_STRIDE
    kc_lane = NSA_HEADS * LANE
    kcv_ref[...] = nsa_ref[:, kc_lane:kc_lane + LANE]
    for t in range(NSA_CMP_STRIDE):
        piece = kcv_ref[pl.ds(t, groups, stride=NSA_CMP_STRIDE), :]
        uk_ref[:, t * NSA_DK:(t + 1) * NSA_DK] = piece[:, :NSA_DK]
        uv_ref[:, t * NSA_DV:(t + 1) * NSA_DV] = piece[:, NSA_DK:]
    kv = jnp.dot(hb, w_ref[:, off:off + 4 * NSA_DK], preferred_element_type=F32)
    low = _iota((kv.shape[0], LANE), 1) < NSA_DK
    sel_kv, win_kv = kv[:, :LANE], kv[:, LANE:]
    pieces = {0: sel_kv, 2: pltpu.roll(sel_kv, NSA_DK, 1), 3: win_kv, 4: pltpu.roll(win_kv, NSA_DK, 1)}
    for slab in range(W_KV // LANE):
        lanes = slice(slab * LANE, (slab + 1) * LANE)
        tab = kvtab_ref[:, lanes]
        if slab in pieces:
            kv_ref[:, lanes] = (jnp.where(low, pieces[slab], 0.0) + tab.astype(F32)).astype(BF16)
        else:
            kv_ref[:, lanes] = tab


def _in_proj(h2, w_p, kv_table, entry_ln=None, tm=512):
    T, D = h2.shape
    S = kv_table.shape[0]
    widths = (W_SSM, W_MLA, W_RET, W_NSA)
    half = NSA_CMP_STRIDE * NSA_DK
    row = lambda w: pl.BlockSpec((tm, w), lambda i: (i, 0))
    const = lambda shape: pl.BlockSpec(shape, lambda i: (0,) * len(shape))
    in_specs = [const((D, W_PROJ)), pl.BlockSpec((tm, W_KV), lambda i: (i % (S // tm), 0))]
    out_specs = [row(w) for w in widths + (W_KV,)] + [pl.BlockSpec((tm // NSA_CMP_STRIDE, half), lambda i: (i, 0))] * 2
    out_shape = ([jax.ShapeDtypeStruct((T, w), F32) for w in widths] + [jax.ShapeDtypeStruct((T, W_KV), BF16)]
                 + [jax.ShapeDtypeStruct((T // NSA_CMP_STRIDE, half), F32)] * 2)
    operands = (w_p, kv_table)
    if entry_ln is not None:
        in_specs = [const((1, D)), const((1, D))] + in_specs
        out_specs = [row(D)] + out_specs
        out_shape = [jax.ShapeDtypeStruct((T, D), F32)] + out_shape
        operands = tuple(v.reshape(1, D) for v in entry_ln) + operands
    return pl.pallas_call(
        functools.partial(_in_proj_kernel, entry_ln=entry_ln is not None), name="in_proj",
        grid=(T // tm,),
        in_specs=[row(D)] + in_specs,
        out_specs=out_specs,
        out_shape=out_shape,
        scratch_shapes=[pltpu.VMEM((tm, LANE), F32)],
        compiler_params=_cparams("parallel"),
    )(h2, *operands)


def _ssm_chunk(p_ref, cw_ref, cb_ref, dtb_ref, alog_ref, dskip_ref, ng_ref, o_ref, state_ref, ext_ref):
    L, H, P, N = SSM_CHUNK, SSM_HEADS, SSM_HEAD_DIM, SSM_STATE
    z = p_ref[:, 0:GROUP_W]
    ext_ref[8:8 + L, :] = p_ref[:, GROUP_W:GROUP_W + SSM_XBC]
    conv = cb_ref[...]
    for j in range(SSM_CONV):
        conv = conv + ext_ref[pl.ds(8 - (SSM_CONV - 1) + j, L), :] * cw_ref[j:j + 1, :]
    ext_ref[0:8, :] = ext_ref[L:L + 8, :]
    xbc = _silu(conv)
    xs = xbc[:, 0:GROUP_W]
    b_in = xbc[:, GROUP_W:GROUP_W + SSM_GROUPS * N]
    c_in = xbc[:, GROUP_W + SSM_GROUPS * N:]

    dt = _softplus(p_ref[:, GROUP_W + SSM_XBC:] + dtb_ref[...])
    a = dt * (-jnp.exp(alog_ref[...]))
    row = _iota((L, L), 0)
    col = _iota((L, L), 1)
    tril = col <= row
    cs = _mm_f32(jnp.where(tril, 1.0, 0.0), a, exact="a")
    cs_t = cs.T
    ecs = jnp.exp(cs)
    dte = jnp.exp(cs[L - 1:L, :] - cs)
    expand = jnp.where(_iota((LANE, H * P), 0) == _iota((LANE, H * P), 1) // P, 1.0, 0.0)
    dt_x = _mm_f32(dt, expand, exact="b")
    ecs_x = _mm_f32(ecs, expand, exact="b")
    dte_x = _mm_f32(dte, expand, exact="b")

    xdt = xs * dt_x
    wx = xdt * dte_x
    head_of_lane = _iota((L, H * P), 1) // P
    y = xs * dskip_ref[...]
    y_off = []
    rep = H // SSM_GROUPS
    for g in range(SSM_GROUPS):
        cg = c_in[:, g * N:(g + 1) * N]
        bg = b_in[:, g * N:(g + 1) * N]
        cb = _mm_nt(cg, bg)
        for h in range(g * rep, (g + 1) * rep):
            diff = cs[:, h:h + 1] - cs_t[h:h + 1, :]
            seg = jnp.where(tril, jnp.exp(jnp.where(tril, diff, 0.0)), 0.0)
            yh = _mm(cb * seg, xdt)
            y = y + jnp.where(head_of_lane == h, yh, 0.0)
        lanes = slice(g * rep * P, (g + 1) * rep * P)
        st_prev = state_ref[:, lanes]
        y_off.append(_mm(cg, st_prev))
        state_ref[:, lanes] = st_prev * ecs_x[L - 1:L, lanes] + _mm(bg.T, wx[:, lanes])
    y = y + jnp.concatenate(y_off, axis=1) * ecs_x
    y = y * _silu(z)
    ms = jnp.mean(y * y, axis=-1, keepdims=True)
    o_ref[...] = (y * lax.rsqrt(ms + EPS) * ng_ref[...]).astype(o_ref.dtype)


def _recurrent_kernel(ps_ref, cw_ref, cb_ref, dtb_ref, alog_ref, dskip_ref, ng_ref,
                      pr_ref, cos_ref, sin_ref, dec_ref, zeta_ref, xi_ref, cd_ref,
                      oa_ref, oc_ref, sstate_ref, ext_ref, rstate_ref):
    @pl.when(pl.program_id(0) == 0)
    def _():
        sstate_ref[...] = jnp.zeros_like(sstate_ref)
        rstate_ref[...] = jnp.zeros_like(rstate_ref)
        ext_ref[:, 0:8, :] = jnp.zeros((ext_ref.shape[0], 8, SSM_XBC), F32)

    for b in range(ps_ref.shape[0]):
        _ssm_chunk(ps_ref.at[b], cw_ref, cb_ref, dtb_ref, alog_ref, dskip_ref, ng_ref,
                   oa_ref.at[b], sstate_ref.at[b], ext_ref.at[b])
        _ret_chunk(pr_ref.at[b], cos_ref, sin_ref, dec_ref, zeta_ref, xi_ref, cd_ref, oc_ref.at[b], rstate_ref.at[b])


def _ret_chunk(p_ref, cos_ref, sin_ref, dec_ref, zeta_ref, xi_ref, cd_ref, o_ref, state_ref):
    L, H, DK, DV = RET_CHUNK, RET_HEADS, RET_DK, RET_DV
    W = H * DK
    q = p_ref[:, 0:W]
    k = p_ref[:, W:2 * W]
    v = p_ref[:, 2 * W:3 * W]
    gate = p_ref[:, 3 * W:4 * W]
    lane = _iota((L, W), 1)
    first_half = (lane % DK) < (DK // 2)
    head_of_lane = lane // DK

    def rope(x):
        partner = jnp.where(first_half, pltpu.roll(x, W - DK // 2, 1), pltpu.roll(x, DK // 2, 1))
        return x * cos_ref[...] + partner * sin_ref[...]

    qr = rope(q)
    kr = rope(k) * (DK ** -0.5)
    y = jnp.zeros((L, H * DV), F32)
    for h in range(H):
        qh = jnp.where(head_of_lane == h, qr, 0.0)
        sc = _mm_nt(qh, kr) * dec_ref[h]
        y = y + jnp.where(head_of_lane == h, _mm(sc, v), 0.0)
    st = state_ref[...]
    y = y + _mm(qr * xi_ref[...], st)
    same_head = (_iota((W, H * DV), 0) // DK) == (_iota((W, H * DV), 1) // DV)
    kv = _mm((kr * zeta_ref[...]).T, v)
    state_ref[...] = st * cd_ref[...] + jnp.where(same_head, kv, 0.0)
    assert DV & (DV - 1) == 0
    ms = _mm_f32(y * y, jnp.where(same_head, 1.0 / DV, 0.0), exact="b")
    o_ref[...] = (y * lax.rsqrt(ms + EPS) * _silu(gate)).astype(o_ref.dtype)


def _ret_tables(S):
    H, DK, L = RET_HEADS, RET_DK, RET_CHUNK
    inv = ROPE_THETA ** (-np.arange(0, DK, 2, dtype=np.float64) / DK)
    ang = np.arange(S, dtype=np.float64)[:, None] * inv[None, :]
    cos, sin = np.cos(ang), np.sin(ang)
    cos_t = np.tile(np.concatenate([cos, cos], axis=1), (1, H))
    sin_t = np.tile(np.concatenate([-sin, sin], axis=1), (1, H))
    log_gamma = np.log1p(-np.exp2(-5.0 - np.arange(H, dtype=np.float64)))
    pos = np.arange(L, dtype=np.float64)
    diff = pos[:, None] - pos[None, :]
    decay_in = np.where(diff >= 0, np.exp(np.maximum(diff, 0.0)[None] * log_gamma[:, None, None]), 0.0)
    zeta = np.exp((L - 1 - pos)[None] * log_gamma[:, None])
    xi = np.exp((pos + 1.0)[None] * log_gamma[:, None])
    chunk_decay = np.exp(L * log_gamma)
    zeta_x = np.repeat(zeta.T, DK, axis=1)
    xi_x = np.repeat(xi.T, DK, axis=1)
    cd_x = np.repeat(chunk_decay, RET_DV).reshape(1, H * RET_DV)
    return tuple(jnp.asarray(t, dtype=F32) for t in (cos_t, sin_t, decay_in, zeta_x, xi_x, cd_x))


def _ssm_and_retention(p_ssm, conv_w, conv_b, dt_bias, a_log, d_skip, norm_g, p_ret, tables):
    B, S, _ = p_ssm.shape
    L, H = SSM_CHUNK, RET_HEADS
    assert RET_CHUNK == L
    W = H * RET_DK
    cos_t, sin_t, decay_in, zeta_x, xi_x, cd_x = tables
    pad_h = lambda v: jnp.pad(v, (0, LANE - SSM_HEADS)).reshape(1, LANE)
    const = lambda shape: pl.BlockSpec(shape, lambda c: (0,) * len(shape))
    chunk = lambda width: pl.BlockSpec((B, L, width), lambda c: (0, c, 0))
    out_shape = jax.ShapeDtypeStruct((B, S, GROUP_W), BF16)
    return pl.pallas_call(
        _recurrent_kernel, name="ssm_retention",
        grid=(S // L,),
        in_specs=[chunk(W_SSM),
                  const((SSM_CONV, SSM_XBC)), const((1, SSM_XBC)), const((1, LANE)), const((1, LANE)),
                  const((1, GROUP_W)), const((1, GROUP_W)),
                  chunk(W_RET),
                  pl.BlockSpec((L, W), lambda c: (c, 0)), pl.BlockSpec((L, W), lambda c: (c, 0)),
                  const((H, L, L)), const((L, W)), const((L, W)), const((1, H * RET_DV))],
        out_specs=[chunk(GROUP_W), chunk(GROUP_W)],
        out_shape=[out_shape, out_shape],
        scratch_shapes=[pltpu.VMEM((B, SSM_STATE, GROUP_W), F32),
                        pltpu.VMEM((B, L + 8, SSM_XBC), F32),
                        pltpu.VMEM((B, W, H * RET_DV), F32)],
        compiler_params=_cparams("arbitrary"),
    )(p_ssm, conv_w, conv_b.reshape(1, -1), pad_h(dt_bias), pad_h(a_log),
      jnp.repeat(d_skip, SSM_HEAD_DIM).reshape(1, GROUP_W), norm_g.reshape(1, GROUP_W),
      p_ret, cos_t, sin_t, decay_in, zeta_x, xi_x, cd_x)


def _mla_prep_kernel(p_ref, qg_ref, wq_ref, kvg_ref, wk_ref, wv_ref, cos_ref, sin_ref,
                     q_ref, k_ref, v_ref):
    tm = p_ref.shape[0]
    cq = p_ref[:, 0:MLA_Q_RANK]
    ckv = p_ref[:, MLA_Q_RANK:MLA_Q_RANK + MLA_KV_RANK]
    kr = p_ref[:, MLA_Q_RANK + MLA_KV_RANK:]

    def rms(x, g):
        return x * lax.rsqrt(jnp.mean(x * x, axis=-1, keepdims=True) + EPS) * g

    q = _mm(rms(cq, qg_ref[...]), wq_ref[...])
    kvl = rms(ckv, kvg_ref[...])
    kn = _mm(kvl, wk_ref[...])
    vv = _mm(kvl, wv_ref[...])
    kr_sh = pltpu.roll(kr, MLA_NOPE, 1)
    lane = _iota((tm, LANE), 1)
    half = MLA_ROPE // 2
    low = (lane >= MLA_NOPE) & (lane < MLA_NOPE + half)
    cos = cos_ref[...]
    sin = sin_ref[...]

    def rope(x):
        partner = jnp.where(low, pltpu.roll(x, LANE - half, 1), pltpu.roll(x, half, 1))
        return x * cos + partner * sin

    scale = (MLA_NOPE + MLA_ROPE) ** -0.5 * LOG2_E
    for h in range(MLA_HEADS):
        sl = slice(h * LANE, (h + 1) * LANE)
        q_ref[h] = (rope(q[:, sl]) * scale).astype(BF16)
        k_ref[h] = rope(kn[:, sl] + kr_sh).astype(BF16)
        v_ref[h] = jnp.where(lane == _mla_ones_lane(h), 1.0, vv[:, sl]).astype(BF16)


def _mla_tables(S):
    inv = ROPE_THETA ** (-np.arange(0, MLA_ROPE, 2, dtype=np.float64) / MLA_ROPE)
    ang = np.arange(S, dtype=np.float64)[:, None] * inv[None, :]
    cos, sin = np.cos(ang), np.sin(ang)
    tail = LANE - MLA_NOPE - MLA_ROPE
    cos_t = np.concatenate([np.ones((S, MLA_NOPE)), cos, cos, np.ones((S, tail))], axis=1)
    sin_t = np.concatenate([np.zeros((S, MLA_NOPE)), -sin, sin, np.zeros((S, tail))], axis=1)
    return jnp.asarray(cos_t, dtype=F32), jnp.asarray(sin_t, dtype=F32)


def _layout_mla_weights(w_uq, w_ukv):
    H = MLA_HEADS
    dq = MLA_NOPE + MLA_ROPE
    wq = jnp.concatenate([_pad_cols(w_uq[:, h * dq:(h + 1) * dq], LANE) for h in range(H)], axis=1)
    dkv = MLA_NOPE + MLA_V
    wk, wv = [], []
    for h in range(H):
        blk = w_ukv[:, h * dkv:(h + 1) * dkv]
        wk.append(_pad_cols(blk[:, :MLA_NOPE], LANE))
        v = blk[:, MLA_NOPE:]
        zero = jnp.zeros_like(v)
        wv.append(jnp.concatenate([v, zero] if h % 2 == 0 else [zero, v], axis=1))
    return wq.astype(BF16), jnp.concatenate(wk, axis=1).astype(BF16), jnp.concatenate(wv, axis=1).astype(BF16)


def _mla_prep(p_mla, q_norm_g, kv_norm_g, wq, wk, wv, tables, tm=1024):
    B, S, _ = p_mla.shape
    H = MLA_HEADS
    cos_t, sin_t = tables
    const = lambda shape: pl.BlockSpec(shape, lambda b, i: (0,) * len(shape))
    qkv_spec = pl.BlockSpec((None, H, tm, LANE), lambda b, i: (b, 0, i, 0))
    qkv_shape = jax.ShapeDtypeStruct((B, H, S, LANE), BF16)
    return pl.pallas_call(
        _mla_prep_kernel, name="mla_prep",
        grid=(B, S // tm),
        in_specs=[pl.BlockSpec((None, tm, W_MLA), lambda b, i: (b, i, 0)),
                  const((1, MLA_Q_RANK)), const((MLA_Q_RANK, H * LANE)),
                  const((1, MLA_KV_RANK)), const((MLA_KV_RANK, H * LANE)), const((MLA_KV_RANK, H * LANE)),
                  pl.BlockSpec((tm, LANE), lambda b, i: (i, 0)),
                  pl.BlockSpec((tm, LANE), lambda b, i: (i, 0))],
        out_specs=[qkv_spec, qkv_spec, qkv_spec],
        out_shape=[qkv_shape, qkv_shape, qkv_shape],
        compiler_params=_cparams("parallel", "parallel"),
    )(p_mla, q_norm_g.reshape(1, -1), wq, kv_norm_g.reshape(1, -1), wk, wv, cos_t, sin_t)


def _mla_ones_lane(h):
    return MLA_V if h % 2 == 0 else 0


def _mla_attn_kernel(qi_ref, kj_ref, q_ref, k_ref, v_ref, o_ref, m_ref, acc_ref):
    H = MLA_HEADS
    tq, tk = q_ref.shape[1], k_ref.shape[1]
    i = qi_ref[pl.program_id(1)]
    j = kj_ref[pl.program_id(1)]

    @pl.when(j == 0)
    def _():
        m_ref[...] = jnp.full(m_ref.shape, NEG_INF, F32)
        acc_ref[...] = jnp.zeros_like(acc_ref)

    def sweep(blocks):
        nt_dims = (((1,), (1,)), ((), ()))
        scores = [[lax.dot_general(q_ref[h, r0:r0 + nr, :], k_ref[h, 0:nk, :], nt_dims, preferred_element_type=F32)
                   for (r0, nr, nk, _) in blocks] for h in range(H)]
        for h in range(H):
            for (r0, nr, nk, offset), s in zip(blocks, scores[h]):
                if offset is not None:
                    s = jnp.where(_iota((nr, nk), 1) - _iota((nr, nk), 0) <= offset, s, NEG_INF)
                rows = slice(r0, r0 + nr)
                m_prev = m_ref[h, rows]
                m_new = jnp.maximum(m_prev, jnp.max(s, axis=-1, keepdims=True))
                p = jnp.exp2(s - jnp.tile(m_new, (1, nk // LANE)))
                acc_ref[h, rows] = (jnp.exp2(m_prev - m_new) * acc_ref[h, rows]
                                    + jnp.dot(p.astype(BF16), v_ref[h, 0:nk, :], preferred_element_type=F32))
                m_ref[h, rows] = m_new

    assert tq == tk
    half = tq // 2

    @pl.when(j < i)
    def _():
        sweep([(0, tq, tk, None)])

    @pl.when(j == i)
    def _():
        sweep([(0, half, half, 0), (half, half, tk, half)])
        lane = _iota((tq, LANE), 1)
        for pair in range(H // 2):
            he, ho = 2 * pair, 2 * pair + 1
            acc_e, acc_o = acc_ref[he], acc_ref[ho]
            le = acc_e[:, _mla_ones_lane(he):_mla_ones_lane(he) + 1]
            lo = acc_o[:, _mla_ones_lane(ho):_mla_ones_lane(ho) + 1]
            o_ref[:, pair * LANE:(pair + 1) * LANE] = jnp.where(lane < MLA_V, acc_e / le, acc_o / lo).astype(o_ref.dtype)


def _mla_attn(q, k, v):
    B, H, S, _ = q.shape
    t = min(MLA_TILE, S)
    tq = t
    pairs = [(i, j) for i in range(S // tq) for j in range((i + 1) * tq // t)]
    qi = jnp.asarray([p[0] for p in pairs], jnp.int32)
    kj = jnp.asarray([p[1] for p in pairs], jnp.int32)
    grid_spec = pltpu.PrefetchScalarGridSpec(
        num_scalar_prefetch=2,
        grid=(B, len(pairs)),
        in_specs=[pl.BlockSpec((None, H, tq, LANE), lambda b, p, qi, kj: (b, 0, qi[p], 0)),
                  pl.BlockSpec((None, H, t, LANE), lambda b, p, qi, kj: (b, 0, kj[p], 0)),
                  pl.BlockSpec((None, H, t, LANE), lambda b, p, qi, kj: (b, 0, kj[p], 0))],
        out_specs=pl.BlockSpec((None, tq, GROUP_W), lambda b, p, qi, kj: (b, qi[p], 0)),
        scratch_shapes=[pltpu.VMEM((H, tq, LANE), F32), pltpu.VMEM((H, tq, LANE), F32)],
    )
    return pl.pallas_call(
        _mla_attn_kernel, name="mla_attn",
        grid_spec=grid_spec,
        out_shape=jax.ShapeDtypeStruct((B, S, GROUP_W), BF16),
        compiler_params=_cparams("parallel", "arbitrary"),
    )(qi, kj, q, k, v)


POS_HI = NSA_DK
POS_LO = NSA_DK + 3
POS_ONE = NSA_DK + 6
ONES_LANE = NSA_DV


def _split_bf16(x, parts=3):
    out, rem = [], np.float64(x)
    for _ in range(parts):
        piece = np.float64(np.float32(rem).astype(jnp.bfloat16).astype(np.float32))
        out.append(float(piece))
        rem = rem - piece
    return out


def _nsa_query_table():
    H = NSA_HEADS
    tab = np.zeros((2 * H, LANE), np.float32)
    for h in range(H):
        c = 2.0 ** (-8.0 * (h + 1) / H) * LOG2_E
        pieces = _split_bf16(c)
        tab[h, POS_HI:POS_HI + 3] = pieces
        tab[h, POS_LO:POS_LO + 3] = pieces
        tab[H + h, POS_ONE] = -sum(pieces)
    return jnp.asarray(tab)


def _nsa_pos_lanes(pos, lo_offset=0.0):
    t = np.zeros((len(pos), LANE - NSA_DK), np.float32)
    t[:, POS_HI - NSA_DK:POS_HI - NSA_DK + 3] = (NSA_SLC_LEN * (pos // NSA_SLC_LEN))[:, None]
    t[:, POS_LO - NSA_DK:POS_LO - NSA_DK + 3] = (pos % NSA_SLC_LEN + lo_offset)[:, None]
    t[:, POS_ONE - NSA_DK] = 1.0
    return t


def _nsa_queries(q_ref, qtab_ref, qb):
    Q, H = q_ref.shape[0], NSA_HEADS
    qpos = (qb * Q + _iota((Q, 1), 0)).astype(F32)
    out = []
    for h in range(H):
        q = q_ref[:, h * LANE:(h + 1) * LANE] * (NSA_DK ** -0.5 * LOG2_E)
        out.append((q + qtab_ref[h:h + 1, :] + qtab_ref[H + h:H + h + 1, :] * qpos).astype(BF16))
    return out


def _normalise(o):
    return o / o[:, ONES_LANE:ONES_LANE + 1]


def _stacked_gate(gates, branch):
    lanes = [3 * h + branch for h in range(NSA_HEADS)]
    return jnp.concatenate([gates[:, c:c + 1] for c in lanes], axis=0)


def _unstack_heads(o):
    Q = o.shape[0] // NSA_HEADS
    lane = _iota((Q, LANE), 1)
    out = []
    for pair in range(NSA_HEADS // 2):
        even = o[(2 * pair) * Q:(2 * pair + 1) * Q]
        odd = o[(2 * pair + 1) * Q:(2 * pair + 2) * Q]
        out.append(jnp.where(lane < NSA_DV, even, pltpu.roll(odd, NSA_DV, 1)))
    return jnp.concatenate(out, axis=1)


def _nsa_cmp_kernel(uk_ref, uv_ref, pek_ref, pev_ref, w1k_ref, w1v_ref, w2k_ref, w2v_ref, cpos_ref,
                    kc_ref, vc_ref, sh_ref):
    nb = uk_ref.shape[0]
    half = uk_ref.shape[1]

    def hidden(u_ref, pe_ref, w1_ref):
        u = u_ref[...]
        first = _mm(u + pe_ref[0:1, :], w1_ref[0:half, :])
        second = _mm(u + pe_ref[1:2, :], w1_ref[half:2 * half, :])
        sh_ref[0:nb, :] = second
        sh_ref[nb:nb + 8, :] = jnp.zeros((8, NSA_CMP_HID), F32)
        return first + sh_ref[pl.ds(1, nb), :]

    hk = _silu(hidden(uk_ref, pek_ref, w1k_ref))
    hv = _silu(hidden(uv_ref, pev_ref, w1v_ref))
    kc_ref[...] = (_mm(hk, w2k_ref[...]) + cpos_ref[...]).astype(BF16)
    ones_lane = jnp.where(_iota((1, LANE), 1) == ONES_LANE, 1.0, 0.0)
    vc_ref[...] = (_mm(hv, w2v_ref[...]) + ones_lane).astype(BF16)


def _nsa_compress(uk, uv, pe_k, w1_k, w2_k, pe_v, w1_v, w2_v):
    B, nb, half = uk.shape
    hid = NSA_CMP_HID
    const = lambda shape: pl.BlockSpec(shape, lambda b: (0,) * len(shape))
    w2k = _pad_cols(w2_k, LANE).astype(BF16)
    w2v = _pad_cols(w2_v, LANE).astype(BF16)
    centre = _nsa_pos_lanes(np.arange(nb) * NSA_CMP_STRIDE, 0.5 * (NSA_CMP_LEN - 1))
    cpos = jnp.asarray(np.concatenate([np.zeros((nb, NSA_DK), np.float32), centre], axis=1))
    out_spec = pl.BlockSpec((None, nb, LANE), lambda b: (b, 0, 0))
    out_shape = jax.ShapeDtypeStruct((B, nb, LANE), BF16)
    return pl.pallas_call(
        _nsa_cmp_kernel, name="nsa_compress",
        grid=(B,),
        in_specs=[pl.BlockSpec((None, nb, half), lambda b: (b, 0, 0)),
                  pl.BlockSpec((None, nb, half), lambda b: (b, 0, 0)),
                  const((2, half)), const((2, half)),
                  const((2 * half, hid)), const((2 * half, hid)),
                  const((hid, LANE)), const((hid, LANE)), const((nb, LANE))],
        out_specs=[out_spec, out_spec],
        out_shape=[out_shape, out_shape],
        scratch_shapes=[pltpu.VMEM((nb + 8, hid), F32)],
        compiler_params=_cparams("parallel"),
    )(uk, uv, pe_k.reshape(2, half), pe_v.reshape(2, half), w1_k.astype(BF16), w1_v.astype(BF16), w2k, w2v, cpos)


def _nsa_sel_kernel(q_ref, qtab_ref, gate_ref, kc_ref, vc_ref, ovt_ref, oc_ref, selb_ref, any_ref, imp_ref, *, n_slc, top_n):
    Q, H = q_ref.shape[0], NSA_HEADS
    qb = pl.program_id(1)
    nc = kc_ref.shape[0]
    qs = jnp.concatenate(_nsa_queries(q_ref, qtab_ref, qb), axis=0)
    nt_dims = (((1,), (1,)), ((), ()))

    def attend(ncols):
        s = lax.dot_general(qs, kc_ref[0:ncols, :], nt_dims, preferred_element_type=F32)
        qpos = qb * Q + (_iota((H * Q, ncols), 0) & (Q - 1))
        block_end = _iota((H * Q, ncols), 1) * NSA_CMP_STRIDE + (NSA_CMP_LEN - 1)
        s = jnp.where(block_end <= qpos, s, NEG_INF)
        e = jnp.exp2(s - jnp.max(s, axis=-1, keepdims=True))
        qpos_col = qb * Q + (_iota((H * Q, 1), 0) & (Q - 1))
        has_block = jnp.where(qpos_col >= NSA_CMP_LEN - 1, 1.0, 0.0)
        p = e * (has_block / jnp.sum(e, axis=-1, keepdims=True))
        o_c = jnp.dot(p.astype(BF16), vc_ref[0:ncols, :], preferred_element_type=F32)
        oc_ref[...] = _unstack_heads(_stacked_gate(jax.nn.sigmoid(gate_ref[...]), 0) * o_c)
        p_sum = p[0:Q]
        for h in range(1, H):
            p_sum = p_sum + p[h * Q:(h + 1) * Q]
        imp_ref[...] = _mm_f32(ovt_ref[:, 0:ncols], p_sum, exact="a", dims=nt_dims)

    tiles_needed = ((qb + 1) * Q // NSA_CMP_STRIDE + LANE - 1) // LANE
    for tiles in range(1, nc // LANE + 1):
        pl.when(tiles_needed == tiles)(functools.partial(attend, tiles * LANE))

    imp = imp_ref[...]
    blk = _iota((LANE, Q), 0)
    q_blk = (qb * Q + _iota((LANE, Q), 1)) >> int(math.log2(NSA_SLC_LEN))
    causal = blk <= q_blk
    for forced_blk in (0, q_blk, q_blk - 1):
        imp = jnp.where(blk == forced_blk, FORCED_SCORE, imp)
    imp = jnp.where(causal, imp, -1.0)
    imp = jnp.where(blk < n_slc, imp, -2.0)
    blk_f = blk.astype(F32)
    sel = jnp.zeros((LANE, Q), F32)
    for _ in range(top_n):
        m = jnp.max(imp, axis=0, keepdims=True)
        first = jnp.min(jnp.where(imp == m, blk_f, float(LANE)), axis=0, keepdims=True)
        hit = blk_f == first
        sel = jnp.where(hit, 1.0, sel)
        imp = jnp.where(hit, -3.0, imp)
    sel = jnp.where(causal, sel, 0.0).T
    selb_ref[...] = jnp.where(sel > 0.5, 0.0, NEG_INF).astype(BF16)
    any_ref[...] = jnp.max(sel, axis=0, keepdims=True)


def _nsa_select(p_nsa, qtab, kc, vc, overlap_t):
    B, S, _ = p_nsa.shape
    Q = NSA_Q
    nqb = S // Q
    nc = kc.shape[1]
    n_slc = S // NSA_SLC_LEN
    kern = functools.partial(_nsa_sel_kernel, n_slc=n_slc, top_n=min(NSA_TOPN, n_slc))
    return pl.pallas_call(
        kern, name="nsa_select",
        grid=(B, nqb),
        in_specs=[pl.BlockSpec((None, Q, NSA_HEADS * LANE), lambda b, i: (b, i, 0)),
                  pl.BlockSpec((2 * NSA_HEADS, LANE), lambda b, i: (0, 0)),
                  pl.BlockSpec((None, Q, LANE), lambda b, i: (b, i, (W_NSA - LANE) // LANE)),
                  pl.BlockSpec((None, nc, LANE), lambda b, i: (b, 0, 0)),
                  pl.BlockSpec((None, nc, LANE), lambda b, i: (b, 0, 0)),
                  pl.BlockSpec((LANE, nc), lambda b, i: (0, 0))],
        out_specs=[pl.BlockSpec((None, Q, GROUP_W), lambda b, i: (b, i, 0)),
                   pl.BlockSpec((None, Q, LANE), lambda b, i: (b, i, 0)),
                   pl.BlockSpec((None, None, 1, LANE), lambda b, i: (b, i, 0, 0))],
        out_shape=[jax.ShapeDtypeStruct((B, S, GROUP_W), F32),
                   jax.ShapeDtypeStruct((B, S, LANE), BF16),
                   jax.ShapeDtypeStruct((B, nqb, 1, LANE), F32)],
        scratch_shapes=[pltpu.VMEM((LANE, Q), F32)],
        compiler_params=_cparams("parallel", "parallel"),
    )(p_nsa, qtab, p_nsa, kc, vc, overlap_t)


def _nsa_attn_kernel(flags_ref, q_ref, qtab_ref, gate_ref, oc_ref, selb_ref, ks_ref, vs_ref, kw_ref, vw_ref,
                     o_ref, m_ref, acc_ref, ow_ref, *, nt):
    Q, H, TK = q_ref.shape[0], NSA_HEADS, NSA_TILE
    PART = 2 * Q
    b = pl.program_id(0)
    qb = pl.program_id(1)
    nqb = pl.num_programs(1)
    qh = _nsa_queries(q_ref, qtab_ref, qb)
    nt_dims = (((1,), (1,)), ((), ()))

    selb = selb_ref[...]
    qs_sel = jnp.concatenate([jnp.concatenate([q, selb], axis=1) for q in qh], axis=0)
    sink_off = jnp.where(_iota(selb.shape, 1) < NSA_SINK // NSA_SLC_LEN, NEG_INF, selb.astype(F32)).astype(BF16)
    qs_loop = jnp.concatenate([jnp.concatenate([q, sink_off], axis=1) for q in qh], axis=0)
    m_ref[...] = jnp.full(m_ref.shape, NEG_INF, F32)
    acc_ref[...] = jnp.zeros_like(acc_ref)

    def update(t, diagonal):
        rows = pl.ds(pl.multiple_of(t * TK, TK), TK)
        k_tile, v_tile = ks_ref[rows, :], vs_ref[rows, :]
        parts = [slice(i * PART, (i + 1) * PART) for i in range(H * Q // PART)]
        qs_t = qs_sel if diagonal else qs_loop
        scores = [lax.dot_general(qs_t[r], k_tile, nt_dims, preferred_element_type=F32) for r in parts]
        if diagonal:
            k_sink, v_sink = ks_ref[0:NSA_SINK, :], vs_ref[0:NSA_SINK, :]
            sink_scores = [lax.dot_general(qs_sel[r], k_sink, nt_dims, preferred_element_type=F32) for r in parts]
            sink_bias = jnp.where(t > 0, 0.0, NEG_INF)
        for idx, (r, s) in enumerate(zip(parts, scores)):
            m_prev = m_ref[r]
            if diagonal:
                ahead = _iota((PART, TK), 1) - (_iota((PART, TK), 0) & (Q - 1))
                s = jnp.where(ahead <= qb * Q - t * TK, s, NEG_INF)
                s_sink = sink_scores[idx] + sink_bias
                m_prev = jnp.maximum(m_prev, jnp.max(s_sink, axis=-1, keepdims=True))
            m_new = jnp.maximum(m_prev, jnp.max(s, axis=-1, keepdims=True))
            p = jnp.exp2(s - jnp.tile(m_new, (1, TK // LANE)))
            update_acc = jnp.dot(p.astype(BF16), v_tile, preferred_element_type=F32)
            if diagonal:
                p_sink = jnp.exp2(s_sink - jnp.tile(m_new, (1, NSA_SINK // LANE)))
                update_acc = update_acc + jnp.dot(p_sink.astype(BF16), v_sink, preferred_element_type=F32)
            acc_ref[r] = jnp.exp2(m_ref[r] - m_new) * acc_ref[r] + update_acc
            m_ref[r] = m_new

    def tile(t, carry):
        @pl.when(flags_ref[(b * nqb + qb) * nt + t] > 0)
        def _():
            update(t, False)
        return carry

    t_diag = (qb * Q) // TK
    lax.fori_loop(0, t_diag, tile, 0)
    update(t_diag, True)

    assert Q == NSA_WIN
    half = Q // 2

    def band(q_rows, first_key, n_keys, masks):
        keys = pl.ds(pl.multiple_of(first_key, half), n_keys)
        s = lax.dot_general(q_rows, kw_ref[keys, :], nt_dims, preferred_element_type=F32)
        groups = [s[:, g * half:(g + 1) * half] for g in range(n_keys // half)]
        s = jnp.concatenate([g if m is None else jnp.where(m, g, NEG_INF) for g, m in zip(groups, masks)], axis=1)
        p = jnp.exp2(s - jnp.max(s, axis=-1, keepdims=True))
        return _normalise(jnp.dot(p.astype(BF16), vw_ref[keys, :], preferred_element_type=F32))

    @pl.when(qb == 0)
    def _():
        qs_win = jnp.concatenate(qh, axis=0)
        row = _iota((H * Q, half), 0) & (Q - 1)
        col = _iota((H * Q, half), 1)
        ow_ref[...] = band(qs_win, 0, Q, [col <= row, col + half <= row])

    @pl.when(qb > 0)
    def _():
        row = _iota((H * half, half), 0) & (half - 1)
        col = _iota((H * half, half), 1)
        masks = [col > row, None, col <= row]
        for part in range(2):
            q_rows = jnp.concatenate([q[part * half:(part + 1) * half] for q in qh], axis=0)
            o_part = band(q_rows, (qb - 1) * Q + part * half, NSA_WIN + half, masks)
            for h in range(H):
                ow_ref[h * Q + part * half:h * Q + (part + 1) * half, :] = o_part[h * half:(h + 1) * half]

    gates = jax.nn.sigmoid(gate_ref[...])
    mixed = _stacked_gate(gates, 1) * _normalise(acc_ref[...]) + _stacked_gate(gates, 2) * ow_ref[...]
    o_ref[...] = (oc_ref[...] + _unstack_heads(mixed)).astype(o_ref.dtype)


def _nsa_attend(p_nsa, qtab, o_c, selb, flags, kv):
    B, S, _ = p_nsa.shape
    Q = NSA_Q
    nqb = S // Q
    nt = S // NSA_TILE
    gate_blk = (W_NSA - LANE) // LANE
    kern = functools.partial(_nsa_attn_kernel, nt=nt)
    slab = lambda width, col: pl.BlockSpec((None, S, width), lambda b, i, f: (b, 0, col))
    grid_spec = pltpu.PrefetchScalarGridSpec(
        num_scalar_prefetch=1,
        grid=(B, nqb),
        in_specs=[pl.BlockSpec((None, Q, NSA_HEADS * LANE), lambda b, i, f: (b, i, 0)),
                  pl.BlockSpec((2 * NSA_HEADS, LANE), lambda b, i, f: (0, 0)),
                  pl.BlockSpec((None, Q, LANE), lambda b, i, f: (b, i, gate_blk)),
                  pl.BlockSpec((None, Q, GROUP_W), lambda b, i, f: (b, i, 0)),
                  pl.BlockSpec((None, Q, LANE), lambda b, i, f: (b, i, 0)),
                  slab(2 * LANE, 0), slab(LANE, 2), slab(LANE, 3), slab(LANE, 4)],
        out_specs=pl.BlockSpec((None, Q, GROUP_W), lambda b, i, f: (b, i, 0)),
        scratch_shapes=[pltpu.VMEM((NSA_HEADS * Q, LANE), F32)] * 3,
    )
    return pl.pallas_call(
        kern, name="nsa_attend",
        grid_spec=grid_spec,
        out_shape=jax.ShapeDtypeStruct((B, S, GROUP_W), BF16),
        compiler_params=_cparams("parallel", "parallel"),
    )(flags, p_nsa, qtab, p_nsa, o_c, selb, kv, kv, kv, kv)


def _nsa_tables(S):
    nc = S // NSA_CMP_STRIDE
    n = np.arange(nc)[None, :]
    j = np.arange(LANE)[:, None]
    start = n * NSA_CMP_STRIDE
    ov = (start < (j + 1) * NSA_SLC_LEN) & (start + NSA_CMP_LEN - 1 >= j * NSA_SLC_LEN)
    ov &= (n < (S - NSA_CMP_LEN) // NSA_CMP_STRIDE + 1) & (j < S // NSA_SLC_LEN)
    pos = np.arange(S)
    k_zero = np.zeros((S, NSA_DK), np.float32)
    block_onehot = (pos[:, None] // NSA_SLC_LEN == np.arange(LANE)[None, :]).astype(np.float32)
    v_lanes = np.zeros((S, LANE), np.float32)
    v_lanes[:, ONES_LANE] = 1.0
    kv_table = np.concatenate([k_zero, _nsa_pos_lanes(pos), block_onehot, v_lanes,
                               k_zero, _nsa_pos_lanes(pos), v_lanes], axis=1)
    assert kv_table.shape[1] == W_KV
    return _nsa_query_table(), jnp.asarray(ov.astype(np.float32)), jnp.asarray(kv_table, dtype=BF16)


def _nsa(p_nsa, kv, uk, uv, pe_k, w1_k, w2_k, pe_v, w1_v, w2_v, tables):
    B, S, _ = p_nsa.shape
    qtab, overlap_t, _ = tables
    kc, vc = _nsa_compress(uk, uv, pe_k, w1_k, w2_k, pe_v, w1_v, w2_v)
    o_c, selb, blk_any = _nsa_select(p_nsa, qtab, kc, vc, overlap_t)
    per_tile = NSA_TILE // NSA_SLC_LEN
    nt = S // NSA_TILE
    not_sink = (np.arange(nt * per_tile) >= NSA_SINK // NSA_SLC_LEN).astype(np.float32)
    blk_any = blk_any[:, :, 0, :nt * per_tile] * not_sink
    flags = blk_any.reshape(B, S // NSA_Q, nt, per_tile).max(axis=-1)
    flags = (flags > 0).astype(jnp.int32).reshape(-1)
    return _nsa_attend(p_nsa, qtab, o_c, selb, flags, kv)


def _out_proj_kernel(h_ref, ya_ref, yb_ref, yc_ref, yd_ref, w_ref, g_ref, b_ref, o_ref, wb_ref):
    @pl.when(pl.program_id(0) == 0)
    def _():
        wb_ref[...] = w_ref[...].astype(BF16)

    mix = None
    for idx, y_ref in enumerate((ya_ref, yb_ref, yc_ref, yd_ref)):
        part = _mm(y_ref[...], wb_ref[idx * GROUP_W:(idx + 1) * GROUP_W, :])
        mix = part if mix is None else mix + part
    o_ref[...] = _layer_norm(DEEPNORM_ALPHA * h_ref[...] + mix, g_ref[...], b_ref[...])


def _out_proj(h2, ys, w_out, layer, g, b, tm=1024):
    T, D = h2.shape
    row = lambda w: pl.BlockSpec((tm, w), lambda i: (i, 0))
    const = lambda shape: pl.BlockSpec(shape, lambda i: (0,) * len(shape))
    return pl.pallas_call(
        _out_proj_kernel, name="out_proj_ln",
        grid=(T // tm,),
        in_specs=[row(D), row(GROUP_W), row(GROUP_W), row(GROUP_W), row(GROUP_W),
                  pl.BlockSpec((None, D, D), lambda i: (layer, 0, 0)), const((1, D)), const((1, D))],
        out_specs=row(D),
        out_shape=jax.ShapeDtypeStruct((T, D), F32),
        scratch_shapes=[pltpu.VMEM((D, D), BF16)],
        compiler_params=_cparams("arbitrary"),
    )(h2, *ys, w_out, g.reshape(1, D), b.reshape(1, D))


def _mlp_kernel(h_ref, w1_ref, w2_ref, g_ref, b_ref, o_ref, acc_ref):
    f = pl.program_id(1)

    @pl.when(f == 0)
    def _():
        acc_ref[...] = jnp.zeros_like(acc_ref)

    a = jnp.maximum(_mm(h_ref[...], w1_ref[...]), 0.0)
    acc_ref[...] += _mm(a * a, w2_ref[...])

    @pl.when(f == pl.num_programs(1) - 1)
    def _():
        o_ref[...] = _layer_norm(DEEPNORM_ALPHA * h_ref[...] + acc_ref[...], g_ref[...], b_ref[...])


def _mlp(h2, w1, w2, layer, g, b, tm=1024, tf=1024):
    T, D = h2.shape
    F = w1.shape[2]
    return pl.pallas_call(
        _mlp_kernel, name="mlp_ln",
        grid=(T // tm, F // tf),
        in_specs=[pl.BlockSpec((tm, D), lambda i, f: (i, 0)),
                  pl.BlockSpec((None, D, tf), lambda i, f: (layer, 0, f)),
                  pl.BlockSpec((None, tf, D), lambda i, f: (layer, f, 0)),
                  pl.BlockSpec((1, D), lambda i, f: (0, 0)),
                  pl.BlockSpec((1, D), lambda i, f: (0, 0))],
        out_specs=pl.BlockSpec((tm, D), lambda i, f: (i, 0)),
        out_shape=jax.ShapeDtypeStruct((T, D), F32),
        scratch_shapes=[pltpu.VMEM((tm, D), F32)],
        compiler_params=_cparams("parallel", "arbitrary"),
    )(h2, w1, w2, g.reshape(1, D), b.reshape(1, D))


def kernel(x, ln_emb_g, ln_emb_b, w_in, conv_w, conv_b, dt_bias, a_log, d_skip, ssm_norm_g, q_norm_g, w_uq, kv_norm_g, w_ukv, cmp_pe_k, cmp_w1_k, cmp_w2_k, cmp_pe_v, cmp_w1_v, cmp_w2_v, w_out, ln1_g, ln1_b, w_mlp1, w_mlp2, ln2_g, ln2_b):
    B, S, D = x.shape
    assert D == D_MODEL and S % NSA_TILE == 0 and S // NSA_SLC_LEN <= LANE
    T = B * S
    ret_tables = _ret_tables(S)
    mla_tables = _mla_tables(S)
    nsa_tables = _nsa_tables(S)
    h = x.reshape(T, D)
    for l in range(w_in.shape[0]):
        if l == 0:
            h, *proj = _in_proj(h, _layout_w_in(w_in[l]), nsa_tables[2], entry_ln=(ln_emb_g, ln_emb_b))
        else:
            proj = _in_proj(h, _layout_w_in(w_in[l]), nsa_tables[2])
        p_ssm, p_mla, p_ret, p_nsa, nsa_kv, uk, uv = proj
        cmp_rows = (B, S // NSA_CMP_STRIDE, uk.shape[-1])
        y_a, y_c = _ssm_and_retention(p_ssm.reshape(B, S, W_SSM), conv_w[l], conv_b[l], dt_bias[l], a_log[l],
                                      d_skip[l], ssm_norm_g[l], p_ret.reshape(B, S, W_RET), ret_tables)
        wq, wk, wv = _layout_mla_weights(w_uq[l], w_ukv[l])
        q, k, v = _mla_prep(p_mla.reshape(B, S, W_MLA), q_norm_g[l], kv_norm_g[l], wq, wk, wv, mla_tables)
        y_b = _mla_attn(q, k, v)
        y_d = _nsa(p_nsa.reshape(B, S, W_NSA), nsa_kv.reshape(B, S, W_KV), uk.reshape(cmp_rows), uv.reshape(cmp_rows),
                   cmp_pe_k[l], cmp_w1_k[l], cmp_w2_k[l], cmp_pe_v[l], cmp_w1_v[l], cmp_w2_v[l], nsa_tables)
        ys = [y.reshape(T, GROUP_W) for y in (y_a, y_b, y_c, y_d)]
        h = _out_proj(h, ys, w_out, l, ln1_g[l], ln1_b[l])
        h = _mlp(h, w_mlp1, w_mlp2, l, ln2_g[l], ln2_b[l])
    return h.reshape(B, S, D)
```

```python
import functools
import math

import jax
import jax.numpy as jnp
import numpy as np
from jax import lax
from jax.experimental import pallas as pl
from jax.experimental.pallas import tpu as pltpu

F32 = jnp.float32
BF16 = jnp.bfloat16

D_MODEL = 1024
DEPTH = 2
GROUP_W = D_MODEL // 4
SSM_HEADS = 4
SSM_HEAD_DIM = GROUP_W // SSM_HEADS
SSM_GROUPS = 2
SSM_STATE = 128
SSM_CONV = 4
SSM_CHUNK = 128
SSM_XBC = GROUP_W + 2 * SSM_GROUPS * SSM_STATE
MLA_HEADS = 4
MLA_NOPE = 64
MLA_ROPE = 32
MLA_V = GROUP_W // MLA_HEADS
MLA_Q_RANK = 256
MLA_KV_RANK = 128
RET_HEADS = 4
RET_DK = 64
RET_DV = GROUP_W // RET_HEADS
RET_CHUNK = 128
NSA_HEADS = 4
NSA_DK = 64
NSA_DV = GROUP_W // NSA_HEADS
NSA_CMP_LEN = 32
NSA_CMP_STRIDE = 16
NSA_CMP_HID = 256
NSA_SLC_LEN = 64
NSA_TOPN = 16
NSA_WIN = 512
D_FF = 4 * D_MODEL
NSA_Q = 512
ROPE_THETA = 10000.0
EPS = 1e-5
NEG_INF = -1e30
LOG2_E = math.log2(math.e)
FORCED_SCORE = 1e9
DEEPNORM_ALPHA = (2.0 * DEPTH) ** 0.25

IN_SPLITS = (
    GROUP_W, SSM_XBC, SSM_HEADS,
    MLA_Q_RANK, MLA_KV_RANK, MLA_ROPE,
    RET_HEADS * RET_DK, RET_HEADS * RET_DK, RET_HEADS * RET_DV, GROUP_W,
    NSA_HEADS * NSA_DK, NSA_DK, NSA_DV, NSA_DK, NSA_DV, NSA_DK, NSA_DV, 3 * NSA_HEADS,
)

LANE = 128
W_SSM = GROUP_W + SSM_XBC + LANE
W_MLA = MLA_Q_RANK + MLA_KV_RANK + LANE
W_RET = 4 * GROUP_W
W_NSA = NSA_HEADS * LANE + LANE + LANE
W_KV = 2 * LANE + 3 * LANE
W_PROJ = W_SSM + W_MLA + W_RET + W_NSA + 4 * NSA_DK

NSA_TILE = 512
NSA_SINK = 128
MLA_TILE = 1024
VMEM_LIMIT = 48 * 1024 * 1024


def _cparams(*sem):
    return pltpu.CompilerParams(dimension_semantics=sem, vmem_limit_bytes=VMEM_LIMIT)


def _mm(a, b):
    return jnp.dot(a.astype(BF16), b.astype(BF16), preferred_element_type=F32)


def _mm_nt(a, b):
    return lax.dot_general(a.astype(BF16), b.astype(BF16), (((1,), (1,)), ((), ())),
                           preferred_element_type=F32)


def _split_f32(x):
    hi = x.astype(BF16)
    rest = x - hi.astype(F32)
    mid = rest.astype(BF16)
    lo = (rest - mid.astype(F32)).astype(BF16)
    return hi, mid, lo


def _mm_f32(a, b, exact, dims=(((1,), (0,)), ((), ()))):
    fixed, pieces = (a.astype(BF16), _split_f32(b)) if exact == "a" else (b.astype(BF16), _split_f32(a))
    out = None
    for piece in pieces:
        lhs, rhs = (fixed, piece) if exact == "a" else (piece, fixed)
        part = lax.dot_general(lhs, rhs, dims, preferred_element_type=F32)
        out = part if out is None else out + part
    return out


def _silu(x):
    return x * jax.nn.sigmoid(x)


def _softplus(x):
    return jnp.maximum(x, 0.0) + jnp.log1p(jnp.exp(-jnp.abs(x)))


def _layer_norm(x, g, b):
    mu = jnp.mean(x, axis=-1, keepdims=True)
    xc = x - mu
    var = jnp.mean(xc * xc, axis=-1, keepdims=True)
    return xc * lax.rsqrt(var + EPS) * g + b


def _iota(shape, dim):
    return lax.broadcasted_iota(jnp.int32, shape, dim)


def _pad_cols(w, width):
    return jnp.pad(w, ((0, 0), (0, width - w.shape[1])))


def _layout_w_in(w):
    offs = np.concatenate([[0], np.cumsum(IN_SPLITS)])
    p = [w[:, int(offs[i]):int(offs[i + 1])] for i in range(len(IN_SPLITS))]
    (ssm_z, ssm_xbc, ssm_dt, mla_cq, mla_ckv, mla_kr, ret_q, ret_k, ret_v, ret_g,
     nsa_q, nsa_kc, nsa_vc, nsa_ks, nsa_vs, nsa_kw, nsa_vw, nsa_gate) = p
    nsa_q_heads = [_pad_cols(nsa_q[:, h * NSA_DK:(h + 1) * NSA_DK], LANE) for h in range(NSA_HEADS)]
    cols = [ssm_z, ssm_xbc, _pad_cols(ssm_dt, LANE),
            mla_cq, mla_ckv, _pad_cols(mla_kr, LANE),
            ret_q, ret_k, ret_v, ret_g,
            *nsa_q_heads, nsa_kc, nsa_vc, _pad_cols(nsa_gate, LANE),
            nsa_ks, nsa_vs, nsa_kw, nsa_vw]
    out = jnp.concatenate(cols, axis=1)
    assert out.shape[1] == W_PROJ
    return out.astype(BF16)


def _in_proj_kernel(*refs, entry_ln):
    if entry_ln:
        h_ref, g_ref, b_ref, w_ref, kvtab_ref, hn_ref, *outs = refs
        hn = _layer_norm(h_ref[...], g_ref[...], b_ref[...])
        hn_ref[...] = hn
    else:
        h_ref, w_ref, kvtab_ref, *outs = refs
        hn = h_ref[...]
    ssm_ref, mla_ref, ret_ref, nsa_ref, kv_ref, uk_ref, uv_ref, kcv_ref = outs
    hb = hn.astype(BF16)
    off = 0
    for ref, width in ((ssm_ref, W_SSM), (mla_ref, W_MLA), (ret_ref, W_RET), (nsa_ref, W_NSA)):
        ref[...] = jnp.dot(hb, w_ref[:, off:off + width], preferred_element_type=F32)
        off += width
    groups = h_ref.shape[0] // NSA_CMP_STRIDE
    kc_lane = NSA_HEADS * LANE
    kcv_ref[...] = nsa_ref[:, kc_lane:kc_lane + LANE]
    for t in range(NSA_CMP_STRIDE):
        piece = kcv_ref[pl.ds(t, groups, stride=NSA_CMP_STRIDE), :]
        uk_ref[:, t * NSA_DK:(t + 1) * NSA_DK] = piece[:, :NSA_DK]
        uv_ref[:, t * NSA_DV:(t + 1) * NSA_DV] = piece[:, NSA_DK:]
    kv = jnp.dot(hb, w_ref[:, off:off + 4 * NSA_DK], preferred_element_type=F32)
    low = _iota((kv.shape[0], LANE), 1) < NSA_DK
    sel_kv, win_kv = kv[:, :LANE], kv[:, LANE:]
    pieces = {0: sel_kv, 2: pltpu.roll(sel_kv, NSA_DK, 1), 3: win_kv, 4: pltpu.roll(win_kv, NSA_DK, 1)}
    for slab in range(W_KV // LANE):
        lanes = slice(slab * LANE, (slab + 1) * LANE)
        tab = kvtab_ref[:, lanes]
        if slab in pieces:
            kv_ref[:, lanes] = (jnp.where(low, pieces[slab], 0.0) + tab.astype(F32)).astype(BF16)
        else:
            kv_ref[:, lanes] = tab


def _in_proj(h2, w_p, kv_table, entry_ln=None, tm=512):
    T, D = h2.shape
    S = kv_table.shape[0]
    widths = (W_SSM, W_MLA, W_RET, W_NSA)
    half = NSA_CMP_STRIDE * NSA_DK
    row = lambda w: pl.BlockSpec((tm, w), lambda i: (i, 0))
    const = lambda shape: pl.BlockSpec(shape, lambda i: (0,) * len(shape))
    in_specs = [const((D, W_PROJ)), pl.BlockSpec((tm, W_KV), lambda i: (i % (S // tm), 0))]
    out_specs = [row(w) for w in widths + (W_KV,)] + [pl.BlockSpec((tm // NSA_CMP_STRIDE, half), lambda i: (i, 0))] * 2
    out_shape = ([jax.ShapeDtypeStruct((T, w), F32) for w in widths] + [jax.ShapeDtypeStruct((T, W_KV), BF16)]
                 + [jax.ShapeDtypeStruct((T // NSA_CMP_STRIDE, half), F32)] * 2)
    operands = (w_p, kv_table)
    if entry_ln is not None:
        in_specs = [const((1, D)), const((1, D))] + in_specs
        out_specs = [row(D)] + out_specs
        out_shape = [jax.ShapeDtypeStruct((T, D), F32)] + out_shape
        operands = tuple(v.reshape(1, D) for v in entry_ln) + operands
    return pl.pallas_call(
        functools.partial(_in_proj_kernel, entry_ln=entry_ln is not None), name="in_proj",
        grid=(T // tm,),
        in_specs=[row(D)] + in_specs,
        out_specs=out_specs,
        out_shape=out_shape,
        scratch_shapes=[pltpu.VMEM((tm, LANE), F32)],
        compiler_params=_cparams("parallel"),
    )(h2, *operands)


def _ssm_chunk(p_ref, cw_ref, cb_ref, dtb_ref, alog_ref, dskip_ref, ng_ref, o_ref, state_ref, ext_ref):
    L, H, P, N = SSM_CHUNK, SSM_HEADS, SSM_HEAD_DIM, SSM_STATE
    z = p_ref[:, 0:GROUP_W]
    ext_ref[8:8 + L, :] = p_ref[:, GROUP_W:GROUP_W + SSM_XBC]
    ext = ext_ref[...]
    conv = cb_ref[...] + ext[8:] * cw_ref[SSM_CONV - 1:SSM_CONV, :]
    for j in range(SSM_CONV - 1):
        shift = SSM_CONV - 1 - j
        conv = conv + pltpu.roll(ext, shift, 0)[8:] * cw_ref[j:j + 1, :]
    ext_ref[0:8, :] = ext_ref[L:L + 8, :]
    xbc = _silu(conv)
    xs = xbc[:, 0:GROUP_W]
    b_in = xbc[:, GROUP_W:GROUP_W + SSM_GROUPS * N]
    c_in = xbc[:, GROUP_W + SSM_GROUPS * N:]

    dt = _softplus(p_ref[:, GROUP_W + SSM_XBC:] + dtb_ref[...])
    a = dt * (-jnp.exp(alog_ref[...]))
    row = _iota((L, L), 0)
    col = _iota((L, L), 1)
    tril = col <= row
    cs = _mm_f32(jnp.where(tril, 1.0, 0.0), a, exact="a")
    cs_t = cs.T
    ecs = jnp.exp(cs)
    dte = jnp.exp(cs[L - 1:L, :] - cs)
    first_head_of_pair = _iota((L, LANE), 1) < P

    def expand(x):
        pairs = [jnp.where(first_head_of_pair, x[:, h:h + 1], x[:, h + 1:h + 2]) for h in range(0, H, LANE // P)]
        return jnp.concatenate(pairs, axis=1)

    assert LANE == 2 * P
    dt_x = expand(dt)
    ecs_x = expand(ecs)
    dte_x = expand(dte)

    xdt = xs * dt_x
    wx = xdt * dte_x
    head_of_lane = _iota((L, H * P), 1) // P
    y = xs * dskip_ref[...]
    y_off = []
    rep = H // SSM_GROUPS
    for g in range(SSM_GROUPS):
        cg = c_in[:, g * N:(g + 1) * N]
        bg = b_in[:, g * N:(g + 1) * N]
        cb = _mm_nt(cg, bg)
        for h in range(g * rep, (g + 1) * rep):
            diff = cs[:, h:h + 1] - cs_t[h:h + 1, :]
            seg = jnp.where(tril, jnp.exp(jnp.where(tril, diff, 0.0)), 0.0)
            yh = _mm(cb * seg, xdt)
            y = y + jnp.where(head_of_lane == h, yh, 0.0)
        lanes = slice(g * rep * P, (g + 1) * rep * P)
        st_prev = state_ref[:, lanes]
        y_off.append(_mm(cg, st_prev))
        state_ref[:, lanes] = st_prev * ecs_x[L - 1:L, lanes] + _mm(bg.T, wx[:, lanes])
    y = y + jnp.concatenate(y_off, axis=1) * ecs_x
    y = y * _silu(z)
    ms = jnp.mean(y * y, axis=-1, keepdims=True)
    o_ref[...] = (y * lax.rsqrt(ms + EPS) * ng_ref[...]).astype(o_ref.dtype)


def _recurrent_kernel(ps_ref, cw_ref, cb_ref, dtb_ref, alog_ref, dskip_ref, ng_ref,
                      pr_ref, cos_ref, sin_ref, dec_ref, zeta_ref, xi_ref, cd_ref,
                      oa_ref, oc_ref, sstate_ref, ext_ref, rstate_ref):
    @pl.when(pl.program_id(0) == 0)
    def _():
        sstate_ref[...] = jnp.zeros_like(sstate_ref)
        rstate_ref[...] = jnp.zeros_like(rstate_ref)
        ext_ref[:, 0:8, :] = jnp.zeros((ext_ref.shape[0], 8, SSM_XBC), F32)

    for b in range(ps_ref.shape[0]):
        _ssm_chunk(ps_ref.at[b], cw_ref, cb_ref, dtb_ref, alog_ref, dskip_ref, ng_ref,
                   oa_ref.at[b], sstate_ref.at[b], ext_ref.at[b])
        _ret_chunk(pr_ref.at[b], cos_ref, sin_ref, dec_ref, zeta_ref, xi_ref, cd_ref, oc_ref.at[b], rstate_ref.at[b])


def _ret_chunk(p_ref, cos_ref, sin_ref, dec_ref, zeta_ref, xi_ref, cd_ref, o_ref, state_ref):
    L, H, DK, DV = RET_CHUNK, RET_HEADS, RET_DK, RET_DV
    W = H * DK
    q = p_ref[:, 0:W]
    k = p_ref[:, W:2 * W]
    v = p_ref[:, 2 * W:3 * W]
    gate = p_ref[:, 3 * W:4 * W]
    lane = _iota((L, W), 1)
    first_half = (lane % DK) < (DK // 2)
    head_of_lane = lane // DK

    def rope(x):
        partner = jnp.where(first_half, pltpu.roll(x, W - DK // 2, 1), pltpu.roll(x, DK // 2, 1))
        return x * cos_ref[...] + partner * sin_ref[...]

    qr = rope(q)
    kr = rope(k) * (DK ** -0.5)
    y = jnp.zeros((L, H * DV), F32)
    for h in range(H):
        qh = jnp.where(head_of_lane == h, qr, 0.0)
        sc = _mm_nt(qh, kr) * dec_ref[h]
        y = y + jnp.where(head_of_lane == h, _mm(sc, v), 0.0)
    st = state_ref[...]
    y = y + _mm(qr * xi_ref[...], st)
    same_head = (_iota((W, H * DV), 0) // DK) == (_iota((W, H * DV), 1) // DV)
    kv = _mm((kr * zeta_ref[...]).T, v)
    state_ref[...] = st * cd_ref[...] + jnp.where(same_head, kv, 0.0)
    assert DV & (DV - 1) == 0
    ms = _mm_f32(y * y, jnp.where(same_head, 1.0 / DV, 0.0), exact="b")
    o_ref[...] = (y * lax.rsqrt(ms + EPS) * _silu(gate)).astype(o_ref.dtype)


def _ret_tables(S):
    H, DK, L = RET_HEADS, RET_DK, RET_CHUNK
    inv = ROPE_THETA ** (-np.arange(0, DK, 2, dtype=np.float64) / DK)
    ang = np.arange(S, dtype=np.float64)[:, None] * inv[None, :]
    cos, sin = np.cos(ang), np.sin(ang)
    cos_t = np.tile(np.concatenate([cos, cos], axis=1), (1, H))
    sin_t = np.tile(np.concatenate([-sin, sin], axis=1), (1, H))
    log_gamma = np.log1p(-np.exp2(-5.0 - np.arange(H, dtype=np.float64)))
    pos = np.arange(L, dtype=np.float64)
    diff = pos[:, None] - pos[None, :]
    decay_in = np.where(diff >= 0, np.exp(np.maximum(diff, 0.0)[None] * log_gamma[:, None, None]), 0.0)
    zeta = np.exp((L - 1 - pos)[None] * log_gamma[:, None])
    xi = np.exp((pos + 1.0)[None] * log_gamma[:, None])
    chunk_decay = np.exp(L * log_gamma)
    zeta_x = np.repeat(zeta.T, DK, axis=1)
    xi_x = np.repeat(xi.T, DK, axis=1)
    cd_x = np.repeat(chunk_decay, RET_DV).reshape(1, H * RET_DV)
    return tuple(jnp.asarray(t, dtype=F32) for t in (cos_t, sin_t, decay_in, zeta_x, xi_x, cd_x))


def _ssm_and_retention(p_ssm, conv_w, conv_b, dt_bias, a_log, d_skip, norm_g, p_ret, tables):
    B, S, _ = p_ssm.shape
    L, H = SSM_CHUNK, RET_HEADS
    assert RET_CHUNK == L
    W = H * RET_DK
    cos_t, sin_t, decay_in, zeta_x, xi_x, cd_x = tables
    pad_h = lambda v: jnp.pad(v, (0, LANE - SSM_HEADS)).reshape(1, LANE)
    const = lambda shape: pl.BlockSpec(shape, lambda c: (0,) * len(shape))
    chunk = lambda width: pl.BlockSpec((B, L, width), lambda c: (0, c, 0))
    out_shape = jax.ShapeDtypeStruct((B, S, GROUP_W), BF16)
    return pl.pallas_call(
        _recurrent_kernel, name="ssm_retention",
        grid=(S // L,),
        in_specs=[chunk(W_SSM),
                  const((SSM_CONV, SSM_XBC)), const((1, SSM_XBC)), const((1, LANE)), const((1, LANE)),
                  const((1, GROUP_W)), const((1, GROUP_W)),
                  chunk(W_RET),
                  pl.BlockSpec((L, W), lambda c: (c, 0)), pl.BlockSpec((L, W), lambda c: (c, 0)),
                  const((H, L, L)), const((L, W)), const((L, W)), const((1, H * RET_DV))],
        out_specs=[chunk(GROUP_W), chunk(GROUP_W)],
        out_shape=[out_shape, out_shape],
        scratch_shapes=[pltpu.VMEM((B, SSM_STATE, GROUP_W), F32),
                        pltpu.VMEM((B, L + 8, SSM_XBC), F32),
                        pltpu.VMEM((B, W, H * RET_DV), F32)],
        compiler_params=_cparams("arbitrary"),
    )(p_ssm, conv_w, conv_b.reshape(1, -1), pad_h(dt_bias), pad_h(a_log),
      jnp.repeat(d_skip, SSM_HEAD_DIM).reshape(1, GROUP_W), norm_g.reshape(1, GROUP_W),
      p_ret, cos_t, sin_t, decay_in, zeta_x, xi_x, cd_x)


def _mla_prep_kernel(p_ref, qg_ref, wq_ref, kvg_ref, wk_ref, wv_ref, cos_ref, sin_ref,
                     q_ref, k_ref, v_ref):
    tm = p_ref.shape[0]
    cq = p_ref[:, 0:MLA_Q_RANK]
    ckv = p_ref[:, MLA_Q_RANK:MLA_Q_RANK + MLA_KV_RANK]
    kr = p_ref[:, MLA_Q_RANK + MLA_KV_RANK:]

    def rms(x, g):
        return x * lax.rsqrt(jnp.mean(x * x, axis=-1, keepdims=True) + EPS) * g

    q = _mm(rms(cq, qg_ref[...]), wq_ref[...])
    kvl = rms(ckv, kvg_ref[...])
    kn = _mm(kvl, wk_ref[...])
    vv = _mm(kvl, wv_ref[...])
    kr_sh = pltpu.roll(kr, MLA_NOPE, 1)
    lane = _iota((tm, LANE), 1)
    half = MLA_ROPE // 2
    low = (lane >= MLA_NOPE) & (lane < MLA_NOPE + half)
    cos = cos_ref[...]
    sin = sin_ref[...]

    def rope(x):
        partner = jnp.where(low, pltpu.roll(x, LANE - half, 1), pltpu.roll(x, half, 1))
        return x * cos + partner * sin

    scale = (MLA_NOPE + MLA_ROPE) ** -0.5 * LOG2_E
    for h in range(MLA_HEADS):
        sl = slice(h * LANE, (h + 1) * LANE)
        q_ref[h] = (rope(q[:, sl]) * scale).astype(BF16)
        k_ref[h] = rope(kn[:, sl] + kr_sh).astype(BF16)
        v_ref[h] = jnp.where(lane == _mla_ones_lane(h), 1.0, vv[:, sl]).astype(BF16)


def _mla_tables(S):
    inv = ROPE_THETA ** (-np.arange(0, MLA_ROPE, 2, dtype=np.float64) / MLA_ROPE)
    ang = np.arange(S, dtype=np.float64)[:, None] * inv[None, :]
    cos, sin = np.cos(ang), np.sin(ang)
    tail = LANE - MLA_NOPE - MLA_ROPE
    cos_t = np.concatenate([np.ones((S, MLA_NOPE)), cos, cos, np.ones((S, tail))], axis=1)
    sin_t = np.concatenate([np.zeros((S, MLA_NOPE)), -sin, sin, np.zeros((S, tail))], axis=1)
    return jnp.asarray(cos_t, dtype=F32), jnp.asarray(sin_t, dtype=F32)


def _layout_mla_weights(w_uq, w_ukv):
    H = MLA_HEADS
    dq = MLA_NOPE + MLA_ROPE
    wq = jnp.concatenate([_pad_cols(w_uq[:, h * dq:(h + 1) * dq], LANE) for h in range(H)], axis=1)
    dkv = MLA_NOPE + MLA_V
    wk, wv = [], []
    for h in range(H):
        blk = w_ukv[:, h * dkv:(h + 1) * dkv]
        wk.append(_pad_cols(blk[:, :MLA_NOPE], LANE))
        v = blk[:, MLA_NOPE:]
        zero = jnp.zeros_like(v)
        wv.append(jnp.concatenate([v, zero] if h % 2 == 0 else [zero, v], axis=1))
    return wq.astype(BF16), jnp.concatenate(wk, axis=1).astype(BF16), jnp.concatenate(wv, axis=1).astype(BF16)


def _mla_prep(p_mla, q_norm_g, kv_norm_g, wq, wk, wv, tables, tm=1024):
    B, S, _ = p_mla.shape
    H = MLA_HEADS
    cos_t, sin_t = tables
    const = lambda shape: pl.BlockSpec(shape, lambda b, i: (0,) * len(shape))
    qkv_spec = pl.BlockSpec((None, H, tm, LANE), lambda b, i: (b, 0, i, 0))
    qkv_shape = jax.ShapeDtypeStruct((B, H, S, LANE), BF16)
    return pl.pallas_call(
        _mla_prep_kernel, name="mla_prep",
        grid=(B, S // tm),
        in_specs=[pl.BlockSpec((None, tm, W_MLA), lambda b, i: (b, i, 0)),
                  const((1, MLA_Q_RANK)), const((MLA_Q_RANK, H * LANE)),
                  const((1, MLA_KV_RANK)), const((MLA_KV_RANK, H * LANE)), const((MLA_KV_RANK, H * LANE)),
                  pl.BlockSpec((tm, LANE), lambda b, i: (i, 0)),
                  pl.BlockSpec((tm, LANE), lambda b, i: (i, 0))],
        out_specs=[qkv_spec, qkv_spec, qkv_spec],
        out_shape=[qkv_shape, qkv_shape, qkv_shape],
        compiler_params=_cparams("parallel", "parallel"),
    )(p_mla, q_norm_g.reshape(1, -1), wq, kv_norm_g.reshape(1, -1), wk, wv, cos_t, sin_t)


def _mla_ones_lane(h):
    return MLA_V if h % 2 == 0 else 0


def _mla_attn_kernel(qi_ref, kj_ref, q_ref, k_ref, v_ref, o_ref, m_ref, acc_ref):
    H = MLA_HEADS
    tq, tk = q_ref.shape[1], k_ref.shape[1]
    i = qi_ref[pl.program_id(1)]
    j = kj_ref[pl.program_id(1)]

    @pl.when(j == 0)
    def _():
        m_ref[...] = jnp.full(m_ref.shape, NEG_INF, F32)
        acc_ref[...] = jnp.zeros_like(acc_ref)

    def sweep(blocks):
        nt_dims = (((1,), (1,)), ((), ()))
        scores = [[lax.dot_general(q_ref[h, r0:r0 + nr, :], k_ref[h, 0:nk, :], nt_dims, preferred_element_type=F32)
                   for (r0, nr, nk, _) in blocks] for h in range(H)]
        for h in range(H):
            for (r0, nr, nk, offset), s in zip(blocks, scores[h]):
                if offset is not None:
                    s = jnp.where(_iota((nr, nk), 1) - _iota((nr, nk), 0) <= offset, s, NEG_INF)
                rows = slice(r0, r0 + nr)
                m_prev = m_ref[h, rows]
                m_new = jnp.maximum(m_prev, jnp.max(s, axis=-1, keepdims=True))
                p = jnp.exp2(s - jnp.tile(m_new, (1, nk // LANE)))
                acc_ref[h, rows] = (jnp.exp2(m_prev - m_new) * acc_ref[h, rows]
                                    + jnp.dot(p.astype(BF16), v_ref[h, 0:nk, :], preferred_element_type=F32))
                m_ref[h, rows] = m_new

    assert tq == tk
    half = tq // 2

    @pl.when(j < i)
    def _():
        sweep([(0, tq, tk, None)])

    @pl.when(j == i)
    def _():
        sweep([(0, half, half, 0), (half, half, tk, half)])
        lane = _iota((tq, LANE), 1)
        for pair in range(H // 2):
            he, ho = 2 * pair, 2 * pair + 1
            acc_e, acc_o = acc_ref[he], acc_ref[ho]
            le = acc_e[:, _mla_ones_lane(he):_mla_ones_lane(he) + 1]
            lo = acc_o[:, _mla_ones_lane(ho):_mla_ones_lane(ho) + 1]
            o_ref[:, pair * LANE:(pair + 1) * LANE] = jnp.where(lane < MLA_V, acc_e / le, acc_o / lo).astype(o_ref.dtype)


def _mla_attn(q, k, v):
    B, H, S, _ = q.shape
    t = min(MLA_TILE, S)
    tq = t
    pairs = [(i, j) for i in range(S // tq) for j in range((i + 1) * tq // t)]
    qi = jnp.asarray([p[0] for p in pairs], jnp.int32)
    kj = jnp.asarray([p[1] for p in pairs], jnp.int32)
    grid_spec = pltpu.PrefetchScalarGridSpec(
        num_scalar_prefetch=2,
        grid=(B, len(pairs)),
        in_specs=[pl.BlockSpec((None, H, tq, LANE), lambda b, p, qi, kj: (b, 0, qi[p], 0)),
                  pl.BlockSpec((None, H, t, LANE), lambda b, p, qi, kj: (b, 0, kj[p], 0)),
                  pl.BlockSpec((None, H, t, LANE), lambda b, p, qi, kj: (b, 0, kj[p], 0))],
        out_specs=pl.BlockSpec((None, tq, GROUP_W), lambda b, p, qi, kj: (b, qi[p], 0)),
        scratch_shapes=[pltpu.VMEM((H, tq, LANE), F32), pltpu.VMEM((H, tq, LANE), F32)],
    )
    return pl.pallas_call(
        _mla_attn_kernel, name="mla_attn",
        grid_spec=grid_spec,
        out_shape=jax.ShapeDtypeStruct((B, S, GROUP_W), BF16),
        compiler_params=_cparams("parallel", "arbitrary"),
    )(qi, kj, q, k, v)


POS_HI = NSA_DK
POS_LO = NSA_DK + 3
POS_ONE = NSA_DK + 6
ONES_LANE = NSA_DV


def _split_bf16(x, parts=3):
    out, rem = [], np.float64(x)
    for _ in range(parts):
        piece = np.float64(np.float32(rem).astype(jnp.bfloat16).astype(np.float32))
        out.append(float(piece))
        rem = rem - piece
    return out


def _nsa_query_table():
    H = NSA_HEADS
    tab = np.zeros((2 * H, LANE), np.float32)
    for h in range(H):
        c = 2.0 ** (-8.0 * (h + 1) / H) * LOG2_E
        pieces = _split_bf16(c)
        tab[h, POS_HI:POS_HI + 3] = pieces
        tab[h, POS_LO:POS_LO + 3] = pieces
        tab[H + h, POS_ONE] = -sum(pieces)
    return jnp.asarray(tab)


def _nsa_pos_lanes(pos, lo_offset=0.0):
    t = np.zeros((len(pos), LANE - NSA_DK), np.float32)
    t[:, POS_HI - NSA_DK:POS_HI - NSA_DK + 3] = (NSA_SLC_LEN * (pos // NSA_SLC_LEN))[:, None]
    t[:, POS_LO - NSA_DK:POS_LO - NSA_DK + 3] = (pos % NSA_SLC_LEN + lo_offset)[:, None]
    t[:, POS_ONE - NSA_DK] = 1.0
    return t


def _nsa_queries(q_ref, qtab_ref, qb):
    Q, H = q_ref.shape[0], NSA_HEADS
    qpos = (qb * Q + _iota((Q, 1), 0)).astype(F32)
    out = []
    for h in range(H):
        q = q_ref[:, h * LANE:(h + 1) * LANE] * (NSA_DK ** -0.5 * LOG2_E)
        out.append((q + qtab_ref[h:h + 1, :] + qtab_ref[H + h:H + h + 1, :] * qpos).astype(BF16))
    return out


def _normalise(o):
    return o / o[:, ONES_LANE:ONES_LANE + 1]


def _stacked_gate(gates, branch):
    lanes = [3 * h + branch for h in range(NSA_HEADS)]
    return jnp.concatenate([gates[:, c:c + 1] for c in lanes], axis=0)


def _unstack_heads(o):
    Q = o.shape[0] // NSA_HEADS
    lane = _iota((Q, LANE), 1)
    out = []
    for pair in range(NSA_HEADS // 2):
        even = o[(2 * pair) * Q:(2 * pair + 1) * Q]
        odd = o[(2 * pair + 1) * Q:(2 * pair + 2) * Q]
        out.append(jnp.where(lane < NSA_DV, even, pltpu.roll(odd, NSA_DV, 1)))
    return jnp.concatenate(out, axis=1)


def _nsa_cmp_kernel(uk_ref, uv_ref, pek_ref, pev_ref, w1k_ref, w1v_ref, w2k_ref, w2v_ref, cpos_ref,
                    kc_ref, vc_ref, sh_ref):
    nb = uk_ref.shape[0]
    half = uk_ref.shape[1]

    def hidden(u_ref, pe_ref, w1_ref):
        u = u_ref[...]
        first = _mm(u + pe_ref[0:1, :], w1_ref[0:half, :])
        second = _mm(u + pe_ref[1:2, :], w1_ref[half:2 * half, :])
        sh_ref[0:nb, :] = second
        sh_ref[nb:nb + 8, :] = jnp.zeros((8, NSA_CMP_HID), F32)
        return first + sh_ref[pl.ds(1, nb), :]

    hk = _silu(hidden(uk_ref, pek_ref, w1k_ref))
    hv = _silu(hidden(uv_ref, pev_ref, w1v_ref))
    kc_ref[...] = (_mm(hk, w2k_ref[...]) + cpos_ref[...]).astype(BF16)
    ones_lane = jnp.where(_iota((1, LANE), 1) == ONES_LANE, 1.0, 0.0)
    vc_ref[...] = (_mm(hv, w2v_ref[...]) + ones_lane).astype(BF16)


def _nsa_compress(uk, uv, pe_k, w1_k, w2_k, pe_v, w1_v, w2_v):
    B, nb, half = uk.shape
    hid = NSA_CMP_HID
    const = lambda shape: pl.BlockSpec(shape, lambda b: (0,) * len(shape))
    w2k = _pad_cols(w2_k, LANE).astype(BF16)
    w2v = _pad_cols(w2_v, LANE).astype(BF16)
    centre = _nsa_pos_lanes(np.arange(nb) * NSA_CMP_STRIDE, 0.5 * (NSA_CMP_LEN - 1))
    cpos = jnp.asarray(np.concatenate([np.zeros((nb, NSA_DK), np.float32), centre], axis=1))
    out_spec = pl.BlockSpec((None, nb, LANE), lambda b: (b, 0, 0))
    out_shape = jax.ShapeDtypeStruct((B, nb, LANE), BF16)
    return pl.pallas_call(
        _nsa_cmp_kernel, name="nsa_compress",
        grid=(B,),
        in_specs=[pl.BlockSpec((None, nb, half), lambda b: (b, 0, 0)),
                  pl.BlockSpec((None, nb, half), lambda b: (b, 0, 0)),
                  const((2, half)), const((2, half)),
                  const((2 * half, hid)), const((2 * half, hid)),
                  const((hid, LANE)), const((hid, LANE)), const((nb, LANE))],
        out_specs=[out_spec, out_spec],
        out_shape=[out_shape, out_shape],
        scratch_shapes=[pltpu.VMEM((nb + 8, hid), F32)],
        compiler_params=_cparams("parallel"),
    )(uk, uv, pe_k.reshape(2, half), pe_v.reshape(2, half), w1_k.astype(BF16), w1_v.astype(BF16), w2k, w2v, cpos)


def _nsa_sel_kernel(q_ref, qtab_ref, gate_ref, kc_ref, vc_ref, ovt_ref, oc_ref, selb_ref, any_ref, imp_ref, *, n_slc, top_n):
    Q, H = q_ref.shape[0], NSA_HEADS
    qb = pl.program_id(1)
    nc = kc_ref.shape[0]
    qs = jnp.concatenate(_nsa_queries(q_ref, qtab_ref, qb), axis=0)
    nt_dims = (((1,), (1,)), ((), ()))

    def attend(ncols):
        s = lax.dot_general(qs, kc_ref[0:ncols, :], nt_dims, preferred_element_type=F32)
        qpos = qb * Q + (_iota((H * Q, ncols), 0) & (Q - 1))
        block_end = _iota((H * Q, ncols), 1) * NSA_CMP_STRIDE + (NSA_CMP_LEN - 1)
        s = jnp.where(block_end <= qpos, s, NEG_INF)
        e = jnp.exp2(s - jnp.max(s, axis=-1, keepdims=True))
        qpos_col = qb * Q + (_iota((H * Q, 1), 0) & (Q - 1))
        has_block = jnp.where(qpos_col >= NSA_CMP_LEN - 1, 1.0, 0.0)
        p = e * (has_block / jnp.sum(e, axis=-1, keepdims=True))
        o_c = jnp.dot(p.astype(BF16), vc_ref[0:ncols, :], preferred_element_type=F32)
        oc_ref[...] = _unstack_heads(_stacked_gate(jax.nn.sigmoid(gate_ref[...]), 0) * o_c)
        p_sum = p[0:Q]
        for h in range(1, H):
            p_sum = p_sum + p[h * Q:(h + 1) * Q]
        imp_ref[...] = _mm_f32(ovt_ref[:, 0:ncols], p_sum, exact="a", dims=nt_dims)

    tiles_needed = ((qb + 1) * Q // NSA_CMP_STRIDE + LANE - 1) // LANE
    for tiles in range(1, nc // LANE + 1):
        pl.when(tiles_needed == tiles)(functools.partial(attend, tiles * LANE))

    imp = imp_ref[...]
    blk = _iota((LANE, Q), 0)
    q_blk = (qb * Q + _iota((LANE, Q), 1)) >> int(math.log2(NSA_SLC_LEN))
    causal = blk <= q_blk
    for forced_blk in (0, q_blk, q_blk - 1):
        imp = jnp.where(blk == forced_blk, FORCED_SCORE, imp)
    imp = jnp.where(causal, imp, -1.0)
    imp = jnp.where(blk < n_slc, imp, -2.0)
    blk_f = blk.astype(F32)
    sel = jnp.zeros((LANE, Q), F32)
    for _ in range(top_n):
        m = jnp.max(imp, axis=0, keepdims=True)
        first = jnp.min(jnp.where(imp == m, blk_f, float(LANE)), axis=0, keepdims=True)
        hit = blk_f == first
        sel = jnp.where(hit, 1.0, sel)
        imp = jnp.where(hit, -3.0, imp)
    sel = jnp.where(causal, sel, 0.0).T
    selb_ref[...] = jnp.where(sel > 0.5, 0.0, NEG_INF).astype(BF16)
    any_ref[...] = jnp.max(sel, axis=0, keepdims=True)


def _nsa_select(p_nsa, qtab, kc, vc, overlap_t):
    B, S, _ = p_nsa.shape
    Q = NSA_Q
    nqb = S // Q
    nc = kc.shape[1]
    n_slc = S // NSA_SLC_LEN
    kern = functools.partial(_nsa_sel_kernel, n_slc=n_slc, top_n=min(NSA_TOPN, n_slc))
    return pl.pallas_call(
        kern, name="nsa_select",
        grid=(B, nqb),
        in_specs=[pl.BlockSpec((None, Q, NSA_HEADS * LANE), lambda b, i: (b, i, 0)),
                  pl.BlockSpec((2 * NSA_HEADS, LANE), lambda b, i: (0, 0)),
                  pl.BlockSpec((None, Q, LANE), lambda b, i: (b, i, (W_NSA - LANE) // LANE)),
                  pl.BlockSpec((None, nc, LANE), lambda b, i: (b, 0, 0)),
                  pl.BlockSpec((None, nc, LANE), lambda b, i: (b, 0, 0)),
                  pl.BlockSpec((LANE, nc), lambda b, i: (0, 0))],
        out_specs=[pl.BlockSpec((None, Q, GROUP_W), lambda b, i: (b, i, 0)),
                   pl.BlockSpec((None, Q, LANE), lambda b, i: (b, i, 0)),
                   pl.BlockSpec((None, None, 1, LANE), lambda b, i: (b, i, 0, 0))],
        out_shape=[jax.ShapeDtypeStruct((B, S, GROUP_W), F32),
                   jax.ShapeDtypeStruct((B, S, LANE), BF16),
                   jax.ShapeDtypeStruct((B, nqb, 1, LANE), F32)],
        scratch_shapes=[pltpu.VMEM((LANE, Q), F32)],
        compiler_params=_cparams("parallel", "parallel"),
    )(p_nsa, qtab, p_nsa, kc, vc, overlap_t)


def _nsa_attn_kernel(flags_ref, q_ref, qtab_ref, gate_ref, oc_ref, selb_ref, ks_ref, vs_ref, kw_ref, vw_ref,
                     o_ref, m_ref, acc_ref, ow_ref, *, nt):
    Q, H, TK = q_ref.shape[0], NSA_HEADS, NSA_TILE
    PART = 2 * Q
    b = pl.program_id(0)
    qb = pl.program_id(1)
    nqb = pl.num_programs(1)
    qh = _nsa_queries(q_ref, qtab_ref, qb)
    nt_dims = (((1,), (1,)), ((), ()))

    selb = selb_ref[...]
    qs_sel = jnp.concatenate([jnp.concatenate([q, selb], axis=1) for q in qh], axis=0)
    sink_off = jnp.where(_iota(selb.shape, 1) < NSA_SINK // NSA_SLC_LEN, NEG_INF, selb.astype(F32)).astype(BF16)
    qs_loop = jnp.concatenate([jnp.concatenate([q, sink_off], axis=1) for q in qh], axis=0)
    m_ref[...] = jnp.full(m_ref.shape, NEG_INF, F32)
    acc_ref[...] = jnp.zeros_like(acc_ref)

    def update(t, diagonal):
        rows = pl.ds(pl.multiple_of(t * TK, TK), TK)
        k_tile, v_tile = ks_ref[rows, :], vs_ref[rows, :]
        parts = [slice(i * PART, (i + 1) * PART) for i in range(H * Q // PART)]
        qs_t = qs_sel if diagonal else qs_loop
        scores = [lax.dot_general(qs_t[r], k_tile, nt_dims, preferred_element_type=F32) for r in parts]
        if diagonal:
            k_sink, v_sink = ks_ref[0:NSA_SINK, :], vs_ref[0:NSA_SINK, :]
            sink_scores = [lax.dot_general(qs_sel[r], k_sink, nt_dims, preferred_element_type=F32) for r in parts]
            sink_bias = jnp.where(t > 0, 0.0, NEG_INF)
        for idx, (r, s) in enumerate(zip(parts, scores)):
            m_prev = m_ref[r]
            if diagonal:
                ahead = _iota((PART, TK), 1) - (_iota((PART, TK), 0) & (Q - 1))
                s = jnp.where(ahead <= qb * Q - t * TK, s, NEG_INF)
                s_sink = sink_scores[idx] + sink_bias
                m_prev = jnp.maximum(m_prev, jnp.max(s_sink, axis=-1, keepdims=True))
            m_new = jnp.maximum(m_prev, jnp.max(s, axis=-1, keepdims=True))
            p = jnp.exp2(s - jnp.tile(m_new, (1, TK // LANE)))
            update_acc = jnp.dot(p.astype(BF16), v_tile, preferred_element_type=F32)
            if diagonal:
                p_sink = jnp.exp2(s_sink - jnp.tile(m_new, (1, NSA_SINK // LANE)))
                update_acc = update_acc + jnp.dot(p_sink.astype(BF16), v_sink, preferred_element_type=F32)
            acc_ref[r] = jnp.exp2(m_ref[r] - m_new) * acc_ref[r] + update_acc
            m_ref[r] = m_new

    def tile(t, carry):
        @pl.when(flags_ref[(b * nqb + qb) * nt + t] > 0)
        def _():
            update(t, False)
        return carry

    t_diag = (qb * Q) // TK
    lax.fori_loop(0, t_diag, tile, 0)
    update(t_diag, True)

    assert Q == NSA_WIN
    half = Q // 2

    def band(q_rows, first_key, n_keys, masks):
        keys = pl.ds(pl.multiple_of(first_key, half), n_keys)
        s = lax.dot_general(q_rows, kw_ref[keys, :], nt_dims, preferred_element_type=F32)
        groups = [s[:, g * half:(g + 1) * half] for g in range(n_keys // half)]
        s = jnp.concatenate([g if m is None else jnp.where(m, g, NEG_INF) for g, m in zip(groups, masks)], axis=1)
        p = jnp.exp2(s - jnp.max(s, axis=-1, keepdims=True))
        return _normalise(jnp.dot(p.astype(BF16), vw_ref[keys, :], preferred_element_type=F32))

    @pl.when(qb == 0)
    def _():
        qs_win = jnp.concatenate(qh, axis=0)
        row = _iota((H * Q, half), 0) & (Q - 1)
        col = _iota((H * Q, half), 1)
        ow_ref[...] = band(qs_win, 0, Q, [col <= row, col + half <= row])

    @pl.when(qb > 0)
    def _():
        row = _iota((H * half, half), 0) & (half - 1)
        col = _iota((H * half, half), 1)
        masks = [col > row, None, col <= row]
        for part in range(2):
            q_rows = jnp.concatenate([q[part * half:(part + 1) * half] for q in qh], axis=0)
            o_part = band(q_rows, (qb - 1) * Q + part * half, NSA_WIN + half, masks)
            for h in range(H):
                ow_ref[h * Q + part * half:h * Q + (part + 1) * half, :] = o_part[h * half:(h + 1) * half]

    gates = jax.nn.sigmoid(gate_ref[...])
    mixed = _stacked_gate(gates, 1) * _normalise(acc_ref[...]) + _stacked_gate(gates, 2) * ow_ref[...]
    o_ref[...] = (oc_ref[...] + _unstack_heads(mixed)).astype(o_ref.dtype)


def _nsa_attend(p_nsa, qtab, o_c, selb, flags, kv):
    B, S, _ = p_nsa.shape
    Q = NSA_Q
    nqb = S // Q
    nt = S // NSA_TILE
    gate_blk = (W_NSA - LANE) // LANE
    kern = functools.partial(_nsa_attn_kernel, nt=nt)
    slab = lambda width, col: pl.BlockSpec((None, S, width), lambda b, i, f: (b, 0, col))
    grid_spec = pltpu.PrefetchScalarGridSpec(
        num_scalar_prefetch=1,
        grid=(B, nqb),
        in_specs=[pl.BlockSpec((None, Q, NSA_HEADS * LANE), lambda b, i, f: (b, i, 0)),
                  pl.BlockSpec((2 * NSA_HEADS, LANE), lambda b, i, f: (0, 0)),
                  pl.BlockSpec((None, Q, LANE), lambda b, i, f: (b, i, gate_blk)),
                  pl.BlockSpec((None, Q, GROUP_W), lambda b, i, f: (b, i, 0)),
                  pl.BlockSpec((None, Q, LANE), lambda b, i, f: (b, i, 0)),
                  slab(2 * LANE, 0), slab(LANE, 2), slab(LANE, 3), slab(LANE, 4)],
        out_specs=pl.BlockSpec((None, Q, GROUP_W), lambda b, i, f: (b, i, 0)),
        scratch_shapes=[pltpu.VMEM((NSA_HEADS * Q, LANE), F32)] * 3,
    )
    return pl.pallas_call(
        kern, name="nsa_attend",
        grid_spec=grid_spec,
        out_shape=jax.ShapeDtypeStruct((B, S, GROUP_W), BF16),
        compiler_params=_cparams("parallel", "parallel"),
    )(flags, p_nsa, qtab, p_nsa, o_c, selb, kv, kv, kv, kv)


def _nsa_tables(S):
    nc = S // NSA_CMP_STRIDE
    n = np.arange(nc)[None, :]
    j = np.arange(LANE)[:, None]
    start = n * NSA_CMP_STRIDE
    ov = (start < (j + 1) * NSA_SLC_LEN) & (start + NSA_CMP_LEN - 1 >= j * NSA_SLC_LEN)
    ov &= (n < (S - NSA_CMP_LEN) // NSA_CMP_STRIDE + 1) & (j < S // NSA_SLC_LEN)
    pos = np.arange(S)
    k_zero = np.zeros((S, NSA_DK), np.float32)
    block_onehot = (pos[:, None] // NSA_SLC_LEN == np.arange(LANE)[None, :]).astype(np.float32)
    v_lanes = np.zeros((S, LANE), np.float32)
    v_lanes[:, ONES_LANE] = 1.0
    kv_table = np.concatenate([k_zero, _nsa_pos_lanes(pos), block_onehot, v_lanes,
                               k_zero, _nsa_pos_lanes(pos), v_lanes], axis=1)
    assert kv_table.shape[1] == W_KV
    return _nsa_query_table(), jnp.asarray(ov.astype(np.float32)), jnp.asarray(kv_table, dtype=BF16)


def _nsa(p_nsa, kv, uk, uv, pe_k, w1_k, w2_k, pe_v, w1_v, w2_v, tables):
    B, S, _ = p_nsa.shape
    qtab, overlap_t, _ = tables
    kc, vc = _nsa_compress(uk, uv, pe_k, w1_k, w2_k, pe_v, w1_v, w2_v)
    o_c, selb, blk_any = _nsa_select(p_nsa, qtab, kc, vc, overlap_t)
    per_tile = NSA_TILE // NSA_SLC_LEN
    nt = S // NSA_TILE
    not_sink = (np.arange(nt * per_tile) >= NSA_SINK // NSA_SLC_LEN).astype(np.float32)
    blk_any = blk_any[:, :, 0, :nt * per_tile] * not_sink
    flags = blk_any.reshape(B, S // NSA_Q, nt, per_tile).max(axis=-1)
    flags = (flags > 0).astype(jnp.int32).reshape(-1)
    return _nsa_attend(p_nsa, qtab, o_c, selb, flags, kv)


def _out_proj_kernel(h_ref, ya_ref, yb_ref, yc_ref, yd_ref, w_ref, g_ref, b_ref, o_ref, wb_ref):
    @pl.when(pl.program_id(0) == 0)
    def _():
        wb_ref[...] = w_ref[...].astype(BF16)

    mix = None
    for idx, y_ref in enumerate((ya_ref, yb_ref, yc_ref, yd_ref)):
        part = _mm(y_ref[...], wb_ref[idx * GROUP_W:(idx + 1) * GROUP_W, :])
        mix = part if mix is None else mix + part
    o_ref[...] = _layer_norm(DEEPNORM_ALPHA * h_ref[...] + mix, g_ref[...], b_ref[...])


def _out_proj(h2, ys, w_out, layer, g, b, tm=1024):
    T, D = h2.shape
    row = lambda w: pl.BlockSpec((tm, w), lambda i: (i, 0))
    const = lambda shape: pl.BlockSpec(shape, lambda i: (0,) * len(shape))
    return pl.pallas_call(
        _out_proj_kernel, name="out_proj_ln",
        grid=(T // tm,),
        in_specs=[row(D), row(GROUP_W), row(GROUP_W), row(GROUP_W), row(GROUP_W),
                  pl.BlockSpec((None, D, D), lambda i: (layer, 0, 0)), const((1, D)), const((1, D))],
        out_specs=row(D),
        out_shape=jax.ShapeDtypeStruct((T, D), F32),
        scratch_shapes=[pltpu.VMEM((D, D), BF16)],
        compiler_params=_cparams("arbitrary"),
    )(h2, *ys, w_out, g.reshape(1, D), b.reshape(1, D))


def _mlp_kernel(h_ref, w1_ref, w2_ref, g_ref, b_ref, o_ref, acc_ref):
    f = pl.program_id(1)

    @pl.when(f == 0)
    def _():
        acc_ref[...] = jnp.zeros_like(acc_ref)

    a = jnp.maximum(_mm(h_ref[...], w1_ref[...]), 0.0)
    acc_ref[...] += _mm(a * a, w2_ref[...])

    @pl.when(f == pl.num_programs(1) - 1)
    def _():
        o_ref[...] = _layer_norm(DEEPNORM_ALPHA * h_ref[...] + acc_ref[...], g_ref[...], b_ref[...])


def _mlp(h2, w1, w2, layer, g, b, tm=1024, tf=1024):
    T, D = h2.shape
    F = w1.shape[2]
    return pl.pallas_call(
        _mlp_kernel, name="mlp_ln",
        grid=(T // tm, F // tf),
        in_specs=[pl.BlockSpec((tm, D), lambda i, f: (i, 0)),
                  pl.BlockSpec((None, D, tf), lambda i, f: (layer, 0, f)),
                  pl.BlockSpec((None, tf, D), lambda i, f: (layer, f, 0)),
                  pl.BlockSpec((1, D), lambda i, f: (0, 0)),
                  pl.BlockSpec((1, D), lambda i, f: (0, 0))],
        out_specs=pl.BlockSpec((tm, D), lambda i, f: (i, 0)),
        out_shape=jax.ShapeDtypeStruct((T, D), F32),
        scratch_shapes=[pltpu.VMEM((tm, D), F32)],
        compiler_params=_cparams("parallel", "arbitrary"),
    )(h2, w1, w2, g.reshape(1, D), b.reshape(1, D))


def kernel(x, ln_emb_g, ln_emb_b, w_in, conv_w, conv_b, dt_bias, a_log, d_skip, ssm_norm_g, q_norm_g, w_uq, kv_norm_g, w_ukv, cmp_pe_k, cmp_w1_k, cmp_w2_k, cmp_pe_v, cmp_w1_v, cmp_w2_v, w_out, ln1_g, ln1_b, w_mlp1, w_mlp2, ln2_g, ln2_b):
    B, S, D = x.shape
    assert D == D_MODEL and S % NSA_TILE == 0 and S // NSA_SLC_LEN <= LANE
    T = B * S
    ret_tables = _ret_tables(S)
    mla_tables = _mla_tables(S)
    nsa_tables = _nsa_tables(S)
    h = x.reshape(T, D)
    for l in range(w_in.shape[0]):
        if l == 0:
            h, *proj = _in_proj(h, _layout_w_in(w_in[l]), nsa_tables[2], entry_ln=(ln_emb_g, ln_emb_b))
        else:
            proj = _in_proj(h, _layout_w_in(w_in[l]), nsa_tables[2])
        p_ssm, p_mla, p_ret, p_nsa, nsa_kv, uk, uv = proj
        cmp_rows = (B, S // NSA_CMP_STRIDE, uk.shape[-1])
        y_a, y_c = _ssm_and_retention(p_ssm.reshape(B, S, W_SSM), conv_w[l], conv_b[l], dt_bias[l], a_log[l],
                                      d_skip[l], ssm_norm_g[l], p_ret.reshape(B, S, W_RET), ret_tables)
        wq, wk, wv = _layout_mla_weights(w_uq[l], w_ukv[l])
        q, k, v = _mla_prep(p_mla.reshape(B, S, W_MLA), q_norm_g[l], kv_norm_g[l], wq, wk, wv, mla_tables)
        y_b = _mla_attn(q, k, v)
        y_d = _nsa(p_nsa.reshape(B, S, W_NSA), nsa_kv.reshape(B, S, W_KV), uk.reshape(cmp_rows), uv.reshape(cmp_rows),
                   cmp_pe_k[l], cmp_w1_k[l], cmp_w2_k[l], cmp_pe_v[l], cmp_w1_v[l], cmp_w2_v[l], nsa_tables)
        ys = [y.reshape(T, GROUP_W) for y in (y_a, y_b, y_c, y_d)]
        h = _out_proj(h, ys, w_out, l, ln1_g[l], ln1_b[l])
        h = _mlp(h, w_mlp1, w_mlp2, l, ln2_g[l], ln2_b[l])
    return h.reshape(B, S, D)
```

```python
import functools
import math

import jax
import jax.numpy as jnp
import numpy as np
from jax import lax
from jax.experimental import pallas as pl
from jax.experimental.pallas import tpu as pltpu

F32 = jnp.float32
BF16 = jnp.bfloat16

D_MODEL = 1024
DEPTH = 2
GROUP_W = D_MODEL // 4
SSM_HEADS = 4
SSM_HEAD_DIM = GROUP_W // SSM_HEADS
SSM_GROUPS = 2
SSM_STATE = 128
SSM_CONV = 4
SSM_CHUNK = 128
SSM_XBC = GROUP_W + 2 * SSM_GROUPS * SSM_STATE
MLA_HEADS = 4
MLA_NOPE = 64
MLA_ROPE = 32
MLA_V = GROUP_W // MLA_HEADS
MLA_Q_RANK = 256
MLA_KV_RANK = 128
RET_HEADS = 4
RET_DK = 64
RET_DV = GROUP_W // RET_HEADS
RET_CHUNK = 128
NSA_HEADS = 4
NSA_DK = 64
NSA_DV = GROUP_W // NSA_HEADS
NSA_CMP_LEN = 32
NSA_CMP_STRIDE = 16
NSA_CMP_HID = 256
NSA_SLC_LEN = 64
NSA_TOPN = 16
NSA_WIN = 512
D_FF = 4 * D_MODEL
NSA_Q = 512
ROPE_THETA = 10000.0
EPS = 1e-5
NEG_INF = -1e30
LOG2_E = math.log2(math.e)
FORCED_SCORE = 1e9
DEEPNORM_ALPHA = (2.0 * DEPTH) ** 0.25

IN_SPLITS = (
    GROUP_W, SSM_XBC, SSM_HEADS,
    MLA_Q_RANK, MLA_KV_RANK, MLA_ROPE,
    RET_HEADS * RET_DK, RET_HEADS * RET_DK, RET_HEADS * RET_DV, GROUP_W,
    NSA_HEADS * NSA_DK, NSA_DK, NSA_DV, NSA_DK, NSA_DV, NSA_DK, NSA_DV, 3 * NSA_HEADS,
)

LANE = 128
W_SSM = GROUP_W + SSM_XBC + LANE
W_MLA = MLA_Q_RANK + MLA_KV_RANK + LANE
W_RET = 4 * GROUP_W
W_NSA = NSA_HEADS * LANE + LANE + LANE
W_KV = 2 * LANE + 3 * LANE
W_PROJ = ((GROUP_W + SSM_XBC) + (MLA_Q_RANK + MLA_KV_RANK) + W_RET
          + (NSA_HEADS * NSA_DK + LANE) + 4 * NSA_DK + LANE)

NSA_TILE = 512
NSA_SINK = 128
MLA_TILE = 1024
VMEM_LIMIT = 48 * 1024 * 1024


def _cparams(*sem):
    return pltpu.CompilerParams(dimension_semantics=sem, vmem_limit_bytes=VMEM_LIMIT)


def _mm(a, b):
    return jnp.dot(a.astype(BF16), b.astype(BF16), preferred_element_type=F32)


def _mm_nt(a, b):
    return lax.dot_general(a.astype(BF16), b.astype(BF16), (((1,), (1,)), ((), ())),
                           preferred_element_type=F32)


def _split_f32(x):
    hi = x.astype(BF16)
    rest = x - hi.astype(F32)
    mid = rest.astype(BF16)
    lo = (rest - mid.astype(F32)).astype(BF16)
    return hi, mid, lo


def _mm_f32(a, b, exact, dims=(((1,), (0,)), ((), ()))):
    fixed, pieces = (a.astype(BF16), _split_f32(b)) if exact == "a" else (b.astype(BF16), _split_f32(a))
    out = None
    for piece in pieces:
        lhs, rhs = (fixed, piece) if exact == "a" else (piece, fixed)
        part = lax.dot_general(lhs, rhs, dims, preferred_element_type=F32)
        out = part if out is None else out + part
    return out


def _silu(x):
    return x * jax.nn.sigmoid(x)


def _softplus(x):
    return jnp.maximum(x, 0.0) + jnp.log1p(jnp.exp(-jnp.abs(x)))


def _layer_norm(x, g, b):
    mu = jnp.mean(x, axis=-1, keepdims=True)
    xc = x - mu
    var = jnp.mean(xc * xc, axis=-1, keepdims=True)
    return xc * lax.rsqrt(var + EPS) * g + b


def _iota(shape, dim):
    return lax.broadcasted_iota(jnp.int32, shape, dim)


def _pad_cols(w, width):
    return jnp.pad(w, ((0, 0), (0, width - w.shape[1])))


def _layout_w_in(w):
    offs = np.concatenate([[0], np.cumsum(IN_SPLITS)])
    p = [w[:, int(offs[i]):int(offs[i + 1])] for i in range(len(IN_SPLITS))]
    (ssm_z, ssm_xbc, ssm_dt, mla_cq, mla_ckv, mla_kr, ret_q, ret_k, ret_v, ret_g,
     nsa_q, nsa_kc, nsa_vc, nsa_ks, nsa_vs, nsa_kw, nsa_vw, nsa_gate) = p
    small = _pad_cols(jnp.concatenate([mla_kr, ssm_dt, nsa_gate], axis=1), LANE)
    cols = [ssm_z, ssm_xbc, ret_q, ret_k, ret_v, ret_g, nsa_q, nsa_ks, nsa_vs, nsa_kw, nsa_vw,
            mla_cq, mla_ckv, nsa_kc, nsa_vc, small]
    out = jnp.concatenate(cols, axis=1)
    assert out.shape[1] == W_PROJ
    return out.astype(BF16)


def _in_proj_kernel(*refs, entry_ln):
    if entry_ln:
        h_ref, g_ref, b_ref, w_ref, kvtab_ref, hn_ref, *outs = refs
        hn = _layer_norm(h_ref[...], g_ref[...], b_ref[...])
        hn_ref[...] = hn
    else:
        h_ref, w_ref, kvtab_ref, *outs = refs
        hn = h_ref[...]
    ssm_ref, mla_ref, ret_ref, nsa_ref, kv_ref, uk_ref, uv_ref, kcv_ref = outs
    hb = hn.astype(BF16)
    tm = hb.shape[0]
    off = 0

    def project(width):
        nonlocal off
        out = jnp.dot(hb, w_ref[:, off:off + width], preferred_element_type=F32)
        off += width
        return out

    ssm_ref[:, 0:GROUP_W + SSM_XBC] = project(GROUP_W + SSM_XBC)
    ret_ref[...] = project(W_RET)
    q = project(NSA_HEADS * NSA_DK)
    kv = project(4 * NSA_DK)
    latent = MLA_Q_RANK + MLA_KV_RANK
    mla_kc = project(latent + LANE)
    mla_ref[:, 0:latent] = mla_kc[:, 0:latent]
    kc_lane = NSA_HEADS * LANE
    nsa_ref[:, kc_lane:kc_lane + LANE] = mla_kc[:, latent:]
    small = project(LANE)
    lane = _iota((tm, LANE), 1)
    low = lane < NSA_DK
    for h in range(NSA_HEADS):
        pair = q[:, (h // 2) * LANE:(h // 2 + 1) * LANE]
        head = pair if h % 2 == 0 else pltpu.roll(pair, NSA_DK, 1)
        nsa_ref[:, h * LANE:(h + 1) * LANE] = jnp.where(low, head, 0.0)
    mla_ref[:, MLA_Q_RANK + MLA_KV_RANK:] = jnp.where(lane < MLA_ROPE, small, 0.0)
    ssm_ref[:, GROUP_W + SSM_XBC:] = jnp.where(lane < SSM_HEADS, pltpu.roll(small, LANE - MLA_ROPE, 1), 0.0)
    gate_at = MLA_ROPE + SSM_HEADS
    nsa_ref[:, kc_lane + LANE:] = jnp.where(lane < 3 * NSA_HEADS, pltpu.roll(small, LANE - gate_at, 1), 0.0)
    groups = h_ref.shape[0] // NSA_CMP_STRIDE
    kc_lane = NSA_HEADS * LANE
    kcv_ref[...] = nsa_ref[:, kc_lane:kc_lane + LANE]
    for t in range(NSA_CMP_STRIDE):
        piece = kcv_ref[pl.ds(t, groups, stride=NSA_CMP_STRIDE), :]
        uk_ref[:, t * NSA_DK:(t + 1) * NSA_DK] = piece[:, :NSA_DK]
        uv_ref[:, t * NSA_DV:(t + 1) * NSA_DV] = piece[:, NSA_DK:]
    sel_kv, win_kv = kv[:, :LANE], kv[:, LANE:]
    pieces = {0: sel_kv, 2: pltpu.roll(sel_kv, NSA_DK, 1), 3: win_kv, 4: pltpu.roll(win_kv, NSA_DK, 1)}
    for slab in range(W_KV // LANE):
        lanes = slice(slab * LANE, (slab + 1) * LANE)
        tab = kvtab_ref[:, lanes]
        if slab in pieces:
            kv_ref[:, lanes] = (jnp.where(low, pieces[slab], 0.0) + tab.astype(F32)).astype(BF16)
        else:
            kv_ref[:, lanes] = tab


def _in_proj(h2, w_p, kv_table, entry_ln=None, tm=512):
    T, D = h2.shape
    S = kv_table.shape[0]
    widths = (W_SSM, W_MLA, W_RET, W_NSA)
    half = NSA_CMP_STRIDE * NSA_DK
    row = lambda w: pl.BlockSpec((tm, w), lambda i: (i, 0))
    const = lambda shape: pl.BlockSpec(shape, lambda i: (0,) * len(shape))
    in_specs = [const((D, W_PROJ)), pl.BlockSpec((tm, W_KV), lambda i: (i % (S // tm), 0))]
    out_specs = [row(w) for w in widths + (W_KV,)] + [pl.BlockSpec((tm // NSA_CMP_STRIDE, half), lambda i: (i, 0))] * 2
    out_shape = ([jax.ShapeDtypeStruct((T, w), F32) for w in widths] + [jax.ShapeDtypeStruct((T, W_KV), BF16)]
                 + [jax.ShapeDtypeStruct((T // NSA_CMP_STRIDE, half), F32)] * 2)
    operands = (w_p, kv_table)
    if entry_ln is not None:
        in_specs = [const((1, D)), const((1, D))] + in_specs
        out_specs = [row(D)] + out_specs
        out_shape = [jax.ShapeDtypeStruct((T, D), F32)] + out_shape
        operands = tuple(v.reshape(1, D) for v in entry_ln) + operands
    return pl.pallas_call(
        functools.partial(_in_proj_kernel, entry_ln=entry_ln is not None), name="in_proj",
        grid=(T // tm,),
        in_specs=[row(D)] + in_specs,
        out_specs=out_specs,
        out_shape=out_shape,
        scratch_shapes=[pltpu.VMEM((tm, LANE), F32)],
        compiler_params=_cparams("parallel"),
    )(h2, *operands)


def _ssm_chunk(p_ref, cw_ref, cb_ref, dtb_ref, alog_ref, dskip_ref, ng_ref, o_ref, state_ref, ext_ref):
    L, H, P, N = SSM_CHUNK, SSM_HEADS, SSM_HEAD_DIM, SSM_STATE
    z = p_ref[:, 0:GROUP_W]
    ext_ref[8:8 + L, :] = p_ref[:, GROUP_W:GROUP_W + SSM_XBC]
    ext = ext_ref[...]
    conv = cb_ref[...] + ext[8:] * cw_ref[SSM_CONV - 1:SSM_CONV, :]
    for j in range(SSM_CONV - 1):
        shift = SSM_CONV - 1 - j
        conv = conv + pltpu.roll(ext, shift, 0)[8:] * cw_ref[j:j + 1, :]
    ext_ref[0:8, :] = ext_ref[L:L + 8, :]
    xbc = _silu(conv)
    xs = xbc[:, 0:GROUP_W]
    b_in = xbc[:, GROUP_W:GROUP_W + SSM_GROUPS * N]
    c_in = xbc[:, GROUP_W + SSM_GROUPS * N:]

    dt = _softplus(p_ref[:, GROUP_W + SSM_XBC:] + dtb_ref[...])
    a = dt * (-jnp.exp(alog_ref[...]))
    row = _iota((L, L), 0)
    col = _iota((L, L), 1)
    tril = col <= row
    cs = _mm_f32(jnp.where(tril, 1.0, 0.0), a, exact="a")
    cs_t = cs.T
    ecs = jnp.exp(cs)
    dte = jnp.exp(cs[L - 1:L, :] - cs)
    first_head_of_pair = _iota((L, LANE), 1) < P

    def expand(x):
        pairs = [jnp.where(first_head_of_pair, x[:, h:h + 1], x[:, h + 1:h + 2]) for h in range(0, H, LANE // P)]
        return jnp.concatenate(pairs, axis=1)

    assert LANE == 2 * P
    dt_x = expand(dt)
    ecs_x = expand(ecs)
    dte_x = expand(dte)

    xdt = xs * dt_x
    wx = xdt * dte_x
    head_of_lane = _iota((L, H * P), 1) // P
    y = xs * dskip_ref[...]
    y_off = []
    rep = H // SSM_GROUPS
    for g in range(SSM_GROUPS):
        cg = c_in[:, g * N:(g + 1) * N]
        bg = b_in[:, g * N:(g + 1) * N]
        cb = _mm_nt(cg, bg)
        for h in range(g * rep, (g + 1) * rep):
            diff = cs[:, h:h + 1] - cs_t[h:h + 1, :]
            seg = jnp.where(tril, jnp.exp(jnp.where(tril, diff, 0.0)), 0.0)
            yh = _mm(cb * seg, xdt)
            y = y + jnp.where(head_of_lane == h, yh, 0.0)
        lanes = slice(g * rep * P, (g + 1) * rep * P)
        st_prev = state_ref[:, lanes]
        y_off.append(_mm(cg, st_prev))
        state_ref[:, lanes] = st_prev * ecs_x[L - 1:L, lanes] + _mm(bg.T, wx[:, lanes])
    y = y + jnp.concatenate(y_off, axis=1) * ecs_x
    y = y * _silu(z)
    ms = jnp.mean(y * y, axis=-1, keepdims=True)
    o_ref[...] = (y * lax.rsqrt(ms + EPS) * ng_ref[...]).astype(o_ref.dtype)


def _recurrent_kernel(ps_ref, cw_ref, cb_ref, dtb_ref, alog_ref, dskip_ref, ng_ref,
                      pr_ref, cos_ref, sin_ref, dec_ref, zeta_ref, xi_ref, cd_ref,
                      oa_ref, oc_ref, sstate_ref, ext_ref, rstate_ref):
    @pl.when(pl.program_id(0) == 0)
    def _():
        sstate_ref[...] = jnp.zeros_like(sstate_ref)
        rstate_ref[...] = jnp.zeros_like(rstate_ref)
        ext_ref[:, 0:8, :] = jnp.zeros((ext_ref.shape[0], 8, SSM_XBC), F32)

    for b in range(ps_ref.shape[0]):
        _ssm_chunk(ps_ref.at[b], cw_ref, cb_ref, dtb_ref, alog_ref, dskip_ref, ng_ref,
                   oa_ref.at[b], sstate_ref.at[b], ext_ref.at[b])
        _ret_chunk(pr_ref.at[b], cos_ref, sin_ref, dec_ref, zeta_ref, xi_ref, cd_ref, oc_ref.at[b], rstate_ref.at[b])


def _ret_chunk(p_ref, cos_ref, sin_ref, dec_ref, zeta_ref, xi_ref, cd_ref, o_ref, state_ref):
    L, H, DK, DV = RET_CHUNK, RET_HEADS, RET_DK, RET_DV
    W = H * DK
    q = p_ref[:, 0:W]
    k = p_ref[:, W:2 * W]
    v = p_ref[:, 2 * W:3 * W]
    gate = p_ref[:, 3 * W:4 * W]
    lane = _iota((L, W), 1)
    first_half = (lane % DK) < (DK // 2)
    head_of_lane = lane // DK

    def rope(x):
        partner = jnp.where(first_half, pltpu.roll(x, W - DK // 2, 1), pltpu.roll(x, DK // 2, 1))
        return x * cos_ref[...] + partner * sin_ref[...]

    qr = rope(q)
    kr = rope(k) * (DK ** -0.5)
    y = jnp.zeros((L, H * DV), F32)
    for h in range(H):
        qh = jnp.where(head_of_lane == h, qr, 0.0)
        sc = _mm_nt(qh, kr) * dec_ref[h]
        y = y + jnp.where(head_of_lane == h, _mm(sc, v), 0.0)
    st = state_ref[...]
    y = y + _mm(qr * xi_ref[...], st)
    same_head = (_iota((W, H * DV), 0) // DK) == (_iota((W, H * DV), 1) // DV)
    kv = _mm((kr * zeta_ref[...]).T, v)
    state_ref[...] = st * cd_ref[...] + jnp.where(same_head, kv, 0.0)
    assert DV & (DV - 1) == 0
    ms = _mm_f32(y * y, jnp.where(same_head, 1.0 / DV, 0.0), exact="b")
    o_ref[...] = (y * lax.rsqrt(ms + EPS) * _silu(gate)).astype(o_ref.dtype)


def _ret_tables(S):
    H, DK, L = RET_HEADS, RET_DK, RET_CHUNK
    inv = ROPE_THETA ** (-np.arange(0, DK, 2, dtype=np.float64) / DK)
    ang = np.arange(S, dtype=np.float64)[:, None] * inv[None, :]
    cos, sin = np.cos(ang), np.sin(ang)
    cos_t = np.tile(np.concatenate([cos, cos], axis=1), (1, H))
    sin_t = np.tile(np.concatenate([-sin, sin], axis=1), (1, H))
    log_gamma = np.log1p(-np.exp2(-5.0 - np.arange(H, dtype=np.float64)))
    pos = np.arange(L, dtype=np.float64)
    diff = pos[:, None] - pos[None, :]
    decay_in = np.where(diff >= 0, np.exp(np.maximum(diff, 0.0)[None] * log_gamma[:, None, None]), 0.0)
    zeta = np.exp((L - 1 - pos)[None] * log_gamma[:, None])
    xi = np.exp((pos + 1.0)[None] * log_gamma[:, None])
    chunk_decay = np.exp(L * log_gamma)
    zeta_x = np.repeat(zeta.T, DK, axis=1)
    xi_x = np.repeat(xi.T, DK, axis=1)
    cd_x = np.repeat(chunk_decay, RET_DV).reshape(1, H * RET_DV)
    return tuple(jnp.asarray(t, dtype=F32) for t in (cos_t, sin_t, decay_in, zeta_x, xi_x, cd_x))


def _ssm_and_retention(p_ssm, conv_w, conv_b, dt_bias, a_log, d_skip, norm_g, p_ret, tables):
    B, S, _ = p_ssm.shape
    L, H = SSM_CHUNK, RET_HEADS
    assert RET_CHUNK == L
    W = H * RET_DK
    cos_t, sin_t, decay_in, zeta_x, xi_x, cd_x = tables
    pad_h = lambda v: jnp.pad(v, (0, LANE - SSM_HEADS)).reshape(1, LANE)
    const = lambda shape: pl.BlockSpec(shape, lambda c: (0,) * len(shape))
    chunk = lambda width: pl.BlockSpec((B, L, width), lambda c: (0, c, 0))
    out_shape = jax.ShapeDtypeStruct((B, S, GROUP_W), BF16)
    return pl.pallas_call(
        _recurrent_kernel, name="ssm_retention",
        grid=(S // L,),
        in_specs=[chunk(W_SSM),
                  const((SSM_CONV, SSM_XBC)), const((1, SSM_XBC)), const((1, LANE)), const((1, LANE)),
                  const((1, GROUP_W)), const((1, GROUP_W)),
                  chunk(W_RET),
                  pl.BlockSpec((L, W), lambda c: (c, 0)), pl.BlockSpec((L, W), lambda c: (c, 0)),
                  const((H, L, L)), const((L, W)), const((L, W)), const((1, H * RET_DV))],
        out_specs=[chunk(GROUP_W), chunk(GROUP_W)],
        out_shape=[out_shape, out_shape],
        scratch_shapes=[pltpu.VMEM((B, SSM_STATE, GROUP_W), F32),
                        pltpu.VMEM((B, L + 8, SSM_XBC), F32),
                        pltpu.VMEM((B, W, H * RET_DV), F32)],
        compiler_params=_cparams("arbitrary"),
    )(p_ssm, conv_w, conv_b.reshape(1, -1), pad_h(dt_bias), pad_h(a_log),
      jnp.repeat(d_skip, SSM_HEAD_DIM).reshape(1, GROUP_W), norm_g.reshape(1, GROUP_W),
      p_ret, cos_t, sin_t, decay_in, zeta_x, xi_x, cd_x)


def _mla_prep_kernel(p_ref, qg_ref, wq_ref, kvg_ref, wk_ref, wv_ref, cos_ref, sin_ref,
                     q_ref, k_ref, v_ref):
    tm = p_ref.shape[0]
    cq = p_ref[:, 0:MLA_Q_RANK]
    ckv = p_ref[:, MLA_Q_RANK:MLA_Q_RANK + MLA_KV_RANK]
    kr = p_ref[:, MLA_Q_RANK + MLA_KV_RANK:]

    def rms(x, g):
        return x * lax.rsqrt(jnp.mean(x * x, axis=-1, keepdims=True) + EPS) * g

    q = _mm(rms(cq, qg_ref[...]), wq_ref[...])
    kvl = rms(ckv, kvg_ref[...])
    kn = _mm(kvl, wk_ref[...])
    vv = _mm(kvl, wv_ref[...])
    kr_sh = pltpu.roll(kr, MLA_NOPE, 1)
    lane = _iota((tm, LANE), 1)
    half = MLA_ROPE // 2
    low = (lane >= MLA_NOPE) & (lane < MLA_NOPE + half)
    cos = cos_ref[...]
    sin = sin_ref[...]

    def rope(x):
        partner = jnp.where(low, pltpu.roll(x, LANE - half, 1), pltpu.roll(x, half, 1))
        return x * cos + partner * sin

    scale = (MLA_NOPE + MLA_ROPE) ** -0.5 * LOG2_E
    for h in range(MLA_HEADS):
        sl = slice(h * LANE, (h + 1) * LANE)
        q_ref[h] = (rope(q[:, sl]) * scale).astype(BF16)
        k_ref[h] = rope(kn[:, sl] + kr_sh).astype(BF16)
        v_ref[h] = jnp.where(lane == _mla_ones_lane(h), 1.0, vv[:, sl]).astype(BF16)


def _mla_tables(S):
    inv = ROPE_THETA ** (-np.arange(0, MLA_ROPE, 2, dtype=np.float64) / MLA_ROPE)
    ang = np.arange(S, dtype=np.float64)[:, None] * inv[None, :]
    cos, sin = np.cos(ang), np.sin(ang)
    tail = LANE - MLA_NOPE - MLA_ROPE
    cos_t = np.concatenate([np.ones((S, MLA_NOPE)), cos, cos, np.ones((S, tail))], axis=1)
    sin_t = np.concatenate([np.zeros((S, MLA_NOPE)), -sin, sin, np.zeros((S, tail))], axis=1)
    return jnp.asarray(cos_t, dtype=F32), jnp.asarray(sin_t, dtype=F32)


def _layout_mla_weights(w_uq, w_ukv):
    H = MLA_HEADS
    dq = MLA_NOPE + MLA_ROPE
    wq = jnp.concatenate([_pad_cols(w_uq[:, h * dq:(h + 1) * dq], LANE) for h in range(H)], axis=1)
    dkv = MLA_NOPE + MLA_V
    wk, wv = [], []
    for h in range(H):
        blk = w_ukv[:, h * dkv:(h + 1) * dkv]
        wk.append(_pad_cols(blk[:, :MLA_NOPE], LANE))
        v = blk[:, MLA_NOPE:]
        zero = jnp.zeros_like(v)
        wv.append(jnp.concatenate([v, zero] if h % 2 == 0 else [zero, v], axis=1))
    return wq.astype(BF16), jnp.concatenate(wk, axis=1).astype(BF16), jnp.concatenate(wv, axis=1).astype(BF16)


def _mla_prep(p_mla, q_norm_g, kv_norm_g, wq, wk, wv, tables, tm=1024):
    B, S, _ = p_mla.shape
    H = MLA_HEADS
    cos_t, sin_t = tables
    const = lambda shape: pl.BlockSpec(shape, lambda b, i: (0,) * len(shape))
    qkv_spec = pl.BlockSpec((None, H, tm, LANE), lambda b, i: (b, 0, i, 0))
    qkv_shape = jax.ShapeDtypeStruct((B, H, S, LANE), BF16)
    return pl.pallas_call(
        _mla_prep_kernel, name="mla_prep",
        grid=(B, S // tm),
        in_specs=[pl.BlockSpec((None, tm, W_MLA), lambda b, i: (b, i, 0)),
                  const((1, MLA_Q_RANK)), const((MLA_Q_RANK, H * LANE)),
                  const((1, MLA_KV_RANK)), const((MLA_KV_RANK, H * LANE)), const((MLA_KV_RANK, H * LANE)),
                  pl.BlockSpec((tm, LANE), lambda b, i: (i, 0)),
                  pl.BlockSpec((tm, LANE), lambda b, i: (i, 0))],
        out_specs=[qkv_spec, qkv_spec, qkv_spec],
        out_shape=[qkv_shape, qkv_shape, qkv_shape],
        compiler_params=_cparams("parallel", "parallel"),
    )(p_mla, q_norm_g.reshape(1, -1), wq, kv_norm_g.reshape(1, -1), wk, wv, cos_t, sin_t)


def _mla_ones_lane(h):
    return MLA_V if h % 2 == 0 else 0


def _mla_attn_kernel(qi_ref, kj_ref, q_ref, k_ref, v_ref, o_ref, m_ref, acc_ref):
    H = MLA_HEADS
    tq, tk = q_ref.shape[1], k_ref.shape[1]
    i = qi_ref[pl.program_id(1)]
    j = kj_ref[pl.program_id(1)]

    @pl.when(j == 0)
    def _():
        m_ref[...] = jnp.full(m_ref.shape, NEG_INF, F32)
        acc_ref[...] = jnp.zeros_like(acc_ref)

    def sweep(blocks):
        nt_dims = (((1,), (1,)), ((), ()))
        scores = [[lax.dot_general(q_ref[h, r0:r0 + nr, :], k_ref[h, 0:nk, :], nt_dims, preferred_element_type=F32)
                   for (r0, nr, nk, _) in blocks] for h in range(H)]
        for h in range(H):
            for (r0, nr, nk, offset), s in zip(blocks, scores[h]):
                if offset is not None:
                    s = jnp.where(_iota((nr, nk), 1) - _iota((nr, nk), 0) <= offset, s, NEG_INF)
                rows = slice(r0, r0 + nr)
                m_prev = m_ref[h, rows]
                m_new = jnp.maximum(m_prev, jnp.max(s, axis=-1, keepdims=True))
                p = jnp.exp2(s - jnp.tile(m_new, (1, nk // LANE)))
                acc_ref[h, rows] = (jnp.exp2(m_prev - m_new) * acc_ref[h, rows]
                                    + jnp.dot(p.astype(BF16), v_ref[h, 0:nk, :], preferred_element_type=F32))
                m_ref[h, rows] = m_new

    assert tq == tk
    half = tq // 2

    @pl.when(j < i)
    def _():
        sweep([(0, tq, tk, None)])

    @pl.when(j == i)
    def _():
        sweep([(0, half, half, 0), (half, half, tk, half)])
        lane = _iota((tq, LANE), 1)
        for pair in range(H // 2):
            he, ho = 2 * pair, 2 * pair + 1
            acc_e, acc_o = acc_ref[he], acc_ref[ho]
            le = acc_e[:, _mla_ones_lane(he):_mla_ones_lane(he) + 1]
            lo = acc_o[:, _mla_ones_lane(ho):_mla_ones_lane(ho) + 1]
            o_ref[:, pair * LANE:(pair + 1) * LANE] = jnp.where(lane < MLA_V, acc_e / le, acc_o / lo).astype(o_ref.dtype)


def _mla_attn(q, k, v):
    B, H, S, _ = q.shape
    t = min(MLA_TILE, S)
    tq = t
    pairs = [(i, j) for i in range(S // tq) for j in range((i + 1) * tq // t)]
    qi = jnp.asarray([p[0] for p in pairs], jnp.int32)
    kj = jnp.asarray([p[1] for p in pairs], jnp.int32)
    grid_spec = pltpu.PrefetchScalarGridSpec(
        num_scalar_prefetch=2,
        grid=(B, len(pairs)),
        in_specs=[pl.BlockSpec((None, H, tq, LANE), lambda b, p, qi, kj: (b, 0, qi[p], 0)),
                  pl.BlockSpec((None, H, t, LANE), lambda b, p, qi, kj: (b, 0, kj[p], 0)),
                  pl.BlockSpec((None, H, t, LANE), lambda b, p, qi, kj: (b, 0, kj[p], 0))],
        out_specs=pl.BlockSpec((None, tq, GROUP_W), lambda b, p, qi, kj: (b, qi[p], 0)),
        scratch_shapes=[pltpu.VMEM((H, tq, LANE), F32), pltpu.VMEM((H, tq, LANE), F32)],
    )
    return pl.pallas_call(
        _mla_attn_kernel, name="mla_attn",
        grid_spec=grid_spec,
        out_shape=jax.ShapeDtypeStruct((B, S, GROUP_W), BF16),
        compiler_params=_cparams("parallel", "arbitrary"),
    )(qi, kj, q, k, v)


POS_HI = NSA_DK
POS_LO = NSA_DK + 3
POS_ONE = NSA_DK + 6
ONES_LANE = NSA_DV


def _split_bf16(x, parts=3):
    out, rem = [], np.float64(x)
    for _ in range(parts):
        piece = np.float64(np.float32(rem).astype(jnp.bfloat16).astype(np.float32))
        out.append(float(piece))
        rem = rem - piece
    return out


def _nsa_query_table():
    H = NSA_HEADS
    tab = np.zeros((2 * H, LANE), np.float32)
    for h in range(H):
        c = 2.0 ** (-8.0 * (h + 1) / H) * LOG2_E
        pieces = _split_bf16(c)
        tab[h, POS_HI:POS_HI + 3] = pieces
        tab[h, POS_LO:POS_LO + 3] = pieces
        tab[H + h, POS_ONE] = -sum(pieces)
    return jnp.asarray(tab)


def _nsa_pos_lanes(pos, lo_offset=0.0):
    t = np.zeros((len(pos), LANE - NSA_DK), np.float32)
    t[:, POS_HI - NSA_DK:POS_HI - NSA_DK + 3] = (NSA_SLC_LEN * (pos // NSA_SLC_LEN))[:, None]
    t[:, POS_LO - NSA_DK:POS_LO - NSA_DK + 3] = (pos % NSA_SLC_LEN + lo_offset)[:, None]
    t[:, POS_ONE - NSA_DK] = 1.0
    return t


def _nsa_queries(q_ref, qtab_ref, qb):
    Q, H = q_ref.shape[0], NSA_HEADS
    qpos = (qb * Q + _iota((Q, 1), 0)).astype(F32)
    out = []
    for h in range(H):
        q = q_ref[:, h * LANE:(h + 1) * LANE] * (NSA_DK ** -0.5 * LOG2_E)
        out.append((q + qtab_ref[h:h + 1, :] + qtab_ref[H + h:H + h + 1, :] * qpos).astype(BF16))
    return out


def _normalise(o):
    return o / o[:, ONES_LANE:ONES_LANE + 1]


def _stacked_gate(gates, branch):
    lanes = [3 * h + branch for h in range(NSA_HEADS)]
    return jnp.concatenate([gates[:, c:c + 1] for c in lanes], axis=0)


def _unstack_heads(o):
    Q = o.shape[0] // NSA_HEADS
    lane = _iota((Q, LANE), 1)
    out = []
    for pair in range(NSA_HEADS // 2):
        even = o[(2 * pair) * Q:(2 * pair + 1) * Q]
        odd = o[(2 * pair + 1) * Q:(2 * pair + 2) * Q]
        out.append(jnp.where(lane < NSA_DV, even, pltpu.roll(odd, NSA_DV, 1)))
    return jnp.concatenate(out, axis=1)


def _nsa_cmp_kernel(uk_ref, uv_ref, pek_ref, pev_ref, w1k_ref, w1v_ref, w2k_ref, w2v_ref, cpos_ref,
                    kc_ref, vc_ref, sh_ref):
    nb = uk_ref.shape[0]
    half = uk_ref.shape[1]

    def hidden(u_ref, pe_ref, w1_ref):
        u = u_ref[...]
        first = _mm(u + pe_ref[0:1, :], w1_ref[0:half, :])
        second = _mm(u + pe_ref[1:2, :], w1_ref[half:2 * half, :])
        sh_ref[0:nb, :] = second
        sh_ref[nb:nb + 8, :] = jnp.zeros((8, NSA_CMP_HID), F32)
        return first + sh_ref[pl.ds(1, nb), :]

    hk = _silu(hidden(uk_ref, pek_ref, w1k_ref))
    hv = _silu(hidden(uv_ref, pev_ref, w1v_ref))
    kc_ref[...] = (_mm(hk, w2k_ref[...]) + cpos_ref[...]).astype(BF16)
    ones_lane = jnp.where(_iota((1, LANE), 1) == ONES_LANE, 1.0, 0.0)
    vc_ref[...] = (_mm(hv, w2v_ref[...]) + ones_lane).astype(BF16)


def _nsa_compress(uk, uv, pe_k, w1_k, w2_k, pe_v, w1_v, w2_v):
    B, nb, half = uk.shape
    hid = NSA_CMP_HID
    const = lambda shape: pl.BlockSpec(shape, lambda b: (0,) * len(shape))
    w2k = _pad_cols(w2_k, LANE).astype(BF16)
    w2v = _pad_cols(w2_v, LANE).astype(BF16)
    centre = _nsa_pos_lanes(np.arange(nb) * NSA_CMP_STRIDE, 0.5 * (NSA_CMP_LEN - 1))
    cpos = jnp.asarray(np.concatenate([np.zeros((nb, NSA_DK), np.float32), centre], axis=1))
    out_spec = pl.BlockSpec((None, nb, LANE), lambda b: (b, 0, 0))
    out_shape = jax.ShapeDtypeStruct((B, nb, LANE), BF16)
    return pl.pallas_call(
        _nsa_cmp_kernel, name="nsa_compress",
        grid=(B,),
        in_specs=[pl.BlockSpec((None, nb, half), lambda b: (b, 0, 0)),
                  pl.BlockSpec((None, nb, half), lambda b: (b, 0, 0)),
                  const((2, half)), const((2, half)),
                  const((2 * half, hid)), const((2 * half, hid)),
                  const((hid, LANE)), const((hid, LANE)), const((nb, LANE))],
        out_specs=[out_spec, out_spec],
        out_shape=[out_shape, out_shape],
        scratch_shapes=[pltpu.VMEM((nb + 8, hid), F32)],
        compiler_params=_cparams("parallel"),
    )(uk, uv, pe_k.reshape(2, half), pe_v.reshape(2, half), w1_k.astype(BF16), w1_v.astype(BF16), w2k, w2v, cpos)


def _nsa_sel_kernel(q_ref, qtab_ref, gate_ref, kc_ref, vc_ref, ovt_ref, oc_ref, selb_ref, any_ref, imp_ref, *, n_slc, top_n):
    Q, H = q_ref.shape[0], NSA_HEADS
    qb = pl.program_id(1)
    nc = kc_ref.shape[0]
    qs = jnp.concatenate(_nsa_queries(q_ref, qtab_ref, qb), axis=0)
    nt_dims = (((1,), (1,)), ((), ()))

    def attend(ncols):
        s = lax.dot_general(qs, kc_ref[0:ncols, :], nt_dims, preferred_element_type=F32)
        qpos = qb * Q + (_iota((H * Q, ncols), 0) & (Q - 1))
        block_end = _iota((H * Q, ncols), 1) * NSA_CMP_STRIDE + (NSA_CMP_LEN - 1)
        s = jnp.where(block_end <= qpos, s, NEG_INF)
        e = jnp.exp2(s - jnp.max(s, axis=-1, keepdims=True))
        qpos_col = qb * Q + (_iota((H * Q, 1), 0) & (Q - 1))
        has_block = jnp.where(qpos_col >= NSA_CMP_LEN - 1, 1.0, 0.0)
        p = e * (has_block / jnp.sum(e, axis=-1, keepdims=True))
        o_c = jnp.dot(p.astype(BF16), vc_ref[0:ncols, :], preferred_element_type=F32)
        oc_ref[...] = _unstack_heads(_stacked_gate(jax.nn.sigmoid(gate_ref[...]), 0) * o_c)
        p_sum = p[0:Q]
        for h in range(1, H):
            p_sum = p_sum + p[h * Q:(h + 1) * Q]
        imp_ref[...] = _mm_f32(ovt_ref[:, 0:ncols], p_sum, exact="a", dims=nt_dims)

    tiles_needed = ((qb + 1) * Q // NSA_CMP_STRIDE + LANE - 1) // LANE
    for tiles in range(1, nc // LANE + 1):
        pl.when(tiles_needed == tiles)(functools.partial(attend, tiles * LANE))

    imp = imp_ref[...]
    blk = _iota((LANE, Q), 0)
    q_blk = (qb * Q + _iota((LANE, Q), 1)) >> int(math.log2(NSA_SLC_LEN))
    causal = blk <= q_blk
    for forced_blk in (0, q_blk, q_blk - 1):
        imp = jnp.where(blk == forced_blk, FORCED_SCORE, imp)
    imp = jnp.where(causal, imp, -1.0)
    imp = jnp.where(blk < n_slc, imp, -2.0)
    blk_f = blk.astype(F32)
    sel = jnp.zeros((LANE, Q), F32)
    for _ in range(top_n):
        m = jnp.max(imp, axis=0, keepdims=True)
        first = jnp.min(jnp.where(imp == m, blk_f, float(LANE)), axis=0, keepdims=True)
        hit = blk_f == first
        sel = jnp.where(hit, 1.0, sel)
        imp = jnp.where(hit, -3.0, imp)
    sel = jnp.where(causal, sel, 0.0).T
    selb_ref[...] = jnp.where(sel > 0.5, 0.0, NEG_INF).astype(BF16)
    any_ref[...] = jnp.max(sel, axis=0, keepdims=True)


def _nsa_select(p_nsa, qtab, kc, vc, overlap_t):
    B, S, _ = p_nsa.shape
    Q = NSA_Q
    nqb = S // Q
    nc = kc.shape[1]
    n_slc = S // NSA_SLC_LEN
    kern = functools.partial(_nsa_sel_kernel, n_slc=n_slc, top_n=min(NSA_TOPN, n_slc))
    return pl.pallas_call(
        kern, name="nsa_select",
        grid=(B, nqb),
        in_specs=[pl.BlockSpec((None, Q, NSA_HEADS * LANE), lambda b, i: (b, i, 0)),
                  pl.BlockSpec((2 * NSA_HEADS, LANE), lambda b, i: (0, 0)),
                  pl.BlockSpec((None, Q, LANE), lambda b, i: (b, i, (W_NSA - LANE) // LANE)),
                  pl.BlockSpec((None, nc, LANE), lambda b, i: (b, 0, 0)),
                  pl.BlockSpec((None, nc, LANE), lambda b, i: (b, 0, 0)),
                  pl.BlockSpec((LANE, nc), lambda b, i: (0, 0))],
        out_specs=[pl.BlockSpec((None, Q, GROUP_W), lambda b, i: (b, i, 0)),
                   pl.BlockSpec((None, Q, LANE), lambda b, i: (b, i, 0)),
                   pl.BlockSpec((None, None, 1, LANE), lambda b, i: (b, i, 0, 0))],
        out_shape=[jax.ShapeDtypeStruct((B, S, GROUP_W), F32),
                   jax.ShapeDtypeStruct((B, S, LANE), BF16),
                   jax.ShapeDtypeStruct((B, nqb, 1, LANE), F32)],
        scratch_shapes=[pltpu.VMEM((LANE, Q), F32)],
        compiler_params=_cparams("parallel", "parallel"),
    )(p_nsa, qtab, p_nsa, kc, vc, overlap_t)


def _nsa_attn_kernel(flags_ref, q_ref, qtab_ref, gate_ref, oc_ref, selb_ref, ks_ref, vs_ref, kw_ref, vw_ref,
                     o_ref, m_ref, acc_ref, ow_ref, *, nt):
    Q, H, TK = q_ref.shape[0], NSA_HEADS, NSA_TILE
    PART = 2 * Q
    b = pl.program_id(0)
    qb = pl.program_id(1)
    nqb = pl.num_programs(1)
    qh = _nsa_queries(q_ref, qtab_ref, qb)
    nt_dims = (((1,), (1,)), ((), ()))

    selb = selb_ref[...]
    qs_sel = jnp.concatenate([jnp.concatenate([q, selb], axis=1) for q in qh], axis=0)
    sink_off = jnp.where(_iota(selb.shape, 1) < NSA_SINK // NSA_SLC_LEN, NEG_INF, selb.astype(F32)).astype(BF16)
    qs_loop = jnp.concatenate([jnp.concatenate([q, sink_off], axis=1) for q in qh], axis=0)
    m_ref[...] = jnp.full(m_ref.shape, NEG_INF, F32)
    acc_ref[...] = jnp.zeros_like(acc_ref)

    def update(t, diagonal):
        rows = pl.ds(pl.multiple_of(t * TK, TK), TK)
        k_tile, v_tile = ks_ref[rows, :], vs_ref[rows, :]
        parts = [slice(i * PART, (i + 1) * PART) for i in range(H * Q // PART)]
        qs_t = qs_sel if diagonal else qs_loop
        scores = [lax.dot_general(qs_t[r], k_tile, nt_dims, preferred_element_type=F32) for r in parts]
        if diagonal:
            k_sink, v_sink = ks_ref[0:NSA_SINK, :], vs_ref[0:NSA_SINK, :]
            sink_scores = [lax.dot_general(qs_sel[r], k_sink, nt_dims, preferred_element_type=F32) for r in parts]
            sink_bias = jnp.where(t > 0, 0.0, NEG_INF)
        for idx, (r, s) in enumerate(zip(parts, scores)):
            m_prev = m_ref[r]
            if diagonal:
                ahead = _iota((PART, TK), 1) - (_iota((PART, TK), 0) & (Q - 1))
                s = jnp.where(ahead <= qb * Q - t * TK, s, NEG_INF)
                s_sink = sink_scores[idx] + sink_bias
                m_prev = jnp.maximum(m_prev, jnp.max(s_sink, axis=-1, keepdims=True))
            m_new = jnp.maximum(m_prev, jnp.max(s, axis=-1, keepdims=True))
            p = jnp.exp2(s - jnp.tile(m_new, (1, TK // LANE)))
            update_acc = jnp.dot(p.astype(BF16), v_tile, preferred_element_type=F32)
            if diagonal:
                p_sink = jnp.exp2(s_sink - jnp.tile(m_new, (1, NSA_SINK // LANE)))
                update_acc = update_acc + jnp.dot(p_sink.astype(BF16), v_sink, preferred_element_type=F32)
            acc_ref[r] = jnp.exp2(m_ref[r] - m_new) * acc_ref[r] + update_acc
            m_ref[r] = m_new

    def tile(t, carry):
        @pl.when(flags_ref[(b * nqb + qb) * nt + t] > 0)
        def _():
            update(t, False)
        return carry

    t_diag = (qb * Q) // TK
    lax.fori_loop(0, t_diag, tile, 0)
    update(t_diag, True)

    assert Q == NSA_WIN
    half = Q // 2

    def band(q_rows, first_key, n_keys, masks):
        keys = pl.ds(pl.multiple_of(first_key, half), n_keys)
        s = lax.dot_general(q_rows, kw_ref[keys, :], nt_dims, preferred_element_type=F32)
        groups = [s[:, g * half:(g + 1) * half] for g in range(n_keys // half)]
        s = jnp.concatenate([g if m is None else jnp.where(m, g, NEG_INF) for g, m in zip(groups, masks)], axis=1)
        p = jnp.exp2(s - jnp.max(s, axis=-1, keepdims=True))
        return _normalise(jnp.dot(p.astype(BF16), vw_ref[keys, :], preferred_element_type=F32))

    @pl.when(qb == 0)
    def _():
        qs_win = jnp.concatenate(qh, axis=0)
        row = _iota((H * Q, half), 0) & (Q - 1)
        col = _iota((H * Q, half), 1)
        ow_ref[...] = band(qs_win, 0, Q, [col <= row, col + half <= row])

    @pl.when(qb > 0)
    def _():
        row = _iota((H * half, half), 0) & (half - 1)
        col = _iota((H * half, half), 1)
        masks = [col > row, None, col <= row]
        for part in range(2):
            q_rows = jnp.concatenate([q[part * half:(part + 1) * half] for q in qh], axis=0)
            o_part = band(q_rows, (qb - 1) * Q + part * half, NSA_WIN + half, masks)
            for h in range(H):
                ow_ref[h * Q + part * half:h * Q + (part + 1) * half, :] = o_part[h * half:(h + 1) * half]

    gates = jax.nn.sigmoid(gate_ref[...])
    mixed = _stacked_gate(gates, 1) * _normalise(acc_ref[...]) + _stacked_gate(gates, 2) * ow_ref[...]
    o_ref[...] = (oc_ref[...] + _unstack_heads(mixed)).astype(o_ref.dtype)


def _nsa_attend(p_nsa, qtab, o_c, selb, flags, kv):
    B, S, _ = p_nsa.shape
    Q = NSA_Q
    nqb = S // Q
    nt = S // NSA_TILE
    gate_blk = (W_NSA - LANE) // LANE
    kern = functools.partial(_nsa_attn_kernel, nt=nt)
    slab = lambda width, col: pl.BlockSpec((None, S, width), lambda b, i, f: (b, 0, col))
    grid_spec = pltpu.PrefetchScalarGridSpec(
        num_scalar_prefetch=1,
        grid=(B, nqb),
        in_specs=[pl.BlockSpec((None, Q, NSA_HEADS * LANE), lambda b, i, f: (b, i, 0)),
                  pl.BlockSpec((2 * NSA_HEADS, LANE), lambda b, i, f: (0, 0)),
                  pl.BlockSpec((None, Q, LANE), lambda b, i, f: (b, i, gate_blk)),
                  pl.BlockSpec((None, Q, GROUP_W), lambda b, i, f: (b, i, 0)),
                  pl.BlockSpec((None, Q, LANE), lambda b, i, f: (b, i, 0)),
                  slab(2 * LANE, 0), slab(LANE, 2), slab(LANE, 3), slab(LANE, 4)],
        out_specs=pl.BlockSpec((None, Q, GROUP_W), lambda b, i, f: (b, i, 0)),
        scratch_shapes=[pltpu.VMEM((NSA_HEADS * Q, LANE), F32)] * 3,
    )
    return pl.pallas_call(
        kern, name="nsa_attend",
        grid_spec=grid_spec,
        out_shape=jax.ShapeDtypeStruct((B, S, GROUP_W), BF16),
        compiler_params=_cparams("parallel", "parallel"),
    )(flags, p_nsa, qtab, p_nsa, o_c, selb, kv, kv, kv, kv)


def _nsa_tables(S):
    nc = S // NSA_CMP_STRIDE
    n = np.arange(nc)[None, :]
    j = np.arange(LANE)[:, None]
    start = n * NSA_CMP_STRIDE
    ov = (start < (j + 1) * NSA_SLC_LEN) & (start + NSA_CMP_LEN - 1 >= j * NSA_SLC_LEN)
    ov &= (n < (S - NSA_CMP_LEN) // NSA_CMP_STRIDE + 1) & (j < S // NSA_SLC_LEN)
    pos = np.arange(S)
    k_zero = np.zeros((S, NSA_DK), np.float32)
    block_onehot = (pos[:, None] // NSA_SLC_LEN == np.arange(LANE)[None, :]).astype(np.float32)
    v_lanes = np.zeros((S, LANE), np.float32)
    v_lanes[:, ONES_LANE] = 1.0
    kv_table = np.concatenate([k_zero, _nsa_pos_lanes(pos), block_onehot, v_lanes,
                               k_zero, _nsa_pos_lanes(pos), v_lanes], axis=1)
    assert kv_table.shape[1] == W_KV
    return _nsa_query_table(), jnp.asarray(ov.astype(np.float32)), jnp.asarray(kv_table, dtype=BF16)


def _nsa(p_nsa, kv, uk, uv, pe_k, w1_k, w2_k, pe_v, w1_v, w2_v, tables):
    B, S, _ = p_nsa.shape
    qtab, overlap_t, _ = tables
    kc, vc = _nsa_compress(uk, uv, pe_k, w1_k, w2_k, pe_v, w1_v, w2_v)
    o_c, selb, blk_any = _nsa_select(p_nsa, qtab, kc, vc, overlap_t)
    per_tile = NSA_TILE // NSA_SLC_LEN
    nt = S // NSA_TILE
    not_sink = (np.arange(nt * per_tile) >= NSA_SINK // NSA_SLC_LEN).astype(np.float32)
    blk_any = blk_any[:, :, 0, :nt * per_tile] * not_sink
    flags = blk_any.reshape(B, S // NSA_Q, nt, per_tile).max(axis=-1)
    flags = (flags > 0).astype(jnp.int32).reshape(-1)
    return _nsa_attend(p_nsa, qtab, o_c, selb, flags, kv)


def _out_proj_kernel(h_ref, ya_ref, yb_ref, yc_ref, yd_ref, w_ref, g_ref, b_ref, o_ref, wb_ref):
    @pl.when(pl.program_id(0) == 0)
    def _():
        wb_ref[...] = w_ref[...].astype(BF16)

    mix = None
    for idx, y_ref in enumerate((ya_ref, yb_ref, yc_ref, yd_ref)):
        part = _mm(y_ref[...], wb_ref[idx * GROUP_W:(idx + 1) * GROUP_W, :])
        mix = part if mix is None else mix + part
    o_ref[...] = _layer_norm(DEEPNORM_ALPHA * h_ref[...] + mix, g_ref[...], b_ref[...])


def _out_proj(h2, ys, w_out, layer, g, b, tm=1024):
    T, D = h2.shape
    row = lambda w: pl.BlockSpec((tm, w), lambda i: (i, 0))
    const = lambda shape: pl.BlockSpec(shape, lambda i: (0,) * len(shape))
    return pl.pallas_call(
        _out_proj_kernel, name="out_proj_ln",
        grid=(T // tm,),
        in_specs=[row(D), row(GROUP_W), row(GROUP_W), row(GROUP_W), row(GROUP_W),
                  pl.BlockSpec((None, D, D), lambda i: (layer, 0, 0)), const((1, D)), const((1, D))],
        out_specs=row(D),
        out_shape=jax.ShapeDtypeStruct((T, D), F32),
        scratch_shapes=[pltpu.VMEM((D, D), BF16)],
        compiler_params=_cparams("arbitrary"),
    )(h2, *ys, w_out, g.reshape(1, D), b.reshape(1, D))


def _mlp_kernel(h_ref, w1_ref, w2_ref, g_ref, b_ref, o_ref, acc_ref):
    f = pl.program_id(1)

    @pl.when(f == 0)
    def _():
        acc_ref[...] = jnp.zeros_like(acc_ref)

    a = jnp.maximum(_mm(h_ref[...], w1_ref[...]), 0.0)
    acc_ref[...] += _mm(a * a, w2_ref[...])

    @pl.when(f == pl.num_programs(1) - 1)
    def _():
        o_ref[...] = _layer_norm(DEEPNORM_ALPHA * h_ref[...] + acc_ref[...], g_ref[...], b_ref[...])


def _mlp(h2, w1, w2, layer, g, b, tm=1024, tf=1024):
    T, D = h2.shape
    F = w1.shape[2]
    return pl.pallas_call(
        _mlp_kernel, name="mlp_ln",
        grid=(T // tm, F // tf),
        in_specs=[pl.BlockSpec((tm, D), lambda i, f: (i, 0)),
                  pl.BlockSpec((None, D, tf), lambda i, f: (layer, 0, f)),
                  pl.BlockSpec((None, tf, D), lambda i, f: (layer, f, 0)),
                  pl.BlockSpec((1, D), lambda i, f: (0, 0)),
                  pl.BlockSpec((1, D), lambda i, f: (0, 0))],
        out_specs=pl.BlockSpec((tm, D), lambda i, f: (i, 0)),
        out_shape=jax.ShapeDtypeStruct((T, D), F32),
        scratch_shapes=[pltpu.VMEM((tm, D), F32)],
        compiler_params=_cparams("parallel", "arbitrary"),
    )(h2, w1, w2, g.reshape(1, D), b.reshape(1, D))


def kernel(x, ln_emb_g, ln_emb_b, w_in, conv_w, conv_b, dt_bias, a_log, d_skip, ssm_norm_g, q_norm_g, w_uq, kv_norm_g, w_ukv, cmp_pe_k, cmp_w1_k, cmp_w2_k, cmp_pe_v, cmp_w1_v, cmp_w2_v, w_out, ln1_g, ln1_b, w_mlp1, w_mlp2, ln2_g, ln2_b):
    B, S, D = x.shape
    assert D == D_MODEL and S % NSA_TILE == 0 and S // NSA_SLC_LEN <= LANE
    T = B * S
    ret_tables = _ret_tables(S)
    mla_tables = _mla_tables(S)
    nsa_tables = _nsa_tables(S)
    h = x.reshape(T, D)
    for l in range(w_in.shape[0]):
        if l == 0:
            h, *proj = _in_proj(h, _layout_w_in(w_in[l]), nsa_tables[2], entry_ln=(ln_emb_g, ln_emb_b))
        else:
            proj = _in_proj(h, _layout_w_in(w_in[l]), nsa_tables[2])
        p_ssm, p_mla, p_ret, p_nsa, nsa_kv, uk, uv = proj
        cmp_rows = (B, S // NSA_CMP_STRIDE, uk.shape[-1])
        y_a, y_c = _ssm_and_retention(p_ssm.reshape(B, S, W_SSM), conv_w[l], conv_b[l], dt_bias[l], a_log[l],
                                      d_skip[l], ssm_norm_g[l], p_ret.reshape(B, S, W_RET), ret_tables)
        wq, wk, wv = _layout_mla_weights(w_uq[l], w_ukv[l])
        q, k, v = _mla_prep(p_mla.reshape(B, S, W_MLA), q_norm_g[l], kv_norm_g[l], wq, wk, wv, mla_tables)
        y_b = _mla_attn(q, k, v)
        y_d = _nsa(p_nsa.reshape(B, S, W_NSA), nsa_kv.reshape(B, S, W_KV), uk.reshape(cmp_rows), uv.reshape(cmp_rows),
                   cmp_pe_k[l], cmp_w1_k[l], cmp_w2_k[l], cmp_pe_v[l], cmp_w1_v[l], cmp_w2_v[l], nsa_tables)
        ys = [y.reshape(T, GROUP_W) for y in (y_a, y_b, y_c, y_d)]
        h = _out_proj(h, ys, w_out, l, ln1_g[l], ln1_b[l])
        h = _mlp(h, w_mlp1, w_mlp2, l, ln2_g[l], ln2_b[l])
    return h.reshape(B, S, D)
```

```python
import functools
import math

import jax
import jax.numpy as jnp
import numpy as np
from jax import lax
from jax.experimental import pallas as pl
from jax.experimental.pallas import tpu as pltpu

F32 = jnp.float32
BF16 = jnp.bfloat16

D_MODEL = 1024
DEPTH = 2
GROUP_W = D_MODEL // 4
SSM_HEADS = 4
SSM_HEAD_DIM = GROUP_W // SSM_HEADS
SSM_GROUPS = 2
SSM_STATE = 128
SSM_CONV = 4
SSM_CHUNK = 128
SSM_XBC = GROUP_W + 2 * SSM_GROUPS * SSM_STATE
MLA_HEADS = 4
MLA_NOPE = 64
MLA_ROPE = 32
MLA_V = GROUP_W // MLA_HEADS
MLA_Q_RANK = 256
MLA_KV_RANK = 128
RET_HEADS = 4
RET_DK = 64
RET_DV = GROUP_W // RET_HEADS
RET_CHUNK = 128
NSA_HEADS = 4
NSA_DK = 64
NSA_DV = GROUP_W // NSA_HEADS
NSA_CMP_LEN = 32
NSA_CMP_STRIDE = 16
NSA_CMP_HID = 256
NSA_SLC_LEN = 64
NSA_TOPN = 16
NSA_WIN = 512
D_FF = 4 * D_MODEL
NSA_Q = 512
ROPE_THETA = 10000.0
EPS = 1e-5
NEG_INF = -1e30
LOG2_E = math.log2(math.e)
FORCED_SCORE = 1e9
DEEPNORM_ALPHA = (2.0 * DEPTH) ** 0.25

IN_SPLITS = (
    GROUP_W, SSM_XBC, SSM_HEADS,
    MLA_Q_RANK, MLA_KV_RANK, MLA_ROPE,
    RET_HEADS * RET_DK, RET_HEADS * RET_DK, RET_HEADS * RET_DV, GROUP_W,
    NSA_HEADS * NSA_DK, NSA_DK, NSA_DV, NSA_DK, NSA_DV, NSA_DK, NSA_DV, 3 * NSA_HEADS,
)

LANE = 128
SUBLANE = 8
W_SSM = GROUP_W + SSM_XBC + LANE
W_MLA = MLA_Q_RANK + MLA_KV_RANK + LANE
W_RET = 4 * GROUP_W
W_NSA = NSA_HEADS * LANE + LANE + LANE
W_KV = 2 * LANE + 3 * LANE
W_PROJ = ((GROUP_W + SSM_XBC) + (MLA_Q_RANK + MLA_KV_RANK) + W_RET
          + (NSA_HEADS * NSA_DK + LANE) + 4 * NSA_DK + LANE)

NSA_TILE = 512
NSA_SINK = 128
MLA_TILE = 1024
VMEM_LIMIT = 48 * 1024 * 1024


def _cparams(*sem):
    return pltpu.CompilerParams(dimension_semantics=sem, vmem_limit_bytes=VMEM_LIMIT)


def _mm(a, b):
    return jnp.dot(a.astype(BF16), b.astype(BF16), preferred_element_type=F32)


def _mm_nt(a, b):
    return lax.dot_general(a.astype(BF16), b.astype(BF16), (((1,), (1,)), ((), ())),
                           preferred_element_type=F32)


def _split_f32(x):
    hi = x.astype(BF16)
    rest = x - hi.astype(F32)
    mid = rest.astype(BF16)
    lo = (rest - mid.astype(F32)).astype(BF16)
    return hi, mid, lo


def _mm_f32(a, b, exact, dims=(((1,), (0,)), ((), ()))):
    fixed, pieces = (a.astype(BF16), _split_f32(b)) if exact == "a" else (b.astype(BF16), _split_f32(a))
    out = None
    for piece in pieces:
        lhs, rhs = (fixed, piece) if exact == "a" else (piece, fixed)
        part = lax.dot_general(lhs, rhs, dims, preferred_element_type=F32)
        out = part if out is None else out + part
    return out


def _silu(x):
    return x * jax.nn.sigmoid(x)


def _softplus(x):
    return jnp.maximum(x, 0.0) + jnp.log1p(jnp.exp(-jnp.abs(x)))


def _layer_norm(x, g, b):
    mu = jnp.mean(x, axis=-1, keepdims=True)
    xc = x - mu
    var = jnp.mean(xc * xc, axis=-1, keepdims=True)
    return xc * lax.rsqrt(var + EPS) * g + b


def _iota(shape, dim):
    return lax.broadcasted_iota(jnp.int32, shape, dim)


def _pad_cols(w, width):
    return jnp.pad(w, ((0, 0), (0, width - w.shape[1])))


def _layout_w_in(w):
    offs = np.concatenate([[0], np.cumsum(IN_SPLITS)])
    p = [w[:, int(offs[i]):int(offs[i + 1])] for i in range(len(IN_SPLITS))]
    (ssm_z, ssm_xbc, ssm_dt, mla_cq, mla_ckv, mla_kr, ret_q, ret_k, ret_v, ret_g,
     nsa_q, nsa_kc, nsa_vc, nsa_ks, nsa_vs, nsa_kw, nsa_vw, nsa_gate) = p
    small = _pad_cols(jnp.concatenate([mla_kr, ssm_dt, nsa_gate], axis=1), LANE)
    cols = [ssm_z, ssm_xbc, ret_q, ret_k, ret_v, ret_g, nsa_q, nsa_ks, nsa_vs, nsa_kw, nsa_vw,
            mla_cq, mla_ckv, nsa_kc, nsa_vc, small]
    out = jnp.concatenate(cols, axis=1)
    assert out.shape[1] == W_PROJ
    return out.astype(BF16)


def _in_proj_kernel(*refs, entry_ln):
    if entry_ln:
        h_ref, g_ref, b_ref, w_ref, kvtab_ref, hn_ref, *outs = refs
        hn = _layer_norm(h_ref[...], g_ref[...], b_ref[...])
        hn_ref[...] = hn
    else:
        h_ref, w_ref, kvtab_ref, *outs = refs
        hn = h_ref[...]
    ssm_ref, mla_ref, ret_ref, nsa_ref, kv_ref, uk_ref, uv_ref, kcv_ref = outs
    hb = hn.astype(BF16)
    tm = hb.shape[0]
    off = 0

    def project(width):
        nonlocal off
        out = jnp.dot(hb, w_ref[:, off:off + width], preferred_element_type=F32)
        off += width
        return out

    ssm_ref[:, 0:GROUP_W + SSM_XBC] = project(GROUP_W + SSM_XBC)
    ret_ref[...] = project(W_RET)
    q = project(NSA_HEADS * NSA_DK)
    kv = project(4 * NSA_DK)
    latent = MLA_Q_RANK + MLA_KV_RANK
    mla_kc = project(latent + LANE)
    mla_ref[:, 0:latent] = mla_kc[:, 0:latent]
    kc_lane = NSA_HEADS * LANE
    nsa_ref[:, kc_lane:kc_lane + LANE] = mla_kc[:, latent:]
    small = project(LANE)
    lane = _iota((tm, LANE), 1)
    low = lane < NSA_DK
    for h in range(NSA_HEADS):
        pair = q[:, (h // 2) * LANE:(h // 2 + 1) * LANE]
        head = pair if h % 2 == 0 else pltpu.roll(pair, NSA_DK, 1)
        nsa_ref[:, h * LANE:(h + 1) * LANE] = jnp.where(low, head, 0.0)
    mla_ref[:, MLA_Q_RANK + MLA_KV_RANK:] = jnp.where(lane < MLA_ROPE, small, 0.0)
    ssm_ref[:, GROUP_W + SSM_XBC:] = jnp.where(lane < SSM_HEADS, pltpu.roll(small, LANE - MLA_ROPE, 1), 0.0)
    gate_at = MLA_ROPE + SSM_HEADS
    nsa_ref[:, kc_lane + LANE:] = jnp.where(lane < 3 * NSA_HEADS, pltpu.roll(small, LANE - gate_at, 1), 0.0)
    groups = h_ref.shape[0] // NSA_CMP_STRIDE
    kc_lane = NSA_HEADS * LANE
    kcv_ref[...] = nsa_ref[:, kc_lane:kc_lane + LANE]
    for t in range(NSA_CMP_STRIDE):
        piece = kcv_ref[pl.ds(t, groups, stride=NSA_CMP_STRIDE), :]
        uk_ref[:, t * NSA_DK:(t + 1) * NSA_DK] = piece[:, :NSA_DK]
        uv_ref[:, t * NSA_DV:(t + 1) * NSA_DV] = piece[:, NSA_DK:]
    sel_kv, win_kv = kv[:, :LANE], kv[:, LANE:]
    pieces = {0: sel_kv, 2: pltpu.roll(sel_kv, NSA_DK, 1), 3: win_kv, 4: pltpu.roll(win_kv, NSA_DK, 1)}
    for slab in range(W_KV // LANE):
        lanes = slice(slab * LANE, (slab + 1) * LANE)
        tab = kvtab_ref[:, lanes]
        if slab in pieces:
            kv_ref[:, lanes] = (jnp.where(low, pieces[slab], 0.0) + tab.astype(F32)).astype(BF16)
        else:
            kv_ref[:, lanes] = tab


def _in_proj(h2, w_p, kv_table, entry_ln=None, tm=512):
    T, D = h2.shape
    S = kv_table.shape[0]
    widths = (W_SSM, W_MLA, W_RET, W_NSA)
    half = NSA_CMP_STRIDE * NSA_DK
    row = lambda w: pl.BlockSpec((tm, w), lambda i: (i, 0))
    const = lambda shape: pl.BlockSpec(shape, lambda i: (0,) * len(shape))
    in_specs = [const((D, W_PROJ)), pl.BlockSpec((tm, W_KV), lambda i: (i % (S // tm), 0))]
    out_specs = [row(w) for w in widths + (W_KV,)] + [pl.BlockSpec((tm // NSA_CMP_STRIDE, half), lambda i: (i, 0))] * 2
    out_shape = ([jax.ShapeDtypeStruct((T, w), F32) for w in widths] + [jax.ShapeDtypeStruct((T, W_KV), BF16)]
                 + [jax.ShapeDtypeStruct((T // NSA_CMP_STRIDE, half), F32)] * 2)
    operands = (w_p, kv_table)
    if entry_ln is not None:
        in_specs = [const((1, D)), const((1, D))] + in_specs
        out_specs = [row(D)] + out_specs
        out_shape = [jax.ShapeDtypeStruct((T, D), F32)] + out_shape
        operands = tuple(v.reshape(1, D) for v in entry_ln) + operands
    return pl.pallas_call(
        functools.partial(_in_proj_kernel, entry_ln=entry_ln is not None), name="in_proj",
        grid=(T // tm,),
        in_specs=[row(D)] + in_specs,
        out_specs=out_specs,
        out_shape=out_shape,
        scratch_shapes=[pltpu.VMEM((tm, LANE), F32)],
        compiler_params=_cparams("parallel"),
    )(h2, *operands)


def _ssm_chunk(p_ref, cw_ref, cb_ref, dtb_ref, alog_ref, dskip_ref, ng_ref, o_ref, state_ref, ext_ref):
    L, H, P, N = SSM_CHUNK, SSM_HEADS, SSM_HEAD_DIM, SSM_STATE
    z = p_ref[:, 0:GROUP_W]
    ext_ref[SUBLANE:SUBLANE + L, :] = p_ref[:, GROUP_W:GROUP_W + SSM_XBC]
    ext = ext_ref[...]
    conv = cb_ref[...] + ext[SUBLANE:] * cw_ref[SSM_CONV - 1:SSM_CONV, :]
    for j in range(SSM_CONV - 1):
        shift = SSM_CONV - 1 - j
        conv = conv + pltpu.roll(ext, shift, 0)[SUBLANE:] * cw_ref[j:j + 1, :]
    ext_ref[0:SUBLANE, :] = ext_ref[L:L + SUBLANE, :]
    xbc = _silu(conv)
    xs = xbc[:, 0:GROUP_W]
    b_in = xbc[:, GROUP_W:GROUP_W + SSM_GROUPS * N]
    c_in = xbc[:, GROUP_W + SSM_GROUPS * N:]

    dt = _softplus(p_ref[:, GROUP_W + SSM_XBC:] + dtb_ref[...])
    a = dt * (-jnp.exp(alog_ref[...]))
    row = _iota((L, L), 0)
    col = _iota((L, L), 1)
    tril = col <= row
    cs = _mm_f32(jnp.where(tril, 1.0, 0.0), a, exact="a")
    cs_t = cs.T
    ecs = jnp.exp(cs)
    dte = jnp.exp(cs[L - 1:L, :] - cs)
    first_head_of_pair = _iota((L, LANE), 1) < P

    def expand(x):
        pairs = [jnp.where(first_head_of_pair, x[:, h:h + 1], x[:, h + 1:h + 2]) for h in range(0, H, LANE // P)]
        return jnp.concatenate(pairs, axis=1)

    assert LANE == 2 * P
    dt_x = expand(dt)
    ecs_x = expand(ecs)
    dte_x = expand(dte)

    xdt = xs * dt_x
    wx = xdt * dte_x
    head_of_lane = _iota((L, H * P), 1) // P
    y = xs * dskip_ref[...]
    y_off = []
    rep = H // SSM_GROUPS
    for g in range(SSM_GROUPS):
        cg = c_in[:, g * N:(g + 1) * N]
        bg = b_in[:, g * N:(g + 1) * N]
        cb = _mm_nt(cg, bg)
        for h in range(g * rep, (g + 1) * rep):
            diff = cs[:, h:h + 1] - cs_t[h:h + 1, :]
            seg = jnp.where(tril, jnp.exp(jnp.where(tril, diff, 0.0)), 0.0)
            yh = _mm(cb * seg, xdt)
            y = y + jnp.where(head_of_lane == h, yh, 0.0)
        lanes = slice(g * rep * P, (g + 1) * rep * P)
        st_prev = state_ref[:, lanes]
        y_off.append(_mm(cg, st_prev))
        state_ref[:, lanes] = st_prev * ecs_x[L - 1:L, lanes] + _mm(bg.T, wx[:, lanes])
    y = y + jnp.concatenate(y_off, axis=1) * ecs_x
    y = y * _silu(z)
    ms = jnp.mean(y * y, axis=-1, keepdims=True)
    o_ref[...] = (y * lax.rsqrt(ms + EPS) * ng_ref[...]).astype(o_ref.dtype)


def _recurrent_kernel(ps_ref, cw_ref, cb_ref, dtb_ref, alog_ref, dskip_ref, ng_ref,
                      pr_ref, cos_ref, sin_ref, dec_ref, zeta_ref, xi_ref, cd_ref,
                      oa_ref, oc_ref, sstate_ref, ext_ref, rstate_ref):
    @pl.when(pl.program_id(0) == 0)
    def _():
        sstate_ref[...] = jnp.zeros_like(sstate_ref)
        rstate_ref[...] = jnp.zeros_like(rstate_ref)
        ext_ref[:, 0:SUBLANE, :] = jnp.zeros((ext_ref.shape[0], SUBLANE, SSM_XBC), F32)

    for b in range(ps_ref.shape[0]):
        _ssm_chunk(ps_ref.at[b], cw_ref, cb_ref, dtb_ref, alog_ref, dskip_ref, ng_ref,
                   oa_ref.at[b], sstate_ref.at[b], ext_ref.at[b])
        _ret_chunk(pr_ref.at[b], cos_ref, sin_ref, dec_ref, zeta_ref, xi_ref, cd_ref, oc_ref.at[b], rstate_ref.at[b])


def _ret_chunk(p_ref, cos_ref, sin_ref, dec_ref, zeta_ref, xi_ref, cd_ref, o_ref, state_ref):
    L, H, DK, DV = RET_CHUNK, RET_HEADS, RET_DK, RET_DV
    W = H * DK
    q = p_ref[:, 0:W]
    k = p_ref[:, W:2 * W]
    v = p_ref[:, 2 * W:3 * W]
    gate = p_ref[:, 3 * W:4 * W]
    lane = _iota((L, W), 1)
    first_half = (lane % DK) < (DK // 2)
    head_of_lane = lane // DK

    def rope(x):
        partner = jnp.where(first_half, pltpu.roll(x, W - DK // 2, 1), pltpu.roll(x, DK // 2, 1))
        return x * cos_ref[...] + partner * sin_ref[...]

    qr = rope(q)
    kr = rope(k) * (DK ** -0.5)
    y = jnp.zeros((L, H * DV), F32)
    for h in range(H):
        qh = jnp.where(head_of_lane == h, qr, 0.0)
        sc = _mm_nt(qh, kr) * dec_ref[h]
        y = y + jnp.where(head_of_lane == h, _mm(sc, v), 0.0)
    st = state_ref[...]
    y = y + _mm(qr * xi_ref[...], st)
    same_head = (_iota((W, H * DV), 0) // DK) == (_iota((W, H * DV), 1) // DV)
    kv = _mm((kr * zeta_ref[...]).T, v)
    state_ref[...] = st * cd_ref[...] + jnp.where(same_head, kv, 0.0)
    assert DV & (DV - 1) == 0
    ms = _mm_f32(y * y, jnp.where(same_head, 1.0 / DV, 0.0), exact="b")
    o_ref[...] = (y * lax.rsqrt(ms + EPS) * _silu(gate)).astype(o_ref.dtype)


def _ret_tables(S):
    H, DK, L = RET_HEADS, RET_DK, RET_CHUNK
    inv = ROPE_THETA ** (-np.arange(0, DK, 2, dtype=np.float64) / DK)
    ang = np.arange(S, dtype=np.float64)[:, None] * inv[None, :]
    cos, sin = np.cos(ang), np.sin(ang)
    cos_t = np.tile(np.concatenate([cos, cos], axis=1), (1, H))
    sin_t = np.tile(np.concatenate([-sin, sin], axis=1), (1, H))
    log_gamma = np.log1p(-np.exp2(-5.0 - np.arange(H, dtype=np.float64)))
    pos = np.arange(L, dtype=np.float64)
    diff = pos[:, None] - pos[None, :]
    decay_in = np.where(diff >= 0, np.exp(np.maximum(diff, 0.0)[None] * log_gamma[:, None, None]), 0.0)
    zeta = np.exp((L - 1 - pos)[None] * log_gamma[:, None])
    xi = np.exp((pos + 1.0)[None] * log_gamma[:, None])
    chunk_decay = np.exp(L * log_gamma)
    zeta_x = np.repeat(zeta.T, DK, axis=1)
    xi_x = np.repeat(xi.T, DK, axis=1)
    cd_x = np.repeat(chunk_decay, RET_DV).reshape(1, H * RET_DV)
    return tuple(jnp.asarray(t, dtype=F32) for t in (cos_t, sin_t, decay_in, zeta_x, xi_x, cd_x))


def _ssm_and_retention(p_ssm, conv_w, conv_b, dt_bias, a_log, d_skip, norm_g, p_ret, tables):
    B, S, _ = p_ssm.shape
    L, H = SSM_CHUNK, RET_HEADS
    assert RET_CHUNK == L
    W = H * RET_DK
    cos_t, sin_t, decay_in, zeta_x, xi_x, cd_x = tables
    pad_h = lambda v: jnp.pad(v, (0, LANE - SSM_HEADS)).reshape(1, LANE)
    const = lambda shape: pl.BlockSpec(shape, lambda c: (0,) * len(shape))
    chunk = lambda width: pl.BlockSpec((B, L, width), lambda c: (0, c, 0))
    out_shape = jax.ShapeDtypeStruct((B, S, GROUP_W), BF16)
    return pl.pallas_call(
        _recurrent_kernel, name="ssm_retention",
        grid=(S // L,),
        in_specs=[chunk(W_SSM),
                  const((SSM_CONV, SSM_XBC)), const((1, SSM_XBC)), const((1, LANE)), const((1, LANE)),
                  const((1, GROUP_W)), const((1, GROUP_W)),
                  chunk(W_RET),
                  pl.BlockSpec((L, W), lambda c: (c, 0)), pl.BlockSpec((L, W), lambda c: (c, 0)),
                  const((H, L, L)), const((L, W)), const((L, W)), const((1, H * RET_DV))],
        out_specs=[chunk(GROUP_W), chunk(GROUP_W)],
        out_shape=[out_shape, out_shape],
        scratch_shapes=[pltpu.VMEM((B, SSM_STATE, GROUP_W), F32),
                        pltpu.VMEM((B, L + SUBLANE, SSM_XBC), F32),
                        pltpu.VMEM((B, W, H * RET_DV), F32)],
        compiler_params=_cparams("arbitrary"),
    )(p_ssm, conv_w, conv_b.reshape(1, -1), pad_h(dt_bias), pad_h(a_log),
      jnp.repeat(d_skip, SSM_HEAD_DIM).reshape(1, GROUP_W), norm_g.reshape(1, GROUP_W),
      p_ret, cos_t, sin_t, decay_in, zeta_x, xi_x, cd_x)


def _mla_prep_kernel(p_ref, qg_ref, wq_ref, kvg_ref, wk_ref, wv_ref, cos_ref, sin_ref,
                     q_ref, k_ref, v_ref):
    tm = p_ref.shape[0]
    cq = p_ref[:, 0:MLA_Q_RANK]
    ckv = p_ref[:, MLA_Q_RANK:MLA_Q_RANK + MLA_KV_RANK]
    kr = p_ref[:, MLA_Q_RANK + MLA_KV_RANK:]

    def rms(x, g):
        return x * lax.rsqrt(jnp.mean(x * x, axis=-1, keepdims=True) + EPS) * g

    q = _mm(rms(cq, qg_ref[...]), wq_ref[...])
    kvl = rms(ckv, kvg_ref[...])
    kn = _mm(kvl, wk_ref[...])
    vv = _mm(kvl, wv_ref[...])
    kr_sh = pltpu.roll(kr, MLA_NOPE, 1)
    lane = _iota((tm, LANE), 1)
    half = MLA_ROPE // 2
    low = (lane >= MLA_NOPE) & (lane < MLA_NOPE + half)
    cos = cos_ref[...]
    sin = sin_ref[...]

    def rope(x):
        partner = jnp.where(low, pltpu.roll(x, LANE - half, 1), pltpu.roll(x, half, 1))
        return x * cos + partner * sin

    scale = (MLA_NOPE + MLA_ROPE) ** -0.5 * LOG2_E
    for h in range(MLA_HEADS):
        sl = slice(h * LANE, (h + 1) * LANE)
        q_ref[h] = (rope(q[:, sl]) * scale).astype(BF16)
        k_ref[h] = rope(kn[:, sl] + kr_sh).astype(BF16)
        v_ref[h] = jnp.where(lane == _mla_ones_lane(h), 1.0, vv[:, sl]).astype(BF16)


def _mla_tables(S):
    inv = ROPE_THETA ** (-np.arange(0, MLA_ROPE, 2, dtype=np.float64) / MLA_ROPE)
    ang = np.arange(S, dtype=np.float64)[:, None] * inv[None, :]
    cos, sin = np.cos(ang), np.sin(ang)
    tail = LANE - MLA_NOPE - MLA_ROPE
    cos_t = np.concatenate([np.ones((S, MLA_NOPE)), cos, cos, np.ones((S, tail))], axis=1)
    sin_t = np.concatenate([np.zeros((S, MLA_NOPE)), -sin, sin, np.zeros((S, tail))], axis=1)
    return jnp.asarray(cos_t, dtype=F32), jnp.asarray(sin_t, dtype=F32)


def _layout_mla_weights(w_uq, w_ukv):
    H = MLA_HEADS
    dq = MLA_NOPE + MLA_ROPE
    wq = jnp.concatenate([_pad_cols(w_uq[:, h * dq:(h + 1) * dq], LANE) for h in range(H)], axis=1)
    dkv = MLA_NOPE + MLA_V
    wk, wv = [], []
    for h in range(H):
        blk = w_ukv[:, h * dkv:(h + 1) * dkv]
        wk.append(_pad_cols(blk[:, :MLA_NOPE], LANE))
        v = blk[:, MLA_NOPE:]
        zero = jnp.zeros_like(v)
        wv.append(jnp.concatenate([v, zero] if h % 2 == 0 else [zero, v], axis=1))
    return wq.astype(BF16), jnp.concatenate(wk, axis=1).astype(BF16), jnp.concatenate(wv, axis=1).astype(BF16)


def _mla_prep(p_mla, q_norm_g, kv_norm_g, wq, wk, wv, tables, tm=1024):
    B, S, _ = p_mla.shape
    H = MLA_HEADS
    cos_t, sin_t = tables
    const = lambda shape: pl.BlockSpec(shape, lambda b, i: (0,) * len(shape))
    qkv_spec = pl.BlockSpec((None, H, tm, LANE), lambda b, i: (b, 0, i, 0))
    qkv_shape = jax.ShapeDtypeStruct((B, H, S, LANE), BF16)
    return pl.pallas_call(
        _mla_prep_kernel, name="mla_prep",
        grid=(B, S // tm),
        in_specs=[pl.BlockSpec((None, tm, W_MLA), lambda b, i: (b, i, 0)),
                  const((1, MLA_Q_RANK)), const((MLA_Q_RANK, H * LANE)),
                  const((1, MLA_KV_RANK)), const((MLA_KV_RANK, H * LANE)), const((MLA_KV_RANK, H * LANE)),
                  pl.BlockSpec((tm, LANE), lambda b, i: (i, 0)),
                  pl.BlockSpec((tm, LANE), lambda b, i: (i, 0))],
        out_specs=[qkv_spec, qkv_spec, qkv_spec],
        out_shape=[qkv_shape, qkv_shape, qkv_shape],
        compiler_params=_cparams("parallel", "parallel"),
    )(p_mla, q_norm_g.reshape(1, -1), wq, kv_norm_g.reshape(1, -1), wk, wv, cos_t, sin_t)


def _mla_ones_lane(h):
    return MLA_V if h % 2 == 0 else 0


def _mla_attn_kernel(qi_ref, kj_ref, q_ref, k_ref, v_ref, o_ref, m_ref, acc_ref):
    H = MLA_HEADS
    tq, tk = q_ref.shape[1], k_ref.shape[1]
    i = qi_ref[pl.program_id(1)]
    j = kj_ref[pl.program_id(1)]

    @pl.when(j == 0)
    def _():
        m_ref[...] = jnp.full(m_ref.shape, NEG_INF, F32)
        acc_ref[...] = jnp.zeros_like(acc_ref)

    def sweep(blocks):
        nt_dims = (((1,), (1,)), ((), ()))
        scores = [[lax.dot_general(q_ref[h, r0:r0 + nr, :], k_ref[h, 0:nk, :], nt_dims, preferred_element_type=F32)
                   for (r0, nr, nk, _) in blocks] for h in range(H)]
        for h in range(H):
            for (r0, nr, nk, offset), s in zip(blocks, scores[h]):
                if offset is not None:
                    s = jnp.where(_iota((nr, nk), 1) - _iota((nr, nk), 0) <= offset, s, NEG_INF)
                rows = slice(r0, r0 + nr)
                m_prev = m_ref[h, rows]
                m_new = jnp.maximum(m_prev, jnp.max(s, axis=-1, keepdims=True))
                p = jnp.exp2(s - jnp.tile(m_new, (1, nk // LANE)))
                acc_ref[h, rows] = (jnp.exp2(m_prev - m_new) * acc_ref[h, rows]
                                    + jnp.dot(p.astype(BF16), v_ref[h, 0:nk, :], preferred_element_type=F32))
                m_ref[h, rows] = m_new

    assert tq == tk
    half = tq // 2

    @pl.when(j < i)
    def _():
        sweep([(0, tq, tk, None)])

    @pl.when(j == i)
    def _():
        sweep([(0, half, half, 0), (half, half, tk, half)])
        lane = _iota((tq, LANE), 1)
        for pair in range(H // 2):
            he, ho = 2 * pair, 2 * pair + 1
            acc_e, acc_o = acc_ref[he], acc_ref[ho]
            le = acc_e[:, _mla_ones_lane(he):_mla_ones_lane(he) + 1]
            lo = acc_o[:, _mla_ones_lane(ho):_mla_ones_lane(ho) + 1]
            o_ref[:, pair * LANE:(pair + 1) * LANE] = jnp.where(lane < MLA_V, acc_e / le, acc_o / lo).astype(o_ref.dtype)


def _mla_attn(q, k, v):
    B, H, S, _ = q.shape
    t = min(MLA_TILE, S)
    tq = t
    pairs = [(i, j) for i in range(S // tq) for j in range((i + 1) * tq // t)]
    qi = jnp.asarray([p[0] for p in pairs], jnp.int32)
    kj = jnp.asarray([p[1] for p in pairs], jnp.int32)
    grid_spec = pltpu.PrefetchScalarGridSpec(
        num_scalar_prefetch=2,
        grid=(B, len(pairs)),
        in_specs=[pl.BlockSpec((None, H, tq, LANE), lambda b, p, qi, kj: (b, 0, qi[p], 0)),
                  pl.BlockSpec((None, H, t, LANE), lambda b, p, qi, kj: (b, 0, kj[p], 0)),
                  pl.BlockSpec((None, H, t, LANE), lambda b, p, qi, kj: (b, 0, kj[p], 0))],
        out_specs=pl.BlockSpec((None, tq, GROUP_W), lambda b, p, qi, kj: (b, qi[p], 0)),
        scratch_shapes=[pltpu.VMEM((H, tq, LANE), F32), pltpu.VMEM((H, tq, LANE), F32)],
    )
    return pl.pallas_call(
        _mla_attn_kernel, name="mla_attn",
        grid_spec=grid_spec,
        out_shape=jax.ShapeDtypeStruct((B, S, GROUP_W), BF16),
        compiler_params=_cparams("parallel", "arbitrary"),
    )(qi, kj, q, k, v)


POS_HI = NSA_DK
POS_LO = NSA_DK + 3
POS_ONE = NSA_DK + 6
ONES_LANE = NSA_DV


def _split_bf16(x, parts=3):
    out, rem = [], np.float64(x)
    for _ in range(parts):
        piece = np.float64(np.float32(rem).astype(jnp.bfloat16).astype(np.float32))
        out.append(float(piece))
        rem = rem - piece
    return out


def _nsa_query_table():
    H = NSA_HEADS
    tab = np.zeros((2 * H, LANE), np.float32)
    for h in range(H):
        c = 2.0 ** (-8.0 * (h + 1) / H) * LOG2_E
        pieces = _split_bf16(c)
        tab[h, POS_HI:POS_HI + 3] = pieces
        tab[h, POS_LO:POS_LO + 3] = pieces
        tab[H + h, POS_ONE] = -sum(pieces)
    return jnp.asarray(tab)


def _nsa_pos_lanes(pos, lo_offset=0.0):
    t = np.zeros((len(pos), LANE - NSA_DK), np.float32)
    t[:, POS_HI - NSA_DK:POS_HI - NSA_DK + 3] = (NSA_SLC_LEN * (pos // NSA_SLC_LEN))[:, None]
    t[:, POS_LO - NSA_DK:POS_LO - NSA_DK + 3] = (pos % NSA_SLC_LEN + lo_offset)[:, None]
    t[:, POS_ONE - NSA_DK] = 1.0
    return t


def _nsa_queries(q_ref, qtab_ref, qb):
    Q, H = q_ref.shape[0], NSA_HEADS
    qpos = (qb * Q + _iota((Q, 1), 0)).astype(F32)
    out = []
    for h in range(H):
        q = q_ref[:, h * LANE:(h + 1) * LANE] * (NSA_DK ** -0.5 * LOG2_E)
        out.append((q + qtab_ref[h:h + 1, :] + qtab_ref[H + h:H + h + 1, :] * qpos).astype(BF16))
    return out


def _normalise(o):
    return o / o[:, ONES_LANE:ONES_LANE + 1]


def _stacked_gate(gates, branch):
    lanes = [3 * h + branch for h in range(NSA_HEADS)]
    return jnp.concatenate([gates[:, c:c + 1] for c in lanes], axis=0)


def _unstack_heads(o):
    Q = o.shape[0] // NSA_HEADS
    lane = _iota((Q, LANE), 1)
    out = []
    for pair in range(NSA_HEADS // 2):
        even = o[(2 * pair) * Q:(2 * pair + 1) * Q]
        odd = o[(2 * pair + 1) * Q:(2 * pair + 2) * Q]
        out.append(jnp.where(lane < NSA_DV, even, pltpu.roll(odd, NSA_DV, 1)))
    return jnp.concatenate(out, axis=1)


def _nsa_cmp_kernel(uk_ref, uv_ref, pek_ref, pev_ref, w1k_ref, w1v_ref, w2k_ref, w2v_ref, cpos_ref,
                    kc_ref, vc_ref, sh_ref):
    nb = uk_ref.shape[0]
    half = uk_ref.shape[1]

    def hidden(u_ref, pe_ref, w1_ref):
        u = u_ref[...]
        first = _mm(u + pe_ref[0:1, :], w1_ref[0:half, :])
        second = _mm(u + pe_ref[1:2, :], w1_ref[half:2 * half, :])
        sh_ref[0:nb, :] = second
        sh_ref[nb:nb + SUBLANE, :] = jnp.zeros((SUBLANE, NSA_CMP_HID), F32)
        return first + sh_ref[pl.ds(1, nb), :]

    hk = _silu(hidden(uk_ref, pek_ref, w1k_ref))
    hv = _silu(hidden(uv_ref, pev_ref, w1v_ref))
    kc_ref[...] = (_mm(hk, w2k_ref[...]) + cpos_ref[...]).astype(BF16)
    ones_lane = jnp.where(_iota((1, LANE), 1) == ONES_LANE, 1.0, 0.0)
    vc_ref[...] = (_mm(hv, w2v_ref[...]) + ones_lane).astype(BF16)


def _nsa_compress(uk, uv, pe_k, w1_k, w2_k, pe_v, w1_v, w2_v):
    B, nb, half = uk.shape
    hid = NSA_CMP_HID
    const = lambda shape: pl.BlockSpec(shape, lambda b: (0,) * len(shape))
    w2k = _pad_cols(w2_k, LANE).astype(BF16)
    w2v = _pad_cols(w2_v, LANE).astype(BF16)
    centre = _nsa_pos_lanes(np.arange(nb) * NSA_CMP_STRIDE, 0.5 * (NSA_CMP_LEN - 1))
    cpos = jnp.asarray(np.concatenate([np.zeros((nb, NSA_DK), np.float32), centre], axis=1))
    out_spec = pl.BlockSpec((None, nb, LANE), lambda b: (b, 0, 0))
    out_shape = jax.ShapeDtypeStruct((B, nb, LANE), BF16)
    return pl.pallas_call(
        _nsa_cmp_kernel, name="nsa_compress",
        grid=(B,),
        in_specs=[pl.BlockSpec((None, nb, half), lambda b: (b, 0, 0)),
                  pl.BlockSpec((None, nb, half), lambda b: (b, 0, 0)),
                  const((2, half)), const((2, half)),
                  const((2 * half, hid)), const((2 * half, hid)),
                  const((hid, LANE)), const((hid, LANE)), const((nb, LANE))],
        out_specs=[out_spec, out_spec],
        out_shape=[out_shape, out_shape],
        scratch_shapes=[pltpu.VMEM((nb + SUBLANE, hid), F32)],
        compiler_params=_cparams("parallel"),
    )(uk, uv, pe_k.reshape(2, half), pe_v.reshape(2, half), w1_k.astype(BF16), w1_v.astype(BF16), w2k, w2v, cpos)


def _nsa_sel_kernel(q_ref, qtab_ref, gate_ref, kc_ref, vc_ref, ovt_ref, oc_ref, selb_ref, any_ref, imp_ref, *, n_slc, top_n):
    Q, H = q_ref.shape[0], NSA_HEADS
    qb = pl.program_id(1)
    nc = kc_ref.shape[0]
    qs = jnp.concatenate(_nsa_queries(q_ref, qtab_ref, qb), axis=0)
    nt_dims = (((1,), (1,)), ((), ()))

    def attend(ncols):
        s = lax.dot_general(qs, kc_ref[0:ncols, :], nt_dims, preferred_element_type=F32)
        qpos = qb * Q + (_iota((H * Q, ncols), 0) & (Q - 1))
        block_end = _iota((H * Q, ncols), 1) * NSA_CMP_STRIDE + (NSA_CMP_LEN - 1)
        s = jnp.where(block_end <= qpos, s, NEG_INF)
        e = jnp.exp2(s - jnp.max(s, axis=-1, keepdims=True))
        qpos_col = qb * Q + (_iota((H * Q, 1), 0) & (Q - 1))
        has_block = jnp.where(qpos_col >= NSA_CMP_LEN - 1, 1.0, 0.0)
        p = e * (has_block / jnp.sum(e, axis=-1, keepdims=True))
        o_c = jnp.dot(p.astype(BF16), vc_ref[0:ncols, :], preferred_element_type=F32)
        oc_ref[...] = _unstack_heads(_stacked_gate(jax.nn.sigmoid(gate_ref[...]), 0) * o_c)
        p_sum = p[0:Q]
        for h in range(1, H):
            p_sum = p_sum + p[h * Q:(h + 1) * Q]
        imp_ref[...] = _mm_f32(ovt_ref[:, 0:ncols], p_sum, exact="a", dims=nt_dims)

    tiles_needed = ((qb + 1) * Q // NSA_CMP_STRIDE + LANE - 1) // LANE
    for tiles in range(1, nc // LANE + 1):
        pl.when(tiles_needed == tiles)(functools.partial(attend, tiles * LANE))

    imp = imp_ref[...]
    blk = _iota((LANE, Q), 0)
    q_blk = (qb * Q + _iota((LANE, Q), 1)) >> int(math.log2(NSA_SLC_LEN))
    causal = blk <= q_blk
    for forced_blk in (0, q_blk, q_blk - 1):
        imp = jnp.where(blk == forced_blk, FORCED_SCORE, imp)
    imp = jnp.where(causal, imp, -1.0)
    imp = jnp.where(blk < n_slc, imp, -2.0)
    blk_f = blk.astype(F32)
    sel = jnp.zeros((LANE, Q), F32)
    for _ in range(top_n):
        m = jnp.max(imp, axis=0, keepdims=True)
        first = jnp.min(jnp.where(imp == m, blk_f, float(LANE)), axis=0, keepdims=True)
        hit = blk_f == first
        sel = jnp.where(hit, 1.0, sel)
        imp = jnp.where(hit, -3.0, imp)
    sel = jnp.where(causal, sel, 0.0).T
    selb_ref[...] = jnp.where(sel > 0.5, 0.0, NEG_INF).astype(BF16)
    any_ref[...] = jnp.max(sel, axis=0, keepdims=True)


def _nsa_select(p_nsa, qtab, kc, vc, overlap_t):
    B, S, _ = p_nsa.shape
    Q = NSA_Q
    nqb = S // Q
    nc = kc.shape[1]
    n_slc = S // NSA_SLC_LEN
    kern = functools.partial(_nsa_sel_kernel, n_slc=n_slc, top_n=min(NSA_TOPN, n_slc))
    return pl.pallas_call(
        kern, name="nsa_select",
        grid=(B, nqb),
        in_specs=[pl.BlockSpec((None, Q, NSA_HEADS * LANE), lambda b, i: (b, i, 0)),
                  pl.BlockSpec((2 * NSA_HEADS, LANE), lambda b, i: (0, 0)),
                  pl.BlockSpec((None, Q, LANE), lambda b, i: (b, i, (W_NSA - LANE) // LANE)),
                  pl.BlockSpec((None, nc, LANE), lambda b, i: (b, 0, 0)),
                  pl.BlockSpec((None, nc, LANE), lambda b, i: (b, 0, 0)),
                  pl.BlockSpec((LANE, nc), lambda b, i: (0, 0))],
        out_specs=[pl.BlockSpec((None, Q, GROUP_W), lambda b, i: (b, i, 0)),
                   pl.BlockSpec((None, Q, LANE), lambda b, i: (b, i, 0)),
                   pl.BlockSpec((None, None, 1, LANE), lambda b, i: (b, i, 0, 0))],
        out_shape=[jax.ShapeDtypeStruct((B, S, GROUP_W), F32),
                   jax.ShapeDtypeStruct((B, S, LANE), BF16),
                   jax.ShapeDtypeStruct((B, nqb, 1, LANE), F32)],
        scratch_shapes=[pltpu.VMEM((LANE, Q), F32)],
        compiler_params=_cparams("parallel", "parallel"),
    )(p_nsa, qtab, p_nsa, kc, vc, overlap_t)


def _nsa_attn_kernel(flags_ref, q_ref, qtab_ref, gate_ref, oc_ref, selb_ref, ks_ref, vs_ref, kw_ref, vw_ref,
                     o_ref, m_ref, acc_ref, ow_ref, *, nt):
    Q, H, TK = q_ref.shape[0], NSA_HEADS, NSA_TILE
    PART = H * Q
    b = pl.program_id(0)
    qb = pl.program_id(1)
    nqb = pl.num_programs(1)
    qh = _nsa_queries(q_ref, qtab_ref, qb)
    nt_dims = (((1,), (1,)), ((), ()))

    selb = selb_ref[...]
    qs_sel = jnp.concatenate([jnp.concatenate([q, selb], axis=1) for q in qh], axis=0)
    sink_off = jnp.where(_iota(selb.shape, 1) < NSA_SINK // NSA_SLC_LEN, NEG_INF, selb.astype(F32)).astype(BF16)
    qs_loop = jnp.concatenate([jnp.concatenate([q, sink_off], axis=1) for q in qh], axis=0)
    m_ref[...] = jnp.full(m_ref.shape, NEG_INF, F32)
    acc_ref[...] = jnp.zeros_like(acc_ref)

    def update(t, diagonal):
        rows = pl.ds(pl.multiple_of(t * TK, TK), TK)
        k_tile, v_tile = ks_ref[rows, :], vs_ref[rows, :]
        parts = [slice(i * PART, (i + 1) * PART) for i in range(H * Q // PART)]
        qs_t = qs_sel if diagonal else qs_loop
        scores = [lax.dot_general(qs_t[r], k_tile, nt_dims, preferred_element_type=F32) for r in parts]
        if diagonal:
            k_sink, v_sink = ks_ref[0:NSA_SINK, :], vs_ref[0:NSA_SINK, :]
            sink_scores = [lax.dot_general(qs_sel[r], k_sink, nt_dims, preferred_element_type=F32) for r in parts]
            sink_bias = jnp.where(t > 0, 0.0, NEG_INF)
        for idx, (r, s) in enumerate(zip(parts, scores)):
            m_prev = m_ref[r]
            if diagonal:
                ahead = _iota((PART, TK), 1) - (_iota((PART, TK), 0) & (Q - 1))
                s = jnp.where(ahead <= qb * Q - t * TK, s, NEG_INF)
                s_sink = sink_scores[idx] + sink_bias
                m_prev = jnp.maximum(m_prev, jnp.max(s_sink, axis=-1, keepdims=True))
            m_new = jnp.maximum(m_prev, jnp.max(s, axis=-1, keepdims=True))
            p = jnp.exp2(s - jnp.tile(m_new, (1, TK // LANE)))
            update_acc = jnp.dot(p.astype(BF16), v_tile, preferred_element_type=F32)
            if diagonal:
                p_sink = jnp.exp2(s_sink - jnp.tile(m_new, (1, NSA_SINK // LANE)))
                update_acc = update_acc + jnp.dot(p_sink.astype(BF16), v_sink, preferred_element_type=F32)
            acc_ref[r] = jnp.exp2(m_ref[r] - m_new) * acc_ref[r] + update_acc
            m_ref[r] = m_new

    def tile(t, carry):
        @pl.when(flags_ref[(b * nqb + qb) * nt + t] > 0)
        def _():
            update(t, False)
        return carry

    t_diag = (qb * Q) // TK
    lax.fori_loop(0, t_diag, tile, 0)
    update(t_diag, True)

    assert Q == NSA_WIN
    half = Q // 2

    def band(q_rows, first_key, n_keys, masks):
        keys = pl.ds(pl.multiple_of(first_key, half), n_keys)
        s = lax.dot_general(q_rows, kw_ref[keys, :], nt_dims, preferred_element_type=F32)
        groups = [s[:, g * half:(g + 1) * half] for g in range(n_keys // half)]
        s = jnp.concatenate([g if m is None else jnp.where(m, g, NEG_INF) for g, m in zip(groups, masks)], axis=1)
        p = jnp.exp2(s - jnp.max(s, axis=-1, keepdims=True))
        return _normalise(jnp.dot(p.astype(BF16), vw_ref[keys, :], preferred_element_type=F32))

    @pl.when(qb == 0)
    def _():
        qs_win = jnp.concatenate(qh, axis=0)
        row = _iota((H * Q, half), 0) & (Q - 1)
        col = _iota((H * Q, half), 1)
        ow_ref[...] = band(qs_win, 0, Q, [col <= row, col + half <= row])

    @pl.when(qb > 0)
    def _():
        row = _iota((H * half, half), 0) & (half - 1)
        col = _iota((H * half, half), 1)
        masks = [col > row, None, col <= row]
        for part in range(2):
            q_rows = jnp.concatenate([q[part * half:(part + 1) * half] for q in qh], axis=0)
            o_part = band(q_rows, (qb - 1) * Q + part * half, NSA_WIN + half, masks)
            for h in range(H):
                ow_ref[h * Q + part * half:h * Q + (part + 1) * half, :] = o_part[h * half:(h + 1) * half]

    gates = jax.nn.sigmoid(gate_ref[...])
    mixed = _stacked_gate(gates, 1) * _normalise(acc_ref[...]) + _stacked_gate(gates, 2) * ow_ref[...]
    o_ref[...] = (oc_ref[...] + _unstack_heads(mixed)).astype(o_ref.dtype)


def _nsa_attend(p_nsa, qtab, o_c, selb, flags, kv):
    B, S, _ = p_nsa.shape
    Q = NSA_Q
    nqb = S // Q
    nt = S // NSA_TILE
    gate_blk = (W_NSA - LANE) // LANE
    kern = functools.partial(_nsa_attn_kernel, nt=nt)
    slab = lambda width, col: pl.BlockSpec((None, S, width), lambda b, i, f: (b, 0, col))
    grid_spec = pltpu.PrefetchScalarGridSpec(
        num_scalar_prefetch=1,
        grid=(B, nqb),
        in_specs=[pl.BlockSpec((None, Q, NSA_HEADS * LANE), lambda b, i, f: (b, i, 0)),
                  pl.BlockSpec((2 * NSA_HEADS, LANE), lambda b, i, f: (0, 0)),
                  pl.BlockSpec((None, Q, LANE), lambda b, i, f: (b, i, gate_blk)),
                  pl.BlockSpec((None, Q, GROUP_W), lambda b, i, f: (b, i, 0)),
                  pl.BlockSpec((None, Q, LANE), lambda b, i, f: (b, i, 0)),
                  slab(2 * LANE, 0), slab(LANE, 2), slab(LANE, 3), slab(LANE, 4)],
        out_specs=pl.BlockSpec((None, Q, GROUP_W), lambda b, i, f: (b, i, 0)),
        scratch_shapes=[pltpu.VMEM((NSA_HEADS * Q, LANE), F32)] * 3,
    )
    return pl.pallas_call(
        kern, name="nsa_attend",
        grid_spec=grid_spec,
        out_shape=jax.ShapeDtypeStruct((B, S, GROUP_W), BF16),
        compiler_params=_cparams("parallel", "parallel"),
    )(flags, p_nsa, qtab, p_nsa, o_c, selb, kv, kv, kv, kv)


def _nsa_tables(S):
    nc = S // NSA_CMP_STRIDE
    n = np.arange(nc)[None, :]
    j = np.arange(LANE)[:, None]
    start = n * NSA_CMP_STRIDE
    ov = (start < (j + 1) * NSA_SLC_LEN) & (start + NSA_CMP_LEN - 1 >= j * NSA_SLC_LEN)
    ov &= (n < (S - NSA_CMP_LEN) // NSA_CMP_STRIDE + 1) & (j < S // NSA_SLC_LEN)
    pos = np.arange(S)
    k_zero = np.zeros((S, NSA_DK), np.float32)
    block_onehot = (pos[:, None] // NSA_SLC_LEN == np.arange(LANE)[None, :]).astype(np.float32)
    v_lanes = np.zeros((S, LANE), np.float32)
    v_lanes[:, ONES_LANE] = 1.0
    kv_table = np.concatenate([k_zero, _nsa_pos_lanes(pos), block_onehot, v_lanes,
                               k_zero, _nsa_pos_lanes(pos), v_lanes], axis=1)
    assert kv_table.shape[1] == W_KV
    return _nsa_query_table(), jnp.asarray(ov.astype(np.float32)), jnp.asarray(kv_table, dtype=BF16)


def _nsa(p_nsa, kv, uk, uv, pe_k, w1_k, w2_k, pe_v, w1_v, w2_v, tables):
    B, S, _ = p_nsa.shape
    qtab, overlap_t, _ = tables
    kc, vc = _nsa_compress(uk, uv, pe_k, w1_k, w2_k, pe_v, w1_v, w2_v)
    o_c, selb, blk_any = _nsa_select(p_nsa, qtab, kc, vc, overlap_t)
    per_tile = NSA_TILE // NSA_SLC_LEN
    nt = S // NSA_TILE
    not_sink = (np.arange(nt * per_tile) >= NSA_SINK // NSA_SLC_LEN).astype(np.float32)
    blk_any = blk_any[:, :, 0, :nt * per_tile] * not_sink
    flags = blk_any.reshape(B, S // NSA_Q, nt, per_tile).max(axis=-1)
    flags = (flags > 0).astype(jnp.int32).reshape(-1)
    return _nsa_attend(p_nsa, qtab, o_c, selb, flags, kv)


def _out_proj_kernel(h_ref, ya_ref, yb_ref, yc_ref, yd_ref, w_ref, g_ref, b_ref, o_ref, wb_ref):
    @pl.when(pl.program_id(0) == 0)
    def _():
        wb_ref[...] = w_ref[...].astype(BF16)

    mix = None
    for idx, y_ref in enumerate((ya_ref, yb_ref, yc_ref, yd_ref)):
        part = _mm(y_ref[...], wb_ref[idx * GROUP_W:(idx + 1) * GROUP_W, :])
        mix = part if mix is None else mix + part
    o_ref[...] = _layer_norm(DEEPNORM_ALPHA * h_ref[...] + mix, g_ref[...], b_ref[...])


def _out_proj(h2, ys, w_out, layer, g, b, tm=1024):
    T, D = h2.shape
    row = lambda w: pl.BlockSpec((tm, w), lambda i: (i, 0))
    const = lambda shape: pl.BlockSpec(shape, lambda i: (0,) * len(shape))
    return pl.pallas_call(
        _out_proj_kernel, name="out_proj_ln",
        grid=(T // tm,),
        in_specs=[row(D), row(GROUP_W), row(GROUP_W), row(GROUP_W), row(GROUP_W),
                  pl.BlockSpec((None, D, D), lambda i: (layer, 0, 0)), const((1, D)), const((1, D))],
        out_specs=row(D),
        out_shape=jax.ShapeDtypeStruct((T, D), F32),
        scratch_shapes=[pltpu.VMEM((D, D), BF16)],
        compiler_params=_cparams("arbitrary"),
    )(h2, *ys, w_out, g.reshape(1, D), b.reshape(1, D))


def _mlp_kernel(h_ref, w1_ref, w2_ref, g_ref, b_ref, o_ref, acc_ref):
    f = pl.program_id(1)

    @pl.when(f == 0)
    def _():
        acc_ref[...] = jnp.zeros_like(acc_ref)

    a = jnp.maximum(_mm(h_ref[...], w1_ref[...]), 0.0)
    acc_ref[...] += _mm(a * a, w2_ref[...])

    @pl.when(f == pl.num_programs(1) - 1)
    def _():
        o_ref[...] = _layer_norm(DEEPNORM_ALPHA * h_ref[...] + acc_ref[...], g_ref[...], b_ref[...])


def _mlp(h2, w1, w2, layer, g, b, tm=1024, tf=1024):
    T, D = h2.shape
    F = w1.shape[2]
    return pl.pallas_call(
        _mlp_kernel, name="mlp_ln",
        grid=(T // tm, F // tf),
        in_specs=[pl.BlockSpec((tm, D), lambda i, f: (i, 0)),
                  pl.BlockSpec((None, D, tf), lambda i, f: (layer, 0, f)),
                  pl.BlockSpec((None, tf, D), lambda i, f: (layer, f, 0)),
                  pl.BlockSpec((1, D), lambda i, f: (0, 0)),
                  pl.BlockSpec((1, D), lambda i, f: (0, 0))],
        out_specs=pl.BlockSpec((tm, D), lambda i, f: (i, 0)),
        out_shape=jax.ShapeDtypeStruct((T, D), F32),
        scratch_shapes=[pltpu.VMEM((tm, D), F32)],
        compiler_params=_cparams("parallel", "arbitrary"),
    )(h2, w1, w2, g.reshape(1, D), b.reshape(1, D))


def kernel(x, ln_emb_g, ln_emb_b, w_in, conv_w, conv_b, dt_bias, a_log, d_skip, ssm_norm_g, q_norm_g, w_uq, kv_norm_g, w_ukv, cmp_pe_k, cmp_w1_k, cmp_w2_k, cmp_pe_v, cmp_w1_v, cmp_w2_v, w_out, ln1_g, ln1_b, w_mlp1, w_mlp2, ln2_g, ln2_b):
    B, S, D = x.shape
    assert D == D_MODEL and S // NSA_SLC_LEN <= LANE
    assert S % NSA_TILE == 0 and S % NSA_Q == 0 and S % min(MLA_TILE, S) == 0 and S % SSM_CHUNK == 0
    T = B * S
    ret_tables = _ret_tables(S)
    mla_tables = _mla_tables(S)
    nsa_tables = _nsa_tables(S)
    h = x.reshape(T, D)
    for l in range(w_in.shape[0]):
        if l == 0:
            h, *proj = _in_proj(h, _layout_w_in(w_in[l]), nsa_tables[2], entry_ln=(ln_emb_g, ln_emb_b))
        else:
            proj = _in_proj(h, _layout_w_in(w_in[l]), nsa_tables[2])
        p_ssm, p_mla, p_ret, p_nsa, nsa_kv, uk, uv = proj
        cmp_rows = (B, S // NSA_CMP_STRIDE, uk.shape[-1])
        y_a, y_c = _ssm_and_retention(p_ssm.reshape(B, S, W_SSM), conv_w[l], conv_b[l], dt_bias[l], a_log[l],
                                      d_skip[l], ssm_norm_g[l], p_ret.reshape(B, S, W_RET), ret_tables)
        wq, wk, wv = _layout_mla_weights(w_uq[l], w_ukv[l])
        q, k, v = _mla_prep(p_mla.reshape(B, S, W_MLA), q_norm_g[l], kv_norm_g[l], wq, wk, wv, mla_tables)
        y_b = _mla_attn(q, k, v)
        y_d = _nsa(p_nsa.reshape(B, S, W_NSA), nsa_kv.reshape(B, S, W_KV), uk.reshape(cmp_rows), uv.reshape(cmp_rows),
                   cmp_pe_k[l], cmp_w1_k[l], cmp_w2_k[l], cmp_pe_v[l], cmp_w1_v[l], cmp_w2_v[l], nsa_tables)
        ys = [y.reshape(T, GROUP_W) for y in (y_a, y_b, y_c, y_d)]
        h = _out_proj(h, ys, w_out, l, ln1_g[l], ln1_b[l])
        h = _mlp(h, w_mlp1, w_mlp2, l, ln2_g[l], ln2_b[l])
    return h.reshape(B, S, D)
```

```python
import functools
import math

import jax
import jax.numpy as jnp
import numpy as np
from jax import lax
from jax.experimental import pallas as pl
from jax.experimental.pallas import tpu as pltpu

F32 = jnp.float32
BF16 = jnp.bfloat16

D_MODEL = 1024
DEPTH = 2
GROUP_W = D_MODEL // 4
SSM_HEADS = 4
SSM_HEAD_DIM = GROUP_W // SSM_HEADS
SSM_GROUPS = 2
SSM_STATE = 128
SSM_CONV = 4
SSM_CHUNK = 128
SSM_XBC = GROUP_W + 2 * SSM_GROUPS * SSM_STATE
MLA_HEADS = 4
MLA_NOPE = 64
MLA_ROPE = 32
MLA_V = GROUP_W // MLA_HEADS
MLA_Q_RANK = 256
MLA_KV_RANK = 128
RET_HEADS = 4
RET_DK = 64
RET_DV = GROUP_W // RET_HEADS
RET_CHUNK = 128
NSA_HEADS = 4
NSA_DK = 64
NSA_DV = GROUP_W // NSA_HEADS
NSA_CMP_LEN = 32
NSA_CMP_STRIDE = 16
NSA_CMP_HID = 256
NSA_SLC_LEN = 64
NSA_TOPN = 16
NSA_WIN = 512
D_FF = 4 * D_MODEL
NSA_Q = 512
ROPE_THETA = 10000.0
EPS = 1e-5
NEG_INF = -1e30
LOG2_E = math.log2(math.e)
FORCED_SCORE = 1e9
DEEPNORM_ALPHA = (2.0 * DEPTH) ** 0.25

IN_SPLITS = (
    GROUP_W, SSM_XBC, SSM_HEADS,
    MLA_Q_RANK, MLA_KV_RANK, MLA_ROPE,
    RET_HEADS * RET_DK, RET_HEADS * RET_DK, RET_HEADS * RET_DV, GROUP_W,
    NSA_HEADS * NSA_DK, NSA_DK, NSA_DV, NSA_DK, NSA_DV, NSA_DK, NSA_DV, 3 * NSA_HEADS,
)

LANE = 128
SUBLANE = 8
W_SSM = GROUP_W + SSM_XBC + LANE
W_MLA = MLA_Q_RANK + MLA_KV_RANK + LANE
W_RET = 4 * GROUP_W
W_NSA = NSA_HEADS * LANE + LANE + LANE
W_KV = 2 * LANE + 3 * LANE
W_PROJ = ((GROUP_W + SSM_XBC) + (MLA_Q_RANK + MLA_KV_RANK) + W_RET
          + (NSA_HEADS * NSA_DK + LANE) + 4 * NSA_DK + LANE)

NSA_TILE = 512
NSA_SINK = 128
MLA_TILE = 1024
VMEM_LIMIT = 48 * 1024 * 1024


def _cparams(*sem):
    return pltpu.CompilerParams(dimension_semantics=sem, vmem_limit_bytes=VMEM_LIMIT)


def _mm(a, b):
    return jnp.dot(a.astype(BF16), b.astype(BF16), preferred_element_type=F32)


def _mm_nt(a, b):
    return lax.dot_general(a.astype(BF16), b.astype(BF16), (((1,), (1,)), ((), ())),
                           preferred_element_type=F32)


def _split_f32(x):
    hi = x.astype(BF16)
    rest = x - hi.astype(F32)
    mid = rest.astype(BF16)
    lo = (rest - mid.astype(F32)).astype(BF16)
    return hi, mid, lo


def _mm_f32(a, b, exact, dims=(((1,), (0,)), ((), ()))):
    fixed, pieces = (a.astype(BF16), _split_f32(b)) if exact == "a" else (b.astype(BF16), _split_f32(a))
    out = None
    for piece in pieces:
        lhs, rhs = (fixed, piece) if exact == "a" else (piece, fixed)
        part = lax.dot_general(lhs, rhs, dims, preferred_element_type=F32)
        out = part if out is None else out + part
    return out


def _silu(x):
    return x * jax.nn.sigmoid(x)


def _softplus(x):
    return jnp.maximum(x, 0.0) + jnp.log1p(jnp.exp(-jnp.abs(x)))


def _layer_norm(x, g, b):
    mu = jnp.mean(x, axis=-1, keepdims=True)
    xc = x - mu
    var = jnp.mean(xc * xc, axis=-1, keepdims=True)
    return xc * lax.rsqrt(var + EPS) * g + b


def _iota(shape, dim):
    return lax.broadcasted_iota(jnp.int32, shape, dim)


def _pad_cols(w, width):
    return jnp.pad(w, ((0, 0), (0, width - w.shape[1])))


def _layout_w_in(w):
    offs = np.concatenate([[0], np.cumsum(IN_SPLITS)])
    p = [w[:, int(offs[i]):int(offs[i + 1])] for i in range(len(IN_SPLITS))]
    (ssm_z, ssm_xbc, ssm_dt, mla_cq, mla_ckv, mla_kr, ret_q, ret_k, ret_v, ret_g,
     nsa_q, nsa_kc, nsa_vc, nsa_ks, nsa_vs, nsa_kw, nsa_vw, nsa_gate) = p
    small = _pad_cols(jnp.concatenate([mla_kr, ssm_dt, nsa_gate], axis=1), LANE)
    cols = [ssm_z, ssm_xbc, ret_q, ret_k, ret_v, ret_g, nsa_q, nsa_ks, nsa_vs, nsa_kw, nsa_vw,
            mla_cq, mla_ckv, nsa_kc, nsa_vc, small]
    out = jnp.concatenate(cols, axis=1)
    assert out.shape[1] == W_PROJ
    return out.astype(BF16)


def _in_proj_kernel(*refs, entry_ln):
    if entry_ln:
        h_ref, g_ref, b_ref, w_ref, kvtab_ref, hn_ref, *outs = refs
        hn = _layer_norm(h_ref[...], g_ref[...], b_ref[...])
        hn_ref[...] = hn
    else:
        h_ref, w_ref, kvtab_ref, *outs = refs
        hn = h_ref[...]
    ssm_ref, mla_ref, ret_ref, nsa_ref, kv_ref, uk_ref, uv_ref, kcv_ref = outs
    hb = hn.astype(BF16)
    tm = hb.shape[0]
    off = 0

    def project(width):
        nonlocal off
        out = jnp.dot(hb, w_ref[:, off:off + width], preferred_element_type=F32)
        off += width
        return out

    ssm_ref[:, 0:GROUP_W + SSM_XBC] = project(GROUP_W + SSM_XBC)
    ret_ref[...] = project(W_RET)
    q = project(NSA_HEADS * NSA_DK)
    kv = project(4 * NSA_DK)
    latent = MLA_Q_RANK + MLA_KV_RANK
    mla_kc = project(latent + LANE)
    mla_ref[:, 0:latent] = mla_kc[:, 0:latent]
    kc_lane = NSA_HEADS * LANE
    nsa_ref[:, kc_lane:kc_lane + LANE] = mla_kc[:, latent:]
    small = project(LANE)
    lane = _iota((tm, LANE), 1)
    low = lane < NSA_DK
    for h in range(NSA_HEADS):
        pair = q[:, (h // 2) * LANE:(h // 2 + 1) * LANE]
        head = pair if h % 2 == 0 else pltpu.roll(pair, NSA_DK, 1)
        nsa_ref[:, h * LANE:(h + 1) * LANE] = jnp.where(low, head, 0.0)
    mla_ref[:, MLA_Q_RANK + MLA_KV_RANK:] = jnp.where(lane < MLA_ROPE, small, 0.0)
    ssm_ref[:, GROUP_W + SSM_XBC:] = jnp.where(lane < SSM_HEADS, pltpu.roll(small, LANE - MLA_ROPE, 1), 0.0)
    gate_at = MLA_ROPE + SSM_HEADS
    nsa_ref[:, kc_lane + LANE:] = jnp.where(lane < 3 * NSA_HEADS, pltpu.roll(small, LANE - gate_at, 1), 0.0)
    groups = h_ref.shape[0] // NSA_CMP_STRIDE
    kc_lane = NSA_HEADS * LANE
    kcv_ref[...] = nsa_ref[:, kc_lane:kc_lane + LANE]
    for t in range(NSA_CMP_STRIDE):
        piece = kcv_ref[pl.ds(t, groups, stride=NSA_CMP_STRIDE), :]
        uk_ref[:, t * NSA_DK:(t + 1) * NSA_DK] = piece[:, :NSA_DK]
        uv_ref[:, t * NSA_DV:(t + 1) * NSA_DV] = piece[:, NSA_DK:]
    sel_kv, win_kv = kv[:, :LANE], kv[:, LANE:]
    pieces = {0: sel_kv, 2: pltpu.roll(sel_kv, NSA_DK, 1), 3: win_kv, 4: pltpu.roll(win_kv, NSA_DK, 1)}
    for slab in range(W_KV // LANE):
        lanes = slice(slab * LANE, (slab + 1) * LANE)
        tab = kvtab_ref[:, lanes]
        if slab in pieces:
            kv_ref[:, lanes] = (jnp.where(low, pieces[slab], 0.0) + tab.astype(F32)).astype(BF16)
        else:
            kv_ref[:, lanes] = tab


def _in_proj(h2, w_p, kv_table, entry_ln=None, tm=512):
    T, D = h2.shape
    S = kv_table.shape[0]
    widths = (W_SSM, W_MLA, W_RET, W_NSA)
    half = NSA_CMP_STRIDE * NSA_DK
    row = lambda w: pl.BlockSpec((tm, w), lambda i: (i, 0))
    const = lambda shape: pl.BlockSpec(shape, lambda i: (0,) * len(shape))
    in_specs = [const((D, W_PROJ)), pl.BlockSpec((tm, W_KV), lambda i: (i % (S // tm), 0))]
    out_specs = [row(w) for w in widths + (W_KV,)] + [pl.BlockSpec((tm // NSA_CMP_STRIDE, half), lambda i: (i, 0))] * 2
    out_shape = ([jax.ShapeDtypeStruct((T, w), F32) for w in widths] + [jax.ShapeDtypeStruct((T, W_KV), BF16)]
                 + [jax.ShapeDtypeStruct((T // NSA_CMP_STRIDE, half), F32)] * 2)
    operands = (w_p, kv_table)
    if entry_ln is not None:
        in_specs = [const((1, D)), const((1, D))] + in_specs
        out_specs = [row(D)] + out_specs
        out_shape = [jax.ShapeDtypeStruct((T, D), F32)] + out_shape
        operands = tuple(v.reshape(1, D) for v in entry_ln) + operands
    return pl.pallas_call(
        functools.partial(_in_proj_kernel, entry_ln=entry_ln is not None), name="in_proj",
        grid=(T // tm,),
        in_specs=[row(D)] + in_specs,
        out_specs=out_specs,
        out_shape=out_shape,
        scratch_shapes=[pltpu.VMEM((tm, LANE), F32)],
        compiler_params=_cparams("parallel"),
    )(h2, *operands)


def _ssm_chunk(p_ref, cw_ref, cb_ref, dtb_ref, alog_ref, dskip_ref, ng_ref, o_ref, state_ref, ext_ref):
    L, H, P, N = SSM_CHUNK, SSM_HEADS, SSM_HEAD_DIM, SSM_STATE
    z = p_ref[:, 0:GROUP_W]
    ext_ref[SUBLANE:SUBLANE + L, :] = p_ref[:, GROUP_W:GROUP_W + SSM_XBC]
    ext = ext_ref[...]
    conv = cb_ref[...] + ext[SUBLANE:] * cw_ref[SSM_CONV - 1:SSM_CONV, :]
    for j in range(SSM_CONV - 1):
        shift = SSM_CONV - 1 - j
        conv = conv + pltpu.roll(ext, shift, 0)[SUBLANE:] * cw_ref[j:j + 1, :]
    ext_ref[0:SUBLANE, :] = ext_ref[L:L + SUBLANE, :]
    xbc = _silu(conv)
    xs = xbc[:, 0:GROUP_W]
    b_in = xbc[:, GROUP_W:GROUP_W + SSM_GROUPS * N]
    c_in = xbc[:, GROUP_W + SSM_GROUPS * N:]

    dt = _softplus(p_ref[:, GROUP_W + SSM_XBC:] + dtb_ref[...])
    a = dt * (-jnp.exp(alog_ref[...]))
    row = _iota((L, L), 0)
    col = _iota((L, L), 1)
    tril = col <= row
    cs = _mm_f32(jnp.where(tril, 1.0, 0.0), a, exact="a")
    cs_t = cs.T
    ecs = jnp.exp(cs)
    dte = jnp.exp(cs[L - 1:L, :] - cs)
    first_head_of_pair = _iota((L, LANE), 1) < P

    def expand(x):
        pairs = [jnp.where(first_head_of_pair, x[:, h:h + 1], x[:, h + 1:h + 2]) for h in range(0, H, LANE // P)]
        return jnp.concatenate(pairs, axis=1)

    assert LANE == 2 * P
    dt_x = expand(dt)
    ecs_x = expand(ecs)
    dte_x = expand(dte)

    xdt = xs * dt_x
    wx = xdt * dte_x
    head_of_lane = _iota((L, H * P), 1) // P
    y = xs * dskip_ref[...]
    y_off = []
    rep = H // SSM_GROUPS
    for g in range(SSM_GROUPS):
        cg = c_in[:, g * N:(g + 1) * N]
        bg = b_in[:, g * N:(g + 1) * N]
        cb = _mm_nt(cg, bg)
        for h in range(g * rep, (g + 1) * rep):
            diff = cs[:, h:h + 1] - cs_t[h:h + 1, :]
            seg = jnp.where(tril, jnp.exp(jnp.where(tril, diff, 0.0)), 0.0)
            yh = _mm(cb * seg, xdt)
            y = y + jnp.where(head_of_lane == h, yh, 0.0)
        lanes = slice(g * rep * P, (g + 1) * rep * P)
        st_prev = state_ref[:, lanes]
        y_off.append(_mm(cg, st_prev))
        state_ref[:, lanes] = st_prev * ecs_x[L - 1:L, lanes] + _mm(bg.T, wx[:, lanes])
    y = y + jnp.concatenate(y_off, axis=1) * ecs_x
    y = y * _silu(z)
    ms = jnp.mean(y * y, axis=-1, keepdims=True)
    o_ref[...] = (y * lax.rsqrt(ms + EPS) * ng_ref[...]).astype(o_ref.dtype)


def _recurrent_kernel(ps_ref, cw_ref, cb_ref, dtb_ref, alog_ref, dskip_ref, ng_ref,
                      pr_ref, cos_ref, sin_ref, dec_ref, zeta_ref, xi_ref, cd_ref,
                      oa_ref, oc_ref, sstate_ref, ext_ref, rstate_ref):
    @pl.when(pl.program_id(0) == 0)
    def _():
        sstate_ref[...] = jnp.zeros_like(sstate_ref)
        rstate_ref[...] = jnp.zeros_like(rstate_ref)
        ext_ref[:, 0:SUBLANE, :] = jnp.zeros((ext_ref.shape[0], SUBLANE, SSM_XBC), F32)

    for b in range(ps_ref.shape[0]):
        _ssm_chunk(ps_ref.at[b], cw_ref, cb_ref, dtb_ref, alog_ref, dskip_ref, ng_ref,
                   oa_ref.at[b], sstate_ref.at[b], ext_ref.at[b])
        _ret_chunk(pr_ref.at[b], cos_ref, sin_ref, dec_ref, zeta_ref, xi_ref, cd_ref, oc_ref.at[b], rstate_ref.at[b])


def _ret_chunk(p_ref, cos_ref, sin_ref, dec_ref, zeta_ref, xi_ref, cd_ref, o_ref, state_ref):
    L, H, DK, DV = RET_CHUNK, RET_HEADS, RET_DK, RET_DV
    W = H * DK
    q = p_ref[:, 0:W]
    k = p_ref[:, W:2 * W]
    v = p_ref[:, 2 * W:3 * W]
    gate = p_ref[:, 3 * W:4 * W]
    lane = _iota((L, W), 1)
    first_half = (lane % DK) < (DK // 2)
    head_of_lane = lane // DK

    def rope(x):
        partner = jnp.where(first_half, pltpu.roll(x, W - DK // 2, 1), pltpu.roll(x, DK // 2, 1))
        return x * cos_ref[...] + partner * sin_ref[...]

    qr = rope(q)
    kr = rope(k) * (DK ** -0.5)
    y = jnp.zeros((L, H * DV), F32)
    for h in range(H):
        qh = jnp.where(head_of_lane == h, qr, 0.0)
        sc = _mm_nt(qh, kr) * dec_ref[h]
        y = y + jnp.where(head_of_lane == h, _mm(sc, v), 0.0)
    st = state_ref[...]
    y = y + _mm(qr * xi_ref[...], st)
    same_head = (_iota((W, H * DV), 0) // DK) == (_iota((W, H * DV), 1) // DV)
    kv = _mm((kr * zeta_ref[...]).T, v)
    state_ref[...] = st * cd_ref[...] + jnp.where(same_head, kv, 0.0)
    assert DV & (DV - 1) == 0
    ms = _mm_f32(y * y, jnp.where(same_head, 1.0 / DV, 0.0), exact="b")
    o_ref[...] = (y * lax.rsqrt(ms + EPS) * _silu(gate)).astype(o_ref.dtype)


def _ret_tables(S):
    H, DK, L = RET_HEADS, RET_DK, RET_CHUNK
    inv = ROPE_THETA ** (-np.arange(0, DK, 2, dtype=np.float64) / DK)
    ang = np.arange(S, dtype=np.float64)[:, None] * inv[None, :]
    cos, sin = np.cos(ang), np.sin(ang)
    cos_t = np.tile(np.concatenate([cos, cos], axis=1), (1, H))
    sin_t = np.tile(np.concatenate([-sin, sin], axis=1), (1, H))
    log_gamma = np.log1p(-np.exp2(-5.0 - np.arange(H, dtype=np.float64)))
    pos = np.arange(L, dtype=np.float64)
    diff = pos[:, None] - pos[None, :]
    decay_in = np.where(diff >= 0, np.exp(np.maximum(diff, 0.0)[None] * log_gamma[:, None, None]), 0.0)
    zeta = np.exp((L - 1 - pos)[None] * log_gamma[:, None])
    xi = np.exp((pos + 1.0)[None] * log_gamma[:, None])
    chunk_decay = np.exp(L * log_gamma)
    zeta_x = np.repeat(zeta.T, DK, axis=1)
    xi_x = np.repeat(xi.T, DK, axis=1)
    cd_x = np.repeat(chunk_decay, RET_DV).reshape(1, H * RET_DV)
    return tuple(jnp.asarray(t, dtype=F32) for t in (cos_t, sin_t, decay_in, zeta_x, xi_x, cd_x))


def _ssm_and_retention(p_ssm, conv_w, conv_b, dt_bias, a_log, d_skip, norm_g, p_ret, tables):
    B, S, _ = p_ssm.shape
    L, H = SSM_CHUNK, RET_HEADS
    assert RET_CHUNK == L
    W = H * RET_DK
    cos_t, sin_t, decay_in, zeta_x, xi_x, cd_x = tables
    pad_h = lambda v: jnp.pad(v, (0, LANE - SSM_HEADS)).reshape(1, LANE)
    const = lambda shape: pl.BlockSpec(shape, lambda c: (0,) * len(shape))
    chunk = lambda width: pl.BlockSpec((B, L, width), lambda c: (0, c, 0))
    out_shape = jax.ShapeDtypeStruct((B, S, GROUP_W), BF16)
    return pl.pallas_call(
        _recurrent_kernel, name="ssm_retention",
        grid=(S // L,),
        in_specs=[chunk(W_SSM),
                  const((SSM_CONV, SSM_XBC)), const((1, SSM_XBC)), const((1, LANE)), const((1, LANE)),
                  const((1, GROUP_W)), const((1, GROUP_W)),
                  chunk(W_RET),
                  pl.BlockSpec((L, W), lambda c: (c, 0)), pl.BlockSpec((L, W), lambda c: (c, 0)),
                  const((H, L, L)), const((L, W)), const((L, W)), const((1, H * RET_DV))],
        out_specs=[chunk(GROUP_W), chunk(GROUP_W)],
        out_shape=[out_shape, out_shape],
        scratch_shapes=[pltpu.VMEM((B, SSM_STATE, GROUP_W), F32),
                        pltpu.VMEM((B, L + SUBLANE, SSM_XBC), F32),
                        pltpu.VMEM((B, W, H * RET_DV), F32)],
        compiler_params=_cparams("arbitrary"),
    )(p_ssm, conv_w, conv_b.reshape(1, -1), pad_h(dt_bias), pad_h(a_log),
      jnp.repeat(d_skip, SSM_HEAD_DIM).reshape(1, GROUP_W), norm_g.reshape(1, GROUP_W),
      p_ret, cos_t, sin_t, decay_in, zeta_x, xi_x, cd_x)


def _mla_prep_kernel(p_ref, qg_ref, wq_ref, kvg_ref, wk_ref, wv_ref, cos_ref, sin_ref,
                     q_ref, k_ref, v_ref):
    tm = p_ref.shape[0]
    cq = p_ref[:, 0:MLA_Q_RANK]
    ckv = p_ref[:, MLA_Q_RANK:MLA_Q_RANK + MLA_KV_RANK]
    kr = p_ref[:, MLA_Q_RANK + MLA_KV_RANK:]

    def rms(x, g):
        return x * lax.rsqrt(jnp.mean(x * x, axis=-1, keepdims=True) + EPS) * g

    q = _mm(rms(cq, qg_ref[...]), wq_ref[...])
    kvl = rms(ckv, kvg_ref[...])
    kn = _mm(kvl, wk_ref[...])
    vv = _mm(kvl, wv_ref[...])
    kr_sh = pltpu.roll(kr, MLA_NOPE, 1)
    lane = _iota((tm, LANE), 1)
    half = MLA_ROPE // 2
    low = (lane >= MLA_NOPE) & (lane < MLA_NOPE + half)
    cos = cos_ref[...]
    sin = sin_ref[...]

    def rope(x):
        partner = jnp.where(low, pltpu.roll(x, LANE - half, 1), pltpu.roll(x, half, 1))
        return x * cos + partner * sin

    scale = (MLA_NOPE + MLA_ROPE) ** -0.5 * LOG2_E
    for h in range(MLA_HEADS):
        sl = slice(h * LANE, (h + 1) * LANE)
        q_ref[h] = (rope(q[:, sl]) * scale).astype(BF16)
        k_ref[h] = rope(kn[:, sl] + kr_sh).astype(BF16)
        v_ref[h] = jnp.where(lane == _mla_ones_lane(h), 1.0, vv[:, sl]).astype(BF16)


def _mla_tables(S):
    inv = ROPE_THETA ** (-np.arange(0, MLA_ROPE, 2, dtype=np.float64) / MLA_ROPE)
    ang = np.arange(S, dtype=np.float64)[:, None] * inv[None, :]
    cos, sin = np.cos(ang), np.sin(ang)
    tail = LANE - MLA_NOPE - MLA_ROPE
    cos_t = np.concatenate([np.ones((S, MLA_NOPE)), cos, cos, np.ones((S, tail))], axis=1)
    sin_t = np.concatenate([np.zeros((S, MLA_NOPE)), -sin, sin, np.zeros((S, tail))], axis=1)
    return jnp.asarray(cos_t, dtype=F32), jnp.asarray(sin_t, dtype=F32)


def _layout_mla_weights(w_uq, w_ukv):
    H = MLA_HEADS
    dq = MLA_NOPE + MLA_ROPE
    wq = jnp.concatenate([_pad_cols(w_uq[:, h * dq:(h + 1) * dq], LANE) for h in range(H)], axis=1)
    dkv = MLA_NOPE + MLA_V
    wk, wv = [], []
    for h in range(H):
        blk = w_ukv[:, h * dkv:(h + 1) * dkv]
        wk.append(_pad_cols(blk[:, :MLA_NOPE], LANE))
        v = blk[:, MLA_NOPE:]
        zero = jnp.zeros_like(v)
        wv.append(jnp.concatenate([v, zero] if h % 2 == 0 else [zero, v], axis=1))
    return wq.astype(BF16), jnp.concatenate(wk, axis=1).astype(BF16), jnp.concatenate(wv, axis=1).astype(BF16)


def _mla_prep(p_mla, q_norm_g, kv_norm_g, wq, wk, wv, tables, tm=1024):
    B, S, _ = p_mla.shape
    H = MLA_HEADS
    cos_t, sin_t = tables
    const = lambda shape: pl.BlockSpec(shape, lambda b, i: (0,) * len(shape))
    qkv_spec = pl.BlockSpec((None, H, tm, LANE), lambda b, i: (b, 0, i, 0))
    qkv_shape = jax.ShapeDtypeStruct((B, H, S, LANE), BF16)
    return pl.pallas_call(
        _mla_prep_kernel, name="mla_prep",
        grid=(B, S // tm),
        in_specs=[pl.BlockSpec((None, tm, W_MLA), lambda b, i: (b, i, 0)),
                  const((1, MLA_Q_RANK)), const((MLA_Q_RANK, H * LANE)),
                  const((1, MLA_KV_RANK)), const((MLA_KV_RANK, H * LANE)), const((MLA_KV_RANK, H * LANE)),
                  pl.BlockSpec((tm, LANE), lambda b, i: (i, 0)),
                  pl.BlockSpec((tm, LANE), lambda b, i: (i, 0))],
        out_specs=[qkv_spec, qkv_spec, qkv_spec],
        out_shape=[qkv_shape, qkv_shape, qkv_shape],
        compiler_params=_cparams("parallel", "parallel"),
    )(p_mla, q_norm_g.reshape(1, -1), wq, kv_norm_g.reshape(1, -1), wk, wv, cos_t, sin_t)


def _mla_ones_lane(h):
    return MLA_V if h % 2 == 0 else 0


def _mla_attn_kernel(qi_ref, kj_ref, q_ref, k_ref, v_ref, o_ref, m_ref, acc_ref):
    H = MLA_HEADS
    tq, tk = q_ref.shape[1], k_ref.shape[1]
    i = qi_ref[pl.program_id(1)]
    j = kj_ref[pl.program_id(1)]

    @pl.when(j == 0)
    def _():
        m_ref[...] = jnp.full(m_ref.shape, NEG_INF, F32)
        acc_ref[...] = jnp.zeros_like(acc_ref)

    def sweep(blocks):
        nt_dims = (((1,), (1,)), ((), ()))
        scores = [[lax.dot_general(q_ref[h, r0:r0 + nr, :], k_ref[h, 0:nk, :], nt_dims, preferred_element_type=F32)
                   for (r0, nr, nk, _) in blocks] for h in range(H)]
        for h in range(H):
            for (r0, nr, nk, offset), s in zip(blocks, scores[h]):
                if offset is not None:
                    s = jnp.where(_iota((nr, nk), 1) - _iota((nr, nk), 0) <= offset, s, NEG_INF)
                rows = slice(r0, r0 + nr)
                m_prev = m_ref[h, rows]
                m_new = jnp.maximum(m_prev, jnp.max(s, axis=-1, keepdims=True))
                p = jnp.exp2(s - jnp.tile(m_new, (1, nk // LANE)))
                acc_ref[h, rows] = (jnp.exp2(m_prev - m_new) * acc_ref[h, rows]
                                    + jnp.dot(p.astype(BF16), v_ref[h, 0:nk, :], preferred_element_type=F32))
                m_ref[h, rows] = m_new

    assert tq == tk
    half = tq // 2

    @pl.when(j < i)
    def _():
        sweep([(0, tq, tk, None)])

    @pl.when(j == i)
    def _():
        sweep([(0, half, half, 0), (half, half, tk, half)])
        lane = _iota((tq, LANE), 1)
        for pair in range(H // 2):
            he, ho = 2 * pair, 2 * pair + 1
            acc_e, acc_o = acc_ref[he], acc_ref[ho]
            le = acc_e[:, _mla_ones_lane(he):_mla_ones_lane(he) + 1]
            lo = acc_o[:, _mla_ones_lane(ho):_mla_ones_lane(ho) + 1]
            o_ref[:, pair * LANE:(pair + 1) * LANE] = jnp.where(lane < MLA_V, acc_e / le, acc_o / lo).astype(o_ref.dtype)


def _mla_attn(q, k, v):
    B, H, S, _ = q.shape
    t = min(MLA_TILE, S)
    tq = t
    pairs = [(i, j) for i in range(S // tq) for j in range((i + 1) * tq // t)]
    qi = jnp.asarray([p[0] for p in pairs], jnp.int32)
    kj = jnp.asarray([p[1] for p in pairs], jnp.int32)
    grid_spec = pltpu.PrefetchScalarGridSpec(
        num_scalar_prefetch=2,
        grid=(B, len(pairs)),
        in_specs=[pl.BlockSpec((None, H, tq, LANE), lambda b, p, qi, kj: (b, 0, qi[p], 0)),
                  pl.BlockSpec((None, H, t, LANE), lambda b, p, qi, kj: (b, 0, kj[p], 0)),
                  pl.BlockSpec((None, H, t, LANE), lambda b, p, qi, kj: (b, 0, kj[p], 0))],
        out_specs=pl.BlockSpec((None, tq, GROUP_W), lambda b, p, qi, kj: (b, qi[p], 0)),
        scratch_shapes=[pltpu.VMEM((H, tq, LANE), F32), pltpu.VMEM((H, tq, LANE), F32)],
    )
    return pl.pallas_call(
        _mla_attn_kernel, name="mla_attn",
        grid_spec=grid_spec,
        out_shape=jax.ShapeDtypeStruct((B, S, GROUP_W), BF16),
        compiler_params=_cparams("parallel", "arbitrary"),
    )(qi, kj, q, k, v)


POS_HI = NSA_DK
POS_LO = NSA_DK + 3
POS_ONE = NSA_DK + 6
ONES_LANE = NSA_DV


def _split_bf16(x, parts=3):
    out, rem = [], np.float64(x)
    for _ in range(parts):
        piece = np.float64(np.float32(rem).astype(jnp.bfloat16).astype(np.float32))
        out.append(float(piece))
        rem = rem - piece
    return out


def _nsa_query_table():
    H = NSA_HEADS
    tab = np.zeros((2 * H, LANE), np.float32)
    for h in range(H):
        c = 2.0 ** (-8.0 * (h + 1) / H) * LOG2_E
        pieces = _split_bf16(c)
        tab[h, POS_HI:POS_HI + 3] = pieces
        tab[h, POS_LO:POS_LO + 3] = pieces
        tab[H + h, POS_ONE] = -sum(pieces)
    return jnp.asarray(tab)


def _nsa_pos_lanes(pos, lo_offset=0.0):
    t = np.zeros((len(pos), LANE - NSA_DK), np.float32)
    t[:, POS_HI - NSA_DK:POS_HI - NSA_DK + 3] = (NSA_SLC_LEN * (pos // NSA_SLC_LEN))[:, None]
    t[:, POS_LO - NSA_DK:POS_LO - NSA_DK + 3] = (pos % NSA_SLC_LEN + lo_offset)[:, None]
    t[:, POS_ONE - NSA_DK] = 1.0
    return t


def _nsa_queries(q_ref, qtab_ref, qb):
    Q, H = q_ref.shape[0], NSA_HEADS
    qpos = (qb * Q + _iota((Q, 1), 0)).astype(F32)
    out = []
    for h in range(H):
        q = q_ref[:, h * LANE:(h + 1) * LANE] * (NSA_DK ** -0.5 * LOG2_E)
        out.append((q + qtab_ref[h:h + 1, :] + qtab_ref[H + h:H + h + 1, :] * qpos).astype(BF16))
    return out


def _normalise(o):
    return o / o[:, ONES_LANE:ONES_LANE + 1]


def _stacked_gate(gates, branch):
    lanes = [3 * h + branch for h in range(NSA_HEADS)]
    return jnp.concatenate([gates[:, c:c + 1] for c in lanes], axis=0)


def _unstack_heads(o):
    Q = o.shape[0] // NSA_HEADS
    lane = _iota((Q, LANE), 1)
    out = []
    for pair in range(NSA_HEADS // 2):
        even = o[(2 * pair) * Q:(2 * pair + 1) * Q]
        odd = o[(2 * pair + 1) * Q:(2 * pair + 2) * Q]
        out.append(jnp.where(lane < NSA_DV, even, pltpu.roll(odd, NSA_DV, 1)))
    return jnp.concatenate(out, axis=1)


def _nsa_cmp_kernel(uk_ref, uv_ref, pek_ref, pev_ref, w1k_ref, w1v_ref, w2k_ref, w2v_ref, cpos_ref,
                    kc_ref, vc_ref, sh_ref):
    nb = uk_ref.shape[0]
    half = uk_ref.shape[1]

    def hidden(u_ref, pe_ref, w1_ref):
        u = u_ref[...]
        first = _mm(u + pe_ref[0:1, :], w1_ref[0:half, :])
        second = _mm(u + pe_ref[1:2, :], w1_ref[half:2 * half, :])
        sh_ref[0:nb, :] = second
        sh_ref[nb:nb + SUBLANE, :] = jnp.zeros((SUBLANE, NSA_CMP_HID), F32)
        return first + sh_ref[pl.ds(1, nb), :]

    hk = _silu(hidden(uk_ref, pek_ref, w1k_ref))
    hv = _silu(hidden(uv_ref, pev_ref, w1v_ref))
    kc_ref[...] = (_mm(hk, w2k_ref[...]) + cpos_ref[...]).astype(BF16)
    ones_lane = jnp.where(_iota((1, LANE), 1) == ONES_LANE, 1.0, 0.0)
    vc_ref[...] = (_mm(hv, w2v_ref[...]) + ones_lane).astype(BF16)


def _nsa_compress(uk, uv, pe_k, w1_k, w2_k, pe_v, w1_v, w2_v):
    B, nb, half = uk.shape
    hid = NSA_CMP_HID
    const = lambda shape: pl.BlockSpec(shape, lambda b: (0,) * len(shape))
    w2k = _pad_cols(w2_k, LANE).astype(BF16)
    w2v = _pad_cols(w2_v, LANE).astype(BF16)
    centre = _nsa_pos_lanes(np.arange(nb) * NSA_CMP_STRIDE, 0.5 * (NSA_CMP_LEN - 1))
    cpos = jnp.asarray(np.concatenate([np.zeros((nb, NSA_DK), np.float32), centre], axis=1))
    out_spec = pl.BlockSpec((None, nb, LANE), lambda b: (b, 0, 0))
    out_shape = jax.ShapeDtypeStruct((B, nb, LANE), BF16)
    return pl.pallas_call(
        _nsa_cmp_kernel, name="nsa_compress",
        grid=(B,),
        in_specs=[pl.BlockSpec((None, nb, half), lambda b: (b, 0, 0)),
                  pl.BlockSpec((None, nb, half), lambda b: (b, 0, 0)),
                  const((2, half)), const((2, half)),
                  const((2 * half, hid)), const((2 * half, hid)),
                  const((hid, LANE)), const((hid, LANE)), const((nb, LANE))],
        out_specs=[out_spec, out_spec],
        out_shape=[out_shape, out_shape],
        scratch_shapes=[pltpu.VMEM((nb + SUBLANE, hid), F32)],
        compiler_params=_cparams("parallel"),
    )(uk, uv, pe_k.reshape(2, half), pe_v.reshape(2, half), w1_k.astype(BF16), w1_v.astype(BF16), w2k, w2v, cpos)


def _nsa_sel_kernel(q_ref, qtab_ref, gate_ref, kc_ref, vc_ref, ovt_ref, oc_ref, selb_ref, any_ref, imp_ref, *, n_slc, top_n):
    Q, H = q_ref.shape[0], NSA_HEADS
    qb = pl.program_id(1)
    nc = kc_ref.shape[0]
    qs = jnp.concatenate(_nsa_queries(q_ref, qtab_ref, qb), axis=0)
    nt_dims = (((1,), (1,)), ((), ()))

    def attend(ncols):
        s = lax.dot_general(qs, kc_ref[0:ncols, :], nt_dims, preferred_element_type=F32)
        qpos = qb * Q + (_iota((H * Q, ncols), 0) & (Q - 1))
        block_end = _iota((H * Q, ncols), 1) * NSA_CMP_STRIDE + (NSA_CMP_LEN - 1)
        s = jnp.where(block_end <= qpos, s, NEG_INF)
        e = jnp.exp2(s - jnp.max(s, axis=-1, keepdims=True))
        qpos_col = qb * Q + (_iota((H * Q, 1), 0) & (Q - 1))
        has_block = jnp.where(qpos_col >= NSA_CMP_LEN - 1, 1.0, 0.0)
        p = e * (has_block / jnp.sum(e, axis=-1, keepdims=True))
        o_c = jnp.dot(p.astype(BF16), vc_ref[0:ncols, :], preferred_element_type=F32)
        oc_ref[...] = _unstack_heads(_stacked_gate(jax.nn.sigmoid(gate_ref[...]), 0) * o_c)
        p_sum = p[0:Q]
        for h in range(1, H):
            p_sum = p_sum + p[h * Q:(h + 1) * Q]
        imp_ref[...] = _mm_f32(ovt_ref[:, 0:ncols], p_sum, exact="a", dims=nt_dims)

    tiles_needed = ((qb + 1) * Q // NSA_CMP_STRIDE + LANE - 1) // LANE
    for tiles in range(1, nc // LANE + 1):
        pl.when(tiles_needed == tiles)(functools.partial(attend, tiles * LANE))

    imp = imp_ref[...]
    blk = _iota((LANE, Q), 0)
    q_blk = (qb * Q + _iota((LANE, Q), 1)) >> int(math.log2(NSA_SLC_LEN))
    causal = blk <= q_blk
    for forced_blk in (0, q_blk, q_blk - 1):
        imp = jnp.where(blk == forced_blk, FORCED_SCORE, imp)
    imp = jnp.where(causal, imp, -1.0)
    imp = jnp.where(blk < n_slc, imp, -2.0)
    blk_f = blk.astype(F32)
    sel = jnp.zeros((LANE, Q), F32)
    for _ in range(top_n):
        m = jnp.max(imp, axis=0, keepdims=True)
        first = jnp.min(jnp.where(imp == m, blk_f, float(LANE)), axis=0, keepdims=True)
        hit = blk_f == first
        sel = jnp.where(hit, 1.0, sel)
        imp = jnp.where(hit, -3.0, imp)
    sel = jnp.where(causal, sel, 0.0).T
    selb_ref[...] = jnp.where(sel > 0.5, 0.0, NEG_INF).astype(BF16)
    any_ref[...] = jnp.max(sel, axis=0, keepdims=True)


def _nsa_select(p_nsa, qtab, kc, vc, overlap_t):
    B, S, _ = p_nsa.shape
    Q = NSA_Q
    nqb = S // Q
    nc = kc.shape[1]
    n_slc = S // NSA_SLC_LEN
    kern = functools.partial(_nsa_sel_kernel, n_slc=n_slc, top_n=min(NSA_TOPN, n_slc))
    return pl.pallas_call(
        kern, name="nsa_select",
        grid=(B, nqb),
        in_specs=[pl.BlockSpec((None, Q, NSA_HEADS * LANE), lambda b, i: (b, i, 0)),
                  pl.BlockSpec((2 * NSA_HEADS, LANE), lambda b, i: (0, 0)),
                  pl.BlockSpec((None, Q, LANE), lambda b, i: (b, i, (W_NSA - LANE) // LANE)),
                  pl.BlockSpec((None, nc, LANE), lambda b, i: (b, 0, 0)),
                  pl.BlockSpec((None, nc, LANE), lambda b, i: (b, 0, 0)),
                  pl.BlockSpec((LANE, nc), lambda b, i: (0, 0))],
        out_specs=[pl.BlockSpec((None, Q, GROUP_W), lambda b, i: (b, i, 0)),
                   pl.BlockSpec((None, Q, LANE), lambda b, i: (b, i, 0)),
                   pl.BlockSpec((None, None, 1, LANE), lambda b, i: (b, i, 0, 0))],
        out_shape=[jax.ShapeDtypeStruct((B, S, GROUP_W), F32),
                   jax.ShapeDtypeStruct((B, S, LANE), BF16),
                   jax.ShapeDtypeStruct((B, nqb, 1, LANE), F32)],
        scratch_shapes=[pltpu.VMEM((LANE, Q), F32)],
        compiler_params=_cparams("parallel", "parallel"),
    )(p_nsa, qtab, p_nsa, kc, vc, overlap_t)


def _nsa_attn_kernel(flags_ref, q_ref, qtab_ref, gate_ref, oc_ref, selb_ref, ks_ref, vs_ref, kw_ref, vw_ref,
                     o_ref, m_ref, acc_ref, ow_ref, *, nt):
    Q, H, TK = q_ref.shape[0], NSA_HEADS, NSA_TILE
    b = pl.program_id(0)
    qb = pl.program_id(1)
    nqb = pl.num_programs(1)
    qh = _nsa_queries(q_ref, qtab_ref, qb)
    nt_dims = (((1,), (1,)), ((), ()))

    selb = selb_ref[...]
    qs_sel = jnp.concatenate([jnp.concatenate([q, selb], axis=1) for q in qh], axis=0)
    sink_off = jnp.where(_iota(selb.shape, 1) < NSA_SINK // NSA_SLC_LEN, NEG_INF, selb.astype(F32)).astype(BF16)
    qs_loop = jnp.concatenate([jnp.concatenate([q, sink_off], axis=1) for q in qh], axis=0)
    m_ref[...] = jnp.full(m_ref.shape, NEG_INF, F32)
    acc_ref[...] = jnp.zeros_like(acc_ref)

    def online_update(m_prev, acc_prev, blocks):
        m_new = m_prev
        for s, _ in blocks:
            m_new = jnp.maximum(m_new, jnp.max(s, axis=-1, keepdims=True))
        acc = jnp.exp2(m_prev - m_new) * acc_prev
        for s, v in blocks:
            p = jnp.exp2(s - jnp.tile(m_new, (1, s.shape[1] // LANE)))
            acc = acc + jnp.dot(p.astype(BF16), v, preferred_element_type=F32)
        return m_new, acc

    def tile(t, carry):
        @pl.when(flags_ref[(b * nqb + qb) * nt + t] > 0)
        def _():
            rows = pl.ds(pl.multiple_of(t * TK, TK), TK)
            s = lax.dot_general(qs_loop, ks_ref[rows, :], nt_dims, preferred_element_type=F32)
            m_ref[...], acc_ref[...] = online_update(m_ref[...], acc_ref[...], [(s, vs_ref[rows, :])])
        return carry

    assert Q == TK == NSA_WIN
    half = Q // 2
    lax.fori_loop(0, qb, tile, 0)

    k_sink, v_sink = ks_ref[0:NSA_SINK, :], vs_ref[0:NSA_SINK, :]
    sink_bias = jnp.where(qb > 0, 0.0, NEG_INF)
    lower = _iota((H * half, half), 1) <= (_iota((H * half, half), 0) & (half - 1))
    for part in range(2):
        row_slices = [slice(h * Q + part * half, h * Q + (part + 1) * half) for h in range(H)]
        q_rows = jnp.concatenate([qs_sel[r] for r in row_slices], axis=0)
        keys = pl.ds(pl.multiple_of(qb * TK, TK), (part + 1) * half)
        s = lax.dot_general(q_rows, ks_ref[keys, :], nt_dims, preferred_element_type=F32)
        own = jnp.where(lower, s[:, part * half:], NEG_INF)
        s = own if part == 0 else jnp.concatenate([s[:, :half], own], axis=1)
        s_sink = lax.dot_general(q_rows, k_sink, nt_dims, preferred_element_type=F32) + sink_bias
        m_new, acc_new = online_update(jnp.concatenate([m_ref[r] for r in row_slices], axis=0),
                                       jnp.concatenate([acc_ref[r] for r in row_slices], axis=0),
                                       [(s, vs_ref[keys, :]), (s_sink, v_sink)])
        for h, r in enumerate(row_slices):
            m_ref[r] = m_new[h * half:(h + 1) * half]
            acc_ref[r] = acc_new[h * half:(h + 1) * half]


    def band(q_rows, first_key, n_keys, masks):
        keys = pl.ds(pl.multiple_of(first_key, half), n_keys)
        s = lax.dot_general(q_rows, kw_ref[keys, :], nt_dims, preferred_element_type=F32)
        groups = [s[:, g * half:(g + 1) * half] for g in range(n_keys // half)]
        s = jnp.concatenate([g if m is None else jnp.where(m, g, NEG_INF) for g, m in zip(groups, masks)], axis=1)
        p = jnp.exp2(s - jnp.max(s, axis=-1, keepdims=True))
        return _normalise(jnp.dot(p.astype(BF16), vw_ref[keys, :], preferred_element_type=F32))

    @pl.when(qb == 0)
    def _():
        qs_win = jnp.concatenate(qh, axis=0)
        row = _iota((H * Q, half), 0) & (Q - 1)
        col = _iota((H * Q, half), 1)
        ow_ref[...] = band(qs_win, 0, Q, [col <= row, col + half <= row])

    @pl.when(qb > 0)
    def _():
        row = _iota((H * half, half), 0) & (half - 1)
        col = _iota((H * half, half), 1)
        masks = [col > row, None, col <= row]
        for part in range(2):
            q_rows = jnp.concatenate([q[part * half:(part + 1) * half] for q in qh], axis=0)
            o_part = band(q_rows, (qb - 1) * Q + part * half, NSA_WIN + half, masks)
            for h in range(H):
                ow_ref[h * Q + part * half:h * Q + (part + 1) * half, :] = o_part[h * half:(h + 1) * half]

    gates = jax.nn.sigmoid(gate_ref[...])
    mixed = _stacked_gate(gates, 1) * _normalise(acc_ref[...]) + _stacked_gate(gates, 2) * ow_ref[...]
    o_ref[...] = (oc_ref[...] + _unstack_heads(mixed)).astype(o_ref.dtype)


def _nsa_attend(p_nsa, qtab, o_c, selb, flags, kv):
    B, S, _ = p_nsa.shape
    Q = NSA_Q
    nqb = S // Q
    nt = S // NSA_TILE
    gate_blk = (W_NSA - LANE) // LANE
    kern = functools.partial(_nsa_attn_kernel, nt=nt)
    slab = lambda width, col: pl.BlockSpec((None, S, width), lambda b, i, f: (b, 0, col))
    grid_spec = pltpu.PrefetchScalarGridSpec(
        num_scalar_prefetch=1,
        grid=(B, nqb),
        in_specs=[pl.BlockSpec((None, Q, NSA_HEADS * LANE), lambda b, i, f: (b, i, 0)),
                  pl.BlockSpec((2 * NSA_HEADS, LANE), lambda b, i, f: (0, 0)),
                  pl.BlockSpec((None, Q, LANE), lambda b, i, f: (b, i, gate_blk)),
                  pl.BlockSpec((None, Q, GROUP_W), lambda b, i, f: (b, i, 0)),
                  pl.BlockSpec((None, Q, LANE), lambda b, i, f: (b, i, 0)),
                  slab(2 * LANE, 0), slab(LANE, 2), slab(LANE, 3), slab(LANE, 4)],
        out_specs=pl.BlockSpec((None, Q, GROUP_W), lambda b, i, f: (b, i, 0)),
        scratch_shapes=[pltpu.VMEM((NSA_HEADS * Q, LANE), F32)] * 3,
    )
    return pl.pallas_call(
        kern, name="nsa_attend",
        grid_spec=grid_spec,
        out_shape=jax.ShapeDtypeStruct((B, S, GROUP_W), BF16),
        compiler_params=_cparams("parallel", "parallel"),
    )(flags, p_nsa, qtab, p_nsa, o_c, selb, kv, kv, kv, kv)


def _nsa_tables(S):
    nc = S // NSA_CMP_STRIDE
    n = np.arange(nc)[None, :]
    j = np.arange(LANE)[:, None]
    start = n * NSA_CMP_STRIDE
    ov = (start < (j + 1) * NSA_SLC_LEN) & (start + NSA_CMP_LEN - 1 >= j * NSA_SLC_LEN)
    ov &= (n < (S - NSA_CMP_LEN) // NSA_CMP_STRIDE + 1) & (j < S // NSA_SLC_LEN)
    pos = np.arange(S)
    k_zero = np.zeros((S, NSA_DK), np.float32)
    block_onehot = (pos[:, None] // NSA_SLC_LEN == np.arange(LANE)[None, :]).astype(np.float32)
    v_lanes = np.zeros((S, LANE), np.float32)
    v_lanes[:, ONES_LANE] = 1.0
    kv_table = np.concatenate([k_zero, _nsa_pos_lanes(pos), block_onehot, v_lanes,
                               k_zero, _nsa_pos_lanes(pos), v_lanes], axis=1)
    assert kv_table.shape[1] == W_KV
    return _nsa_query_table(), jnp.asarray(ov.astype(np.float32)), jnp.asarray(kv_table, dtype=BF16)


def _nsa(p_nsa, kv, uk, uv, pe_k, w1_k, w2_k, pe_v, w1_v, w2_v, tables):
    B, S, _ = p_nsa.shape
    qtab, overlap_t, _ = tables
    kc, vc = _nsa_compress(uk, uv, pe_k, w1_k, w2_k, pe_v, w1_v, w2_v)
    o_c, selb, blk_any = _nsa_select(p_nsa, qtab, kc, vc, overlap_t)
    per_tile = NSA_TILE // NSA_SLC_LEN
    nt = S // NSA_TILE
    not_sink = (np.arange(nt * per_tile) >= NSA_SINK // NSA_SLC_LEN).astype(np.float32)
    blk_any = blk_any[:, :, 0, :nt * per_tile] * not_sink
    flags = blk_any.reshape(B, S // NSA_Q, nt, per_tile).max(axis=-1)
    flags = (flags > 0).astype(jnp.int32).reshape(-1)
    return _nsa_attend(p_nsa, qtab, o_c, selb, flags, kv)


def _out_proj_kernel(h_ref, ya_ref, yb_ref, yc_ref, yd_ref, w_ref, g_ref, b_ref, o_ref, wb_ref):
    @pl.when(pl.program_id(0) == 0)
    def _():
        wb_ref[...] = w_ref[...].astype(BF16)

    mix = None
    for idx, y_ref in enumerate((ya_ref, yb_ref, yc_ref, yd_ref)):
        part = _mm(y_ref[...], wb_ref[idx * GROUP_W:(idx + 1) * GROUP_W, :])
        mix = part if mix is None else mix + part
    o_ref[...] = _layer_norm(DEEPNORM_ALPHA * h_ref[...] + mix, g_ref[...], b_ref[...])


def _out_proj(h2, ys, w_out, layer, g, b, tm=1024):
    T, D = h2.shape
    row = lambda w: pl.BlockSpec((tm, w), lambda i: (i, 0))
    const = lambda shape: pl.BlockSpec(shape, lambda i: (0,) * len(shape))
    return pl.pallas_call(
        _out_proj_kernel, name="out_proj_ln",
        grid=(T // tm,),
        in_specs=[row(D), row(GROUP_W), row(GROUP_W), row(GROUP_W), row(GROUP_W),
                  pl.BlockSpec((None, D, D), lambda i: (layer, 0, 0)), const((1, D)), const((1, D))],
        out_specs=row(D),
        out_shape=jax.ShapeDtypeStruct((T, D), F32),
        scratch_shapes=[pltpu.VMEM((D, D), BF16)],
        compiler_params=_cparams("arbitrary"),
    )(h2, *ys, w_out, g.reshape(1, D), b.reshape(1, D))


def _mlp_kernel(h_ref, w1_ref, w2_ref, g_ref, b_ref, o_ref, acc_ref):
    f = pl.program_id(1)

    @pl.when(f == 0)
    def _():
        acc_ref[...] = jnp.zeros_like(acc_ref)

    a = jnp.maximum(_mm(h_ref[...], w1_ref[...]), 0.0)
    acc_ref[...] += _mm(a * a, w2_ref[...])

    @pl.when(f == pl.num_programs(1) - 1)
    def _():
        o_ref[...] = _layer_norm(DEEPNORM_ALPHA * h_ref[...] + acc_ref[...], g_ref[...], b_ref[...])


def _mlp(h2, w1, w2, layer, g, b, tm=1024, tf=1024):
    T, D = h2.shape
    F = w1.shape[2]
    return pl.pallas_call(
        _mlp_kernel, name="mlp_ln",
        grid=(T // tm, F // tf),
        in_specs=[pl.BlockSpec((tm, D), lambda i, f: (i, 0)),
                  pl.BlockSpec((None, D, tf), lambda i, f: (layer, 0, f)),
                  pl.BlockSpec((None, tf, D), lambda i, f: (layer, f, 0)),
                  pl.BlockSpec((1, D), lambda i, f: (0, 0)),
                  pl.BlockSpec((1, D), lambda i, f: (0, 0))],
        out_specs=pl.BlockSpec((tm, D), lambda i, f: (i, 0)),
        out_shape=jax.ShapeDtypeStruct((T, D), F32),
        scratch_shapes=[pltpu.VMEM((tm, D), F32)],
        compiler_params=_cparams("parallel", "arbitrary"),
    )(h2, w1, w2, g.reshape(1, D), b.reshape(1, D))


def kernel(x, ln_emb_g, ln_emb_b, w_in, conv_w, conv_b, dt_bias, a_log, d_skip, ssm_norm_g, q_norm_g, w_uq, kv_norm_g, w_ukv, cmp_pe_k, cmp_w1_k, cmp_w2_k, cmp_pe_v, cmp_w1_v, cmp_w2_v, w_out, ln1_g, ln1_b, w_mlp1, w_mlp2, ln2_g, ln2_b):
    B, S, D = x.shape
    assert D == D_MODEL and S // NSA_SLC_LEN <= LANE
    assert S % NSA_TILE == 0 and S % NSA_Q == 0 and S % min(MLA_TILE, S) == 0 and S % SSM_CHUNK == 0
    T = B * S
    ret_tables = _ret_tables(S)
    mla_tables = _mla_tables(S)
    nsa_tables = _nsa_tables(S)
    h = x.reshape(T, D)
    for l in range(w_in.shape[0]):
        if l == 0:
            h, *proj = _in_proj(h, _layout_w_in(w_in[l]), nsa_tables[2], entry_ln=(ln_emb_g, ln_emb_b))
        else:
            proj = _in_proj(h, _layout_w_in(w_in[l]), nsa_tables[2])
        p_ssm, p_mla, p_ret, p_nsa, nsa_kv, uk, uv = proj
        cmp_rows = (B, S // NSA_CMP_STRIDE, uk.shape[-1])
        y_a, y_c = _ssm_and_retention(p_ssm.reshape(B, S, W_SSM), conv_w[l], conv_b[l], dt_bias[l], a_log[l],
                                      d_skip[l], ssm_norm_g[l], p_ret.reshape(B, S, W_RET), ret_tables)
        wq, wk, wv = _layout_mla_weights(w_uq[l], w_ukv[l])
        q, k, v = _mla_prep(p_mla.reshape(B, S, W_MLA), q_norm_g[l], kv_norm_g[l], wq, wk, wv, mla_tables)
        y_b = _mla_attn(q, k, v)
        y_d = _nsa(p_nsa.reshape(B, S, W_NSA), nsa_kv.reshape(B, S, W_KV), uk.reshape(cmp_rows), uv.reshape(cmp_rows),
                   cmp_pe_k[l], cmp_w1_k[l], cmp_w2_k[l], cmp_pe_v[l], cmp_w1_v[l], cmp_w2_v[l], nsa_tables)
        ys = [y.reshape(T, GROUP_W) for y in (y_a, y_b, y_c, y_d)]
        h = _out_proj(h, ys, w_out, l, ln1_g[l], ln1_b[l])
        h = _mlp(h, w_mlp1, w_mlp2, l, ln2_g[l], ln2_b[l])
    return h.reshape(B, S, D)
```

```python
import functools
import math

import jax
import jax.numpy as jnp
import numpy as np
from jax import lax
from jax.experimental import pallas as pl
from jax.experimental.pallas import tpu as pltpu

F32 = jnp.float32
BF16 = jnp.bfloat16

D_MODEL = 1024
DEPTH = 2
GROUP_W = D_MODEL // 4
SSM_HEADS = 4
SSM_HEAD_DIM = GROUP_W // SSM_HEADS
SSM_GROUPS = 2
SSM_STATE = 128
SSM_CONV = 4
SSM_CHUNK = 128
SSM_XBC = GROUP_W + 2 * SSM_GROUPS * SSM_STATE
MLA_HEADS = 4
MLA_NOPE = 64
MLA_ROPE = 32
MLA_V = GROUP_W // MLA_HEADS
MLA_Q_RANK = 256
MLA_KV_RANK = 128
RET_HEADS = 4
RET_DK = 64
RET_DV = GROUP_W // RET_HEADS
RET_CHUNK = 128
NSA_HEADS = 4
NSA_DK = 64
NSA_DV = GROUP_W // NSA_HEADS
NSA_CMP_LEN = 32
NSA_CMP_STRIDE = 16
NSA_CMP_HID = 256
NSA_SLC_LEN = 64
NSA_TOPN = 16
NSA_WIN = 512
D_FF = 4 * D_MODEL
NSA_Q = 512
ROPE_THETA = 10000.0
EPS = 1e-5
NEG_INF = -1e30
LOG2_E = math.log2(math.e)
FORCED_SCORE = 1e9
DEEPNORM_ALPHA = (2.0 * DEPTH) ** 0.25

IN_SPLITS = (
    GROUP_W, SSM_XBC, SSM_HEADS,
    MLA_Q_RANK, MLA_KV_RANK, MLA_ROPE,
    RET_HEADS * RET_DK, RET_HEADS * RET_DK, RET_HEADS * RET_DV, GROUP_W,
    NSA_HEADS * NSA_DK, NSA_DK, NSA_DV, NSA_DK, NSA_DV, NSA_DK, NSA_DV, 3 * NSA_HEADS,
)

LANE = 128
SUBLANE = 8
W_SSM = GROUP_W + SSM_XBC + LANE
W_MLA = MLA_Q_RANK + MLA_KV_RANK + LANE
W_RET = 4 * GROUP_W
W_NSA = NSA_HEADS * LANE + LANE + LANE
W_KV = 2 * LANE + 3 * LANE
W_PROJ = ((GROUP_W + SSM_XBC) + (MLA_Q_RANK + MLA_KV_RANK) + W_RET
          + (NSA_HEADS * NSA_DK + LANE) + 4 * NSA_DK + LANE)

RECURRENT_CHUNKS_PER_STEP = 4
NSA_TILE = 512
NSA_SINK = 128
MLA_TILE = 1024
VMEM_LIMIT = 48 * 1024 * 1024


def _cparams(*sem):
    return pltpu.CompilerParams(dimension_semantics=sem, vmem_limit_bytes=VMEM_LIMIT)


def _mm(a, b):
    return jnp.dot(a.astype(BF16), b.astype(BF16), preferred_element_type=F32)


def _mm_nt(a, b):
    return lax.dot_general(a.astype(BF16), b.astype(BF16), (((1,), (1,)), ((), ())),
                           preferred_element_type=F32)


def _split_f32(x):
    hi = x.astype(BF16)
    rest = x - hi.astype(F32)
    mid = rest.astype(BF16)
    lo = (rest - mid.astype(F32)).astype(BF16)
    return hi, mid, lo


def _mm_f32(a, b, exact, dims=(((1,), (0,)), ((), ()))):
    fixed, pieces = (a.astype(BF16), _split_f32(b)) if exact == "a" else (b.astype(BF16), _split_f32(a))
    out = None
    for piece in pieces:
        lhs, rhs = (fixed, piece) if exact == "a" else (piece, fixed)
        part = lax.dot_general(lhs, rhs, dims, preferred_element_type=F32)
        out = part if out is None else out + part
    return out


def _silu(x):
    return x * jax.nn.sigmoid(x)


def _softplus(x):
    return jnp.maximum(x, 0.0) + jnp.log1p(jnp.exp(-jnp.abs(x)))


def _layer_norm(x, g, b):
    mu = jnp.mean(x, axis=-1, keepdims=True)
    xc = x - mu
    var = jnp.mean(xc * xc, axis=-1, keepdims=True)
    return xc * lax.rsqrt(var + EPS) * g + b


def _iota(shape, dim):
    return lax.broadcasted_iota(jnp.int32, shape, dim)


def _pad_cols(w, width):
    return jnp.pad(w, ((0, 0), (0, width - w.shape[1])))


def _layout_w_in(w):
    offs = np.concatenate([[0], np.cumsum(IN_SPLITS)])
    p = [w[:, int(offs[i]):int(offs[i + 1])] for i in range(len(IN_SPLITS))]
    (ssm_z, ssm_xbc, ssm_dt, mla_cq, mla_ckv, mla_kr, ret_q, ret_k, ret_v, ret_g,
     nsa_q, nsa_kc, nsa_vc, nsa_ks, nsa_vs, nsa_kw, nsa_vw, nsa_gate) = p
    small = _pad_cols(jnp.concatenate([mla_kr, ssm_dt, nsa_gate], axis=1), LANE)
    cols = [ssm_z, ssm_xbc, ret_q, ret_k, ret_v, ret_g, nsa_q, nsa_ks, nsa_vs, nsa_kw, nsa_vw,
            mla_cq, mla_ckv, nsa_kc, nsa_vc, small]
    out = jnp.concatenate(cols, axis=1)
    assert out.shape[1] == W_PROJ
    return out.astype(BF16)


def _in_proj_kernel(*refs, entry_ln):
    if entry_ln:
        h_ref, g_ref, b_ref, w_ref, kvtab_ref, hn_ref, *outs = refs
        hn = _layer_norm(h_ref[...], g_ref[...], b_ref[...])
        hn_ref[...] = hn
    else:
        h_ref, w_ref, kvtab_ref, *outs = refs
        hn = h_ref[...]
    ssm_ref, mla_ref, ret_ref, nsa_ref, kv_ref, uk_ref, uv_ref, kcv_ref = outs
    hb = hn.astype(BF16)
    tm = hb.shape[0]
    off = 0

    def project(width):
        nonlocal off
        out = jnp.dot(hb, w_ref[:, off:off + width], preferred_element_type=F32)
        off += width
        return out

    ssm_ref[:, 0:GROUP_W + SSM_XBC] = project(GROUP_W + SSM_XBC)
    ret_ref[...] = project(W_RET)
    q = project(NSA_HEADS * NSA_DK)
    kv = project(4 * NSA_DK)
    latent = MLA_Q_RANK + MLA_KV_RANK
    mla_kc = project(latent + LANE)
    mla_ref[:, 0:latent] = mla_kc[:, 0:latent]
    kc_lane = NSA_HEADS * LANE
    nsa_ref[:, kc_lane:kc_lane + LANE] = mla_kc[:, latent:]
    small = project(LANE)
    lane = _iota((tm, LANE), 1)
    low = lane < NSA_DK
    for h in range(NSA_HEADS):
        pair = q[:, (h // 2) * LANE:(h // 2 + 1) * LANE]
        head = pair if h % 2 == 0 else pltpu.roll(pair, NSA_DK, 1)
        nsa_ref[:, h * LANE:(h + 1) * LANE] = jnp.where(low, head, 0.0)
    mla_ref[:, MLA_Q_RANK + MLA_KV_RANK:] = jnp.where(lane < MLA_ROPE, small, 0.0)
    ssm_ref[:, GROUP_W + SSM_XBC:] = jnp.where(lane < SSM_HEADS, pltpu.roll(small, LANE - MLA_ROPE, 1), 0.0)
    gate_at = MLA_ROPE + SSM_HEADS
    nsa_ref[:, kc_lane + LANE:] = jnp.where(lane < 3 * NSA_HEADS, pltpu.roll(small, LANE - gate_at, 1), 0.0)
    groups = h_ref.shape[0] // NSA_CMP_STRIDE
    kc_lane = NSA_HEADS * LANE
    kcv_ref[...] = nsa_ref[:, kc_lane:kc_lane + LANE]
    for t in range(NSA_CMP_STRIDE):
        piece = kcv_ref[pl.ds(t, groups, stride=NSA_CMP_STRIDE), :]
        uk_ref[:, t * NSA_DK:(t + 1) * NSA_DK] = piece[:, :NSA_DK]
        uv_ref[:, t * NSA_DV:(t + 1) * NSA_DV] = piece[:, NSA_DK:]
    sel_kv, win_kv = kv[:, :LANE], kv[:, LANE:]
    pieces = {0: sel_kv, 2: pltpu.roll(sel_kv, NSA_DK, 1), 3: win_kv, 4: pltpu.roll(win_kv, NSA_DK, 1)}
    for slab in range(W_KV // LANE):
        lanes = slice(slab * LANE, (slab + 1) * LANE)
        tab = kvtab_ref[:, lanes]
        if slab in pieces:
            kv_ref[:, lanes] = (jnp.where(low, pieces[slab], 0.0) + tab.astype(F32)).astype(BF16)
        else:
            kv_ref[:, lanes] = tab


def _in_proj(h2, w_p, kv_table, entry_ln=None, tm=512):
    T, D = h2.shape
    S = kv_table.shape[0]
    widths = (W_SSM, W_MLA, W_RET, W_NSA)
    half = NSA_CMP_STRIDE * NSA_DK
    row = lambda w: pl.BlockSpec((tm, w), lambda i: (i, 0))
    const = lambda shape: pl.BlockSpec(shape, lambda i: (0,) * len(shape))
    in_specs = [const((D, W_PROJ)), pl.BlockSpec((tm, W_KV), lambda i: (i % (S // tm), 0))]
    out_specs = [row(w) for w in widths + (W_KV,)] + [pl.BlockSpec((tm // NSA_CMP_STRIDE, half), lambda i: (i, 0))] * 2
    out_shape = ([jax.ShapeDtypeStruct((T, w), F32) for w in widths] + [jax.ShapeDtypeStruct((T, W_KV), BF16)]
                 + [jax.ShapeDtypeStruct((T // NSA_CMP_STRIDE, half), F32)] * 2)
    operands = (w_p, kv_table)
    if entry_ln is not None:
        in_specs = [const((1, D)), const((1, D))] + in_specs
        out_specs = [row(D)] + out_specs
        out_shape = [jax.ShapeDtypeStruct((T, D), F32)] + out_shape
        operands = tuple(v.reshape(1, D) for v in entry_ln) + operands
    return pl.pallas_call(
        functools.partial(_in_proj_kernel, entry_ln=entry_ln is not None), name="in_proj",
        grid=(T // tm,),
        in_specs=[row(D)] + in_specs,
        out_specs=out_specs,
        out_shape=out_shape,
        scratch_shapes=[pltpu.VMEM((tm, LANE), F32)],
        compiler_params=_cparams("parallel"),
    )(h2, *operands)


def _ssm_chunk(p_ref, cw_ref, cb_ref, dtb_ref, alog_ref, dskip_ref, ng_ref, o_ref, state_ref, ext_ref):
    L, H, P, N = SSM_CHUNK, SSM_HEADS, SSM_HEAD_DIM, SSM_STATE
    z = p_ref[:, 0:GROUP_W]
    ext_ref[SUBLANE:SUBLANE + L, :] = p_ref[:, GROUP_W:GROUP_W + SSM_XBC]
    ext = ext_ref[...]
    conv = cb_ref[...] + ext[SUBLANE:] * cw_ref[SSM_CONV - 1:SSM_CONV, :]
    for j in range(SSM_CONV - 1):
        shift = SSM_CONV - 1 - j
        conv = conv + pltpu.roll(ext, shift, 0)[SUBLANE:] * cw_ref[j:j + 1, :]
    ext_ref[0:SUBLANE, :] = ext_ref[L:L + SUBLANE, :]
    xbc = _silu(conv)
    xs = xbc[:, 0:GROUP_W]
    b_in = xbc[:, GROUP_W:GROUP_W + SSM_GROUPS * N]
    c_in = xbc[:, GROUP_W + SSM_GROUPS * N:]

    dt = _softplus(p_ref[:, GROUP_W + SSM_XBC:] + dtb_ref[...])
    a = dt * (-jnp.exp(alog_ref[...]))
    row = _iota((L, L), 0)
    col = _iota((L, L), 1)
    tril = col <= row
    cs = _mm_f32(jnp.where(tril, 1.0, 0.0), a, exact="a")
    cs_t = cs.T
    ecs = jnp.exp(cs)
    dte = jnp.exp(cs[L - 1:L, :] - cs)
    first_head_of_pair = _iota((L, LANE), 1) < P

    def expand(x):
        pairs = [jnp.where(first_head_of_pair, x[:, h:h + 1], x[:, h + 1:h + 2]) for h in range(0, H, LANE // P)]
        return jnp.concatenate(pairs, axis=1)

    assert LANE == 2 * P
    dt_x = expand(dt)
    ecs_x = expand(ecs)
    dte_x = expand(dte)

    xdt = xs * dt_x
    wx = xdt * dte_x
    head_of_lane = _iota((L, H * P), 1) // P
    y = xs * dskip_ref[...]
    y_off = []
    rep = H // SSM_GROUPS
    for g in range(SSM_GROUPS):
        cg = c_in[:, g * N:(g + 1) * N]
        bg = b_in[:, g * N:(g + 1) * N]
        cb = _mm_nt(cg, bg)
        for h in range(g * rep, (g + 1) * rep):
            diff = cs[:, h:h + 1] - cs_t[h:h + 1, :]
            seg = jnp.where(tril, jnp.exp(jnp.where(tril, diff, 0.0)), 0.0)
            yh = _mm(cb * seg, xdt)
            y = y + jnp.where(head_of_lane == h, yh, 0.0)
        lanes = slice(g * rep * P, (g + 1) * rep * P)
        st_prev = state_ref[:, lanes]
        y_off.append(_mm(cg, st_prev))
        state_ref[:, lanes] = st_prev * ecs_x[L - 1:L, lanes] + _mm(bg.T, wx[:, lanes])
    y = y + jnp.concatenate(y_off, axis=1) * ecs_x
    y = y * _silu(z)
    ms = jnp.mean(y * y, axis=-1, keepdims=True)
    o_ref[...] = (y * lax.rsqrt(ms + EPS) * ng_ref[...]).astype(o_ref.dtype)


def _recurrent_kernel(ps_ref, cw_ref, cb_ref, dtb_ref, alog_ref, dskip_ref, ng_ref,
                      pr_ref, cos_ref, sin_ref, dec_ref, zeta_ref, xi_ref, cd_ref,
                      oa_ref, oc_ref, sstate_ref, ext_ref, rstate_ref):
    @pl.when(pl.program_id(0) == 0)
    def _():
        sstate_ref[...] = jnp.zeros_like(sstate_ref)
        rstate_ref[...] = jnp.zeros_like(rstate_ref)
        ext_ref[:, 0:SUBLANE, :] = jnp.zeros((ext_ref.shape[0], SUBLANE, SSM_XBC), F32)

    L = SSM_CHUNK
    for sub in range(ps_ref.shape[1] // L):
        rows = pl.ds(sub * L, L)
        for b in range(ps_ref.shape[0]):
            _ssm_chunk(ps_ref.at[b, rows], cw_ref, cb_ref, dtb_ref, alog_ref, dskip_ref, ng_ref,
                       oa_ref.at[b, rows], sstate_ref.at[b], ext_ref.at[b])
            _ret_chunk(pr_ref.at[b, rows], cos_ref.at[rows], sin_ref.at[rows], dec_ref, zeta_ref, xi_ref, cd_ref,
                       oc_ref.at[b, rows], rstate_ref.at[b])


def _ret_chunk(p_ref, cos_ref, sin_ref, dec_ref, zeta_ref, xi_ref, cd_ref, o_ref, state_ref):
    L, H, DK, DV = RET_CHUNK, RET_HEADS, RET_DK, RET_DV
    W = H * DK
    q = p_ref[:, 0:W]
    k = p_ref[:, W:2 * W]
    v = p_ref[:, 2 * W:3 * W]
    gate = p_ref[:, 3 * W:4 * W]
    lane = _iota((L, W), 1)
    first_half = (lane % DK) < (DK // 2)
    head_of_lane = lane // DK

    def rope(x):
        partner = jnp.where(first_half, pltpu.roll(x, W - DK // 2, 1), pltpu.roll(x, DK // 2, 1))
        return x * cos_ref[...] + partner * sin_ref[...]

    qr = rope(q)
    kr = rope(k) * (DK ** -0.5)
    y = jnp.zeros((L, H * DV), F32)
    for h in range(H):
        qh = jnp.where(head_of_lane == h, qr, 0.0)
        sc = _mm_nt(qh, kr) * dec_ref[h]
        y = y + jnp.where(head_of_lane == h, _mm(sc, v), 0.0)
    st = state_ref[...]
    y = y + _mm(qr * xi_ref[...], st)
    same_head = (_iota((W, H * DV), 0) // DK) == (_iota((W, H * DV), 1) // DV)
    kv = _mm((kr * zeta_ref[...]).T, v)
    state_ref[...] = st * cd_ref[...] + jnp.where(same_head, kv, 0.0)
    assert DV & (DV - 1) == 0
    ms = _mm_f32(y * y, jnp.where(same_head, 1.0 / DV, 0.0), exact="b")
    o_ref[...] = (y * lax.rsqrt(ms + EPS) * _silu(gate)).astype(o_ref.dtype)


def _ret_tables(S):
    H, DK, L = RET_HEADS, RET_DK, RET_CHUNK
    inv = ROPE_THETA ** (-np.arange(0, DK, 2, dtype=np.float64) / DK)
    ang = np.arange(S, dtype=np.float64)[:, None] * inv[None, :]
    cos, sin = np.cos(ang), np.sin(ang)
    cos_t = np.tile(np.concatenate([cos, cos], axis=1), (1, H))
    sin_t = np.tile(np.concatenate([-sin, sin], axis=1), (1, H))
    log_gamma = np.log1p(-np.exp2(-5.0 - np.arange(H, dtype=np.float64)))
    pos = np.arange(L, dtype=np.float64)
    diff = pos[:, None] - pos[None, :]
    decay_in = np.where(diff >= 0, np.exp(np.maximum(diff, 0.0)[None] * log_gamma[:, None, None]), 0.0)
    zeta = np.exp((L - 1 - pos)[None] * log_gamma[:, None])
    xi = np.exp((pos + 1.0)[None] * log_gamma[:, None])
    chunk_decay = np.exp(L * log_gamma)
    zeta_x = np.repeat(zeta.T, DK, axis=1)
    xi_x = np.repeat(xi.T, DK, axis=1)
    cd_x = np.repeat(chunk_decay, RET_DV).reshape(1, H * RET_DV)
    return tuple(jnp.asarray(t, dtype=F32) for t in (cos_t, sin_t, decay_in, zeta_x, xi_x, cd_x))


def _ssm_and_retention(p_ssm, conv_w, conv_b, dt_bias, a_log, d_skip, norm_g, p_ret, tables):
    B, S, _ = p_ssm.shape
    L, H = SSM_CHUNK, RET_HEADS
    assert RET_CHUNK == L
    W = H * RET_DK
    cos_t, sin_t, decay_in, zeta_x, xi_x, cd_x = tables
    pad_h = lambda v: jnp.pad(v, (0, LANE - SSM_HEADS)).reshape(1, LANE)
    const = lambda shape: pl.BlockSpec(shape, lambda c: (0,) * len(shape))
    rows = RECURRENT_CHUNKS_PER_STEP * L
    assert S % rows == 0
    chunk = lambda width: pl.BlockSpec((B, rows, width), lambda c: (0, c, 0))
    out_shape = jax.ShapeDtypeStruct((B, S, GROUP_W), BF16)
    return pl.pallas_call(
        _recurrent_kernel, name="ssm_retention",
        grid=(S // rows,),
        in_specs=[chunk(W_SSM),
                  const((SSM_CONV, SSM_XBC)), const((1, SSM_XBC)), const((1, LANE)), const((1, LANE)),
                  const((1, GROUP_W)), const((1, GROUP_W)),
                  chunk(W_RET),
                  pl.BlockSpec((rows, W), lambda c: (c, 0)), pl.BlockSpec((rows, W), lambda c: (c, 0)),
                  const((H, L, L)), const((L, W)), const((L, W)), const((1, H * RET_DV))],
        out_specs=[chunk(GROUP_W), chunk(GROUP_W)],
        out_shape=[out_shape, out_shape],
        scratch_shapes=[pltpu.VMEM((B, SSM_STATE, GROUP_W), F32),
                        pltpu.VMEM((B, L + SUBLANE, SSM_XBC), F32),
                        pltpu.VMEM((B, W, H * RET_DV), F32)],
        compiler_params=_cparams("arbitrary"),
    )(p_ssm, conv_w, conv_b.reshape(1, -1), pad_h(dt_bias), pad_h(a_log),
      jnp.repeat(d_skip, SSM_HEAD_DIM).reshape(1, GROUP_W), norm_g.reshape(1, GROUP_W),
      p_ret, cos_t, sin_t, decay_in, zeta_x, xi_x, cd_x)


def _mla_prep_kernel(p_ref, qg_ref, wq_ref, kvg_ref, wk_ref, wv_ref, cos_ref, sin_ref,
                     q_ref, k_ref, v_ref):
    tm = p_ref.shape[0]
    cq = p_ref[:, 0:MLA_Q_RANK]
    ckv = p_ref[:, MLA_Q_RANK:MLA_Q_RANK + MLA_KV_RANK]
    kr = p_ref[:, MLA_Q_RANK + MLA_KV_RANK:]

    def rms(x, g):
        return x * lax.rsqrt(jnp.mean(x * x, axis=-1, keepdims=True) + EPS) * g

    q = _mm(rms(cq, qg_ref[...]), wq_ref[...])
    kvl = rms(ckv, kvg_ref[...])
    kn = _mm(kvl, wk_ref[...])
    vv = _mm(kvl, wv_ref[...])
    kr_sh = pltpu.roll(kr, MLA_NOPE, 1)
    lane = _iota((tm, LANE), 1)
    half = MLA_ROPE // 2
    low = (lane >= MLA_NOPE) & (lane < MLA_NOPE + half)
    cos = cos_ref[...]
    sin = sin_ref[...]

    def rope(x):
        partner = jnp.where(low, pltpu.roll(x, LANE - half, 1), pltpu.roll(x, half, 1))
        return x * cos + partner * sin

    scale = (MLA_NOPE + MLA_ROPE) ** -0.5 * LOG2_E
    for h in range(MLA_HEADS):
        sl = slice(h * LANE, (h + 1) * LANE)
        q_ref[h] = (rope(q[:, sl]) * scale).astype(BF16)
        k_ref[h] = rope(kn[:, sl] + kr_sh).astype(BF16)
        v_ref[h] = jnp.where(lane == _mla_ones_lane(h), 1.0, vv[:, sl]).astype(BF16)


def _mla_tables(S):
    inv = ROPE_THETA ** (-np.arange(0, MLA_ROPE, 2, dtype=np.float64) / MLA_ROPE)
    ang = np.arange(S, dtype=np.float64)[:, None] * inv[None, :]
    cos, sin = np.cos(ang), np.sin(ang)
    tail = LANE - MLA_NOPE - MLA_ROPE
    cos_t = np.concatenate([np.ones((S, MLA_NOPE)), cos, cos, np.ones((S, tail))], axis=1)
    sin_t = np.concatenate([np.zeros((S, MLA_NOPE)), -sin, sin, np.zeros((S, tail))], axis=1)
    return jnp.asarray(cos_t, dtype=F32), jnp.asarray(sin_t, dtype=F32)


def _layout_mla_weights(w_uq, w_ukv):
    H = MLA_HEADS
    dq = MLA_NOPE + MLA_ROPE
    wq = jnp.concatenate([_pad_cols(w_uq[:, h * dq:(h + 1) * dq], LANE) for h in range(H)], axis=1)
    dkv = MLA_NOPE + MLA_V
    wk, wv = [], []
    for h in range(H):
        blk = w_ukv[:, h * dkv:(h + 1) * dkv]
        wk.append(_pad_cols(blk[:, :MLA_NOPE], LANE))
        v = blk[:, MLA_NOPE:]
        zero = jnp.zeros_like(v)
        wv.append(jnp.concatenate([v, zero] if h % 2 == 0 else [zero, v], axis=1))
    return wq.astype(BF16), jnp.concatenate(wk, axis=1).astype(BF16), jnp.concatenate(wv, axis=1).astype(BF16)


def _mla_prep(p_mla, q_norm_g, kv_norm_g, wq, wk, wv, tables, tm=1024):
    B, S, _ = p_mla.shape
    H = MLA_HEADS
    cos_t, sin_t = tables
    const = lambda shape: pl.BlockSpec(shape, lambda b, i: (0,) * len(shape))
    qkv_spec = pl.BlockSpec((None, H, tm, LANE), lambda b, i: (b, 0, i, 0))
    qkv_shape = jax.ShapeDtypeStruct((B, H, S, LANE), BF16)
    return pl.pallas_call(
        _mla_prep_kernel, name="mla_prep",
        grid=(B, S // tm),
        in_specs=[pl.BlockSpec((None, tm, W_MLA), lambda b, i: (b, i, 0)),
                  const((1, MLA_Q_RANK)), const((MLA_Q_RANK, H * LANE)),
                  const((1, MLA_KV_RANK)), const((MLA_KV_RANK, H * LANE)), const((MLA_KV_RANK, H * LANE)),
                  pl.BlockSpec((tm, LANE), lambda b, i: (i, 0)),
                  pl.BlockSpec((tm, LANE), lambda b, i: (i, 0))],
        out_specs=[qkv_spec, qkv_spec, qkv_spec],
        out_shape=[qkv_shape, qkv_shape, qkv_shape],
        compiler_params=_cparams("parallel", "parallel"),
    )(p_mla, q_norm_g.reshape(1, -1), wq, kv_norm_g.reshape(1, -1), wk, wv, cos_t, sin_t)


def _mla_ones_lane(h):
    return MLA_V if h % 2 == 0 else 0


def _mla_attn_kernel(qi_ref, kj_ref, q_ref, k_ref, v_ref, o_ref, m_ref, acc_ref):
    H = MLA_HEADS
    tq, tk = q_ref.shape[1], k_ref.shape[1]
    i = qi_ref[pl.program_id(1)]
    j = kj_ref[pl.program_id(1)]

    @pl.when(j == 0)
    def _():
        m_ref[...] = jnp.full(m_ref.shape, NEG_INF, F32)
        acc_ref[...] = jnp.zeros_like(acc_ref)

    def sweep(blocks):
        nt_dims = (((1,), (1,)), ((), ()))
        scores = [[lax.dot_general(q_ref[h, r0:r0 + nr, :], k_ref[h, 0:nk, :], nt_dims, preferred_element_type=F32)
                   for (r0, nr, nk, _) in blocks] for h in range(H)]
        for h in range(H):
            for (r0, nr, nk, offset), s in zip(blocks, scores[h]):
                if offset is not None:
                    s = jnp.where(_iota((nr, nk), 1) - _iota((nr, nk), 0) <= offset, s, NEG_INF)
                rows = slice(r0, r0 + nr)
                m_prev = m_ref[h, rows]
                m_new = jnp.maximum(m_prev, jnp.max(s, axis=-1, keepdims=True))
                p = jnp.exp2(s - jnp.tile(m_new, (1, nk // LANE)))
                acc_ref[h, rows] = (jnp.exp2(m_prev - m_new) * acc_ref[h, rows]
                                    + jnp.dot(p.astype(BF16), v_ref[h, 0:nk, :], preferred_element_type=F32))
                m_ref[h, rows] = m_new

    assert tq == tk
    half = tq // 2

    @pl.when(j < i)
    def _():
        sweep([(0, tq, tk, None)])

    @pl.when(j == i)
    def _():
        sweep([(0, half, half, 0), (half, half, tk, half)])
        lane = _iota((tq, LANE), 1)
        for pair in range(H // 2):
            he, ho = 2 * pair, 2 * pair + 1
            acc_e, acc_o = acc_ref[he], acc_ref[ho]
            le = acc_e[:, _mla_ones_lane(he):_mla_ones_lane(he) + 1]
            lo = acc_o[:, _mla_ones_lane(ho):_mla_ones_lane(ho) + 1]
            o_ref[:, pair * LANE:(pair + 1) * LANE] = jnp.where(lane < MLA_V, acc_e / le, acc_o / lo).astype(o_ref.dtype)


def _mla_attn(q, k, v):
    B, H, S, _ = q.shape
    t = min(MLA_TILE, S)
    tq = t
    pairs = [(i, j) for i in range(S // tq) for j in range((i + 1) * tq // t)]
    qi = jnp.asarray([p[0] for p in pairs], jnp.int32)
    kj = jnp.asarray([p[1] for p in pairs], jnp.int32)
    grid_spec = pltpu.PrefetchScalarGridSpec(
        num_scalar_prefetch=2,
        grid=(B, len(pairs)),
        in_specs=[pl.BlockSpec((None, H, tq, LANE), lambda b, p, qi, kj: (b, 0, qi[p], 0)),
                  pl.BlockSpec((None, H, t, LANE), lambda b, p, qi, kj: (b, 0, kj[p], 0)),
                  pl.BlockSpec((None, H, t, LANE), lambda b, p, qi, kj: (b, 0, kj[p], 0))],
        out_specs=pl.BlockSpec((None, tq, GROUP_W), lambda b, p, qi, kj: (b, qi[p], 0)),
        scratch_shapes=[pltpu.VMEM((H, tq, LANE), F32), pltpu.VMEM((H, tq, LANE), F32)],
    )
    return pl.pallas_call(
        _mla_attn_kernel, name="mla_attn",
        grid_spec=grid_spec,
        out_shape=jax.ShapeDtypeStruct((B, S, GROUP_W), BF16),
        compiler_params=_cparams("parallel", "arbitrary"),
    )(qi, kj, q, k, v)


POS_HI = NSA_DK
POS_LO = NSA_DK + 3
POS_ONE = NSA_DK + 6
ONES_LANE = NSA_DV


def _split_bf16(x, parts=3):
    out, rem = [], np.float64(x)
    for _ in range(parts):
        piece = np.float64(np.float32(rem).astype(jnp.bfloat16).astype(np.float32))
        out.append(float(piece))
        rem = rem - piece
    return out


def _nsa_query_table():
    H = NSA_HEADS
    tab = np.zeros((2 * H, LANE), np.float32)
    for h in range(H):
        c = 2.0 ** (-8.0 * (h + 1) / H) * LOG2_E
        pieces = _split_bf16(c)
        tab[h, POS_HI:POS_HI + 3] = pieces
        tab[h, POS_LO:POS_LO + 3] = pieces
        tab[H + h, POS_ONE] = -sum(pieces)
    return jnp.asarray(tab)


def _nsa_pos_lanes(pos, lo_offset=0.0):
    t = np.zeros((len(pos), LANE - NSA_DK), np.float32)
    t[:, POS_HI - NSA_DK:POS_HI - NSA_DK + 3] = (NSA_SLC_LEN * (pos // NSA_SLC_LEN))[:, None]
    t[:, POS_LO - NSA_DK:POS_LO - NSA_DK + 3] = (pos % NSA_SLC_LEN + lo_offset)[:, None]
    t[:, POS_ONE - NSA_DK] = 1.0
    return t


def _nsa_queries(q_ref, qtab_ref, qb):
    Q, H = q_ref.shape[0], NSA_HEADS
    qpos = (qb * Q + _iota((Q, 1), 0)).astype(F32)
    out = []
    for h in range(H):
        q = q_ref[:, h * LANE:(h + 1) * LANE] * (NSA_DK ** -0.5 * LOG2_E)
        out.append((q + qtab_ref[h:h + 1, :] + qtab_ref[H + h:H + h + 1, :] * qpos).astype(BF16))
    return out


def _normalise(o):
    return o / o[:, ONES_LANE:ONES_LANE + 1]


def _stacked_gate(gates, branch):
    lanes = [3 * h + branch for h in range(NSA_HEADS)]
    return jnp.concatenate([gates[:, c:c + 1] for c in lanes], axis=0)


def _unstack_heads(o):
    Q = o.shape[0] // NSA_HEADS
    lane = _iota((Q, LANE), 1)
    out = []
    for pair in range(NSA_HEADS // 2):
        even = o[(2 * pair) * Q:(2 * pair + 1) * Q]
        odd = o[(2 * pair + 1) * Q:(2 * pair + 2) * Q]
        out.append(jnp.where(lane < NSA_DV, even, pltpu.roll(odd, NSA_DV, 1)))
    return jnp.concatenate(out, axis=1)


def _nsa_cmp_kernel(uk_ref, uv_ref, pek_ref, pev_ref, w1k_ref, w1v_ref, w2k_ref, w2v_ref, cpos_ref,
                    kc_ref, vc_ref, sh_ref):
    nb = uk_ref.shape[0]
    half = uk_ref.shape[1]

    def hidden(u_ref, pe_ref, w1_ref):
        u = u_ref[...]
        first = _mm(u + pe_ref[0:1, :], w1_ref[0:half, :])
        second = _mm(u + pe_ref[1:2, :], w1_ref[half:2 * half, :])
        sh_ref[0:nb, :] = second
        sh_ref[nb:nb + SUBLANE, :] = jnp.zeros((SUBLANE, NSA_CMP_HID), F32)
        return first + sh_ref[pl.ds(1, nb), :]

    hk = _silu(hidden(uk_ref, pek_ref, w1k_ref))
    hv = _silu(hidden(uv_ref, pev_ref, w1v_ref))
    kc_ref[...] = (_mm(hk, w2k_ref[...]) + cpos_ref[...]).astype(BF16)
    ones_lane = jnp.where(_iota((1, LANE), 1) == ONES_LANE, 1.0, 0.0)
    vc_ref[...] = (_mm(hv, w2v_ref[...]) + ones_lane).astype(BF16)


def _nsa_compress(uk, uv, pe_k, w1_k, w2_k, pe_v, w1_v, w2_v):
    B, nb, half = uk.shape
    hid = NSA_CMP_HID
    const = lambda shape: pl.BlockSpec(shape, lambda b: (0,) * len(shape))
    w2k = _pad_cols(w2_k, LANE).astype(BF16)
    w2v = _pad_cols(w2_v, LANE).astype(BF16)
    centre = _nsa_pos_lanes(np.arange(nb) * NSA_CMP_STRIDE, 0.5 * (NSA_CMP_LEN - 1))
    cpos = jnp.asarray(np.concatenate([np.zeros((nb, NSA_DK), np.float32), centre], axis=1))
    out_spec = pl.BlockSpec((None, nb, LANE), lambda b: (b, 0, 0))
    out_shape = jax.ShapeDtypeStruct((B, nb, LANE), BF16)
    return pl.pallas_call(
        _nsa_cmp_kernel, name="nsa_compress",
        grid=(B,),
        in_specs=[pl.BlockSpec((None, nb, half), lambda b: (b, 0, 0)),
                  pl.BlockSpec((None, nb, half), lambda b: (b, 0, 0)),
                  const((2, half)), const((2, half)),
                  const((2 * half, hid)), const((2 * half, hid)),
                  const((hid, LANE)), const((hid, LANE)), const((nb, LANE))],
        out_specs=[out_spec, out_spec],
        out_shape=[out_shape, out_shape],
        scratch_shapes=[pltpu.VMEM((nb + SUBLANE, hid), F32)],
        compiler_params=_cparams("parallel"),
    )(uk, uv, pe_k.reshape(2, half), pe_v.reshape(2, half), w1_k.astype(BF16), w1_v.astype(BF16), w2k, w2v, cpos)


def _nsa_sel_kernel(q_ref, qtab_ref, gate_ref, kc_ref, vc_ref, ovt_ref, oc_ref, selb_ref, any_ref, imp_ref, *, n_slc, top_n):
    Q, H = q_ref.shape[0], NSA_HEADS
    qb = pl.program_id(1)
    nc = kc_ref.shape[0]
    qs = jnp.concatenate(_nsa_queries(q_ref, qtab_ref, qb), axis=0)
    nt_dims = (((1,), (1,)), ((), ()))

    def attend(ncols):
        s = lax.dot_general(qs, kc_ref[0:ncols, :], nt_dims, preferred_element_type=F32)
        qpos = qb * Q + (_iota((H * Q, ncols), 0) & (Q - 1))
        block_end = _iota((H * Q, ncols), 1) * NSA_CMP_STRIDE + (NSA_CMP_LEN - 1)
        s = jnp.where(block_end <= qpos, s, NEG_INF)
        e = jnp.exp2(s - jnp.max(s, axis=-1, keepdims=True))
        qpos_col = qb * Q + (_iota((H * Q, 1), 0) & (Q - 1))
        has_block = jnp.where(qpos_col >= NSA_CMP_LEN - 1, 1.0, 0.0)
        p = e * (has_block / jnp.sum(e, axis=-1, keepdims=True))
        o_c = jnp.dot(p.astype(BF16), vc_ref[0:ncols, :], preferred_element_type=F32)
        oc_ref[...] = _unstack_heads(_stacked_gate(jax.nn.sigmoid(gate_ref[...]), 0) * o_c)
        p_sum = p[0:Q]
        for h in range(1, H):
            p_sum = p_sum + p[h * Q:(h + 1) * Q]
        imp_ref[...] = _mm_f32(ovt_ref[:, 0:ncols], p_sum, exact="a", dims=nt_dims)

    tiles_needed = ((qb + 1) * Q // NSA_CMP_STRIDE + LANE - 1) // LANE
    for tiles in range(1, nc // LANE + 1):
        pl.when(tiles_needed == tiles)(functools.partial(attend, tiles * LANE))

    imp = imp_ref[...]
    blk = _iota((LANE, Q), 0)
    q_blk = (qb * Q + _iota((LANE, Q), 1)) >> int(math.log2(NSA_SLC_LEN))
    causal = blk <= q_blk
    for forced_blk in (0, q_blk, q_blk - 1):
        imp = jnp.where(blk == forced_blk, FORCED_SCORE, imp)
    imp = jnp.where(causal, imp, -1.0)
    imp = jnp.where(blk < n_slc, imp, -2.0)
    blk_f = blk.astype(F32)
    sel = jnp.zeros((LANE, Q), F32)
    for _ in range(top_n):
        m = jnp.max(imp, axis=0, keepdims=True)
        first = jnp.min(jnp.where(imp == m, blk_f, float(LANE)), axis=0, keepdims=True)
        hit = blk_f == first
        sel = jnp.where(hit, 1.0, sel)
        imp = jnp.where(hit, -3.0, imp)
    sel = jnp.where(causal, sel, 0.0).T
    selb_ref[...] = jnp.where(sel > 0.5, 0.0, NEG_INF).astype(BF16)
    any_ref[...] = jnp.max(sel, axis=0, keepdims=True)


def _nsa_select(p_nsa, qtab, kc, vc, overlap_t):
    B, S, _ = p_nsa.shape
    Q = NSA_Q
    nqb = S // Q
    nc = kc.shape[1]
    n_slc = S // NSA_SLC_LEN
    kern = functools.partial(_nsa_sel_kernel, n_slc=n_slc, top_n=min(NSA_TOPN, n_slc))
    return pl.pallas_call(
        kern, name="nsa_select",
        grid=(B, nqb),
        in_specs=[pl.BlockSpec((None, Q, NSA_HEADS * LANE), lambda b, i: (b, i, 0)),
                  pl.BlockSpec((2 * NSA_HEADS, LANE), lambda b, i: (0, 0)),
                  pl.BlockSpec((None, Q, LANE), lambda b, i: (b, i, (W_NSA - LANE) // LANE)),
                  pl.BlockSpec((None, nc, LANE), lambda b, i: (b, 0, 0)),
                  pl.BlockSpec((None, nc, LANE), lambda b, i: (b, 0, 0)),
                  pl.BlockSpec((LANE, nc), lambda b, i: (0, 0))],
        out_specs=[pl.BlockSpec((None, Q, GROUP_W), lambda b, i: (b, i, 0)),
                   pl.BlockSpec((None, Q, LANE), lambda b, i: (b, i, 0)),
                   pl.BlockSpec((None, None, 1, LANE), lambda b, i: (b, i, 0, 0))],
        out_shape=[jax.ShapeDtypeStruct((B, S, GROUP_W), F32),
                   jax.ShapeDtypeStruct((B, S, LANE), BF16),
                   jax.ShapeDtypeStruct((B, nqb, 1, LANE), F32)],
        scratch_shapes=[pltpu.VMEM((LANE, Q), F32)],
        compiler_params=_cparams("parallel", "parallel"),
    )(p_nsa, qtab, p_nsa, kc, vc, overlap_t)


def _nsa_attn_kernel(flags_ref, q_ref, qtab_ref, gate_ref, oc_ref, selb_ref, ks_ref, vs_ref, kw_ref, vw_ref,
                     o_ref, m_ref, acc_ref, ow_ref, *, nt):
    Q, H, TK = q_ref.shape[0], NSA_HEADS, NSA_TILE
    b = pl.program_id(0)
    qb = pl.program_id(1)
    nqb = pl.num_programs(1)
    qh = _nsa_queries(q_ref, qtab_ref, qb)
    nt_dims = (((1,), (1,)), ((), ()))

    selb = selb_ref[...]
    qs_sel = jnp.concatenate([jnp.concatenate([q, selb], axis=1) for q in qh], axis=0)
    sink_off = jnp.where(_iota(selb.shape, 1) < NSA_SINK // NSA_SLC_LEN, NEG_INF, selb.astype(F32)).astype(BF16)
    qs_loop = jnp.concatenate([jnp.concatenate([q, sink_off], axis=1) for q in qh], axis=0)
    m_ref[...] = jnp.full(m_ref.shape, NEG_INF, F32)
    acc_ref[...] = jnp.zeros_like(acc_ref)

    def online_update(m_prev, acc_prev, blocks):
        m_new = m_prev
        for s, _ in blocks:
            m_new = jnp.maximum(m_new, jnp.max(s, axis=-1, keepdims=True))
        acc = jnp.exp2(m_prev - m_new) * acc_prev
        for s, v in blocks:
            p = jnp.exp2(s - jnp.tile(m_new, (1, s.shape[1] // LANE)))
            acc = acc + jnp.dot(p.astype(BF16), v, preferred_element_type=F32)
        return m_new, acc

    def tile(t, carry):
        @pl.when(flags_ref[(b * nqb + qb) * nt + t] > 0)
        def _():
            rows = pl.ds(pl.multiple_of(t * TK, TK), TK)
            s = lax.dot_general(qs_loop, ks_ref[rows, :], nt_dims, preferred_element_type=F32)
            m_ref[...], acc_ref[...] = online_update(m_ref[...], acc_ref[...], [(s, vs_ref[rows, :])])
        return carry

    assert Q == TK == NSA_WIN
    half = Q // 2
    lax.fori_loop(0, qb, tile, 0)

    k_sink, v_sink = ks_ref[0:NSA_SINK, :], vs_ref[0:NSA_SINK, :]
    sink_bias = jnp.where(qb > 0, 0.0, NEG_INF)
    lower = _iota((H * half, half), 1) <= (_iota((H * half, half), 0) & (half - 1))
    for part in range(2):
        row_slices = [slice(h * Q + part * half, h * Q + (part + 1) * half) for h in range(H)]
        q_rows = jnp.concatenate([qs_sel[r] for r in row_slices], axis=0)
        keys = pl.ds(pl.multiple_of(qb * TK, TK), (part + 1) * half)
        s = lax.dot_general(q_rows, ks_ref[keys, :], nt_dims, preferred_element_type=F32)
        own = jnp.where(lower, s[:, part * half:], NEG_INF)
        s = own if part == 0 else jnp.concatenate([s[:, :half], own], axis=1)
        s_sink = lax.dot_general(q_rows, k_sink, nt_dims, preferred_element_type=F32) + sink_bias
        m_new, acc_new = online_update(jnp.concatenate([m_ref[r] for r in row_slices], axis=0),
                                       jnp.concatenate([acc_ref[r] for r in row_slices], axis=0),
                                       [(s, vs_ref[keys, :]), (s_sink, v_sink)])
        for h, r in enumerate(row_slices):
            m_ref[r] = m_new[h * half:(h + 1) * half]
            acc_ref[r] = acc_new[h * half:(h + 1) * half]


    def band(q_rows, first_key, n_keys, masks):
        keys = pl.ds(pl.multiple_of(first_key, half), n_keys)
        s = lax.dot_general(q_rows, kw_ref[keys, :], nt_dims, preferred_element_type=F32)
        groups = [s[:, g * half:(g + 1) * half] for g in range(n_keys // half)]
        s = jnp.concatenate([g if m is None else jnp.where(m, g, NEG_INF) for g, m in zip(groups, masks)], axis=1)
        p = jnp.exp2(s - jnp.max(s, axis=-1, keepdims=True))
        return _normalise(jnp.dot(p.astype(BF16), vw_ref[keys, :], preferred_element_type=F32))

    @pl.when(qb == 0)
    def _():
        qs_win = jnp.concatenate(qh, axis=0)
        row = _iota((H * Q, half), 0) & (Q - 1)
        col = _iota((H * Q, half), 1)
        ow_ref[...] = band(qs_win, 0, Q, [col <= row, col + half <= row])

    @pl.when(qb > 0)
    def _():
        row = _iota((H * half, half), 0) & (half - 1)
        col = _iota((H * half, half), 1)
        masks = [col > row, None, col <= row]
        for part in range(2):
            q_rows = jnp.concatenate([q[part * half:(part + 1) * half] for q in qh], axis=0)
            o_part = band(q_rows, (qb - 1) * Q + part * half, NSA_WIN + half, masks)
            for h in range(H):
                ow_ref[h * Q + part * half:h * Q + (part + 1) * half, :] = o_part[h * half:(h + 1) * half]

    gates = jax.nn.sigmoid(gate_ref[...])
    mixed = _stacked_gate(gates, 1) * _normalise(acc_ref[...]) + _stacked_gate(gates, 2) * ow_ref[...]
    o_ref[...] = (oc_ref[...] + _unstack_heads(mixed)).astype(o_ref.dtype)


def _nsa_attend(p_nsa, qtab, o_c, selb, flags, kv):
    B, S, _ = p_nsa.shape
    Q = NSA_Q
    nqb = S // Q
    nt = S // NSA_TILE
    gate_blk = (W_NSA - LANE) // LANE
    kern = functools.partial(_nsa_attn_kernel, nt=nt)
    slab = lambda width, col: pl.BlockSpec((None, S, width), lambda b, i, f: (b, 0, col))
    grid_spec = pltpu.PrefetchScalarGridSpec(
        num_scalar_prefetch=1,
        grid=(B, nqb),
        in_specs=[pl.BlockSpec((None, Q, NSA_HEADS * LANE), lambda b, i, f: (b, i, 0)),
                  pl.BlockSpec((2 * NSA_HEADS, LANE), lambda b, i, f: (0, 0)),
                  pl.BlockSpec((None, Q, LANE), lambda b, i, f: (b, i, gate_blk)),
                  pl.BlockSpec((None, Q, GROUP_W), lambda b, i, f: (b, i, 0)),
                  pl.BlockSpec((None, Q, LANE), lambda b, i, f: (b, i, 0)),
                  slab(2 * LANE, 0), slab(LANE, 2), slab(LANE, 3), slab(LANE, 4)],
        out_specs=pl.BlockSpec((None, Q, GROUP_W), lambda b, i, f: (b, i, 0)),
        scratch_shapes=[pltpu.VMEM((NSA_HEADS * Q, LANE), F32)] * 3,
    )
    return pl.pallas_call(
        kern, name="nsa_attend",
        grid_spec=grid_spec,
        out_shape=jax.ShapeDtypeStruct((B, S, GROUP_W), BF16),
        compiler_params=_cparams("parallel", "parallel"),
    )(flags, p_nsa, qtab, p_nsa, o_c, selb, kv, kv, kv, kv)


def _nsa_tables(S):
    nc = S // NSA_CMP_STRIDE
    n = np.arange(nc)[None, :]
    j = np.arange(LANE)[:, None]
    start = n * NSA_CMP_STRIDE
    ov = (start < (j + 1) * NSA_SLC_LEN) & (start + NSA_CMP_LEN - 1 >= j * NSA_SLC_LEN)
    ov &= (n < (S - NSA_CMP_LEN) // NSA_CMP_STRIDE + 1) & (j < S // NSA_SLC_LEN)
    pos = np.arange(S)
    k_zero = np.zeros((S, NSA_DK), np.float32)
    block_onehot = (pos[:, None] // NSA_SLC_LEN == np.arange(LANE)[None, :]).astype(np.float32)
    v_lanes = np.zeros((S, LANE), np.float32)
    v_lanes[:, ONES_LANE] = 1.0
    kv_table = np.concatenate([k_zero, _nsa_pos_lanes(pos), block_onehot, v_lanes,
                               k_zero, _nsa_pos_lanes(pos), v_lanes], axis=1)
    assert kv_table.shape[1] == W_KV
    return _nsa_query_table(), jnp.asarray(ov.astype(np.float32)), jnp.asarray(kv_table, dtype=BF16)


def _nsa(p_nsa, kv, uk, uv, pe_k, w1_k, w2_k, pe_v, w1_v, w2_v, tables):
    B, S, _ = p_nsa.shape
    qtab, overlap_t, _ = tables
    kc, vc = _nsa_compress(uk, uv, pe_k, w1_k, w2_k, pe_v, w1_v, w2_v)
    o_c, selb, blk_any = _nsa_select(p_nsa, qtab, kc, vc, overlap_t)
    per_tile = NSA_TILE // NSA_SLC_LEN
    nt = S // NSA_TILE
    not_sink = (np.arange(nt * per_tile) >= NSA_SINK // NSA_SLC_LEN).astype(np.float32)
    blk_any = blk_any[:, :, 0, :nt * per_tile] * not_sink
    flags = blk_any.reshape(B, S // NSA_Q, nt, per_tile).max(axis=-1)
    flags = (flags > 0).astype(jnp.int32).reshape(-1)
    return _nsa_attend(p_nsa, qtab, o_c, selb, flags, kv)


def _out_proj_kernel(h_ref, ya_ref, yb_ref, yc_ref, yd_ref, w_ref, g_ref, b_ref, o_ref, wb_ref):
    @pl.when(pl.program_id(0) == 0)
    def _():
        wb_ref[...] = w_ref[...].astype(BF16)

    mix = None
    for idx, y_ref in enumerate((ya_ref, yb_ref, yc_ref, yd_ref)):
        part = _mm(y_ref[...], wb_ref[idx * GROUP_W:(idx + 1) * GROUP_W, :])
        mix = part if mix is None else mix + part
    o_ref[...] = _layer_norm(DEEPNORM_ALPHA * h_ref[...] + mix, g_ref[...], b_ref[...])


def _out_proj(h2, ys, w_out, layer, g, b, tm=1024):
    T, D = h2.shape
    row = lambda w: pl.BlockSpec((tm, w), lambda i: (i, 0))
    const = lambda shape: pl.BlockSpec(shape, lambda i: (0,) * len(shape))
    return pl.pallas_call(
        _out_proj_kernel, name="out_proj_ln",
        grid=(T // tm,),
        in_specs=[row(D), row(GROUP_W), row(GROUP_W), row(GROUP_W), row(GROUP_W),
                  pl.BlockSpec((None, D, D), lambda i: (layer, 0, 0)), const((1, D)), const((1, D))],
        out_specs=row(D),
        out_shape=jax.ShapeDtypeStruct((T, D), F32),
        scratch_shapes=[pltpu.VMEM((D, D), BF16)],
        compiler_params=_cparams("arbitrary"),
    )(h2, *ys, w_out, g.reshape(1, D), b.reshape(1, D))


def _mlp_kernel(h_ref, w1_ref, w2_ref, g_ref, b_ref, o_ref, acc_ref):
    f = pl.program_id(1)

    @pl.when(f == 0)
    def _():
        acc_ref[...] = jnp.zeros_like(acc_ref)

    a = jnp.maximum(_mm(h_ref[...], w1_ref[...]), 0.0)
    acc_ref[...] += _mm(a * a, w2_ref[...])

    @pl.when(f == pl.num_programs(1) - 1)
    def _():
        o_ref[...] = _layer_norm(DEEPNORM_ALPHA * h_ref[...] + acc_ref[...], g_ref[...], b_ref[...])


def _mlp(h2, w1, w2, layer, g, b, tm=1024, tf=1024):
    T, D = h2.shape
    F = w1.shape[2]
    return pl.pallas_call(
        _mlp_kernel, name="mlp_ln",
        grid=(T // tm, F // tf),
        in_specs=[pl.BlockSpec((tm, D), lambda i, f: (i, 0)),
                  pl.BlockSpec((None, D, tf), lambda i, f: (layer, 0, f)),
                  pl.BlockSpec((None, tf, D), lambda i, f: (layer, f, 0)),
                  pl.BlockSpec((1, D), lambda i, f: (0, 0)),
                  pl.BlockSpec((1, D), lambda i, f: (0, 0))],
        out_specs=pl.BlockSpec((tm, D), lambda i, f: (i, 0)),
        out_shape=jax.ShapeDtypeStruct((T, D), F32),
        scratch_shapes=[pltpu.VMEM((tm, D), F32)],
        compiler_params=_cparams("parallel", "arbitrary"),
    )(h2, w1, w2, g.reshape(1, D), b.reshape(1, D))


def kernel(x, ln_emb_g, ln_emb_b, w_in, conv_w, conv_b, dt_bias, a_log, d_skip, ssm_norm_g, q_norm_g, w_uq, kv_norm_g, w_ukv, cmp_pe_k, cmp_w1_k, cmp_w2_k, cmp_pe_v, cmp_w1_v, cmp_w2_v, w_out, ln1_g, ln1_b, w_mlp1, w_mlp2, ln2_g, ln2_b):
    B, S, D = x.shape
    assert D == D_MODEL and S // NSA_SLC_LEN <= LANE
    assert S % NSA_TILE == 0 and S % NSA_Q == 0 and S % min(MLA_TILE, S) == 0 and S % SSM_CHUNK == 0
    T = B * S
    ret_tables = _ret_tables(S)
    mla_tables = _mla_tables(S)
    nsa_tables = _nsa_tables(S)
    h = x.reshape(T, D)
    for l in range(w_in.shape[0]):
        if l == 0:
            h, *proj = _in_proj(h, _layout_w_in(w_in[l]), nsa_tables[2], entry_ln=(ln_emb_g, ln_emb_b))
        else:
            proj = _in_proj(h, _layout_w_in(w_in[l]), nsa_tables[2])
        p_ssm, p_mla, p_ret, p_nsa, nsa_kv, uk, uv = proj
        cmp_rows = (B, S // NSA_CMP_STRIDE, uk.shape[-1])
        y_a, y_c = _ssm_and_retention(p_ssm.reshape(B, S, W_SSM), conv_w[l], conv_b[l], dt_bias[l], a_log[l],
                                      d_skip[l], ssm_norm_g[l], p_ret.reshape(B, S, W_RET), ret_tables)
        wq, wk, wv = _layout_mla_weights(w_uq[l], w_ukv[l])
        q, k, v = _mla_prep(p_mla.reshape(B, S, W_MLA), q_norm_g[l], kv_norm_g[l], wq, wk, wv, mla_tables)
        y_b = _mla_attn(q, k, v)
        y_d = _nsa(p_nsa.reshape(B, S, W_NSA), nsa_kv.reshape(B, S, W_KV), uk.reshape(cmp_rows), uv.reshape(cmp_rows),
                   cmp_pe_k[l], cmp_w1_k[l], cmp_w2_k[l], cmp_pe_v[l], cmp_w1_v[l], cmp_w2_v[l], nsa_tables)
        ys = [y.reshape(T, GROUP_W) for y in (y_a, y_b, y_c, y_d)]
        h = _out_proj(h, ys, w_out, l, ln1_g[l], ln1_b[l])
        h = _mlp(h, w_mlp1, w_mlp2, l, ln2_g[l], ln2_b[l])
    return h.reshape(B, S, D)
```

```python
import functools
import math

import jax
import jax.numpy as jnp
import numpy as np
from jax import lax
from jax.experimental import pallas as pl
from jax.experimental.pallas import tpu as pltpu

F32 = jnp.float32
BF16 = jnp.bfloat16

D_MODEL = 1024
DEPTH = 2
GROUP_W = D_MODEL // 4
SSM_HEADS = 4
SSM_HEAD_DIM = GROUP_W // SSM_HEADS
SSM_GROUPS = 2
SSM_STATE = 128
SSM_CONV = 4
SSM_CHUNK = 128
SSM_XBC = GROUP_W + 2 * SSM_GROUPS * SSM_STATE
MLA_HEADS = 4
MLA_NOPE = 64
MLA_ROPE = 32
MLA_V = GROUP_W // MLA_HEADS
MLA_Q_RANK = 256
MLA_KV_RANK = 128
RET_HEADS = 4
RET_DK = 64
RET_DV = GROUP_W // RET_HEADS
RET_CHUNK = 128
NSA_HEADS = 4
NSA_DK = 64
NSA_DV = GROUP_W // NSA_HEADS
NSA_CMP_LEN = 32
NSA_CMP_STRIDE = 16
NSA_CMP_HID = 256
NSA_SLC_LEN = 64
NSA_TOPN = 16
NSA_WIN = 512
D_FF = 4 * D_MODEL
NSA_Q = 512
ROPE_THETA = 10000.0
EPS = 1e-5
NEG_INF = -1e30
LOG2_E = math.log2(math.e)
FORCED_SCORE = 1e9
DEEPNORM_ALPHA = (2.0 * DEPTH) ** 0.25

IN_SPLITS = (
    GROUP_W, SSM_XBC, SSM_HEADS,
    MLA_Q_RANK, MLA_KV_RANK, MLA_ROPE,
    RET_HEADS * RET_DK, RET_HEADS * RET_DK, RET_HEADS * RET_DV, GROUP_W,
    NSA_HEADS * NSA_DK, NSA_DK, NSA_DV, NSA_DK, NSA_DV, NSA_DK, NSA_DV, 3 * NSA_HEADS,
)

LANE = 128
SUBLANE = 8
W_SSM = GROUP_W + SSM_XBC + LANE
W_MLA = MLA_Q_RANK + MLA_KV_RANK + LANE
W_RET = 4 * GROUP_W
W_NSA = NSA_HEADS * LANE + LANE + LANE
W_KV = 2 * LANE + 3 * LANE
W_PROJ = ((GROUP_W + SSM_XBC) + (MLA_Q_RANK + MLA_KV_RANK) + W_RET
          + (NSA_HEADS * NSA_DK + LANE) + 4 * NSA_DK + LANE)

RECURRENT_CHUNKS_PER_STEP = 4
NSA_TILE = 512
NSA_SINK = 128
MLA_TILE = 1024
VMEM_LIMIT = 48 * 1024 * 1024


def _cparams(*sem):
    return pltpu.CompilerParams(dimension_semantics=sem, vmem_limit_bytes=VMEM_LIMIT)


def _mm(a, b):
    return jnp.dot(a.astype(BF16), b.astype(BF16), preferred_element_type=F32)


def _mm_nt(a, b):
    return lax.dot_general(a.astype(BF16), b.astype(BF16), (((1,), (1,)), ((), ())),
                           preferred_element_type=F32)


def _split_f32(x):
    hi = x.astype(BF16)
    rest = x - hi.astype(F32)
    mid = rest.astype(BF16)
    lo = (rest - mid.astype(F32)).astype(BF16)
    return hi, mid, lo


def _mm_f32(a, b, exact, dims=(((1,), (0,)), ((), ()))):
    fixed, pieces = (a.astype(BF16), _split_f32(b)) if exact == "a" else (b.astype(BF16), _split_f32(a))
    out = None
    for piece in pieces:
        lhs, rhs = (fixed, piece) if exact == "a" else (piece, fixed)
        part = lax.dot_general(lhs, rhs, dims, preferred_element_type=F32)
        out = part if out is None else out + part
    return out


def _silu(x):
    return x * jax.nn.sigmoid(x)


def _softplus(x):
    return jnp.maximum(x, 0.0) + jnp.log1p(jnp.exp(-jnp.abs(x)))


def _layer_norm(x, g, b):
    mu = jnp.mean(x, axis=-1, keepdims=True)
    xc = x - mu
    var = jnp.mean(xc * xc, axis=-1, keepdims=True)
    return xc * lax.rsqrt(var + EPS) * g + b


def _iota(shape, dim):
    return lax.broadcasted_iota(jnp.int32, shape, dim)


def _pad_cols(w, width):
    return jnp.pad(w, ((0, 0), (0, width - w.shape[1])))


def _layout_w_in(w):
    offs = np.concatenate([[0], np.cumsum(IN_SPLITS)])
    p = [w[:, int(offs[i]):int(offs[i + 1])] for i in range(len(IN_SPLITS))]
    (ssm_z, ssm_xbc, ssm_dt, mla_cq, mla_ckv, mla_kr, ret_q, ret_k, ret_v, ret_g,
     nsa_q, nsa_kc, nsa_vc, nsa_ks, nsa_vs, nsa_kw, nsa_vw, nsa_gate) = p
    small = _pad_cols(jnp.concatenate([mla_kr, ssm_dt, nsa_gate], axis=1), LANE)
    cols = [ssm_z, ssm_xbc, ret_q, ret_k, ret_v, ret_g, nsa_q, nsa_ks, nsa_vs, nsa_kw, nsa_vw,
            mla_cq, mla_ckv, nsa_kc, nsa_vc, small]
    out = jnp.concatenate(cols, axis=1)
    assert out.shape[1] == W_PROJ
    return out.astype(BF16)


def _in_proj_kernel(*refs, entry_ln):
    if entry_ln:
        h_ref, g_ref, b_ref, w_ref, kvtab_ref, hn_ref, *outs = refs
        hn = _layer_norm(h_ref[...], g_ref[...], b_ref[...])
        hn_ref[...] = hn
    else:
        h_ref, w_ref, kvtab_ref, *outs = refs
        hn = h_ref[...]
    ssm_ref, mla_ref, ret_ref, nsa_ref, kv_ref, uk_ref, uv_ref, kcv_ref = outs
    hb = hn.astype(BF16)
    tm = hb.shape[0]
    off = 0

    def project(width):
        nonlocal off
        out = jnp.dot(hb, w_ref[:, off:off + width], preferred_element_type=F32)
        off += width
        return out

    ssm_ref[:, 0:GROUP_W + SSM_XBC] = project(GROUP_W + SSM_XBC)
    ret_ref[...] = project(W_RET)
    q = project(NSA_HEADS * NSA_DK)
    kv = project(4 * NSA_DK)
    latent = MLA_Q_RANK + MLA_KV_RANK
    mla_kc = project(latent + LANE)
    mla_ref[:, 0:latent] = mla_kc[:, 0:latent]
    kc_lane = NSA_HEADS * LANE
    nsa_ref[:, kc_lane:kc_lane + LANE] = mla_kc[:, latent:]
    small = project(LANE)
    lane = _iota((tm, LANE), 1)
    low = lane < NSA_DK
    for h in range(NSA_HEADS):
        pair = q[:, (h // 2) * LANE:(h // 2 + 1) * LANE]
        head = pair if h % 2 == 0 else pltpu.roll(pair, NSA_DK, 1)
        nsa_ref[:, h * LANE:(h + 1) * LANE] = jnp.where(low, head, 0.0)
    mla_ref[:, MLA_Q_RANK + MLA_KV_RANK:] = jnp.where(lane < MLA_ROPE, small, 0.0)
    ssm_ref[:, GROUP_W + SSM_XBC:] = jnp.where(lane < SSM_HEADS, pltpu.roll(small, LANE - MLA_ROPE, 1), 0.0)
    gate_at = MLA_ROPE + SSM_HEADS
    nsa_ref[:, kc_lane + LANE:] = jnp.where(lane < 3 * NSA_HEADS, pltpu.roll(small, LANE - gate_at, 1), 0.0)
    groups = h_ref.shape[0] // NSA_CMP_STRIDE
    kc_lane = NSA_HEADS * LANE
    kcv_ref[...] = nsa_ref[:, kc_lane:kc_lane + LANE]
    for t in range(NSA_CMP_STRIDE):
        piece = kcv_ref[pl.ds(t, groups, stride=NSA_CMP_STRIDE), :]
        uk_ref[:, t * NSA_DK:(t + 1) * NSA_DK] = piece[:, :NSA_DK]
        uv_ref[:, t * NSA_DV:(t + 1) * NSA_DV] = piece[:, NSA_DK:]
    sel_kv, win_kv = kv[:, :LANE], kv[:, LANE:]
    pieces = {0: sel_kv, 2: pltpu.roll(sel_kv, NSA_DK, 1), 3: win_kv, 4: pltpu.roll(win_kv, NSA_DK, 1)}
    for slab in range(W_KV // LANE):
        lanes = slice(slab * LANE, (slab + 1) * LANE)
        tab = kvtab_ref[:, lanes]
        if slab in pieces:
            kv_ref[:, lanes] = (jnp.where(low, pieces[slab], 0.0) + tab.astype(F32)).astype(BF16)
        else:
            kv_ref[:, lanes] = tab


def _in_proj(h2, w_p, kv_table, entry_ln=None, tm=512):
    T, D = h2.shape
    S = kv_table.shape[0]
    widths = (W_SSM, W_MLA, W_RET, W_NSA)
    half = NSA_CMP_STRIDE * NSA_DK
    row = lambda w: pl.BlockSpec((tm, w), lambda i: (i, 0))
    const = lambda shape: pl.BlockSpec(shape, lambda i: (0,) * len(shape))
    in_specs = [const((D, W_PROJ)), pl.BlockSpec((tm, W_KV), lambda i: (i % (S // tm), 0))]
    out_specs = [row(w) for w in widths + (W_KV,)] + [pl.BlockSpec((tm // NSA_CMP_STRIDE, half), lambda i: (i, 0))] * 2
    out_shape = ([jax.ShapeDtypeStruct((T, w), F32) for w in widths] + [jax.ShapeDtypeStruct((T, W_KV), BF16)]
                 + [jax.ShapeDtypeStruct((T // NSA_CMP_STRIDE, half), F32)] * 2)
    operands = (w_p, kv_table)
    if entry_ln is not None:
        in_specs = [const((1, D)), const((1, D))] + in_specs
        out_specs = [row(D)] + out_specs
        out_shape = [jax.ShapeDtypeStruct((T, D), F32)] + out_shape
        operands = tuple(v.reshape(1, D) for v in entry_ln) + operands
    return pl.pallas_call(
        functools.partial(_in_proj_kernel, entry_ln=entry_ln is not None), name="in_proj",
        grid=(T // tm,),
        in_specs=[row(D)] + in_specs,
        out_specs=out_specs,
        out_shape=out_shape,
        scratch_shapes=[pltpu.VMEM((tm, LANE), F32)],
        compiler_params=_cparams("parallel"),
    )(h2, *operands)


def _ssm_chunk(p_ref, cw_ref, cb_ref, dtb_ref, alog_ref, dskip_ref, ng_ref, o_ref, state_ref, ext_ref):
    L, H, P, N = SSM_CHUNK, SSM_HEADS, SSM_HEAD_DIM, SSM_STATE
    z = p_ref[:, 0:GROUP_W]
    ext_ref[SUBLANE:SUBLANE + L, :] = p_ref[:, GROUP_W:GROUP_W + SSM_XBC]
    ext = ext_ref[...]
    conv = cb_ref[...] + ext[SUBLANE:] * cw_ref[SSM_CONV - 1:SSM_CONV, :]
    for j in range(SSM_CONV - 1):
        shift = SSM_CONV - 1 - j
        conv = conv + pltpu.roll(ext, shift, 0)[SUBLANE:] * cw_ref[j:j + 1, :]
    ext_ref[0:SUBLANE, :] = ext_ref[L:L + SUBLANE, :]
    xbc = _silu(conv)
    xs = xbc[:, 0:GROUP_W]
    b_in = xbc[:, GROUP_W:GROUP_W + SSM_GROUPS * N]
    c_in = xbc[:, GROUP_W + SSM_GROUPS * N:]

    dt = _softplus(p_ref[:, GROUP_W + SSM_XBC:] + dtb_ref[...])
    a = dt * (-jnp.exp(alog_ref[...]))
    row = _iota((L, L), 0)
    col = _iota((L, L), 1)
    tril = col <= row
    cs = _mm_f32(jnp.where(tril, 1.0, 0.0), a, exact="a")
    cs_t = cs.T
    ecs = jnp.exp(cs)
    dte = jnp.exp(cs[L - 1:L, :] - cs)
    first_head_of_pair = _iota((L, LANE), 1) < P

    def expand(x):
        pairs = [jnp.where(first_head_of_pair, x[:, h:h + 1], x[:, h + 1:h + 2]) for h in range(0, H, LANE // P)]
        return jnp.concatenate(pairs, axis=1)

    assert LANE == 2 * P
    dt_x = expand(dt)
    ecs_x = expand(ecs)
    dte_x = expand(dte)

    xdt = xs * dt_x
    wx = xdt * dte_x
    head_of_lane = _iota((L, H * P), 1) // P
    y = xs * dskip_ref[...]
    y_off = []
    rep = H // SSM_GROUPS
    for g in range(SSM_GROUPS):
        cg = c_in[:, g * N:(g + 1) * N]
        bg = b_in[:, g * N:(g + 1) * N]
        cb = _mm_nt(cg, bg)
        for h in range(g * rep, (g + 1) * rep):
            diff = cs[:, h:h + 1] - cs_t[h:h + 1, :]
            seg = jnp.where(tril, jnp.exp(jnp.where(tril, diff, 0.0)), 0.0)
            yh = _mm(cb * seg, xdt)
            y = y + jnp.where(head_of_lane == h, yh, 0.0)
        lanes = slice(g * rep * P, (g + 1) * rep * P)
        st_prev = state_ref[:, lanes]
        y_off.append(_mm(cg, st_prev))
        state_ref[:, lanes] = st_prev * ecs_x[L - 1:L, lanes] + _mm(bg.T, wx[:, lanes])
    y = y + jnp.concatenate(y_off, axis=1) * ecs_x
    y = y * _silu(z)
    ms = jnp.mean(y * y, axis=-1, keepdims=True)
    o_ref[...] = (y * lax.rsqrt(ms + EPS) * ng_ref[...]).astype(o_ref.dtype)


def _recurrent_kernel(ps_ref, cw_ref, cb_ref, dtb_ref, alog_ref, dskip_ref, ng_ref,
                      pr_ref, cos_ref, sin_ref, dec_ref, zeta_ref, xi_ref, cd_ref,
                      oa_ref, oc_ref, sstate_ref, ext_ref, rstate_ref):
    @pl.when(pl.program_id(0) == 0)
    def _():
        sstate_ref[...] = jnp.zeros_like(sstate_ref)
        rstate_ref[...] = jnp.zeros_like(rstate_ref)
        ext_ref[:, 0:SUBLANE, :] = jnp.zeros((ext_ref.shape[0], SUBLANE, SSM_XBC), F32)

    L = SSM_CHUNK
    for sub in range(ps_ref.shape[1] // L):
        rows = pl.ds(sub * L, L)
        for b in range(ps_ref.shape[0]):
            _ssm_chunk(ps_ref.at[b, rows], cw_ref, cb_ref, dtb_ref, alog_ref, dskip_ref, ng_ref,
                       oa_ref.at[b, rows], sstate_ref.at[b], ext_ref.at[b])
            _ret_chunk(pr_ref.at[b, rows], cos_ref.at[rows], sin_ref.at[rows], dec_ref, zeta_ref, xi_ref, cd_ref,
                       oc_ref.at[b, rows], rstate_ref.at[b])


def _ret_chunk(p_ref, cos_ref, sin_ref, dec_ref, zeta_ref, xi_ref, cd_ref, o_ref, state_ref):
    L, H, DK, DV = RET_CHUNK, RET_HEADS, RET_DK, RET_DV
    W = H * DK
    q = p_ref[:, 0:W]
    k = p_ref[:, W:2 * W]
    v = p_ref[:, 2 * W:3 * W]
    gate = p_ref[:, 3 * W:4 * W]
    lane = _iota((L, W), 1)
    first_half = (lane % DK) < (DK // 2)
    head_of_lane = lane // DK

    def rope(x):
        partner = jnp.where(first_half, pltpu.roll(x, W - DK // 2, 1), pltpu.roll(x, DK // 2, 1))
        return x * cos_ref[...] + partner * sin_ref[...]

    qr = rope(q)
    kr = rope(k) * (DK ** -0.5)
    y = jnp.zeros((L, H * DV), F32)
    for h in range(H):
        qh = jnp.where(head_of_lane == h, qr, 0.0)
        sc = _mm_nt(qh, kr) * dec_ref[h]
        y = y + jnp.where(head_of_lane == h, _mm(sc, v), 0.0)
    st = state_ref[...]
    y = y + _mm(qr * xi_ref[...], st)
    same_head = (_iota((W, H * DV), 0) // DK) == (_iota((W, H * DV), 1) // DV)
    kv = _mm((kr * zeta_ref[...]).T, v)
    state_ref[...] = st * cd_ref[...] + jnp.where(same_head, kv, 0.0)
    assert DV & (DV - 1) == 0
    ms = _mm_f32(y * y, jnp.where(same_head, 1.0 / DV, 0.0), exact="b")
    o_ref[...] = (y * lax.rsqrt(ms + EPS) * _silu(gate)).astype(o_ref.dtype)


def _ret_tables(S):
    H, DK, L = RET_HEADS, RET_DK, RET_CHUNK
    inv = ROPE_THETA ** (-np.arange(0, DK, 2, dtype=np.float64) / DK)
    ang = np.arange(S, dtype=np.float64)[:, None] * inv[None, :]
    cos, sin = np.cos(ang), np.sin(ang)
    cos_t = np.tile(np.concatenate([cos, cos], axis=1), (1, H))
    sin_t = np.tile(np.concatenate([-sin, sin], axis=1), (1, H))
    log_gamma = np.log1p(-np.exp2(-5.0 - np.arange(H, dtype=np.float64)))
    pos = np.arange(L, dtype=np.float64)
    diff = pos[:, None] - pos[None, :]
    decay_in = np.where(diff >= 0, np.exp(np.maximum(diff, 0.0)[None] * log_gamma[:, None, None]), 0.0)
    zeta = np.exp((L - 1 - pos)[None] * log_gamma[:, None])
    xi = np.exp((pos + 1.0)[None] * log_gamma[:, None])
    chunk_decay = np.exp(L * log_gamma)
    zeta_x = np.repeat(zeta.T, DK, axis=1)
    xi_x = np.repeat(xi.T, DK, axis=1)
    cd_x = np.repeat(chunk_decay, RET_DV).reshape(1, H * RET_DV)
    return tuple(jnp.asarray(t, dtype=F32) for t in (cos_t, sin_t, decay_in, zeta_x, xi_x, cd_x))


def _ssm_and_retention(p_ssm, conv_w, conv_b, dt_bias, a_log, d_skip, norm_g, p_ret, tables):
    B, S, _ = p_ssm.shape
    L, H = SSM_CHUNK, RET_HEADS
    assert RET_CHUNK == L
    W = H * RET_DK
    cos_t, sin_t, decay_in, zeta_x, xi_x, cd_x = tables
    pad_h = lambda v: jnp.pad(v, (0, LANE - SSM_HEADS)).reshape(1, LANE)
    const = lambda shape: pl.BlockSpec(shape, lambda c: (0,) * len(shape))
    rows = RECURRENT_CHUNKS_PER_STEP * L
    assert S % rows == 0
    chunk = lambda width: pl.BlockSpec((B, rows, width), lambda c: (0, c, 0))
    out_shape = jax.ShapeDtypeStruct((B, S, GROUP_W), BF16)
    return pl.pallas_call(
        _recurrent_kernel, name="ssm_retention",
        grid=(S // rows,),
        in_specs=[chunk(W_SSM),
                  const((SSM_CONV, SSM_XBC)), const((1, SSM_XBC)), const((1, LANE)), const((1, LANE)),
                  const((1, GROUP_W)), const((1, GROUP_W)),
                  chunk(W_RET),
                  pl.BlockSpec((rows, W), lambda c: (c, 0)), pl.BlockSpec((rows, W), lambda c: (c, 0)),
                  const((H, L, L)), const((L, W)), const((L, W)), const((1, H * RET_DV))],
        out_specs=[chunk(GROUP_W), chunk(GROUP_W)],
        out_shape=[out_shape, out_shape],
        scratch_shapes=[pltpu.VMEM((B, SSM_STATE, GROUP_W), F32),
                        pltpu.VMEM((B, L + SUBLANE, SSM_XBC), F32),
                        pltpu.VMEM((B, W, H * RET_DV), F32)],
        compiler_params=_cparams("arbitrary"),
    )(p_ssm, conv_w, conv_b.reshape(1, -1), pad_h(dt_bias), pad_h(a_log),
      jnp.repeat(d_skip, SSM_HEAD_DIM).reshape(1, GROUP_W), norm_g.reshape(1, GROUP_W),
      p_ret, cos_t, sin_t, decay_in, zeta_x, xi_x, cd_x)


def _mla_prep_kernel(p_ref, qg_ref, wq_ref, kvg_ref, wk_ref, wv_ref, cos_ref, sin_ref,
                     q_ref, k_ref, v_ref):
    tm = p_ref.shape[0]
    cq = p_ref[:, 0:MLA_Q_RANK]
    ckv = p_ref[:, MLA_Q_RANK:MLA_Q_RANK + MLA_KV_RANK]
    kr = p_ref[:, MLA_Q_RANK + MLA_KV_RANK:]

    def rms(x, g):
        return x * lax.rsqrt(jnp.mean(x * x, axis=-1, keepdims=True) + EPS) * g

    q = _mm(rms(cq, qg_ref[...]), wq_ref[...])
    kvl = rms(ckv, kvg_ref[...])
    kn = _mm(kvl, wk_ref[...])
    vv = _mm(kvl, wv_ref[...])
    kr_sh = pltpu.roll(kr, MLA_NOPE, 1)
    lane = _iota((tm, LANE), 1)
    half = MLA_ROPE // 2
    low = (lane >= MLA_NOPE) & (lane < MLA_NOPE + half)
    cos = cos_ref[...]
    sin = sin_ref[...]

    def rope(x):
        partner = jnp.where(low, pltpu.roll(x, LANE - half, 1), pltpu.roll(x, half, 1))
        return x * cos + partner * sin

    scale = (MLA_NOPE + MLA_ROPE) ** -0.5 * LOG2_E
    for h in range(MLA_HEADS):
        sl = slice(h * LANE, (h + 1) * LANE)
        q_ref[h] = (rope(q[:, sl]) * scale).astype(BF16)
        k_ref[h] = rope(kn[:, sl] + kr_sh).astype(BF16)
        v_ref[h] = jnp.where(lane == _mla_ones_lane(h), 1.0, vv[:, sl]).astype(BF16)


def _mla_tables(S):
    inv = ROPE_THETA ** (-np.arange(0, MLA_ROPE, 2, dtype=np.float64) / MLA_ROPE)
    ang = np.arange(S, dtype=np.float64)[:, None] * inv[None, :]
    cos, sin = np.cos(ang), np.sin(ang)
    tail = LANE - MLA_NOPE - MLA_ROPE
    cos_t = np.concatenate([np.ones((S, MLA_NOPE)), cos, cos, np.ones((S, tail))], axis=1)
    sin_t = np.concatenate([np.zeros((S, MLA_NOPE)), -sin, sin, np.zeros((S, tail))], axis=1)
    return jnp.asarray(cos_t, dtype=F32), jnp.asarray(sin_t, dtype=F32)


def _layout_mla_weights(w_uq, w_ukv):
    H = MLA_HEADS
    dq = MLA_NOPE + MLA_ROPE
    wq = jnp.concatenate([_pad_cols(w_uq[:, h * dq:(h + 1) * dq], LANE) for h in range(H)], axis=1)
    dkv = MLA_NOPE + MLA_V
    wk, wv = [], []
    for h in range(H):
        blk = w_ukv[:, h * dkv:(h + 1) * dkv]
        wk.append(_pad_cols(blk[:, :MLA_NOPE], LANE))
        v = blk[:, MLA_NOPE:]
        zero = jnp.zeros_like(v)
        wv.append(jnp.concatenate([v, zero] if h % 2 == 0 else [zero, v], axis=1))
    return wq.astype(BF16), jnp.concatenate(wk, axis=1).astype(BF16), jnp.concatenate(wv, axis=1).astype(BF16)


def _mla_prep(p_mla, q_norm_g, kv_norm_g, wq, wk, wv, tables, tm=1024):
    B, S, _ = p_mla.shape
    H = MLA_HEADS
    cos_t, sin_t = tables
    const = lambda shape: pl.BlockSpec(shape, lambda b, i: (0,) * len(shape))
    qkv_spec = pl.BlockSpec((None, H, tm, LANE), lambda b, i: (b, 0, i, 0))
    qkv_shape = jax.ShapeDtypeStruct((B, H, S, LANE), BF16)
    return pl.pallas_call(
        _mla_prep_kernel, name="mla_prep",
        grid=(B, S // tm),
        in_specs=[pl.BlockSpec((None, tm, W_MLA), lambda b, i: (b, i, 0)),
                  const((1, MLA_Q_RANK)), const((MLA_Q_RANK, H * LANE)),
                  const((1, MLA_KV_RANK)), const((MLA_KV_RANK, H * LANE)), const((MLA_KV_RANK, H * LANE)),
                  pl.BlockSpec((tm, LANE), lambda b, i: (i, 0)),
                  pl.BlockSpec((tm, LANE), lambda b, i: (i, 0))],
        out_specs=[qkv_spec, qkv_spec, qkv_spec],
        out_shape=[qkv_shape, qkv_shape, qkv_shape],
        compiler_params=_cparams("parallel", "parallel"),
    )(p_mla, q_norm_g.reshape(1, -1), wq, kv_norm_g.reshape(1, -1), wk, wv, cos_t, sin_t)


def _mla_ones_lane(h):
    return MLA_V if h % 2 == 0 else 0


def _mla_attn_kernel(qi_ref, kj_ref, q_ref, k_ref, v_ref, o_ref, m_ref, acc_ref):
    H = MLA_HEADS
    tq, tk = q_ref.shape[1], k_ref.shape[1]
    i = qi_ref[pl.program_id(1)]
    j = kj_ref[pl.program_id(1)]

    @pl.when(j == 0)
    def _():
        m_ref[...] = jnp.full(m_ref.shape, NEG_INF, F32)
        acc_ref[...] = jnp.zeros_like(acc_ref)

    def sweep(blocks):
        nt_dims = (((1,), (1,)), ((), ()))
        scores = [[lax.dot_general(q_ref[h, r0:r0 + nr, :], k_ref[h, 0:nk, :], nt_dims, preferred_element_type=F32)
                   for (r0, nr, nk, _) in blocks] for h in range(H)]
        for h in range(H):
            for (r0, nr, nk, offset), s in zip(blocks, scores[h]):
                if offset is not None:
                    s = jnp.where(_iota((nr, nk), 1) - _iota((nr, nk), 0) <= offset, s, NEG_INF)
                rows = slice(r0, r0 + nr)
                m_prev = m_ref[h, rows]
                m_new = jnp.maximum(m_prev, jnp.max(s, axis=-1, keepdims=True))
                p = jnp.exp2(s - jnp.tile(m_new, (1, nk // LANE)))
                acc_ref[h, rows] = (jnp.exp2(m_prev - m_new) * acc_ref[h, rows]
                                    + jnp.dot(p.astype(BF16), v_ref[h, 0:nk, :], preferred_element_type=F32))
                m_ref[h, rows] = m_new

    assert tq == tk
    half = tq // 2

    @pl.when(j < i)
    def _():
        sweep([(0, tq, tk, None)])

    @pl.when(j == i)
    def _():
        sweep([(0, half, half, 0), (half, half, tk, half)])
        lane = _iota((tq, LANE), 1)
        for pair in range(H // 2):
            he, ho = 2 * pair, 2 * pair + 1
            acc_e, acc_o = acc_ref[he], acc_ref[ho]
            le = acc_e[:, _mla_ones_lane(he):_mla_ones_lane(he) + 1]
            lo = acc_o[:, _mla_ones_lane(ho):_mla_ones_lane(ho) + 1]
            o_ref[:, pair * LANE:(pair + 1) * LANE] = jnp.where(lane < MLA_V, acc_e / le, acc_o / lo).astype(o_ref.dtype)


def _mla_attn(q, k, v):
    B, H, S, _ = q.shape
    t = min(MLA_TILE, S)
    tq = t
    pairs = [(i, j) for i in range(S // tq) for j in range((i + 1) * tq // t)]
    qi = jnp.asarray([p[0] for p in pairs], jnp.int32)
    kj = jnp.asarray([p[1] for p in pairs], jnp.int32)
    grid_spec = pltpu.PrefetchScalarGridSpec(
        num_scalar_prefetch=2,
        grid=(B, len(pairs)),
        in_specs=[pl.BlockSpec((None, H, tq, LANE), lambda b, p, qi, kj: (b, 0, qi[p], 0)),
                  pl.BlockSpec((None, H, t, LANE), lambda b, p, qi, kj: (b, 0, kj[p], 0)),
                  pl.BlockSpec((None, H, t, LANE), lambda b, p, qi, kj: (b, 0, kj[p], 0))],
        out_specs=pl.BlockSpec((None, tq, GROUP_W), lambda b, p, qi, kj: (b, qi[p], 0)),
        scratch_shapes=[pltpu.VMEM((H, tq, LANE), F32), pltpu.VMEM((H, tq, LANE), F32)],
    )
    return pl.pallas_call(
        _mla_attn_kernel, name="mla_attn",
        grid_spec=grid_spec,
        out_shape=jax.ShapeDtypeStruct((B, S, GROUP_W), BF16),
        compiler_params=_cparams("parallel", "arbitrary"),
    )(qi, kj, q, k, v)


POS_HI = NSA_DK
POS_LO = NSA_DK + 3
POS_ONE = NSA_DK + 6
ONES_LANE = NSA_DV


def _split_bf16(x, parts=3):
    out, rem = [], np.float64(x)
    for _ in range(parts):
        piece = np.float64(np.float32(rem).astype(jnp.bfloat16).astype(np.float32))
        out.append(float(piece))
        rem = rem - piece
    return out


def _nsa_query_table():
    H = NSA_HEADS
    tab = np.zeros((2 * H, LANE), np.float32)
    for h in range(H):
        c = 2.0 ** (-8.0 * (h + 1) / H) * LOG2_E
        pieces = _split_bf16(c)
        tab[h, POS_HI:POS_HI + 3] = pieces
        tab[h, POS_LO:POS_LO + 3] = pieces
        tab[H + h, POS_ONE] = -sum(pieces)
    return jnp.asarray(tab)


def _nsa_pos_lanes(pos, lo_offset=0.0):
    t = np.zeros((len(pos), LANE - NSA_DK), np.float32)
    t[:, POS_HI - NSA_DK:POS_HI - NSA_DK + 3] = (NSA_SLC_LEN * (pos // NSA_SLC_LEN))[:, None]
    t[:, POS_LO - NSA_DK:POS_LO - NSA_DK + 3] = (pos % NSA_SLC_LEN + lo_offset)[:, None]
    t[:, POS_ONE - NSA_DK] = 1.0
    return t


def _nsa_queries(q_ref, qtab_ref, qb):
    Q, H = q_ref.shape[0], NSA_HEADS
    qpos = (qb * Q + _iota((Q, 1), 0)).astype(F32)
    out = []
    for h in range(H):
        q = q_ref[:, h * LANE:(h + 1) * LANE] * (NSA_DK ** -0.5 * LOG2_E)
        out.append((q + qtab_ref[h:h + 1, :] + qtab_ref[H + h:H + h + 1, :] * qpos).astype(BF16))
    return out


def _normalise(o):
    return o / o[:, ONES_LANE:ONES_LANE + 1]


def _stacked_gate(gates, branch):
    lanes = [3 * h + branch for h in range(NSA_HEADS)]
    return jnp.concatenate([gates[:, c:c + 1] for c in lanes], axis=0)


def _unstack_heads(o):
    Q = o.shape[0] // NSA_HEADS
    lane = _iota((Q, LANE), 1)
    out = []
    for pair in range(NSA_HEADS // 2):
        even = o[(2 * pair) * Q:(2 * pair + 1) * Q]
        odd = o[(2 * pair + 1) * Q:(2 * pair + 2) * Q]
        out.append(jnp.where(lane < NSA_DV, even, pltpu.roll(odd, NSA_DV, 1)))
    return jnp.concatenate(out, axis=1)


def _nsa_cmp_kernel(uk_ref, uv_ref, pek_ref, pev_ref, w1k_ref, w1v_ref, w2k_ref, w2v_ref, cpos_ref,
                    kc_ref, vc_ref, sh_ref):
    nb = uk_ref.shape[0]
    half = uk_ref.shape[1]

    def hidden(u_ref, pe_ref, w1_ref):
        u = u_ref[...]
        first = _mm(u + pe_ref[0:1, :], w1_ref[0:half, :])
        second = _mm(u + pe_ref[1:2, :], w1_ref[half:2 * half, :])
        sh_ref[0:nb, :] = second
        sh_ref[nb:nb + SUBLANE, :] = jnp.zeros((SUBLANE, NSA_CMP_HID), F32)
        return first + sh_ref[pl.ds(1, nb), :]

    hk = _silu(hidden(uk_ref, pek_ref, w1k_ref))
    hv = _silu(hidden(uv_ref, pev_ref, w1v_ref))
    kc_ref[...] = (_mm(hk, w2k_ref[...]) + cpos_ref[...]).astype(BF16)
    ones_lane = jnp.where(_iota((1, LANE), 1) == ONES_LANE, 1.0, 0.0)
    vc_ref[...] = (_mm(hv, w2v_ref[...]) + ones_lane).astype(BF16)


def _nsa_compress(uk, uv, pe_k, w1_k, w2_k, pe_v, w1_v, w2_v):
    B, nb, half = uk.shape
    hid = NSA_CMP_HID
    const = lambda shape: pl.BlockSpec(shape, lambda b: (0,) * len(shape))
    w2k = _pad_cols(w2_k, LANE).astype(BF16)
    w2v = _pad_cols(w2_v, LANE).astype(BF16)
    centre = _nsa_pos_lanes(np.arange(nb) * NSA_CMP_STRIDE, 0.5 * (NSA_CMP_LEN - 1))
    cpos = jnp.asarray(np.concatenate([np.zeros((nb, NSA_DK), np.float32), centre], axis=1))
    out_spec = pl.BlockSpec((None, nb, LANE), lambda b: (b, 0, 0))
    out_shape = jax.ShapeDtypeStruct((B, nb, LANE), BF16)
    return pl.pallas_call(
        _nsa_cmp_kernel, name="nsa_compress",
        grid=(B,),
        in_specs=[pl.BlockSpec((None, nb, half), lambda b: (b, 0, 0)),
                  pl.BlockSpec((None, nb, half), lambda b: (b, 0, 0)),
                  const((2, half)), const((2, half)),
                  const((2 * half, hid)), const((2 * half, hid)),
                  const((hid, LANE)), const((hid, LANE)), const((nb, LANE))],
        out_specs=[out_spec, out_spec],
        out_shape=[out_shape, out_shape],
        scratch_shapes=[pltpu.VMEM((nb + SUBLANE, hid), F32)],
        compiler_params=_cparams("parallel"),
    )(uk, uv, pe_k.reshape(2, half), pe_v.reshape(2, half), w1_k.astype(BF16), w1_v.astype(BF16), w2k, w2v, cpos)


def _nsa_sel_kernel(q_ref, qtab_ref, gate_ref, kc_ref, vc_ref, ovt_ref, oc_ref, selb_ref, any_ref, *, n_slc, top_n):
    Q, H = q_ref.shape[0], NSA_HEADS
    qb = pl.program_id(1)
    nc = kc_ref.shape[0]
    pick = functools.partial(_nsa_pick_blocks, qb=qb, selb_ref=selb_ref, any_ref=any_ref, n_slc=n_slc, top_n=top_n)
    qs = jnp.concatenate(_nsa_queries(q_ref, qtab_ref, qb), axis=0)
    nt_dims = (((1,), (1,)), ((), ()))

    def attend(ncols):
        s = lax.dot_general(qs, kc_ref[0:ncols, :], nt_dims, preferred_element_type=F32)
        qpos = qb * Q + (_iota((H * Q, ncols), 0) & (Q - 1))
        block_end = _iota((H * Q, ncols), 1) * NSA_CMP_STRIDE + (NSA_CMP_LEN - 1)
        s = jnp.where(block_end <= qpos, s, NEG_INF)
        e = jnp.exp2(s - jnp.max(s, axis=-1, keepdims=True))
        qpos_col = qb * Q + (_iota((H * Q, 1), 0) & (Q - 1))
        has_block = jnp.where(qpos_col >= NSA_CMP_LEN - 1, 1.0, 0.0)
        p = e * (has_block / jnp.sum(e, axis=-1, keepdims=True))
        o_c = jnp.dot(p.astype(BF16), vc_ref[0:ncols, :], preferred_element_type=F32)
        oc_ref[...] = _unstack_heads(_stacked_gate(jax.nn.sigmoid(gate_ref[...]), 0) * o_c)
        p_sum = p[0:Q]
        for h in range(1, H):
            p_sum = p_sum + p[h * Q:(h + 1) * Q]
        pick(_mm_f32(ovt_ref[:, 0:ncols], p_sum, exact="a", dims=nt_dims))

    tiles_needed = ((qb + 1) * Q // NSA_CMP_STRIDE + LANE - 1) // LANE
    for tiles in range(1, nc // LANE + 1):
        pl.when(tiles_needed == tiles)(functools.partial(attend, tiles * LANE))


def _nsa_pick_blocks(imp, qb, selb_ref, any_ref, n_slc, top_n):
    Q = imp.shape[1]
    blk = _iota((LANE, Q), 0)
    q_blk = (qb * Q + _iota((LANE, Q), 1)) >> int(math.log2(NSA_SLC_LEN))
    causal = blk <= q_blk
    for forced_blk in (0, q_blk, q_blk - 1):
        imp = jnp.where(blk == forced_blk, FORCED_SCORE, imp)
    imp = jnp.where(causal, imp, -1.0)
    imp = jnp.where(blk < n_slc, imp, -2.0)
    blk_f = blk.astype(F32)
    sel = jnp.zeros((LANE, Q), F32)
    for _ in range(top_n):
        m = jnp.max(imp, axis=0, keepdims=True)
        first = jnp.min(jnp.where(imp == m, blk_f, float(LANE)), axis=0, keepdims=True)
        hit = blk_f == first
        sel = jnp.where(hit, 1.0, sel)
        imp = jnp.where(hit, -3.0, imp)
    sel = jnp.where(causal, sel, 0.0).T
    selb_ref[...] = jnp.where(sel > 0.5, 0.0, NEG_INF).astype(BF16)
    any_ref[...] = jnp.max(sel, axis=0, keepdims=True)


def _nsa_select(p_nsa, qtab, kc, vc, overlap_t):
    B, S, _ = p_nsa.shape
    Q = NSA_Q
    nqb = S // Q
    nc = kc.shape[1]
    n_slc = S // NSA_SLC_LEN
    kern = functools.partial(_nsa_sel_kernel, n_slc=n_slc, top_n=min(NSA_TOPN, n_slc))
    return pl.pallas_call(
        kern, name="nsa_select",
        grid=(B, nqb),
        in_specs=[pl.BlockSpec((None, Q, NSA_HEADS * LANE), lambda b, i: (b, i, 0)),
                  pl.BlockSpec((2 * NSA_HEADS, LANE), lambda b, i: (0, 0)),
                  pl.BlockSpec((None, Q, LANE), lambda b, i: (b, i, (W_NSA - LANE) // LANE)),
                  pl.BlockSpec((None, nc, LANE), lambda b, i: (b, 0, 0)),
                  pl.BlockSpec((None, nc, LANE), lambda b, i: (b, 0, 0)),
                  pl.BlockSpec((LANE, nc), lambda b, i: (0, 0))],
        out_specs=[pl.BlockSpec((None, Q, GROUP_W), lambda b, i: (b, i, 0)),
                   pl.BlockSpec((None, Q, LANE), lambda b, i: (b, i, 0)),
                   pl.BlockSpec((None, None, 1, LANE), lambda b, i: (b, i, 0, 0))],
        out_shape=[jax.ShapeDtypeStruct((B, S, GROUP_W), F32),
                   jax.ShapeDtypeStruct((B, S, LANE), BF16),
                   jax.ShapeDtypeStruct((B, nqb, 1, LANE), F32)],
        compiler_params=_cparams("parallel", "parallel"),
    )(p_nsa, qtab, p_nsa, kc, vc, overlap_t)


def _nsa_attn_kernel(flags_ref, q_ref, qtab_ref, gate_ref, oc_ref, selb_ref, ks_ref, vs_ref, kw_ref, vw_ref,
                     o_ref, m_ref, acc_ref, ow_ref, *, nt):
    Q, H, TK = q_ref.shape[0], NSA_HEADS, NSA_TILE
    b = pl.program_id(0)
    qb = pl.program_id(1)
    nqb = pl.num_programs(1)
    qh = _nsa_queries(q_ref, qtab_ref, qb)
    nt_dims = (((1,), (1,)), ((), ()))

    selb = selb_ref[...]
    qs_sel = jnp.concatenate([jnp.concatenate([q, selb], axis=1) for q in qh], axis=0)
    sink_off = jnp.where(_iota(selb.shape, 1) < NSA_SINK // NSA_SLC_LEN, NEG_INF, selb.astype(F32)).astype(BF16)
    qs_loop = jnp.concatenate([jnp.concatenate([q, sink_off], axis=1) for q in qh], axis=0)
    m_ref[...] = jnp.full(m_ref.shape, NEG_INF, F32)
    acc_ref[...] = jnp.zeros_like(acc_ref)

    def online_update(m_prev, acc_prev, blocks):
        m_new = m_prev
        for s, _ in blocks:
            m_new = jnp.maximum(m_new, jnp.max(s, axis=-1, keepdims=True))
        acc = jnp.exp2(m_prev - m_new) * acc_prev
        for s, v in blocks:
            p = jnp.exp2(s - jnp.tile(m_new, (1, s.shape[1] // LANE)))
            acc = acc + jnp.dot(p.astype(BF16), v, preferred_element_type=F32)
        return m_new, acc

    def tile(t, carry):
        @pl.when(flags_ref[(b * nqb + qb) * nt + t] > 0)
        def _():
            rows = pl.ds(pl.multiple_of(t * TK, TK), TK)
            s = lax.dot_general(qs_loop, ks_ref[rows, :], nt_dims, preferred_element_type=F32)
            m_ref[...], acc_ref[...] = online_update(m_ref[...], acc_ref[...], [(s, vs_ref[rows, :])])
        return carry

    assert Q == TK == NSA_WIN
    half = Q // 2
    lax.fori_loop(0, qb, tile, 0)

    k_sink, v_sink = ks_ref[0:NSA_SINK, :], vs_ref[0:NSA_SINK, :]
    sink_bias = jnp.where(qb > 0, 0.0, NEG_INF)
    lower = _iota((H * half, half), 1) <= (_iota((H * half, half), 0) & (half - 1))
    for part in range(2):
        row_slices = [slice(h * Q + part * half, h * Q + (part + 1) * half) for h in range(H)]
        q_rows = jnp.concatenate([qs_sel[r] for r in row_slices], axis=0)
        keys = pl.ds(pl.multiple_of(qb * TK, TK), (part + 1) * half)
        s = lax.dot_general(q_rows, ks_ref[keys, :], nt_dims, preferred_element_type=F32)
        own = jnp.where(lower, s[:, part * half:], NEG_INF)
        s = own if part == 0 else jnp.concatenate([s[:, :half], own], axis=1)
        s_sink = lax.dot_general(q_rows, k_sink, nt_dims, preferred_element_type=F32) + sink_bias
        m_new, acc_new = online_update(jnp.concatenate([m_ref[r] for r in row_slices], axis=0),
                                       jnp.concatenate([acc_ref[r] for r in row_slices], axis=0),
                                       [(s, vs_ref[keys, :]), (s_sink, v_sink)])
        for h, r in enumerate(row_slices):
            m_ref[r] = m_new[h * half:(h + 1) * half]
            acc_ref[r] = acc_new[h * half:(h + 1) * half]


    def band(q_rows, first_key, n_keys, masks):
        keys = pl.ds(pl.multiple_of(first_key, half), n_keys)
        s = lax.dot_general(q_rows, kw_ref[keys, :], nt_dims, preferred_element_type=F32)
        groups = [s[:, g * half:(g + 1) * half] for g in range(n_keys // half)]
        s = jnp.concatenate([g if m is None else jnp.where(m, g, NEG_INF) for g, m in zip(groups, masks)], axis=1)
        p = jnp.exp2(s - jnp.max(s, axis=-1, keepdims=True))
        return _normalise(jnp.dot(p.astype(BF16), vw_ref[keys, :], preferred_element_type=F32))

    @pl.when(qb == 0)
    def _():
        qs_win = jnp.concatenate(qh, axis=0)
        row = _iota((H * Q, half), 0) & (Q - 1)
        col = _iota((H * Q, half), 1)
        ow_ref[...] = band(qs_win, 0, Q, [col <= row, col + half <= row])

    @pl.when(qb > 0)
    def _():
        row = _iota((H * half, half), 0) & (half - 1)
        col = _iota((H * half, half), 1)
        masks = [col > row, None, col <= row]
        for part in range(2):
            q_rows = jnp.concatenate([q[part * half:(part + 1) * half] for q in qh], axis=0)
            o_part = band(q_rows, (qb - 1) * Q + part * half, NSA_WIN + half, masks)
            for h in range(H):
                ow_ref[h * Q + part * half:h * Q + (part + 1) * half, :] = o_part[h * half:(h + 1) * half]

    gates = jax.nn.sigmoid(gate_ref[...])
    mixed = _stacked_gate(gates, 1) * _normalise(acc_ref[...]) + _stacked_gate(gates, 2) * ow_ref[...]
    o_ref[...] = (oc_ref[...] + _unstack_heads(mixed)).astype(o_ref.dtype)


def _nsa_attend(p_nsa, qtab, o_c, selb, flags, kv):
    B, S, _ = p_nsa.shape
    Q = NSA_Q
    nqb = S // Q
    nt = S // NSA_TILE
    gate_blk = (W_NSA - LANE) // LANE
    kern = functools.partial(_nsa_attn_kernel, nt=nt)
    slab = lambda width, col: pl.BlockSpec((None, S, width), lambda b, i, f: (b, 0, col))
    grid_spec = pltpu.PrefetchScalarGridSpec(
        num_scalar_prefetch=1,
        grid=(B, nqb),
        in_specs=[pl.BlockSpec((None, Q, NSA_HEADS * LANE), lambda b, i, f: (b, i, 0)),
                  pl.BlockSpec((2 * NSA_HEADS, LANE), lambda b, i, f: (0, 0)),
                  pl.BlockSpec((None, Q, LANE), lambda b, i, f: (b, i, gate_blk)),
                  pl.BlockSpec((None, Q, GROUP_W), lambda b, i, f: (b, i, 0)),
                  pl.BlockSpec((None, Q, LANE), lambda b, i, f: (b, i, 0)),
                  slab(2 * LANE, 0), slab(LANE, 2), slab(LANE, 3), slab(LANE, 4)],
        out_specs=pl.BlockSpec((None, Q, GROUP_W), lambda b, i, f: (b, i, 0)),
        scratch_shapes=[pltpu.VMEM((NSA_HEADS * Q, LANE), F32)] * 3,
    )
    return pl.pallas_call(
        kern, name="nsa_attend",
        grid_spec=grid_spec,
        out_shape=jax.ShapeDtypeStruct((B, S, GROUP_W), BF16),
        compiler_params=_cparams("parallel", "parallel"),
    )(flags, p_nsa, qtab, p_nsa, o_c, selb, kv, kv, kv, kv)


def _nsa_tables(S):
    nc = S // NSA_CMP_STRIDE
    n = np.arange(nc)[None, :]
    j = np.arange(LANE)[:, None]
    start = n * NSA_CMP_STRIDE
    ov = (start < (j + 1) * NSA_SLC_LEN) & (start + NSA_CMP_LEN - 1 >= j * NSA_SLC_LEN)
    ov &= (n < (S - NSA_CMP_LEN) // NSA_CMP_STRIDE + 1) & (j < S // NSA_SLC_LEN)
    pos = np.arange(S)
    k_zero = np.zeros((S, NSA_DK), np.float32)
    block_onehot = (pos[:, None] // NSA_SLC_LEN == np.arange(LANE)[None, :]).astype(np.float32)
    v_lanes = np.zeros((S, LANE), np.float32)
    v_lanes[:, ONES_LANE] = 1.0
    kv_table = np.concatenate([k_zero, _nsa_pos_lanes(pos), block_onehot, v_lanes,
                               k_zero, _nsa_pos_lanes(pos), v_lanes], axis=1)
    assert kv_table.shape[1] == W_KV
    return _nsa_query_table(), jnp.asarray(ov.astype(np.float32)), jnp.asarray(kv_table, dtype=BF16)


def _nsa(p_nsa, kv, uk, uv, pe_k, w1_k, w2_k, pe_v, w1_v, w2_v, tables):
    B, S, _ = p_nsa.shape
    qtab, overlap_t, _ = tables
    kc, vc = _nsa_compress(uk, uv, pe_k, w1_k, w2_k, pe_v, w1_v, w2_v)
    o_c, selb, blk_any = _nsa_select(p_nsa, qtab, kc, vc, overlap_t)
    per_tile = NSA_TILE // NSA_SLC_LEN
    nt = S // NSA_TILE
    not_sink = (np.arange(nt * per_tile) >= NSA_SINK // NSA_SLC_LEN).astype(np.float32)
    blk_any = blk_any[:, :, 0, :nt * per_tile] * not_sink
    flags = blk_any.reshape(B, S // NSA_Q, nt, per_tile).max(axis=-1)
    flags = (flags > 0).astype(jnp.int32).reshape(-1)
    return _nsa_attend(p_nsa, qtab, o_c, selb, flags, kv)


def _out_proj_kernel(h_ref, ya_ref, yb_ref, yc_ref, yd_ref, w_ref, g_ref, b_ref, o_ref, wb_ref):
    @pl.when(pl.program_id(0) == 0)
    def _():
        wb_ref[...] = w_ref[...].astype(BF16)

    mix = None
    for idx, y_ref in enumerate((ya_ref, yb_ref, yc_ref, yd_ref)):
        part = _mm(y_ref[...], wb_ref[idx * GROUP_W:(idx + 1) * GROUP_W, :])
        mix = part if mix is None else mix + part
    o_ref[...] = _layer_norm(DEEPNORM_ALPHA * h_ref[...] + mix, g_ref[...], b_ref[...])


def _out_proj(h2, ys, w_out, layer, g, b, tm=1024):
    T, D = h2.shape
    row = lambda w: pl.BlockSpec((tm, w), lambda i: (i, 0))
    const = lambda shape: pl.BlockSpec(shape, lambda i: (0,) * len(shape))
    return pl.pallas_call(
        _out_proj_kernel, name="out_proj_ln",
        grid=(T // tm,),
        in_specs=[row(D), row(GROUP_W), row(GROUP_W), row(GROUP_W), row(GROUP_W),
                  pl.BlockSpec((None, D, D), lambda i: (layer, 0, 0)), const((1, D)), const((1, D))],
        out_specs=row(D),
        out_shape=jax.ShapeDtypeStruct((T, D), F32),
        scratch_shapes=[pltpu.VMEM((D, D), BF16)],
        compiler_params=_cparams("arbitrary"),
    )(h2, *ys, w_out, g.reshape(1, D), b.reshape(1, D))


def _mlp_kernel(h_ref, w1_ref, w2_ref, g_ref, b_ref, o_ref, acc_ref):
    f = pl.program_id(1)

    @pl.when(f == 0)
    def _():
        acc_ref[...] = jnp.zeros_like(acc_ref)

    a = jnp.maximum(_mm(h_ref[...], w1_ref[...]), 0.0)
    acc_ref[...] += _mm(a * a, w2_ref[...])

    @pl.when(f == pl.num_programs(1) - 1)
    def _():
        o_ref[...] = _layer_norm(DEEPNORM_ALPHA * h_ref[...] + acc_ref[...], g_ref[...], b_ref[...])


def _mlp(h2, w1, w2, layer, g, b, tm=1024, tf=1024):
    T, D = h2.shape
    F = w1.shape[2]
    return pl.pallas_call(
        _mlp_kernel, name="mlp_ln",
        grid=(T // tm, F // tf),
        in_specs=[pl.BlockSpec((tm, D), lambda i, f: (i, 0)),
                  pl.BlockSpec((None, D, tf), lambda i, f: (layer, 0, f)),
                  pl.BlockSpec((None, tf, D), lambda i, f: (layer, f, 0)),
                  pl.BlockSpec((1, D), lambda i, f: (0, 0)),
                  pl.BlockSpec((1, D), lambda i, f: (0, 0))],
        out_specs=pl.BlockSpec((tm, D), lambda i, f: (i, 0)),
        out_shape=jax.ShapeDtypeStruct((T, D), F32),
        scratch_shapes=[pltpu.VMEM((tm, D), F32)],
        compiler_params=_cparams("parallel", "arbitrary"),
    )(h2, w1, w2, g.reshape(1, D), b.reshape(1, D))


def kernel(x, ln_emb_g, ln_emb_b, w_in, conv_w, conv_b, dt_bias, a_log, d_skip, ssm_norm_g, q_norm_g, w_uq, kv_norm_g, w_ukv, cmp_pe_k, cmp_w1_k, cmp_w2_k, cmp_pe_v, cmp_w1_v, cmp_w2_v, w_out, ln1_g, ln1_b, w_mlp1, w_mlp2, ln2_g, ln2_b):
    B, S, D = x.shape
    assert D == D_MODEL and S // NSA_SLC_LEN <= LANE
    assert S % NSA_TILE == 0 and S % NSA_Q == 0 and S % min(MLA_TILE, S) == 0 and S % SSM_CHUNK == 0
    T = B * S
    ret_tables = _ret_tables(S)
    mla_tables = _mla_tables(S)
    nsa_tables = _nsa_tables(S)
    h = x.reshape(T, D)
    for l in range(w_in.shape[0]):
        if l == 0:
            h, *proj = _in_proj(h, _layout_w_in(w_in[l]), nsa_tables[2], entry_ln=(ln_emb_g, ln_emb_b))
        else:
            proj = _in_proj(h, _layout_w_in(w_in[l]), nsa_tables[2])
        p_ssm, p_mla, p_ret, p_nsa, nsa_kv, uk, uv = proj
        cmp_rows = (B, S // NSA_CMP_STRIDE, uk.shape[-1])
        y_a, y_c = _ssm_and_retention(p_ssm.reshape(B, S, W_SSM), conv_w[l], conv_b[l], dt_bias[l], a_log[l],
                                      d_skip[l], ssm_norm_g[l], p_ret.reshape(B, S, W_RET), ret_tables)
        wq, wk, wv = _layout_mla_weights(w_uq[l], w_ukv[l])
        q, k, v = _mla_prep(p_mla.reshape(B, S, W_MLA), q_norm_g[l], kv_norm_g[l], wq, wk, wv, mla_tables)
        y_b = _mla_attn(q, k, v)
        y_d = _nsa(p_nsa.reshape(B, S, W_NSA), nsa_kv.reshape(B, S, W_KV), uk.reshape(cmp_rows), uv.reshape(cmp_rows),
                   cmp_pe_k[l], cmp_w1_k[l], cmp_w2_k[l], cmp_pe_v[l], cmp_w1_v[l], cmp_w2_v[l], nsa_tables)
        ys = [y.reshape(T, GROUP_W) for y in (y_a, y_b, y_c, y_d)]
        h = _out_proj(h, ys, w_out, l, ln1_g[l], ln1_b[l])
        h = _mlp(h, w_mlp1, w_mlp2, l, ln2_g[l], ln2_b[l])
    return h.reshape(B, S, D)
```

```python
import functools
import math

import jax
import jax.numpy as jnp
import numpy as np
from jax import lax
from jax.experimental import pallas as pl
from jax.experimental.pallas import tpu as pltpu

F32 = jnp.float32
BF16 = jnp.bfloat16

D_MODEL = 1024
DEPTH = 2
GROUP_W = D_MODEL // 4
SSM_HEADS = 4
SSM_HEAD_DIM = GROUP_W // SSM_HEADS
SSM_GROUPS = 2
SSM_STATE = 128
SSM_CONV = 4
SSM_CHUNK = 128
SSM_XBC = GROUP_W + 2 * SSM_GROUPS * SSM_STATE
MLA_HEADS = 4
MLA_NOPE = 64
MLA_ROPE = 32
MLA_V = GROUP_W // MLA_HEADS
MLA_Q_RANK = 256
MLA_KV_RANK = 128
RET_HEADS = 4
RET_DK = 64
RET_DV = GROUP_W // RET_HEADS
RET_CHUNK = 128
NSA_HEADS = 4
NSA_DK = 64
NSA_DV = GROUP_W // NSA_HEADS
NSA_CMP_LEN = 32
NSA_CMP_STRIDE = 16
NSA_CMP_HID = 256
NSA_SLC_LEN = 64
NSA_TOPN = 16
NSA_WIN = 512
D_FF = 4 * D_MODEL
NSA_Q = 512
ROPE_THETA = 10000.0
EPS = 1e-5
NEG_INF = -1e30
LOG2_E = math.log2(math.e)
FORCED_SCORE = 1e9
DEEPNORM_ALPHA = (2.0 * DEPTH) ** 0.25

IN_SPLITS = (
    GROUP_W, SSM_XBC, SSM_HEADS,
    MLA_Q_RANK, MLA_KV_RANK, MLA_ROPE,
    RET_HEADS * RET_DK, RET_HEADS * RET_DK, RET_HEADS * RET_DV, GROUP_W,
    NSA_HEADS * NSA_DK, NSA_DK, NSA_DV, NSA_DK, NSA_DV, NSA_DK, NSA_DV, 3 * NSA_HEADS,
)

LANE = 128
SUBLANE = 8
W_SSM = GROUP_W + SSM_XBC + LANE
W_MLA = MLA_Q_RANK + MLA_KV_RANK + LANE
W_RET = 4 * GROUP_W
W_NSA = NSA_HEADS * LANE + LANE + LANE
W_KV = 2 * LANE + 3 * LANE
W_PROJ = ((GROUP_W + SSM_XBC) + (MLA_Q_RANK + MLA_KV_RANK) + W_RET
          + (NSA_HEADS * NSA_DK + LANE) + 4 * NSA_DK + LANE)

RECURRENT_CHUNKS_PER_STEP = 4
NSA_TILE = 512
NSA_SINK = 128
MLA_TILE = 1024
VMEM_LIMIT = 48 * 1024 * 1024


def _cparams(*sem):
    return pltpu.CompilerParams(dimension_semantics=sem, vmem_limit_bytes=VMEM_LIMIT)


def _mm(a, b):
    return jnp.dot(a.astype(BF16), b.astype(BF16), preferred_element_type=F32)


def _mm_nt(a, b):
    return lax.dot_general(a.astype(BF16), b.astype(BF16), (((1,), (1,)), ((), ())),
                           preferred_element_type=F32)


def _split_f32(x):
    hi = x.astype(BF16)
    rest = x - hi.astype(F32)
    mid = rest.astype(BF16)
    lo = (rest - mid.astype(F32)).astype(BF16)
    return hi, mid, lo


def _mm_f32(a, b, exact, dims=(((1,), (0,)), ((), ()))):
    fixed, pieces = (a.astype(BF16), _split_f32(b)) if exact == "a" else (b.astype(BF16), _split_f32(a))
    out = None
    for piece in pieces:
        lhs, rhs = (fixed, piece) if exact == "a" else (piece, fixed)
        part = lax.dot_general(lhs, rhs, dims, preferred_element_type=F32)
        out = part if out is None else out + part
    return out


def _silu(x):
    return x * jax.nn.sigmoid(x)


def _softplus(x):
    return jnp.maximum(x, 0.0) + jnp.log1p(jnp.exp(-jnp.abs(x)))


def _layer_norm(x, g, b):
    mu = jnp.mean(x, axis=-1, keepdims=True)
    xc = x - mu
    var = jnp.mean(xc * xc, axis=-1, keepdims=True)
    return xc * lax.rsqrt(var + EPS) * g + b


def _iota(shape, dim):
    return lax.broadcasted_iota(jnp.int32, shape, dim)


def _pad_cols(w, width):
    return jnp.pad(w, ((0, 0), (0, width - w.shape[1])))


def _layout_w_in(w):
    offs = np.concatenate([[0], np.cumsum(IN_SPLITS)])
    p = [w[:, int(offs[i]):int(offs[i + 1])] for i in range(len(IN_SPLITS))]
    (ssm_z, ssm_xbc, ssm_dt, mla_cq, mla_ckv, mla_kr, ret_q, ret_k, ret_v, ret_g,
     nsa_q, nsa_kc, nsa_vc, nsa_ks, nsa_vs, nsa_kw, nsa_vw, nsa_gate) = p
    small = _pad_cols(jnp.concatenate([mla_kr, ssm_dt, nsa_gate], axis=1), LANE)
    cols = [ssm_z, ssm_xbc, ret_q, ret_k, ret_v, ret_g, nsa_q, nsa_ks, nsa_vs, nsa_kw, nsa_vw,
            mla_cq, mla_ckv, nsa_kc, nsa_vc, small]
    out = jnp.concatenate(cols, axis=1)
    assert out.shape[1] == W_PROJ
    return out.astype(BF16)


def _in_proj_kernel(*refs, entry_ln):
    if entry_ln:
        h_ref, g_ref, b_ref, w_ref, kvtab_ref, hn_ref, *outs = refs
        hn = _layer_norm(h_ref[...], g_ref[...], b_ref[...])
        hn_ref[...] = hn
    else:
        h_ref, w_ref, kvtab_ref, *outs = refs
        hn = h_ref[...]
    ssm_ref, mla_ref, ret_ref, nsa_ref, kv_ref, uk_ref, uv_ref, kcv_ref = outs
    hb = hn.astype(BF16)
    tm = hb.shape[0]
    off = 0

    def project(width):
        nonlocal off
        out = jnp.dot(hb, w_ref[:, off:off + width], preferred_element_type=F32)
        off += width
        return out

    ssm_ref[:, 0:GROUP_W + SSM_XBC] = project(GROUP_W + SSM_XBC)
    ret_ref[...] = project(W_RET)
    q = project(NSA_HEADS * NSA_DK)
    kv = project(4 * NSA_DK)
    latent = MLA_Q_RANK + MLA_KV_RANK
    mla_kc = project(latent + LANE)
    mla_ref[:, 0:latent] = mla_kc[:, 0:latent]
    kc_lane = NSA_HEADS * LANE
    nsa_ref[:, kc_lane:kc_lane + LANE] = mla_kc[:, latent:]
    small = project(LANE)
    lane = _iota((tm, LANE), 1)
    low = lane < NSA_DK
    for h in range(NSA_HEADS):
        pair = q[:, (h // 2) * LANE:(h // 2 + 1) * LANE]
        head = pair if h % 2 == 0 else pltpu.roll(pair, NSA_DK, 1)
        nsa_ref[:, h * LANE:(h + 1) * LANE] = jnp.where(low, head, 0.0)
    mla_ref[:, MLA_Q_RANK + MLA_KV_RANK:] = jnp.where(lane < MLA_ROPE, small, 0.0)
    ssm_ref[:, GROUP_W + SSM_XBC:] = jnp.where(lane < SSM_HEADS, pltpu.roll(small, LANE - MLA_ROPE, 1), 0.0)
    gate_at = MLA_ROPE + SSM_HEADS
    nsa_ref[:, kc_lane + LANE:] = jnp.where(lane < 3 * NSA_HEADS, pltpu.roll(small, LANE - gate_at, 1), 0.0)
    groups = h_ref.shape[0] // NSA_CMP_STRIDE
    kc_lane = NSA_HEADS * LANE
    kcv_ref[...] = nsa_ref[:, kc_lane:kc_lane + LANE]
    for t in range(NSA_CMP_STRIDE):
        piece = kcv_ref[pl.ds(t, groups, stride=NSA_CMP_STRIDE), :]
        uk_ref[:, t * NSA_DK:(t + 1) * NSA_DK] = piece[:, :NSA_DK]
        uv_ref[:, t * NSA_DV:(t + 1) * NSA_DV] = piece[:, NSA_DK:]
    sel_kv, win_kv = kv[:, :LANE], kv[:, LANE:]
    pieces = {0: sel_kv, 2: pltpu.roll(sel_kv, NSA_DK, 1), 3: win_kv, 4: pltpu.roll(win_kv, NSA_DK, 1)}
    for slab in range(W_KV // LANE):
        lanes = slice(slab * LANE, (slab + 1) * LANE)
        tab = kvtab_ref[:, lanes]
        if slab in pieces:
            kv_ref[:, lanes] = (jnp.where(low, pieces[slab], 0.0) + tab.astype(F32)).astype(BF16)
        else:
            kv_ref[:, lanes] = tab


def _in_proj(h2, w_p, kv_table, entry_ln=None, tm=512):
    T, D = h2.shape
    S = kv_table.shape[0]
    widths = (W_SSM, W_MLA, W_RET, W_NSA)
    half = NSA_CMP_STRIDE * NSA_DK
    row = lambda w: pl.BlockSpec((tm, w), lambda i: (i, 0))
    const = lambda shape: pl.BlockSpec(shape, lambda i: (0,) * len(shape))
    in_specs = [const((D, W_PROJ)), pl.BlockSpec((tm, W_KV), lambda i: (i % (S // tm), 0))]
    out_specs = [row(w) for w in widths + (W_KV,)] + [pl.BlockSpec((tm // NSA_CMP_STRIDE, half), lambda i: (i, 0))] * 2
    out_shape = ([jax.ShapeDtypeStruct((T, w), F32) for w in widths] + [jax.ShapeDtypeStruct((T, W_KV), BF16)]
                 + [jax.ShapeDtypeStruct((T // NSA_CMP_STRIDE, half), F32)] * 2)
    operands = (w_p, kv_table)
    if entry_ln is not None:
        in_specs = [const((1, D)), const((1, D))] + in_specs
        out_specs = [row(D)] + out_specs
        out_shape = [jax.ShapeDtypeStruct((T, D), F32)] + out_shape
        operands = tuple(v.reshape(1, D) for v in entry_ln) + operands
    return pl.pallas_call(
        functools.partial(_in_proj_kernel, entry_ln=entry_ln is not None), name="in_proj",
        grid=(T // tm,),
        in_specs=[row(D)] + in_specs,
        out_specs=out_specs,
        out_shape=out_shape,
        scratch_shapes=[pltpu.VMEM((tm, LANE), F32)],
        compiler_params=_cparams("parallel"),
    )(h2, *operands)


def _ssm_chunk(p_ref, cw_ref, cb_ref, dtb_ref, alog_ref, dskip_ref, ng_ref, o_ref, state_ref, ext_ref):
    L, H, P, N = SSM_CHUNK, SSM_HEADS, SSM_HEAD_DIM, SSM_STATE
    z = p_ref[:, 0:GROUP_W]
    ext_ref[SUBLANE:SUBLANE + L, :] = p_ref[:, GROUP_W:GROUP_W + SSM_XBC]
    ext = ext_ref[...]
    conv = cb_ref[...] + ext[SUBLANE:] * cw_ref[SSM_CONV - 1:SSM_CONV, :]
    for j in range(SSM_CONV - 1):
        shift = SSM_CONV - 1 - j
        conv = conv + pltpu.roll(ext, shift, 0)[SUBLANE:] * cw_ref[j:j + 1, :]
    ext_ref[0:SUBLANE, :] = ext_ref[L:L + SUBLANE, :]
    xbc = _silu(conv)
    xs = xbc[:, 0:GROUP_W]
    b_in = xbc[:, GROUP_W:GROUP_W + SSM_GROUPS * N]
    c_in = xbc[:, GROUP_W + SSM_GROUPS * N:]

    dt = _softplus(p_ref[:, GROUP_W + SSM_XBC:] + dtb_ref[...])
    a = dt * (-jnp.exp(alog_ref[...]))
    row = _iota((L, L), 0)
    col = _iota((L, L), 1)
    tril = col <= row
    cs = _mm_f32(jnp.where(tril, 1.0, 0.0), a, exact="a")
    cs_t = cs.T
    ecs = jnp.exp(cs)
    dte = jnp.exp(cs[L - 1:L, :] - cs)
    first_head_of_pair = _iota((L, LANE), 1) < P

    def expand(x):
        pairs = [jnp.where(first_head_of_pair, x[:, h:h + 1], x[:, h + 1:h + 2]) for h in range(0, H, LANE // P)]
        return jnp.concatenate(pairs, axis=1)

    assert LANE == 2 * P
    dt_x = expand(dt)
    ecs_x = expand(ecs)
    dte_x = expand(dte)

    xdt = xs * dt_x
    wx = xdt * dte_x
    head_of_lane = _iota((L, H * P), 1) // P
    y = xs * dskip_ref[...]
    y_off = []
    rep = H // SSM_GROUPS
    for g in range(SSM_GROUPS):
        cg = c_in[:, g * N:(g + 1) * N]
        bg = b_in[:, g * N:(g + 1) * N]
        cb = _mm_nt(cg, bg)
        for h in range(g * rep, (g + 1) * rep):
            diff = cs[:, h:h + 1] - cs_t[h:h + 1, :]
            seg = jnp.where(tril, jnp.exp(jnp.where(tril, diff, 0.0)), 0.0)
            yh = _mm(cb * seg, xdt)
            y = y + jnp.where(head_of_lane == h, yh, 0.0)
        lanes = slice(g * rep * P, (g + 1) * rep * P)
        st_prev = state_ref[:, lanes]
        y_off.append(_mm(cg, st_prev))
        state_ref[:, lanes] = st_prev * ecs_x[L - 1:L, lanes] + _mm(bg.T, wx[:, lanes])
    y = y + jnp.concatenate(y_off, axis=1) * ecs_x
    y = y * _silu(z)
    ms = jnp.mean(y * y, axis=-1, keepdims=True)
    o_ref[...] = (y * lax.rsqrt(ms + EPS) * ng_ref[...]).astype(o_ref.dtype)


def _recurrent_kernel(ps_ref, cw_ref, cb_ref, dtb_ref, alog_ref, dskip_ref, ng_ref,
                      pr_ref, cos_ref, sin_ref, dec_ref, zeta_ref, xi_ref, cd_ref,
                      oa_ref, oc_ref, sstate_ref, ext_ref, rstate_ref):
    @pl.when(pl.program_id(0) == 0)
    def _():
        sstate_ref[...] = jnp.zeros_like(sstate_ref)
        rstate_ref[...] = jnp.zeros_like(rstate_ref)
        ext_ref[:, 0:SUBLANE, :] = jnp.zeros((ext_ref.shape[0], SUBLANE, SSM_XBC), F32)

    L = SSM_CHUNK
    for sub in range(ps_ref.shape[1] // L):
        rows = pl.ds(sub * L, L)
        for b in range(ps_ref.shape[0]):
            _ssm_chunk(ps_ref.at[b, rows], cw_ref, cb_ref, dtb_ref, alog_ref, dskip_ref, ng_ref,
                       oa_ref.at[b, rows], sstate_ref.at[b], ext_ref.at[b])
            _ret_chunk(pr_ref.at[b, rows], cos_ref.at[rows], sin_ref.at[rows], dec_ref, zeta_ref, xi_ref, cd_ref,
                       oc_ref.at[b, rows], rstate_ref.at[b])


def _ret_chunk(p_ref, cos_ref, sin_ref, dec_ref, zeta_ref, xi_ref, cd_ref, o_ref, state_ref):
    L, H, DK, DV = RET_CHUNK, RET_HEADS, RET_DK, RET_DV
    W = H * DK
    q = p_ref[:, 0:W]
    k = p_ref[:, W:2 * W]
    v = p_ref[:, 2 * W:3 * W]
    gate = p_ref[:, 3 * W:4 * W]
    lane = _iota((L, W), 1)
    first_half = (lane % DK) < (DK // 2)
    head_of_lane = lane // DK

    def rope(x):
        partner = jnp.where(first_half, pltpu.roll(x, W - DK // 2, 1), pltpu.roll(x, DK // 2, 1))
        return x * cos_ref[...] + partner * sin_ref[...]

    qr = rope(q)
    kr = rope(k) * (DK ** -0.5)
    y = jnp.zeros((L, H * DV), F32)
    for h in range(H):
        qh = jnp.where(head_of_lane == h, qr, 0.0)
        sc = _mm_nt(qh, kr) * dec_ref[h]
        y = y + jnp.where(head_of_lane == h, _mm(sc, v), 0.0)
    st = state_ref[...]
    y = y + _mm(qr * xi_ref[...], st)
    same_head = (_iota((W, H * DV), 0) // DK) == (_iota((W, H * DV), 1) // DV)
    kv = _mm((kr * zeta_ref[...]).T, v)
    state_ref[...] = st * cd_ref[...] + jnp.where(same_head, kv, 0.0)
    assert DV & (DV - 1) == 0
    ms = _mm_f32(y * y, jnp.where(same_head, 1.0 / DV, 0.0), exact="b")
    o_ref[...] = (y * lax.rsqrt(ms + EPS) * _silu(gate)).astype(o_ref.dtype)


def _ret_tables(S):
    H, DK, L = RET_HEADS, RET_DK, RET_CHUNK
    inv = ROPE_THETA ** (-np.arange(0, DK, 2, dtype=np.float64) / DK)
    ang = np.arange(S, dtype=np.float64)[:, None] * inv[None, :]
    cos, sin = np.cos(ang), np.sin(ang)
    cos_t = np.tile(np.concatenate([cos, cos], axis=1), (1, H))
    sin_t = np.tile(np.concatenate([-sin, sin], axis=1), (1, H))
    log_gamma = np.log1p(-np.exp2(-5.0 - np.arange(H, dtype=np.float64)))
    pos = np.arange(L, dtype=np.float64)
    diff = pos[:, None] - pos[None, :]
    decay_in = np.where(diff >= 0, np.exp(np.maximum(diff, 0.0)[None] * log_gamma[:, None, None]), 0.0)
    zeta = np.exp((L - 1 - pos)[None] * log_gamma[:, None])
    xi = np.exp((pos + 1.0)[None] * log_gamma[:, None])
    chunk_decay = np.exp(L * log_gamma)
    zeta_x = np.repeat(zeta.T, DK, axis=1)
    xi_x = np.repeat(xi.T, DK, axis=1)
    cd_x = np.repeat(chunk_decay, RET_DV).reshape(1, H * RET_DV)
    return tuple(jnp.asarray(t, dtype=F32) for t in (cos_t, sin_t, decay_in, zeta_x, xi_x, cd_x))


def _ssm_and_retention(p_ssm, conv_w, conv_b, dt_bias, a_log, d_skip, norm_g, p_ret, tables):
    B, S, _ = p_ssm.shape
    L, H = SSM_CHUNK, RET_HEADS
    assert RET_CHUNK == L
    W = H * RET_DK
    cos_t, sin_t, decay_in, zeta_x, xi_x, cd_x = tables
    pad_h = lambda v: jnp.pad(v, (0, LANE - SSM_HEADS)).reshape(1, LANE)
    const = lambda shape: pl.BlockSpec(shape, lambda c: (0,) * len(shape))
    rows = RECURRENT_CHUNKS_PER_STEP * L
    assert S % rows == 0
    chunk = lambda width: pl.BlockSpec((B, rows, width), lambda c: (0, c, 0))
    out_shape = jax.ShapeDtypeStruct((B, S, GROUP_W), BF16)
    return pl.pallas_call(
        _recurrent_kernel, name="ssm_retention",
        grid=(S // rows,),
        in_specs=[chunk(W_SSM),
                  const((SSM_CONV, SSM_XBC)), const((1, SSM_XBC)), const((1, LANE)), const((1, LANE)),
                  const((1, GROUP_W)), const((1, GROUP_W)),
                  chunk(W_RET),
                  pl.BlockSpec((rows, W), lambda c: (c, 0)), pl.BlockSpec((rows, W), lambda c: (c, 0)),
                  const((H, L, L)), const((L, W)), const((L, W)), const((1, H * RET_DV))],
        out_specs=[chunk(GROUP_W), chunk(GROUP_W)],
        out_shape=[out_shape, out_shape],
        scratch_shapes=[pltpu.VMEM((B, SSM_STATE, GROUP_W), F32),
                        pltpu.VMEM((B, L + SUBLANE, SSM_XBC), F32),
                        pltpu.VMEM((B, W, H * RET_DV), F32)],
        compiler_params=_cparams("arbitrary"),
    )(p_ssm, conv_w, conv_b.reshape(1, -1), pad_h(dt_bias), pad_h(a_log),
      jnp.repeat(d_skip, SSM_HEAD_DIM).reshape(1, GROUP_W), norm_g.reshape(1, GROUP_W),
      p_ret, cos_t, sin_t, decay_in, zeta_x, xi_x, cd_x)


def _mla_prep_kernel(p_ref, qg_ref, wq_ref, kvg_ref, wk_ref, wv_ref, cos_ref, sin_ref,
                     q_ref, k_ref, v_ref):
    tm = p_ref.shape[0]
    cq = p_ref[:, 0:MLA_Q_RANK]
    ckv = p_ref[:, MLA_Q_RANK:MLA_Q_RANK + MLA_KV_RANK]
    kr = p_ref[:, MLA_Q_RANK + MLA_KV_RANK:]

    def rms(x, g):
        return x * lax.rsqrt(jnp.mean(x * x, axis=-1, keepdims=True) + EPS) * g

    q = _mm(rms(cq, qg_ref[...]), wq_ref[...])
    kvl = rms(ckv, kvg_ref[...])
    kn = _mm(kvl, wk_ref[...])
    vv = _mm(kvl, wv_ref[...])
    kr_sh = pltpu.roll(kr, MLA_NOPE, 1)
    lane = _iota((tm, LANE), 1)
    half = MLA_ROPE // 2
    low = (lane >= MLA_NOPE) & (lane < MLA_NOPE + half)
    cos = cos_ref[...]
    sin = sin_ref[...]

    def rope(x):
        partner = jnp.where(low, pltpu.roll(x, LANE - half, 1), pltpu.roll(x, half, 1))
        return x * cos + partner * sin

    scale = (MLA_NOPE + MLA_ROPE) ** -0.5 * LOG2_E
    k_pe = rope(kr_sh)
    for h in range(MLA_HEADS):
        sl = slice(h * LANE, (h + 1) * LANE)
        q_ref[h] = (rope(q[:, sl]) * scale).astype(BF16)
        k_ref[h] = (kn[:, sl] + k_pe).astype(BF16)
        v_ref[h] = jnp.where(lane == _mla_ones_lane(h), 1.0, vv[:, sl]).astype(BF16)


def _mla_tables(S):
    inv = ROPE_THETA ** (-np.arange(0, MLA_ROPE, 2, dtype=np.float64) / MLA_ROPE)
    ang = np.arange(S, dtype=np.float64)[:, None] * inv[None, :]
    cos, sin = np.cos(ang), np.sin(ang)
    tail = LANE - MLA_NOPE - MLA_ROPE
    cos_t = np.concatenate([np.ones((S, MLA_NOPE)), cos, cos, np.ones((S, tail))], axis=1)
    sin_t = np.concatenate([np.zeros((S, MLA_NOPE)), -sin, sin, np.zeros((S, tail))], axis=1)
    return jnp.asarray(cos_t, dtype=F32), jnp.asarray(sin_t, dtype=F32)


def _layout_mla_weights(w_uq, w_ukv):
    H = MLA_HEADS
    dq = MLA_NOPE + MLA_ROPE
    wq = jnp.concatenate([_pad_cols(w_uq[:, h * dq:(h + 1) * dq], LANE) for h in range(H)], axis=1)
    dkv = MLA_NOPE + MLA_V
    wk, wv = [], []
    for h in range(H):
        blk = w_ukv[:, h * dkv:(h + 1) * dkv]
        wk.append(_pad_cols(blk[:, :MLA_NOPE], LANE))
        v = blk[:, MLA_NOPE:]
        zero = jnp.zeros_like(v)
        wv.append(jnp.concatenate([v, zero] if h % 2 == 0 else [zero, v], axis=1))
    return wq.astype(BF16), jnp.concatenate(wk, axis=1).astype(BF16), jnp.concatenate(wv, axis=1).astype(BF16)


def _mla_prep(p_mla, q_norm_g, kv_norm_g, wq, wk, wv, tables, tm=1024):
    B, S, _ = p_mla.shape
    H = MLA_HEADS
    cos_t, sin_t = tables
    const = lambda shape: pl.BlockSpec(shape, lambda b, i: (0,) * len(shape))
    qkv_spec = pl.BlockSpec((None, H, tm, LANE), lambda b, i: (b, 0, i, 0))
    qkv_shape = jax.ShapeDtypeStruct((B, H, S, LANE), BF16)
    return pl.pallas_call(
        _mla_prep_kernel, name="mla_prep",
        grid=(B, S // tm),
        in_specs=[pl.BlockSpec((None, tm, W_MLA), lambda b, i: (b, i, 0)),
                  const((1, MLA_Q_RANK)), const((MLA_Q_RANK, H * LANE)),
                  const((1, MLA_KV_RANK)), const((MLA_KV_RANK, H * LANE)), const((MLA_KV_RANK, H * LANE)),
                  pl.BlockSpec((tm, LANE), lambda b, i: (i, 0)),
                  pl.BlockSpec((tm, LANE), lambda b, i: (i, 0))],
        out_specs=[qkv_spec, qkv_spec, qkv_spec],
        out_shape=[qkv_shape, qkv_shape, qkv_shape],
        compiler_params=_cparams("parallel", "parallel"),
    )(p_mla, q_norm_g.reshape(1, -1), wq, kv_norm_g.reshape(1, -1), wk, wv, cos_t, sin_t)


def _mla_ones_lane(h):
    return MLA_V if h % 2 == 0 else 0


def _mla_attn_kernel(qi_ref, kj_ref, q_ref, k_ref, v_ref, o_ref, m_ref, acc_ref):
    H = MLA_HEADS
    tq, tk = q_ref.shape[1], k_ref.shape[1]
    i = qi_ref[pl.program_id(1)]
    j = kj_ref[pl.program_id(1)]

    @pl.when(j == 0)
    def _():
        m_ref[...] = jnp.full(m_ref.shape, NEG_INF, F32)
        acc_ref[...] = jnp.zeros_like(acc_ref)

    def sweep(blocks):
        nt_dims = (((1,), (1,)), ((), ()))
        scores = [[lax.dot_general(q_ref[h, r0:r0 + nr, :], k_ref[h, 0:nk, :], nt_dims, preferred_element_type=F32)
                   for (r0, nr, nk, _) in blocks] for h in range(H)]
        for h in range(H):
            for (r0, nr, nk, offset), s in zip(blocks, scores[h]):
                if offset is not None:
                    s = jnp.where(_iota((nr, nk), 1) - _iota((nr, nk), 0) <= offset, s, NEG_INF)
                rows = slice(r0, r0 + nr)
                m_prev = m_ref[h, rows]
                m_new = jnp.maximum(m_prev, jnp.max(s, axis=-1, keepdims=True))
                p = jnp.exp2(s - jnp.tile(m_new, (1, nk // LANE)))
                acc_ref[h, rows] = (jnp.exp2(m_prev - m_new) * acc_ref[h, rows]
                                    + jnp.dot(p.astype(BF16), v_ref[h, 0:nk, :], preferred_element_type=F32))
                m_ref[h, rows] = m_new

    assert tq == tk
    half = tq // 2

    @pl.when(j < i)
    def _():
        sweep([(0, tq, tk, None)])

    @pl.when(j == i)
    def _():
        sweep([(0, half, half, 0), (half, half, tk, half)])
        lane = _iota((tq, LANE), 1)
        for pair in range(H // 2):
            he, ho = 2 * pair, 2 * pair + 1
            acc_e, acc_o = acc_ref[he], acc_ref[ho]
            le = acc_e[:, _mla_ones_lane(he):_mla_ones_lane(he) + 1]
            lo = acc_o[:, _mla_ones_lane(ho):_mla_ones_lane(ho) + 1]
            o_ref[:, pair * LANE:(pair + 1) * LANE] = jnp.where(lane < MLA_V, acc_e / le, acc_o / lo).astype(o_ref.dtype)


def _mla_attn(q, k, v):
    B, H, S, _ = q.shape
    t = min(MLA_TILE, S)
    tq = t
    pairs = [(i, j) for i in range(S // tq) for j in range((i + 1) * tq // t)]
    qi = jnp.asarray([p[0] for p in pairs], jnp.int32)
    kj = jnp.asarray([p[1] for p in pairs], jnp.int32)
    grid_spec = pltpu.PrefetchScalarGridSpec(
        num_scalar_prefetch=2,
        grid=(B, len(pairs)),
        in_specs=[pl.BlockSpec((None, H, tq, LANE), lambda b, p, qi, kj: (b, 0, qi[p], 0)),
                  pl.BlockSpec((None, H, t, LANE), lambda b, p, qi, kj: (b, 0, kj[p], 0)),
                  pl.BlockSpec((None, H, t, LANE), lambda b, p, qi, kj: (b, 0, kj[p], 0))],
        out_specs=pl.BlockSpec((None, tq, GROUP_W), lambda b, p, qi, kj: (b, qi[p], 0)),
        scratch_shapes=[pltpu.VMEM((H, tq, LANE), F32), pltpu.VMEM((H, tq, LANE), F32)],
    )
    return pl.pallas_call(
        _mla_attn_kernel, name="mla_attn",
        grid_spec=grid_spec,
        out_shape=jax.ShapeDtypeStruct((B, S, GROUP_W), BF16),
        compiler_params=_cparams("parallel", "arbitrary"),
    )(qi, kj, q, k, v)


POS_HI = NSA_DK
POS_LO = NSA_DK + 3
POS_ONE = NSA_DK + 6
ONES_LANE = NSA_DV


def _split_bf16(x, parts=3):
    out, rem = [], np.float64(x)
    for _ in range(parts):
        piece = np.float64(np.float32(rem).astype(jnp.bfloat16).astype(np.float32))
        out.append(float(piece))
        rem = rem - piece
    return out


def _nsa_query_table():
    H = NSA_HEADS
    tab = np.zeros((2 * H, LANE), np.float32)
    for h in range(H):
        c = 2.0 ** (-8.0 * (h + 1) / H) * LOG2_E
        pieces = _split_bf16(c)
        tab[h, POS_HI:POS_HI + 3] = pieces
        tab[h, POS_LO:POS_LO + 3] = pieces
        tab[H + h, POS_ONE] = -sum(pieces)
    return jnp.asarray(tab)


def _nsa_pos_lanes(pos, lo_offset=0.0):
    t = np.zeros((len(pos), LANE - NSA_DK), np.float32)
    t[:, POS_HI - NSA_DK:POS_HI - NSA_DK + 3] = (NSA_SLC_LEN * (pos // NSA_SLC_LEN))[:, None]
    t[:, POS_LO - NSA_DK:POS_LO - NSA_DK + 3] = (pos % NSA_SLC_LEN + lo_offset)[:, None]
    t[:, POS_ONE - NSA_DK] = 1.0
    return t


def _nsa_queries(q_ref, qtab_ref, qb):
    Q, H = q_ref.shape[0], NSA_HEADS
    qpos = (qb * Q + _iota((Q, 1), 0)).astype(F32)
    out = []
    for h in range(H):
        q = q_ref[:, h * LANE:(h + 1) * LANE] * (NSA_DK ** -0.5 * LOG2_E)
        out.append((q + qtab_ref[h:h + 1, :] + qtab_ref[H + h:H + h + 1, :] * qpos).astype(BF16))
    return out


def _normalise(o):
    return o / o[:, ONES_LANE:ONES_LANE + 1]


def _stacked_gate(gates, branch):
    lanes = [3 * h + branch for h in range(NSA_HEADS)]
    return jnp.concatenate([gates[:, c:c + 1] for c in lanes], axis=0)


def _unstack_heads(o):
    Q = o.shape[0] // NSA_HEADS
    lane = _iota((Q, LANE), 1)
    out = []
    for pair in range(NSA_HEADS // 2):
        even = o[(2 * pair) * Q:(2 * pair + 1) * Q]
        odd = o[(2 * pair + 1) * Q:(2 * pair + 2) * Q]
        out.append(jnp.where(lane < NSA_DV, even, pltpu.roll(odd, NSA_DV, 1)))
    return jnp.concatenate(out, axis=1)


def _nsa_cmp_kernel(uk_ref, uv_ref, pek_ref, pev_ref, w1k_ref, w1v_ref, w2k_ref, w2v_ref, cpos_ref,
                    kc_ref, vc_ref, sh_ref):
    nb = uk_ref.shape[0]
    half = uk_ref.shape[1]

    def hidden(u_ref, pe_ref, w1_ref):
        u = u_ref[...]
        first = _mm(u + pe_ref[0:1, :], w1_ref[0:half, :])
        second = _mm(u + pe_ref[1:2, :], w1_ref[half:2 * half, :])
        sh_ref[0:nb, :] = second
        sh_ref[nb:nb + SUBLANE, :] = jnp.zeros((SUBLANE, NSA_CMP_HID), F32)
        return first + sh_ref[pl.ds(1, nb), :]

    hk = _silu(hidden(uk_ref, pek_ref, w1k_ref))
    hv = _silu(hidden(uv_ref, pev_ref, w1v_ref))
    kc_ref[...] = (_mm(hk, w2k_ref[...]) + cpos_ref[...]).astype(BF16)
    ones_lane = jnp.where(_iota((1, LANE), 1) == ONES_LANE, 1.0, 0.0)
    vc_ref[...] = (_mm(hv, w2v_ref[...]) + ones_lane).astype(BF16)


def _nsa_compress(uk, uv, pe_k, w1_k, w2_k, pe_v, w1_v, w2_v):
    B, nb, half = uk.shape
    hid = NSA_CMP_HID
    const = lambda shape: pl.BlockSpec(shape, lambda b: (0,) * len(shape))
    w2k = _pad_cols(w2_k, LANE).astype(BF16)
    w2v = _pad_cols(w2_v, LANE).astype(BF16)
    centre = _nsa_pos_lanes(np.arange(nb) * NSA_CMP_STRIDE, 0.5 * (NSA_CMP_LEN - 1))
    cpos = jnp.asarray(np.concatenate([np.zeros((nb, NSA_DK), np.float32), centre], axis=1))
    out_spec = pl.BlockSpec((None, nb, LANE), lambda b: (b, 0, 0))
    out_shape = jax.ShapeDtypeStruct((B, nb, LANE), BF16)
    return pl.pallas_call(
        _nsa_cmp_kernel, name="nsa_compress",
        grid=(B,),
        in_specs=[pl.BlockSpec((None, nb, half), lambda b: (b, 0, 0)),
                  pl.BlockSpec((None, nb, half), lambda b: (b, 0, 0)),
                  const((2, half)), const((2, half)),
                  const((2 * half, hid)), const((2 * half, hid)),
                  const((hid, LANE)), const((hid, LANE)), const((nb, LANE))],
        out_specs=[out_spec, out_spec],
        out_shape=[out_shape, out_shape],
        scratch_shapes=[pltpu.VMEM((nb + SUBLANE, hid), F32)],
        compiler_params=_cparams("parallel"),
    )(uk, uv, pe_k.reshape(2, half), pe_v.reshape(2, half), w1_k.astype(BF16), w1_v.astype(BF16), w2k, w2v, cpos)


def _nsa_sel_kernel(q_ref, qtab_ref, gate_ref, kc_ref, vc_ref, ovt_ref, oc_ref, selb_ref, any_ref, *, n_slc, top_n):
    Q, H = q_ref.shape[0], NSA_HEADS
    qb = pl.program_id(1)
    nc = kc_ref.shape[0]
    pick = functools.partial(_nsa_pick_blocks, qb=qb, selb_ref=selb_ref, any_ref=any_ref, n_slc=n_slc, top_n=top_n)
    qs = jnp.concatenate(_nsa_queries(q_ref, qtab_ref, qb), axis=0)
    nt_dims = (((1,), (1,)), ((), ()))

    def attend(ncols):
        s = lax.dot_general(qs, kc_ref[0:ncols, :], nt_dims, preferred_element_type=F32)
        qpos = qb * Q + (_iota((H * Q, ncols), 0) & (Q - 1))
        block_end = _iota((H * Q, ncols), 1) * NSA_CMP_STRIDE + (NSA_CMP_LEN - 1)
        s = jnp.where(block_end <= qpos, s, NEG_INF)
        e = jnp.exp2(s - jnp.max(s, axis=-1, keepdims=True))
        qpos_col = qb * Q + (_iota((H * Q, 1), 0) & (Q - 1))
        has_block = jnp.where(qpos_col >= NSA_CMP_LEN - 1, 1.0, 0.0)
        p = e * (has_block / jnp.sum(e, axis=-1, keepdims=True))
        o_c = jnp.dot(p.astype(BF16), vc_ref[0:ncols, :], preferred_element_type=F32)
        oc_ref[...] = _unstack_heads(_stacked_gate(jax.nn.sigmoid(gate_ref[...]), 0) * o_c)
        p_sum = p[0:Q]
        for h in range(1, H):
            p_sum = p_sum + p[h * Q:(h + 1) * Q]
        pick(_mm_f32(ovt_ref[:, 0:ncols], p_sum, exact="a", dims=nt_dims))

    tiles_needed = ((qb + 1) * Q // NSA_CMP_STRIDE + LANE - 1) // LANE
    for tiles in range(1, nc // LANE + 1):
        pl.when(tiles_needed == tiles)(functools.partial(attend, tiles * LANE))


def _nsa_pick_blocks(imp, qb, selb_ref, any_ref, n_slc, top_n):
    Q = imp.shape[1]
    blk = _iota((LANE, Q), 0)
    q_blk = (qb * Q + _iota((LANE, Q), 1)) >> int(math.log2(NSA_SLC_LEN))
    causal = blk <= q_blk
    for forced_blk in (0, q_blk, q_blk - 1):
        imp = jnp.where(blk == forced_blk, FORCED_SCORE, imp)
    imp = jnp.where(causal, imp, -1.0)
    imp = jnp.where(blk < n_slc, imp, -2.0)
    blk_f = blk.astype(F32)
    sel = jnp.zeros((LANE, Q), F32)
    for _ in range(top_n):
        m = jnp.max(imp, axis=0, keepdims=True)
        first = jnp.min(jnp.where(imp == m, blk_f, float(LANE)), axis=0, keepdims=True)
        hit = blk_f == first
        sel = jnp.where(hit, 1.0, sel)
        imp = jnp.where(hit, -3.0, imp)
    sel = jnp.where(causal, sel, 0.0).T
    selb_ref[...] = jnp.where(sel > 0.5, 0.0, NEG_INF).astype(BF16)
    any_ref[...] = jnp.max(sel, axis=0, keepdims=True)


def _nsa_select(p_nsa, qtab, kc, vc, overlap_t):
    B, S, _ = p_nsa.shape
    Q = NSA_Q
    nqb = S // Q
    nc = kc.shape[1]
    n_slc = S // NSA_SLC_LEN
    kern = functools.partial(_nsa_sel_kernel, n_slc=n_slc, top_n=min(NSA_TOPN, n_slc))
    return pl.pallas_call(
        kern, name="nsa_select",
        grid=(B, nqb),
        in_specs=[pl.BlockSpec((None, Q, NSA_HEADS * LANE), lambda b, i: (b, i, 0)),
                  pl.BlockSpec((2 * NSA_HEADS, LANE), lambda b, i: (0, 0)),
                  pl.BlockSpec((None, Q, LANE), lambda b, i: (b, i, (W_NSA - LANE) // LANE)),
                  pl.BlockSpec((None, nc, LANE), lambda b, i: (b, 0, 0)),
                  pl.BlockSpec((None, nc, LANE), lambda b, i: (b, 0, 0)),
                  pl.BlockSpec((LANE, nc), lambda b, i: (0, 0))],
        out_specs=[pl.BlockSpec((None, Q, GROUP_W), lambda b, i: (b, i, 0)),
                   pl.BlockSpec((None, Q, LANE), lambda b, i: (b, i, 0)),
                   pl.BlockSpec((None, None, 1, LANE), lambda b, i: (b, i, 0, 0))],
        out_shape=[jax.ShapeDtypeStruct((B, S, GROUP_W), F32),
                   jax.ShapeDtypeStruct((B, S, LANE), BF16),
                   jax.ShapeDtypeStruct((B, nqb, 1, LANE), F32)],
        compiler_params=_cparams("parallel", "parallel"),
    )(p_nsa, qtab, p_nsa, kc, vc, overlap_t)


def _nsa_attn_kernel(flags_ref, q_ref, qtab_ref, gate_ref, oc_ref, selb_ref, ks_ref, vs_ref, kw_ref, vw_ref,
                     o_ref, m_ref, acc_ref, ow_ref, *, nt):
    Q, H, TK = q_ref.shape[0], NSA_HEADS, NSA_TILE
    b = pl.program_id(0)
    qb = pl.program_id(1)
    nqb = pl.num_programs(1)
    qh = _nsa_queries(q_ref, qtab_ref, qb)
    nt_dims = (((1,), (1,)), ((), ()))

    selb = selb_ref[...]
    qs_sel = jnp.concatenate([jnp.concatenate([q, selb], axis=1) for q in qh], axis=0)
    sink_off = jnp.where(_iota(selb.shape, 1) < NSA_SINK // NSA_SLC_LEN, NEG_INF, selb.astype(F32)).astype(BF16)
    qs_loop = jnp.concatenate([jnp.concatenate([q, sink_off], axis=1) for q in qh], axis=0)
    m_ref[...] = jnp.full(m_ref.shape, NEG_INF, F32)
    acc_ref[...] = jnp.zeros_like(acc_ref)

    def online_update(m_prev, acc_prev, blocks):
        m_new = m_prev
        for s, _ in blocks:
            m_new = jnp.maximum(m_new, jnp.max(s, axis=-1, keepdims=True))
        acc = jnp.exp2(m_prev - m_new) * acc_prev
        for s, v in blocks:
            p = jnp.exp2(s - jnp.tile(m_new, (1, s.shape[1] // LANE)))
            acc = acc + jnp.dot(p.astype(BF16), v, preferred_element_type=F32)
        return m_new, acc

    def tile(t, carry):
        @pl.when(flags_ref[(b * nqb + qb) * nt + t] > 0)
        def _():
            rows = pl.ds(pl.multiple_of(t * TK, TK), TK)
            s = lax.dot_general(qs_loop, ks_ref[rows, :], nt_dims, preferred_element_type=F32)
            m_ref[...], acc_ref[...] = online_update(m_ref[...], acc_ref[...], [(s, vs_ref[rows, :])])
        return carry

    assert Q == TK == NSA_WIN
    half = Q // 2
    lax.fori_loop(0, qb, tile, 0)

    k_sink, v_sink = ks_ref[0:NSA_SINK, :], vs_ref[0:NSA_SINK, :]
    sink_bias = jnp.where(qb > 0, 0.0, NEG_INF)
    lower = _iota((H * half, half), 1) <= (_iota((H * half, half), 0) & (half - 1))
    for part in range(2):
        row_slices = [slice(h * Q + part * half, h * Q + (part + 1) * half) for h in range(H)]
        q_rows = jnp.concatenate([qs_sel[r] for r in row_slices], axis=0)
        keys = pl.ds(pl.multiple_of(qb * TK, TK), (part + 1) * half)
        s = lax.dot_general(q_rows, ks_ref[keys, :], nt_dims, preferred_element_type=F32)
        own = jnp.where(lower, s[:, part * half:], NEG_INF)
        s = own if part == 0 else jnp.concatenate([s[:, :half], own], axis=1)
        s_sink = lax.dot_general(q_rows, k_sink, nt_dims, preferred_element_type=F32) + sink_bias
        m_new, acc_new = online_update(jnp.concatenate([m_ref[r] for r in row_slices], axis=0),
                                       jnp.concatenate([acc_ref[r] for r in row_slices], axis=0),
                                       [(s, vs_ref[keys, :]), (s_sink, v_sink)])
        for h, r in enumerate(row_slices):
            m_ref[r] = m_new[h * half:(h + 1) * half]
            acc_ref[r] = acc_new[h * half:(h + 1) * half]


    def band(q_rows, first_key, n_keys, masks):
        keys = pl.ds(pl.multiple_of(first_key, half), n_keys)
        s = lax.dot_general(q_rows, kw_ref[keys, :], nt_dims, preferred_element_type=F32)
        groups = [s[:, g * half:(g + 1) * half] for g in range(n_keys // half)]
        s = jnp.concatenate([g if m is None else jnp.where(m, g, NEG_INF) for g, m in zip(groups, masks)], axis=1)
        p = jnp.exp2(s - jnp.max(s, axis=-1, keepdims=True))
        return _normalise(jnp.dot(p.astype(BF16), vw_ref[keys, :], preferred_element_type=F32))

    @pl.when(qb == 0)
    def _():
        qs_win = jnp.concatenate(qh, axis=0)
        row = _iota((H * Q, half), 0) & (Q - 1)
        col = _iota((H * Q, half), 1)
        ow_ref[...] = band(qs_win, 0, Q, [col <= row, col + half <= row])

    @pl.when(qb > 0)
    def _():
        row = _iota((H * half, half), 0) & (half - 1)
        col = _iota((H * half, half), 1)
        masks = [col > row, None, col <= row]
        for part in range(2):
            q_rows = jnp.concatenate([q[part * half:(part + 1) * half] for q in qh], axis=0)
            o_part = band(q_rows, (qb - 1) * Q + part * half, NSA_WIN + half, masks)
            for h in range(H):
                ow_ref[h * Q + part * half:h * Q + (part + 1) * half, :] = o_part[h * half:(h + 1) * half]

    gates = jax.nn.sigmoid(gate_ref[...])
    mixed = _stacked_gate(gates, 1) * _normalise(acc_ref[...]) + _stacked_gate(gates, 2) * ow_ref[...]
    o_ref[...] = (oc_ref[...] + _unstack_heads(mixed)).astype(o_ref.dtype)


def _nsa_attend(p_nsa, qtab, o_c, selb, flags, kv):
    B, S, _ = p_nsa.shape
    Q = NSA_Q
    nqb = S // Q
    nt = S // NSA_TILE
    gate_blk = (W_NSA - LANE) // LANE
    kern = functools.partial(_nsa_attn_kernel, nt=nt)
    slab = lambda width, col: pl.BlockSpec((None, S, width), lambda b, i, f: (b, 0, col))
    grid_spec = pltpu.PrefetchScalarGridSpec(
        num_scalar_prefetch=1,
        grid=(B, nqb),
        in_specs=[pl.BlockSpec((None, Q, NSA_HEADS * LANE), lambda b, i, f: (b, i, 0)),
                  pl.BlockSpec((2 * NSA_HEADS, LANE), lambda b, i, f: (0, 0)),
                  pl.BlockSpec((None, Q, LANE), lambda b, i, f: (b, i, gate_blk)),
                  pl.BlockSpec((None, Q, GROUP_W), lambda b, i, f: (b, i, 0)),
                  pl.BlockSpec((None, Q, LANE), lambda b, i, f: (b, i, 0)),
                  slab(2 * LANE, 0), slab(LANE, 2), slab(LANE, 3), slab(LANE, 4)],
        out_specs=pl.BlockSpec((None, Q, GROUP_W), lambda b, i, f: (b, i, 0)),
        scratch_shapes=[pltpu.VMEM((NSA_HEADS * Q, LANE), F32)] * 3,
    )
    return pl.pallas_call(
        kern, name="nsa_attend",
        grid_spec=grid_spec,
        out_shape=jax.ShapeDtypeStruct((B, S, GROUP_W), BF16),
        compiler_params=_cparams("parallel", "parallel"),
    )(flags, p_nsa, qtab, p_nsa, o_c, selb, kv, kv, kv, kv)


def _nsa_tables(S):
    nc = S // NSA_CMP_STRIDE
    n = np.arange(nc)[None, :]
    j = np.arange(LANE)[:, None]
    start = n * NSA_CMP_STRIDE
    ov = (start < (j + 1) * NSA_SLC_LEN) & (start + NSA_CMP_LEN - 1 >= j * NSA_SLC_LEN)
    ov &= (n < (S - NSA_CMP_LEN) // NSA_CMP_STRIDE + 1) & (j < S // NSA_SLC_LEN)
    pos = np.arange(S)
    k_zero = np.zeros((S, NSA_DK), np.float32)
    block_onehot = (pos[:, None] // NSA_SLC_LEN == np.arange(LANE)[None, :]).astype(np.float32)
    v_lanes = np.zeros((S, LANE), np.float32)
    v_lanes[:, ONES_LANE] = 1.0
    kv_table = np.concatenate([k_zero, _nsa_pos_lanes(pos), block_onehot, v_lanes,
                               k_zero, _nsa_pos_lanes(pos), v_lanes], axis=1)
    assert kv_table.shape[1] == W_KV
    return _nsa_query_table(), jnp.asarray(ov.astype(np.float32)), jnp.asarray(kv_table, dtype=BF16)


def _nsa(p_nsa, kv, uk, uv, pe_k, w1_k, w2_k, pe_v, w1_v, w2_v, tables):
    B, S, _ = p_nsa.shape
    qtab, overlap_t, _ = tables
    kc, vc = _nsa_compress(uk, uv, pe_k, w1_k, w2_k, pe_v, w1_v, w2_v)
    o_c, selb, blk_any = _nsa_select(p_nsa, qtab, kc, vc, overlap_t)
    per_tile = NSA_TILE // NSA_SLC_LEN
    nt = S // NSA_TILE
    not_sink = (np.arange(nt * per_tile) >= NSA_SINK // NSA_SLC_LEN).astype(np.float32)
    blk_any = blk_any[:, :, 0, :nt * per_tile] * not_sink
    flags = blk_any.reshape(B, S // NSA_Q, nt, per_tile).max(axis=-1)
    flags = (flags > 0).astype(jnp.int32).reshape(-1)
    return _nsa_attend(p_nsa, qtab, o_c, selb, flags, kv)


def _out_proj_kernel(h_ref, ya_ref, yb_ref, yc_ref, yd_ref, w_ref, g_ref, b_ref, o_ref, wb_ref):
    @pl.when(pl.program_id(0) == 0)
    def _():
        wb_ref[...] = w_ref[...].astype(BF16)

    mix = None
    for idx, y_ref in enumerate((ya_ref, yb_ref, yc_ref, yd_ref)):
        part = _mm(y_ref[...], wb_ref[idx * GROUP_W:(idx + 1) * GROUP_W, :])
        mix = part if mix is None else mix + part
    o_ref[...] = _layer_norm(DEEPNORM_ALPHA * h_ref[...] + mix, g_ref[...], b_ref[...])


def _out_proj(h2, ys, w_out, layer, g, b, tm=1024):
    T, D = h2.shape
    row = lambda w: pl.BlockSpec((tm, w), lambda i: (i, 0))
    const = lambda shape: pl.BlockSpec(shape, lambda i: (0,) * len(shape))
    return pl.pallas_call(
        _out_proj_kernel, name="out_proj_ln",
        grid=(T // tm,),
        in_specs=[row(D), row(GROUP_W), row(GROUP_W), row(GROUP_W), row(GROUP_W),
                  pl.BlockSpec((None, D, D), lambda i: (layer, 0, 0)), const((1, D)), const((1, D))],
        out_specs=row(D),
        out_shape=jax.ShapeDtypeStruct((T, D), F32),
        scratch_shapes=[pltpu.VMEM((D, D), BF16)],
        compiler_params=_cparams("arbitrary"),
    )(h2, *ys, w_out, g.reshape(1, D), b.reshape(1, D))


def _mlp_kernel(h_ref, w1_ref, w2_ref, g_ref, b_ref, o_ref, acc_ref):
    f = pl.program_id(1)

    @pl.when(f == 0)
    def _():
        acc_ref[...] = jnp.zeros_like(acc_ref)

    a = jnp.maximum(_mm(h_ref[...], w1_ref[...]), 0.0)
    acc_ref[...] += _mm(a * a, w2_ref[...])

    @pl.when(f == pl.num_programs(1) - 1)
    def _():
        o_ref[...] = _layer_norm(DEEPNORM_ALPHA * h_ref[...] + acc_ref[...], g_ref[...], b_ref[...])


def _mlp(h2, w1, w2, layer, g, b, tm=1024, tf=1024):
    T, D = h2.shape
    F = w1.shape[2]
    return pl.pallas_call(
        _mlp_kernel, name="mlp_ln",
        grid=(T // tm, F // tf),
        in_specs=[pl.BlockSpec((tm, D), lambda i, f: (i, 0)),
                  pl.BlockSpec((None, D, tf), lambda i, f: (layer, 0, f)),
                  pl.BlockSpec((None, tf, D), lambda i, f: (layer, f, 0)),
                  pl.BlockSpec((1, D), lambda i, f: (0, 0)),
                  pl.BlockSpec((1, D), lambda i, f: (0, 0))],
        out_specs=pl.BlockSpec((tm, D), lambda i, f: (i, 0)),
        out_shape=jax.ShapeDtypeStruct((T, D), F32),
        scratch_shapes=[pltpu.VMEM((tm, D), F32)],
        compiler_params=_cparams("parallel", "arbitrary"),
    )(h2, w1, w2, g.reshape(1, D), b.reshape(1, D))


def kernel(x, ln_emb_g, ln_emb_b, w_in, conv_w, conv_b, dt_bias, a_log, d_skip, ssm_norm_g, q_norm_g, w_uq, kv_norm_g, w_ukv, cmp_pe_k, cmp_w1_k, cmp_w2_k, cmp_pe_v, cmp_w1_v, cmp_w2_v, w_out, ln1_g, ln1_b, w_mlp1, w_mlp2, ln2_g, ln2_b):
    B, S, D = x.shape
    assert D == D_MODEL and S // NSA_SLC_LEN <= LANE
    assert S % NSA_TILE == 0 and S % NSA_Q == 0 and S % min(MLA_TILE, S) == 0 and S % SSM_CHUNK == 0
    T = B * S
    ret_tables = _ret_tables(S)
    mla_tables = _mla_tables(S)
    nsa_tables = _nsa_tables(S)
    h = x.reshape(T, D)
    for l in range(w_in.shape[0]):
        if l == 0:
            h, *proj = _in_proj(h, _layout_w_in(w_in[l]), nsa_tables[2], entry_ln=(ln_emb_g, ln_emb_b))
        else:
            proj = _in_proj(h, _layout_w_in(w_in[l]), nsa_tables[2])
        p_ssm, p_mla, p_ret, p_nsa, nsa_kv, uk, uv = proj
        cmp_rows = (B, S // NSA_CMP_STRIDE, uk.shape[-1])
        y_a, y_c = _ssm_and_retention(p_ssm.reshape(B, S, W_SSM), conv_w[l], conv_b[l], dt_bias[l], a_log[l],
                                      d_skip[l], ssm_norm_g[l], p_ret.reshape(B, S, W_RET), ret_tables)
        wq, wk, wv = _layout_mla_weights(w_uq[l], w_ukv[l])
        q, k, v = _mla_prep(p_mla.reshape(B, S, W_MLA), q_norm_g[l], kv_norm_g[l], wq, wk, wv, mla_tables)
        y_b = _mla_attn(q, k, v)
        y_d = _nsa(p_nsa.reshape(B, S, W_NSA), nsa_kv.reshape(B, S, W_KV), uk.reshape(cmp_rows), uv.reshape(cmp_rows),
                   cmp_pe_k[l], cmp_w1_k[l], cmp_w2_k[l], cmp_pe_v[l], cmp_w1_v[l], cmp_w2_v[l], nsa_tables)
        ys = [y.reshape(T, GROUP_W) for y in (y_a, y_b, y_c, y_d)]
        h = _out_proj(h, ys, w_out, l, ln1_g[l], ln1_b[l])
        h = _mlp(h, w_mlp1, w_mlp2, l, ln2_g[l], ln2_b[l])
    return h.reshape(B, S, D)
```

```python
import functools
import math

import jax
import jax.numpy as jnp
import numpy as np
from jax import lax
from jax.experimental import pallas as pl
from jax.experimental.pallas import tpu as pltpu

F32 = jnp.float32
BF16 = jnp.bfloat16

D_MODEL = 1024
DEPTH = 2
GROUP_W = D_MODEL // 4
SSM_HEADS = 4
SSM_HEAD_DIM = GROUP_W // SSM_HEADS
SSM_GROUPS = 2
SSM_STATE = 128
SSM_CONV = 4
SSM_CHUNK = 128
SSM_XBC = GROUP_W + 2 * SSM_GROUPS * SSM_STATE
MLA_HEADS = 4
MLA_NOPE = 64
MLA_ROPE = 32
MLA_V = GROUP_W // MLA_HEADS
MLA_Q_RANK = 256
MLA_KV_RANK = 128
RET_HEADS = 4
RET_DK = 64
RET_DV = GROUP_W // RET_HEADS
RET_CHUNK = 128
NSA_HEADS = 4
NSA_DK = 64
NSA_DV = GROUP_W // NSA_HEADS
NSA_CMP_LEN = 32
NSA_CMP_STRIDE = 16
NSA_CMP_HID = 256
NSA_SLC_LEN = 64
NSA_TOPN = 16
NSA_WIN = 512
D_FF = 4 * D_MODEL
NSA_Q = 512
ROPE_THETA = 10000.0
EPS = 1e-5
NEG_INF = -1e30
LOG2_E = math.log2(math.e)
FORCED_SCORE = 1e9
DEEPNORM_ALPHA = (2.0 * DEPTH) ** 0.25

IN_SPLITS = (
    GROUP_W, SSM_XBC, SSM_HEADS,
    MLA_Q_RANK, MLA_KV_RANK, MLA_ROPE,
    RET_HEADS * RET_DK, RET_HEADS * RET_DK, RET_HEADS * RET_DV, GROUP_W,
    NSA_HEADS * NSA_DK, NSA_DK, NSA_DV, NSA_DK, NSA_DV, NSA_DK, NSA_DV, 3 * NSA_HEADS,
)

LANE = 128
SUBLANE = 8
W_SSM = GROUP_W + SSM_XBC + LANE
W_MLA = MLA_Q_RANK + MLA_KV_RANK + LANE
W_RET = 4 * GROUP_W
W_NSA = NSA_HEADS * LANE + LANE + LANE
W_KV = 2 * LANE + 3 * LANE
W_PROJ = ((GROUP_W + SSM_XBC) + (MLA_Q_RANK + MLA_KV_RANK) + W_RET
          + (NSA_HEADS * NSA_DK + LANE) + 4 * NSA_DK + LANE)

RECURRENT_CHUNKS_PER_STEP = 4
NSA_TILE = 512
NSA_SINK = 128
MLA_TILE = 1024
VMEM_LIMIT = 48 * 1024 * 1024


def _cparams(*sem):
    return pltpu.CompilerParams(dimension_semantics=sem, vmem_limit_bytes=VMEM_LIMIT)


def _mm(a, b):
    return jnp.dot(a.astype(BF16), b.astype(BF16), preferred_element_type=F32)


def _mm_nt(a, b):
    return lax.dot_general(a.astype(BF16), b.astype(BF16), (((1,), (1,)), ((), ())),
                           preferred_element_type=F32)


def _split_f32(x):
    hi = x.astype(BF16)
    rest = x - hi.astype(F32)
    mid = rest.astype(BF16)
    lo = (rest - mid.astype(F32)).astype(BF16)
    return hi, mid, lo


def _mm_f32(a, b, exact, dims=(((1,), (0,)), ((), ()))):
    fixed, pieces = (a.astype(BF16), _split_f32(b)) if exact == "a" else (b.astype(BF16), _split_f32(a))
    out = None
    for piece in pieces:
        lhs, rhs = (fixed, piece) if exact == "a" else (piece, fixed)
        part = lax.dot_general(lhs, rhs, dims, preferred_element_type=F32)
        out = part if out is None else out + part
    return out


def _silu(x):
    return x * jax.nn.sigmoid(x)


def _softplus(x):
    return jnp.maximum(x, 0.0) + jnp.log1p(jnp.exp(-jnp.abs(x)))


def _layer_norm(x, g, b):
    mu = jnp.mean(x, axis=-1, keepdims=True)
    xc = x - mu
    var = jnp.mean(xc * xc, axis=-1, keepdims=True)
    return xc * lax.rsqrt(var + EPS) * g + b


def _iota(shape, dim):
    return lax.broadcasted_iota(jnp.int32, shape, dim)


def _pad_cols(w, width):
    return jnp.pad(w, ((0, 0), (0, width - w.shape[1])))


def _layout_w_in(w):
    offs = np.concatenate([[0], np.cumsum(IN_SPLITS)])
    p = [w[:, int(offs[i]):int(offs[i + 1])] for i in range(len(IN_SPLITS))]
    (ssm_z, ssm_xbc, ssm_dt, mla_cq, mla_ckv, mla_kr, ret_q, ret_k, ret_v, ret_g,
     nsa_q, nsa_kc, nsa_vc, nsa_ks, nsa_vs, nsa_kw, nsa_vw, nsa_gate) = p
    small = _pad_cols(jnp.concatenate([mla_kr, ssm_dt, nsa_gate], axis=1), LANE)
    cols = [ssm_z, ssm_xbc, ret_q, ret_k, ret_v, ret_g, nsa_q, nsa_ks, nsa_vs, nsa_kw, nsa_vw,
            mla_cq, mla_ckv, nsa_kc, nsa_vc, small]
    out = jnp.concatenate(cols, axis=1)
    assert out.shape[1] == W_PROJ
    return out.astype(BF16)


def _in_proj_kernel(*refs, entry_ln):
    if entry_ln:
        h_ref, g_ref, b_ref, w_ref, kvtab_ref, hn_ref, *outs = refs
        hn = _layer_norm(h_ref[...], g_ref[...], b_ref[...])
        hn_ref[...] = hn
    else:
        h_ref, w_ref, kvtab_ref, *outs = refs
        hn = h_ref[...]
    ssm_ref, mla_ref, ret_ref, nsa_ref, kv_ref, uk_ref, uv_ref, kcv_ref = outs
    hb = hn.astype(BF16)
    tm = hb.shape[0]
    off = 0

    def project(width):
        nonlocal off
        out = jnp.dot(hb, w_ref[:, off:off + width], preferred_element_type=F32)
        off += width
        return out

    ssm_ref[:, 0:GROUP_W + SSM_XBC] = project(GROUP_W + SSM_XBC)
    ret_ref[...] = project(W_RET)
    q = project(NSA_HEADS * NSA_DK)
    kv = project(4 * NSA_DK)
    latent = MLA_Q_RANK + MLA_KV_RANK
    mla_kc = project(latent + LANE)
    mla_ref[:, 0:latent] = mla_kc[:, 0:latent]
    kc_lane = NSA_HEADS * LANE
    nsa_ref[:, kc_lane:kc_lane + LANE] = mla_kc[:, latent:]
    small = project(LANE)
    lane = _iota((tm, LANE), 1)
    low = lane < NSA_DK
    for h in range(NSA_HEADS):
        pair = q[:, (h // 2) * LANE:(h // 2 + 1) * LANE]
        head = pair if h % 2 == 0 else pltpu.roll(pair, NSA_DK, 1)
        nsa_ref[:, h * LANE:(h + 1) * LANE] = jnp.where(low, head, 0.0)
    mla_ref[:, MLA_Q_RANK + MLA_KV_RANK:] = jnp.where(lane < MLA_ROPE, small, 0.0)
    ssm_ref[:, GROUP_W + SSM_XBC:] = jnp.where(lane < SSM_HEADS, pltpu.roll(small, LANE - MLA_ROPE, 1), 0.0)
    gate_at = MLA_ROPE + SSM_HEADS
    nsa_ref[:, kc_lane + LANE:] = jnp.where(lane < 3 * NSA_HEADS, pltpu.roll(small, LANE - gate_at, 1), 0.0)
    groups = h_ref.shape[0] // NSA_CMP_STRIDE
    kc_lane = NSA_HEADS * LANE
    kcv_ref[...] = nsa_ref[:, kc_lane:kc_lane + LANE]
    for t in range(NSA_CMP_STRIDE):
        piece = kcv_ref[pl.ds(t, groups, stride=NSA_CMP_STRIDE), :]
        uk_ref[:, t * NSA_DK:(t + 1) * NSA_DK] = piece[:, :NSA_DK]
        uv_ref[:, t * NSA_DV:(t + 1) * NSA_DV] = piece[:, NSA_DK:]
    sel_kv, win_kv = kv[:, :LANE], kv[:, LANE:]
    pieces = {0: sel_kv, 2: pltpu.roll(sel_kv, NSA_DK, 1), 3: win_kv, 4: pltpu.roll(win_kv, NSA_DK, 1)}
    for slab in range(W_KV // LANE):
        lanes = slice(slab * LANE, (slab + 1) * LANE)
        tab = kvtab_ref[:, lanes]
        if slab in pieces:
            kv_ref[:, lanes] = (jnp.where(low, pieces[slab], 0.0) + tab.astype(F32)).astype(BF16)
        else:
            kv_ref[:, lanes] = tab


def _in_proj(h2, w_p, kv_table, entry_ln=None, tm=512):
    T, D = h2.shape
    S = kv_table.shape[0]
    widths = (W_SSM, W_MLA, W_RET, W_NSA)
    half = NSA_CMP_STRIDE * NSA_DK
    row = lambda w: pl.BlockSpec((tm, w), lambda i: (i, 0))
    const = lambda shape: pl.BlockSpec(shape, lambda i: (0,) * len(shape))
    in_specs = [const((D, W_PROJ)), pl.BlockSpec((tm, W_KV), lambda i: (i % (S // tm), 0))]
    out_specs = [row(w) for w in widths + (W_KV,)] + [pl.BlockSpec((tm // NSA_CMP_STRIDE, half), lambda i: (i, 0))] * 2
    out_shape = ([jax.ShapeDtypeStruct((T, w), F32) for w in widths] + [jax.ShapeDtypeStruct((T, W_KV), BF16)]
                 + [jax.ShapeDtypeStruct((T // NSA_CMP_STRIDE, half), F32)] * 2)
    operands = (w_p, kv_table)
    if entry_ln is not None:
        in_specs = [const((1, D)), const((1, D))] + in_specs
        out_specs = [row(D)] + out_specs
        out_shape = [jax.ShapeDtypeStruct((T, D), F32)] + out_shape
        operands = tuple(v.reshape(1, D) for v in entry_ln) + operands
    return pl.pallas_call(
        functools.partial(_in_proj_kernel, entry_ln=entry_ln is not None), name="in_proj",
        grid=(T // tm,),
        in_specs=[row(D)] + in_specs,
        out_specs=out_specs,
        out_shape=out_shape,
        scratch_shapes=[pltpu.VMEM((tm, LANE), F32)],
        compiler_params=_cparams("parallel"),
    )(h2, *operands)


def _ssm_chunk(p_ref, cw_ref, cb_ref, dtb_ref, alog_ref, dskip_ref, ng_ref, o_ref, state_ref, ext_ref):
    L, H, P, N = SSM_CHUNK, SSM_HEADS, SSM_HEAD_DIM, SSM_STATE
    z = p_ref[:, 0:GROUP_W]
    ext_ref[SUBLANE:SUBLANE + L, :] = p_ref[:, GROUP_W:GROUP_W + SSM_XBC]
    ext = ext_ref[...]
    conv = cb_ref[...] + ext[SUBLANE:] * cw_ref[SSM_CONV - 1:SSM_CONV, :]
    for j in range(SSM_CONV - 1):
        shift = SSM_CONV - 1 - j
        conv = conv + pltpu.roll(ext, shift, 0)[SUBLANE:] * cw_ref[j:j + 1, :]
    ext_ref[0:SUBLANE, :] = ext_ref[L:L + SUBLANE, :]
    xbc = _silu(conv)
    xs = xbc[:, 0:GROUP_W]
    b_in = xbc[:, GROUP_W:GROUP_W + SSM_GROUPS * N]
    c_in = xbc[:, GROUP_W + SSM_GROUPS * N:]

    dt = _softplus(p_ref[:, GROUP_W + SSM_XBC:] + dtb_ref[...])
    a = dt * (-jnp.exp(alog_ref[...]))
    row = _iota((L, L), 0)
    col = _iota((L, L), 1)
    tril = col <= row
    cs = _mm_f32(jnp.where(tril, 1.0, 0.0), a, exact="a")
    cs_t = cs.T
    ecs = jnp.exp(cs)
    dte = jnp.exp(cs[L - 1:L, :] - cs)
    first_head_of_pair = _iota((L, LANE), 1) < P

    def expand(x):
        pairs = [jnp.where(first_head_of_pair, x[:, h:h + 1], x[:, h + 1:h + 2]) for h in range(0, H, LANE // P)]
        return jnp.concatenate(pairs, axis=1)

    assert LANE == 2 * P
    dt_x = expand(dt)
    ecs_x = expand(ecs)
    dte_x = expand(dte)

    xdt = xs * dt_x
    wx = xdt * dte_x
    head_of_lane = _iota((L, H * P), 1) // P
    y = xs * dskip_ref[...]
    y_off = []
    rep = H // SSM_GROUPS
    for g in range(SSM_GROUPS):
        cg = c_in[:, g * N:(g + 1) * N]
        bg = b_in[:, g * N:(g + 1) * N]
        cb = _mm_nt(cg, bg)
        for h in range(g * rep, (g + 1) * rep):
            diff = cs[:, h:h + 1] - cs_t[h:h + 1, :]
            seg = jnp.where(tril, jnp.exp(jnp.where(tril, diff, 0.0)), 0.0)
            yh = _mm(cb * seg, xdt)
            y = y + jnp.where(head_of_lane == h, yh, 0.0)
        lanes = slice(g * rep * P, (g + 1) * rep * P)
        st_prev = state_ref[:, lanes]
        y_off.append(_mm(cg, st_prev))
        state_ref[:, lanes] = st_prev * ecs_x[L - 1:L, lanes] + _mm(bg.T, wx[:, lanes])
    y = y + jnp.concatenate(y_off, axis=1) * ecs_x
    y = y * _silu(z)
    ms = jnp.mean(y * y, axis=-1, keepdims=True)
    o_ref[...] = (y * lax.rsqrt(ms + EPS) * ng_ref[...]).astype(o_ref.dtype)


def _recurrent_kernel(ps_ref, cw_ref, cb_ref, dtb_ref, alog_ref, dskip_ref, ng_ref,
                      pr_ref, cos_ref, sin_ref, dec_ref, zeta_ref, xi_ref, cd_ref,
                      oa_ref, oc_ref, sstate_ref, ext_ref, rstate_ref):
    @pl.when(pl.program_id(0) == 0)
    def _():
        sstate_ref[...] = jnp.zeros_like(sstate_ref)
        rstate_ref[...] = jnp.zeros_like(rstate_ref)
        ext_ref[:, 0:SUBLANE, :] = jnp.zeros((ext_ref.shape[0], SUBLANE, SSM_XBC), F32)

    L = SSM_CHUNK
    for sub in range(ps_ref.shape[1] // L):
        rows = pl.ds(sub * L, L)
        for b in range(ps_ref.shape[0]):
            _ssm_chunk(ps_ref.at[b, rows], cw_ref, cb_ref, dtb_ref, alog_ref, dskip_ref, ng_ref,
                       oa_ref.at[b, rows], sstate_ref.at[b], ext_ref.at[b])
            _ret_chunk(pr_ref.at[b, rows], cos_ref.at[rows], sin_ref.at[rows], dec_ref, zeta_ref, xi_ref, cd_ref,
                       oc_ref.at[b, rows], rstate_ref.at[b])


def _ret_chunk(p_ref, cos_ref, sin_ref, dec_ref, zeta_ref, xi_ref, cd_ref, o_ref, state_ref):
    L, H, DK, DV = RET_CHUNK, RET_HEADS, RET_DK, RET_DV
    W = H * DK
    q = p_ref[:, 0:W]
    k = p_ref[:, W:2 * W]
    v = p_ref[:, 2 * W:3 * W]
    gate = p_ref[:, 3 * W:4 * W]
    lane = _iota((L, W), 1)
    first_half = (lane % DK) < (DK // 2)
    head_of_lane = lane // DK

    def rope(x):
        partner = jnp.where(first_half, pltpu.roll(x, W - DK // 2, 1), pltpu.roll(x, DK // 2, 1))
        return x * cos_ref[...] + partner * sin_ref[...]

    qr = rope(q)
    kr = rope(k) * (DK ** -0.5)
    y = jnp.zeros((L, H * DV), F32)
    for h in range(H):
        qh = jnp.where(head_of_lane == h, qr, 0.0)
        sc = _mm_nt(qh, kr) * dec_ref[h]
        y = y + jnp.where(head_of_lane == h, _mm(sc, v), 0.0)
    st = state_ref[...]
    y = y + _mm(qr * xi_ref[...], st)
    same_head = (_iota((W, H * DV), 0) // DK) == (_iota((W, H * DV), 1) // DV)
    kv = _mm((kr * zeta_ref[...]).T, v)
    state_ref[...] = st * cd_ref[...] + jnp.where(same_head, kv, 0.0)
    assert DV & (DV - 1) == 0
    ms = _mm_f32(y * y, jnp.where(same_head, 1.0 / DV, 0.0), exact="b")
    o_ref[...] = (y * lax.rsqrt(ms + EPS) * _silu(gate)).astype(o_ref.dtype)


def _ret_tables(S):
    H, DK, L = RET_HEADS, RET_DK, RET_CHUNK
    inv = ROPE_THETA ** (-np.arange(0, DK, 2, dtype=np.float64) / DK)
    ang = np.arange(S, dtype=np.float64)[:, None] * inv[None, :]
    cos, sin = np.cos(ang), np.sin(ang)
    cos_t = np.tile(np.concatenate([cos, cos], axis=1), (1, H))
    sin_t = np.tile(np.concatenate([-sin, sin], axis=1), (1, H))
    log_gamma = np.log1p(-np.exp2(-5.0 - np.arange(H, dtype=np.float64)))
    pos = np.arange(L, dtype=np.float64)
    diff = pos[:, None] - pos[None, :]
    decay_in = np.where(diff >= 0, np.exp(np.maximum(diff, 0.0)[None] * log_gamma[:, None, None]), 0.0)
    zeta = np.exp((L - 1 - pos)[None] * log_gamma[:, None])
    xi = np.exp((pos + 1.0)[None] * log_gamma[:, None])
    chunk_decay = np.exp(L * log_gamma)
    zeta_x = np.repeat(zeta.T, DK, axis=1)
    xi_x = np.repeat(xi.T, DK, axis=1)
    cd_x = np.repeat(chunk_decay, RET_DV).reshape(1, H * RET_DV)
    return tuple(jnp.asarray(t, dtype=F32) for t in (cos_t, sin_t, decay_in, zeta_x, xi_x, cd_x))


def _ssm_and_retention(p_ssm, conv_w, conv_b, dt_bias, a_log, d_skip, norm_g, p_ret, tables):
    B, S, _ = p_ssm.shape
    L, H = SSM_CHUNK, RET_HEADS
    assert RET_CHUNK == L
    W = H * RET_DK
    cos_t, sin_t, decay_in, zeta_x, xi_x, cd_x = tables
    pad_h = lambda v: jnp.pad(v, (0, LANE - SSM_HEADS)).reshape(1, LANE)
    const = lambda shape: pl.BlockSpec(shape, lambda c: (0,) * len(shape))
    rows = RECURRENT_CHUNKS_PER_STEP * L
    assert S % rows == 0
    chunk = lambda width: pl.BlockSpec((B, rows, width), lambda c: (0, c, 0))
    out_shape = jax.ShapeDtypeStruct((B, S, GROUP_W), BF16)
    return pl.pallas_call(
        _recurrent_kernel, name="ssm_retention",
        grid=(S // rows,),
        in_specs=[chunk(W_SSM),
                  const((SSM_CONV, SSM_XBC)), const((1, SSM_XBC)), const((1, LANE)), const((1, LANE)),
                  const((1, GROUP_W)), const((1, GROUP_W)),
                  chunk(W_RET),
                  pl.BlockSpec((rows, W), lambda c: (c, 0)), pl.BlockSpec((rows, W), lambda c: (c, 0)),
                  const((H, L, L)), const((L, W)), const((L, W)), const((1, H * RET_DV))],
        out_specs=[chunk(GROUP_W), chunk(GROUP_W)],
        out_shape=[out_shape, out_shape],
        scratch_shapes=[pltpu.VMEM((B, SSM_STATE, GROUP_W), F32),
                        pltpu.VMEM((B, L + SUBLANE, SSM_XBC), F32),
                        pltpu.VMEM((B, W, H * RET_DV), F32)],
        compiler_params=_cparams("arbitrary"),
    )(p_ssm, conv_w, conv_b.reshape(1, -1), pad_h(dt_bias), pad_h(a_log),
      jnp.repeat(d_skip, SSM_HEAD_DIM).reshape(1, GROUP_W), norm_g.reshape(1, GROUP_W),
      p_ret, cos_t, sin_t, decay_in, zeta_x, xi_x, cd_x)


def _mla_prep_kernel(p_ref, qg_ref, wq_ref, kvg_ref, wk_ref, wv_ref, cos_ref, sin_ref,
                     q_ref, k_ref, v_ref):
    tm = p_ref.shape[0]
    cq = p_ref[:, 0:MLA_Q_RANK]
    ckv = p_ref[:, MLA_Q_RANK:MLA_Q_RANK + MLA_KV_RANK]
    kr = p_ref[:, MLA_Q_RANK + MLA_KV_RANK:]

    def rms(x, g):
        return x * lax.rsqrt(jnp.mean(x * x, axis=-1, keepdims=True) + EPS) * g

    q_both = _mm(rms(cq, qg_ref[...]), wq_ref[...])
    q = q_both[:, :MLA_HEADS * LANE]
    q_partner = q_both[:, MLA_HEADS * LANE:]
    kvl = rms(ckv, kvg_ref[...])
    kn = _mm(kvl, wk_ref[...])
    vv = _mm(kvl, wv_ref[...])
    kr_sh = pltpu.roll(kr, MLA_NOPE, 1)
    lane = _iota((tm, LANE), 1)
    half = MLA_ROPE // 2
    low = (lane >= MLA_NOPE) & (lane < MLA_NOPE + half)
    cos = cos_ref[...]
    sin = sin_ref[...]

    def rope(x):
        partner = jnp.where(low, pltpu.roll(x, LANE - half, 1), pltpu.roll(x, half, 1))
        return x * cos + partner * sin

    scale = (MLA_NOPE + MLA_ROPE) ** -0.5 * LOG2_E
    k_pe = rope(kr_sh)
    for h in range(MLA_HEADS):
        sl = slice(h * LANE, (h + 1) * LANE)
        q_ref[h] = ((q[:, sl] * cos + q_partner[:, sl] * sin) * scale).astype(BF16)
        k_ref[h] = (kn[:, sl] + k_pe).astype(BF16)
        v_ref[h] = jnp.where(lane == _mla_ones_lane(h), 1.0, vv[:, sl]).astype(BF16)


def _mla_tables(S):
    inv = ROPE_THETA ** (-np.arange(0, MLA_ROPE, 2, dtype=np.float64) / MLA_ROPE)
    ang = np.arange(S, dtype=np.float64)[:, None] * inv[None, :]
    cos, sin = np.cos(ang), np.sin(ang)
    tail = LANE - MLA_NOPE - MLA_ROPE
    cos_t = np.concatenate([np.ones((S, MLA_NOPE)), cos, cos, np.ones((S, tail))], axis=1)
    sin_t = np.concatenate([np.zeros((S, MLA_NOPE)), -sin, sin, np.zeros((S, tail))], axis=1)
    return jnp.asarray(cos_t, dtype=F32), jnp.asarray(sin_t, dtype=F32)


def _layout_mla_weights(w_uq, w_ukv):
    H = MLA_HEADS
    dq = MLA_NOPE + MLA_ROPE
    wq = jnp.concatenate([_pad_cols(w_uq[:, h * dq:(h + 1) * dq], LANE) for h in range(H)], axis=1)
    half = MLA_ROPE // 2
    partner = []
    for h in range(H):
        pe = w_uq[:, h * dq + MLA_NOPE:(h + 1) * dq]
        swapped = jnp.concatenate([jnp.zeros_like(w_uq[:, :MLA_NOPE]), pe[:, half:], pe[:, :half]], axis=1)
        partner.append(_pad_cols(swapped, LANE))
    wq = jnp.concatenate([wq] + partner, axis=1)
    dkv = MLA_NOPE + MLA_V
    wk, wv = [], []
    for h in range(H):
        blk = w_ukv[:, h * dkv:(h + 1) * dkv]
        wk.append(_pad_cols(blk[:, :MLA_NOPE], LANE))
        v = blk[:, MLA_NOPE:]
        zero = jnp.zeros_like(v)
        wv.append(jnp.concatenate([v, zero] if h % 2 == 0 else [zero, v], axis=1))
    return wq.astype(BF16), jnp.concatenate(wk, axis=1).astype(BF16), jnp.concatenate(wv, axis=1).astype(BF16)


def _mla_prep(p_mla, q_norm_g, kv_norm_g, wq, wk, wv, tables, tm=1024):
    B, S, _ = p_mla.shape
    H = MLA_HEADS
    cos_t, sin_t = tables
    const = lambda shape: pl.BlockSpec(shape, lambda b, i: (0,) * len(shape))
    qkv_spec = pl.BlockSpec((None, H, tm, LANE), lambda b, i: (b, 0, i, 0))
    qkv_shape = jax.ShapeDtypeStruct((B, H, S, LANE), BF16)
    return pl.pallas_call(
        _mla_prep_kernel, name="mla_prep",
        grid=(B, S // tm),
        in_specs=[pl.BlockSpec((None, tm, W_MLA), lambda b, i: (b, i, 0)),
                  const((1, MLA_Q_RANK)), const((MLA_Q_RANK, 2 * H * LANE)),
                  const((1, MLA_KV_RANK)), const((MLA_KV_RANK, H * LANE)), const((MLA_KV_RANK, H * LANE)),
                  pl.BlockSpec((tm, LANE), lambda b, i: (i, 0)),
                  pl.BlockSpec((tm, LANE), lambda b, i: (i, 0))],
        out_specs=[qkv_spec, qkv_spec, qkv_spec],
        out_shape=[qkv_shape, qkv_shape, qkv_shape],
        compiler_params=_cparams("parallel", "parallel"),
    )(p_mla, q_norm_g.reshape(1, -1), wq, kv_norm_g.reshape(1, -1), wk, wv, cos_t, sin_t)


def _mla_ones_lane(h):
    return MLA_V if h % 2 == 0 else 0


def _mla_attn_kernel(qi_ref, kj_ref, q_ref, k_ref, v_ref, o_ref, m_ref, acc_ref):
    H = MLA_HEADS
    tq, tk = q_ref.shape[1], k_ref.shape[1]
    i = qi_ref[pl.program_id(1)]
    j = kj_ref[pl.program_id(1)]

    @pl.when(j == 0)
    def _():
        m_ref[...] = jnp.full(m_ref.shape, NEG_INF, F32)
        acc_ref[...] = jnp.zeros_like(acc_ref)

    def sweep(blocks):
        nt_dims = (((1,), (1,)), ((), ()))
        scores = [[lax.dot_general(q_ref[h, r0:r0 + nr, :], k_ref[h, 0:nk, :], nt_dims, preferred_element_type=F32)
                   for (r0, nr, nk, _) in blocks] for h in range(H)]
        for h in range(H):
            for (r0, nr, nk, offset), s in zip(blocks, scores[h]):
                if offset is not None:
                    s = jnp.where(_iota((nr, nk), 1) - _iota((nr, nk), 0) <= offset, s, NEG_INF)
                rows = slice(r0, r0 + nr)
                m_prev = m_ref[h, rows]
                m_new = jnp.maximum(m_prev, jnp.max(s, axis=-1, keepdims=True))
                p = jnp.exp2(s - jnp.tile(m_new, (1, nk // LANE)))
                acc_ref[h, rows] = (jnp.exp2(m_prev - m_new) * acc_ref[h, rows]
                                    + jnp.dot(p.astype(BF16), v_ref[h, 0:nk, :], preferred_element_type=F32))
                m_ref[h, rows] = m_new

    assert tq == tk
    half = tq // 2

    @pl.when(j < i)
    def _():
        sweep([(0, tq, tk, None)])

    @pl.when(j == i)
    def _():
        sweep([(0, half, half, 0), (half, half, tk, half)])
        lane = _iota((tq, LANE), 1)
        for pair in range(H // 2):
            he, ho = 2 * pair, 2 * pair + 1
            acc_e, acc_o = acc_ref[he], acc_ref[ho]
            le = acc_e[:, _mla_ones_lane(he):_mla_ones_lane(he) + 1]
            lo = acc_o[:, _mla_ones_lane(ho):_mla_ones_lane(ho) + 1]
            o_ref[:, pair * LANE:(pair + 1) * LANE] = jnp.where(lane < MLA_V, acc_e / le, acc_o / lo).astype(o_ref.dtype)


def _mla_attn(q, k, v):
    B, H, S, _ = q.shape
    t = min(MLA_TILE, S)
    tq = t
    pairs = [(i, j) for i in range(S // tq) for j in range((i + 1) * tq // t)]
    qi = jnp.asarray([p[0] for p in pairs], jnp.int32)
    kj = jnp.asarray([p[1] for p in pairs], jnp.int32)
    grid_spec = pltpu.PrefetchScalarGridSpec(
        num_scalar_prefetch=2,
        grid=(B, len(pairs)),
        in_specs=[pl.BlockSpec((None, H, tq, LANE), lambda b, p, qi, kj: (b, 0, qi[p], 0)),
                  pl.BlockSpec((None, H, t, LANE), lambda b, p, qi, kj: (b, 0, kj[p], 0)),
                  pl.BlockSpec((None, H, t, LANE), lambda b, p, qi, kj: (b, 0, kj[p], 0))],
        out_specs=pl.BlockSpec((None, tq, GROUP_W), lambda b, p, qi, kj: (b, qi[p], 0)),
        scratch_shapes=[pltpu.VMEM((H, tq, LANE), F32), pltpu.VMEM((H, tq, LANE), F32)],
    )
    return pl.pallas_call(
        _mla_attn_kernel, name="mla_attn",
        grid_spec=grid_spec,
        out_shape=jax.ShapeDtypeStruct((B, S, GROUP_W), BF16),
        compiler_params=_cparams("parallel", "arbitrary"),
    )(qi, kj, q, k, v)


POS_HI = NSA_DK
POS_LO = NSA_DK + 3
POS_ONE = NSA_DK + 6
ONES_LANE = NSA_DV


def _split_bf16(x, parts=3):
    out, rem = [], np.float64(x)
    for _ in range(parts):
        piece = np.float64(np.float32(rem).astype(jnp.bfloat16).astype(np.float32))
        out.append(float(piece))
        rem = rem - piece
    return out


def _nsa_query_table():
    H = NSA_HEADS
    tab = np.zeros((2 * H, LANE), np.float32)
    for h in range(H):
        c = 2.0 ** (-8.0 * (h + 1) / H) * LOG2_E
        pieces = _split_bf16(c)
        tab[h, POS_HI:POS_HI + 3] = pieces
        tab[h, POS_LO:POS_LO + 3] = pieces
        tab[H + h, POS_ONE] = -sum(pieces)
    return jnp.asarray(tab)


def _nsa_pos_lanes(pos, lo_offset=0.0):
    t = np.zeros((len(pos), LANE - NSA_DK), np.float32)
    t[:, POS_HI - NSA_DK:POS_HI - NSA_DK + 3] = (NSA_SLC_LEN * (pos // NSA_SLC_LEN))[:, None]
    t[:, POS_LO - NSA_DK:POS_LO - NSA_DK + 3] = (pos % NSA_SLC_LEN + lo_offset)[:, None]
    t[:, POS_ONE - NSA_DK] = 1.0
    return t


def _nsa_queries(q_ref, qtab_ref, qb):
    Q, H = q_ref.shape[0], NSA_HEADS
    qpos = (qb * Q + _iota((Q, 1), 0)).astype(F32)
    out = []
    for h in range(H):
        q = q_ref[:, h * LANE:(h + 1) * LANE] * (NSA_DK ** -0.5 * LOG2_E)
        out.append((q + qtab_ref[h:h + 1, :] + qtab_ref[H + h:H + h + 1, :] * qpos).astype(BF16))
    return out


def _normalise(o):
    return o / o[:, ONES_LANE:ONES_LANE + 1]


def _stacked_gate(gates, branch):
    lanes = [3 * h + branch for h in range(NSA_HEADS)]
    return jnp.concatenate([gates[:, c:c + 1] for c in lanes], axis=0)


def _unstack_heads(o):
    Q = o.shape[0] // NSA_HEADS
    lane = _iota((Q, LANE), 1)
    out = []
    for pair in range(NSA_HEADS // 2):
        even = o[(2 * pair) * Q:(2 * pair + 1) * Q]
        odd = o[(2 * pair + 1) * Q:(2 * pair + 2) * Q]
        out.append(jnp.where(lane < NSA_DV, even, pltpu.roll(odd, NSA_DV, 1)))
    return jnp.concatenate(out, axis=1)


def _nsa_cmp_kernel(uk_ref, uv_ref, pek_ref, pev_ref, w1k_ref, w1v_ref, w2k_ref, w2v_ref, cpos_ref,
                    kc_ref, vc_ref, sh_ref):
    nb = uk_ref.shape[0]
    half = uk_ref.shape[1]

    def hidden(u_ref, pe_ref, w1_ref):
        u = u_ref[...]
        first = _mm(u + pe_ref[0:1, :], w1_ref[0:half, :])
        second = _mm(u + pe_ref[1:2, :], w1_ref[half:2 * half, :])
        sh_ref[0:nb, :] = second
        sh_ref[nb:nb + SUBLANE, :] = jnp.zeros((SUBLANE, NSA_CMP_HID), F32)
        return first + sh_ref[pl.ds(1, nb), :]

    hk = _silu(hidden(uk_ref, pek_ref, w1k_ref))
    hv = _silu(hidden(uv_ref, pev_ref, w1v_ref))
    kc_ref[...] = (_mm(hk, w2k_ref[...]) + cpos_ref[...]).astype(BF16)
    ones_lane = jnp.where(_iota((1, LANE), 1) == ONES_LANE, 1.0, 0.0)
    vc_ref[...] = (_mm(hv, w2v_ref[...]) + ones_lane).astype(BF16)


def _nsa_compress(uk, uv, pe_k, w1_k, w2_k, pe_v, w1_v, w2_v):
    B, nb, half = uk.shape
    hid = NSA_CMP_HID
    const = lambda shape: pl.BlockSpec(shape, lambda b: (0,) * len(shape))
    w2k = _pad_cols(w2_k, LANE).astype(BF16)
    w2v = _pad_cols(w2_v, LANE).astype(BF16)
    centre = _nsa_pos_lanes(np.arange(nb) * NSA_CMP_STRIDE, 0.5 * (NSA_CMP_LEN - 1))
    cpos = jnp.asarray(np.concatenate([np.zeros((nb, NSA_DK), np.float32), centre], axis=1))
    out_spec = pl.BlockSpec((None, nb, LANE), lambda b: (b, 0, 0))
    out_shape = jax.ShapeDtypeStruct((B, nb, LANE), BF16)
    return pl.pallas_call(
        _nsa_cmp_kernel, name="nsa_compress",
        grid=(B,),
        in_specs=[pl.BlockSpec((None, nb, half), lambda b: (b, 0, 0)),
                  pl.BlockSpec((None, nb, half), lambda b: (b, 0, 0)),
                  const((2, half)), const((2, half)),
                  const((2 * half, hid)), const((2 * half, hid)),
                  const((hid, LANE)), const((hid, LANE)), const((nb, LANE))],
        out_specs=[out_spec, out_spec],
        out_shape=[out_shape, out_shape],
        scratch_shapes=[pltpu.VMEM((nb + SUBLANE, hid), F32)],
        compiler_params=_cparams("parallel"),
    )(uk, uv, pe_k.reshape(2, half), pe_v.reshape(2, half), w1_k.astype(BF16), w1_v.astype(BF16), w2k, w2v, cpos)


def _nsa_sel_kernel(q_ref, qtab_ref, gate_ref, kc_ref, vc_ref, ovt_ref, oc_ref, selb_ref, any_ref, *, n_slc, top_n):
    Q, H = q_ref.shape[0], NSA_HEADS
    qb = pl.program_id(1)
    nc = kc_ref.shape[0]
    pick = functools.partial(_nsa_pick_blocks, qb=qb, selb_ref=selb_ref, any_ref=any_ref, n_slc=n_slc, top_n=top_n)
    qs = jnp.concatenate(_nsa_queries(q_ref, qtab_ref, qb), axis=0)
    nt_dims = (((1,), (1,)), ((), ()))

    def attend(ncols):
        s = lax.dot_general(qs, kc_ref[0:ncols, :], nt_dims, preferred_element_type=F32)
        qpos = qb * Q + (_iota((H * Q, ncols), 0) & (Q - 1))
        block_end = _iota((H * Q, ncols), 1) * NSA_CMP_STRIDE + (NSA_CMP_LEN - 1)
        s = jnp.where(block_end <= qpos, s, NEG_INF)
        e = jnp.exp2(s - jnp.max(s, axis=-1, keepdims=True))
        qpos_col = qb * Q + (_iota((H * Q, 1), 0) & (Q - 1))
        has_block = jnp.where(qpos_col >= NSA_CMP_LEN - 1, 1.0, 0.0)
        p = e * (has_block / jnp.sum(e, axis=-1, keepdims=True))
        o_c = jnp.dot(p.astype(BF16), vc_ref[0:ncols, :], preferred_element_type=F32)
        oc_ref[...] = _unstack_heads(_stacked_gate(jax.nn.sigmoid(gate_ref[...]), 0) * o_c)
        p_sum = p[0:Q]
        for h in range(1, H):
            p_sum = p_sum + p[h * Q:(h + 1) * Q]
        pick(_mm_f32(ovt_ref[:, 0:ncols], p_sum, exact="a", dims=nt_dims))

    tiles_needed = ((qb + 1) * Q // NSA_CMP_STRIDE + LANE - 1) // LANE
    for tiles in range(1, nc // LANE + 1):
        pl.when(tiles_needed == tiles)(functools.partial(attend, tiles * LANE))


def _nsa_pick_blocks(imp, qb, selb_ref, any_ref, n_slc, top_n):
    Q = imp.shape[1]
    blk = _iota((LANE, Q), 0)
    q_blk = (qb * Q + _iota((LANE, Q), 1)) >> int(math.log2(NSA_SLC_LEN))
    causal = blk <= q_blk
    for forced_blk in (0, q_blk, q_blk - 1):
        imp = jnp.where(blk == forced_blk, FORCED_SCORE, imp)
    imp = jnp.where(causal, imp, -1.0)
    imp = jnp.where(blk < n_slc, imp, -2.0)
    blk_f = blk.astype(F32)
    sel = jnp.zeros((LANE, Q), F32)
    for _ in range(top_n):
        m = jnp.max(imp, axis=0, keepdims=True)
        first = jnp.min(jnp.where(imp == m, blk_f, float(LANE)), axis=0, keepdims=True)
        hit = blk_f == first
        sel = jnp.where(hit, 1.0, sel)
        imp = jnp.where(hit, -3.0, imp)
    sel = jnp.where(causal, sel, 0.0).T
    selb_ref[...] = jnp.where(sel > 0.5, 0.0, NEG_INF).astype(BF16)
    any_ref[...] = jnp.max(sel, axis=0, keepdims=True)


def _nsa_select(p_nsa, qtab, kc, vc, overlap_t):
    B, S, _ = p_nsa.shape
    Q = NSA_Q
    nqb = S // Q
    nc = kc.shape[1]
    n_slc = S // NSA_SLC_LEN
    kern = functools.partial(_nsa_sel_kernel, n_slc=n_slc, top_n=min(NSA_TOPN, n_slc))
    return pl.pallas_call(
        kern, name="nsa_select",
        grid=(B, nqb),
        in_specs=[pl.BlockSpec((None, Q, NSA_HEADS * LANE), lambda b, i: (b, i, 0)),
                  pl.BlockSpec((2 * NSA_HEADS, LANE), lambda b, i: (0, 0)),
                  pl.BlockSpec((None, Q, LANE), lambda b, i: (b, i, (W_NSA - LANE) // LANE)),
                  pl.BlockSpec((None, nc, LANE), lambda b, i: (b, 0, 0)),
                  pl.BlockSpec((None, nc, LANE), lambda b, i: (b, 0, 0)),
                  pl.BlockSpec((LANE, nc), lambda b, i: (0, 0))],
        out_specs=[pl.BlockSpec((None, Q, GROUP_W), lambda b, i: (b, i, 0)),
                   pl.BlockSpec((None, Q, LANE), lambda b, i: (b, i, 0)),
                   pl.BlockSpec((None, None, 1, LANE), lambda b, i: (b, i, 0, 0))],
        out_shape=[jax.ShapeDtypeStruct((B, S, GROUP_W), F32),
                   jax.ShapeDtypeStruct((B, S, LANE), BF16),
                   jax.ShapeDtypeStruct((B, nqb, 1, LANE), F32)],
        compiler_params=_cparams("parallel", "parallel"),
    )(p_nsa, qtab, p_nsa, kc, vc, overlap_t)


def _nsa_attn_kernel(flags_ref, q_ref, qtab_ref, gate_ref, oc_ref, selb_ref, ks_ref, vs_ref, kw_ref, vw_ref,
                     o_ref, m_ref, acc_ref, ow_ref, *, nt):
    Q, H, TK = q_ref.shape[0], NSA_HEADS, NSA_TILE
    b = pl.program_id(0)
    qb = pl.program_id(1)
    nqb = pl.num_programs(1)
    qh = _nsa_queries(q_ref, qtab_ref, qb)
    nt_dims = (((1,), (1,)), ((), ()))

    selb = selb_ref[...]
    qs_sel = jnp.concatenate([jnp.concatenate([q, selb], axis=1) for q in qh], axis=0)
    sink_off = jnp.where(_iota(selb.shape, 1) < NSA_SINK // NSA_SLC_LEN, NEG_INF, selb.astype(F32)).astype(BF16)
    qs_loop = jnp.concatenate([jnp.concatenate([q, sink_off], axis=1) for q in qh], axis=0)
    m_ref[...] = jnp.full(m_ref.shape, NEG_INF, F32)
    acc_ref[...] = jnp.zeros_like(acc_ref)

    def online_update(m_prev, acc_prev, blocks):
        m_new = m_prev
        for s, _ in blocks:
            m_new = jnp.maximum(m_new, jnp.max(s, axis=-1, keepdims=True))
        acc = jnp.exp2(m_prev - m_new) * acc_prev
        for s, v in blocks:
            p = jnp.exp2(s - jnp.tile(m_new, (1, s.shape[1] // LANE)))
            acc = acc + jnp.dot(p.astype(BF16), v, preferred_element_type=F32)
        return m_new, acc

    def tile(t, carry):
        @pl.when(flags_ref[(b * nqb + qb) * nt + t] > 0)
        def _():
            rows = pl.ds(pl.multiple_of(t * TK, TK), TK)
            s = lax.dot_general(qs_loop, ks_ref[rows, :], nt_dims, preferred_element_type=F32)
            m_ref[...], acc_ref[...] = online_update(m_ref[...], acc_ref[...], [(s, vs_ref[rows, :])])
        return carry

    assert Q == TK == NSA_WIN
    half = Q // 2
    lax.fori_loop(0, qb, tile, 0)

    k_sink, v_sink = ks_ref[0:NSA_SINK, :], vs_ref[0:NSA_SINK, :]
    sink_bias = jnp.where(qb > 0, 0.0, NEG_INF)
    lower = _iota((H * half, half), 1) <= (_iota((H * half, half), 0) & (half - 1))
    for part in range(2):
        row_slices = [slice(h * Q + part * half, h * Q + (part + 1) * half) for h in range(H)]
        q_rows = jnp.concatenate([qs_sel[r] for r in row_slices], axis=0)
        keys = pl.ds(pl.multiple_of(qb * TK, TK), (part + 1) * half)
        s = lax.dot_general(q_rows, ks_ref[keys, :], nt_dims, preferred_element_type=F32)
        own = jnp.where(lower, s[:, part * half:], NEG_INF)
        s = own if part == 0 else jnp.concatenate([s[:, :half], own], axis=1)
        s_sink = lax.dot_general(q_rows, k_sink, nt_dims, preferred_element_type=F32) + sink_bias
        m_new, acc_new = online_update(jnp.concatenate([m_ref[r] for r in row_slices], axis=0),
                                       jnp.concatenate([acc_ref[r] for r in row_slices], axis=0),
                                       [(s, vs_ref[keys, :]), (s_sink, v_sink)])
        for h, r in enumerate(row_slices):
            m_ref[r] = m_new[h * half:(h + 1) * half]
            acc_ref[r] = acc_new[h * half:(h + 1) * half]


    def band(q_rows, first_key, n_keys, masks):
        keys = pl.ds(pl.multiple_of(first_key, half), n_keys)
        s = lax.dot_general(q_rows, kw_ref[keys, :], nt_dims, preferred_element_type=F32)
        groups = [s[:, g * half:(g + 1) * half] for g in range(n_keys // half)]
        s = jnp.concatenate([g if m is None else jnp.where(m, g, NEG_INF) for g, m in zip(groups, masks)], axis=1)
        p = jnp.exp2(s - jnp.max(s, axis=-1, keepdims=True))
        return _normalise(jnp.dot(p.astype(BF16), vw_ref[keys, :], preferred_element_type=F32))

    @pl.when(qb == 0)
    def _():
        qs_win = jnp.concatenate(qh, axis=0)
        row = _iota((H * Q, half), 0) & (Q - 1)
        col = _iota((H * Q, half), 1)
        ow_ref[...] = band(qs_win, 0, Q, [col <= row, col + half <= row])

    @pl.when(qb > 0)
    def _():
        row = _iota((H * half, half), 0) & (half - 1)
        col = _iota((H * half, half), 1)
        masks = [col > row, None, col <= row]
        for part in range(2):
            q_rows = jnp.concatenate([q[part * half:(part + 1) * half] for q in qh], axis=0)
            o_part = band(q_rows, (qb - 1) * Q + part * half, NSA_WIN + half, masks)
            for h in range(H):
                ow_ref[h * Q + part * half:h * Q + (part + 1) * half, :] = o_part[h * half:(h + 1) * half]

    gates = jax.nn.sigmoid(gate_ref[...])
    mixed = _stacked_gate(gates, 1) * _normalise(acc_ref[...]) + _stacked_gate(gates, 2) * ow_ref[...]
    o_ref[...] = (oc_ref[...] + _unstack_heads(mixed)).astype(o_ref.dtype)


def _nsa_attend(p_nsa, qtab, o_c, selb, flags, kv):
    B, S, _ = p_nsa.shape
    Q = NSA_Q
    nqb = S // Q
    nt = S // NSA_TILE
    gate_blk = (W_NSA - LANE) // LANE
    kern = functools.partial(_nsa_attn_kernel, nt=nt)
    slab = lambda width, col: pl.BlockSpec((None, S, width), lambda b, i, f: (b, 0, col))
    grid_spec = pltpu.PrefetchScalarGridSpec(
        num_scalar_prefetch=1,
        grid=(B, nqb),
        in_specs=[pl.BlockSpec((None, Q, NSA_HEADS * LANE), lambda b, i, f: (b, i, 0)),
                  pl.BlockSpec((2 * NSA_HEADS, LANE), lambda b, i, f: (0, 0)),
                  pl.BlockSpec((None, Q, LANE), lambda b, i, f: (b, i, gate_blk)),
                  pl.BlockSpec((None, Q, GROUP_W), lambda b, i, f: (b, i, 0)),
                  pl.BlockSpec((None, Q, LANE), lambda b, i, f: (b, i, 0)),
                  slab(2 * LANE, 0), slab(LANE, 2), slab(LANE, 3), slab(LANE, 4)],
        out_specs=pl.BlockSpec((None, Q, GROUP_W), lambda b, i, f: (b, i, 0)),
        scratch_shapes=[pltpu.VMEM((NSA_HEADS * Q, LANE), F32)] * 3,
    )
    return pl.pallas_call(
        kern, name="nsa_attend",
        grid_spec=grid_spec,
        out_shape=jax.ShapeDtypeStruct((B, S, GROUP_W), BF16),
        compiler_params=_cparams("parallel", "parallel"),
    )(flags, p_nsa, qtab, p_nsa, o_c, selb, kv, kv, kv, kv)


def _nsa_tables(S):
    nc = S // NSA_CMP_STRIDE
    n = np.arange(nc)[None, :]
    j = np.arange(LANE)[:, None]
    start = n * NSA_CMP_STRIDE
    ov = (start < (j + 1) * NSA_SLC_LEN) & (start + NSA_CMP_LEN - 1 >= j * NSA_SLC_LEN)
    ov &= (n < (S - NSA_CMP_LEN) // NSA_CMP_STRIDE + 1) & (j < S // NSA_SLC_LEN)
    pos = np.arange(S)
    k_zero = np.zeros((S, NSA_DK), np.float32)
    block_onehot = (pos[:, None] // NSA_SLC_LEN == np.arange(LANE)[None, :]).astype(np.float32)
    v_lanes = np.zeros((S, LANE), np.float32)
    v_lanes[:, ONES_LANE] = 1.0
    kv_table = np.concatenate([k_zero, _nsa_pos_lanes(pos), block_onehot, v_lanes,
                               k_zero, _nsa_pos_lanes(pos), v_lanes], axis=1)
    assert kv_table.shape[1] == W_KV
    return _nsa_query_table(), jnp.asarray(ov.astype(np.float32)), jnp.asarray(kv_table, dtype=BF16)


def _nsa(p_nsa, kv, uk, uv, pe_k, w1_k, w2_k, pe_v, w1_v, w2_v, tables):
    B, S, _ = p_nsa.shape
    qtab, overlap_t, _ = tables
    kc, vc = _nsa_compress(uk, uv, pe_k, w1_k, w2_k, pe_v, w1_v, w2_v)
    o_c, selb, blk_any = _nsa_select(p_nsa, qtab, kc, vc, overlap_t)
    per_tile = NSA_TILE // NSA_SLC_LEN
    nt = S // NSA_TILE
    not_sink = (np.arange(nt * per_tile) >= NSA_SINK // NSA_SLC_LEN).astype(np.float32)
    blk_any = blk_any[:, :, 0, :nt * per_tile] * not_sink
    flags = blk_any.reshape(B, S // NSA_Q, nt, per_tile).max(axis=-1)
    flags = (flags > 0).astype(jnp.int32).reshape(-1)
    return _nsa_attend(p_nsa, qtab, o_c, selb, flags, kv)


def _out_proj_kernel(h_ref, ya_ref, yb_ref, yc_ref, yd_ref, w_ref, g_ref, b_ref, o_ref, wb_ref):
    @pl.when(pl.program_id(0) == 0)
    def _():
        wb_ref[...] = w_ref[...].astype(BF16)

    mix = None
    for idx, y_ref in enumerate((ya_ref, yb_ref, yc_ref, yd_ref)):
        part = _mm(y_ref[...], wb_ref[idx * GROUP_W:(idx + 1) * GROUP_W, :])
        mix = part if mix is None else mix + part
    o_ref[...] = _layer_norm(DEEPNORM_ALPHA * h_ref[...] + mix, g_ref[...], b_ref[...])


def _out_proj(h2, ys, w_out, layer, g, b, tm=1024):
    T, D = h2.shape
    row = lambda w: pl.BlockSpec((tm, w), lambda i: (i, 0))
    const = lambda shape: pl.BlockSpec(shape, lambda i: (0,) * len(shape))
    return pl.pallas_call(
        _out_proj_kernel, name="out_proj_ln",
        grid=(T // tm,),
        in_specs=[row(D), row(GROUP_W), row(GROUP_W), row(GROUP_W), row(GROUP_W),
                  pl.BlockSpec((None, D, D), lambda i: (layer, 0, 0)), const((1, D)), const((1, D))],
        out_specs=row(D),
        out_shape=jax.ShapeDtypeStruct((T, D), F32),
        scratch_shapes=[pltpu.VMEM((D, D), BF16)],
        compiler_params=_cparams("arbitrary"),
    )(h2, *ys, w_out, g.reshape(1, D), b.reshape(1, D))


def _mlp_kernel(h_ref, w1_ref, w2_ref, g_ref, b_ref, o_ref, acc_ref):
    f = pl.program_id(1)

    @pl.when(f == 0)
    def _():
        acc_ref[...] = jnp.zeros_like(acc_ref)

    a = jnp.maximum(_mm(h_ref[...], w1_ref[...]), 0.0)
    acc_ref[...] += _mm(a * a, w2_ref[...])

    @pl.when(f == pl.num_programs(1) - 1)
    def _():
        o_ref[...] = _layer_norm(DEEPNORM_ALPHA * h_ref[...] + acc_ref[...], g_ref[...], b_ref[...])


def _mlp(h2, w1, w2, layer, g, b, tm=1024, tf=1024):
    T, D = h2.shape
    F = w1.shape[2]
    return pl.pallas_call(
        _mlp_kernel, name="mlp_ln",
        grid=(T // tm, F // tf),
        in_specs=[pl.BlockSpec((tm, D), lambda i, f: (i, 0)),
                  pl.BlockSpec((None, D, tf), lambda i, f: (layer, 0, f)),
                  pl.BlockSpec((None, tf, D), lambda i, f: (layer, f, 0)),
                  pl.BlockSpec((1, D), lambda i, f: (0, 0)),
                  pl.BlockSpec((1, D), lambda i, f: (0, 0))],
        out_specs=pl.BlockSpec((tm, D), lambda i, f: (i, 0)),
        out_shape=jax.ShapeDtypeStruct((T, D), F32),
        scratch_shapes=[pltpu.VMEM((tm, D), F32)],
        compiler_params=_cparams("parallel", "arbitrary"),
    )(h2, w1, w2, g.reshape(1, D), b.reshape(1, D))


def kernel(x, ln_emb_g, ln_emb_b, w_in, conv_w, conv_b, dt_bias, a_log, d_skip, ssm_norm_g, q_norm_g, w_uq, kv_norm_g, w_ukv, cmp_pe_k, cmp_w1_k, cmp_w2_k, cmp_pe_v, cmp_w1_v, cmp_w2_v, w_out, ln1_g, ln1_b, w_mlp1, w_mlp2, ln2_g, ln2_b):
    B, S, D = x.shape
    assert D == D_MODEL and S // NSA_SLC_LEN <= LANE
    assert S % NSA_TILE == 0 and S % NSA_Q == 0 and S % min(MLA_TILE, S) == 0 and S % SSM_CHUNK == 0
    T = B * S
    ret_tables = _ret_tables(S)
    mla_tables = _mla_tables(S)
    nsa_tables = _nsa_tables(S)
    h = x.reshape(T, D)
    for l in range(w_in.shape[0]):
        if l == 0:
            h, *proj = _in_proj(h, _layout_w_in(w_in[l]), nsa_tables[2], entry_ln=(ln_emb_g, ln_emb_b))
        else:
            proj = _in_proj(h, _layout_w_in(w_in[l]), nsa_tables[2])
        p_ssm, p_mla, p_ret, p_nsa, nsa_kv, uk, uv = proj
        cmp_rows = (B, S // NSA_CMP_STRIDE, uk.shape[-1])
        y_a, y_c = _ssm_and_retention(p_ssm.reshape(B, S, W_SSM), conv_w[l], conv_b[l], dt_bias[l], a_log[l],
                                      d_skip[l], ssm_norm_g[l], p_ret.reshape(B, S, W_RET), ret_tables)
        wq, wk, wv = _layout_mla_weights(w_uq[l], w_ukv[l])
        q, k, v = _mla_prep(p_mla.reshape(B, S, W_MLA), q_norm_g[l], kv_norm_g[l], wq, wk, wv, mla_tables)
        y_b = _mla_attn(q, k, v)
        y_d = _nsa(p_nsa.reshape(B, S, W_NSA), nsa_kv.reshape(B, S, W_KV), uk.reshape(cmp_rows), uv.reshape(cmp_rows),
                   cmp_pe_k[l], cmp_w1_k[l], cmp_w2_k[l], cmp_pe_v[l], cmp_w1_v[l], cmp_w2_v[l], nsa_tables)
        ys = [y.reshape(T, GROUP_W) for y in (y_a, y_b, y_c, y_d)]
        h = _out_proj(h, ys, w_out, l, ln1_g[l], ln1_b[l])
        h = _mlp(h, w_mlp1, w_mlp2, l, ln2_g[l], ln2_b[l])
    return h.reshape(B, S, D)
```

```python
import functools
import math

import jax
import jax.numpy as jnp
import numpy as np
from jax import lax
from jax.experimental import pallas as pl
from jax.experimental.pallas import tpu as pltpu

F32 = jnp.float32
BF16 = jnp.bfloat16

D_MODEL = 1024
DEPTH = 2
GROUP_W = D_MODEL // 4
SSM_HEADS = 4
SSM_HEAD_DIM = GROUP_W // SSM_HEADS
SSM_GROUPS = 2
SSM_STATE = 128
SSM_CONV = 4
SSM_CHUNK = 128
SSM_XBC = GROUP_W + 2 * SSM_GROUPS * SSM_STATE
MLA_HEADS = 4
MLA_NOPE = 64
MLA_ROPE = 32
MLA_V = GROUP_W // MLA_HEADS
MLA_Q_RANK = 256
MLA_KV_RANK = 128
RET_HEADS = 4
RET_DK = 64
RET_DV = GROUP_W // RET_HEADS
RET_CHUNK = 128
NSA_HEADS = 4
NSA_DK = 64
NSA_DV = GROUP_W // NSA_HEADS
NSA_CMP_LEN = 32
NSA_CMP_STRIDE = 16
NSA_CMP_HID = 256
NSA_SLC_LEN = 64
NSA_TOPN = 16
NSA_WIN = 512
D_FF = 4 * D_MODEL
NSA_Q = 512
ROPE_THETA = 10000.0
EPS = 1e-5
NEG_INF = -1e30
LOG2_E = math.log2(math.e)
FORCED_SCORE = 1e9
DEEPNORM_ALPHA = (2.0 * DEPTH) ** 0.25

IN_SPLITS = (
    GROUP_W, SSM_XBC, SSM_HEADS,
    MLA_Q_RANK, MLA_KV_RANK, MLA_ROPE,
    RET_HEADS * RET_DK, RET_HEADS * RET_DK, RET_HEADS * RET_DV, GROUP_W,
    NSA_HEADS * NSA_DK, NSA_DK, NSA_DV, NSA_DK, NSA_DV, NSA_DK, NSA_DV, 3 * NSA_HEADS,
)

LANE = 128
SUBLANE = 8
W_SSM = GROUP_W + SSM_XBC + LANE
W_MLA = MLA_Q_RANK + MLA_KV_RANK + LANE
W_RET = 4 * GROUP_W
W_NSA = NSA_HEADS * LANE + LANE + LANE
W_KV = 2 * LANE + 3 * LANE
W_PROJ = ((GROUP_W + SSM_XBC) + (MLA_Q_RANK + MLA_KV_RANK) + W_RET
          + (NSA_HEADS * NSA_DK + LANE) + 4 * NSA_DK + LANE)

RECURRENT_CHUNKS_PER_STEP = 4
NSA_TILE = 512
NSA_SINK = 128
MLA_TILE = 1024
VMEM_LIMIT = 48 * 1024 * 1024


def _cparams(*sem):
    return pltpu.CompilerParams(dimension_semantics=sem, vmem_limit_bytes=VMEM_LIMIT)


def _mm(a, b):
    return jnp.dot(a.astype(BF16), b.astype(BF16), preferred_element_type=F32)


def _mm_nt(a, b):
    return lax.dot_general(a.astype(BF16), b.astype(BF16), (((1,), (1,)), ((), ())),
                           preferred_element_type=F32)


def _split_f32(x):
    hi = x.astype(BF16)
    rest = x - hi.astype(F32)
    mid = rest.astype(BF16)
    lo = (rest - mid.astype(F32)).astype(BF16)
    return hi, mid, lo


def _mm_f32(a, b, exact, dims=(((1,), (0,)), ((), ()))):
    fixed, pieces = (a.astype(BF16), _split_f32(b)) if exact == "a" else (b.astype(BF16), _split_f32(a))
    out = None
    for piece in pieces:
        lhs, rhs = (fixed, piece) if exact == "a" else (piece, fixed)
        part = lax.dot_general(lhs, rhs, dims, preferred_element_type=F32)
        out = part if out is None else out + part
    return out


def _silu(x):
    return x * jax.nn.sigmoid(x)


def _softplus(x):
    return jnp.maximum(x, 0.0) + jnp.log1p(jnp.exp(-jnp.abs(x)))


def _layer_norm(x, g, b):
    mu = jnp.mean(x, axis=-1, keepdims=True)
    xc = x - mu
    var = jnp.mean(xc * xc, axis=-1, keepdims=True)
    return xc * lax.rsqrt(var + EPS) * g + b


def _iota(shape, dim):
    return lax.broadcasted_iota(jnp.int32, shape, dim)


def _pad_cols(w, width):
    return jnp.pad(w, ((0, 0), (0, width - w.shape[1])))


def _layout_w_in(w):
    offs = np.concatenate([[0], np.cumsum(IN_SPLITS)])
    p = [w[:, int(offs[i]):int(offs[i + 1])] for i in range(len(IN_SPLITS))]
    (ssm_z, ssm_xbc, ssm_dt, mla_cq, mla_ckv, mla_kr, ret_q, ret_k, ret_v, ret_g,
     nsa_q, nsa_kc, nsa_vc, nsa_ks, nsa_vs, nsa_kw, nsa_vw, nsa_gate) = p
    small = _pad_cols(jnp.concatenate([mla_kr, ssm_dt, nsa_gate], axis=1), LANE)
    cols = [ssm_z, ssm_xbc, ret_q, ret_k, ret_v, ret_g, nsa_q, nsa_ks, nsa_vs, nsa_kw, nsa_vw,
            mla_cq, mla_ckv, nsa_kc, nsa_vc, small]
    out = jnp.concatenate(cols, axis=1)
    assert out.shape[1] == W_PROJ
    return out.astype(BF16)


def _in_proj_kernel(*refs, entry_ln):
    if entry_ln:
        h_ref, g_ref, b_ref, w_ref, kvtab_ref, hn_ref, *outs = refs
        hn = _layer_norm(h_ref[...], g_ref[...], b_ref[...])
        hn_ref[...] = hn
    else:
        h_ref, w_ref, kvtab_ref, *outs = refs
        hn = h_ref[...]
    ssm_ref, mla_ref, ret_ref, nsa_ref, kv_ref, uk_ref, uv_ref, kcv_ref = outs
    hb = hn.astype(BF16)
    tm = hb.shape[0]
    off = 0

    def project(width):
        nonlocal off
        out = jnp.dot(hb, w_ref[:, off:off + width], preferred_element_type=F32)
        off += width
        return out

    ssm_ref[:, 0:GROUP_W + SSM_XBC] = project(GROUP_W + SSM_XBC)
    ret_ref[...] = project(W_RET)
    q = project(NSA_HEADS * NSA_DK)
    kv = project(4 * NSA_DK)
    latent = MLA_Q_RANK + MLA_KV_RANK
    mla_kc = project(latent + LANE)
    mla_ref[:, 0:latent] = mla_kc[:, 0:latent]
    kc_lane = NSA_HEADS * LANE
    nsa_ref[:, kc_lane:kc_lane + LANE] = mla_kc[:, latent:]
    small = project(LANE)
    lane = _iota((tm, LANE), 1)
    low = lane < NSA_DK
    for h in range(NSA_HEADS):
        pair = q[:, (h // 2) * LANE:(h // 2 + 1) * LANE]
        head = pair if h % 2 == 0 else pltpu.roll(pair, NSA_DK, 1)
        nsa_ref[:, h * LANE:(h + 1) * LANE] = jnp.where(low, head, 0.0)
    mla_ref[:, MLA_Q_RANK + MLA_KV_RANK:] = jnp.where(lane < MLA_ROPE, small, 0.0)
    ssm_ref[:, GROUP_W + SSM_XBC:] = jnp.where(lane < SSM_HEADS, pltpu.roll(small, LANE - MLA_ROPE, 1), 0.0)
    gate_at = MLA_ROPE + SSM_HEADS
    nsa_ref[:, kc_lane + LANE:] = jnp.where(lane < 3 * NSA_HEADS, pltpu.roll(small, LANE - gate_at, 1), 0.0)
    groups = h_ref.shape[0] // NSA_CMP_STRIDE
    kc_lane = NSA_HEADS * LANE
    kcv_ref[...] = nsa_ref[:, kc_lane:kc_lane + LANE]
    for t in range(NSA_CMP_STRIDE):
        piece = kcv_ref[pl.ds(t, groups, stride=NSA_CMP_STRIDE), :]
        uk_ref[:, t * NSA_DK:(t + 1) * NSA_DK] = piece[:, :NSA_DK]
        uv_ref[:, t * NSA_DV:(t + 1) * NSA_DV] = piece[:, NSA_DK:]
    sel_kv, win_kv = kv[:, :LANE], kv[:, LANE:]
    pieces = {0: sel_kv, 2: pltpu.roll(sel_kv, NSA_DK, 1), 3: win_kv, 4: pltpu.roll(win_kv, NSA_DK, 1)}
    for slab in range(W_KV // LANE):
        lanes = slice(slab * LANE, (slab + 1) * LANE)
        tab = kvtab_ref[:, lanes]
        if slab in pieces:
            kv_ref[:, lanes] = (jnp.where(low, pieces[slab], 0.0) + tab.astype(F32)).astype(BF16)
        else:
            kv_ref[:, lanes] = tab


def _in_proj(h2, w_p, kv_table, entry_ln=None, tm=512):
    T, D = h2.shape
    S = kv_table.shape[0]
    widths = (W_SSM, W_MLA, W_RET, W_NSA)
    half = NSA_CMP_STRIDE * NSA_DK
    row = lambda w: pl.BlockSpec((tm, w), lambda i: (i, 0))
    const = lambda shape: pl.BlockSpec(shape, lambda i: (0,) * len(shape))
    in_specs = [const((D, W_PROJ)), pl.BlockSpec((tm, W_KV), lambda i: (i % (S // tm), 0))]
    out_specs = [row(w) for w in widths + (W_KV,)] + [pl.BlockSpec((tm // NSA_CMP_STRIDE, half), lambda i: (i, 0))] * 2
    out_shape = ([jax.ShapeDtypeStruct((T, w), F32) for w in widths] + [jax.ShapeDtypeStruct((T, W_KV), BF16)]
                 + [jax.ShapeDtypeStruct((T // NSA_CMP_STRIDE, half), F32)] * 2)
    operands = (w_p, kv_table)
    if entry_ln is not None:
        in_specs = [const((1, D)), const((1, D))] + in_specs
        out_specs = [row(D)] + out_specs
        out_shape = [jax.ShapeDtypeStruct((T, D), F32)] + out_shape
        operands = tuple(v.reshape(1, D) for v in entry_ln) + operands
    return pl.pallas_call(
        functools.partial(_in_proj_kernel, entry_ln=entry_ln is not None), name="in_proj",
        grid=(T // tm,),
        in_specs=[row(D)] + in_specs,
        out_specs=out_specs,
        out_shape=out_shape,
        scratch_shapes=[pltpu.VMEM((tm, LANE), F32)],
        compiler_params=_cparams("parallel"),
    )(h2, *operands)


def _ssm_chunk(p_ref, cw_ref, cb_ref, dtb_ref, alog_ref, dskip_ref, ng_ref, o_ref, state_ref, ext_ref):
    L, H, P, N = SSM_CHUNK, SSM_HEADS, SSM_HEAD_DIM, SSM_STATE
    z = p_ref[:, 0:GROUP_W]
    ext_ref[SUBLANE:SUBLANE + L, :] = p_ref[:, GROUP_W:GROUP_W + SSM_XBC]
    ext = ext_ref[...]
    conv = cb_ref[...] + ext[SUBLANE:] * cw_ref[SSM_CONV - 1:SSM_CONV, :]
    for j in range(SSM_CONV - 1):
        shift = SSM_CONV - 1 - j
        conv = conv + pltpu.roll(ext, shift, 0)[SUBLANE:] * cw_ref[j:j + 1, :]
    ext_ref[0:SUBLANE, :] = ext_ref[L:L + SUBLANE, :]
    xbc = _silu(conv)
    xs = xbc[:, 0:GROUP_W]
    b_in = xbc[:, GROUP_W:GROUP_W + SSM_GROUPS * N]
    c_in = xbc[:, GROUP_W + SSM_GROUPS * N:]

    dt = _softplus(p_ref[:, GROUP_W + SSM_XBC:] + dtb_ref[...])
    a = dt * (-jnp.exp(alog_ref[...]))
    row = _iota((L, L), 0)
    col = _iota((L, L), 1)
    tril = col <= row
    cs = _mm_f32(jnp.where(tril, 1.0, 0.0), a, exact="a")
    cs_t = cs.T
    ecs = jnp.exp(cs)
    dte = jnp.exp(cs[L - 1:L, :] - cs)
    first_head_of_pair = _iota((L, LANE), 1) < P

    def expand(x):
        pairs = [jnp.where(first_head_of_pair, x[:, h:h + 1], x[:, h + 1:h + 2]) for h in range(0, H, LANE // P)]
        return jnp.concatenate(pairs, axis=1)

    assert LANE == 2 * P
    dt_x = expand(dt)
    ecs_x = expand(ecs)
    dte_x = expand(dte)

    xdt = xs * dt_x
    wx = xdt * dte_x
    head_of_lane = _iota((L, H * P), 1) // P
    y = xs * dskip_ref[...]
    y_off = []
    rep = H // SSM_GROUPS
    for g in range(SSM_GROUPS):
        cg = c_in[:, g * N:(g + 1) * N]
        bg = b_in[:, g * N:(g + 1) * N]
        cb = _mm_nt(cg, bg)
        for h in range(g * rep, (g + 1) * rep):
            diff = cs[:, h:h + 1] - cs_t[h:h + 1, :]
            seg = jnp.where(tril, jnp.exp(jnp.where(tril, diff, 0.0)), 0.0)
            yh = _mm(cb * seg, xdt)
            y = y + jnp.where(head_of_lane == h, yh, 0.0)
        lanes = slice(g * rep * P, (g + 1) * rep * P)
        st_prev = state_ref[:, lanes]
        y_off.append(_mm(cg, st_prev))
        state_ref[:, lanes] = st_prev * ecs_x[L - 1:L, lanes] + _mm(bg.T, wx[:, lanes])
    y = y + jnp.concatenate(y_off, axis=1) * ecs_x
    y = y * _silu(z)
    ms = jnp.mean(y * y, axis=-1, keepdims=True)
    o_ref[...] = (y * lax.rsqrt(ms + EPS) * ng_ref[...]).astype(o_ref.dtype)


def _recurrent_kernel(ps_ref, cw_ref, cb_ref, dtb_ref, alog_ref, dskip_ref, ng_ref,
                      pr_ref, cos_ref, sin_ref, dec_ref, zeta_ref, xi_ref, cd_ref,
                      oa_ref, oc_ref, sstate_ref, ext_ref, rstate_ref):
    @pl.when(pl.program_id(0) == 0)
    def _():
        sstate_ref[...] = jnp.zeros_like(sstate_ref)
        rstate_ref[...] = jnp.zeros_like(rstate_ref)
        ext_ref[:, 0:SUBLANE, :] = jnp.zeros((ext_ref.shape[0], SUBLANE, SSM_XBC), F32)

    L = SSM_CHUNK
    for sub in range(ps_ref.shape[1] // L):
        rows = pl.ds(sub * L, L)
        for b in range(ps_ref.shape[0]):
            _ssm_chunk(ps_ref.at[b, rows], cw_ref, cb_ref, dtb_ref, alog_ref, dskip_ref, ng_ref,
                       oa_ref.at[b, rows], sstate_ref.at[b], ext_ref.at[b])
            _ret_chunk(pr_ref.at[b, rows], cos_ref.at[rows], sin_ref.at[rows], dec_ref, zeta_ref, xi_ref, cd_ref,
                       oc_ref.at[b, rows], rstate_ref.at[b])


def _ret_chunk(p_ref, cos_ref, sin_ref, dec_ref, zeta_ref, xi_ref, cd_ref, o_ref, state_ref):
    L, H, DK, DV = RET_CHUNK, RET_HEADS, RET_DK, RET_DV
    W = H * DK
    q = p_ref[:, 0:W]
    k = p_ref[:, W:2 * W]
    v = p_ref[:, 2 * W:3 * W]
    gate = p_ref[:, 3 * W:4 * W]
    lane = _iota((L, W), 1)
    first_half = (lane % DK) < (DK // 2)
    head_of_lane = lane // DK

    def rope(x):
        partner = jnp.where(first_half, pltpu.roll(x, W - DK // 2, 1), pltpu.roll(x, DK // 2, 1))
        return x * cos_ref[...] + partner * sin_ref[...]

    qr = rope(q)
    kr = rope(k) * (DK ** -0.5)
    y = jnp.zeros((L, H * DV), F32)
    for h in range(H):
        qh = jnp.where(head_of_lane == h, qr, 0.0)
        sc = _mm_nt(qh, kr) * dec_ref[h]
        y = y + jnp.where(head_of_lane == h, _mm(sc, v), 0.0)
    st = state_ref[...]
    y = y + _mm(qr * xi_ref[...], st)
    same_head = (_iota((W, H * DV), 0) // DK) == (_iota((W, H * DV), 1) // DV)
    kv = _mm((kr * zeta_ref[...]).T, v)
    state_ref[...] = st * cd_ref[...] + jnp.where(same_head, kv, 0.0)
    assert DV & (DV - 1) == 0
    ms = _mm_f32(y * y, jnp.where(same_head, 1.0 / DV, 0.0), exact="b")
    o_ref[...] = (y * lax.rsqrt(ms + EPS) * _silu(gate)).astype(o_ref.dtype)


def _ret_tables(S):
    H, DK, L = RET_HEADS, RET_DK, RET_CHUNK
    inv = ROPE_THETA ** (-np.arange(0, DK, 2, dtype=np.float64) / DK)
    ang = np.arange(S, dtype=np.float64)[:, None] * inv[None, :]
    cos, sin = np.cos(ang), np.sin(ang)
    cos_t = np.tile(np.concatenate([cos, cos], axis=1), (1, H))
    sin_t = np.tile(np.concatenate([-sin, sin], axis=1), (1, H))
    log_gamma = np.log1p(-np.exp2(-5.0 - np.arange(H, dtype=np.float64)))
    pos = np.arange(L, dtype=np.float64)
    diff = pos[:, None] - pos[None, :]
    decay_in = np.where(diff >= 0, np.exp(np.maximum(diff, 0.0)[None] * log_gamma[:, None, None]), 0.0)
    zeta = np.exp((L - 1 - pos)[None] * log_gamma[:, None])
    xi = np.exp((pos + 1.0)[None] * log_gamma[:, None])
    chunk_decay = np.exp(L * log_gamma)
    zeta_x = np.repeat(zeta.T, DK, axis=1)
    xi_x = np.repeat(xi.T, DK, axis=1)
    cd_x = np.repeat(chunk_decay, RET_DV).reshape(1, H * RET_DV)
    return tuple(jnp.asarray(t, dtype=F32) for t in (cos_t, sin_t, decay_in, zeta_x, xi_x, cd_x))


def _ssm_and_retention(p_ssm, conv_w, conv_b, dt_bias, a_log, d_skip, norm_g, p_ret, tables):
    B, S, _ = p_ssm.shape
    L, H = SSM_CHUNK, RET_HEADS
    assert RET_CHUNK == L
    W = H * RET_DK
    cos_t, sin_t, decay_in, zeta_x, xi_x, cd_x = tables
    pad_h = lambda v: jnp.pad(v, (0, LANE - SSM_HEADS)).reshape(1, LANE)
    const = lambda shape: pl.BlockSpec(shape, lambda c: (0,) * len(shape))
    rows = RECURRENT_CHUNKS_PER_STEP * L
    assert S % rows == 0
    chunk = lambda width: pl.BlockSpec((B, rows, width), lambda c: (0, c, 0))
    out_shape = jax.ShapeDtypeStruct((B, S, GROUP_W), BF16)
    return pl.pallas_call(
        _recurrent_kernel, name="ssm_retention",
        grid=(S // rows,),
        in_specs=[chunk(W_SSM),
                  const((SSM_CONV, SSM_XBC)), const((1, SSM_XBC)), const((1, LANE)), const((1, LANE)),
                  const((1, GROUP_W)), const((1, GROUP_W)),
                  chunk(W_RET),
                  pl.BlockSpec((rows, W), lambda c: (c, 0)), pl.BlockSpec((rows, W), lambda c: (c, 0)),
                  const((H, L, L)), const((L, W)), const((L, W)), const((1, H * RET_DV))],
        out_specs=[chunk(GROUP_W), chunk(GROUP_W)],
        out_shape=[out_shape, out_shape],
        scratch_shapes=[pltpu.VMEM((B, SSM_STATE, GROUP_W), F32),
                        pltpu.VMEM((B, L + SUBLANE, SSM_XBC), F32),
                        pltpu.VMEM((B, W, H * RET_DV), F32)],
        compiler_params=_cparams("arbitrary"),
    )(p_ssm, conv_w, conv_b.reshape(1, -1), pad_h(dt_bias), pad_h(a_log),
      jnp.repeat(d_skip, SSM_HEAD_DIM).reshape(1, GROUP_W), norm_g.reshape(1, GROUP_W),
      p_ret, cos_t, sin_t, decay_in, zeta_x, xi_x, cd_x)


def _mla_prep_kernel(p_ref, qg_ref, wq_ref, kvg_ref, wk_ref, wv_ref, cos_ref, sin_ref,
                     q_ref, k_ref, v_ref):
    tm = p_ref.shape[0]
    cq = p_ref[:, 0:MLA_Q_RANK]
    ckv = p_ref[:, MLA_Q_RANK:MLA_Q_RANK + MLA_KV_RANK]
    kr = p_ref[:, MLA_Q_RANK + MLA_KV_RANK:]

    def rms(x, g):
        return x * lax.rsqrt(jnp.mean(x * x, axis=-1, keepdims=True) + EPS) * g

    q_both = _mm(rms(cq, qg_ref[...]), wq_ref[...])
    q = q_both[:, :MLA_HEADS * LANE]
    q_partner = q_both[:, MLA_HEADS * LANE:]
    kvl = rms(ckv, kvg_ref[...])
    kn = _mm(kvl, wk_ref[...])
    vv = _mm(kvl, wv_ref[...])
    kr_sh = pltpu.roll(kr, MLA_NOPE, 1)
    lane = _iota((tm, LANE), 1)
    half = MLA_ROPE // 2
    low = (lane >= MLA_NOPE) & (lane < MLA_NOPE + half)
    cos = cos_ref[...]
    sin = sin_ref[...]

    def rope(x):
        partner = jnp.where(low, pltpu.roll(x, LANE - half, 1), pltpu.roll(x, half, 1))
        return x * cos + partner * sin

    scale = (MLA_NOPE + MLA_ROPE) ** -0.5 * LOG2_E
    k_pe = rope(kr_sh)
    for h in range(MLA_HEADS):
        sl = slice(h * LANE, (h + 1) * LANE)
        q_ref[h] = ((q[:, sl] * cos + q_partner[:, sl] * sin) * scale).astype(BF16)
        k_ref[h] = (kn[:, sl] + k_pe).astype(BF16)
        v_ref[h] = jnp.where(lane == _mla_ones_lane(h), 1.0, vv[:, sl]).astype(BF16)


def _mla_tables(S):
    inv = ROPE_THETA ** (-np.arange(0, MLA_ROPE, 2, dtype=np.float64) / MLA_ROPE)
    ang = np.arange(S, dtype=np.float64)[:, None] * inv[None, :]
    cos, sin = np.cos(ang), np.sin(ang)
    tail = LANE - MLA_NOPE - MLA_ROPE
    cos_t = np.concatenate([np.ones((S, MLA_NOPE)), cos, cos, np.ones((S, tail))], axis=1)
    sin_t = np.concatenate([np.zeros((S, MLA_NOPE)), -sin, sin, np.zeros((S, tail))], axis=1)
    return jnp.asarray(cos_t, dtype=F32), jnp.asarray(sin_t, dtype=F32)


def _layout_mla_weights(w_uq, w_ukv):
    H = MLA_HEADS
    dq = MLA_NOPE + MLA_ROPE
    wq = jnp.concatenate([_pad_cols(w_uq[:, h * dq:(h + 1) * dq], LANE) for h in range(H)], axis=1)
    half = MLA_ROPE // 2
    partner = []
    for h in range(H):
        pe = w_uq[:, h * dq + MLA_NOPE:(h + 1) * dq]
        swapped = jnp.concatenate([jnp.zeros_like(w_uq[:, :MLA_NOPE]), pe[:, half:], pe[:, :half]], axis=1)
        partner.append(_pad_cols(swapped, LANE))
    wq = jnp.concatenate([wq] + partner, axis=1)
    dkv = MLA_NOPE + MLA_V
    wk, wv = [], []
    for h in range(H):
        blk = w_ukv[:, h * dkv:(h + 1) * dkv]
        wk.append(_pad_cols(blk[:, :MLA_NOPE], LANE))
        v = blk[:, MLA_NOPE:]
        zero = jnp.zeros_like(v)
        wv.append(jnp.concatenate([v, zero] if h % 2 == 0 else [zero, v], axis=1))
    return wq.astype(BF16), jnp.concatenate(wk, axis=1).astype(BF16), jnp.concatenate(wv, axis=1).astype(BF16)


def _mla_prep(p_mla, q_norm_g, kv_norm_g, wq, wk, wv, tables, tm=1024):
    B, S, _ = p_mla.shape
    H = MLA_HEADS
    cos_t, sin_t = tables
    const = lambda shape: pl.BlockSpec(shape, lambda b, i: (0,) * len(shape))
    qkv_spec = pl.BlockSpec((None, H, tm, LANE), lambda b, i: (b, 0, i, 0))
    qkv_shape = jax.ShapeDtypeStruct((B, H, S, LANE), BF16)
    return pl.pallas_call(
        _mla_prep_kernel, name="mla_prep",
        grid=(B, S // tm),
        in_specs=[pl.BlockSpec((None, tm, W_MLA), lambda b, i: (b, i, 0)),
                  const((1, MLA_Q_RANK)), const((MLA_Q_RANK, 2 * H * LANE)),
                  const((1, MLA_KV_RANK)), const((MLA_KV_RANK, H * LANE)), const((MLA_KV_RANK, H * LANE)),
                  pl.BlockSpec((tm, LANE), lambda b, i: (i, 0)),
                  pl.BlockSpec((tm, LANE), lambda b, i: (i, 0))],
        out_specs=[qkv_spec, qkv_spec, qkv_spec],
        out_shape=[qkv_shape, qkv_shape, qkv_shape],
        compiler_params=_cparams("parallel", "parallel"),
    )(p_mla, q_norm_g.reshape(1, -1), wq, kv_norm_g.reshape(1, -1), wk, wv, cos_t, sin_t)


def _mla_ones_lane(h):
    return MLA_V if h % 2 == 0 else 0


def _mla_attn_kernel(qi_ref, kj_ref, q_ref, k_ref, v_ref, o_ref, m_ref, acc_ref):
    H = MLA_HEADS
    tq, tk = q_ref.shape[1], k_ref.shape[1]
    i = qi_ref[pl.program_id(1)]
    j = kj_ref[pl.program_id(1)]

    @pl.when(j == 0)
    def _():
        m_ref[...] = jnp.full(m_ref.shape, NEG_INF, F32)
        acc_ref[...] = jnp.zeros_like(acc_ref)

    def sweep(blocks):
        nt_dims = (((1,), (1,)), ((), ()))
        scores = [[lax.dot_general(q_ref[h, r0:r0 + nr, :], k_ref[h, 0:nk, :], nt_dims, preferred_element_type=F32)
                   for (r0, nr, nk, _) in blocks] for h in range(H)]
        for h in range(H):
            for (r0, nr, nk, offset), s in zip(blocks, scores[h]):
                if offset is not None:
                    s = jnp.where(_iota((nr, nk), 1) - _iota((nr, nk), 0) <= offset, s, NEG_INF)
                rows = slice(r0, r0 + nr)
                m_prev = m_ref[h, rows]
                m_new = jnp.maximum(m_prev, jnp.max(s, axis=-1, keepdims=True))
                p = jnp.exp2(s - jnp.tile(m_new, (1, nk // LANE)))
                acc_ref[h, rows] = (jnp.exp2(m_prev - m_new) * acc_ref[h, rows]
                                    + jnp.dot(p.astype(BF16), v_ref[h, 0:nk, :], preferred_element_type=F32))
                m_ref[h, rows] = m_new

    assert tq == tk
    half = tq // 2

    @pl.when(j < i)
    def _():
        sweep([(0, tq, tk, None)])

    @pl.when(j == i)
    def _():
        sweep([(0, half, half, 0), (half, half, tk, half)])
        lane = _iota((tq, LANE), 1)
        for pair in range(H // 2):
            he, ho = 2 * pair, 2 * pair + 1
            acc_e, acc_o = acc_ref[he], acc_ref[ho]
            le = acc_e[:, _mla_ones_lane(he):_mla_ones_lane(he) + 1]
            lo = acc_o[:, _mla_ones_lane(ho):_mla_ones_lane(ho) + 1]
            o_ref[:, pair * LANE:(pair + 1) * LANE] = jnp.where(lane < MLA_V, acc_e / le, acc_o / lo).astype(o_ref.dtype)


def _mla_attn(q, k, v):
    B, H, S, _ = q.shape
    t = min(MLA_TILE, S)
    tq = t
    pairs = [(i, j) for i in range(S // tq) for j in range((i + 1) * tq // t)]
    qi = jnp.asarray([p[0] for p in pairs], jnp.int32)
    kj = jnp.asarray([p[1] for p in pairs], jnp.int32)
    grid_spec = pltpu.PrefetchScalarGridSpec(
        num_scalar_prefetch=2,
        grid=(B, len(pairs)),
        in_specs=[pl.BlockSpec((None, H, tq, LANE), lambda b, p, qi, kj: (b, 0, qi[p], 0)),
                  pl.BlockSpec((None, H, t, LANE), lambda b, p, qi, kj: (b, 0, kj[p], 0)),
                  pl.BlockSpec((None, H, t, LANE), lambda b, p, qi, kj: (b, 0, kj[p], 0))],
        out_specs=pl.BlockSpec((None, tq, GROUP_W), lambda b, p, qi, kj: (b, qi[p], 0)),
        scratch_shapes=[pltpu.VMEM((H, tq, LANE), F32), pltpu.VMEM((H, tq, LANE), F32)],
    )
    return pl.pallas_call(
        _mla_attn_kernel, name="mla_attn",
        grid_spec=grid_spec,
        out_shape=jax.ShapeDtypeStruct((B, S, GROUP_W), BF16),
        compiler_params=_cparams("parallel", "arbitrary"),
    )(qi, kj, q, k, v)


POS_HI = NSA_DK
POS_LO = NSA_DK + 3
POS_ONE = NSA_DK + 6
ONES_LANE = NSA_DV


def _split_bf16(x, parts=3):
    out, rem = [], np.float64(x)
    for _ in range(parts):
        piece = np.float64(np.float32(rem).astype(jnp.bfloat16).astype(np.float32))
        out.append(float(piece))
        rem = rem - piece
    return out


def _nsa_query_table():
    H = NSA_HEADS
    tab = np.zeros((2 * H, LANE), np.float32)
    for h in range(H):
        c = 2.0 ** (-8.0 * (h + 1) / H) * LOG2_E
        pieces = _split_bf16(c)
        tab[h, POS_HI:POS_HI + 3] = pieces
        tab[h, POS_LO:POS_LO + 3] = pieces
        tab[H + h, POS_ONE] = -sum(pieces)
    return jnp.asarray(tab)


def _nsa_pos_lanes(pos, lo_offset=0.0):
    t = np.zeros((len(pos), LANE - NSA_DK), np.float32)
    t[:, POS_HI - NSA_DK:POS_HI - NSA_DK + 3] = (NSA_SLC_LEN * (pos // NSA_SLC_LEN))[:, None]
    t[:, POS_LO - NSA_DK:POS_LO - NSA_DK + 3] = (pos % NSA_SLC_LEN + lo_offset)[:, None]
    t[:, POS_ONE - NSA_DK] = 1.0
    return t


def _nsa_queries(q_ref, qtab_ref, qb):
    Q, H = q_ref.shape[0], NSA_HEADS
    qpos = (qb * Q + _iota((Q, 1), 0)).astype(F32)
    out = []
    for h in range(H):
        q = q_ref[:, h * LANE:(h + 1) * LANE] * (NSA_DK ** -0.5 * LOG2_E)
        out.append((q + qtab_ref[h:h + 1, :] + qtab_ref[H + h:H + h + 1, :] * qpos).astype(BF16))
    return out


def _normalise(o):
    return o / o[:, ONES_LANE:ONES_LANE + 1]


def _stacked_gate(gates, branch):
    lanes = [3 * h + branch for h in range(NSA_HEADS)]
    return jnp.concatenate([gates[:, c:c + 1] for c in lanes], axis=0)


def _unstack_heads(o):
    Q = o.shape[0] // NSA_HEADS
    lane = _iota((Q, LANE), 1)
    out = []
    for pair in range(NSA_HEADS // 2):
        even = o[(2 * pair) * Q:(2 * pair + 1) * Q]
        odd = o[(2 * pair + 1) * Q:(2 * pair + 2) * Q]
        out.append(jnp.where(lane < NSA_DV, even, pltpu.roll(odd, NSA_DV, 1)))
    return jnp.concatenate(out, axis=1)


def _nsa_cmp_kernel(uk_ref, uv_ref, pek_ref, pev_ref, w1k_ref, w1v_ref, w2k_ref, w2v_ref, cpos_ref,
                    kc_ref, vc_ref, sh_ref):
    nb = uk_ref.shape[0]
    half = uk_ref.shape[1]

    def hidden(u_ref, pe_ref, w1_ref):
        u = u_ref[...]
        first = _mm(u + pe_ref[0:1, :], w1_ref[0:half, :])
        second = _mm(u + pe_ref[1:2, :], w1_ref[half:2 * half, :])
        sh_ref[0:nb, :] = second
        sh_ref[nb:nb + SUBLANE, :] = jnp.zeros((SUBLANE, NSA_CMP_HID), F32)
        return first + sh_ref[pl.ds(1, nb), :]

    hk = _silu(hidden(uk_ref, pek_ref, w1k_ref))
    hv = _silu(hidden(uv_ref, pev_ref, w1v_ref))
    kc_ref[...] = (_mm(hk, w2k_ref[...]) + cpos_ref[...]).astype(BF16)
    ones_lane = jnp.where(_iota((1, LANE), 1) == ONES_LANE, 1.0, 0.0)
    vc_ref[...] = (_mm(hv, w2v_ref[...]) + ones_lane).astype(BF16)


def _nsa_compress(uk, uv, pe_k, w1_k, w2_k, pe_v, w1_v, w2_v):
    B, nb, half = uk.shape
    hid = NSA_CMP_HID
    const = lambda shape: pl.BlockSpec(shape, lambda b: (0,) * len(shape))
    w2k = _pad_cols(w2_k, LANE).astype(BF16)
    w2v = _pad_cols(w2_v, LANE).astype(BF16)
    centre = _nsa_pos_lanes(np.arange(nb) * NSA_CMP_STRIDE, 0.5 * (NSA_CMP_LEN - 1))
    cpos = jnp.asarray(np.concatenate([np.zeros((nb, NSA_DK), np.float32), centre], axis=1))
    out_spec = pl.BlockSpec((None, nb, LANE), lambda b: (b, 0, 0))
    out_shape = jax.ShapeDtypeStruct((B, nb, LANE), BF16)
    return pl.pallas_call(
        _nsa_cmp_kernel, name="nsa_compress",
        grid=(B,),
        in_specs=[pl.BlockSpec((None, nb, half), lambda b: (b, 0, 0)),
                  pl.BlockSpec((None, nb, half), lambda b: (b, 0, 0)),
                  const((2, half)), const((2, half)),
                  const((2 * half, hid)), const((2 * half, hid)),
                  const((hid, LANE)), const((hid, LANE)), const((nb, LANE))],
        out_specs=[out_spec, out_spec],
        out_shape=[out_shape, out_shape],
        scratch_shapes=[pltpu.VMEM((nb + SUBLANE, hid), F32)],
        compiler_params=_cparams("parallel"),
    )(uk, uv, pe_k.reshape(2, half), pe_v.reshape(2, half), w1_k.astype(BF16), w1_v.astype(BF16), w2k, w2v, cpos)


def _nsa_sel_kernel(q_ref, qtab_ref, gate_ref, kc_ref, vc_ref, ovt_ref, oc_ref, selb_ref, any_ref, *, n_slc, top_n):
    Q, H = q_ref.shape[0], NSA_HEADS
    qb = pl.program_id(1)
    nc = kc_ref.shape[0]
    pick = functools.partial(_nsa_pick_blocks, qb=qb, selb_ref=selb_ref, any_ref=any_ref, n_slc=n_slc, top_n=top_n)
    qs = jnp.concatenate(_nsa_queries(q_ref, qtab_ref, qb), axis=0)
    nt_dims = (((1,), (1,)), ((), ()))

    def attend(ncols):
        s = lax.dot_general(qs, kc_ref[0:ncols, :], nt_dims, preferred_element_type=F32)
        qpos = qb * Q + (_iota((H * Q, ncols), 0) & (Q - 1))
        block_end = _iota((H * Q, ncols), 1) * NSA_CMP_STRIDE + (NSA_CMP_LEN - 1)
        s = jnp.where(block_end <= qpos, s, NEG_INF)
        e = jnp.exp2(s - jnp.max(s, axis=-1, keepdims=True))
        qpos_col = qb * Q + (_iota((H * Q, 1), 0) & (Q - 1))
        has_block = jnp.where(qpos_col >= NSA_CMP_LEN - 1, 1.0, 0.0)
        p = e * (has_block / jnp.sum(e, axis=-1, keepdims=True))
        o_c = jnp.dot(p.astype(BF16), vc_ref[0:ncols, :], preferred_element_type=F32)
        oc_ref[...] = _unstack_heads(_stacked_gate(jax.nn.sigmoid(gate_ref[...]), 0) * o_c)
        p_sum = p[0:Q]
        for h in range(1, H):
            p_sum = p_sum + p[h * Q:(h + 1) * Q]
        pick(_mm_f32(ovt_ref[:, 0:ncols], p_sum, exact="a", dims=nt_dims))

    tiles_needed = ((qb + 1) * Q // NSA_CMP_STRIDE + LANE - 1) // LANE
    for tiles in range(1, nc // LANE + 1):
        pl.when(tiles_needed == tiles)(functools.partial(attend, tiles * LANE))


def _nsa_pick_blocks(imp, qb, selb_ref, any_ref, n_slc, top_n):
    Q = imp.shape[1]
    blk = _iota((LANE, Q), 0)
    q_blk = (qb * Q + _iota((LANE, Q), 1)) >> int(math.log2(NSA_SLC_LEN))
    causal = blk <= q_blk
    for forced_blk in (0, q_blk, q_blk - 1):
        imp = jnp.where(blk == forced_blk, FORCED_SCORE, imp)
    imp = jnp.where(causal, imp, -1.0)
    imp = jnp.where(blk < n_slc, imp, -2.0)
    blk_f = blk.astype(F32)
    sel = jnp.zeros((LANE, Q), F32)
    for _ in range(top_n):
        m = jnp.max(imp, axis=0, keepdims=True)
        first = jnp.min(jnp.where(imp == m, blk_f, float(LANE)), axis=0, keepdims=True)
        hit = blk_f == first
        sel = jnp.where(hit, 1.0, sel)
        imp = jnp.where(hit, -3.0, imp)
    sel = jnp.where(causal, sel, 0.0).T
    selb_ref[...] = jnp.where(sel > 0.5, 0.0, NEG_INF).astype(BF16)
    any_ref[...] = jnp.max(sel, axis=0, keepdims=True)


def _nsa_select(p_nsa, qtab, kc, vc, overlap_t):
    B, S, _ = p_nsa.shape
    Q = NSA_Q
    nqb = S // Q
    nc = kc.shape[1]
    n_slc = S // NSA_SLC_LEN
    kern = functools.partial(_nsa_sel_kernel, n_slc=n_slc, top_n=min(NSA_TOPN, n_slc))
    return pl.pallas_call(
        kern, name="nsa_select",
        grid=(B, nqb),
        in_specs=[pl.BlockSpec((None, Q, NSA_HEADS * LANE), lambda b, i: (b, i, 0)),
                  pl.BlockSpec((2 * NSA_HEADS, LANE), lambda b, i: (0, 0)),
                  pl.BlockSpec((None, Q, LANE), lambda b, i: (b, i, (W_NSA - LANE) // LANE)),
                  pl.BlockSpec((None, nc, LANE), lambda b, i: (b, 0, 0)),
                  pl.BlockSpec((None, nc, LANE), lambda b, i: (b, 0, 0)),
                  pl.BlockSpec((LANE, nc), lambda b, i: (0, 0))],
        out_specs=[pl.BlockSpec((None, Q, GROUP_W), lambda b, i: (b, i, 0)),
                   pl.BlockSpec((None, Q, LANE), lambda b, i: (b, i, 0)),
                   pl.BlockSpec((None, None, 1, LANE), lambda b, i: (b, i, 0, 0))],
        out_shape=[jax.ShapeDtypeStruct((B, S, GROUP_W), F32),
                   jax.ShapeDtypeStruct((B, S, LANE), BF16),
                   jax.ShapeDtypeStruct((B, nqb, 1, LANE), F32)],
        compiler_params=_cparams("parallel", "parallel"),
    )(p_nsa, qtab, p_nsa, kc, vc, overlap_t)


def _nsa_attn_kernel(flags_ref, q_ref, qtab_ref, gate_ref, oc_ref, selb_ref, ks_ref, vs_ref, kw_ref, vw_ref,
                     o_ref, m_ref, acc_ref, ow_ref, *, nt):
    Q, H, TK = q_ref.shape[0], NSA_HEADS, NSA_TILE
    b = pl.program_id(0)
    qb = pl.program_id(1)
    nqb = pl.num_programs(1)
    qh = _nsa_queries(q_ref, qtab_ref, qb)
    nt_dims = (((1,), (1,)), ((), ()))

    selb = selb_ref[...]
    qs_sel = jnp.concatenate([jnp.concatenate([q, selb], axis=1) for q in qh], axis=0)
    sink_off = jnp.where(_iota(selb.shape, 1) < NSA_SINK // NSA_SLC_LEN, NEG_INF, selb.astype(F32)).astype(BF16)
    qs_loop = jnp.concatenate([jnp.concatenate([q, sink_off], axis=1) for q in qh], axis=0)
    m_ref[...] = jnp.full(m_ref.shape, NEG_INF, F32)
    acc_ref[...] = jnp.zeros_like(acc_ref)

    def online_update(m_prev, acc_prev, blocks):
        m_new = m_prev
        for s, _ in blocks:
            m_new = jnp.maximum(m_new, jnp.max(s, axis=-1, keepdims=True))
        acc = jnp.exp2(m_prev - m_new) * acc_prev
        for s, v in blocks:
            p = jnp.exp2(s - jnp.tile(m_new, (1, s.shape[1] // LANE)))
            acc = acc + jnp.dot(p.astype(BF16), v, preferred_element_type=F32)
        return m_new, acc

    def tile(t, carry):
        @pl.when(flags_ref[(b * nqb + qb) * nt + t] > 0)
        def _():
            rows = pl.ds(pl.multiple_of(t * TK, TK), TK)
            s = lax.dot_general(qs_loop, ks_ref[rows, :], nt_dims, preferred_element_type=F32)
            m_ref[...], acc_ref[...] = online_update(m_ref[...], acc_ref[...], [(s, vs_ref[rows, :])])
        return carry

    assert Q == TK == NSA_WIN
    half = Q // 2
    lax.fori_loop(0, qb, tile, 0)

    k_sink, v_sink = ks_ref[0:NSA_SINK, :], vs_ref[0:NSA_SINK, :]
    sink_bias = jnp.where(qb > 0, 0.0, NEG_INF)
    lower = _iota((H * half, half), 1) <= (_iota((H * half, half), 0) & (half - 1))
    for part in range(2):
        row_slices = [slice(h * Q + part * half, h * Q + (part + 1) * half) for h in range(H)]
        q_rows = jnp.concatenate([qs_sel[r] for r in row_slices], axis=0)
        keys = pl.ds(pl.multiple_of(qb * TK, TK), (part + 1) * half)
        s = lax.dot_general(q_rows, ks_ref[keys, :], nt_dims, preferred_element_type=F32)
        own = jnp.where(lower, s[:, part * half:], NEG_INF)
        s = own if part == 0 else jnp.concatenate([s[:, :half], own], axis=1)
        s_sink = lax.dot_general(q_rows, k_sink, nt_dims, preferred_element_type=F32) + sink_bias
        m_new, acc_new = online_update(jnp.concatenate([m_ref[r] for r in row_slices], axis=0),
                                       jnp.concatenate([acc_ref[r] for r in row_slices], axis=0),
                                       [(s, vs_ref[keys, :]), (s_sink, v_sink)])
        for h, r in enumerate(row_slices):
            m_ref[r] = m_new[h * half:(h + 1) * half]
            acc_ref[r] = acc_new[h * half:(h + 1) * half]


    def band(q_rows, first_key, n_keys, masks):
        keys = pl.ds(pl.multiple_of(first_key, half), n_keys)
        s = lax.dot_general(q_rows, kw_ref[keys, :], nt_dims, preferred_element_type=F32)
        groups = [s[:, g * half:(g + 1) * half] for g in range(n_keys // half)]
        s = jnp.concatenate([g if m is None else jnp.where(m, g, NEG_INF) for g, m in zip(groups, masks)], axis=1)
        p = jnp.exp2(s - jnp.max(s, axis=-1, keepdims=True))
        return _normalise(jnp.dot(p.astype(BF16), vw_ref[keys, :], preferred_element_type=F32))

    @pl.when(qb == 0)
    def _():
        qs_win = jnp.concatenate(qh, axis=0)
        row = _iota((H * Q, half), 0) & (Q - 1)
        col = _iota((H * Q, half), 1)
        ow_ref[...] = band(qs_win, 0, Q, [col <= row, col + half <= row])

    @pl.when(qb > 0)
    def _():
        row = _iota((H * half, half), 0) & (half - 1)
        col = _iota((H * half, half), 1)
        masks = [col > row, None, col <= row]
        for part in range(2):
            q_rows = jnp.concatenate([q[part * half:(part + 1) * half] for q in qh], axis=0)
            o_part = band(q_rows, (qb - 1) * Q + part * half, NSA_WIN + half, masks)
            for h in range(H):
                ow_ref[h * Q + part * half:h * Q + (part + 1) * half, :] = o_part[h * half:(h + 1) * half]

    gates = jax.nn.sigmoid(gate_ref[...])
    mixed = _stacked_gate(gates, 1) * _normalise(acc_ref[...]) + _stacked_gate(gates, 2) * ow_ref[...]
    o_ref[...] = (oc_ref[...] + _unstack_heads(mixed)).astype(o_ref.dtype)


def _nsa_attend(p_nsa, qtab, o_c, selb, flags, kv):
    B, S, _ = p_nsa.shape
    Q = NSA_Q
    nqb = S // Q
    nt = S // NSA_TILE
    gate_blk = (W_NSA - LANE) // LANE
    kern = functools.partial(_nsa_attn_kernel, nt=nt)
    slab = lambda width, col: pl.BlockSpec((None, S, width), lambda b, i, f: (b, 0, col))
    grid_spec = pltpu.PrefetchScalarGridSpec(
        num_scalar_prefetch=1,
        grid=(B, nqb),
        in_specs=[pl.BlockSpec((None, Q, NSA_HEADS * LANE), lambda b, i, f: (b, i, 0)),
                  pl.BlockSpec((2 * NSA_HEADS, LANE), lambda b, i, f: (0, 0)),
                  pl.BlockSpec((None, Q, LANE), lambda b, i, f: (b, i, gate_blk)),
                  pl.BlockSpec((None, Q, GROUP_W), lambda b, i, f: (b, i, 0)),
                  pl.BlockSpec((None, Q, LANE), lambda b, i, f: (b, i, 0)),
                  slab(2 * LANE, 0), slab(LANE, 2), slab(LANE, 3), slab(LANE, 4)],
        out_specs=pl.BlockSpec((None, Q, GROUP_W), lambda b, i, f: (b, i, 0)),
        scratch_shapes=[pltpu.VMEM((NSA_HEADS * Q, LANE), F32)] * 3,
    )
    return pl.pallas_call(
        kern, name="nsa_attend",
        grid_spec=grid_spec,
        out_shape=jax.ShapeDtypeStruct((B, S, GROUP_W), BF16),
        compiler_params=_cparams("parallel", "parallel"),
    )(flags, p_nsa, qtab, p_nsa, o_c, selb, kv, kv, kv, kv)


def _nsa_tables(S):
    nc = S // NSA_CMP_STRIDE
    n = np.arange(nc)[None, :]
    j = np.arange(LANE)[:, None]
    start = n * NSA_CMP_STRIDE
    ov = (start < (j + 1) * NSA_SLC_LEN) & (start + NSA_CMP_LEN - 1 >= j * NSA_SLC_LEN)
    ov &= (n < (S - NSA_CMP_LEN) // NSA_CMP_STRIDE + 1) & (j < S // NSA_SLC_LEN)
    pos = np.arange(S)
    k_zero = np.zeros((S, NSA_DK), np.float32)
    block_onehot = (pos[:, None] // NSA_SLC_LEN == np.arange(LANE)[None, :]).astype(np.float32)
    v_lanes = np.zeros((S, LANE), np.float32)
    v_lanes[:, ONES_LANE] = 1.0
    kv_table = np.concatenate([k_zero, _nsa_pos_lanes(pos), block_onehot, v_lanes,
                               k_zero, _nsa_pos_lanes(pos), v_lanes], axis=1)
    assert kv_table.shape[1] == W_KV
    return _nsa_query_table(), jnp.asarray(ov.astype(np.float32)), jnp.asarray(kv_table, dtype=BF16)


def _nsa(p_nsa, kv, uk, uv, pe_k, w1_k, w2_k, pe_v, w1_v, w2_v, tables):
    B, S, _ = p_nsa.shape
    qtab, overlap_t, _ = tables
    kc, vc = _nsa_compress(uk, uv, pe_k, w1_k, w2_k, pe_v, w1_v, w2_v)
    o_c, selb, blk_any = _nsa_select(p_nsa, qtab, kc, vc, overlap_t)
    per_tile = NSA_TILE // NSA_SLC_LEN
    nt = S // NSA_TILE
    not_sink = (np.arange(nt * per_tile) >= NSA_SINK // NSA_SLC_LEN).astype(np.float32)
    blk_any = blk_any[:, :, 0, :nt * per_tile] * not_sink
    flags = blk_any.reshape(B, S // NSA_Q, nt, per_tile).max(axis=-1)
    flags = (flags > 0).astype(jnp.int32).reshape(-1)
    return _nsa_attend(p_nsa, qtab, o_c, selb, flags, kv)


def _out_proj_kernel(h_ref, ya_ref, yb_ref, yc_ref, yd_ref, w_ref, g_ref, b_ref, o_ref, wb_ref):
    @pl.when(pl.program_id(0) == 0)
    def _():
        wb_ref[...] = w_ref[...].astype(BF16)

    y = jnp.concatenate([ya_ref[...], yb_ref[...], yc_ref[...], yd_ref[...]], axis=1)
    mix = jnp.dot(y, wb_ref[...], preferred_element_type=F32)
    o_ref[...] = _layer_norm(DEEPNORM_ALPHA * h_ref[...] + mix, g_ref[...], b_ref[...])


def _out_proj(h2, ys, w_out, layer, g, b, tm=1024):
    T, D = h2.shape
    row = lambda w: pl.BlockSpec((tm, w), lambda i: (i, 0))
    const = lambda shape: pl.BlockSpec(shape, lambda i: (0,) * len(shape))
    return pl.pallas_call(
        _out_proj_kernel, name="out_proj_ln",
        grid=(T // tm,),
        in_specs=[row(D), row(GROUP_W), row(GROUP_W), row(GROUP_W), row(GROUP_W),
                  pl.BlockSpec((None, D, D), lambda i: (layer, 0, 0)), const((1, D)), const((1, D))],
        out_specs=row(D),
        out_shape=jax.ShapeDtypeStruct((T, D), F32),
        scratch_shapes=[pltpu.VMEM((D, D), BF16)],
        compiler_params=_cparams("arbitrary"),
    )(h2, *ys, w_out, g.reshape(1, D), b.reshape(1, D))


def _mlp_kernel(h_ref, w1_ref, w2_ref, g_ref, b_ref, o_ref, acc_ref):
    f = pl.program_id(1)

    @pl.when(f == 0)
    def _():
        acc_ref[...] = jnp.zeros_like(acc_ref)

    a = jnp.maximum(_mm(h_ref[...], w1_ref[...]), 0.0)
    acc_ref[...] += _mm(a * a, w2_ref[...])

    @pl.when(f == pl.num_programs(1) - 1)
    def _():
        o_ref[...] = _layer_norm(DEEPNORM_ALPHA * h_ref[...] + acc_ref[...], g_ref[...], b_ref[...])


def _mlp(h2, w1, w2, layer, g, b, tm=1024, tf=1024):
    T, D = h2.shape
    F = w1.shape[2]
    return pl.pallas_call(
        _mlp_kernel, name="mlp_ln",
        grid=(T // tm, F // tf),
        in_specs=[pl.BlockSpec((tm, D), lambda i, f: (i, 0)),
                  pl.BlockSpec((None, D, tf), lambda i, f: (layer, 0, f)),
                  pl.BlockSpec((None, tf, D), lambda i, f: (layer, f, 0)),
                  pl.BlockSpec((1, D), lambda i, f: (0, 0)),
                  pl.BlockSpec((1, D), lambda i, f: (0, 0))],
        out_specs=pl.BlockSpec((tm, D), lambda i, f: (i, 0)),
        out_shape=jax.ShapeDtypeStruct((T, D), F32),
        scratch_shapes=[pltpu.VMEM((tm, D), F32)],
        compiler_params=_cparams("parallel", "arbitrary"),
    )(h2, w1, w2, g.reshape(1, D), b.reshape(1, D))


def kernel(x, ln_emb_g, ln_emb_b, w_in, conv_w, conv_b, dt_bias, a_log, d_skip, ssm_norm_g, q_norm_g, w_uq, kv_norm_g, w_ukv, cmp_pe_k, cmp_w1_k, cmp_w2_k, cmp_pe_v, cmp_w1_v, cmp_w2_v, w_out, ln1_g, ln1_b, w_mlp1, w_mlp2, ln2_g, ln2_b):
    B, S, D = x.shape
    assert D == D_MODEL and S // NSA_SLC_LEN <= LANE
    assert S % NSA_TILE == 0 and S % NSA_Q == 0 and S % min(MLA_TILE, S) == 0 and S % SSM_CHUNK == 0
    T = B * S
    ret_tables = _ret_tables(S)
    mla_tables = _mla_tables(S)
    nsa_tables = _nsa_tables(S)
    h = x.reshape(T, D)
    for l in range(w_in.shape[0]):
        if l == 0:
            h, *proj = _in_proj(h, _layout_w_in(w_in[l]), nsa_tables[2], entry_ln=(ln_emb_g, ln_emb_b))
        else:
            proj = _in_proj(h, _layout_w_in(w_in[l]), nsa_tables[2])
        p_ssm, p_mla, p_ret, p_nsa, nsa_kv, uk, uv = proj
        cmp_rows = (B, S // NSA_CMP_STRIDE, uk.shape[-1])
        y_a, y_c = _ssm_and_retention(p_ssm.reshape(B, S, W_SSM), conv_w[l], conv_b[l], dt_bias[l], a_log[l],
                                      d_skip[l], ssm_norm_g[l], p_ret.reshape(B, S, W_RET), ret_tables)
        wq, wk, wv = _layout_mla_weights(w_uq[l], w_ukv[l])
        q, k, v = _mla_prep(p_mla.reshape(B, S, W_MLA), q_norm_g[l], kv_norm_g[l], wq, wk, wv, mla_tables)
        y_b = _mla_attn(q, k, v)
        y_d = _nsa(p_nsa.reshape(B, S, W_NSA), nsa_kv.reshape(B, S, W_KV), uk.reshape(cmp_rows), uv.reshape(cmp_rows),
                   cmp_pe_k[l], cmp_w1_k[l], cmp_w2_k[l], cmp_pe_v[l], cmp_w1_v[l], cmp_w2_v[l], nsa_tables)
        ys = [y.reshape(T, GROUP_W) for y in (y_a, y_b, y_c, y_d)]
        h = _out_proj(h, ys, w_out, l, ln1_g[l], ln1_b[l])
        h = _mlp(h, w_mlp1, w_mlp2, l, ln2_g[l], ln2_b[l])
    return h.reshape(B, S, D)
```

```python
import functools
import math

import jax
import jax.numpy as jnp
import numpy as np
from jax import lax
from jax.experimental import pallas as pl
from jax.experimental.pallas import tpu as pltpu

F32 = jnp.float32
BF16 = jnp.bfloat16

D_MODEL = 1024
DEPTH = 2
GROUP_W = D_MODEL // 4
SSM_HEADS = 4
SSM_HEAD_DIM = GROUP_W // SSM_HEADS
SSM_GROUPS = 2
SSM_STATE = 128
SSM_CONV = 4
SSM_CHUNK = 128
SSM_XBC = GROUP_W + 2 * SSM_GROUPS * SSM_STATE
MLA_HEADS = 4
MLA_NOPE = 64
MLA_ROPE = 32
MLA_V = GROUP_W // MLA_HEADS
MLA_Q_RANK = 256
MLA_KV_RANK = 128
RET_HEADS = 4
RET_DK = 64
RET_DV = GROUP_W // RET_HEADS
RET_CHUNK = 128
NSA_HEADS = 4
NSA_DK = 64
NSA_DV = GROUP_W // NSA_HEADS
NSA_CMP_LEN = 32
NSA_CMP_STRIDE = 16
NSA_CMP_HID = 256
NSA_SLC_LEN = 64
NSA_TOPN = 16
NSA_WIN = 512
D_FF = 4 * D_MODEL
NSA_Q = 512
ROPE_THETA = 10000.0
EPS = 1e-5
NEG_INF = -1e30
LOG2_E = math.log2(math.e)
FORCED_SCORE = 1e9
DEEPNORM_ALPHA = (2.0 * DEPTH) ** 0.25

IN_SPLITS = (
    GROUP_W, SSM_XBC, SSM_HEADS,
    MLA_Q_RANK, MLA_KV_RANK, MLA_ROPE,
    RET_HEADS * RET_DK, RET_HEADS * RET_DK, RET_HEADS * RET_DV, GROUP_W,
    NSA_HEADS * NSA_DK, NSA_DK, NSA_DV, NSA_DK, NSA_DV, NSA_DK, NSA_DV, 3 * NSA_HEADS,
)

LANE = 128
SUBLANE = 8
W_SSM = GROUP_W + SSM_XBC + LANE
W_MLA = MLA_Q_RANK + MLA_KV_RANK + LANE
W_RET = 4 * GROUP_W
W_NSA = NSA_HEADS * LANE + LANE + LANE
W_KV = 2 * LANE + 3 * LANE
W_PROJ = ((GROUP_W + SSM_XBC) + (MLA_Q_RANK + MLA_KV_RANK) + W_RET
          + (NSA_HEADS * NSA_DK + LANE) + 4 * NSA_DK + LANE)

RECURRENT_CHUNKS_PER_STEP = 4
NSA_TILE = 512
NSA_SINK = 128
MLA_TILE = 1024
VMEM_LIMIT = 48 * 1024 * 1024


def _cparams(*sem):
    return pltpu.CompilerParams(dimension_semantics=sem, vmem_limit_bytes=VMEM_LIMIT)


def _mm(a, b):
    return jnp.dot(a.astype(BF16), b.astype(BF16), preferred_element_type=F32)


def _mm_nt(a, b):
    return lax.dot_general(a.astype(BF16), b.astype(BF16), (((1,), (1,)), ((), ())),
                           preferred_element_type=F32)


def _split_f32(x):
    hi = x.astype(BF16)
    rest = x - hi.astype(F32)
    mid = rest.astype(BF16)
    lo = (rest - mid.astype(F32)).astype(BF16)
    return hi, mid, lo


def _mm_f32(a, b, exact, dims=(((1,), (0,)), ((), ()))):
    fixed, pieces = (a.astype(BF16), _split_f32(b)) if exact == "a" else (b.astype(BF16), _split_f32(a))
    out = None
    for piece in pieces:
        lhs, rhs = (fixed, piece) if exact == "a" else (piece, fixed)
        part = lax.dot_general(lhs, rhs, dims, preferred_element_type=F32)
        out = part if out is None else out + part
    return out


def _silu(x):
    return x * jax.nn.sigmoid(x)


def _softplus(x):
    return jnp.maximum(x, 0.0) + jnp.log1p(jnp.exp(-jnp.abs(x)))


def _layer_norm(x, g, b):
    mu = jnp.mean(x, axis=-1, keepdims=True)
    xc = x - mu
    var = jnp.mean(xc * xc, axis=-1, keepdims=True)
    return xc * lax.rsqrt(var + EPS) * g + b


def _iota(shape, dim):
    return lax.broadcasted_iota(jnp.int32, shape, dim)


def _pad_cols(w, width):
    return jnp.pad(w, ((0, 0), (0, width - w.shape[1])))


def _layout_w_in(w):
    offs = np.concatenate([[0], np.cumsum(IN_SPLITS)])
    p = [w[:, int(offs[i]):int(offs[i + 1])] for i in range(len(IN_SPLITS))]
    (ssm_z, ssm_xbc, ssm_dt, mla_cq, mla_ckv, mla_kr, ret_q, ret_k, ret_v, ret_g,
     nsa_q, nsa_kc, nsa_vc, nsa_ks, nsa_vs, nsa_kw, nsa_vw, nsa_gate) = p
    small = _pad_cols(jnp.concatenate([mla_kr, ssm_dt, nsa_gate], axis=1), LANE)
    cols = [ssm_z, ssm_xbc, ret_q, ret_k, ret_v, ret_g, nsa_q, nsa_ks, nsa_vs, nsa_kw, nsa_vw,
            mla_cq, mla_ckv, nsa_kc, nsa_vc, small]
    out = jnp.concatenate(cols, axis=1)
    assert out.shape[1] == W_PROJ
    return out.astype(BF16)


def _in_proj_kernel(*refs, entry_ln):
    if entry_ln:
        h_ref, g_ref, b_ref, w_ref, kvtab_ref, hn_ref, *outs = refs
        hn = _layer_norm(h_ref[...], g_ref[...], b_ref[...])
        hn_ref[...] = hn
    else:
        h_ref, w_ref, kvtab_ref, *outs = refs
        hn = h_ref[...]
    ssm_ref, mla_ref, ret_ref, nsa_ref, kv_ref, uk_ref, uv_ref, kcv_ref = outs
    hb = hn.astype(BF16)
    tm = hb.shape[0]
    off = 0

    def project(width):
        nonlocal off
        out = jnp.dot(hb, w_ref[:, off:off + width], preferred_element_type=F32)
        off += width
        return out

    ssm_ref[:, 0:GROUP_W + SSM_XBC] = project(GROUP_W + SSM_XBC)
    ret_ref[...] = project(W_RET)
    q = project(NSA_HEADS * NSA_DK)
    kv = project(4 * NSA_DK)
    latent = MLA_Q_RANK + MLA_KV_RANK
    mla_kc = project(latent + LANE)
    mla_ref[:, 0:latent] = mla_kc[:, 0:latent]
    kc_lane = NSA_HEADS * LANE
    nsa_ref[:, kc_lane:kc_lane + LANE] = mla_kc[:, latent:]
    small = project(LANE)
    lane = _iota((tm, LANE), 1)
    low = lane < NSA_DK
    for h in range(NSA_HEADS):
        pair = q[:, (h // 2) * LANE:(h // 2 + 1) * LANE]
        head = pair if h % 2 == 0 else pltpu.roll(pair, NSA_DK, 1)
        nsa_ref[:, h * LANE:(h + 1) * LANE] = jnp.where(low, head, 0.0)
    mla_ref[:, MLA_Q_RANK + MLA_KV_RANK:] = jnp.where(lane < MLA_ROPE, small, 0.0)
    ssm_ref[:, GROUP_W + SSM_XBC:] = jnp.where(lane < SSM_HEADS, pltpu.roll(small, LANE - MLA_ROPE, 1), 0.0)
    gate_at = MLA_ROPE + SSM_HEADS
    nsa_ref[:, kc_lane + LANE:] = jnp.where(lane < 3 * NSA_HEADS, pltpu.roll(small, LANE - gate_at, 1), 0.0)
    groups = h_ref.shape[0] // NSA_CMP_STRIDE
    kc_lane = NSA_HEADS * LANE
    kcv_ref[...] = nsa_ref[:, kc_lane:kc_lane + LANE]
    for t in range(NSA_CMP_STRIDE):
        piece = kcv_ref[pl.ds(t, groups, stride=NSA_CMP_STRIDE), :]
        uk_ref[:, t * NSA_DK:(t + 1) * NSA_DK] = piece[:, :NSA_DK]
        uv_ref[:, t * NSA_DV:(t + 1) * NSA_DV] = piece[:, NSA_DK:]
    sel_kv, win_kv = kv[:, :LANE], kv[:, LANE:]
    pieces = {0: sel_kv, 2: pltpu.roll(sel_kv, NSA_DK, 1), 3: win_kv, 4: pltpu.roll(win_kv, NSA_DK, 1)}
    for slab in range(W_KV // LANE):
        lanes = slice(slab * LANE, (slab + 1) * LANE)
        tab = kvtab_ref[:, lanes]
        if slab in pieces:
            kv_ref[:, lanes] = (jnp.where(low, pieces[slab], 0.0) + tab.astype(F32)).astype(BF16)
        else:
            kv_ref[:, lanes] = tab


def _in_proj(h2, w_p, kv_table, entry_ln=None, tm=512):
    T, D = h2.shape
    S = kv_table.shape[0]
    widths = (W_SSM, W_MLA, W_RET, W_NSA)
    half = NSA_CMP_STRIDE * NSA_DK
    row = lambda w: pl.BlockSpec((tm, w), lambda i: (i, 0))
    const = lambda shape: pl.BlockSpec(shape, lambda i: (0,) * len(shape))
    in_specs = [const((D, W_PROJ)), pl.BlockSpec((tm, W_KV), lambda i: (i % (S // tm), 0))]
    out_specs = [row(w) for w in widths + (W_KV,)] + [pl.BlockSpec((tm // NSA_CMP_STRIDE, half), lambda i: (i, 0))] * 2
    out_shape = ([jax.ShapeDtypeStruct((T, w), F32) for w in widths] + [jax.ShapeDtypeStruct((T, W_KV), BF16)]
                 + [jax.ShapeDtypeStruct((T // NSA_CMP_STRIDE, half), F32)] * 2)
    operands = (w_p, kv_table)
    if entry_ln is not None:
        in_specs = [const((1, D)), const((1, D))] + in_specs
        out_specs = [row(D)] + out_specs
        out_shape = [jax.ShapeDtypeStruct((T, D), F32)] + out_shape
        operands = tuple(v.reshape(1, D) for v in entry_ln) + operands
    return pl.pallas_call(
        functools.partial(_in_proj_kernel, entry_ln=entry_ln is not None), name="in_proj",
        grid=(T // tm,),
        in_specs=[row(D)] + in_specs,
        out_specs=out_specs,
        out_shape=out_shape,
        scratch_shapes=[pltpu.VMEM((tm, LANE), F32)],
        compiler_params=_cparams("parallel"),
    )(h2, *operands)


def _ssm_chunk(p_ref, cw_ref, cb_ref, dtb_ref, alog_ref, dskip_ref, ng_ref, o_ref, state_ref, ext_ref):
    L, H, P, N = SSM_CHUNK, SSM_HEADS, SSM_HEAD_DIM, SSM_STATE
    z = p_ref[:, 0:GROUP_W]
    ext_ref[SUBLANE:SUBLANE + L, :] = p_ref[:, GROUP_W:GROUP_W + SSM_XBC]
    ext = ext_ref[...]
    conv = cb_ref[...] + ext[SUBLANE:] * cw_ref[SSM_CONV - 1:SSM_CONV, :]
    for j in range(SSM_CONV - 1):
        shift = SSM_CONV - 1 - j
        conv = conv + pltpu.roll(ext, shift, 0)[SUBLANE:] * cw_ref[j:j + 1, :]
    ext_ref[0:SUBLANE, :] = ext_ref[L:L + SUBLANE, :]
    xbc = _silu(conv)
    xs = xbc[:, 0:GROUP_W]
    b_in = xbc[:, GROUP_W:GROUP_W + SSM_GROUPS * N]
    c_in = xbc[:, GROUP_W + SSM_GROUPS * N:]

    dt = _softplus(p_ref[:, GROUP_W + SSM_XBC:] + dtb_ref[...])
    a = dt * (-jnp.exp(alog_ref[...]))
    row = _iota((L, L), 0)
    col = _iota((L, L), 1)
    tril = col <= row
    cs = _mm_f32(jnp.where(tril, 1.0, 0.0), a, exact="a")
    cs_t = cs.T
    ecs = jnp.exp(cs)
    dte = jnp.exp(cs[L - 1:L, :] - cs)
    first_head_of_pair = _iota((L, LANE), 1) < P

    def expand(x):
        pairs = [jnp.where(first_head_of_pair, x[:, h:h + 1], x[:, h + 1:h + 2]) for h in range(0, H, LANE // P)]
        return jnp.concatenate(pairs, axis=1)

    assert LANE == 2 * P
    dt_x = expand(dt)
    ecs_x = expand(ecs)
    dte_x = expand(dte)

    xdt = xs * dt_x
    wx = xdt * dte_x
    head_of_lane = _iota((L, H * P), 1) // P
    y = xs * dskip_ref[...]
    y_off = []
    rep = H // SSM_GROUPS
    for g in range(SSM_GROUPS):
        cg = c_in[:, g * N:(g + 1) * N]
        bg = b_in[:, g * N:(g + 1) * N]
        cb = _mm_nt(cg, bg)
        for h in range(g * rep, (g + 1) * rep):
            diff = cs[:, h:h + 1] - cs_t[h:h + 1, :]
            seg = jnp.where(tril, jnp.exp(jnp.where(tril, diff, 0.0)), 0.0)
            yh = _mm(cb * seg, xdt)
            y = y + jnp.where(head_of_lane == h, yh, 0.0)
        lanes = slice(g * rep * P, (g + 1) * rep * P)
        st_prev = state_ref[:, lanes]
        y_off.append(_mm(cg, st_prev))
        state_ref[:, lanes] = st_prev * ecs_x[L - 1:L, lanes] + _mm(bg.T, wx[:, lanes])
    y = y + jnp.concatenate(y_off, axis=1) * ecs_x
    y = y * _silu(z)
    ms = jnp.mean(y * y, axis=-1, keepdims=True)
    o_ref[...] = (y * lax.rsqrt(ms + EPS) * ng_ref[...]).astype(o_ref.dtype)


def _recurrent_kernel(ps_ref, cw_ref, cb_ref, dtb_ref, alog_ref, dskip_ref, ng_ref,
                      pr_ref, cos_ref, sin_ref, dec_ref, zeta_ref, xi_ref, cd_ref,
                      oa_ref, oc_ref, sstate_ref, ext_ref, rstate_ref):
    @pl.when(pl.program_id(0) == 0)
    def _():
        sstate_ref[...] = jnp.zeros_like(sstate_ref)
        rstate_ref[...] = jnp.zeros_like(rstate_ref)
        ext_ref[:, 0:SUBLANE, :] = jnp.zeros((ext_ref.shape[0], SUBLANE, SSM_XBC), F32)

    L = SSM_CHUNK
    for sub in range(ps_ref.shape[1] // L):
        rows = pl.ds(sub * L, L)
        for b in range(ps_ref.shape[0]):
            _ssm_chunk(ps_ref.at[b, rows], cw_ref, cb_ref, dtb_ref, alog_ref, dskip_ref, ng_ref,
                       oa_ref.at[b, rows], sstate_ref.at[b], ext_ref.at[b])
            _ret_chunk(pr_ref.at[b, rows], cos_ref.at[rows], sin_ref.at[rows], dec_ref, zeta_ref, xi_ref, cd_ref,
                       oc_ref.at[b, rows], rstate_ref.at[b])


def _ret_chunk(p_ref, cos_ref, sin_ref, dec_ref, zeta_ref, xi_ref, cd_ref, o_ref, state_ref):
    L, H, DK, DV = RET_CHUNK, RET_HEADS, RET_DK, RET_DV
    W = H * DK
    q = p_ref[:, 0:W]
    k = p_ref[:, W:2 * W]
    v = p_ref[:, 2 * W:3 * W]
    gate = p_ref[:, 3 * W:4 * W]
    lane = _iota((L, W), 1)
    first_half = (lane % DK) < (DK // 2)
    head_of_lane = lane // DK

    def rope(x):
        partner = jnp.where(first_half, pltpu.roll(x, W - DK // 2, 1), pltpu.roll(x, DK // 2, 1))
        return x * cos_ref[...] + partner * sin_ref[...]

    qr = rope(q)
    kr = rope(k) * (DK ** -0.5)
    y = jnp.zeros((L, H * DV), F32)
    for h in range(H):
        qh = jnp.where(head_of_lane == h, qr, 0.0)
        sc = _mm_nt(qh, kr) * dec_ref[h]
        y = y + jnp.where(head_of_lane == h, _mm(sc, v), 0.0)
    st = state_ref[...]
    y = y + _mm(qr * xi_ref[...], st)
    same_head = (_iota((W, H * DV), 0) // DK) == (_iota((W, H * DV), 1) // DV)
    kv = _mm((kr * zeta_ref[...]).T, v)
    state_ref[...] = st * cd_ref[...] + jnp.where(same_head, kv, 0.0)
    assert DV & (DV - 1) == 0
    ms = _mm_f32(y * y, jnp.where(same_head, 1.0 / DV, 0.0), exact="b")
    o_ref[...] = (y * lax.rsqrt(ms + EPS) * _silu(gate)).astype(o_ref.dtype)


def _ret_tables(S):
    H, DK, L = RET_HEADS, RET_DK, RET_CHUNK
    inv = ROPE_THETA ** (-np.arange(0, DK, 2, dtype=np.float64) / DK)
    ang = np.arange(S, dtype=np.float64)[:, None] * inv[None, :]
    cos, sin = np.cos(ang), np.sin(ang)
    cos_t = np.tile(np.concatenate([cos, cos], axis=1), (1, H))
    sin_t = np.tile(np.concatenate([-sin, sin], axis=1), (1, H))
    log_gamma = np.log1p(-np.exp2(-5.0 - np.arange(H, dtype=np.float64)))
    pos = np.arange(L, dtype=np.float64)
    diff = pos[:, None] - pos[None, :]
    decay_in = np.where(diff >= 0, np.exp(np.maximum(diff, 0.0)[None] * log_gamma[:, None, None]), 0.0)
    zeta = np.exp((L - 1 - pos)[None] * log_gamma[:, None])
    xi = np.exp((pos + 1.0)[None] * log_gamma[:, None])
    chunk_decay = np.exp(L * log_gamma)
    zeta_x = np.repeat(zeta.T, DK, axis=1)
    xi_x = np.repeat(xi.T, DK, axis=1)
    cd_x = np.repeat(chunk_decay, RET_DV).reshape(1, H * RET_DV)
    return tuple(jnp.asarray(t, dtype=F32) for t in (cos_t, sin_t, decay_in, zeta_x, xi_x, cd_x))


def _ssm_and_retention(p_ssm, conv_w, conv_b, dt_bias, a_log, d_skip, norm_g, p_ret, tables):
    B, S, _ = p_ssm.shape
    L, H = SSM_CHUNK, RET_HEADS
    assert RET_CHUNK == L
    W = H * RET_DK
    cos_t, sin_t, decay_in, zeta_x, xi_x, cd_x = tables
    pad_h = lambda v: jnp.pad(v, (0, LANE - SSM_HEADS)).reshape(1, LANE)
    const = lambda shape: pl.BlockSpec(shape, lambda c: (0,) * len(shape))
    rows = RECURRENT_CHUNKS_PER_STEP * L
    assert S % rows == 0
    chunk = lambda width: pl.BlockSpec((B, rows, width), lambda c: (0, c, 0))
    out_shape = jax.ShapeDtypeStruct((B, S, GROUP_W), BF16)
    return pl.pallas_call(
        _recurrent_kernel, name="ssm_retention",
        grid=(S // rows,),
        in_specs=[chunk(W_SSM),
                  const((SSM_CONV, SSM_XBC)), const((1, SSM_XBC)), const((1, LANE)), const((1, LANE)),
                  const((1, GROUP_W)), const((1, GROUP_W)),
                  chunk(W_RET),
                  pl.BlockSpec((rows, W), lambda c: (c, 0)), pl.BlockSpec((rows, W), lambda c: (c, 0)),
                  const((H, L, L)), const((L, W)), const((L, W)), const((1, H * RET_DV))],
        out_specs=[chunk(GROUP_W), chunk(GROUP_W)],
        out_shape=[out_shape, out_shape],
        scratch_shapes=[pltpu.VMEM((B, SSM_STATE, GROUP_W), F32),
                        pltpu.VMEM((B, L + SUBLANE, SSM_XBC), F32),
                        pltpu.VMEM((B, W, H * RET_DV), F32)],
        compiler_params=_cparams("arbitrary"),
    )(p_ssm, conv_w, conv_b.reshape(1, -1), pad_h(dt_bias), pad_h(a_log),
      jnp.repeat(d_skip, SSM_HEAD_DIM).reshape(1, GROUP_W), norm_g.reshape(1, GROUP_W),
      p_ret, cos_t, sin_t, decay_in, zeta_x, xi_x, cd_x)


def _mla_prep_kernel(p_ref, qg_ref, wq_ref, kvg_ref, wk_ref, wv_ref, cos_ref, sin_ref,
                     q_ref, k_ref, v_ref):
    tm = p_ref.shape[0]
    cq = p_ref[:, 0:MLA_Q_RANK]
    ckv = p_ref[:, MLA_Q_RANK:MLA_Q_RANK + MLA_KV_RANK]
    kr = p_ref[:, MLA_Q_RANK + MLA_KV_RANK:]

    def rms(x, g):
        return x * lax.rsqrt(jnp.mean(x * x, axis=-1, keepdims=True) + EPS) * g

    q_both = _mm(rms(cq, qg_ref[...]), wq_ref[...])
    q = q_both[:, :MLA_HEADS * LANE]
    q_partner = q_both[:, MLA_HEADS * LANE:]
    kvl = rms(ckv, kvg_ref[...])
    kn = _mm(kvl, wk_ref[...])
    vv = _mm(kvl, wv_ref[...])
    kr_sh = pltpu.roll(kr, MLA_NOPE, 1)
    lane = _iota((tm, LANE), 1)
    half = MLA_ROPE // 2
    low = (lane >= MLA_NOPE) & (lane < MLA_NOPE + half)
    cos = cos_ref[...]
    sin = sin_ref[...]

    def rope(x):
        partner = jnp.where(low, pltpu.roll(x, LANE - half, 1), pltpu.roll(x, half, 1))
        return x * cos + partner * sin

    scale = (MLA_NOPE + MLA_ROPE) ** -0.5 * LOG2_E
    k_pe = rope(kr_sh)
    for h in range(MLA_HEADS):
        sl = slice(h * LANE, (h + 1) * LANE)
        q_ref[h] = ((q[:, sl] * cos + q_partner[:, sl] * sin) * scale).astype(BF16)
        k_ref[h] = (kn[:, sl] + k_pe).astype(BF16)
        v_ref[h] = jnp.where(lane == _mla_ones_lane(h), 1.0, vv[:, sl]).astype(BF16)


def _mla_tables(S):
    inv = ROPE_THETA ** (-np.arange(0, MLA_ROPE, 2, dtype=np.float64) / MLA_ROPE)
    ang = np.arange(S, dtype=np.float64)[:, None] * inv[None, :]
    cos, sin = np.cos(ang), np.sin(ang)
    tail = LANE - MLA_NOPE - MLA_ROPE
    cos_t = np.concatenate([np.ones((S, MLA_NOPE)), cos, cos, np.ones((S, tail))], axis=1)
    sin_t = np.concatenate([np.zeros((S, MLA_NOPE)), -sin, sin, np.zeros((S, tail))], axis=1)
    return jnp.asarray(cos_t, dtype=F32), jnp.asarray(sin_t, dtype=F32)


def _layout_mla_weights(w_uq, w_ukv):
    H = MLA_HEADS
    dq = MLA_NOPE + MLA_ROPE
    wq = jnp.concatenate([_pad_cols(w_uq[:, h * dq:(h + 1) * dq], LANE) for h in range(H)], axis=1)
    half = MLA_ROPE // 2
    partner = []
    for h in range(H):
        pe = w_uq[:, h * dq + MLA_NOPE:(h + 1) * dq]
        swapped = jnp.concatenate([jnp.zeros_like(w_uq[:, :MLA_NOPE]), pe[:, half:], pe[:, :half]], axis=1)
        partner.append(_pad_cols(swapped, LANE))
    wq = jnp.concatenate([wq] + partner, axis=1)
    dkv = MLA_NOPE + MLA_V
    wk, wv = [], []
    for h in range(H):
        blk = w_ukv[:, h * dkv:(h + 1) * dkv]
        wk.append(_pad_cols(blk[:, :MLA_NOPE], LANE))
        v = blk[:, MLA_NOPE:]
        zero = jnp.zeros_like(v)
        wv.append(jnp.concatenate([v, zero] if h % 2 == 0 else [zero, v], axis=1))
    return wq.astype(BF16), jnp.concatenate(wk, axis=1).astype(BF16), jnp.concatenate(wv, axis=1).astype(BF16)


def _mla_prep(p_mla, q_norm_g, kv_norm_g, wq, wk, wv, tables, tm=1024):
    B, S, _ = p_mla.shape
    H = MLA_HEADS
    cos_t, sin_t = tables
    const = lambda shape: pl.BlockSpec(shape, lambda b, i: (0,) * len(shape))
    qkv_spec = pl.BlockSpec((None, H, tm, LANE), lambda b, i: (b, 0, i, 0))
    qkv_shape = jax.ShapeDtypeStruct((B, H, S, LANE), BF16)
    return pl.pallas_call(
        _mla_prep_kernel, name="mla_prep",
        grid=(B, S // tm),
        in_specs=[pl.BlockSpec((None, tm, W_MLA), lambda b, i: (b, i, 0)),
                  const((1, MLA_Q_RANK)), const((MLA_Q_RANK, 2 * H * LANE)),
                  const((1, MLA_KV_RANK)), const((MLA_KV_RANK, H * LANE)), const((MLA_KV_RANK, H * LANE)),
                  pl.BlockSpec((tm, LANE), lambda b, i: (i, 0)),
                  pl.BlockSpec((tm, LANE), lambda b, i: (i, 0))],
        out_specs=[qkv_spec, qkv_spec, qkv_spec],
        out_shape=[qkv_shape, qkv_shape, qkv_shape],
        compiler_params=_cparams("parallel", "parallel"),
    )(p_mla, q_norm_g.reshape(1, -1), wq, kv_norm_g.reshape(1, -1), wk, wv, cos_t, sin_t)


def _mla_ones_lane(h):
    return MLA_V if h % 2 == 0 else 0


def _mla_attn_kernel(qi_ref, kj_ref, q_ref, k_ref, v_ref, o_ref, m_ref, acc_ref):
    H = MLA_HEADS
    tq, tk = q_ref.shape[1], k_ref.shape[1]
    i = qi_ref[pl.program_id(1)]
    j = kj_ref[pl.program_id(1)]

    @pl.when(j == 0)
    def _():
        m_ref[...] = jnp.full(m_ref.shape, NEG_INF, F32)
        acc_ref[...] = jnp.zeros_like(acc_ref)

    def sweep(blocks):
        nt_dims = (((1,), (1,)), ((), ()))
        scores = [[lax.dot_general(q_ref[h, r0:r0 + nr, :], k_ref[h, 0:nk, :], nt_dims, preferred_element_type=F32)
                   for (r0, nr, nk, _) in blocks] for h in range(H)]
        for h in range(H):
            for (r0, nr, nk, offset), s in zip(blocks, scores[h]):
                if offset is not None:
                    s = jnp.where(_iota((nr, nk), 1) - _iota((nr, nk), 0) <= offset, s, NEG_INF)
                rows = slice(r0, r0 + nr)
                m_prev = m_ref[h, rows]
                m_new = jnp.maximum(m_prev, jnp.max(s, axis=-1, keepdims=True))
                p = jnp.exp2(s - jnp.tile(m_new, (1, nk // LANE)))
                acc_ref[h, rows] = (jnp.exp2(m_prev - m_new) * acc_ref[h, rows]
                                    + jnp.dot(p.astype(BF16), v_ref[h, 0:nk, :], preferred_element_type=F32))
                m_ref[h, rows] = m_new

    assert tq == tk
    half = tq // 2

    @pl.when(j < i)
    def _():
        sweep([(0, tq, tk, None)])

    @pl.when(j == i)
    def _():
        sweep([(0, half, half, 0), (half, half, tk, half)])
        lane = _iota((tq, LANE), 1)
        for pair in range(H // 2):
            he, ho = 2 * pair, 2 * pair + 1
            acc_e, acc_o = acc_ref[he], acc_ref[ho]
            le = acc_e[:, _mla_ones_lane(he):_mla_ones_lane(he) + 1]
            lo = acc_o[:, _mla_ones_lane(ho):_mla_ones_lane(ho) + 1]
            o_ref[:, pair * LANE:(pair + 1) * LANE] = jnp.where(lane < MLA_V, acc_e / le, acc_o / lo).astype(o_ref.dtype)


def _mla_attn(q, k, v):
    B, H, S, _ = q.shape
    t = min(MLA_TILE, S)
    tq = t
    pairs = [(i, j) for i in range(S // tq) for j in range((i + 1) * tq // t)]
    qi = jnp.asarray([p[0] for p in pairs], jnp.int32)
    kj = jnp.asarray([p[1] for p in pairs], jnp.int32)
    grid_spec = pltpu.PrefetchScalarGridSpec(
        num_scalar_prefetch=2,
        grid=(B, len(pairs)),
        in_specs=[pl.BlockSpec((None, H, tq, LANE), lambda b, p, qi, kj: (b, 0, qi[p], 0)),
                  pl.BlockSpec((None, H, t, LANE), lambda b, p, qi, kj: (b, 0, kj[p], 0)),
                  pl.BlockSpec((None, H, t, LANE), lambda b, p, qi, kj: (b, 0, kj[p], 0))],
        out_specs=pl.BlockSpec((None, tq, GROUP_W), lambda b, p, qi, kj: (b, qi[p], 0)),
        scratch_shapes=[pltpu.VMEM((H, tq, LANE), F32), pltpu.VMEM((H, tq, LANE), F32)],
    )
    return pl.pallas_call(
        _mla_attn_kernel, name="mla_attn",
        grid_spec=grid_spec,
        out_shape=jax.ShapeDtypeStruct((B, S, GROUP_W), BF16),
        compiler_params=_cparams("parallel", "arbitrary"),
    )(qi, kj, q, k, v)


POS_HI = NSA_DK
POS_LO = NSA_DK + 3
POS_ONE = NSA_DK + 6
ONES_LANE = NSA_DV


def _split_bf16(x, parts=3):
    out, rem = [], np.float64(x)
    for _ in range(parts):
        piece = np.float64(np.float32(rem).astype(jnp.bfloat16).astype(np.float32))
        out.append(float(piece))
        rem = rem - piece
    return out


def _nsa_query_table():
    H = NSA_HEADS
    tab = np.zeros((2 * H, LANE), np.float32)
    for h in range(H):
        c = 2.0 ** (-8.0 * (h + 1) / H) * LOG2_E
        pieces = _split_bf16(c)
        tab[h, POS_HI:POS_HI + 3] = pieces
        tab[h, POS_LO:POS_LO + 3] = pieces
        tab[H + h, POS_ONE] = -sum(pieces)
    return jnp.asarray(tab)


def _nsa_pos_lanes(pos, lo_offset=0.0):
    t = np.zeros((len(pos), LANE - NSA_DK), np.float32)
    t[:, POS_HI - NSA_DK:POS_HI - NSA_DK + 3] = (NSA_SLC_LEN * (pos // NSA_SLC_LEN))[:, None]
    t[:, POS_LO - NSA_DK:POS_LO - NSA_DK + 3] = (pos % NSA_SLC_LEN + lo_offset)[:, None]
    t[:, POS_ONE - NSA_DK] = 1.0
    return t


def _nsa_queries(q_ref, qtab_ref, qb):
    Q, H = q_ref.shape[0], NSA_HEADS
    qpos = (qb * Q + _iota((Q, 1), 0)).astype(F32)
    out = []
    for h in range(H):
        q = q_ref[:, h * LANE:(h + 1) * LANE] * (NSA_DK ** -0.5 * LOG2_E)
        out.append((q + qtab_ref[h:h + 1, :] + qtab_ref[H + h:H + h + 1, :] * qpos).astype(BF16))
    return out


def _normalise(o):
    return o / o[:, ONES_LANE:ONES_LANE + 1]


def _stacked_gate(gates, branch):
    lanes = [3 * h + branch for h in range(NSA_HEADS)]
    return jnp.concatenate([gates[:, c:c + 1] for c in lanes], axis=0)


def _unstack_heads(o):
    Q = o.shape[0] // NSA_HEADS
    lane = _iota((Q, LANE), 1)
    out = []
    for pair in range(NSA_HEADS // 2):
        even = o[(2 * pair) * Q:(2 * pair + 1) * Q]
        odd = o[(2 * pair + 1) * Q:(2 * pair + 2) * Q]
        out.append(jnp.where(lane < NSA_DV, even, pltpu.roll(odd, NSA_DV, 1)))
    return jnp.concatenate(out, axis=1)


def _nsa_cmp_kernel(uk_ref, uv_ref, pek_ref, pev_ref, w1k_ref, w1v_ref, w2k_ref, w2v_ref, cpos_ref,
                    kc_ref, vc_ref, sh_ref):
    nb = uk_ref.shape[0]
    half = uk_ref.shape[1]

    def hidden(u_ref, pe_ref, w1_ref):
        u = u_ref[...]
        first = _mm(u + pe_ref[0:1, :], w1_ref[0:half, :])
        second = _mm(u + pe_ref[1:2, :], w1_ref[half:2 * half, :])
        sh_ref[0:nb, :] = second
        sh_ref[nb:nb + SUBLANE, :] = jnp.zeros((SUBLANE, NSA_CMP_HID), F32)
        return first + sh_ref[pl.ds(1, nb), :]

    hk = _silu(hidden(uk_ref, pek_ref, w1k_ref))
    hv = _silu(hidden(uv_ref, pev_ref, w1v_ref))
    kc_ref[...] = (_mm(hk, w2k_ref[...]) + cpos_ref[...]).astype(BF16)
    ones_lane = jnp.where(_iota((1, LANE), 1) == ONES_LANE, 1.0, 0.0)
    vc_ref[...] = (_mm(hv, w2v_ref[...]) + ones_lane).astype(BF16)


def _nsa_compress(uk, uv, pe_k, w1_k, w2_k, pe_v, w1_v, w2_v):
    B, nb, half = uk.shape
    hid = NSA_CMP_HID
    const = lambda shape: pl.BlockSpec(shape, lambda b: (0,) * len(shape))
    w2k = _pad_cols(w2_k, LANE).astype(BF16)
    w2v = _pad_cols(w2_v, LANE).astype(BF16)
    centre = _nsa_pos_lanes(np.arange(nb) * NSA_CMP_STRIDE, 0.5 * (NSA_CMP_LEN - 1))
    cpos = jnp.asarray(np.concatenate([np.zeros((nb, NSA_DK), np.float32), centre], axis=1))
    out_spec = pl.BlockSpec((None, nb, LANE), lambda b: (b, 0, 0))
    out_shape = jax.ShapeDtypeStruct((B, nb, LANE), BF16)
    return pl.pallas_call(
        _nsa_cmp_kernel, name="nsa_compress",
        grid=(B,),
        in_specs=[pl.BlockSpec((None, nb, half), lambda b: (b, 0, 0)),
                  pl.BlockSpec((None, nb, half), lambda b: (b, 0, 0)),
                  const((2, half)), const((2, half)),
                  const((2 * half, hid)), const((2 * half, hid)),
                  const((hid, LANE)), const((hid, LANE)), const((nb, LANE))],
        out_specs=[out_spec, out_spec],
        out_shape=[out_shape, out_shape],
        scratch_shapes=[pltpu.VMEM((nb + SUBLANE, hid), F32)],
        compiler_params=_cparams("parallel"),
    )(uk, uv, pe_k.reshape(2, half), pe_v.reshape(2, half), w1_k.astype(BF16), w1_v.astype(BF16), w2k, w2v, cpos)


def _nsa_sel_kernel(q_ref, qtab_ref, gate_ref, kc_ref, vc_ref, ovt_ref, oc_ref, selb_ref, any_ref, *, n_slc, top_n):
    Q, H = q_ref.shape[0], NSA_HEADS
    qb = pl.program_id(1)
    nc = kc_ref.shape[0]
    pick = functools.partial(_nsa_pick_blocks, qb=qb, selb_ref=selb_ref, any_ref=any_ref, n_slc=n_slc, top_n=top_n)
    qs = jnp.concatenate(_nsa_queries(q_ref, qtab_ref, qb), axis=0)
    nt_dims = (((1,), (1,)), ((), ()))

    def attend(ncols):
        s = lax.dot_general(qs, kc_ref[0:ncols, :], nt_dims, preferred_element_type=F32)
        qpos = qb * Q + (_iota((H * Q, ncols), 0) & (Q - 1))
        block_end = _iota((H * Q, ncols), 1) * NSA_CMP_STRIDE + (NSA_CMP_LEN - 1)
        s = jnp.where(block_end <= qpos, s, NEG_INF)
        e = jnp.exp2(s - jnp.max(s, axis=-1, keepdims=True))
        qpos_col = qb * Q + (_iota((H * Q, 1), 0) & (Q - 1))
        has_block = jnp.where(qpos_col >= NSA_CMP_LEN - 1, 1.0, 0.0)
        p = e * (has_block / jnp.sum(e, axis=-1, keepdims=True))
        o_c = jnp.dot(p.astype(BF16), vc_ref[0:ncols, :], preferred_element_type=F32)
        oc_ref[...] = _unstack_heads(_stacked_gate(jax.nn.sigmoid(gate_ref[...]), 0) * o_c)
        p_sum = p[0:Q]
        for h in range(1, H):
            p_sum = p_sum + p[h * Q:(h + 1) * Q]
        pick(_mm_f32(ovt_ref[:, 0:ncols], p_sum, exact="a", dims=nt_dims))

    tiles_needed = ((qb + 1) * Q // NSA_CMP_STRIDE + LANE - 1) // LANE
    for tiles in range(1, nc // LANE + 1):
        pl.when(tiles_needed == tiles)(functools.partial(attend, tiles * LANE))


def _nsa_pick_blocks(imp, qb, selb_ref, any_ref, n_slc, top_n):
    Q = imp.shape[1]
    blk = _iota((LANE, Q), 0)
    q_blk = (qb * Q + _iota((LANE, Q), 1)) >> int(math.log2(NSA_SLC_LEN))
    causal = blk <= q_blk
    for forced_blk in (0, q_blk, q_blk - 1):
        imp = jnp.where(blk == forced_blk, FORCED_SCORE, imp)
    imp = jnp.where(causal, imp, -1.0)
    imp = jnp.where(blk < n_slc, imp, -2.0)
    blk_f = blk.astype(F32)
    sel = jnp.zeros((LANE, Q), F32)
    for _ in range(top_n):
        m = jnp.max(imp, axis=0, keepdims=True)
        first = jnp.min(jnp.where(imp == m, blk_f, float(LANE)), axis=0, keepdims=True)
        hit = blk_f == first
        sel = jnp.where(hit, 1.0, sel)
        imp = jnp.where(hit, -3.0, imp)
    sel = jnp.where(causal, sel, 0.0).T
    selb_ref[...] = jnp.where(sel > 0.5, 0.0, NEG_INF).astype(BF16)
    any_ref[...] = jnp.max(sel, axis=0, keepdims=True)


def _nsa_select(p_nsa, qtab, kc, vc, overlap_t):
    B, S, _ = p_nsa.shape
    Q = NSA_Q
    nqb = S // Q
    nc = kc.shape[1]
    n_slc = S // NSA_SLC_LEN
    kern = functools.partial(_nsa_sel_kernel, n_slc=n_slc, top_n=min(NSA_TOPN, n_slc))
    return pl.pallas_call(
        kern, name="nsa_select",
        grid=(B, nqb),
        in_specs=[pl.BlockSpec((None, Q, NSA_HEADS * LANE), lambda b, i: (b, i, 0)),
                  pl.BlockSpec((2 * NSA_HEADS, LANE), lambda b, i: (0, 0)),
                  pl.BlockSpec((None, Q, LANE), lambda b, i: (b, i, (W_NSA - LANE) // LANE)),
                  pl.BlockSpec((None, nc, LANE), lambda b, i: (b, 0, 0)),
                  pl.BlockSpec((None, nc, LANE), lambda b, i: (b, 0, 0)),
                  pl.BlockSpec((LANE, nc), lambda b, i: (0, 0))],
        out_specs=[pl.BlockSpec((None, Q, GROUP_W), lambda b, i: (b, i, 0)),
                   pl.BlockSpec((None, Q, LANE), lambda b, i: (b, i, 0)),
                   pl.BlockSpec((None, None, 1, LANE), lambda b, i: (b, i, 0, 0))],
        out_shape=[jax.ShapeDtypeStruct((B, S, GROUP_W), F32),
                   jax.ShapeDtypeStruct((B, S, LANE), BF16),
                   jax.ShapeDtypeStruct((B, nqb, 1, LANE), F32)],
        compiler_params=_cparams("parallel", "parallel"),
    )(p_nsa, qtab, p_nsa, kc, vc, overlap_t)


def _nsa_attn_kernel(flags_ref, q_ref, qtab_ref, gate_ref, oc_ref, selb_ref, ks_ref, vs_ref, kw_ref, vw_ref,
                     o_ref, m_ref, acc_ref, ow_ref, *, nt):
    Q, H, TK = q_ref.shape[0], NSA_HEADS, NSA_TILE
    b = pl.program_id(0)
    qb = pl.program_id(1)
    nqb = pl.num_programs(1)
    qh = _nsa_queries(q_ref, qtab_ref, qb)
    nt_dims = (((1,), (1,)), ((), ()))

    selb = selb_ref[...]
    qs_sel = jnp.concatenate([jnp.concatenate([q, selb], axis=1) for q in qh], axis=0)
    sink_off = jnp.where(_iota(selb.shape, 1) < NSA_SINK // NSA_SLC_LEN, NEG_INF, selb.astype(F32)).astype(BF16)
    qs_loop = jnp.concatenate([jnp.concatenate([q, sink_off], axis=1) for q in qh], axis=0)
    m_ref[...] = jnp.full(m_ref.shape, NEG_INF, F32)
    acc_ref[...] = jnp.zeros_like(acc_ref)

    def online_update(m_prev, acc_prev, blocks):
        m_new = m_prev
        for s, _ in blocks:
            m_new = jnp.maximum(m_new, jnp.max(s, axis=-1, keepdims=True))
        acc = jnp.exp2(m_prev - m_new) * acc_prev
        for s, v in blocks:
            p = jnp.exp2(s - jnp.tile(m_new, (1, s.shape[1] // LANE)))
            acc = acc + jnp.dot(p.astype(BF16), v, preferred_element_type=F32)
        return m_new, acc

    def tile(t, carry):
        @pl.when(flags_ref[(b * nqb + qb) * nt + t] > 0)
        def _():
            rows = pl.ds(pl.multiple_of(t * TK, TK), TK)
            s = lax.dot_general(qs_loop, ks_ref[rows, :], nt_dims, preferred_element_type=F32)
            m_ref[...], acc_ref[...] = online_update(m_ref[...], acc_ref[...], [(s, vs_ref[rows, :])])
        return carry

    assert Q == TK == NSA_WIN
    half = Q // 2
    lax.fori_loop(0, qb, tile, 0)

    k_sink, v_sink = ks_ref[0:NSA_SINK, :], vs_ref[0:NSA_SINK, :]
    sink_bias = jnp.where(qb > 0, 0.0, NEG_INF)
    lower = _iota((H * half, half), 1) <= (_iota((H * half, half), 0) & (half - 1))
    for part in range(2):
        row_slices = [slice(h * Q + part * half, h * Q + (part + 1) * half) for h in range(H)]
        q_rows = jnp.concatenate([qs_sel[r] for r in row_slices], axis=0)
        keys = pl.ds(pl.multiple_of(qb * TK, TK), (part + 1) * half)
        s = lax.dot_general(q_rows, ks_ref[keys, :], nt_dims, preferred_element_type=F32)
        own = jnp.where(lower, s[:, part * half:], NEG_INF)
        s = own if part == 0 else jnp.concatenate([s[:, :half], own], axis=1)
        s_sink = lax.dot_general(q_rows, k_sink, nt_dims, preferred_element_type=F32) + sink_bias
        m_new, acc_new = online_update(jnp.concatenate([m_ref[r] for r in row_slices], axis=0),
                                       jnp.concatenate([acc_ref[r] for r in row_slices], axis=0),
                                       [(s, vs_ref[keys, :]), (s_sink, v_sink)])
        for h, r in enumerate(row_slices):
            m_ref[r] = m_new[h * half:(h + 1) * half]
            acc_ref[r] = acc_new[h * half:(h + 1) * half]


    def band(q_rows, first_key, n_keys, masks):
        keys = pl.ds(pl.multiple_of(first_key, half), n_keys)
        s = lax.dot_general(q_rows, kw_ref[keys, :], nt_dims, preferred_element_type=F32)
        groups = [s[:, g * half:(g + 1) * half] for g in range(n_keys // half)]
        s = jnp.concatenate([g if m is None else jnp.where(m, g, NEG_INF) for g, m in zip(groups, masks)], axis=1)
        p = jnp.exp2(s - jnp.max(s, axis=-1, keepdims=True))
        return _normalise(jnp.dot(p.astype(BF16), vw_ref[keys, :], preferred_element_type=F32))

    @pl.when(qb == 0)
    def _():
        qs_win = jnp.concatenate(qh, axis=0)
        row = _iota((H * Q, half), 0) & (Q - 1)
        col = _iota((H * Q, half), 1)
        ow_ref[...] = band(qs_win, 0, Q, [col <= row, col + half <= row])

    @pl.when(qb > 0)
    def _():
        row = _iota((H * half, half), 0) & (half - 1)
        col = _iota((H * half, half), 1)
        masks = [col > row, None, col <= row]
        for part in range(2):
            q_rows = jnp.concatenate([q[part * half:(part + 1) * half] for q in qh], axis=0)
            o_part = band(q_rows, (qb - 1) * Q + part * half, NSA_WIN + half, masks)
            for h in range(H):
                ow_ref[h * Q + part * half:h * Q + (part + 1) * half, :] = o_part[h * half:(h + 1) * half]

    gates = jax.nn.sigmoid(gate_ref[...])
    mixed = _stacked_gate(gates, 1) * _normalise(acc_ref[...]) + _stacked_gate(gates, 2) * ow_ref[...]
    o_ref[...] = (oc_ref[...] + _unstack_heads(mixed)).astype(o_ref.dtype)


def _nsa_attend(p_nsa, qtab, o_c, selb, flags, kv):
    B, S, _ = p_nsa.shape
    Q = NSA_Q
    nqb = S // Q
    nt = S // NSA_TILE
    gate_blk = (W_NSA - LANE) // LANE
    kern = functools.partial(_nsa_attn_kernel, nt=nt)
    slab = lambda width, col: pl.BlockSpec((None, S, width), lambda b, i, f: (b, 0, col))
    grid_spec = pltpu.PrefetchScalarGridSpec(
        num_scalar_prefetch=1,
        grid=(B, nqb),
        in_specs=[pl.BlockSpec((None, Q, NSA_HEADS * LANE), lambda b, i, f: (b, i, 0)),
                  pl.BlockSpec((2 * NSA_HEADS, LANE), lambda b, i, f: (0, 0)),
                  pl.BlockSpec((None, Q, LANE), lambda b, i, f: (b, i, gate_blk)),
                  pl.BlockSpec((None, Q, GROUP_W), lambda b, i, f: (b, i, 0)),
                  pl.BlockSpec((None, Q, LANE), lambda b, i, f: (b, i, 0)),
                  slab(2 * LANE, 0), slab(LANE, 2), slab(LANE, 3), slab(LANE, 4)],
        out_specs=pl.BlockSpec((None, Q, GROUP_W), lambda b, i, f: (b, i, 0)),
        scratch_shapes=[pltpu.VMEM((NSA_HEADS * Q, LANE), F32)] * 3,
    )
    return pl.pallas_call(
        kern, name="nsa_attend",
        grid_spec=grid_spec,
        out_shape=jax.ShapeDtypeStruct((B, S, GROUP_W), BF16),
        compiler_params=_cparams("parallel", "parallel"),
    )(flags, p_nsa, qtab, p_nsa, o_c, selb, kv, kv, kv, kv)


def _nsa_tables(S):
    nc = S // NSA_CMP_STRIDE
    n = np.arange(nc)[None, :]
    j = np.arange(LANE)[:, None]
    start = n * NSA_CMP_STRIDE
    ov = (start < (j + 1) * NSA_SLC_LEN) & (start + NSA_CMP_LEN - 1 >= j * NSA_SLC_LEN)
    ov &= (n < (S - NSA_CMP_LEN) // NSA_CMP_STRIDE + 1) & (j < S // NSA_SLC_LEN)
    pos = np.arange(S)
    k_zero = np.zeros((S, NSA_DK), np.float32)
    block_onehot = (pos[:, None] // NSA_SLC_LEN == np.arange(LANE)[None, :]).astype(np.float32)
    v_lanes = np.zeros((S, LANE), np.float32)
    v_lanes[:, ONES_LANE] = 1.0
    kv_table = np.concatenate([k_zero, _nsa_pos_lanes(pos), block_onehot, v_lanes,
                               k_zero, _nsa_pos_lanes(pos), v_lanes], axis=1)
    assert kv_table.shape[1] == W_KV
    return _nsa_query_table(), jnp.asarray(ov.astype(np.float32)), jnp.asarray(kv_table, dtype=BF16)


def _nsa(p_nsa, kv, uk, uv, pe_k, w1_k, w2_k, pe_v, w1_v, w2_v, tables):
    B, S, _ = p_nsa.shape
    qtab, overlap_t, _ = tables
    kc, vc = _nsa_compress(uk, uv, pe_k, w1_k, w2_k, pe_v, w1_v, w2_v)
    o_c, selb, blk_any = _nsa_select(p_nsa, qtab, kc, vc, overlap_t)
    per_tile = NSA_TILE // NSA_SLC_LEN
    nt = S // NSA_TILE
    not_sink = (np.arange(nt * per_tile) >= NSA_SINK // NSA_SLC_LEN).astype(np.float32)
    blk_any = blk_any[:, :, 0, :nt * per_tile] * not_sink
    flags = blk_any.reshape(B, S // NSA_Q, nt, per_tile).max(axis=-1)
    flags = (flags > 0).astype(jnp.int32).reshape(-1)
    return _nsa_attend(p_nsa, qtab, o_c, selb, flags, kv)


def _out_proj_kernel(h_ref, ya_ref, yb_ref, yc_ref, yd_ref, w_ref, g_ref, b_ref, o_ref, wb_ref):
    @pl.when(pl.program_id(0) == 0)
    def _():
        wb_ref[...] = w_ref[...].astype(BF16)

    y = jnp.concatenate([ya_ref[...], yb_ref[...], yc_ref[...], yd_ref[...]], axis=1)
    mix = jnp.dot(y, wb_ref[...], preferred_element_type=F32)
    o_ref[...] = _layer_norm(DEEPNORM_ALPHA * h_ref[...] + mix, g_ref[...], b_ref[...])


def _out_proj(h2, ys, w_out, layer, g, b, tm=1024):
    T, D = h2.shape
    row = lambda w: pl.BlockSpec((tm, w), lambda i: (i, 0))
    const = lambda shape: pl.BlockSpec(shape, lambda i: (0,) * len(shape))
    return pl.pallas_call(
        _out_proj_kernel, name="out_proj_ln",
        grid=(T // tm,),
        in_specs=[row(D), row(GROUP_W), row(GROUP_W), row(GROUP_W), row(GROUP_W),
                  pl.BlockSpec((None, D, D), lambda i: (layer, 0, 0)), const((1, D)), const((1, D))],
        out_specs=row(D),
        out_shape=jax.ShapeDtypeStruct((T, D), F32),
        scratch_shapes=[pltpu.VMEM((D, D), BF16)],
        compiler_params=_cparams("arbitrary"),
    )(h2, *ys, w_out, g.reshape(1, D), b.reshape(1, D))


def _mlp_kernel(h_ref, w1_ref, w2_ref, g_ref, b_ref, o_ref, acc_ref):
    f = pl.program_id(1)

    @pl.when(f == 0)
    def _():
        acc_ref[...] = jnp.zeros_like(acc_ref)

    hb = h_ref[...].astype(BF16)
    tf = w1_ref.shape[1]
    halves = [slice(i * tf // 2, (i + 1) * tf // 2) for i in range(2)]
    ups = [jnp.dot(hb, w1_ref[:, c].astype(BF16), preferred_element_type=F32) for c in halves]
    part = None
    for c, up in zip(halves, ups):
        a = jnp.maximum(up, 0.0)
        down = _mm(a * a, w2_ref[c, :])
        part = down if part is None else part + down
    acc_ref[...] += part

    @pl.when(f == pl.num_programs(1) - 1)
    def _():
        o_ref[...] = _layer_norm(DEEPNORM_ALPHA * h_ref[...] + acc_ref[...], g_ref[...], b_ref[...])


def _mlp(h2, w1, w2, layer, g, b, tm=1024, tf=1024):
    T, D = h2.shape
    F = w1.shape[2]
    return pl.pallas_call(
        _mlp_kernel, name="mlp_ln",
        grid=(T // tm, F // tf),
        in_specs=[pl.BlockSpec((tm, D), lambda i, f: (i, 0)),
                  pl.BlockSpec((None, D, tf), lambda i, f: (layer, 0, f)),
                  pl.BlockSpec((None, tf, D), lambda i, f: (layer, f, 0)),
                  pl.BlockSpec((1, D), lambda i, f: (0, 0)),
                  pl.BlockSpec((1, D), lambda i, f: (0, 0))],
        out_specs=pl.BlockSpec((tm, D), lambda i, f: (i, 0)),
        out_shape=jax.ShapeDtypeStruct((T, D), F32),
        scratch_shapes=[pltpu.VMEM((tm, D), F32)],
        compiler_params=_cparams("parallel", "arbitrary"),
    )(h2, w1, w2, g.reshape(1, D), b.reshape(1, D))


def kernel(x, ln_emb_g, ln_emb_b, w_in, conv_w, conv_b, dt_bias, a_log, d_skip, ssm_norm_g, q_norm_g, w_uq, kv_norm_g, w_ukv, cmp_pe_k, cmp_w1_k, cmp_w2_k, cmp_pe_v, cmp_w1_v, cmp_w2_v, w_out, ln1_g, ln1_b, w_mlp1, w_mlp2, ln2_g, ln2_b):
    B, S, D = x.shape
    assert D == D_MODEL and S // NSA_SLC_LEN <= LANE
    assert S % NSA_TILE == 0 and S % NSA_Q == 0 and S % min(MLA_TILE, S) == 0 and S % SSM_CHUNK == 0
    T = B * S
    ret_tables = _ret_tables(S)
    mla_tables = _mla_tables(S)
    nsa_tables = _nsa_tables(S)
    h = x.reshape(T, D)
    for l in range(w_in.shape[0]):
        if l == 0:
            h, *proj = _in_proj(h, _layout_w_in(w_in[l]), nsa_tables[2], entry_ln=(ln_emb_g, ln_emb_b))
        else:
            proj = _in_proj(h, _layout_w_in(w_in[l]), nsa_tables[2])
        p_ssm, p_mla, p_ret, p_nsa, nsa_kv, uk, uv = proj
        cmp_rows = (B, S // NSA_CMP_STRIDE, uk.shape[-1])
        y_a, y_c = _ssm_and_retention(p_ssm.reshape(B, S, W_SSM), conv_w[l], conv_b[l], dt_bias[l], a_log[l],
                                      d_skip[l], ssm_norm_g[l], p_ret.reshape(B, S, W_RET), ret_tables)
        wq, wk, wv = _layout_mla_weights(w_uq[l], w_ukv[l])
        q, k, v = _mla_prep(p_mla.reshape(B, S, W_MLA), q_norm_g[l], kv_norm_g[l], wq, wk, wv, mla_tables)
        y_b = _mla_attn(q, k, v)
        y_d = _nsa(p_nsa.reshape(B, S, W_NSA), nsa_kv.reshape(B, S, W_KV), uk.reshape(cmp_rows), uv.reshape(cmp_rows),
                   cmp_pe_k[l], cmp_w1_k[l], cmp_w2_k[l], cmp_pe_v[l], cmp_w1_v[l], cmp_w2_v[l], nsa_tables)
        ys = [y.reshape(T, GROUP_W) for y in (y_a, y_b, y_c, y_d)]
        h = _out_proj(h, ys, w_out, l, ln1_g[l], ln1_b[l])
        h = _mlp(h, w_mlp1, w_mlp2, l, ln2_g[l], ln2_b[l])
    return h.reshape(B, S, D)
```
